```python
import math
import jax, jax.numpy as jnp
from jax import lax
import numpy as np

D_MODEL = 1024
BATCH = 4
SEQ = 8192
DEPTH = 1

HG_HEADS = 8
HG_DK = 128
HG_DV = D_MODEL // HG_HEADS
HG_CHUNK = 32

NSA_HEADS = 8
NSA_GROUPS = 2
NSA_REP = NSA_HEADS // NSA_GROUPS
NSA_DK = 128
NSA_DV = 128
CMP_BLOCK = 32
CMP_STRIDE = 16
CMP_HIDDEN = 128
SLC_BLOCK = 64
SLC_TOPK = 16
WINDOW = 512
Q_BLOCK = 64
FORCE_SCORE = 1e4

REL_BUCKETS = 32
REL_MAX_DIST = 128

N_GROUPS = 4
EXP_PER_GROUP = 8
N_EXPERTS = N_GROUPS * EXP_PER_GROUP
TOPK_IN_GROUP = 2
D_EXPERT = D_MODEL // 2
MOE_BLOCK = 128

ALPHA = (2 * DEPTH) ** 0.25
BETA = (8 * DEPTH) ** -0.25

SPLITS = (HG_HEADS * HG_DK, HG_HEADS * HG_DK, HG_HEADS * HG_DV, HG_HEADS * HG_DV,
          NSA_HEADS * NSA_DK,
          NSA_GROUPS * NSA_DK, NSA_GROUPS * NSA_DV,
          NSA_GROUPS * NSA_DK, NSA_GROUPS * NSA_DV,
          NSA_GROUPS * NSA_DK, NSA_GROUPS * NSA_DV,
          3 * NSA_HEADS, D_MODEL, D_MODEL)
D_IN = sum(SPLITS)

kernel_name = 'hybrid_hgrn2_nsa_hmoe_block'


def _layer_norm(x, eps=1e-5):
    xf = x.astype(jnp.float32)
    mu = jnp.mean(xf, axis=-1, keepdims=True)
    var = jnp.mean(jnp.square(xf - mu), axis=-1, keepdims=True)
    return ((xf - mu) * lax.rsqrt(var + eps)).astype(x.dtype)


def _rms_norm(x, eps=1e-6):
    xf = x.astype(jnp.float32)
    return (xf * lax.rsqrt(jnp.mean(jnp.square(xf), axis=-1, keepdims=True) + eps)).astype(x.dtype)


def _masked_softmax(logits, mask):
    lf = jnp.where(mask, logits.astype(jnp.float32), -1e30)
    p = jax.nn.softmax(lf, axis=-1)
    return jnp.where(mask, p, 0.0)


def _rel_bucket(dist):
    n = jnp.maximum(dist, 0)
    max_exact = REL_BUCKETS // 2
    nf = jnp.maximum(n, 1).astype(jnp.float32)
    large = max_exact + (jnp.log(nf / max_exact) / math.log(REL_MAX_DIST / max_exact)
                         * (REL_BUCKETS - max_exact)).astype(jnp.int32)
    large = jnp.minimum(large, REL_BUCKETS - 1)
    return jnp.where(n < max_exact, n, large)


def _hgrn2(q, f_raw, v, lb):
    B, T = q.shape[:2]
    N = T // HG_CHUNK
    lb = lb.reshape(HG_HEADS, HG_DK).astype(q.dtype)
    f = lb + (1 - lb) * jax.nn.sigmoid(f_raw)
    log_f = jnp.log(f)
    k = 1 - f

    def chunks(a):
        return a.reshape(B, N, HG_CHUNK, HG_HEADS, a.shape[-1]).transpose(1, 0, 3, 2, 4)

    qc, kc, vc, lfc = chunks(q), chunks(k), chunks(v), chunks(log_f)
    b = jnp.cumsum(lfc, axis=3)
    b_last = b[:, :, :, -1:, :]
    q_in = qc * jnp.exp(b)
    k_in = kc * jnp.exp(-b)
    k_end = kc * jnp.exp(b_last - b)
    causal = jnp.tril(jnp.ones((HG_CHUNK, HG_CHUNK), dtype=bool))
    att = jnp.einsum('nbhid,nbhjd->nbhij', q_in, k_in)
    att = jnp.where(causal, att, 0)
    o_intra = jnp.einsum('nbhij,nbhje->nbhie', att, vc)
    upd = jnp.einsum('nbhcd,nbhce->nbhde', k_end, vc)
    decay = jnp.exp(b_last[:, :, :, 0, :])

    def step(S, xs):
        q_n, dec_n, u_n = xs
        o_n = jnp.einsum('bhcd,bhde->bhce', q_n, S)
        S = dec_n[..., None] * S + u_n
        return S, o_n

    S0 = jnp.zeros((B, HG_HEADS, HG_DK, HG_DV), q.dtype)
    _, o_inter = lax.scan(step, S0, (q_in, decay, upd))
    o = o_intra + o_inter
    return o.transpose(1, 0, 3, 2, 4).reshape(B, T, HG_HEADS, HG_DV)


def _compress(kv, pos, w1, b1, w2):
    B, G, T, d = kv.shape
    r = CMP_BLOCK // CMP_STRIDE
    ns = T // CMP_STRIDE
    sub = kv.reshape(B, G, ns, CMP_STRIDE, d)
    blocks = jnp.concatenate([sub[:, :, j:ns - r + 1 + j] for j in range(r)], axis=3)
    blocks = (blocks + pos).reshape(B, G, ns - r + 1, CMP_BLOCK * d)
    return jax.nn.silu(blocks @ w1 + b1) @ w2


def _nsa(q, kc, vc, ks, vs, kw, vw, gate_raw, rel_bias,
         pos_k, w1_k, b1_k, w2_k, pos_v, w1_v, b1_v, w2_v):
    B, T = q.shape[:2]
    G, R = NSA_GROUPS, NSA_REP
    NQB = T // Q_BLOCK
    NSLC = T // SLC_BLOCK
    NSEL = min(SLC_TOPK, NSLC)
    scale = NSA_DK ** -0.5
    to_g = lambda a: a.transpose(0, 2, 1, 3)

    k_cmp = _compress(to_g(kc), pos_k, w1_k, b1_k, w2_k)
    v_cmp = _compress(to_g(vc), pos_v, w1_v, b1_v, w2_v)
    NC = k_cmp.shape[2]
    cmp_end = jnp.arange(NC) * CMP_STRIDE + CMP_BLOCK - 1
    cs = np.arange(NC) * CMP_STRIDE
    ss = np.arange(NSLC) * SLC_BLOCK
    overlap = jnp.asarray(((cs[:, None] < ss[None, :] + SLC_BLOCK)
                           & (cs[:, None] + CMP_BLOCK > ss[None, :])).astype(np.float32))

    k_slc = to_g(ks).reshape(B, G, NSLC, SLC_BLOCK, NSA_DK)
    v_slc = to_g(vs).reshape(B, G, NSLC, SLC_BLOCK, NSA_DV)
    pad = ((0, 0), (0, 0), (WINDOW, 0), (0, 0))
    k_win = jnp.pad(to_g(kw), pad)
    v_win = jnp.pad(to_g(vw), pad)
    table = rel_bias.reshape(G, R, REL_BUCKETS)
    g_ix = jnp.arange(G)[None, :, None, None, None]
    r_ix = jnp.arange(R)[None, None, :, None, None]
    blk_start = jnp.arange(NSLC) * SLC_BLOCK
    j_ix = jnp.arange(NSLC)

    q_blocks = q.reshape(B, NQB, Q_BLOCK, G, R, NSA_DK).transpose(1, 0, 3, 4, 2, 5)
    gate_blocks = jax.nn.sigmoid(gate_raw).reshape(B, NQB, Q_BLOCK, G, R, 3).transpose(1, 0, 3, 4, 2, 5)
    gather = jax.vmap(jax.vmap(lambda blk, ix: blk[ix]))

    def block_fn(args):
        qb, q_blk, g_blk = args
        t = qb * Q_BLOCK + jnp.arange(Q_BLOCK)
        dist_c = t[:, None] - cmp_end[None, :]
        s_c = jnp.einsum('bgrqd,bgnd->bgrqn', q_blk, k_cmp) * scale + table[:, :, _rel_bucket(dist_c)]
        p_c = _masked_softmax(s_c, dist_c >= 0)
        o_c = jnp.einsum('bgrqn,bgnd->bgrqd', p_c.astype(v_cmp.dtype), v_cmp)
        imp = jnp.einsum('bgqn,nj->bgqj', p_c.sum(axis=2), overlap)
        cur = t // SLC_BLOCK
        forced = (j_ix[None, :] == 0) | (j_ix[None, :] == cur[:, None]) | (j_ix[None, :] == cur[:, None] - 1)
        valid = blk_start[None, :] <= t[:, None]
        score = jnp.where(valid, jnp.where(forced, FORCE_SCORE, imp), -1.0)
        _, idx = lax.top_k(score, NSEL)
        k_sel = gather(k_slc, idx).reshape(B, G, Q_BLOCK, NSEL * SLC_BLOCK, NSA_DK)
        v_sel = gather(v_slc, idx).reshape(B, G, Q_BLOCK, NSEL * SLC_BLOCK, NSA_DV)
        s_pos = (idx[..., None] * SLC_BLOCK + jnp.arange(SLC_BLOCK)).reshape(B, G, Q_BLOCK, NSEL * SLC_BLOCK)
        dist_s = t[None, None, :, None] - s_pos
        s_s = (jnp.einsum('bgrqd,bgqkd->bgrqk', q_blk, k_sel) * scale
               + table[g_ix, r_ix, _rel_bucket(dist_s)[:, :, None]])
        p_s = _masked_softmax(s_s, (dist_s >= 0)[:, :, None])
        o_s = jnp.einsum('bgrqk,bgqkd->bgrqd', p_s.astype(v_sel.dtype), v_sel)
        k_w = lax.dynamic_slice_in_dim(k_win, qb * Q_BLOCK, WINDOW + Q_BLOCK, axis=2)
        v_w = lax.dynamic_slice_in_dim(v_win, qb * Q_BLOCK, WINDOW + Q_BLOCK, axis=2)
        pos_w = qb * Q_BLOCK - WINDOW + jnp.arange(WINDOW + Q_BLOCK)
        dist_w = t[:, None] - pos_w[None, :]
        mask_w = (dist_w >= 0) & (dist_w < WINDOW) & (pos_w[None, :] >= 0)
        s_w = jnp.einsum('bgrqd,bgkd->bgrqk', q_blk, k_w) * scale + table[:, :, _rel_bucket(dist_w)]
        p_w = _masked_softmax(s_w, mask_w)
        o_w = jnp.einsum('bgrqk,bgkd->bgrqd', p_w.astype(v_w.dtype), v_w)
        return g_blk[..., 0:1] * o_c + g_blk[..., 1:2] * o_s + g_blk[..., 2:3] * o_w

    o = lax.map(block_fn, (jnp.arange(NQB), q_blocks, gate_blocks))
    return o.transpose(1, 0, 4, 2, 3, 5).reshape(B, T, NSA_HEADS * NSA_DV)


def _hier_moe(h, w_grp, b_grp, w_exp, b_exp, w1, w3, w2):
    B, T, D = h.shape
    x = h.reshape(-1, D)
    NT = x.shape[0]
    grp_prob = jax.nn.softmax((x @ w_grp + b_grp).astype(jnp.float32), axis=-1)
    grp_p, grp_idx = lax.top_k(grp_prob, 1)
    exp_logits = (x @ w_exp + b_exp).astype(jnp.float32).reshape(NT, N_GROUPS, EXP_PER_GROUP)
    in_grp = exp_logits[jnp.arange(NT), grp_idx[:, 0]]
    top_l, top_i = lax.top_k(in_grp, TOPK_IN_GROUP)
    comb = grp_p * jax.nn.softmax(top_l, axis=-1)
    expert = grp_idx * EXP_PER_GROUP + top_i

    A = NT * TOPK_IN_GROUP
    e_flat = expert.reshape(-1)
    tok_flat = jnp.repeat(jnp.arange(NT), TOPK_IN_GROUP)
    order = jnp.argsort(e_flat)
    e_s, tok_s, w_s = e_flat[order], tok_flat[order], comb.reshape(-1)[order]
    counts = jnp.bincount(e_flat, length=N_EXPERTS)
    starts = jnp.cumsum(counts) - counts
    padded = (counts + MOE_BLOCK - 1) // MOE_BLOCK * MOE_BLOCK
    pend = jnp.cumsum(padded)
    pstarts = pend - padded
    dest = pstarts[e_s] + jnp.arange(A) - starts[e_s]
    n_blocks = -(-A // MOE_BLOCK) + N_EXPERTS
    x_pad = jnp.zeros((n_blocks * MOE_BLOCK, D), x.dtype).at[dest].set(x[tok_s])
    block_expert = jnp.minimum(jnp.searchsorted(pend, jnp.arange(n_blocks) * MOE_BLOCK, side='right'),
                               N_EXPERTS - 1)

    def expert_block(args):
        xb, e = args
        return (jax.nn.silu(xb @ w1[e]) * (xb @ w3[e])) @ w2[e]

    y_pad = lax.map(expert_block, (x_pad.reshape(n_blocks, MOE_BLOCK, D), block_expert)).reshape(-1, D)
    y = y_pad[dest] * w_s[:, None].astype(x.dtype)
    out = jnp.zeros_like(x).at[tok_s].add(y)
    return out.reshape(B, T, D)


def setup_inputs(seed: int = 0) -> dict:
    key = jax.random.key(seed)
    ks = iter(jax.random.split(key, 40))
    nrm = lambda shape, s: jax.random.normal(next(ks), shape, jnp.float32) * s
    L, D = DEPTH, D_MODEL
    return {
        'x': nrm((BATCH, SEQ, D), 1.0),
        'c': nrm((BATCH, D), 1.0),
        'ada_w': nrm((L, D, 6 * D), 0.1 * D ** -0.5),
        'ada_b': nrm((L, 6 * D), 0.02),
        'w_in': nrm((L, D, D_IN), D ** -0.5),
        'b_in': nrm((L, D_IN), 0.02),
        'hg_lb_logits': nrm((DEPTH + 1, HG_HEADS * HG_DK), 0.1),
        'hg_norm_w': 1.0 + nrm((L, HG_HEADS * HG_DV), 0.05),
        'cmp_pos_k': nrm((L, CMP_BLOCK, NSA_DK), 0.1),
        'cmp_w1_k': nrm((L, CMP_BLOCK * NSA_DK, CMP_HIDDEN), (CMP_BLOCK * NSA_DK) ** -0.5),
        'cmp_b1_k': nrm((L, CMP_HIDDEN), 0.02),
        'cmp_w2_k': nrm((L, CMP_HIDDEN, NSA_DK), CMP_HIDDEN ** -0.5),
        'cmp_pos_v': nrm((L, CMP_BLOCK, NSA_DV), 0.1),
        'cmp_w1_v': nrm((L, CMP_BLOCK * NSA_DV, CMP_HIDDEN), (CMP_BLOCK * NSA_DV) ** -0.5),
        'cmp_b1_v': nrm((L, CMP_HIDDEN), 0.02),
        'cmp_w2_v': nrm((L, CMP_HIDDEN, NSA_DV), CMP_HIDDEN ** -0.5),
        'rel_bias': nrm((NSA_HEADS, REL_BUCKETS), 0.5),
        'w_br_hg': nrm((L, HG_HEADS * HG_DV, D), BETA * (HG_HEADS * HG_DV) ** -0.5),
        'w_br_nsa': nrm((L, NSA_HEADS * NSA_DV, D), BETA * (NSA_HEADS * NSA_DV) ** -0.5),
        'w_out': nrm((L, D, D), BETA * D ** -0.5),
        'ln1_g': 1.0 + nrm((L, D), 0.05),
        'ln1_b': nrm((L, D), 0.02),
        'router_grp_w': nrm((L, D, N_GROUPS), D ** -0.5),
        'router_grp_b': nrm((L, N_GROUPS), 0.01),
        'router_exp_w': nrm((L, D, N_EXPERTS), D ** -0.5),
        'router_exp_b': nrm((L, N_EXPERTS), 0.01),
        'exp_w1': nrm((L, N_EXPERTS, D, D_EXPERT), D ** -0.5),
        'exp_w3': nrm((L, N_EXPERTS, D, D_EXPERT), D ** -0.5),
        'exp_w2': nrm((L, N_EXPERTS, D_EXPERT, D), BETA * D_EXPERT ** -0.5),
        'ln2_g': 1.0 + nrm((L, D), 0.05),
        'ln2_b': nrm((L, D), 0.02),
    }


def reference(x, c, ada_w, ada_b, w_in, b_in, hg_lb_logits, hg_norm_w,
              cmp_pos_k, cmp_w1_k, cmp_b1_k, cmp_w2_k, cmp_pos_v, cmp_w1_v, cmp_b1_v, cmp_w2_v,
              rel_bias, w_br_hg, w_br_nsa, w_out, ln1_g, ln1_b,
              router_grp_w, router_grp_b, router_exp_w, router_exp_b,
              exp_w1, exp_w3, exp_w2, ln2_g, ln2_b):
    B, T, D = x.shape
    split_idx = [int(v) for v in np.cumsum(SPLITS)[:-1]]
    lb_all = jnp.cumsum(jax.nn.softmax(hg_lb_logits.astype(jnp.float32), axis=0), axis=0)
    c_act = jax.nn.silu(c)
    for l in range(DEPTH):
        mod = c_act @ ada_w[l] + ada_b[l]
        sh1, sc1, g1, sh2, sc2, g2 = jnp.split(mod, 6, axis=-1)

        h = _layer_norm(x) * (1 + sc1[:, None]) + sh1[:, None]
        proj = h @ w_in[l] + b_in[l]
        (hq, hf, hi, hg, nq, kc, vc, ks, vs, kw, vw, ngate, mg_h, mg_n) = jnp.split(proj, split_idx, axis=-1)
        o_h = _hgrn2(hq.reshape(B, T, HG_HEADS, HG_DK), hf.reshape(B, T, HG_HEADS, HG_DK),
                     hi.reshape(B, T, HG_HEADS, HG_DV), lb_all[l])
        o_h = (_rms_norm(o_h) * hg_norm_w[l].reshape(HG_HEADS, HG_DV)
               * jax.nn.sigmoid(hg.reshape(B, T, HG_HEADS, HG_DV))).reshape(B, T, HG_HEADS * HG_DV)
        gsz = (B, T, NSA_GROUPS, NSA_DK)
        o_n = _nsa(nq.reshape(B, T, NSA_HEADS, NSA_DK),
                   kc.reshape(gsz), vc.reshape(gsz), ks.reshape(gsz), vs.reshape(gsz),
                   kw.reshape(gsz), vw.reshape(gsz),
                   ngate.reshape(B, T, NSA_HEADS, 3), rel_bias,
                   cmp_pos_k[l], cmp_w1_k[l], cmp_b1_k[l], cmp_w2_k[l],
                   cmp_pos_v[l], cmp_w1_v[l], cmp_b1_v[l], cmp_w2_v[l])
        merged = jax.nn.sigmoid(mg_h) * (o_h @ w_br_hg[l]) + jax.nn.sigmoid(mg_n) * (o_n @ w_br_nsa[l])
        y = (1 + g1[:, None]) * (merged @ w_out[l])
        x = _layer_norm(ALPHA * x + y) * ln1_g[l] + ln1_b[l]

        h = _layer_norm(x) * (1 + sc2[:, None]) + sh2[:, None]
        y = (1 + g2[:, None]) * _hier_moe(h, router_grp_w[l], router_grp_b[l], router_exp_w[l],
                                          router_exp_b[l], exp_w1[l], exp_w3[l], exp_w2[l])
        x = _layer_norm(ALPHA * x + y) * ln2_g[l] + ln2_b[l]
    return x
```

```python
import functools
import math

import numpy as np
import jax
import jax.numpy as jnp
from jax import lax
from jax.experimental import pallas as pl
from jax.experimental.pallas import tpu as pltpu

F32 = jnp.float32
BF16 = jnp.bfloat16

D_MODEL = 1024
HG_HEADS = 8
HG_DK = 128
HG_DV = 128
HG_CHUNK = 32
NSA_HEADS = 8
NSA_GROUPS = 2
NSA_REP = NSA_HEADS // NSA_GROUPS
NSA_DK = 128
CMP_BLOCK = 32
CMP_STRIDE = 16
SLC_BLOCK = 64
SLC_TOPK = 16
WINDOW = 512
FORCE_SCORE = 1e4
REL_BUCKETS = 32
REL_MAX_DIST = 128
N_GROUPS = 4
EXP_PER_GROUP = 8
N_EXPERTS = N_GROUPS * EXP_PER_GROUP
D_EXPERT = D_MODEL // 2
DEPTH = 1
ALPHA = (2 * DEPTH) ** 0.25

LANE = 128
QT = 128
NEG = -1e30
SCALE = NSA_DK ** -0.5
CMP_PAD = 120
VMEM_LIMIT = 56 * 1024 * 1024

CB_HQ, CB_HF, CB_HI, CB_HG = 0, 8, 16, 24
CB_NQ = 32
CB_KC, CB_VC, CB_KS, CB_VS, CB_KW, CB_VW = 40, 42, 44, 46, 48, 50
CB_GATE = 52
CB_MGH, CB_MGN = 56, 64
NCB = 72
MAIN_COLS = 52 * LANE
GATE_COLS = 3 * NSA_HEADS

MOE_ROWS = 512


def _cparams(sem):
    return pltpu.CompilerParams(dimension_semantics=sem, vmem_limit_bytes=VMEM_LIMIT)


def _dot(a, b):
    return jnp.dot(a, b, preferred_element_type=F32)


def _dot_nt(a, b):
    return lax.dot_general(a, b, (((1,), (1,)), ((), ())), preferred_element_type=F32)


def _dot_tn(a, b):
    return lax.dot_general(a, b, (((0,), (0,)), ((), ())), preferred_element_type=F32)


def _split3(x):
    hi = x.astype(BF16)
    r = x - hi.astype(F32)
    mid = r.astype(BF16)
    lo = (r - mid.astype(F32)).astype(BF16)
    return hi, mid, lo


def _dot01(m01, x):
    hi, mid, lo = _split3(x)
    return _dot(m01, hi) + _dot(m01, mid) + _dot(m01, lo)


def _dot01_r(x, m01):
    hi, mid, lo = _split3(x)
    return _dot(hi, m01) + _dot(mid, m01) + _dot(lo, m01)


def _layer_norm(x, eps=1e-5):
    mu = jnp.mean(x, axis=-1, keepdims=True)
    xc = x - mu
    var = jnp.mean(xc * xc, axis=-1, keepdims=True)
    return xc * lax.rsqrt(var + eps)


def _adaln_kernel(c_ref, w_ref, b_ref, o_ref):
    c = c_ref[...]
    ca = c * jax.nn.sigmoid(c)
    o_ref[...] = jnp.dot(ca, w_ref[...], precision=lax.Precision.HIGHEST,
                         preferred_element_type=F32) + b_ref[...]


def _adaln(c_pad, w, b):
    rows, d = c_pad.shape
    n = w.shape[1]
    return pl.pallas_call(
        _adaln_kernel,
        grid=(n // d,),
        in_specs=[pl.BlockSpec((rows, d), lambda j: (0, 0)),
                  pl.BlockSpec((d, d), lambda j: (0, j)),
                  pl.BlockSpec((1, d), lambda j: (0, j))],
        out_specs=pl.BlockSpec((rows, d), lambda j: (0, j)),
        out_shape=jax.ShapeDtypeStruct((rows, n), F32),
        compiler_params=_cparams(("arbitrary",)),
        name="adaln",
    )(c_pad, w, b)


def _inproj_kernel(x_ref, mod_ref, w_ref, b_ref, o_ref, hn_ref, *, ncb_tile):
    @pl.when(pl.program_id(1) == 0)
    def _():
        hn = _layer_norm(x_ref[...])
        sh = mod_ref[0, 0:1, :]
        sc = mod_ref[0, 1:2, :]
        hn_ref[...] = (hn * (1.0 + sc) + sh).astype(BF16)

    res = _dot(hn_ref[...], w_ref[...]) + b_ref[...]
    for c in range(ncb_tile):
        o_ref[c] = res[:, c * LANE:(c + 1) * LANE]


def _inproj(x2, mod, w, b, seq):
    nt, d = x2.shape
    tm = min(1024, seq)
    ncb_tile = 9
    tn = ncb_tile * LANE
    return pl.pallas_call(
        functools.partial(_inproj_kernel, ncb_tile=ncb_tile),
        grid=(nt // tm, NCB // ncb_tile),
        in_specs=[pl.BlockSpec((tm, d), lambda i, j: (i, 0)),
                  pl.BlockSpec((1, 6, d), lambda i, j: (i * tm // seq, 0, 0)),
                  pl.BlockSpec((d, tn), lambda i, j: (0, j)),
                  pl.BlockSpec((1, tn), lambda i, j: (0, j))],
        out_specs=pl.BlockSpec((ncb_tile, tm, LANE), lambda i, j: (j, i, 0)),
        out_shape=jax.ShapeDtypeStruct((NCB, nt, LANE), F32),
        scratch_shapes=[pltpu.VMEM((tm, d), BF16)],
        compiler_params=_cparams(("arbitrary", "arbitrary")),
        name="inproj",
    )(x2, mod, w, b)


def _hgrn_kernel(q_ref, f_ref, v_ref, g_ref, lb_ref, nw_ref, o_ref, st_ref, *, rows):
    @pl.when(pl.program_id(2) == 0)
    def _():
        st_ref[...] = jnp.zeros_like(st_ref)

    lb = lb_ref[0]
    nw = nw_ref[0]
    ri = lax.broadcasted_iota(jnp.int32, (LANE, LANE), 0)
    ci = lax.broadcasted_iota(jnp.int32, (LANE, LANE), 1)
    same = (ri // HG_CHUNK) == (ci // HG_CHUNK)
    tril = same & (ci <= ri)
    cum_m = jnp.where(tril, 1.0, 0.0).astype(BF16)
    tot_m = jnp.where(same, 1.0, 0.0).astype(BF16)
    per = LANE // HG_CHUNK

    def body(i, carry):
        r0 = pl.multiple_of(i * LANE, LANE)
        q = q_ref[0, pl.ds(r0, LANE), :]
        fr = f_ref[0, pl.ds(r0, LANE), :]
        v = v_ref[0, pl.ds(r0, LANE), :]
        g = g_ref[0, pl.ds(r0, LANE), :]
        f = lb + (1.0 - lb) * jax.nn.sigmoid(fr)
        lf = jnp.log(f)
        k = 1.0 - f
        hi, mid, lo = _split3(lf)
        b = _dot(cum_m, hi) + _dot(cum_m, mid) + _dot(cum_m, lo)
        bl = _dot(tot_m, hi) + _dot(tot_m, mid) + _dot(tot_m, lo)
        q_in = (q * jnp.exp(b)).astype(BF16)
        k_in = (k * jnp.exp(-b)).astype(BF16)
        k_end = (k * jnp.exp(bl - b)).astype(BF16)
        vb = v.astype(BF16)
        att = jnp.where(tril, _dot_nt(q_in, k_in), 0.0)
        o = _dot(att.astype(BF16), vb)
        st = st_ref[...]
        inter = []
        for c in range(per):
            sl = slice(c * HG_CHUNK, (c + 1) * HG_CHUNK)
            inter.append(_dot_nt(q_in[sl], st.astype(BF16)))
            dec = jnp.exp(bl[c * HG_CHUNK:c * HG_CHUNK + 1, :])
            st = dec * st + _dot_tn(vb[sl], k_end[sl])
        st_ref[...] = st
        o = o + jnp.concatenate(inter, axis=0)
        ms = jnp.mean(o * o, axis=-1, keepdims=True)
        on = o * lax.rsqrt(ms + 1e-6) * nw * jax.nn.sigmoid(g)
        o_ref[pl.ds(r0, LANE), :] = on.astype(BF16)
        return carry

    lax.fori_loop(0, rows // LANE, body, 0)


def _hgrn(proj, lb, nw, batch, seq):
    nt = proj.shape[1]
    tb = min(1024, seq)
    nblk = seq // tb

    def slab(cb0):
        return pl.BlockSpec((1, tb, LANE), lambda b, h, t: (cb0 + h, b * nblk + t, 0))

    vec = pl.BlockSpec((1, 1, LANE), lambda b, h, t: (h, 0, 0))
    return pl.pallas_call(
        functools.partial(_hgrn_kernel, rows=tb),
        grid=(batch, HG_HEADS, nblk),
        in_specs=[slab(CB_HQ), slab(CB_HF), slab(CB_HI), slab(CB_HG), vec, vec],
        out_specs=pl.BlockSpec((tb, LANE), lambda b, h, t: (b * nblk + t, h)),
        out_shape=jax.ShapeDtypeStruct((nt, HG_HEADS * HG_DV), BF16),
        scratch_shapes=[pltpu.VMEM((HG_DV, HG_DK), F32)],
        compiler_params=_cparams(("arbitrary", "arbitrary", "arbitrary")),
        name="hgrn2",
    )(proj, proj, proj, proj, lb, nw)


def _compress_kernel(sub_ref, pos_ref, w1_ref, b1_ref, w2_ref, o_ref, *, ns):
    half = CMP_STRIDE * LANE
    sub = sub_ref[0]
    xa = (sub + pos_ref[0, 0:1, :]).astype(BF16)
    xb = (sub + pos_ref[0, 1:2, :]).astype(BF16)
    p0 = _dot(xa, w1_ref[0, 0:half, :])
    p1 = _dot(xb, w1_ref[0, half:2 * half, :])
    h = p0 + pltpu.roll(p1, ns - 1, axis=0) + b1_ref[0]
    a = h * jax.nn.sigmoid(h)
    out = _dot(a.astype(BF16), w2_ref[0])
    row = lax.broadcasted_iota(jnp.int32, out.shape, 0)
    out = jnp.where(row < ns - 1, out, 0.0)
    o_ref[0, 0, 0, 0:CMP_PAD, :] = jnp.zeros((CMP_PAD, LANE), F32)
    o_ref[0, 0, 0, CMP_PAD:CMP_PAD + ns, :] = out
    o_ref[0, 0, 0, CMP_PAD + ns:, :] = jnp.zeros((LANE - CMP_PAD, LANE), F32)


def _compress(proj16, pos, w1, b1, w2, batch, seq):
    ns = seq // CMP_STRIDE
    np_rows = ns + LANE
    width = CMP_STRIDE * LANE
    return pl.pallas_call(
        functools.partial(_compress_kernel, ns=ns),
        grid=(2, batch, NSA_GROUPS),
        in_specs=[pl.BlockSpec((1, ns, width), lambda s, b, g: (CB_KC + 2 * s + g, b, 0)),
                  pl.BlockSpec((1, 2, width), lambda s, b, g: (s, 0, 0)),
                  pl.BlockSpec((1, 2 * width, LANE), lambda s, b, g: (s, 0, 0)),
                  pl.BlockSpec((1, 1, LANE), lambda s, b, g: (s, 0, 0)),
                  pl.BlockSpec((1, LANE, LANE), lambda s, b, g: (s, 0, 0))],
        out_specs=pl.BlockSpec((1, 1, 1, np_rows, LANE), lambda s, b, g: (s, b, g, 0, 0)),
        out_shape=jax.ShapeDtypeStruct((2, batch, NSA_GROUPS, np_rows, LANE), F32),
        compiler_params=_cparams(("arbitrary", "arbitrary", "arbitrary")),
        name="nsa_compress",
    )(proj16, pos, w1, b1, w2)


def _stack_q(q_refs):
    return jnp.concatenate([r[0] for r in q_refs], axis=0).astype(BF16)


def _cmp_sel_kernel(q0, q1, q2, q3, kc_ref, vc_ref, gate_ref, bc_ref, c31_ref, ov_ref,
                    oc_ref, sel_ref, *, np_rows):
    ti = pl.program_id(2)
    rows = NSA_REP * QT
    qb = _stack_q((q0, q1, q2, q3))
    w0 = pl.multiple_of(ti * (QT // CMP_STRIDE), 8)

    kall = kc_ref[0, 0, 0].astype(BF16)
    s_all = _dot_nt(qb, kall) * SCALE
    c31 = jnp.concatenate(
        [jnp.broadcast_to(c31_ref[r][:, 0:1], (QT, 1)) for r in range(NSA_REP)], axis=0)
    pidx = lax.broadcasted_iota(jnp.int32, (rows, np_rows), 1)
    m_old = (pidx >= CMP_PAD) & (pidx < w0)
    s_all = jnp.where(m_old, s_all + c31, NEG)

    kwin = kc_ref[0, 0, 0, pl.ds(w0, LANE), :].astype(BF16)
    bias = jnp.concatenate([bc_ref[r] for r in range(NSA_REP)], axis=0)
    s_rec = _dot_nt(qb, kwin) * SCALE + bias
    ii = lax.broadcasted_iota(jnp.int32, (rows, LANE), 0) % QT
    mm = lax.broadcasted_iota(jnp.int32, (rows, LANE), 1)
    dist = ii - CMP_STRIDE * mm + (CMP_STRIDE * CMP_PAD - (CMP_BLOCK - 1))
    m_rec = (dist >= 0) & (mm >= CMP_PAD - w0)
    s_rec = jnp.where(m_rec, s_rec, NEG)

    mx = jnp.maximum(jnp.max(s_all, axis=-1, keepdims=True), jnp.max(s_rec, axis=-1, keepdims=True))
    e_all = jnp.where(m_old, jnp.exp(s_all - mx), 0.0)
    e_rec = jnp.where(m_rec, jnp.exp(s_rec - mx), 0.0)
    den = jnp.sum(e_all, axis=-1, keepdims=True) + jnp.sum(e_rec, axis=-1, keepdims=True)
    inv = 1.0 / jnp.where(den == 0.0, 1.0, den)
    p_all = e_all * inv
    p_rec = e_rec * inv

    vall = vc_ref[0, 0, 0].astype(BF16)
    vwin = vc_ref[0, 0, 0, pl.ds(w0, LANE), :].astype(BF16)
    o = _dot(p_all.astype(BF16), vall) + _dot(p_rec.astype(BF16), vwin)
    gt = jax.nn.sigmoid(gate_ref[0])
    for r in range(NSA_REP):
        oc_ref[:, r * LANE:(r + 1) * LANE] = gt[:, 3 * r:3 * r + 1] * o[r * QT:(r + 1) * QT]

    ps_all = p_all[0:QT]
    ps_rec = p_rec[0:QT]
    for r in range(1, NSA_REP):
        ps_all = ps_all + p_all[r * QT:(r + 1) * QT]
        ps_rec = ps_rec + p_rec[r * QT:(r + 1) * QT]
    imp = _dot01_r(ps_all, ov_ref[...]) + _dot01_r(ps_rec, ov_ref[pl.ds(w0, LANE), :])

    i1 = lax.broadcasted_iota(jnp.int32, (QT, LANE), 0)
    jj = lax.broadcasted_iota(jnp.int32, (QT, LANE), 1)
    cur = (ti * QT + i1) // SLC_BLOCK
    forced = (jj == 0) | (jj == cur) | (jj == cur - 1)
    score = jnp.where(jj <= cur, jnp.where(forced, FORCE_SCORE, imp), -1.0)
    sel = jnp.zeros((QT, LANE), F32)
    jf = jj.astype(F32)
    for _ in range(SLC_TOPK):
        mval = jnp.max(score, axis=-1, keepdims=True)
        first = jnp.min(jnp.where(score == mval, jf, float(LANE)), axis=-1, keepdims=True)
        pick = jf == first
        sel = jnp.where(pick, 1.0, sel)
        score = jnp.where(pick, -jnp.inf, score)
    sel_ref[0, 0] = sel.astype(BF16)


def _q_specs(nqt):
    return [pl.BlockSpec((1, QT, LANE),
                         functools.partial(lambda b, g, t, r: (CB_NQ + NSA_REP * g + r, b * nqt + t, 0), r=r))
            for r in range(NSA_REP)]


def _cmp_sel(proj, kvc, bc, c31, ov, batch, seq):
    nt = proj.shape[1]
    nqt = seq // QT
    np_rows = kvc.shape[3]
    kv_spec = lambda s: pl.BlockSpec((1, 1, 1, np_rows, LANE), lambda b, g, t: (s, b, g, 0, 0))
    return pl.pallas_call(
        functools.partial(_cmp_sel_kernel, np_rows=np_rows),
        grid=(batch, NSA_GROUPS, nqt),
        in_specs=_q_specs(nqt) + [
            kv_spec(0), kv_spec(1),
            pl.BlockSpec((1, QT, LANE), lambda b, g, t: (CB_GATE + g, b * nqt + t, 0)),
            pl.BlockSpec((NSA_REP, QT, LANE), lambda b, g, t: (g, 0, 0)),
            pl.BlockSpec((NSA_REP, 1, LANE), lambda b, g, t: (g, 0, 0)),
            pl.BlockSpec((np_rows, LANE), lambda b, g, t: (0, 0))],
        out_specs=[pl.BlockSpec((QT, NSA_REP * LANE), lambda b, g, t: (b * nqt + t, g)),
                   pl.BlockSpec((1, 1, QT, LANE), lambda b, g, t: (b, g, t, 0))],
        out_shape=[jax.ShapeDtypeStruct((nt, NSA_HEADS * LANE), F32),
                   jax.ShapeDtypeStruct((batch, NSA_GROUPS, seq, LANE), BF16)],
        compiler_params=_cparams(("arbitrary", "arbitrary", "arbitrary")),
        name="nsa_cmp_select",
    )(proj, proj, proj, proj, kvc, kvc, proj, bc, c31, ov)


def _attend(qb, k, v, bias, mask, state):
    m, l, acc = state
    s = _dot_nt(qb, k.astype(BF16)) * SCALE + bias
    if mask is not None:
        s = jnp.where(mask, s, NEG)
    m_new = jnp.maximum(m, jnp.max(s, axis=-1, keepdims=True))
    alpha = jnp.exp(m - m_new)
    p = jnp.exp(s - m_new)
    if mask is not None:
        p = jnp.where(mask, p, 0.0)
    l = alpha * l + jnp.sum(p, axis=-1, keepdims=True)
    acc = alpha * acc + _dot(p.astype(BF16), v.astype(BF16))
    return m_new, l, acc


def _finish(state, gate_ref, branch, o_ref):
    _, l, acc = state
    o = acc / jnp.where(l == 0.0, 1.0, l)
    gt = jax.nn.sigmoid(gate_ref[0])
    for r in range(NSA_REP):
        col = 3 * r + branch
        o_ref[:, r * LANE:(r + 1) * LANE] = gt[:, col:col + 1] * o[r * QT:(r + 1) * QT]


def _head_rows(ref):
    return jnp.concatenate([ref[r] for r in range(NSA_REP)], axis=0)


def _head_const(c31_ref):
    return jnp.concatenate(
        [jnp.broadcast_to(c31_ref[r][:, 0:1], (QT, 1)) for r in range(NSA_REP)], axis=0)


def _win_kernel(q0, q1, q2, q3, k_ref, v_ref, gate_ref, t0_ref, t1_ref, c31_ref, o_ref):
    ti = pl.program_id(2)
    rows = NSA_REP * QT
    qb = _stack_q((q0, q1, q2, q3))
    ii = lax.broadcasted_iota(jnp.int32, (rows, LANE), 0) % QT
    jj = lax.broadcasted_iota(jnp.int32, (rows, LANE), 1)
    c31 = _head_const(c31_ref)
    tiv = ti + jnp.zeros((rows, LANE), jnp.int32)
    state = (jnp.full((rows, 1), NEG, F32), jnp.zeros((rows, 1), F32), jnp.zeros((rows, LANE), F32))
    ntile = WINDOW // QT
    for delta in range(ntile, -1, -1):
        kt = jnp.maximum(ti - delta, 0)
        r0 = pl.multiple_of(kt * QT, QT)
        k = k_ref[0, pl.ds(r0, QT), :]
        v = v_ref[0, pl.ds(r0, QT), :]
        exists = tiv >= delta
        if delta == 0:
            bias, mask = _head_rows(t0_ref), jj <= ii
        elif delta == 1:
            bias, mask = _head_rows(t1_ref), exists
        elif delta == ntile:
            bias, mask = c31, (jj > ii) & exists
        else:
            bias, mask = c31, exists
        state = _attend(qb, k, v, bias, mask, state)
    _finish(state, gate_ref, 2, o_ref)


def _window(proj, t0, t1, c31, batch, seq):
    nt = proj.shape[1]
    nqt = seq // QT
    slab = lambda cb0: pl.BlockSpec((1, seq, LANE), lambda b, g, t: (cb0 + g, b, 0))
    tab = pl.BlockSpec((NSA_REP, QT, LANE), lambda b, g, t: (g, 0, 0))
    return pl.pallas_call(
        _win_kernel,
        grid=(batch, NSA_GROUPS, nqt),
        in_specs=_q_specs(nqt) + [
            slab(CB_KW), slab(CB_VW),
            pl.BlockSpec((1, QT, LANE), lambda b, g, t: (CB_GATE + g, b * nqt + t, 0)),
            tab, tab,
            pl.BlockSpec((NSA_REP, 1, LANE), lambda b, g, t: (g, 0, 0))],
        out_specs=pl.BlockSpec((QT, NSA_REP * LANE), lambda b, g, t: (b * nqt + t, g)),
        out_shape=jax.ShapeDtypeStruct((nt, NSA_HEADS * LANE), F32),
        compiler_params=_cparams(("arbitrary", "arbitrary", "arbitrary")),
        name="nsa_window",
    )(proj, proj, proj, proj, proj, proj, proj, t0, t1, c31)


def _sel_kernel(q0, q1, q2, q3, k_ref, v_ref, sel_ref, gate_ref, t0_ref, t1_ref, c31_ref, o_ref,
                m_ref, l_ref, acc_ref):
    ti = pl.program_id(2)
    rows = NSA_REP * QT
    qb = _stack_q((q0, q1, q2, q3))
    selm = sel_ref[0, 0]
    ii = lax.broadcasted_iota(jnp.int32, (rows, LANE), 0) % QT
    jj = lax.broadcasted_iota(jnp.int32, (rows, LANE), 1)
    c31 = _head_const(c31_ref)
    blk = lax.broadcasted_iota(jnp.int32, (LANE, LANE), 0)
    key_blk = lax.broadcasted_iota(jnp.int32, (LANE, LANE), 1) // SLC_BLOCK

    def sel_mask(kt):
        expand = jnp.where(blk == kt * (QT // SLC_BLOCK) + key_blk, 1.0, 0.0).astype(BF16)
        mk = _dot(selm, expand) > 0.5
        return jnp.concatenate([mk] * NSA_REP, axis=0)

    def tile(kt):
        r0 = pl.multiple_of(kt * QT, QT)
        return k_ref[0, pl.ds(r0, QT), :], v_ref[0, pl.ds(r0, QT), :]

    m_ref[...] = jnp.full((rows, 1), NEG, F32)
    l_ref[...] = jnp.zeros((rows, 1), F32)
    acc_ref[...] = jnp.zeros((rows, LANE), F32)

    def body(kt, carry):
        k, v = tile(kt)
        state = _attend(qb, k, v, c31, sel_mask(kt), (m_ref[...], l_ref[...], acc_ref[...]))
        m_ref[...], l_ref[...], acc_ref[...] = state
        return carry

    lax.fori_loop(0, jnp.maximum(ti - 1, 0), body, 0)

    state = (m_ref[...], l_ref[...], acc_ref[...])
    kt = jnp.maximum(ti - 1, 0)
    k, v = tile(kt)
    tiv = ti + jnp.zeros((rows, LANE), jnp.int32)
    state = _attend(qb, k, v, _head_rows(t1_ref), sel_mask(kt) & (tiv >= 1), state)
    k, v = tile(ti)
    state = _attend(qb, k, v, _head_rows(t0_ref), sel_mask(ti) & (jj <= ii), state)
    _finish(state, gate_ref, 1, o_ref)


def _selected(proj, sel, t0, t1, c31, batch, seq):
    nt = proj.shape[1]
    nqt = seq // QT
    rows = NSA_REP * QT
    slab = lambda cb0: pl.BlockSpec((1, seq, LANE), lambda b, g, t: (cb0 + g, b, 0))
    tab = pl.BlockSpec((NSA_REP, QT, LANE), lambda b, g, t: (g, 0, 0))
    return pl.pallas_call(
        _sel_kernel,
        grid=(batch, NSA_GROUPS, nqt),
        in_specs=_q_specs(nqt) + [
            slab(CB_KS), slab(CB_VS),
            pl.BlockSpec((1, 1, QT, LANE), lambda b, g, t: (b, g, t, 0)),
            pl.BlockSpec((1, QT, LANE), lambda b, g, t: (CB_GATE + g, b * nqt + t, 0)),
            tab, tab,
            pl.BlockSpec((NSA_REP, 1, LANE), lambda b, g, t: (g, 0, 0))],
        out_specs=pl.BlockSpec((QT, NSA_REP * LANE), lambda b, g, t: (b * nqt + t, g)),
        out_shape=jax.ShapeDtypeStruct((nt, NSA_HEADS * LANE), F32),
        scratch_shapes=[pltpu.VMEM((rows, 1), F32), pltpu.VMEM((rows, 1), F32),
                        pltpu.VMEM((rows, LANE), F32)],
        compiler_params=_cparams(("arbitrary", "arbitrary", "arbitrary")),
        name="nsa_selected",
    )(proj, proj, proj, proj, proj, proj, sel, proj, t0, t1, c31)


def _merge_kernel(oh_ref, oc_ref, os_ref, ow_ref, mgh_ref, mgn_ref, x_ref, mod_ref,
                  wh_ref, wn_ref, wo_ref, g_ref, b_ref, wr_ref, br_ref,
                  x1_ref, h2_ref, lg_ref):
    nblk = D_MODEL // LANE
    a_h = _dot(oh_ref[...], wh_ref[...])
    o_n = (oc_ref[...] + os_ref[...] + ow_ref[...]).astype(BF16)
    a_n = _dot(o_n, wn_ref[...])
    gh = jnp.concatenate([mgh_ref[c] for c in range(nblk)], axis=-1)
    gn = jnp.concatenate([mgn_ref[c] for c in range(nblk)], axis=-1)
    merged = jax.nn.sigmoid(gh) * a_h + jax.nn.sigmoid(gn) * a_n
    y = (1.0 + mod_ref[0, 2:3, :]) * _dot(merged.astype(BF16), wo_ref[...])
    x1 = _layer_norm(ALPHA * x_ref[...] + y) * g_ref[...] + b_ref[...]
    x1_ref[...] = x1
    h2 = _layer_norm(x1) * (1.0 + mod_ref[0, 4:5, :]) + mod_ref[0, 3:4, :]
    h2_ref[...] = h2
    lg_ref[...] = jnp.dot(h2, wr_ref[...], precision=lax.Precision.HIGHEST,
                          preferred_element_type=F32) + br_ref[...]


def _merge(o_h, o_c, o_s, o_w, proj, x2, mod, w_h, w_n, w_o, ln_g, ln_b, w_r, b_r, seq):
    nt, d = x2.shape
    tm = min(256, seq)
    nblk = d // LANE
    row = lambda w: pl.BlockSpec((tm, w), lambda i: (i, 0))
    full = lambda a: pl.BlockSpec(a.shape, lambda i: (0,) * a.ndim)
    return pl.pallas_call(
        _merge_kernel,
        grid=(nt // tm,),
        in_specs=[row(d), row(d), row(d), row(d),
                  pl.BlockSpec((nblk, tm, LANE), lambda i: (CB_MGH // nblk, i, 0)),
                  pl.BlockSpec((nblk, tm, LANE), lambda i: (CB_MGN // nblk, i, 0)),
                  row(d),
                  pl.BlockSpec((1, 6, d), lambda i: (i * tm // seq, 0, 0)),
                  full(w_h), full(w_n), full(w_o), full(ln_g), full(ln_b), full(w_r), full(b_r)],
        out_specs=[row(d), row(d), row(LANE)],
        out_shape=[jax.ShapeDtypeStruct((nt, d), F32),
                   jax.ShapeDtypeStruct((nt, d), F32),
                   jax.ShapeDtypeStruct((nt, LANE), F32)],
        compiler_params=_cparams(("arbitrary",)),
        name="merge_outproj",
    )(o_h, o_c, o_s, o_w, proj, proj, x2, mod, w_h, w_n, w_o, ln_g, ln_b, w_r, b_r)


def _route_kernel(lg_ref, rec_ref, cnt_ref, carry_ref, *, tm):
    @pl.when(pl.program_id(0) == 0)
    def _():
        carry_ref[...] = jnp.zeros_like(carry_ref)

    lg = lg_ref[...]
    lane = lax.broadcasted_iota(jnp.int32, (tm, LANE), 1).astype(F32)
    far = float(LANE)
    gmask = lane < N_GROUPS
    gl = jnp.where(gmask, lg, -jnp.inf)
    gmax = jnp.max(gl, axis=-1, keepdims=True)
    gsum = jnp.sum(jnp.where(gmask, jnp.exp(gl - gmax), 0.0), axis=-1, keepdims=True)
    grp_p = 1.0 / gsum
    gidx = jnp.min(jnp.where(gl == gmax, lane, far), axis=-1, keepdims=True)
    lo = N_GROUPS + EXP_PER_GROUP * gidx
    emask = (lane >= lo) & (lane < lo + EXP_PER_GROUP)
    el = jnp.where(emask, lg, -jnp.inf)
    m1 = jnp.max(el, axis=-1, keepdims=True)
    i1 = jnp.min(jnp.where(el == m1, lane, far), axis=-1, keepdims=True)
    el2 = jnp.where(lane == i1, -jnp.inf, el)
    m2 = jnp.max(el2, axis=-1, keepdims=True)
    i2 = jnp.min(jnp.where(emask & (lane != i1) & (el2 == m2), lane, far), axis=-1, keepdims=True)
    e = jnp.exp(m2 - m1)
    w0 = grp_p / (1.0 + e)
    w1 = grp_p * e / (1.0 + e)

    oh0 = lane == i1
    oh1 = lane == i2
    f0 = jnp.where(oh0, 1.0, 0.0)
    f1 = jnp.where(oh1, 1.0, 0.0)
    ri = lax.broadcasted_iota(jnp.int32, (tm, tm), 0)
    ci = lax.broadcasted_iota(jnp.int32, (tm, tm), 1)
    before = jnp.where(ci < ri, 1.0, 0.0).astype(BF16)
    cum0 = _dot(before, f0.astype(BF16))
    cum1 = _dot(before, f1.astype(BF16))
    tot0 = jnp.sum(f0, axis=0, keepdims=True)
    tot1 = jnp.sum(f1, axis=0, keepdims=True)
    carry = carry_ref[...]
    rank0 = jnp.sum(jnp.where(oh0, carry + cum0, 0.0), axis=-1, keepdims=True)
    rank1 = jnp.sum(jnp.where(oh1, carry + tot0 + cum1, 0.0), axis=-1, keepdims=True)
    carry = carry + tot0 + tot1
    carry_ref[...] = carry
    cnt_ref[...] = carry

    rec = jnp.where(lane == 0, i1 - N_GROUPS, 0.0)
    rec = jnp.where(lane == 1, i2 - N_GROUPS, rec)
    rec = jnp.where(lane == 2, w0, rec)
    rec = jnp.where(lane == 3, w1, rec)
    rec = jnp.where(lane == 4, rank0, rec)
    rec = jnp.where(lane == 5, rank1, rec)
    rec_ref[...] = rec


def _route(logits):
    nt = logits.shape[0]
    tm = min(512, nt)
    return pl.pallas_call(
        functools.partial(_route_kernel, tm=tm),
        grid=(nt // tm,),
        in_specs=[pl.BlockSpec((tm, LANE), lambda i: (i, 0))],
        out_specs=[pl.BlockSpec((tm, LANE), lambda i: (i, 0)),
                   pl.BlockSpec((1, LANE), lambda i: (0, 0))],
        out_shape=[jax.ShapeDtypeStruct((nt, LANE), F32),
                   jax.ShapeDtypeStruct((1, LANE), F32)],
        scratch_shapes=[pltpu.VMEM((1, LANE), F32)],
        compiler_params=_cparams(("arbitrary",)),
        name="moe_route",
    )(logits)


def _row_copy(src, dst, sem):
    return pltpu.make_async_copy(src, dst, sem)


def _dispatch_kernel(dest_ref, h_ref, zero_ref, xp_ref, sem, *, tm):
    del zero_ref
    base = pl.program_id(0) * tm

    def issue(r, carry):
        for k in range(2):
            d = dest_ref[2 * (base + r) + k]
            _row_copy(h_ref.at[pl.ds(r, 1), :], xp_ref.at[pl.ds(d, 1), :], sem).start()
        return carry

    lax.fori_loop(0, tm, issue, 0)

    def drain(r, carry):
        _row_copy(h_ref.at[pl.ds(0, 1), :], xp_ref.at[pl.ds(0, 1), :], sem).wait()
        return carry

    lax.fori_loop(0, 2 * tm, drain, 0)


def _dispatch(dest, h2, x_pad0):
    nt, d = h2.shape
    tm = min(256, nt)
    return pl.pallas_call(
        functools.partial(_dispatch_kernel, tm=tm),
        grid_spec=pltpu.PrefetchScalarGridSpec(
            num_scalar_prefetch=1,
            grid=(nt // tm,),
            in_specs=[pl.BlockSpec((tm, d), lambda i, dest: (i, 0)),
                      pl.BlockSpec(memory_space=pl.ANY)],
            out_specs=pl.BlockSpec(memory_space=pl.ANY),
            scratch_shapes=[pltpu.SemaphoreType.DMA(())]),
        out_shape=jax.ShapeDtypeStruct(x_pad0.shape, x_pad0.dtype),
        input_output_aliases={2: 0},
        compiler_params=_cparams(("arbitrary",)),
        name="moe_dispatch",
    )(dest, h2, x_pad0)


def _expert_kernel(be_ref, nu_ref, x_ref, w1_ref, w3_ref, w2_ref, y_ref):
    del be_ref
    i = pl.program_id(0)

    @pl.when(i < nu_ref[0])
    def _():
        xb = x_ref[...].astype(BF16)
        a = _dot(xb, w1_ref[0])
        b = _dot(xb, w3_ref[0])
        hmid = (a * jax.nn.sigmoid(a) * b).astype(BF16)
        y_ref[...] = _dot(hmid, w2_ref[0])

    @pl.when(i >= nu_ref[0])
    def _():
        y_ref[...] = jnp.zeros_like(y_ref)


def _experts(block_expert, n_used, x_pad, w1, w3, w2):
    npad, d = x_pad.shape
    de = w1.shape[2]
    nb = npad // MOE_ROWS
    return pl.pallas_call(
        _expert_kernel,
        grid_spec=pltpu.PrefetchScalarGridSpec(
            num_scalar_prefetch=2,
            grid=(nb,),
            in_specs=[pl.BlockSpec((MOE_ROWS, d), lambda i, be, nu: (i, 0)),
                      pl.BlockSpec((1, d, de), lambda i, be, nu: (be[i], 0, 0)),
                      pl.BlockSpec((1, d, de), lambda i, be, nu: (be[i], 0, 0)),
                      pl.BlockSpec((1, de, d), lambda i, be, nu: (be[i], 0, 0))],
            out_specs=pl.BlockSpec((MOE_ROWS, d), lambda i, be, nu: (i, 0))),
        out_shape=jax.ShapeDtypeStruct((npad, d), F32),
        compiler_params=_cparams(("arbitrary",)),
        name="moe_experts",
    )(block_expert, n_used, x_pad, w1, w3, w2)


def _combine_kernel(dest_ref, yp_ref, rec_ref, x1_ref, mod_ref, g_ref, b_ref, o_ref, buf_ref, sem, *, tm):
    base = pl.program_id(0) * tm

    def issue(r, carry):
        for k in range(2):
            d = dest_ref[2 * (base + r) + k]
            _row_copy(yp_ref.at[pl.ds(d, 1), :], buf_ref.at[k, pl.ds(r, 1), :], sem).start()
        return carry

    lax.fori_loop(0, tm, issue, 0)

    def drain(r, carry):
        _row_copy(yp_ref.at[pl.ds(0, 1), :], buf_ref.at[0, pl.ds(0, 1), :], sem).wait()
        return carry

    lax.fori_loop(0, 2 * tm, drain, 0)

    rec = rec_ref[...]
    y = rec[:, 2:3] * buf_ref[0] + rec[:, 3:4] * buf_ref[1]
    y = (1.0 + mod_ref[0, 5:6, :]) * y
    o_ref[...] = _layer_norm(ALPHA * x1_ref[...] + y) * g_ref[...] + b_ref[...]


def _combine(dest, y_pad, rec, x1, mod, ln_g, ln_b, seq):
    nt, d = x1.shape
    tm = min(256, seq)
    return pl.pallas_call(
        functools.partial(_combine_kernel, tm=tm),
        grid_spec=pltpu.PrefetchScalarGridSpec(
            num_scalar_prefetch=1,
            grid=(nt // tm,),
            in_specs=[pl.BlockSpec(memory_space=pl.ANY),
                      pl.BlockSpec((tm, LANE), lambda i, dest: (i, 0)),
                      pl.BlockSpec((tm, d), lambda i, dest: (i, 0)),
                      pl.BlockSpec((1, 6, d), lambda i, dest: (i * tm // seq, 0, 0)),
                      pl.BlockSpec((1, d), lambda i, dest: (0, 0)),
                      pl.BlockSpec((1, d), lambda i, dest: (0, 0))],
            out_specs=pl.BlockSpec((tm, d), lambda i, dest: (i, 0)),
            scratch_shapes=[pltpu.VMEM((2, tm, d), F32), pltpu.SemaphoreType.DMA(())]),
        out_shape=jax.ShapeDtypeStruct((nt, d), F32),
        compiler_params=_cparams(("arbitrary",)),
        name="moe_combine",
    )(dest, y_pad, rec, x1, mod, ln_g, ln_b)


def _rel_bucket(dist):
    n = jnp.maximum(dist, 0)
    max_exact = REL_BUCKETS // 2
    nf = jnp.maximum(n, 1).astype(F32)
    large = max_exact + (jnp.log(nf / max_exact) / math.log(REL_MAX_DIST / max_exact)
                         * (REL_BUCKETS - max_exact)).astype(jnp.int32)
    large = jnp.minimum(large, REL_BUCKETS - 1)
    return jnp.where(n < max_exact, n, large)


def _bias_tables(rel_bias):
    tab_d = rel_bias[:, _rel_bucket(jnp.arange(LANE))]
    i = np.arange(QT)[:, None]
    j = np.arange(LANE)[None, :]
    t0 = tab_d[:, np.clip(i - j, 0, LANE - 1)]
    t1 = tab_d[:, np.clip(i - j + QT, 0, LANE - 1)]
    d_c = i - CMP_STRIDE * j + (CMP_STRIDE * CMP_PAD - (CMP_BLOCK - 1))
    bc = tab_d[:, np.clip(d_c, 0, LANE - 1)]
    c31 = jnp.broadcast_to(tab_d[:, LANE - 1][:, None, None], (NSA_HEADS, 1, LANE))
    return t0, t1, bc, c31


def _overlap_matrix(seq):
    ns = seq // CMP_STRIDE
    nslc = seq // SLC_BLOCK
    ov = np.zeros((ns + LANE, LANE), np.float32)
    cs = np.arange(ns - 1) * CMP_STRIDE
    ss = np.arange(nslc) * SLC_BLOCK
    ov[CMP_PAD:CMP_PAD + ns - 1, :nslc] = ((cs[:, None] < ss[None, :] + SLC_BLOCK)
                                           & (cs[:, None] + CMP_BLOCK > ss[None, :]))
    return jnp.asarray(ov, BF16)


def _reorder_cols(a):
    lead = a.shape[:-1]
    gate = a[..., MAIN_COLS:MAIN_COLS + GATE_COLS]
    per = GATE_COLS // NSA_GROUPS
    gate_blocks = []
    for g in range(NSA_GROUPS):
        gate_blocks.append(gate[..., g * per:(g + 1) * per])
        gate_blocks.append(jnp.zeros(lead + (LANE - per,), a.dtype))
    pad = jnp.zeros(lead + ((CB_MGH - CB_GATE - NSA_GROUPS) * LANE,), a.dtype)
    return jnp.concatenate([a[..., :MAIN_COLS]] + gate_blocks + [pad, a[..., MAIN_COLS + GATE_COLS:]], axis=-1)


def kernel(x, c, ada_w, ada_b, w_in, b_in, hg_lb_logits, hg_norm_w, cmp_pos_k, cmp_w1_k, cmp_b1_k, cmp_w2_k, cmp_pos_v, cmp_w1_v, cmp_b1_v, cmp_w2_v, rel_bias, w_br_hg, w_br_nsa, w_out, ln1_g, ln1_b, router_grp_w, router_grp_b, router_exp_w, router_exp_b, exp_w1, exp_w3, exp_w2, ln2_g, ln2_b):
    batch, seq, d = x.shape
    nt = batch * seq
    assert d == D_MODEL and seq % 1024 == 0 and seq // SLC_BLOCK <= LANE
    l = 0
    x2 = x.reshape(nt, d)

    c_pad = jnp.zeros((8, d), F32).at[:batch].set(c)
    mod = _adaln(c_pad, ada_w[l], ada_b[l][None])[:batch].reshape(batch, 6, d)

    proj = _inproj(x2, mod, _reorder_cols(w_in[l]).astype(BF16), _reorder_cols(b_in[l])[None], seq)

    lb_all = jnp.cumsum(jax.nn.softmax(hg_lb_logits.astype(F32), axis=0), axis=0)
    o_h = _hgrn(proj, lb_all[l].reshape(HG_HEADS, 1, HG_DK), hg_norm_w[l].reshape(HG_HEADS, 1, HG_DV),
                batch, seq)

    half = CMP_STRIDE * LANE
    pos = jnp.stack([cmp_pos_k[l].reshape(2, half), cmp_pos_v[l].reshape(2, half)])
    kvc = _compress(proj.reshape(NCB, nt // CMP_STRIDE, half), pos,
                    jnp.stack([cmp_w1_k[l], cmp_w1_v[l]]).astype(BF16),
                    jnp.stack([cmp_b1_k[l], cmp_b1_v[l]])[:, None, :],
                    jnp.stack([cmp_w2_k[l], cmp_w2_v[l]]).astype(BF16), batch, seq)

    t0, t1, bc, c31 = _bias_tables(rel_bias)
    o_c, sel = _cmp_sel(proj, kvc, bc, c31, _overlap_matrix(seq), batch, seq)
    o_w = _window(proj, t0, t1, c31, batch, seq)
    o_s = _selected(proj, sel, t0, t1, c31, batch, seq)

    w_r = jnp.zeros((d, LANE), F32).at[:, :N_GROUPS].set(router_grp_w[l])
    w_r = w_r.at[:, N_GROUPS:N_GROUPS + N_EXPERTS].set(router_exp_w[l])
    b_r = jnp.zeros((1, LANE), F32).at[0, :N_GROUPS].set(router_grp_b[l])
    b_r = b_r.at[0, N_GROUPS:N_GROUPS + N_EXPERTS].set(router_exp_b[l])
    x1, h2, logits = _merge(o_h, o_c, o_s, o_w, proj, x2, mod,
                            w_br_hg[l].astype(BF16), w_br_nsa[l].astype(BF16), w_out[l].astype(BF16),
                            ln1_g[l][None], ln1_b[l][None], w_r, b_r, seq)

    rec, cnt = _route(logits)
    counts = cnt[0, N_GROUPS:N_GROUPS + N_EXPERTS].astype(jnp.int32)
    padded = (counts + MOE_ROWS - 1) // MOE_ROWS * MOE_ROWS
    pend = jnp.cumsum(padded)
    pstart = pend - padded
    n_assign = 2 * nt
    nb = n_assign // MOE_ROWS + N_EXPERTS
    expert = rec[:, 0:2].astype(jnp.int32)
    dest = (pstart[expert] + rec[:, 4:6].astype(jnp.int32)).reshape(-1)
    block_expert = jnp.minimum(
        jnp.searchsorted(pend, jnp.arange(nb, dtype=jnp.int32) * MOE_ROWS, side='right'),
        N_EXPERTS - 1).astype(jnp.int32)
    n_used = (pend[-1:] // MOE_ROWS).astype(jnp.int32)

    x_pad = _dispatch(dest, h2, jnp.zeros((nb * MOE_ROWS, d), F32))
    y_pad = _experts(block_expert, n_used, x_pad,
                     exp_w1[l].astype(BF16), exp_w3[l].astype(BF16), exp_w2[l].astype(BF16))
    out = _combine(dest, y_pad, rec, x1, mod, ln2_g[l][None], ln2_b[l][None], seq)
    return out.reshape(batch, seq, d)
```

```python
import functools
import math

import numpy as np
import jax
import jax.numpy as jnp
from jax import lax
from jax.experimental import pallas as pl
from jax.experimental.pallas import tpu as pltpu

F32 = jnp.float32
BF16 = jnp.bfloat16

D_MODEL = 1024
HG_HEADS = 8
HG_DK = 128
HG_DV = 128
HG_CHUNK = 32
NSA_HEADS = 8
NSA_GROUPS = 2
NSA_REP = NSA_HEADS // NSA_GROUPS
NSA_DK = 128
CMP_BLOCK = 32
CMP_STRIDE = 16
SLC_BLOCK = 64
SLC_TOPK = 16
WINDOW = 512
FORCE_SCORE = 1e4
REL_BUCKETS = 32
REL_MAX_DIST = 128
N_GROUPS = 4
EXP_PER_GROUP = 8
N_EXPERTS = N_GROUPS * EXP_PER_GROUP
D_EXPERT = D_MODEL // 2
DEPTH = 1
ALPHA = (2 * DEPTH) ** 0.25

LANE = 128
QT = 128
NEG = -1e30
SCALE = NSA_DK ** -0.5
LOG2E = math.log2(math.e)
SCALE_LOG2 = SCALE * LOG2E
KEY_BLK = 512
CMP_PAD = 120
VMEM_LIMIT = 56 * 1024 * 1024

CB_HQ, CB_HF, CB_HI, CB_HG = 0, 8, 16, 24
CB_NQ = 32
CB_KC, CB_VC, CB_KS, CB_VS, CB_KW, CB_VW = 40, 42, 44, 46, 48, 50
CB_GATE = 52
CB_MGH, CB_MGN = 56, 64
NCB = 72
MAIN_COLS = 52 * LANE
GATE_COLS = 3 * NSA_HEADS

MOE_ROWS = 512


def _cparams(sem):
    return pltpu.CompilerParams(dimension_semantics=sem, vmem_limit_bytes=VMEM_LIMIT)


def _dot(a, b):
    return jnp.dot(a, b, preferred_element_type=F32)


def _dot_nt(a, b):
    return lax.dot_general(a, b, (((1,), (1,)), ((), ())), preferred_element_type=F32)


def _dot_tn(a, b):
    return lax.dot_general(a, b, (((0,), (0,)), ((), ())), preferred_element_type=F32)


def _split3(x):
    hi = x.astype(BF16)
    r = x - hi.astype(F32)
    mid = r.astype(BF16)
    lo = (r - mid.astype(F32)).astype(BF16)
    return hi, mid, lo


def _dot01(m01, x):
    hi, mid, lo = _split3(x)
    return _dot(m01, hi) + _dot(m01, mid) + _dot(m01, lo)


def _dot01_r(x, m01):
    hi, mid, lo = _split3(x)
    return _dot(hi, m01) + _dot(mid, m01) + _dot(lo, m01)


def _layer_norm(x, eps=1e-5):
    mu = jnp.mean(x, axis=-1, keepdims=True)
    xc = x - mu
    var = jnp.mean(xc * xc, axis=-1, keepdims=True)
    return xc * lax.rsqrt(var + eps)


def _adaln_kernel(c_ref, w_ref, b_ref, o_ref):
    c = c_ref[...]
    ca = c * jax.nn.sigmoid(c)
    o_ref[...] = jnp.dot(ca, w_ref[...], precision=lax.Precision.HIGHEST,
                         preferred_element_type=F32) + b_ref[...]


def _adaln(c_pad, w, b):
    rows, d = c_pad.shape
    n = w.shape[1]
    return pl.pallas_call(
        _adaln_kernel,
        grid=(n // d,),
        in_specs=[pl.BlockSpec((rows, d), lambda j: (0, 0)),
                  pl.BlockSpec((d, d), lambda j: (0, j)),
                  pl.BlockSpec((1, d), lambda j: (0, j))],
        out_specs=pl.BlockSpec((rows, d), lambda j: (0, j)),
        out_shape=jax.ShapeDtypeStruct((rows, n), F32),
        compiler_params=_cparams(("arbitrary",)),
        name="adaln",
    )(c_pad, w, b)


def _inproj_kernel(x_ref, mod_ref, w_ref, b_ref, o_ref, hn_ref, *, ncb_tile):
    @pl.when(pl.program_id(1) == 0)
    def _():
        hn = _layer_norm(x_ref[...])
        sh = mod_ref[0, 0:1, :]
        sc = mod_ref[0, 1:2, :]
        hn_ref[...] = (hn * (1.0 + sc) + sh).astype(BF16)

    res = _dot(hn_ref[...], w_ref[...]) + b_ref[...]
    for c in range(ncb_tile):
        o_ref[c] = res[:, c * LANE:(c + 1) * LANE]


def _inproj(x2, mod, w, b, seq):
    nt, d = x2.shape
    tm = min(1024, seq)
    ncb_tile = 9
    tn = ncb_tile * LANE
    return pl.pallas_call(
        functools.partial(_inproj_kernel, ncb_tile=ncb_tile),
        grid=(nt // tm, NCB // ncb_tile),
        in_specs=[pl.BlockSpec((tm, d), lambda i, j: (i, 0)),
                  pl.BlockSpec((1, 6, d), lambda i, j: (i * tm // seq, 0, 0)),
                  pl.BlockSpec((d, tn), lambda i, j: (0, j)),
                  pl.BlockSpec((1, tn), lambda i, j: (0, j))],
        out_specs=pl.BlockSpec((ncb_tile, tm, LANE), lambda i, j: (j, i, 0)),
        out_shape=jax.ShapeDtypeStruct((NCB, nt, LANE), F32),
        scratch_shapes=[pltpu.VMEM((tm, d), BF16)],
        compiler_params=_cparams(("arbitrary", "arbitrary")),
        name="inproj",
    )(x2, mod, w, b)


def _hgrn_kernel(q_ref, f_ref, v_ref, g_ref, lb_ref, nw_ref, o_ref, st_ref, *, rows):
    @pl.when(pl.program_id(2) == 0)
    def _():
        st_ref[...] = jnp.zeros_like(st_ref)

    lb = lb_ref[0]
    nw = nw_ref[0]
    ri = lax.broadcasted_iota(jnp.int32, (LANE, LANE), 0)
    ci = lax.broadcasted_iota(jnp.int32, (LANE, LANE), 1)
    same = (ri // HG_CHUNK) == (ci // HG_CHUNK)
    tril = same & (ci <= ri)
    cum_m = jnp.where(tril, 1.0, 0.0).astype(BF16)
    tot_m = jnp.where(same, 1.0, 0.0).astype(BF16)
    per = LANE // HG_CHUNK

    def body(i, carry):
        r0 = pl.multiple_of(i * LANE, LANE)
        q = q_ref[0, pl.ds(r0, LANE), :]
        fr = f_ref[0, pl.ds(r0, LANE), :]
        v = v_ref[0, pl.ds(r0, LANE), :]
        g = g_ref[0, pl.ds(r0, LANE), :]
        f = lb + (1.0 - lb) * jax.nn.sigmoid(fr)
        lf = jnp.log(f)
        k = 1.0 - f
        hi, mid, lo = _split3(lf)
        b = _dot(cum_m, hi) + _dot(cum_m, mid) + _dot(cum_m, lo)
        bl = _dot(tot_m, hi) + _dot(tot_m, mid) + _dot(tot_m, lo)
        q_in = (q * jnp.exp(b)).astype(BF16)
        k_in = (k * jnp.exp(-b)).astype(BF16)
        k_end = (k * jnp.exp(bl - b)).astype(BF16)
        vb = v.astype(BF16)
        att = jnp.where(tril, _dot_nt(q_in, k_in), 0.0)
        o = _dot(att.astype(BF16), vb)
        st = st_ref[...]
        inter = []
        for c in range(per):
            sl = slice(c * HG_CHUNK, (c + 1) * HG_CHUNK)
            inter.append(_dot_nt(q_in[sl], st.astype(BF16)))
            dec = jnp.exp(bl[c * HG_CHUNK:c * HG_CHUNK + 1, :])
            st = dec * st + _dot_tn(vb[sl], k_end[sl])
        st_ref[...] = st
        o = o + jnp.concatenate(inter, axis=0)
        ms = jnp.mean(o * o, axis=-1, keepdims=True)
        on = o * lax.rsqrt(ms + 1e-6) * nw * jax.nn.sigmoid(g)
        o_ref[pl.ds(r0, LANE), :] = on.astype(BF16)
        return carry

    lax.fori_loop(0, rows // LANE, body, 0)


def _hgrn(proj, lb, nw, batch, seq):
    nt = proj.shape[1]
    tb = min(1024, seq)
    nblk = seq // tb

    def slab(cb0):
        return pl.BlockSpec((1, tb, LANE), lambda b, h, t: (cb0 + h, b * nblk + t, 0))

    vec = pl.BlockSpec((1, 1, LANE), lambda b, h, t: (h, 0, 0))
    return pl.pallas_call(
        functools.partial(_hgrn_kernel, rows=tb),
        grid=(batch, HG_HEADS, nblk),
        in_specs=[slab(CB_HQ), slab(CB_HF), slab(CB_HI), slab(CB_HG), vec, vec],
        out_specs=pl.BlockSpec((tb, LANE), lambda b, h, t: (b * nblk + t, h)),
        out_shape=jax.ShapeDtypeStruct((nt, HG_HEADS * HG_DV), BF16),
        scratch_shapes=[pltpu.VMEM((HG_DV, HG_DK), F32)],
        compiler_params=_cparams(("arbitrary", "arbitrary", "arbitrary")),
        name="hgrn2",
    )(proj, proj, proj, proj, lb, nw)


def _compress_kernel(sub_ref, pos_ref, w1_ref, b1_ref, w2_ref, o_ref, *, ns):
    half = CMP_STRIDE * LANE
    sub = sub_ref[0]
    xa = (sub + pos_ref[0, 0:1, :]).astype(BF16)
    xb = (sub + pos_ref[0, 1:2, :]).astype(BF16)
    p0 = _dot(xa, w1_ref[0, 0:half, :])
    p1 = _dot(xb, w1_ref[0, half:2 * half, :])
    h = p0 + pltpu.roll(p1, ns - 1, axis=0) + b1_ref[0]
    a = h * jax.nn.sigmoid(h)
    out = _dot(a.astype(BF16), w2_ref[0])
    row = lax.broadcasted_iota(jnp.int32, out.shape, 0)
    out = jnp.where(row < ns - 1, out, 0.0)
    o_ref[0, 0, 0, 0:CMP_PAD, :] = jnp.zeros((CMP_PAD, LANE), F32)
    o_ref[0, 0, 0, CMP_PAD:CMP_PAD + ns, :] = out
    o_ref[0, 0, 0, CMP_PAD + ns:, :] = jnp.zeros((LANE - CMP_PAD, LANE), F32)


def _compress(proj16, pos, w1, b1, w2, batch, seq):
    ns = seq // CMP_STRIDE
    np_rows = ns + LANE
    width = CMP_STRIDE * LANE
    return pl.pallas_call(
        functools.partial(_compress_kernel, ns=ns),
        grid=(2, batch, NSA_GROUPS),
        in_specs=[pl.BlockSpec((1, ns, width), lambda s, b, g: (NSA_GROUPS * s + g, b, 0)),
                  pl.BlockSpec((1, 2, width), lambda s, b, g: (s, 0, 0)),
                  pl.BlockSpec((1, 2 * width, LANE), lambda s, b, g: (s, 0, 0)),
                  pl.BlockSpec((1, 1, LANE), lambda s, b, g: (s, 0, 0)),
                  pl.BlockSpec((1, LANE, LANE), lambda s, b, g: (s, 0, 0))],
        out_specs=pl.BlockSpec((1, 1, 1, np_rows, LANE), lambda s, b, g: (s, b, g, 0, 0)),
        out_shape=jax.ShapeDtypeStruct((2, batch, NSA_GROUPS, np_rows, LANE), F32),
        compiler_params=_cparams(("arbitrary", "arbitrary", "arbitrary")),
        name="nsa_compress",
    )(proj16, pos, w1, b1, w2)


def _stack_q(q_refs):
    return jnp.concatenate([r[0] for r in q_refs], axis=0).astype(BF16)


def _cmp_sel_kernel(q0, q1, q2, q3, kc_ref, vc_ref, gate_ref, bc_ref, c31_ref, ov_ref,
                    oc_ref, sel_ref, *, np_rows):
    ti = pl.program_id(2)
    rows = NSA_REP * QT
    qb = _stack_q((q0, q1, q2, q3))
    w0 = pl.multiple_of(ti * (QT // CMP_STRIDE), 8)

    kall = kc_ref[0, 0, 0].astype(BF16)
    s_all = _dot_nt(qb, kall) * SCALE
    c31 = jnp.concatenate(
        [jnp.broadcast_to(c31_ref[r][:, 0:1], (QT, 1)) for r in range(NSA_REP)], axis=0)
    pidx = lax.broadcasted_iota(jnp.int32, (rows, np_rows), 1)
    m_old = (pidx >= CMP_PAD) & (pidx < w0)
    s_all = jnp.where(m_old, s_all + c31, NEG)

    kwin = kc_ref[0, 0, 0, pl.ds(w0, LANE), :].astype(BF16)
    bias = jnp.concatenate([bc_ref[r] for r in range(NSA_REP)], axis=0)
    s_rec = _dot_nt(qb, kwin) * SCALE + bias
    ii = lax.broadcasted_iota(jnp.int32, (rows, LANE), 0) % QT
    mm = lax.broadcasted_iota(jnp.int32, (rows, LANE), 1)
    dist = ii - CMP_STRIDE * mm + (CMP_STRIDE * CMP_PAD - (CMP_BLOCK - 1))
    m_rec = (dist >= 0) & (mm >= CMP_PAD - w0)
    s_rec = jnp.where(m_rec, s_rec, NEG)

    mx = jnp.maximum(jnp.max(s_all, axis=-1, keepdims=True), jnp.max(s_rec, axis=-1, keepdims=True))
    e_all = jnp.where(m_old, jnp.exp(s_all - mx), 0.0)
    e_rec = jnp.where(m_rec, jnp.exp(s_rec - mx), 0.0)
    den = jnp.sum(e_all, axis=-1, keepdims=True) + jnp.sum(e_rec, axis=-1, keepdims=True)
    inv = 1.0 / jnp.where(den == 0.0, 1.0, den)
    p_all = e_all * inv
    p_rec = e_rec * inv

    vall = vc_ref[0, 0, 0].astype(BF16)
    vwin = vc_ref[0, 0, 0, pl.ds(w0, LANE), :].astype(BF16)
    o = _dot(p_all.astype(BF16), vall) + _dot(p_rec.astype(BF16), vwin)
    gt = jax.nn.sigmoid(gate_ref[0])
    for r in range(NSA_REP):
        oc_ref[:, r * LANE:(r + 1) * LANE] = gt[:, 3 * r:3 * r + 1] * o[r * QT:(r + 1) * QT]

    ps_all = p_all[0:QT]
    ps_rec = p_rec[0:QT]
    for r in range(1, NSA_REP):
        ps_all = ps_all + p_all[r * QT:(r + 1) * QT]
        ps_rec = ps_rec + p_rec[r * QT:(r + 1) * QT]
    imp = _dot01_r(ps_all, ov_ref[...]) + _dot01_r(ps_rec, ov_ref[pl.ds(w0, LANE), :])

    i1 = lax.broadcasted_iota(jnp.int32, (QT, LANE), 0)
    jj = lax.broadcasted_iota(jnp.int32, (QT, LANE), 1)
    cur = (ti * QT + i1) // SLC_BLOCK
    forced = (jj == 0) | (jj == cur) | (jj == cur - 1)
    score = jnp.where(jj <= cur, jnp.where(forced, FORCE_SCORE, imp), -1.0)
    sel = jnp.zeros((QT, LANE), F32)
    jf = jj.astype(F32)
    for _ in range(SLC_TOPK):
        mval = jnp.max(score, axis=-1, keepdims=True)
        first = jnp.min(jnp.where(score == mval, jf, float(LANE)), axis=-1, keepdims=True)
        pick = jf == first
        sel = jnp.where(pick, 1.0, sel)
        score = jnp.where(pick, -jnp.inf, score)
    sel_ref[0, 0] = jnp.where(sel.T > 0.5, 0.0, NEG)


def _q_specs(nqt):
    return [pl.BlockSpec((1, QT, LANE),
                         functools.partial(lambda b, g, t, r: (CB_NQ + NSA_REP * g + r, b * nqt + t, 0), r=r))
            for r in range(NSA_REP)]


def _cmp_sel(proj, kvc, bc, c31, ov, batch, seq):
    nt = proj.shape[1]
    nqt = seq // QT
    np_rows = kvc.shape[3]
    kv_spec = lambda s: pl.BlockSpec((1, 1, 1, np_rows, LANE), lambda b, g, t: (s, b, g, 0, 0))
    return pl.pallas_call(
        functools.partial(_cmp_sel_kernel, np_rows=np_rows),
        grid=(batch, NSA_GROUPS, nqt),
        in_specs=_q_specs(nqt) + [
            kv_spec(0), kv_spec(1),
            pl.BlockSpec((1, QT, LANE), lambda b, g, t: (CB_GATE + g, b * nqt + t, 0)),
            pl.BlockSpec((NSA_REP, QT, LANE), lambda b, g, t: (g, 0, 0)),
            pl.BlockSpec((NSA_REP, 1, LANE), lambda b, g, t: (g, 0, 0)),
            pl.BlockSpec((np_rows, LANE), lambda b, g, t: (0, 0))],
        out_specs=[pl.BlockSpec((QT, NSA_REP * LANE), lambda b, g, t: (b * nqt + t, g)),
                   pl.BlockSpec((1, 1, LANE, QT), lambda b, g, t: (b, g, 0, t))],
        out_shape=[jax.ShapeDtypeStruct((nt, NSA_HEADS * LANE), F32),
                   jax.ShapeDtypeStruct((batch, NSA_GROUPS, LANE, seq), F32)],
        compiler_params=_cparams(("arbitrary", "arbitrary", "arbitrary")),
        name="nsa_cmp_select",
    )(proj, proj, proj, proj, kvc, kvc, proj, bc, c31, ov)


def _q_transposed(q_refs):
    return jnp.concatenate([r[0].T for r in q_refs], axis=1).astype(BF16)


def _softmax_step(s, vt, state):
    m, l, acc = state
    m_new = jnp.maximum(m, jnp.max(s, axis=0, keepdims=True))
    alpha = jnp.exp2(m - m_new)
    p = jnp.exp2(s - m_new)
    l = alpha * l + jnp.sum(p, axis=0, keepdims=True)
    acc = alpha * acc + _dot(vt, p.astype(BF16))
    return m_new, l, acc


def _finish_t(state, gate_ref, branch, o_ref):
    _, l, acc = state
    o = acc / jnp.where(l == 0.0, 1.0, l)
    gt = jax.nn.sigmoid(gate_ref[0])
    for r in range(NSA_REP):
        col = 3 * r + branch
        o_ref[:, r * LANE:(r + 1) * LANE] = gt[:, col:col + 1] * o[:, r * QT:(r + 1) * QT].T


def _init_state(cols):
    return (jnp.full((1, cols), NEG, F32), jnp.zeros((1, cols), F32), jnp.zeros((LANE, cols), F32))


def _win_kernel(q0, q1, q2, q3, k_ref, v_ref, gate_ref, wb_ref, o_ref, kb_ref, vt_ref, *, seq):
    ti = pl.program_id(2)
    cols = NSA_REP * QT
    wt = WINDOW // QT

    @pl.when(ti == 0)
    def _():
        kb_ref[0:WINDOW, :] = jnp.zeros((WINDOW, LANE), BF16)
        for i in range(wt):
            vt_ref[i] = jnp.zeros((LANE, QT), BF16)

        def fill(i, carry):
            r0 = pl.multiple_of(i * QT, QT)
            kb_ref[pl.ds(WINDOW + r0, QT), :] = k_ref[0, pl.ds(r0, QT), :].astype(BF16)
            vt_ref[wt + i] = v_ref[0, pl.ds(r0, QT), :].T.astype(BF16)
            return carry

        lax.fori_loop(0, seq // QT, fill, 0)

    qt = _q_transposed((q0, q1, q2, q3))
    r0 = pl.multiple_of(ti * QT, QT)
    s = _dot(kb_ref[pl.ds(r0, WINDOW + QT), :], qt) * SCALE_LOG2 + wb_ref[0]
    rho = lax.broadcasted_iota(jnp.int32, (WINDOW + QT, cols), 0)
    s = jnp.where(rho >= (wt - ti) * QT, s, NEG)
    m = jnp.max(s, axis=0, keepdims=True)
    p = jnp.exp2(s - m)
    l = jnp.sum(p, axis=0, keepdims=True)
    pb = p.astype(BF16)
    acc = _dot(vt_ref[ti], pb[0:QT])
    for d in range(1, wt + 1):
        acc = acc + _dot(vt_ref[ti + d], pb[d * QT:(d + 1) * QT])
    _finish_t((m, l, acc), gate_ref, 2, o_ref)


def _window(proj, wb, batch, seq):
    nt = proj.shape[1]
    nqt = seq // QT
    cols = NSA_REP * QT
    slab = lambda cb0: pl.BlockSpec((1, seq, LANE), lambda b, g, t: (cb0 + g, b, 0))
    return pl.pallas_call(
        functools.partial(_win_kernel, seq=seq),
        grid=(batch, NSA_GROUPS, nqt),
        in_specs=_q_specs(nqt) + [
            slab(CB_KW), slab(CB_VW),
            pl.BlockSpec((1, QT, LANE), lambda b, g, t: (CB_GATE + g, b * nqt + t, 0)),
            pl.BlockSpec((1, WINDOW + QT, cols), lambda b, g, t: (g, 0, 0))],
        out_specs=pl.BlockSpec((QT, cols), lambda b, g, t: (b * nqt + t, g)),
        out_shape=jax.ShapeDtypeStruct((nt, NSA_HEADS * LANE), F32),
        scratch_shapes=[pltpu.VMEM((seq + WINDOW, LANE), BF16),
                        pltpu.VMEM((seq // QT + WINDOW // QT, LANE, QT), BF16)],
        compiler_params=_cparams(("arbitrary", "arbitrary", "arbitrary")),
        name="nsa_window",
    )(proj, proj, proj, proj, proj, proj, proj, wb)


def _sel_kernel(q0, q1, q2, q3, k_ref, v_ref, sel_ref, gate_ref, t0_ref, t1_ref, o_ref,
                kb_ref, vt_ref, bias_ref, m_ref, l_ref, acc_ref, *, seq):
    ti = pl.program_id(2)
    cols = NSA_REP * QT
    tpb = KEY_BLK // QT
    spb = KEY_BLK // SLC_BLOCK
    zero_tile = jnp.zeros((QT, cols), F32)

    def near_rows(t):
        r = t % tpb
        return (pl.multiple_of(KEY_BLK + r * QT, QT), pl.multiple_of((r + tpb - 1) * QT, QT))

    @pl.when(ti == 0)
    def _():
        def fill(i, carry):
            r0 = pl.multiple_of(i * QT, QT)
            kb_ref[pl.ds(r0, QT), :] = k_ref[0, pl.ds(r0, QT), :].astype(BF16)
            vt = v_ref[0, pl.ds(r0, QT), :].T.astype(BF16)
            for c in range(tpb):
                @pl.when(i % tpb == c)
                def _():
                    vt_ref[i // tpb, :, c * QT:(c + 1) * QT] = vt
            return carry

        lax.fori_loop(0, seq // QT, fill, 0)
        bias_ref[...] = jnp.zeros_like(bias_ref)

    @pl.when(ti > 0)
    def _():
        p0, p1 = near_rows(ti - 1)
        bias_ref[pl.ds(p0, QT), :] = zero_tile
        bias_ref[pl.ds(p1, QT), :] = zero_tile

    p0, p1 = near_rows(ti)
    bias_ref[pl.ds(p1, QT), :] = t1_ref[0]
    bias_ref[pl.ds(p0, QT), :] = t0_ref[0]

    qt = _q_transposed((q0, q1, q2, q3))
    last = ti // tpb

    def block(kb):
        r0 = pl.multiple_of(kb * KEY_BLK, KEY_BLK)
        return kb_ref[pl.ds(r0, KEY_BLK), :], vt_ref[kb]

    def sel_bias(kb):
        rows = sel_ref[0, 0, pl.ds(pl.multiple_of(kb * spb, spb), spb), :]
        one = jnp.concatenate(
            [jnp.broadcast_to(rows[r:r + 1], (SLC_BLOCK, QT)) for r in range(spb)], axis=0)
        return jnp.concatenate([one] * NSA_REP, axis=1)

    diff = (lax.broadcasted_iota(jnp.int32, (KEY_BLK, cols), 0)
            - lax.broadcasted_iota(jnp.int32, (KEY_BLK, cols), 1) % QT)

    def near(kb_true, slot, state):
        kb = jnp.maximum(kb_true, 0)
        thr = jnp.where(kb_true >= 0, ti * QT - kb_true * KEY_BLK, -KEY_BLK)
        k, vt = block(kb)
        s = _dot(k, qt) * SCALE_LOG2 + bias_ref[slot * KEY_BLK:(slot + 1) * KEY_BLK, :] + sel_bias(kb)
        s = jnp.where(diff <= thr, s, NEG)
        return _softmax_step(s, vt, state)

    state = near(last, 1, _init_state(cols))
    state = near(last - 1, 0, state)
    m_ref[...], l_ref[...], acc_ref[...] = state

    def body(kb, carry):
        k, vt = block(kb)
        s = _dot(k, qt) * SCALE_LOG2 + sel_bias(kb)
        st = _softmax_step(s, vt, (m_ref[...], l_ref[...], acc_ref[...]))
        m_ref[...], l_ref[...], acc_ref[...] = st
        return carry

    lax.fori_loop(0, jnp.maximum(last - 1, 0), body, 0)
    _finish_t((m_ref[...], l_ref[...], acc_ref[...]), gate_ref, 1, o_ref)


def _selected(proj, sel_t, t0t, t1t, batch, seq):
    nt = proj.shape[1]
    nqt = seq // QT
    cols = NSA_REP * QT
    slab = lambda cb0: pl.BlockSpec((1, seq, LANE), lambda b, g, t: (cb0 + g, b, 0))
    tab = pl.BlockSpec((1, LANE, cols), lambda b, g, t: (g, 0, 0))
    return pl.pallas_call(
        functools.partial(_sel_kernel, seq=seq),
        grid=(batch, NSA_GROUPS, nqt),
        in_specs=_q_specs(nqt) + [
            slab(CB_KS), slab(CB_VS),
            pl.BlockSpec((1, 1, LANE, QT), lambda b, g, t: (b, g, 0, t)),
            pl.BlockSpec((1, QT, LANE), lambda b, g, t: (CB_GATE + g, b * nqt + t, 0)),
            tab, tab],
        out_specs=pl.BlockSpec((QT, cols), lambda b, g, t: (b * nqt + t, g)),
        out_shape=jax.ShapeDtypeStruct((nt, NSA_HEADS * LANE), F32),
        scratch_shapes=[pltpu.VMEM((seq, LANE), BF16),
                        pltpu.VMEM((seq // KEY_BLK, LANE, KEY_BLK), BF16),
                        pltpu.VMEM((2 * KEY_BLK, cols), F32),
                        pltpu.VMEM((1, cols), F32), pltpu.VMEM((1, cols), F32),
                        pltpu.VMEM((LANE, cols), F32)],
        compiler_params=_cparams(("arbitrary", "arbitrary", "arbitrary")),
        name="nsa_selected",
    )(proj, proj, proj, proj, proj, proj, sel_t, proj, t0t, t1t)


def _merge_kernel(oh_ref, oc_ref, os_ref, ow_ref, mgh_ref, mgn_ref, x_ref, mod_ref,
                  wh_ref, wn_ref, wo_ref, g_ref, b_ref, wr_ref, br_ref,
                  x1_ref, h2_ref, lg_ref):
    nblk = D_MODEL // LANE
    a_h = _dot(oh_ref[...], wh_ref[...])
    o_n = (oc_ref[...] + os_ref[...] + ow_ref[...]).astype(BF16)
    a_n = _dot(o_n, wn_ref[...])
    gh = jnp.concatenate([mgh_ref[c] for c in range(nblk)], axis=-1)
    gn = jnp.concatenate([mgn_ref[c] for c in range(nblk)], axis=-1)
    merged = jax.nn.sigmoid(gh) * a_h + jax.nn.sigmoid(gn) * a_n
    y = (1.0 + mod_ref[0, 2:3, :]) * _dot(merged.astype(BF16), wo_ref[...])
    x1 = _layer_norm(ALPHA * x_ref[...] + y) * g_ref[...] + b_ref[...]
    x1_ref[...] = x1
    h2 = _layer_norm(x1) * (1.0 + mod_ref[0, 4:5, :]) + mod_ref[0, 3:4, :]
    h2_ref[...] = h2
    lg_ref[...] = jnp.dot(h2, wr_ref[...], precision=lax.Precision.HIGHEST,
                          preferred_element_type=F32) + br_ref[...]


def _merge(o_h, o_c, o_s, o_w, proj, x2, mod, w_h, w_n, w_o, ln_g, ln_b, w_r, b_r, seq):
    nt, d = x2.shape
    tm = min(256, seq)
    nblk = d // LANE
    row = lambda w: pl.BlockSpec((tm, w), lambda i: (i, 0))
    full = lambda a: pl.BlockSpec(a.shape, lambda i: (0,) * a.ndim)
    return pl.pallas_call(
        _merge_kernel,
        grid=(nt // tm,),
        in_specs=[row(d), row(d), row(d), row(d),
                  pl.BlockSpec((nblk, tm, LANE), lambda i: (CB_MGH // nblk, i, 0)),
                  pl.BlockSpec((nblk, tm, LANE), lambda i: (CB_MGN // nblk, i, 0)),
                  row(d),
                  pl.BlockSpec((1, 6, d), lambda i: (i * tm // seq, 0, 0)),
                  full(w_h), full(w_n), full(w_o), full(ln_g), full(ln_b), full(w_r), full(b_r)],
        out_specs=[row(d), row(d), row(LANE)],
        out_shape=[jax.ShapeDtypeStruct((nt, d), F32),
                   jax.ShapeDtypeStruct((nt, d), F32),
                   jax.ShapeDtypeStruct((nt, LANE), F32)],
        compiler_params=_cparams(("arbitrary",)),
        name="merge_outproj",
    )(o_h, o_c, o_s, o_w, proj, proj, x2, mod, w_h, w_n, w_o, ln_g, ln_b, w_r, b_r)


def _route_kernel(lg_ref, rec_ref, cnt_ref, carry_ref, *, tm):
    @pl.when(pl.program_id(0) == 0)
    def _():
        carry_ref[...] = jnp.zeros_like(carry_ref)

    lg = lg_ref[...]
    lane = lax.broadcasted_iota(jnp.int32, (tm, LANE), 1).astype(F32)
    far = float(LANE)
    gmask = lane < N_GROUPS
    gl = jnp.where(gmask, lg, -jnp.inf)
    gmax = jnp.max(gl, axis=-1, keepdims=True)
    gsum = jnp.sum(jnp.where(gmask, jnp.exp(gl - gmax), 0.0), axis=-1, keepdims=True)
    grp_p = 1.0 / gsum
    gidx = jnp.min(jnp.where(gl == gmax, lane, far), axis=-1, keepdims=True)
    lo = N_GROUPS + EXP_PER_GROUP * gidx
    emask = (lane >= lo) & (lane < lo + EXP_PER_GROUP)
    el = jnp.where(emask, lg, -jnp.inf)
    m1 = jnp.max(el, axis=-1, keepdims=True)
    i1 = jnp.min(jnp.where(el == m1, lane, far), axis=-1, keepdims=True)
    el2 = jnp.where(lane == i1, -jnp.inf, el)
    m2 = jnp.max(el2, axis=-1, keepdims=True)
    i2 = jnp.min(jnp.where(emask & (lane != i1) & (el2 == m2), lane, far), axis=-1, keepdims=True)
    e = jnp.exp(m2 - m1)
    w0 = grp_p / (1.0 + e)
    w1 = grp_p * e / (1.0 + e)

    oh0 = lane == i1
    oh1 = lane == i2
    f0 = jnp.where(oh0, 1.0, 0.0)
    f1 = jnp.where(oh1, 1.0, 0.0)
    ri = lax.broadcasted_iota(jnp.int32, (tm, tm), 0)
    ci = lax.broadcasted_iota(jnp.int32, (tm, tm), 1)
    before = jnp.where(ci < ri, 1.0, 0.0).astype(BF16)
    cum0 = _dot(before, f0.astype(BF16))
    cum1 = _dot(before, f1.astype(BF16))
    tot0 = jnp.sum(f0, axis=0, keepdims=True)
    tot1 = jnp.sum(f1, axis=0, keepdims=True)
    carry = carry_ref[...]
    rank0 = jnp.sum(jnp.where(oh0, carry + cum0, 0.0), axis=-1, keepdims=True)
    rank1 = jnp.sum(jnp.where(oh1, carry + tot0 + cum1, 0.0), axis=-1, keepdims=True)
    carry = carry + tot0 + tot1
    carry_ref[...] = carry
    cnt_ref[...] = carry

    rec = jnp.where(lane == 0, i1 - N_GROUPS, 0.0)
    rec = jnp.where(lane == 1, i2 - N_GROUPS, rec)
    rec = jnp.where(lane == 2, w0, rec)
    rec = jnp.where(lane == 3, w1, rec)
    rec = jnp.where(lane == 4, rank0, rec)
    rec = jnp.where(lane == 5, rank1, rec)
    rec_ref[...] = rec


def _route(logits):
    nt = logits.shape[0]
    tm = min(512, nt)
    return pl.pallas_call(
        functools.partial(_route_kernel, tm=tm),
        grid=(nt // tm,),
        in_specs=[pl.BlockSpec((tm, LANE), lambda i: (i, 0))],
        out_specs=[pl.BlockSpec((tm, LANE), lambda i: (i, 0)),
                   pl.BlockSpec((1, LANE), lambda i: (0, 0))],
        out_shape=[jax.ShapeDtypeStruct((nt, LANE), F32),
                   jax.ShapeDtypeStruct((1, LANE), F32)],
        scratch_shapes=[pltpu.VMEM((1, LANE), F32)],
        compiler_params=_cparams(("arbitrary",)),
        name="moe_route",
    )(logits)


def _row_copy(src, dst, sem):
    return pltpu.make_async_copy(src, dst, sem)


def _dispatch_kernel(dest_ref, h_ref, zero_ref, xp_ref, sem, *, tm):
    del zero_ref
    base = pl.program_id(0) * tm

    def issue(r, carry):
        for k in range(2):
            d = dest_ref[2 * (base + r) + k]
            _row_copy(h_ref.at[pl.ds(r, 1), :], xp_ref.at[pl.ds(d, 1), :], sem).start()
        return carry

    lax.fori_loop(0, tm, issue, 0)

    def drain(r, carry):
        _row_copy(h_ref.at[pl.ds(0, 1), :], xp_ref.at[pl.ds(0, 1), :], sem).wait()
        return carry

    lax.fori_loop(0, 2 * tm, drain, 0)


def _dispatch(dest, h2, x_pad0):
    nt, d = h2.shape
    tm = min(256, nt)
    return pl.pallas_call(
        functools.partial(_dispatch_kernel, tm=tm),
        grid_spec=pltpu.PrefetchScalarGridSpec(
            num_scalar_prefetch=1,
            grid=(nt // tm,),
            in_specs=[pl.BlockSpec((tm, d), lambda i, dest: (i, 0)),
                      pl.BlockSpec(memory_space=pl.ANY)],
            out_specs=pl.BlockSpec(memory_space=pl.ANY),
            scratch_shapes=[pltpu.SemaphoreType.DMA(())]),
        out_shape=jax.ShapeDtypeStruct(x_pad0.shape, x_pad0.dtype),
        input_output_aliases={2: 0},
        compiler_params=_cparams(("arbitrary",)),
        name="moe_dispatch",
    )(dest, h2, x_pad0)


def _expert_kernel(be_ref, nu_ref, x_ref, w1_ref, w3_ref, w2_ref, y_ref):
    del be_ref
    i = pl.program_id(0)

    @pl.when(i < nu_ref[0])
    def _():
        xb = x_ref[...].astype(BF16)
        a = _dot(xb, w1_ref[0])
        b = _dot(xb, w3_ref[0])
        hmid = (a * jax.nn.sigmoid(a) * b).astype(BF16)
        y_ref[...] = _dot(hmid, w2_ref[0])

    @pl.when(i >= nu_ref[0])
    def _():
        y_ref[...] = jnp.zeros_like(y_ref)


def _experts(block_expert, n_used, x_pad, w1, w3, w2):
    npad, d = x_pad.shape
    de = w1.shape[2]
    nb = npad // MOE_ROWS
    return pl.pallas_call(
        _expert_kernel,
        grid_spec=pltpu.PrefetchScalarGridSpec(
            num_scalar_prefetch=2,
            grid=(nb,),
            in_specs=[pl.BlockSpec((MOE_ROWS, d), lambda i, be, nu: (i, 0)),
                      pl.BlockSpec((1, d, de), lambda i, be, nu: (be[i], 0, 0)),
                      pl.BlockSpec((1, d, de), lambda i, be, nu: (be[i], 0, 0)),
                      pl.BlockSpec((1, de, d), lambda i, be, nu: (be[i], 0, 0))],
            out_specs=pl.BlockSpec((MOE_ROWS, d), lambda i, be, nu: (i, 0))),
        out_shape=jax.ShapeDtypeStruct((npad, d), F32),
        compiler_params=_cparams(("arbitrary",)),
        name="moe_experts",
    )(block_expert, n_used, x_pad, w1, w3, w2)


def _combine_kernel(dest_ref, yp_ref, rec_ref, x1_ref, mod_ref, g_ref, b_ref, o_ref, buf_ref, sem, *, tm):
    base = pl.program_id(0) * tm

    def issue(r, carry):
        for k in range(2):
            d = dest_ref[2 * (base + r) + k]
            _row_copy(yp_ref.at[pl.ds(d, 1), :], buf_ref.at[k, pl.ds(r, 1), :], sem).start()
        return carry

    lax.fori_loop(0, tm, issue, 0)

    def drain(r, carry):
        _row_copy(yp_ref.at[pl.ds(0, 1), :], buf_ref.at[0, pl.ds(0, 1), :], sem).wait()
        return carry

    lax.fori_loop(0, 2 * tm, drain, 0)

    rec = rec_ref[...]
    y = rec[:, 2:3] * buf_ref[0] + rec[:, 3:4] * buf_ref[1]
    y = (1.0 + mod_ref[0, 5:6, :]) * y
    o_ref[...] = _layer_norm(ALPHA * x1_ref[...] + y) * g_ref[...] + b_ref[...]


def _combine(dest, y_pad, rec, x1, mod, ln_g, ln_b, seq):
    nt, d = x1.shape
    tm = min(256, seq)
    return pl.pallas_call(
        functools.partial(_combine_kernel, tm=tm),
        grid_spec=pltpu.PrefetchScalarGridSpec(
            num_scalar_prefetch=1,
            grid=(nt // tm,),
            in_specs=[pl.BlockSpec(memory_space=pl.ANY),
                      pl.BlockSpec((tm, LANE), lambda i, dest: (i, 0)),
                      pl.BlockSpec((tm, d), lambda i, dest: (i, 0)),
                      pl.BlockSpec((1, 6, d), lambda i, dest: (i * tm // seq, 0, 0)),
                      pl.BlockSpec((1, d), lambda i, dest: (0, 0)),
                      pl.BlockSpec((1, d), lambda i, dest: (0, 0))],
            out_specs=pl.BlockSpec((tm, d), lambda i, dest: (i, 0)),
            scratch_shapes=[pltpu.VMEM((2, tm, d), F32), pltpu.SemaphoreType.DMA(())]),
        out_shape=jax.ShapeDtypeStruct((nt, d), F32),
        compiler_params=_cparams(("arbitrary",)),
        name="moe_combine",
    )(dest, y_pad, rec, x1, mod, ln_g, ln_b)


def _rel_bucket(dist):
    n = jnp.maximum(dist, 0)
    max_exact = REL_BUCKETS // 2
    nf = jnp.maximum(n, 1).astype(F32)
    large = max_exact + (jnp.log(nf / max_exact) / math.log(REL_MAX_DIST / max_exact)
                         * (REL_BUCKETS - max_exact)).astype(jnp.int32)
    large = jnp.minimum(large, REL_BUCKETS - 1)
    return jnp.where(n < max_exact, n, large)


def _bias_tables(rel_bias):
    bucket_onehot = (_rel_bucket(jnp.arange(LANE))[:, None] == jnp.arange(REL_BUCKETS)).astype(F32)
    tab_d = jnp.einsum('db,hb->hd', bucket_onehot, rel_bias,
                       precision=lax.Precision.HIGHEST)
    i = np.arange(QT)[:, None]
    j = np.arange(LANE)[None, :]
    far = tab_d[:, LANE - 1]
    d_c = i - CMP_STRIDE * j + (CMP_STRIDE * CMP_PAD - (CMP_BLOCK - 1))
    def by_distance(dist):
        idx = jnp.asarray(np.clip(dist, 0, LANE - 1).astype(np.int32))
        onehot = (idx[..., None] == jnp.arange(LANE, dtype=jnp.int32)).astype(F32)
        return jnp.einsum('ijd,hd->hij', onehot, tab_d, precision=lax.Precision.HIGHEST)

    bc = by_distance(d_c)
    c31 = jnp.broadcast_to(far[:, None, None], (NSA_HEADS, 1, LANE))

    def transposed(dist):
        t = (by_distance(dist) - far[:, None, None]) * LOG2E
        t = t.reshape(NSA_GROUPS, NSA_REP, LANE, QT).transpose(0, 2, 1, 3)
        return t.reshape(NSA_GROUPS, LANE, NSA_REP * QT)

    t0t = transposed(i.T - j.T)
    t1t = transposed(i.T - j.T + QT)

    rho = np.arange(WINDOW + QT)[:, None]
    tok = np.tile(np.arange(QT), NSA_REP)[None, :]
    band = (rho > tok) & (rho <= tok + WINDOW)
    rows = jnp.concatenate([jnp.zeros((NSA_GROUPS, WINDOW - QT, NSA_REP * QT), F32), t1t, t0t], axis=1)
    wb = jnp.where(band[None], rows, NEG)
    return t0t, t1t, wb, bc, c31


def _overlap_matrix(seq):
    ns = seq // CMP_STRIDE
    nslc = seq // SLC_BLOCK
    ov = np.zeros((ns + LANE, LANE), np.float32)
    cs = np.arange(ns - 1) * CMP_STRIDE
    ss = np.arange(nslc) * SLC_BLOCK
    ov[CMP_PAD:CMP_PAD + ns - 1, :nslc] = ((cs[:, None] < ss[None, :] + SLC_BLOCK)
                                           & (cs[:, None] + CMP_BLOCK > ss[None, :]))
    return jnp.asarray(ov, BF16)


def _reorder_cols(a):
    lead = a.shape[:-1]
    gate = a[..., MAIN_COLS:MAIN_COLS + GATE_COLS]
    per = GATE_COLS // NSA_GROUPS
    gate_blocks = []
    for g in range(NSA_GROUPS):
        gate_blocks.append(gate[..., g * per:(g + 1) * per])
        gate_blocks.append(jnp.zeros(lead + (LANE - per,), a.dtype))
    pad = jnp.zeros(lead + ((CB_MGH - CB_GATE - NSA_GROUPS) * LANE,), a.dtype)
    return jnp.concatenate([a[..., :MAIN_COLS]] + gate_blocks + [pad, a[..., MAIN_COLS + GATE_COLS:]], axis=-1)


def kernel(x, c, ada_w, ada_b, w_in, b_in, hg_lb_logits, hg_norm_w, cmp_pos_k, cmp_w1_k, cmp_b1_k, cmp_w2_k, cmp_pos_v, cmp_w1_v, cmp_b1_v, cmp_w2_v, rel_bias, w_br_hg, w_br_nsa, w_out, ln1_g, ln1_b, router_grp_w, router_grp_b, router_exp_w, router_exp_b, exp_w1, exp_w3, exp_w2, ln2_g, ln2_b):
    batch, seq, d = x.shape
    nt = batch * seq
    assert d == D_MODEL and seq % 1024 == 0 and seq // SLC_BLOCK <= LANE
    l = 0
    x2 = x.reshape(nt, d)

    c_pad = jnp.zeros((8, d), F32).at[:batch].set(c)
    mod = _adaln(c_pad, ada_w[l], ada_b[l][None])[:batch].reshape(batch, 6, d)

    proj = _inproj(x2, mod, _reorder_cols(w_in[l]).astype(BF16), _reorder_cols(b_in[l])[None], seq)

    lb_all = jnp.cumsum(jax.nn.softmax(hg_lb_logits.astype(F32), axis=0), axis=0)
    o_h = _hgrn(proj, lb_all[l].reshape(HG_HEADS, 1, HG_DK), hg_norm_w[l].reshape(HG_HEADS, 1, HG_DV),
                batch, seq)

    half = CMP_STRIDE * LANE
    pos = jnp.stack([cmp_pos_k[l].reshape(2, half), cmp_pos_v[l].reshape(2, half)])
    kv_cols = proj[CB_KC:CB_KC + 2 * NSA_GROUPS].reshape(2 * NSA_GROUPS, nt // CMP_STRIDE, half)
    kvc = _compress(kv_cols, pos,
                    jnp.stack([cmp_w1_k[l], cmp_w1_v[l]]).astype(BF16),
                    jnp.stack([cmp_b1_k[l], cmp_b1_v[l]])[:, None, :],
                    jnp.stack([cmp_w2_k[l], cmp_w2_v[l]]).astype(BF16), batch, seq)

    t0t, t1t, wb, bc, c31 = _bias_tables(rel_bias)
    o_c, sel_t = _cmp_sel(proj, kvc, bc, c31, _overlap_matrix(seq), batch, seq)
    o_w = _window(proj, wb, batch, seq)
    o_s = _selected(proj, sel_t, t0t, t1t, batch, seq)

    w_r = jnp.zeros((d, LANE), F32).at[:, :N_GROUPS].set(router_grp_w[l])
    w_r = w_r.at[:, N_GROUPS:N_GROUPS + N_EXPERTS].set(router_exp_w[l])
    b_r = jnp.zeros((1, LANE), F32).at[0, :N_GROUPS].set(router_grp_b[l])
    b_r = b_r.at[0, N_GROUPS:N_GROUPS + N_EXPERTS].set(router_exp_b[l])
    x1, h2, logits = _merge(o_h, o_c, o_s, o_w, proj, x2, mod,
                            w_br_hg[l].astype(BF16), w_br_nsa[l].astype(BF16), w_out[l].astype(BF16),
                            ln1_g[l][None], ln1_b[l][None], w_r, b_r, seq)

    rec, cnt = _route(logits)
    counts = cnt[0, N_GROUPS:N_GROUPS + N_EXPERTS].astype(jnp.int32)
    padded = (counts + MOE_ROWS - 1) // MOE_ROWS * MOE_ROWS
    pend = jnp.cumsum(padded)
    pstart = pend - padded
    n_assign = 2 * nt
    nb = n_assign // MOE_ROWS + N_EXPERTS
    expert = rec[:, 0:2].astype(jnp.int32)
    dest = (pstart[expert] + rec[:, 4:6].astype(jnp.int32)).reshape(-1)
    block_start = jnp.arange(nb, dtype=jnp.int32) * MOE_ROWS
    block_expert = jnp.minimum(jnp.sum(pend[None, :] <= block_start[:, None], axis=1),
                               N_EXPERTS - 1).astype(jnp.int32)
    n_used = (pend[-1:] // MOE_ROWS).astype(jnp.int32)

    x_pad = _dispatch(dest, h2, jnp.zeros((nb * MOE_ROWS, d), F32))
    y_pad = _experts(block_expert, n_used, x_pad,
                     exp_w1[l].astype(BF16), exp_w3[l].astype(BF16), exp_w2[l].astype(BF16))
    out = _combine(dest, y_pad, rec, x1, mod, ln2_g[l][None], ln2_b[l][None], seq)
    return out.reshape(batch, seq, d)
```

```python
import functools
import math

import numpy as np
import jax
import jax.numpy as jnp
from jax import lax
from jax.experimental import pallas as pl
from jax.experimental.pallas import tpu as pltpu

F32 = jnp.float32
BF16 = jnp.bfloat16

D_MODEL = 1024
HG_HEADS = 8
HG_DK = 128
HG_DV = 128
HG_CHUNK = 32
NSA_HEADS = 8
NSA_GROUPS = 2
NSA_REP = NSA_HEADS // NSA_GROUPS
NSA_DK = 128
CMP_BLOCK = 32
CMP_STRIDE = 16
SLC_BLOCK = 64
SLC_TOPK = 16
WINDOW = 512
FORCE_SCORE = 1e4
REL_BUCKETS = 32
REL_MAX_DIST = 128
N_GROUPS = 4
EXP_PER_GROUP = 8
N_EXPERTS = N_GROUPS * EXP_PER_GROUP
D_EXPERT = D_MODEL // 2
DEPTH = 1
ALPHA = (2 * DEPTH) ** 0.25

LANE = 128
QT = 128
NEG = -1e30
SCALE = NSA_DK ** -0.5
LOG2E = math.log2(math.e)
SCALE_LOG2 = SCALE * LOG2E
KEY_BLK = 512
WIN_SUB = 2
CMP_SUB = 4
CMP_PAD = 120
VMEM_LIMIT = 56 * 1024 * 1024

CB_HQ, CB_HF, CB_HI, CB_HG = 0, 8, 16, 24
CB_NQ = 32
CB_KC, CB_VC, CB_KS, CB_VS, CB_KW, CB_VW = 40, 42, 44, 46, 48, 50
CB_GATE = 52
CB_MGH, CB_MGN = 56, 64
NCB = 72
MAIN_COLS = 52 * LANE
GATE_COLS = 3 * NSA_HEADS

MOE_ROWS = 512


def _cparams(sem):
    return pltpu.CompilerParams(dimension_semantics=sem, vmem_limit_bytes=VMEM_LIMIT)


def _dot(a, b):
    return jnp.dot(a, b, preferred_element_type=F32)


def _dot_nt(a, b):
    return lax.dot_general(a, b, (((1,), (1,)), ((), ())), preferred_element_type=F32)


def _dot_tn(a, b):
    return lax.dot_general(a, b, (((0,), (0,)), ((), ())), preferred_element_type=F32)


def _split3(x):
    hi = x.astype(BF16)
    r = x - hi.astype(F32)
    mid = r.astype(BF16)
    lo = (r - mid.astype(F32)).astype(BF16)
    return hi, mid, lo


def _dot01(m01, x):
    hi, mid, lo = _split3(x)
    return _dot(m01, hi) + _dot(m01, mid) + _dot(m01, lo)


def _dot01_r(x, m01):
    hi, mid, lo = _split3(x)
    return _dot(hi, m01) + _dot(mid, m01) + _dot(lo, m01)


def _layer_norm(x, eps=1e-5):
    mu = jnp.mean(x, axis=-1, keepdims=True)
    xc = x - mu
    var = jnp.mean(xc * xc, axis=-1, keepdims=True)
    return xc * lax.rsqrt(var + eps)


def _adaln_kernel(c_ref, w_ref, b_ref, o_ref):
    c = c_ref[...]
    ca = c * jax.nn.sigmoid(c)
    o_ref[...] = jnp.dot(ca, w_ref[...], precision=lax.Precision.HIGHEST,
                         preferred_element_type=F32) + b_ref[...]


def _adaln(c_pad, w, b):
    rows, d = c_pad.shape
    n = w.shape[1]
    return pl.pallas_call(
        _adaln_kernel,
        grid=(n // d,),
        in_specs=[pl.BlockSpec((rows, d), lambda j: (0, 0)),
                  pl.BlockSpec((d, d), lambda j: (0, j)),
                  pl.BlockSpec((1, d), lambda j: (0, j))],
        out_specs=pl.BlockSpec((rows, d), lambda j: (0, j)),
        out_shape=jax.ShapeDtypeStruct((rows, n), F32),
        compiler_params=_cparams(("arbitrary",)),
        name="adaln",
    )(c_pad, w, b)


def _inproj_kernel(x_ref, mod_ref, w_ref, b_ref, o_ref, hn_ref, *, ncb_tile):
    @pl.when(pl.program_id(1) == 0)
    def _():
        hn = _layer_norm(x_ref[...])
        sh = mod_ref[0, 0:1, :]
        sc = mod_ref[0, 1:2, :]
        hn_ref[...] = (hn * (1.0 + sc) + sh).astype(BF16)

    res = _dot(hn_ref[...], w_ref[...]) + b_ref[...]
    for c in range(ncb_tile):
        o_ref[c] = res[:, c * LANE:(c + 1) * LANE]


def _inproj(x2, mod, w, b, seq):
    nt, d = x2.shape
    tm = min(1024, seq)
    ncb_tile = 9
    tn = ncb_tile * LANE
    return pl.pallas_call(
        functools.partial(_inproj_kernel, ncb_tile=ncb_tile),
        grid=(nt // tm, NCB // ncb_tile),
        in_specs=[pl.BlockSpec((tm, d), lambda i, j: (i, 0)),
                  pl.BlockSpec((1, 6, d), lambda i, j: (i * tm // seq, 0, 0)),
                  pl.BlockSpec((d, tn), lambda i, j: (0, j)),
                  pl.BlockSpec((1, tn), lambda i, j: (0, j))],
        out_specs=pl.BlockSpec((ncb_tile, tm, LANE), lambda i, j: (j, i, 0)),
        out_shape=jax.ShapeDtypeStruct((NCB, nt, LANE), F32),
        scratch_shapes=[pltpu.VMEM((tm, d), BF16)],
        compiler_params=_cparams(("arbitrary", "arbitrary")),
        name="inproj",
    )(x2, mod, w, b)


def _hgrn_kernel(q_ref, f_ref, v_ref, g_ref, lb_ref, nw_ref, o_ref, st_ref, *, rows):
    @pl.when(pl.program_id(1) == 0)
    def _():
        st_ref[...] = jnp.zeros_like(st_ref)

    ri = lax.broadcasted_iota(jnp.int32, (LANE, LANE), 0)
    ci = lax.broadcasted_iota(jnp.int32, (LANE, LANE), 1)
    same = (ri // HG_CHUNK) == (ci // HG_CHUNK)
    tril = same & (ci <= ri)
    cum_m = jnp.where(tril, 1.0, 0.0).astype(BF16)
    tot_m = jnp.where(same, 1.0, 0.0).astype(BF16)
    per = LANE // HG_CHUNK

    heads = range(HG_HEADS)
    hs = [slice(h * LANE, (h + 1) * LANE) for h in heads]

    def wide(ref, r0):
        return jnp.concatenate([ref[h, pl.ds(r0, LANE), :] for h in heads], axis=1)

    def body(i, carry):
        r0 = pl.multiple_of(i * LANE, LANE)
        lb = lb_ref[...]
        f = lb + (1.0 - lb) * jax.nn.sigmoid(wide(f_ref, r0))
        lf = jnp.log(f)
        k = 1.0 - f
        hi, mid, lo = _split3(lf)
        b = _dot(cum_m, hi) + _dot(cum_m, mid) + _dot(cum_m, lo)
        bl = _dot(tot_m, hi) + _dot(tot_m, mid) + _dot(tot_m, lo)
        q_in = (wide(q_ref, r0) * jnp.exp(b)).astype(BF16)
        k_in = (k * jnp.exp(-b)).astype(BF16)
        k_end = (k * jnp.exp(bl - b)).astype(BF16)
        vb = wide(v_ref, r0).astype(BF16)
        dec = [jnp.exp(bl[c * HG_CHUNK:c * HG_CHUNK + 1, :]) for c in range(per)]
        chunk = [slice(c * HG_CHUNK, (c + 1) * HG_CHUNK) for c in range(per)]

        att = [jnp.where(tril, _dot_nt(q_in[:, hs[h]], k_in[:, hs[h]]), 0.0).astype(BF16) for h in heads]
        upd = [[_dot_tn(vb[chunk[c], hs[h]], k_end[chunk[c], hs[h]]) for c in range(per)] for h in heads]
        intra = [_dot(att[h], vb[:, hs[h]]) for h in heads]

        st = [st_ref[h] for h in heads]
        inter = [[] for _ in heads]
        for c in range(per):
            for h in heads:
                inter[h].append(_dot_nt(q_in[chunk[c], hs[h]], st[h].astype(BF16)))
                st[h] = dec[c][:, hs[h]] * st[h] + upd[h][c]
        for h in heads:
            st_ref[h] = st[h]

        o = [intra[h] + jnp.concatenate(inter[h], axis=0) for h in heads]
        scale = [lax.rsqrt(jnp.mean(o[h] * o[h], axis=-1, keepdims=True) + 1e-6) for h in heads]
        on = jnp.concatenate([o[h] * scale[h] for h in heads], axis=1)
        on = on * nw_ref[...] * jax.nn.sigmoid(wide(g_ref, r0))
        o_ref[pl.ds(r0, LANE), :] = on.astype(BF16)
        return carry

    lax.fori_loop(0, rows // LANE, body, 0)


def _hgrn(proj, lb, nw, batch, seq):
    nt = proj.shape[1]
    tb = min(512, seq)
    nblk = seq // tb

    def slab(cb0):
        return pl.BlockSpec((HG_HEADS, tb, LANE), lambda b, t: (cb0 // HG_HEADS, b * nblk + t, 0))

    vec = pl.BlockSpec((1, HG_HEADS * LANE), lambda b, t: (0, 0))
    return pl.pallas_call(
        functools.partial(_hgrn_kernel, rows=tb),
        grid=(batch, nblk),
        in_specs=[slab(CB_HQ), slab(CB_HF), slab(CB_HI), slab(CB_HG), vec, vec],
        out_specs=pl.BlockSpec((tb, HG_HEADS * HG_DV), lambda b, t: (b * nblk + t, 0)),
        out_shape=jax.ShapeDtypeStruct((nt, HG_HEADS * HG_DV), BF16),
        scratch_shapes=[pltpu.VMEM((HG_HEADS, HG_DV, HG_DK), F32)],
        compiler_params=_cparams(("arbitrary", "arbitrary")),
        name="hgrn2",
    )(proj, proj, proj, proj, lb, nw)


def _compress_kernel(sub_ref, pos_ref, w1_ref, b1_ref, w2_ref, o_ref, *, ns):
    half = CMP_STRIDE * LANE
    sub = sub_ref[0]
    xa = (sub + pos_ref[0, 0:1, :]).astype(BF16)
    xb = (sub + pos_ref[0, 1:2, :]).astype(BF16)
    p0 = _dot(xa, w1_ref[0, 0:half, :])
    p1 = _dot(xb, w1_ref[0, half:2 * half, :])
    h = p0 + pltpu.roll(p1, ns - 1, axis=0) + b1_ref[0]
    a = h * jax.nn.sigmoid(h)
    out = _dot(a.astype(BF16), w2_ref[0])
    row = lax.broadcasted_iota(jnp.int32, out.shape, 0)
    out = jnp.where(row < ns - 1, out, 0.0)
    o_ref[0, 0, 0, 0:CMP_PAD, :] = jnp.zeros((CMP_PAD, LANE), F32)
    o_ref[0, 0, 0, CMP_PAD:CMP_PAD + ns, :] = out
    o_ref[0, 0, 0, CMP_PAD + ns:, :] = jnp.zeros((LANE - CMP_PAD, LANE), F32)


def _compress(proj16, pos, w1, b1, w2, batch, seq):
    ns = seq // CMP_STRIDE
    np_rows = ns + LANE
    width = CMP_STRIDE * LANE
    return pl.pallas_call(
        functools.partial(_compress_kernel, ns=ns),
        grid=(2, batch, NSA_GROUPS),
        in_specs=[pl.BlockSpec((1, ns, width), lambda s, b, g: (NSA_GROUPS * s + g, b, 0)),
                  pl.BlockSpec((1, 2, width), lambda s, b, g: (s, 0, 0)),
                  pl.BlockSpec((1, 2 * width, LANE), lambda s, b, g: (s, 0, 0)),
                  pl.BlockSpec((1, 1, LANE), lambda s, b, g: (s, 0, 0)),
                  pl.BlockSpec((1, LANE, LANE), lambda s, b, g: (s, 0, 0))],
        out_specs=pl.BlockSpec((1, 1, 1, np_rows, LANE), lambda s, b, g: (s, b, g, 0, 0)),
        out_shape=jax.ShapeDtypeStruct((2, batch, NSA_GROUPS, np_rows, LANE), F32),
        compiler_params=_cparams(("arbitrary", "arbitrary", "arbitrary")),
        name="nsa_compress",
    )(proj16, pos, w1, b1, w2)


def _cmp_sel_kernel(q0, q1, q2, q3, kc_ref, vc_ref, gate_ref, tt_ref, ov_ref, oc_ref, sel_ref,
                    kb_ref, vt_ref, *, np_rows):
    step = pl.program_id(2)
    cols = NSA_REP * QT
    ns = np_rows - LANE

    @pl.when(step == 0)
    def _():
        kb_ref[...] = kc_ref[0, 0, 0].astype(BF16)
        for c in range(np_rows // LANE):
            vt_ref[:, c * LANE:(c + 1) * LANE] = vc_ref[0, 0, 0, c * LANE:(c + 1) * LANE, :].T.astype(BF16)

    subs = range(CMP_SUB)
    tis = [step * CMP_SUB + u for u in subs]
    real = lax.broadcasted_iota(jnp.int32, (np_rows, cols), 0) >= CMP_PAD
    kb = kb_ref[...]
    qts = [_q_transposed((q0, q1, q2, q3), u) for u in subs]
    bias = [tt_ref[0, pl.ds(pl.multiple_of(ns - tis[u] * (QT // CMP_STRIDE), 8), np_rows), :] for u in subs]
    ss = [jnp.where(real, _dot(kb, qts[u]) * SCALE_LOG2 + bias[u], NEG) for u in subs]
    ms = [jnp.max(ss[u], axis=0, keepdims=True) for u in subs]
    ps = [jnp.exp2(ss[u] - ms[u]) for u in subs]
    ls = [jnp.sum(ps[u], axis=0, keepdims=True) for u in subs]
    invs = [jnp.where(ms[u] > 0.5 * NEG, 1.0 / ls[u], 0.0) for u in subs]
    pn = [ps[u] * invs[u] for u in subs]
    vt = vt_ref[...]
    os_ = [_dot(vt, pn[u].astype(BF16)) for u in subs]
    for u in subs:
        rows = slice(u * QT, (u + 1) * QT)
        gt = jax.nn.sigmoid(gate_ref[0, rows, :])
        for r in range(NSA_REP):
            oc_ref[rows, r * LANE:(r + 1) * LANE] = gt[:, 3 * r:3 * r + 1] * os_[u][:, r * QT:(r + 1) * QT].T

    def group_sum(p):
        tot = p[:, 0:QT]
        for r in range(1, NSA_REP):
            tot = tot + p[:, r * QT:(r + 1) * QT]
        return tot

    imp = jnp.concatenate([_dot01(ov_ref[...], group_sum(pn[u])) for u in subs], axis=1)

    width = CMP_SUB * QT
    jj = lax.broadcasted_iota(jnp.int32, (LANE, width), 0)
    tok = step * width + lax.broadcasted_iota(jnp.int32, (LANE, width), 1)
    cur = tok // SLC_BLOCK
    forced = (jj == 0) | (jj == cur) | (jj == cur - 1)
    score = jnp.where(jj <= cur, jnp.where(forced, FORCE_SCORE, imp), -1.0)
    selb = jnp.full((LANE, width), NEG, F32)
    jf = jj.astype(F32)
    for _ in range(SLC_TOPK):
        mval = jnp.max(score, axis=0, keepdims=True)
        first = jnp.min(jnp.where(score == mval, jf, float(LANE)), axis=0, keepdims=True)
        pick = jf == first
        selb = jnp.where(pick, 0.0, selb)
        score = jnp.where(pick, -jnp.inf, score)
    sel_ref[0, 0] = selb


def _q_specs(nsteps, rows=QT):
    return [pl.BlockSpec((1, rows, LANE),
                         functools.partial(lambda b, g, t, r: (CB_NQ + NSA_REP * g + r, b * nsteps + t, 0), r=r))
            for r in range(NSA_REP)]


def _cmp_sel(proj, kvc, tt, ov_t, batch, seq):
    nt = proj.shape[1]
    rows = CMP_SUB * QT
    nsteps = seq // rows
    cols = NSA_REP * QT
    np_rows = kvc.shape[3]
    kv_spec = lambda s: pl.BlockSpec((1, 1, 1, np_rows, LANE), lambda b, g, t: (s, b, g, 0, 0))
    return pl.pallas_call(
        functools.partial(_cmp_sel_kernel, np_rows=np_rows),
        grid=(batch, NSA_GROUPS, nsteps),
        in_specs=_q_specs(nsteps, rows) + [
            kv_spec(0), kv_spec(1),
            pl.BlockSpec((1, rows, LANE), lambda b, g, t: (CB_GATE + g, b * nsteps + t, 0)),
            pl.BlockSpec((1, tt.shape[1], cols), lambda b, g, t: (g, 0, 0)),
            pl.BlockSpec((LANE, np_rows), lambda b, g, t: (0, 0))],
        out_specs=[pl.BlockSpec((rows, cols), lambda b, g, t: (b * nsteps + t, g)),
                   pl.BlockSpec((1, 1, LANE, rows), lambda b, g, t: (b, g, 0, t))],
        out_shape=[jax.ShapeDtypeStruct((nt, NSA_HEADS * LANE), F32),
                   jax.ShapeDtypeStruct((batch, NSA_GROUPS, LANE, seq), F32)],
        scratch_shapes=[pltpu.VMEM((np_rows, LANE), BF16), pltpu.VMEM((LANE, np_rows), BF16)],
        compiler_params=_cparams(("arbitrary", "arbitrary", "arbitrary")),
        name="nsa_cmp_select",
    )(proj, proj, proj, proj, kvc, kvc, proj, tt, ov_t)


def _q_transposed(q_refs, sub=0):
    return jnp.concatenate([r[0, sub * QT:(sub + 1) * QT, :].T for r in q_refs], axis=1).astype(BF16)


def _softmax_step(s, vt, state):
    m, l, acc = state
    m_new = jnp.maximum(m, jnp.max(s, axis=0, keepdims=True))
    alpha = jnp.exp2(m - m_new)
    p = jnp.exp2(s - m_new)
    l = alpha * l + jnp.sum(p, axis=0, keepdims=True)
    acc = alpha * acc + _dot(vt, p.astype(BF16))
    return m_new, l, acc


def _finish_t(state, gate_ref, branch, o_ref, sub=0):
    _, l, acc = state
    o = acc / jnp.where(l == 0.0, 1.0, l)
    rows = slice(sub * QT, (sub + 1) * QT)
    gt = jax.nn.sigmoid(gate_ref[0, rows, :])
    for r in range(NSA_REP):
        col = 3 * r + branch
        o_ref[rows, r * LANE:(r + 1) * LANE] = gt[:, col:col + 1] * o[:, r * QT:(r + 1) * QT].T


def _init_state(cols):
    return (jnp.full((1, cols), NEG, F32), jnp.zeros((1, cols), F32), jnp.zeros((LANE, cols), F32))


def _win_kernel(q0, q1, q2, q3, k_ref, v_ref, gate_ref, wb_ref, o_ref, kb_ref, vt_ref, *, seq):
    step = pl.program_id(2)
    cols = NSA_REP * QT
    wt = WINDOW // QT

    @pl.when(step == 0)
    def _():
        kb_ref[0:WINDOW, :] = jnp.zeros((WINDOW, LANE), BF16)
        for i in range(wt):
            vt_ref[i] = jnp.zeros((LANE, QT), BF16)

        def fill(i, carry):
            r0 = pl.multiple_of(i * QT, QT)
            kb_ref[pl.ds(WINDOW + r0, QT), :] = k_ref[0, pl.ds(r0, QT), :].astype(BF16)
            vt_ref[wt + i] = v_ref[0, pl.ds(r0, QT), :].T.astype(BF16)
            return carry

        lax.fori_loop(0, seq // QT, fill, 0)

    rho = lax.broadcasted_iota(jnp.int32, (WINDOW + QT, cols), 0)
    subs = range(WIN_SUB)
    tis = [step * WIN_SUB + sub for sub in subs]
    qts = [_q_transposed((q0, q1, q2, q3), sub) for sub in subs]
    ss = [_dot(kb_ref[pl.ds(pl.multiple_of(tis[u] * QT, QT), WINDOW + QT), :], qts[u]) * SCALE_LOG2 + wb_ref[0]
          for u in subs]
    ss = [jnp.where(rho >= (wt - tis[u]) * QT, ss[u], NEG) for u in subs]
    ms = [jnp.max(ss[u], axis=0, keepdims=True) for u in subs]
    ps = [jnp.exp2(ss[u] - ms[u]) for u in subs]
    ls = [jnp.sum(ps[u], axis=0, keepdims=True) for u in subs]
    pbs = [ps[u].astype(BF16) for u in subs]
    accs = [_dot(vt_ref[tis[u]], pbs[u][0:QT]) for u in subs]
    for d in range(1, wt + 1):
        accs = [accs[u] + _dot(vt_ref[tis[u] + d], pbs[u][d * QT:(d + 1) * QT]) for u in subs]
    for u in subs:
        _finish_t((ms[u], ls[u], accs[u]), gate_ref, 2, o_ref, u)


def _window(proj, wb, batch, seq):
    nt = proj.shape[1]
    rows = WIN_SUB * QT
    nqt = seq // rows
    cols = NSA_REP * QT
    slab = lambda cb0: pl.BlockSpec((1, seq, LANE), lambda b, g, t: (cb0 + g, b, 0))
    return pl.pallas_call(
        functools.partial(_win_kernel, seq=seq),
        grid=(batch, NSA_GROUPS, nqt),
        in_specs=_q_specs(nqt, rows) + [
            slab(CB_KW), slab(CB_VW),
            pl.BlockSpec((1, rows, LANE), lambda b, g, t: (CB_GATE + g, b * nqt + t, 0)),
            pl.BlockSpec((1, WINDOW + QT, cols), lambda b, g, t: (g, 0, 0))],
        out_specs=pl.BlockSpec((rows, cols), lambda b, g, t: (b * nqt + t, g)),
        out_shape=jax.ShapeDtypeStruct((nt, NSA_HEADS * LANE), F32),
        scratch_shapes=[pltpu.VMEM((seq + WINDOW, LANE), BF16),
                        pltpu.VMEM((seq // QT + WINDOW // QT, LANE, QT), BF16)],
        compiler_params=_cparams(("arbitrary", "arbitrary", "arbitrary")),
        name="nsa_window",
    )(proj, proj, proj, proj, proj, proj, proj, wb)


def _sel_kernel(q0, q1, q2, q3, k_ref, v_ref, sel_ref, gate_ref, t0_ref, t1_ref, o_ref,
                kb_ref, vt_ref, bias_ref, m_ref, l_ref, acc_ref, *, seq):
    ti = pl.program_id(2)
    cols = NSA_REP * QT
    tpb = KEY_BLK // QT
    spb = KEY_BLK // SLC_BLOCK
    zero_tile = jnp.zeros((QT, cols), F32)

    def near_rows(t):
        r = t % tpb
        return (pl.multiple_of(KEY_BLK + r * QT, QT), pl.multiple_of((r + tpb - 1) * QT, QT))

    @pl.when(ti == 0)
    def _():
        def fill(i, carry):
            r0 = pl.multiple_of(i * QT, QT)
            kb_ref[pl.ds(r0, QT), :] = k_ref[0, pl.ds(r0, QT), :].astype(BF16)
            vt = v_ref[0, pl.ds(r0, QT), :].T.astype(BF16)
            for c in range(tpb):
                @pl.when(i % tpb == c)
                def _():
                    vt_ref[i // tpb, :, c * QT:(c + 1) * QT] = vt
            return carry

        lax.fori_loop(0, seq // QT, fill, 0)
        bias_ref[...] = jnp.zeros_like(bias_ref)

    @pl.when(ti > 0)
    def _():
        p0, p1 = near_rows(ti - 1)
        bias_ref[pl.ds(p0, QT), :] = zero_tile
        bias_ref[pl.ds(p1, QT), :] = zero_tile

    p0, p1 = near_rows(ti)
    bias_ref[pl.ds(p1, QT), :] = t1_ref[0]
    bias_ref[pl.ds(p0, QT), :] = t0_ref[0]

    qt = _q_transposed((q0, q1, q2, q3))
    last = ti // tpb

    def block(kb):
        r0 = pl.multiple_of(kb * KEY_BLK, KEY_BLK)
        return kb_ref[pl.ds(r0, KEY_BLK), :], vt_ref[kb]

    def sel_bias(kb):
        rows = sel_ref[0, 0, pl.ds(pl.multiple_of(kb * spb, spb), spb), :]
        one = jnp.concatenate(
            [jnp.broadcast_to(rows[r:r + 1], (SLC_BLOCK, QT)) for r in range(spb)], axis=0)
        return jnp.concatenate([one] * NSA_REP, axis=1)

    diff = (lax.broadcasted_iota(jnp.int32, (KEY_BLK, cols), 0)
            - lax.broadcasted_iota(jnp.int32, (KEY_BLK, cols), 1) % QT)

    def near(kb_true, slot, state):
        kb = jnp.maximum(kb_true, 0)
        thr = jnp.where(kb_true >= 0, ti * QT - kb_true * KEY_BLK, -KEY_BLK)
        k, vt = block(kb)
        s = _dot(k, qt) * SCALE_LOG2 + bias_ref[slot * KEY_BLK:(slot + 1) * KEY_BLK, :] + sel_bias(kb)
        s = jnp.where(diff <= thr, s, NEG)
        return _softmax_step(s, vt, state)

    state = near(last, 1, _init_state(cols))
    state = near(last - 1, 0, state)
    m_ref[...], l_ref[...], acc_ref[...] = state

    def body(kb, carry):
        k, vt = block(kb)
        s = _dot(k, qt) * SCALE_LOG2 + sel_bias(kb)
        st = _softmax_step(s, vt, (m_ref[...], l_ref[...], acc_ref[...]))
        m_ref[...], l_ref[...], acc_ref[...] = st
        return carry

    lax.fori_loop(0, jnp.maximum(last - 1, 0), body, 0)
    _finish_t((m_ref[...], l_ref[...], acc_ref[...]), gate_ref, 1, o_ref)


def _selected(proj, sel_t, t0t, t1t, batch, seq):
    nt = proj.shape[1]
    nqt = seq // QT
    cols = NSA_REP * QT
    slab = lambda cb0: pl.BlockSpec((1, seq, LANE), lambda b, g, t: (cb0 + g, b, 0))
    tab = pl.BlockSpec((1, LANE, cols), lambda b, g, t: (g, 0, 0))
    return pl.pallas_call(
        functools.partial(_sel_kernel, seq=seq),
        grid=(batch, NSA_GROUPS, nqt),
        in_specs=_q_specs(nqt) + [
            slab(CB_KS), slab(CB_VS),
            pl.BlockSpec((1, 1, LANE, QT), lambda b, g, t: (b, g, 0, t)),
            pl.BlockSpec((1, QT, LANE), lambda b, g, t: (CB_GATE + g, b * nqt + t, 0)),
            tab, tab],
        out_specs=pl.BlockSpec((QT, cols), lambda b, g, t: (b * nqt + t, g)),
        out_shape=jax.ShapeDtypeStruct((nt, NSA_HEADS * LANE), F32),
        scratch_shapes=[pltpu.VMEM((seq, LANE), BF16),
                        pltpu.VMEM((seq // KEY_BLK, LANE, KEY_BLK), BF16),
                        pltpu.VMEM((2 * KEY_BLK, cols), F32),
                        pltpu.VMEM((1, cols), F32), pltpu.VMEM((1, cols), F32),
                        pltpu.VMEM((LANE, cols), F32)],
        compiler_params=_cparams(("arbitrary", "arbitrary", "arbitrary")),
        name="nsa_selected",
    )(proj, proj, proj, proj, proj, proj, sel_t, proj, t0t, t1t)


def _merge_kernel(oh_ref, oc_ref, os_ref, ow_ref, mgh_ref, mgn_ref, x_ref, mod_ref,
                  wh_ref, wn_ref, wo_ref, g_ref, b_ref, wr_ref, br_ref,
                  x1_ref, h2_ref, lg_ref):
    nblk = D_MODEL // LANE
    a_h = _dot(oh_ref[...], wh_ref[...])
    o_n = (oc_ref[...] + os_ref[...] + ow_ref[...]).astype(BF16)
    a_n = _dot(o_n, wn_ref[...])
    gh = jnp.concatenate([mgh_ref[c] for c in range(nblk)], axis=-1)
    gn = jnp.concatenate([mgn_ref[c] for c in range(nblk)], axis=-1)
    merged = jax.nn.sigmoid(gh) * a_h + jax.nn.sigmoid(gn) * a_n
    y = (1.0 + mod_ref[0, 2:3, :]) * _dot(merged.astype(BF16), wo_ref[...])
    x1 = _layer_norm(ALPHA * x_ref[...] + y) * g_ref[...] + b_ref[...]
    x1_ref[...] = x1
    h2 = _layer_norm(x1) * (1.0 + mod_ref[0, 4:5, :]) + mod_ref[0, 3:4, :]
    h2_ref[...] = h2
    lg_ref[...] = jnp.dot(h2, wr_ref[...], precision=lax.Precision.HIGHEST,
                          preferred_element_type=F32) + br_ref[...]


def _merge(o_h, o_c, o_s, o_w, proj, x2, mod, w_h, w_n, w_o, ln_g, ln_b, w_r, b_r, seq):
    nt, d = x2.shape
    tm = min(256, seq)
    nblk = d // LANE
    row = lambda w: pl.BlockSpec((tm, w), lambda i: (i, 0))
    full = lambda a: pl.BlockSpec(a.shape, lambda i: (0,) * a.ndim)
    return pl.pallas_call(
        _merge_kernel,
        grid=(nt // tm,),
        in_specs=[row(d), row(d), row(d), row(d),
                  pl.BlockSpec((nblk, tm, LANE), lambda i: (CB_MGH // nblk, i, 0)),
                  pl.BlockSpec((nblk, tm, LANE), lambda i: (CB_MGN // nblk, i, 0)),
                  row(d),
                  pl.BlockSpec((1, 6, d), lambda i: (i * tm // seq, 0, 0)),
                  full(w_h), full(w_n), full(w_o), full(ln_g), full(ln_b), full(w_r), full(b_r)],
        out_specs=[row(d), row(d), row(LANE)],
        out_shape=[jax.ShapeDtypeStruct((nt, d), F32),
                   jax.ShapeDtypeStruct((nt, d), F32),
                   jax.ShapeDtypeStruct((nt, LANE), F32)],
        compiler_params=_cparams(("arbitrary",)),
        name="merge_outproj",
    )(o_h, o_c, o_s, o_w, proj, proj, x2, mod, w_h, w_n, w_o, ln_g, ln_b, w_r, b_r)


def _route_kernel(lg_ref, rec_ref, cnt_ref, carry_ref, *, tm):
    @pl.when(pl.program_id(0) == 0)
    def _():
        carry_ref[...] = jnp.zeros_like(carry_ref)

    lg = lg_ref[...]
    lane = lax.broadcasted_iota(jnp.int32, (tm, LANE), 1).astype(F32)
    far = float(LANE)
    gmask = lane < N_GROUPS
    gl = jnp.where(gmask, lg, -jnp.inf)
    gmax = jnp.max(gl, axis=-1, keepdims=True)
    gsum = jnp.sum(jnp.where(gmask, jnp.exp(gl - gmax), 0.0), axis=-1, keepdims=True)
    grp_p = 1.0 / gsum
    gidx = jnp.min(jnp.where(gl == gmax, lane, far), axis=-1, keepdims=True)
    lo = N_GROUPS + EXP_PER_GROUP * gidx
    emask = (lane >= lo) & (lane < lo + EXP_PER_GROUP)
    el = jnp.where(emask, lg, -jnp.inf)
    m1 = jnp.max(el, axis=-1, keepdims=True)
    i1 = jnp.min(jnp.where(el == m1, lane, far), axis=-1, keepdims=True)
    el2 = jnp.where(lane == i1, -jnp.inf, el)
    m2 = jnp.max(el2, axis=-1, keepdims=True)
    i2 = jnp.min(jnp.where(emask & (lane != i1) & (el2 == m2), lane, far), axis=-1, keepdims=True)
    e = jnp.exp(m2 - m1)
    w0 = grp_p / (1.0 + e)
    w1 = grp_p * e / (1.0 + e)

    oh0 = lane == i1
    oh1 = lane == i2
    f0 = jnp.where(oh0, 1.0, 0.0)
    f1 = jnp.where(oh1, 1.0, 0.0)
    ri = lax.broadcasted_iota(jnp.int32, (tm, tm), 0)
    ci = lax.broadcasted_iota(jnp.int32, (tm, tm), 1)
    before = jnp.where(ci < ri, 1.0, 0.0).astype(BF16)
    cum0 = _dot(before, f0.astype(BF16))
    cum1 = _dot(before, f1.astype(BF16))
    tot0 = jnp.sum(f0, axis=0, keepdims=True)
    tot1 = jnp.sum(f1, axis=0, keepdims=True)
    carry = carry_ref[...]
    rank0 = jnp.sum(jnp.where(oh0, carry + cum0, 0.0), axis=-1, keepdims=True)
    rank1 = jnp.sum(jnp.where(oh1, carry + tot0 + cum1, 0.0), axis=-1, keepdims=True)
    carry = carry + tot0 + tot1
    carry_ref[...] = carry
    cnt_ref[...] = carry

    rec = jnp.where(lane == 0, i1 - N_GROUPS, 0.0)
    rec = jnp.where(lane == 1, i2 - N_GROUPS, rec)
    rec = jnp.where(lane == 2, w0, rec)
    rec = jnp.where(lane == 3, w1, rec)
    rec = jnp.where(lane == 4, rank0, rec)
    rec = jnp.where(lane == 5, rank1, rec)
    rec_ref[...] = rec


def _route(logits):
    nt = logits.shape[0]
    tm = min(512, nt)
    return pl.pallas_call(
        functools.partial(_route_kernel, tm=tm),
        grid=(nt // tm,),
        in_specs=[pl.BlockSpec((tm, LANE), lambda i: (i, 0))],
        out_specs=[pl.BlockSpec((tm, LANE), lambda i: (i, 0)),
                   pl.BlockSpec((1, LANE), lambda i: (0, 0))],
        out_shape=[jax.ShapeDtypeStruct((nt, LANE), F32),
                   jax.ShapeDtypeStruct((1, LANE), F32)],
        scratch_shapes=[pltpu.VMEM((1, LANE), F32)],
        compiler_params=_cparams(("arbitrary",)),
        name="moe_route",
    )(logits)


def _row_copy(src, dst, sem):
    return pltpu.make_async_copy(src, dst, sem)


def _dispatch_kernel(dest_ref, h_ref, zero_ref, xp_ref, sem, *, tm):
    del zero_ref
    base = pl.program_id(0) * tm

    def issue(r, carry):
        for k in range(2):
            d = dest_ref[2 * (base + r) + k]
            _row_copy(h_ref.at[pl.ds(r, 1), :], xp_ref.at[pl.ds(d, 1), :], sem).start()
        return carry

    lax.fori_loop(0, tm, issue, 0)

    def drain(r, carry):
        _row_copy(h_ref.at[pl.ds(0, 1), :], xp_ref.at[pl.ds(0, 1), :], sem).wait()
        return carry

    lax.fori_loop(0, 2 * tm, drain, 0)


def _dispatch(dest, h2, x_pad0):
    nt, d = h2.shape
    tm = min(256, nt)
    return pl.pallas_call(
        functools.partial(_dispatch_kernel, tm=tm),
        grid_spec=pltpu.PrefetchScalarGridSpec(
            num_scalar_prefetch=1,
            grid=(nt // tm,),
            in_specs=[pl.BlockSpec((tm, d), lambda i, dest: (i, 0)),
                      pl.BlockSpec(memory_space=pl.ANY)],
            out_specs=pl.BlockSpec(memory_space=pl.ANY),
            scratch_shapes=[pltpu.SemaphoreType.DMA(())]),
        out_shape=jax.ShapeDtypeStruct(x_pad0.shape, x_pad0.dtype),
        input_output_aliases={2: 0},
        compiler_params=_cparams(("arbitrary",)),
        name="moe_dispatch",
    )(dest, h2, x_pad0)


def _expert_kernel(be_ref, nu_ref, x_ref, w1_ref, w3_ref, w2_ref, y_ref):
    del be_ref
    i = pl.program_id(0)

    @pl.when(i < nu_ref[0])
    def _():
        xb = x_ref[...].astype(BF16)
        a = _dot(xb, w1_ref[0])
        b = _dot(xb, w3_ref[0])
        hmid = (a * jax.nn.sigmoid(a) * b).astype(BF16)
        y_ref[...] = _dot(hmid, w2_ref[0])

    @pl.when(i >= nu_ref[0])
    def _():
        y_ref[...] = jnp.zeros_like(y_ref)


def _experts(block_expert, n_used, x_pad, w1, w3, w2):
    npad, d = x_pad.shape
    de = w1.shape[2]
    nb = npad // MOE_ROWS
    return pl.pallas_call(
        _expert_kernel,
        grid_spec=pltpu.PrefetchScalarGridSpec(
            num_scalar_prefetch=2,
            grid=(nb,),
            in_specs=[pl.BlockSpec((MOE_ROWS, d), lambda i, be, nu: (i, 0)),
                      pl.BlockSpec((1, d, de), lambda i, be, nu: (be[i], 0, 0)),
                      pl.BlockSpec((1, d, de), lambda i, be, nu: (be[i], 0, 0)),
                      pl.BlockSpec((1, de, d), lambda i, be, nu: (be[i], 0, 0))],
            out_specs=pl.BlockSpec((MOE_ROWS, d), lambda i, be, nu: (i, 0))),
        out_shape=jax.ShapeDtypeStruct((npad, d), F32),
        compiler_params=_cparams(("arbitrary",)),
        name="moe_experts",
    )(block_expert, n_used, x_pad, w1, w3, w2)


def _combine_kernel(dest_ref, yp_ref, rec_ref, x1_ref, mod_ref, g_ref, b_ref, o_ref, buf_ref, sem, *, tm):
    base = pl.program_id(0) * tm

    def issue(r, carry):
        for k in range(2):
            d = dest_ref[2 * (base + r) + k]
            _row_copy(yp_ref.at[pl.ds(d, 1), :], buf_ref.at[k, pl.ds(r, 1), :], sem).start()
        return carry

    lax.fori_loop(0, tm, issue, 0)

    def drain(r, carry):
        _row_copy(yp_ref.at[pl.ds(0, 1), :], buf_ref.at[0, pl.ds(0, 1), :], sem).wait()
        return carry

    lax.fori_loop(0, 2 * tm, drain, 0)

    rec = rec_ref[...]
    y = rec[:, 2:3] * buf_ref[0] + rec[:, 3:4] * buf_ref[1]
    y = (1.0 + mod_ref[0, 5:6, :]) * y
    o_ref[...] = _layer_norm(ALPHA * x1_ref[...] + y) * g_ref[...] + b_ref[...]


def _combine(dest, y_pad, rec, x1, mod, ln_g, ln_b, seq):
    nt, d = x1.shape
    tm = min(256, seq)
    return pl.pallas_call(
        functools.partial(_combine_kernel, tm=tm),
        grid_spec=pltpu.PrefetchScalarGridSpec(
            num_scalar_prefetch=1,
            grid=(nt // tm,),
            in_specs=[pl.BlockSpec(memory_space=pl.ANY),
                      pl.BlockSpec((tm, LANE), lambda i, dest: (i, 0)),
                      pl.BlockSpec((tm, d), lambda i, dest: (i, 0)),
                      pl.BlockSpec((1, 6, d), lambda i, dest: (i * tm // seq, 0, 0)),
                      pl.BlockSpec((1, d), lambda i, dest: (0, 0)),
                      pl.BlockSpec((1, d), lambda i, dest: (0, 0))],
            out_specs=pl.BlockSpec((tm, d), lambda i, dest: (i, 0)),
            scratch_shapes=[pltpu.VMEM((2, tm, d), F32), pltpu.SemaphoreType.DMA(())]),
        out_shape=jax.ShapeDtypeStruct((nt, d), F32),
        compiler_params=_cparams(("arbitrary",)),
        name="moe_combine",
    )(dest, y_pad, rec, x1, mod, ln_g, ln_b)


def _rel_bucket(dist):
    n = jnp.maximum(dist, 0)
    max_exact = REL_BUCKETS // 2
    nf = jnp.maximum(n, 1).astype(F32)
    large = max_exact + (jnp.log(nf / max_exact) / math.log(REL_MAX_DIST / max_exact)
                         * (REL_BUCKETS - max_exact)).astype(jnp.int32)
    large = jnp.minimum(large, REL_BUCKETS - 1)
    return jnp.where(n < max_exact, n, large)


def _bias_tables(rel_bias, seq):
    bucket_onehot = (_rel_bucket(jnp.arange(LANE))[:, None] == jnp.arange(REL_BUCKETS)).astype(F32)
    tab_d = jnp.einsum('db,hb->hd', bucket_onehot, rel_bias,
                       precision=lax.Precision.HIGHEST)
    tok = np.arange(QT)[None, :]
    key = np.arange(LANE)[:, None]
    far = tab_d[:, LANE - 1]
    cols = NSA_REP * QT

    def transposed(dist):
        idx = jnp.asarray(np.clip(dist, 0, LANE - 1).astype(np.int32))
        onehot = (idx[..., None] == jnp.arange(LANE, dtype=jnp.int32)).astype(F32)
        t = jnp.einsum('ijd,hd->hij', onehot, tab_d, precision=lax.Precision.HIGHEST)
        t = (t - far[:, None, None]) * LOG2E
        t = t.reshape(NSA_GROUPS, NSA_REP, LANE, QT).transpose(0, 2, 1, 3)
        return t.reshape(NSA_GROUPS, LANE, cols)

    t0t = transposed(tok - key)
    t1t = transposed(tok - key + QT)

    ns = seq // CMP_STRIDE
    d_c = tok - CMP_STRIDE * key + (CMP_STRIDE * CMP_PAD - (CMP_BLOCK - 1))
    seen = np.tile(d_c >= 0, (1, NSA_REP))
    recent = jnp.where(seen[None], transposed(d_c), NEG)
    tt = jnp.concatenate([jnp.zeros((NSA_GROUPS, ns, cols), F32), recent,
                          jnp.full((NSA_GROUPS, ns, cols), NEG, F32)], axis=1)

    rho = np.arange(WINDOW + QT)[:, None]
    tok = np.tile(np.arange(QT), NSA_REP)[None, :]
    band = (rho > tok) & (rho <= tok + WINDOW)
    rows = jnp.concatenate([jnp.zeros((NSA_GROUPS, WINDOW - QT, NSA_REP * QT), F32), t1t, t0t], axis=1)
    wb = jnp.where(band[None], rows, NEG)
    return t0t, t1t, wb, tt


def _overlap_matrix(seq):
    ns = seq // CMP_STRIDE
    nslc = seq // SLC_BLOCK
    ov = np.zeros((LANE, ns + LANE), np.float32)
    cs = np.arange(ns - 1) * CMP_STRIDE
    ss = np.arange(nslc) * SLC_BLOCK
    ov[:nslc, CMP_PAD:CMP_PAD + ns - 1] = ((cs[None, :] < ss[:, None] + SLC_BLOCK)
                                           & (cs[None, :] + CMP_BLOCK > ss[:, None]))
    return jnp.asarray(ov, BF16)


def _reorder_cols(a):
    lead = a.shape[:-1]
    gate = a[..., MAIN_COLS:MAIN_COLS + GATE_COLS]
    per = GATE_COLS // NSA_GROUPS
    gate_blocks = []
    for g in range(NSA_GROUPS):
        gate_blocks.append(gate[..., g * per:(g + 1) * per])
        gate_blocks.append(jnp.zeros(lead + (LANE - per,), a.dtype))
    pad = jnp.zeros(lead + ((CB_MGH - CB_GATE - NSA_GROUPS) * LANE,), a.dtype)
    return jnp.concatenate([a[..., :MAIN_COLS]] + gate_blocks + [pad, a[..., MAIN_COLS + GATE_COLS:]], axis=-1)


def kernel(x, c, ada_w, ada_b, w_in, b_in, hg_lb_logits, hg_norm_w, cmp_pos_k, cmp_w1_k, cmp_b1_k, cmp_w2_k, cmp_pos_v, cmp_w1_v, cmp_b1_v, cmp_w2_v, rel_bias, w_br_hg, w_br_nsa, w_out, ln1_g, ln1_b, router_grp_w, router_grp_b, router_exp_w, router_exp_b, exp_w1, exp_w3, exp_w2, ln2_g, ln2_b):
    batch, seq, d = x.shape
    nt = batch * seq
    assert d == D_MODEL and seq % 1024 == 0 and seq // SLC_BLOCK <= LANE
    l = 0
    x2 = x.reshape(nt, d)

    c_pad = jnp.zeros((8, d), F32).at[:batch].set(c)
    mod = _adaln(c_pad, ada_w[l], ada_b[l][None])[:batch].reshape(batch, 6, d)

    proj = _inproj(x2, mod, _reorder_cols(w_in[l]).astype(BF16), _reorder_cols(b_in[l])[None], seq)

    lb_all = jnp.cumsum(jax.nn.softmax(hg_lb_logits.astype(F32), axis=0), axis=0)
    o_h = _hgrn(proj, lb_all[l][None], hg_norm_w[l][None], batch, seq)

    half = CMP_STRIDE * LANE
    pos = jnp.stack([cmp_pos_k[l].reshape(2, half), cmp_pos_v[l].reshape(2, half)])
    kv_cols = proj[CB_KC:CB_KC + 2 * NSA_GROUPS].reshape(2 * NSA_GROUPS, nt // CMP_STRIDE, half)
    kvc = _compress(kv_cols, pos,
                    jnp.stack([cmp_w1_k[l], cmp_w1_v[l]]).astype(BF16),
                    jnp.stack([cmp_b1_k[l], cmp_b1_v[l]])[:, None, :],
                    jnp.stack([cmp_w2_k[l], cmp_w2_v[l]]).astype(BF16), batch, seq)

    t0t, t1t, wb, tt = _bias_tables(rel_bias, seq)
    o_c, sel_t = _cmp_sel(proj, kvc, tt, _overlap_matrix(seq), batch, seq)
    o_w = _window(proj, wb, batch, seq)
    o_s = _selected(proj, sel_t, t0t, t1t, batch, seq)

    w_r = jnp.zeros((d, LANE), F32).at[:, :N_GROUPS].set(router_grp_w[l])
    w_r = w_r.at[:, N_GROUPS:N_GROUPS + N_EXPERTS].set(router_exp_w[l])
    b_r = jnp.zeros((1, LANE), F32).at[0, :N_GROUPS].set(router_grp_b[l])
    b_r = b_r.at[0, N_GROUPS:N_GROUPS + N_EXPERTS].set(router_exp_b[l])
    x1, h2, logits = _merge(o_h, o_c, o_s, o_w, proj, x2, mod,
                            w_br_hg[l].astype(BF16), w_br_nsa[l].astype(BF16), w_out[l].astype(BF16),
                            ln1_g[l][None], ln1_b[l][None], w_r, b_r, seq)

    rec, cnt = _route(logits)
    counts = cnt[0, N_GROUPS:N_GROUPS + N_EXPERTS].astype(jnp.int32)
    padded = (counts + MOE_ROWS - 1) // MOE_ROWS * MOE_ROWS
    pend = jnp.cumsum(padded)
    pstart = pend - padded
    n_assign = 2 * nt
    nb = n_assign // MOE_ROWS + N_EXPERTS
    expert = rec[:, 0:2].astype(jnp.int32)
    dest = (pstart[expert] + rec[:, 4:6].astype(jnp.int32)).reshape(-1)
    block_start = jnp.arange(nb, dtype=jnp.int32) * MOE_ROWS
    block_expert = jnp.minimum(jnp.sum(pend[None, :] <= block_start[:, None], axis=1),
                               N_EXPERTS - 1).astype(jnp.int32)
    n_used = (pend[-1:] // MOE_ROWS).astype(jnp.int32)

    x_pad = _dispatch(dest, h2, jnp.zeros((nb * MOE_ROWS, d), F32))
    y_pad = _experts(block_expert, n_used, x_pad,
                     exp_w1[l].astype(BF16), exp_w3[l].astype(BF16), exp_w2[l].astype(BF16))
    out = _combine(dest, y_pad, rec, x1, mod, ln2_g[l][None], ln2_b[l][None], seq)
    return out.reshape(batch, seq, d)
```

```python
import functools
import math

import numpy as np
import jax
import jax.numpy as jnp
from jax import lax
from jax.experimental import pallas as pl
from jax.experimental.pallas import tpu as pltpu

F32 = jnp.float32
BF16 = jnp.bfloat16

D_MODEL = 1024
HG_HEADS = 8
HG_DK = 128
HG_DV = 128
HG_CHUNK = 32
NSA_HEADS = 8
NSA_GROUPS = 2
NSA_REP = NSA_HEADS // NSA_GROUPS
NSA_DK = 128
CMP_BLOCK = 32
CMP_STRIDE = 16
SLC_BLOCK = 64
SLC_TOPK = 16
WINDOW = 512
FORCE_SCORE = 1e4
REL_BUCKETS = 32
REL_MAX_DIST = 128
N_GROUPS = 4
EXP_PER_GROUP = 8
N_EXPERTS = N_GROUPS * EXP_PER_GROUP
D_EXPERT = D_MODEL // 2
DEPTH = 1
ALPHA = (2 * DEPTH) ** 0.25

LANE = 128
QT = 128
NEG = -1e30
SCALE = NSA_DK ** -0.5
LOG2E = math.log2(math.e)
SCALE_LOG2 = SCALE * LOG2E
KEY_BLK = 512
SEL_ROWS = 16
SEL_SUB = KEY_BLK // QT
WIN_SUB = 2
CMP_SUB = 4
CMP_PAD = 120
VMEM_LIMIT = 56 * 1024 * 1024

CB_HQ, CB_HF, CB_HI, CB_HG = 0, 8, 16, 24
CB_NQ = 32
CB_KC, CB_VC, CB_KS, CB_VS, CB_KW, CB_VW = 40, 42, 44, 46, 48, 50
CB_GATE = 52
CB_MGH, CB_MGN = 56, 64
NCB = 72
MAIN_COLS = 52 * LANE
GATE_COLS = 3 * NSA_HEADS

MOE_ROWS = 512


def _cparams(sem):
    return pltpu.CompilerParams(dimension_semantics=sem, vmem_limit_bytes=VMEM_LIMIT)


def _dot(a, b):
    return jnp.dot(a, b, preferred_element_type=F32)


def _dot_nt(a, b):
    return lax.dot_general(a, b, (((1,), (1,)), ((), ())), preferred_element_type=F32)


def _dot_tn(a, b):
    return lax.dot_general(a, b, (((0,), (0,)), ((), ())), preferred_element_type=F32)


def _split3(x):
    hi = x.astype(BF16)
    r = x - hi.astype(F32)
    mid = r.astype(BF16)
    lo = (r - mid.astype(F32)).astype(BF16)
    return hi, mid, lo


def _dot01(m01, x):
    hi, mid, lo = _split3(x)
    return _dot(m01, hi) + _dot(m01, mid) + _dot(m01, lo)


def _dot01_r(x, m01):
    hi, mid, lo = _split3(x)
    return _dot(hi, m01) + _dot(mid, m01) + _dot(lo, m01)


def _layer_norm(x, eps=1e-5):
    mu = jnp.mean(x, axis=-1, keepdims=True)
    xc = x - mu
    var = jnp.mean(xc * xc, axis=-1, keepdims=True)
    return xc * lax.rsqrt(var + eps)


def _adaln_kernel(c_ref, w_ref, b_ref, o_ref):
    c = c_ref[...]
    ca = c * jax.nn.sigmoid(c)
    o_ref[...] = jnp.dot(ca, w_ref[...], precision=lax.Precision.HIGHEST,
                         preferred_element_type=F32) + b_ref[...]


def _adaln(c_pad, w, b):
    rows, d = c_pad.shape
    n = w.shape[1]
    return pl.pallas_call(
        _adaln_kernel,
        grid=(n // d,),
        in_specs=[pl.BlockSpec((rows, d), lambda j: (0, 0)),
                  pl.BlockSpec((d, d), lambda j: (0, j)),
                  pl.BlockSpec((1, d), lambda j: (0, j))],
        out_specs=pl.BlockSpec((rows, d), lambda j: (0, j)),
        out_shape=jax.ShapeDtypeStruct((rows, n), F32),
        compiler_params=_cparams(("arbitrary",)),
        name="adaln",
    )(c_pad, w, b)


def _inproj_kernel(x_ref, mod_ref, w_ref, b_ref, o_ref, hn_ref, *, ncb_tile):
    @pl.when(pl.program_id(1) == 0)
    def _():
        hn = _layer_norm(x_ref[...])
        sh = mod_ref[0, 0:1, :]
        sc = mod_ref[0, 1:2, :]
        hn_ref[...] = (hn * (1.0 + sc) + sh).astype(BF16)

    res = _dot(hn_ref[...], w_ref[...]) + b_ref[...]
    for c in range(ncb_tile):
        o_ref[c] = res[:, c * LANE:(c + 1) * LANE]


def _inproj(x2, mod, w, b, seq):
    nt, d = x2.shape
    tm = min(1024, seq)
    ncb_tile = 9
    tn = ncb_tile * LANE
    return pl.pallas_call(
        functools.partial(_inproj_kernel, ncb_tile=ncb_tile),
        grid=(nt // tm, NCB // ncb_tile),
        in_specs=[pl.BlockSpec((tm, d), lambda i, j: (i, 0)),
                  pl.BlockSpec((1, 6, d), lambda i, j: (i * tm // seq, 0, 0)),
                  pl.BlockSpec((d, tn), lambda i, j: (0, j)),
                  pl.BlockSpec((1, tn), lambda i, j: (0, j))],
        out_specs=pl.BlockSpec((ncb_tile, tm, LANE), lambda i, j: (j, i, 0)),
        out_shape=jax.ShapeDtypeStruct((NCB, nt, LANE), F32),
        scratch_shapes=[pltpu.VMEM((tm, d), BF16)],
        compiler_params=_cparams(("arbitrary", "arbitrary")),
        name="inproj",
    )(x2, mod, w, b)


def _hgrn_kernel(q_ref, f_ref, v_ref, g_ref, lb_ref, nw_ref, o_ref, st_ref, *, rows):
    @pl.when(pl.program_id(1) == 0)
    def _():
        st_ref[...] = jnp.zeros_like(st_ref)

    ri = lax.broadcasted_iota(jnp.int32, (LANE, LANE), 0)
    ci = lax.broadcasted_iota(jnp.int32, (LANE, LANE), 1)
    same = (ri // HG_CHUNK) == (ci // HG_CHUNK)
    tril = same & (ci <= ri)
    cum_m = jnp.where(tril, 1.0, 0.0).astype(BF16)
    tot_m = jnp.where(same, 1.0, 0.0).astype(BF16)
    per = LANE // HG_CHUNK

    heads = range(HG_HEADS)
    hs = [slice(h * LANE, (h + 1) * LANE) for h in heads]

    def wide(ref, r0):
        return jnp.concatenate([ref[h, pl.ds(r0, LANE), :] for h in heads], axis=1)

    def body(i, carry):
        r0 = pl.multiple_of(i * LANE, LANE)
        lb = lb_ref[...]
        f = lb + (1.0 - lb) * jax.nn.sigmoid(wide(f_ref, r0))
        lf = jnp.log(f)
        k = 1.0 - f
        hi, mid, lo = _split3(lf)
        b = _dot(cum_m, hi) + _dot(cum_m, mid) + _dot(cum_m, lo)
        bl = _dot(tot_m, hi) + _dot(tot_m, mid) + _dot(tot_m, lo)
        q_in = (wide(q_ref, r0) * jnp.exp(b)).astype(BF16)
        k_in = (k * jnp.exp(-b)).astype(BF16)
        k_end = (k * jnp.exp(bl - b)).astype(BF16)
        vb = wide(v_ref, r0).astype(BF16)
        dec = [jnp.exp(bl[c * HG_CHUNK:c * HG_CHUNK + 1, :]) for c in range(per)]
        chunk = [slice(c * HG_CHUNK, (c + 1) * HG_CHUNK) for c in range(per)]

        att = [jnp.where(tril, _dot_nt(q_in[:, hs[h]], k_in[:, hs[h]]), 0.0).astype(BF16) for h in heads]
        upd = [[_dot_tn(vb[chunk[c], hs[h]], k_end[chunk[c], hs[h]]) for c in range(per)] for h in heads]
        intra = [_dot(att[h], vb[:, hs[h]]) for h in heads]

        st = [st_ref[h] for h in heads]
        inter = [[] for _ in heads]
        for c in range(per):
            for h in heads:
                inter[h].append(_dot_nt(q_in[chunk[c], hs[h]], st[h].astype(BF16)))
                st[h] = dec[c][:, hs[h]] * st[h] + upd[h][c]
        for h in heads:
            st_ref[h] = st[h]

        o = [intra[h] + jnp.concatenate(inter[h], axis=0) for h in heads]
        scale = [lax.rsqrt(jnp.mean(o[h] * o[h], axis=-1, keepdims=True) + 1e-6) for h in heads]
        on = jnp.concatenate([o[h] * scale[h] for h in heads], axis=1)
        on = on * nw_ref[...] * jax.nn.sigmoid(wide(g_ref, r0))
        o_ref[pl.ds(r0, LANE), :] = on.astype(BF16)
        return carry

    lax.fori_loop(0, rows // LANE, body, 0)


def _hgrn(proj, lb, nw, batch, seq):
    nt = proj.shape[1]
    tb = min(512, seq)
    nblk = seq // tb

    def slab(cb0):
        return pl.BlockSpec((HG_HEADS, tb, LANE), lambda b, t: (cb0 // HG_HEADS, b * nblk + t, 0))

    vec = pl.BlockSpec((1, HG_HEADS * LANE), lambda b, t: (0, 0))
    return pl.pallas_call(
        functools.partial(_hgrn_kernel, rows=tb),
        grid=(batch, nblk),
        in_specs=[slab(CB_HQ), slab(CB_HF), slab(CB_HI), slab(CB_HG), vec, vec],
        out_specs=pl.BlockSpec((tb, HG_HEADS * HG_DV), lambda b, t: (b * nblk + t, 0)),
        out_shape=jax.ShapeDtypeStruct((nt, HG_HEADS * HG_DV), BF16),
        scratch_shapes=[pltpu.VMEM((HG_HEADS, HG_DV, HG_DK), F32)],
        compiler_params=_cparams(("arbitrary", "arbitrary")),
        name="hgrn2",
    )(proj, proj, proj, proj, lb, nw)


def _compress_kernel(sub_ref, pos_ref, w1_ref, b1_ref, w2_ref, o_ref, *, ns):
    half = CMP_STRIDE * LANE
    sub = sub_ref[0]
    xa = (sub + pos_ref[0, 0:1, :]).astype(BF16)
    xb = (sub + pos_ref[0, 1:2, :]).astype(BF16)
    p0 = _dot(xa, w1_ref[0, 0:half, :])
    p1 = _dot(xb, w1_ref[0, half:2 * half, :])
    h = p0 + pltpu.roll(p1, ns - 1, axis=0) + b1_ref[0]
    a = h * jax.nn.sigmoid(h)
    out = _dot(a.astype(BF16), w2_ref[0])
    row = lax.broadcasted_iota(jnp.int32, out.shape, 0)
    out = jnp.where(row < ns - 1, out, 0.0)
    o_ref[0, 0, 0] = out


def _compress(proj16, pos, w1, b1, w2, batch, seq):
    ns = seq // CMP_STRIDE
    np_rows = ns
    width = CMP_STRIDE * LANE
    return pl.pallas_call(
        functools.partial(_compress_kernel, ns=ns),
        grid=(2, batch, NSA_GROUPS),
        in_specs=[pl.BlockSpec((1, ns, width), lambda s, b, g: (NSA_GROUPS * s + g, b, 0)),
                  pl.BlockSpec((1, 2, width), lambda s, b, g: (s, 0, 0)),
                  pl.BlockSpec((1, 2 * width, LANE), lambda s, b, g: (s, 0, 0)),
                  pl.BlockSpec((1, 1, LANE), lambda s, b, g: (s, 0, 0)),
                  pl.BlockSpec((1, LANE, LANE), lambda s, b, g: (s, 0, 0))],
        out_specs=pl.BlockSpec((1, 1, 1, np_rows, LANE), lambda s, b, g: (s, b, g, 0, 0)),
        out_shape=jax.ShapeDtypeStruct((2, batch, NSA_GROUPS, np_rows, LANE), F32),
        compiler_params=_cparams(("arbitrary", "arbitrary", "arbitrary")),
        name="nsa_compress",
    )(proj16, pos, w1, b1, w2)


def _cmp_sel_kernel(q0, q1, q2, q3, kc_ref, vc_ref, gate_ref, tt_ref, ov_ref, oc_ref, sel_ref,
                    kb_ref, vt_ref, *, np_rows):
    step = pl.program_id(2)
    cols = NSA_REP * QT
    ns = np_rows

    @pl.when(step == 0)
    def _():
        kb_ref[...] = (kc_ref[0, 0, 0] * SCALE_LOG2).astype(BF16)
        for c in range(np_rows // LANE):
            vt_ref[:, c * LANE:(c + 1) * LANE] = vc_ref[0, 0, 0, c * LANE:(c + 1) * LANE, :].T.astype(BF16)

    subs = range(CMP_SUB)
    tis = [step * CMP_SUB + u for u in subs]
    kb = kb_ref[...]
    qts = [_q_transposed((q0, q1, q2, q3), u) for u in subs]
    bias = [tt_ref[0, pl.ds(pl.multiple_of(ns + CMP_PAD - tis[u] * (QT // CMP_STRIDE), 8), np_rows), :]
            for u in subs]
    ss = [_dot(kb, qts[u]) + bias[u] for u in subs]
    ms = [jnp.max(ss[u], axis=0, keepdims=True) for u in subs]
    ps = [jnp.exp2(ss[u] - ms[u]) for u in subs]
    ls = [jnp.sum(ps[u], axis=0, keepdims=True) for u in subs]
    invs = [jnp.where(ms[u] > 0.5 * NEG, 1.0 / ls[u], 0.0) for u in subs]
    pn = [ps[u] * invs[u] for u in subs]
    vt = vt_ref[...]
    os_ = [_dot(vt, pn[u].astype(BF16)) for u in subs]
    for u in subs:
        rows = slice(u * QT, (u + 1) * QT)
        gt = jax.nn.sigmoid(gate_ref[0, rows, :])
        for r in range(NSA_REP):
            oc_ref[rows, r * LANE:(r + 1) * LANE] = gt[:, 3 * r:3 * r + 1] * os_[u][:, r * QT:(r + 1) * QT].T

    def group_sum(p):
        tot = p[:, 0:QT]
        for r in range(1, NSA_REP):
            tot = tot + p[:, r * QT:(r + 1) * QT]
        return tot

    imp = jnp.concatenate([_dot01(ov_ref[...], group_sum(pn[u])) for u in subs], axis=1)

    width = CMP_SUB * QT
    jj = lax.broadcasted_iota(jnp.int32, (LANE, width), 0)
    tok = step * width + lax.broadcasted_iota(jnp.int32, (LANE, width), 1)
    cur = tok // SLC_BLOCK
    forced = (jj == 0) | (jj == cur) | (jj == cur - 1)
    score = jnp.where(jj <= cur, jnp.where(forced, FORCE_SCORE, imp), -1.0)
    selb = jnp.full((LANE, width), NEG, F32)
    jf = jj.astype(F32)
    for _ in range(SLC_TOPK):
        mval = jnp.max(score, axis=0, keepdims=True)
        first = jnp.min(jnp.where(score == mval, jf, float(LANE)), axis=0, keepdims=True)
        pick = jf == first
        selb = jnp.where(pick, 0.0, selb)
        score = jnp.where(pick, -jnp.inf, score)
    sel_ref[0, 0] = selb


def _q_specs(nsteps, rows=QT):
    return [pl.BlockSpec((1, rows, LANE),
                         functools.partial(lambda b, g, t, r: (CB_NQ + NSA_REP * g + r, b * nsteps + t, 0), r=r))
            for r in range(NSA_REP)]


def _cmp_sel(proj, kvc, tt, ov_t, batch, seq):
    nt = proj.shape[1]
    rows = CMP_SUB * QT
    nsteps = seq // rows
    cols = NSA_REP * QT
    np_rows = kvc.shape[3]
    kv_spec = lambda s: pl.BlockSpec((1, 1, 1, np_rows, LANE), lambda b, g, t: (s, b, g, 0, 0))
    return pl.pallas_call(
        functools.partial(_cmp_sel_kernel, np_rows=np_rows),
        grid=(batch, NSA_GROUPS, nsteps),
        in_specs=_q_specs(nsteps, rows) + [
            kv_spec(0), kv_spec(1),
            pl.BlockSpec((1, rows, LANE), lambda b, g, t: (CB_GATE + g, b * nsteps + t, 0)),
            pl.BlockSpec((1, tt.shape[1], cols), lambda b, g, t: (g, 0, 0)),
            pl.BlockSpec((LANE, np_rows), lambda b, g, t: (0, 0))],
        out_specs=[pl.BlockSpec((rows, cols), lambda b, g, t: (b * nsteps + t, g)),
                   pl.BlockSpec((1, 1, LANE, rows), lambda b, g, t: (b, g, 0, t))],
        out_shape=[jax.ShapeDtypeStruct((nt, NSA_HEADS * LANE), F32),
                   jax.ShapeDtypeStruct((batch, NSA_GROUPS, LANE, seq), F32)],
        scratch_shapes=[pltpu.VMEM((np_rows, LANE), BF16), pltpu.VMEM((LANE, np_rows), BF16)],
        compiler_params=_cparams(("arbitrary", "arbitrary", "arbitrary")),
        name="nsa_cmp_select",
    )(proj, proj, proj, proj, kvc, kvc, proj, tt, ov_t)


def _q_transposed(q_refs, sub=0):
    return jnp.concatenate([r[0, sub * QT:(sub + 1) * QT, :].T for r in q_refs], axis=1).astype(BF16)


def _softmax_step(s, vt, state):
    m, l, acc = state
    m_new = jnp.maximum(m, jnp.max(s, axis=0, keepdims=True))
    alpha = jnp.exp2(m - m_new)
    p = jnp.exp2(s - m_new)
    l = alpha * l + jnp.sum(p, axis=0, keepdims=True)
    acc = alpha * acc + _dot(vt, p.astype(BF16))
    return m_new, l, acc


def _finish_t(state, gate_ref, branch, o_ref, sub=0):
    _, l, acc = state
    o = acc / jnp.where(l == 0.0, 1.0, l)
    rows = slice(sub * QT, (sub + 1) * QT)
    gt = jax.nn.sigmoid(gate_ref[0, rows, :])
    for r in range(NSA_REP):
        col = 3 * r + branch
        o_ref[rows, r * LANE:(r + 1) * LANE] = gt[:, col:col + 1] * o[:, r * QT:(r + 1) * QT].T


def _init_state(cols):
    return (jnp.full((1, cols), NEG, F32), jnp.zeros((1, cols), F32), jnp.zeros((LANE, cols), F32))


def _win_kernel(q0, q1, q2, q3, k_ref, v_ref, gate_ref, wb_ref, o_ref, kb_ref, vt_ref, *, seq):
    step = pl.program_id(2)
    cols = NSA_REP * QT
    wt = WINDOW // QT

    @pl.when(step == 0)
    def _():
        kb_ref[0:WINDOW, :] = jnp.zeros((WINDOW, LANE), BF16)
        for i in range(wt):
            vt_ref[i] = jnp.zeros((LANE, QT), BF16)

        def fill(i, carry):
            r0 = pl.multiple_of(i * QT, QT)
            kb_ref[pl.ds(WINDOW + r0, QT), :] = (k_ref[0, pl.ds(r0, QT), :] * SCALE_LOG2).astype(BF16)
            vt_ref[wt + i] = v_ref[0, pl.ds(r0, QT), :].T.astype(BF16)
            return carry

        lax.fori_loop(0, seq // QT, fill, 0)

    rho = lax.broadcasted_iota(jnp.int32, (WINDOW + QT, cols), 0)
    subs = range(WIN_SUB)
    tis = [step * WIN_SUB + sub for sub in subs]
    qts = [_q_transposed((q0, q1, q2, q3), sub) for sub in subs]
    ss = [_dot(kb_ref[pl.ds(pl.multiple_of(tis[u] * QT, QT), WINDOW + QT), :], qts[u]) + wb_ref[0]
          for u in subs]
    ss = [jnp.where(rho >= (wt - tis[u]) * QT, ss[u], NEG) for u in subs]
    ms = [jnp.max(ss[u], axis=0, keepdims=True) for u in subs]
    ps = [jnp.exp2(ss[u] - ms[u]) for u in subs]
    ls = [jnp.sum(ps[u], axis=0, keepdims=True) for u in subs]
    pbs = [ps[u].astype(BF16) for u in subs]
    accs = [_dot(vt_ref[tis[u]], pbs[u][0:QT]) for u in subs]
    for d in range(1, wt + 1):
        accs = [accs[u] + _dot(vt_ref[tis[u] + d], pbs[u][d * QT:(d + 1) * QT]) for u in subs]
    for u in subs:
        _finish_t((ms[u], ls[u], accs[u]), gate_ref, 2, o_ref, u)


def _window(proj, wb, batch, seq):
    nt = proj.shape[1]
    rows = WIN_SUB * QT
    nqt = seq // rows
    cols = NSA_REP * QT
    slab = lambda cb0: pl.BlockSpec((1, seq, LANE), lambda b, g, t: (cb0 + g, b, 0))
    return pl.pallas_call(
        functools.partial(_win_kernel, seq=seq),
        grid=(batch, NSA_GROUPS, nqt),
        in_specs=_q_specs(nqt, rows) + [
            slab(CB_KW), slab(CB_VW),
            pl.BlockSpec((1, rows, LANE), lambda b, g, t: (CB_GATE + g, b * nqt + t, 0)),
            pl.BlockSpec((1, WINDOW + QT, cols), lambda b, g, t: (g, 0, 0))],
        out_specs=pl.BlockSpec((rows, cols), lambda b, g, t: (b * nqt + t, g)),
        out_shape=jax.ShapeDtypeStruct((nt, NSA_HEADS * LANE), F32),
        scratch_shapes=[pltpu.VMEM((seq + WINDOW, LANE), BF16),
                        pltpu.VMEM((seq // QT + WINDOW // QT, LANE, QT), BF16)],
        compiler_params=_cparams(("arbitrary", "arbitrary", "arbitrary")),
        name="nsa_window",
    )(proj, proj, proj, proj, proj, proj, proj, wb)


def _softmax_steps(ss, vt, states):
    n = range(len(ss))
    m_new = [jnp.maximum(states[u][0], jnp.max(ss[u], axis=0, keepdims=True)) for u in n]
    alpha = [jnp.exp2(states[u][0] - m_new[u]) for u in n]
    p = [jnp.exp2(ss[u] - m_new[u]) for u in n]
    l = [alpha[u] * states[u][1] + jnp.sum(p[u], axis=0, keepdims=True) for u in n]
    pv = [_dot(vt, p[u].astype(BF16)) for u in n]
    return [(m_new[u], l[u], alpha[u] * states[u][2] + pv[u]) for u in n]


def _sel_kernel(q0, q1, q2, q3, k_ref, v_ref, sel_ref, gate_ref, nd_ref, npv_ref, o_ref,
                kb_ref, vt_ref, qa_ref, m_ref, l_ref, acc_ref, *, seq):
    step = pl.program_id(2)
    cols = NSA_REP * QT
    spb = KEY_BLK // SLC_BLOCK
    subs = range(SEL_SUB)

    @pl.when(step == 0)
    def _():
        blk = lax.broadcasted_iota(jnp.int32, (KEY_BLK, LANE), 0) // SLC_BLOCK
        onehot = jnp.where(blk == lax.broadcasted_iota(jnp.int32, (KEY_BLK, LANE), 1), 1.0, 0.0).astype(BF16)

        def fill(i, carry):
            r0 = pl.multiple_of(i * KEY_BLK, KEY_BLK)
            kb_ref[pl.ds(r0, KEY_BLK), 0:LANE] = (k_ref[0, pl.ds(r0, KEY_BLK), :] * SCALE_LOG2).astype(BF16)
            kb_ref[pl.ds(r0, KEY_BLK), LANE:2 * LANE] = onehot
            for c in range(SEL_SUB):
                vt_ref[i, :, c * QT:(c + 1) * QT] = v_ref[0, pl.ds(r0 + c * QT, QT), :].T.astype(BF16)
            return carry

        lax.fori_loop(0, seq // KEY_BLK, fill, 0)
        qa_ref[...] = jnp.zeros_like(qa_ref)

    for u in subs:
        qa_ref[u, 0:LANE, :] = _q_transposed((q0, q1, q2, q3), u)

    def block(kb):
        r0 = pl.multiple_of(kb * KEY_BLK, KEY_BLK)
        return kb_ref[pl.ds(r0, KEY_BLK), :], vt_ref[kb]

    pad_rows = jnp.zeros((SEL_ROWS - spb, cols), F32)

    def scores(kb, k):
        for u in subs:
            rows = sel_ref[0, 0, pl.ds(pl.multiple_of(kb * spb, spb), spb), u * QT:(u + 1) * QT]
            rows = jnp.concatenate([rows] * NSA_REP, axis=1)
            qa_ref[u, LANE:LANE + SEL_ROWS, :] = jnp.concatenate([rows, pad_rows], axis=0).astype(BF16)
        return [_dot(k, qa_ref[u]) for u in subs]

    k, vt = block(step)
    ss = scores(step, k)
    ss = [ss[u] + nd_ref[0, u] for u in subs]
    states = _softmax_steps(ss, vt, [_init_state(cols) for _ in subs])

    prev = jnp.maximum(step - 1, 0)
    k, vt = block(prev)
    ss = scores(prev, k)
    ss[0] = ss[0] + npv_ref[0]
    exists = (step + jnp.zeros((KEY_BLK, cols), jnp.int32)) >= 1
    ss = [jnp.where(exists, ss[u], NEG) for u in subs]
    states = _softmax_steps(ss, vt, states)
    for u in subs:
        m_ref[u], l_ref[u], acc_ref[u] = states[u]

    def body(kb, carry):
        k, vt = block(kb)
        st = _softmax_steps(scores(kb, k), vt, [(m_ref[u], l_ref[u], acc_ref[u]) for u in subs])
        for u in subs:
            m_ref[u], l_ref[u], acc_ref[u] = st[u]
        return carry

    lax.fori_loop(0, jnp.maximum(step - 1, 0), body, 0)
    for u in subs:
        _finish_t((m_ref[u], l_ref[u], acc_ref[u]), gate_ref, 1, o_ref, u)


def _selected(proj, sel_t, nd, npv, batch, seq):
    nt = proj.shape[1]
    rows = SEL_SUB * QT
    nsteps = seq // rows
    cols = NSA_REP * QT
    slab = lambda cb0: pl.BlockSpec((1, seq, LANE), lambda b, g, t: (cb0 + g, b, 0))
    return pl.pallas_call(
        functools.partial(_sel_kernel, seq=seq),
        grid=(batch, NSA_GROUPS, nsteps),
        in_specs=_q_specs(nsteps, rows) + [
            slab(CB_KS), slab(CB_VS),
            pl.BlockSpec((1, 1, LANE, rows), lambda b, g, t: (b, g, 0, t)),
            pl.BlockSpec((1, rows, LANE), lambda b, g, t: (CB_GATE + g, b * nsteps + t, 0)),
            pl.BlockSpec((1, SEL_SUB, KEY_BLK, cols), lambda b, g, t: (g, 0, 0, 0)),
            pl.BlockSpec((1, KEY_BLK, cols), lambda b, g, t: (g, 0, 0))],
        out_specs=pl.BlockSpec((rows, cols), lambda b, g, t: (b * nsteps + t, g)),
        out_shape=jax.ShapeDtypeStruct((nt, NSA_HEADS * LANE), F32),
        scratch_shapes=[pltpu.VMEM((seq, 2 * LANE), BF16),
                        pltpu.VMEM((seq // KEY_BLK, LANE, KEY_BLK), BF16),
                        pltpu.VMEM((SEL_SUB, 2 * LANE, cols), BF16),
                        pltpu.VMEM((SEL_SUB, 1, cols), F32), pltpu.VMEM((SEL_SUB, 1, cols), F32),
                        pltpu.VMEM((SEL_SUB, LANE, cols), F32)],
        compiler_params=_cparams(("arbitrary", "arbitrary", "arbitrary")),
        name="nsa_selected",
    )(proj, proj, proj, proj, proj, proj, sel_t, proj, nd, npv)


def _merge_kernel(oh_ref, oc_ref, os_ref, ow_ref, mgh_ref, mgn_ref, x_ref, mod_ref,
                  wh_ref, wn_ref, wo_ref, g_ref, b_ref, wr_ref, br_ref,
                  x1_ref, h2_ref, lg_ref):
    nblk = D_MODEL // LANE
    a_h = _dot(oh_ref[...], wh_ref[...])
    o_n = (oc_ref[...] + os_ref[...] + ow_ref[...]).astype(BF16)
    a_n = _dot(o_n, wn_ref[...])
    gh = jnp.concatenate([mgh_ref[c] for c in range(nblk)], axis=-1)
    gn = jnp.concatenate([mgn_ref[c] for c in range(nblk)], axis=-1)
    merged = jax.nn.sigmoid(gh) * a_h + jax.nn.sigmoid(gn) * a_n
    y = (1.0 + mod_ref[0, 2:3, :]) * _dot(merged.astype(BF16), wo_ref[...])
    x1 = _layer_norm(ALPHA * x_ref[...] + y) * g_ref[...] + b_ref[...]
    x1_ref[...] = x1
    h2 = _layer_norm(x1) * (1.0 + mod_ref[0, 4:5, :]) + mod_ref[0, 3:4, :]
    h2_ref[...] = h2
    lg_ref[...] = jnp.dot(h2, wr_ref[...], precision=lax.Precision.HIGHEST,
                          preferred_element_type=F32) + br_ref[...]


def _merge(o_h, o_c, o_s, o_w, proj, x2, mod, w_h, w_n, w_o, ln_g, ln_b, w_r, b_r, seq):
    nt, d = x2.shape
    tm = min(256, seq)
    nblk = d // LANE
    row = lambda w: pl.BlockSpec((tm, w), lambda i: (i, 0))
    full = lambda a: pl.BlockSpec(a.shape, lambda i: (0,) * a.ndim)
    return pl.pallas_call(
        _merge_kernel,
        grid=(nt // tm,),
        in_specs=[row(d), row(d), row(d), row(d),
                  pl.BlockSpec((nblk, tm, LANE), lambda i: (CB_MGH // nblk, i, 0)),
                  pl.BlockSpec((nblk, tm, LANE), lambda i: (CB_MGN // nblk, i, 0)),
                  row(d),
                  pl.BlockSpec((1, 6, d), lambda i: (i * tm // seq, 0, 0)),
                  full(w_h), full(w_n), full(w_o), full(ln_g), full(ln_b), full(w_r), full(b_r)],
        out_specs=[row(d), row(d), row(LANE)],
        out_shape=[jax.ShapeDtypeStruct((nt, d), F32),
                   jax.ShapeDtypeStruct((nt, d), F32),
                   jax.ShapeDtypeStruct((nt, LANE), F32)],
        compiler_params=_cparams(("arbitrary",)),
        name="merge_outproj",
    )(o_h, o_c, o_s, o_w, proj, proj, x2, mod, w_h, w_n, w_o, ln_g, ln_b, w_r, b_r)


def _route_kernel(lg_ref, rec_ref, cnt_ref, carry_ref, *, tm):
    @pl.when(pl.program_id(0) == 0)
    def _():
        carry_ref[...] = jnp.zeros_like(carry_ref)

    lg = lg_ref[...]
    lane = lax.broadcasted_iota(jnp.int32, (tm, LANE), 1).astype(F32)
    far = float(LANE)
    gmask = lane < N_GROUPS
    gl = jnp.where(gmask, lg, -jnp.inf)
    gmax = jnp.max(gl, axis=-1, keepdims=True)
    gsum = jnp.sum(jnp.where(gmask, jnp.exp(gl - gmax), 0.0), axis=-1, keepdims=True)
    grp_p = 1.0 / gsum
    gidx = jnp.min(jnp.where(gl == gmax, lane, far), axis=-1, keepdims=True)
    lo = N_GROUPS + EXP_PER_GROUP * gidx
    emask = (lane >= lo) & (lane < lo + EXP_PER_GROUP)
    el = jnp.where(emask, lg, -jnp.inf)
    m1 = jnp.max(el, axis=-1, keepdims=True)
    i1 = jnp.min(jnp.where(el == m1, lane, far), axis=-1, keepdims=True)
    el2 = jnp.where(lane == i1, -jnp.inf, el)
    m2 = jnp.max(el2, axis=-1, keepdims=True)
    i2 = jnp.min(jnp.where(emask & (lane != i1) & (el2 == m2), lane, far), axis=-1, keepdims=True)
    e = jnp.exp(m2 - m1)
    w0 = grp_p / (1.0 + e)
    w1 = grp_p * e / (1.0 + e)

    oh0 = lane == i1
    oh1 = lane == i2
    f0 = jnp.where(oh0, 1.0, 0.0)
    f1 = jnp.where(oh1, 1.0, 0.0)
    ri = lax.broadcasted_iota(jnp.int32, (tm, tm), 0)
    ci = lax.broadcasted_iota(jnp.int32, (tm, tm), 1)
    before = jnp.where(ci < ri, 1.0, 0.0).astype(BF16)
    cum0 = _dot(before, f0.astype(BF16))
    cum1 = _dot(before, f1.astype(BF16))
    tot0 = jnp.sum(f0, axis=0, keepdims=True)
    tot1 = jnp.sum(f1, axis=0, keepdims=True)
    carry = carry_ref[...]
    rank0 = jnp.sum(jnp.where(oh0, carry + cum0, 0.0), axis=-1, keepdims=True)
    rank1 = jnp.sum(jnp.where(oh1, carry + tot0 + cum1, 0.0), axis=-1, keepdims=True)
    carry = carry + tot0 + tot1
    carry_ref[...] = carry
    cnt_ref[...] = carry

    rec = jnp.where(lane == 0, i1 - N_GROUPS, 0.0)
    rec = jnp.where(lane == 1, i2 - N_GROUPS, rec)
    rec = jnp.where(lane == 2, w0, rec)
    rec = jnp.where(lane == 3, w1, rec)
    rec = jnp.where(lane == 4, rank0, rec)
    rec = jnp.where(lane == 5, rank1, rec)
    rec_ref[...] = rec


def _route(logits):
    nt = logits.shape[0]
    tm = min(512, nt)
    return pl.pallas_call(
        functools.partial(_route_kernel, tm=tm),
        grid=(nt // tm,),
        in_specs=[pl.BlockSpec((tm, LANE), lambda i: (i, 0))],
        out_specs=[pl.BlockSpec((tm, LANE), lambda i: (i, 0)),
                   pl.BlockSpec((1, LANE), lambda i: (0, 0))],
        out_shape=[jax.ShapeDtypeStruct((nt, LANE), F32),
                   jax.ShapeDtypeStruct((1, LANE), F32)],
        scratch_shapes=[pltpu.VMEM((1, LANE), F32)],
        compiler_params=_cparams(("arbitrary",)),
        name="moe_route",
    )(logits)


def _row_copy(src, dst, sem):
    return pltpu.make_async_copy(src, dst, sem)


def _dispatch_kernel(dest_ref, h_ref, zero_ref, xp_ref, sem, *, tm):
    del zero_ref
    base = pl.program_id(0) * tm

    def issue(r, carry):
        for k in range(2):
            d = dest_ref[2 * (base + r) + k]
            _row_copy(h_ref.at[pl.ds(r, 1), :], xp_ref.at[pl.ds(d, 1), :], sem).start()
        return carry

    lax.fori_loop(0, tm, issue, 0)

    def drain(r, carry):
        _row_copy(h_ref.at[pl.ds(0, 1), :], xp_ref.at[pl.ds(0, 1), :], sem).wait()
        return carry

    lax.fori_loop(0, 2 * tm, drain, 0)


def _dispatch(dest, h2, x_pad0):
    nt, d = h2.shape
    tm = min(256, nt)
    return pl.pallas_call(
        functools.partial(_dispatch_kernel, tm=tm),
        grid_spec=pltpu.PrefetchScalarGridSpec(
            num_scalar_prefetch=1,
            grid=(nt // tm,),
            in_specs=[pl.BlockSpec((tm, d), lambda i, dest: (i, 0)),
                      pl.BlockSpec(memory_space=pl.ANY)],
            out_specs=pl.BlockSpec(memory_space=pl.ANY),
            scratch_shapes=[pltpu.SemaphoreType.DMA(())]),
        out_shape=jax.ShapeDtypeStruct(x_pad0.shape, x_pad0.dtype),
        input_output_aliases={2: 0},
        compiler_params=_cparams(("arbitrary",)),
        name="moe_dispatch",
    )(dest, h2, x_pad0)


def _expert_kernel(be_ref, nu_ref, x_ref, w1_ref, w3_ref, w2_ref, y_ref):
    del be_ref
    i = pl.program_id(0)

    @pl.when(i < nu_ref[0])
    def _():
        xb = x_ref[...].astype(BF16)
        a = _dot(xb, w1_ref[0])
        b = _dot(xb, w3_ref[0])
        hmid = (a * jax.nn.sigmoid(a) * b).astype(BF16)
        y_ref[...] = _dot(hmid, w2_ref[0])

    @pl.when(i >= nu_ref[0])
    def _():
        y_ref[...] = jnp.zeros_like(y_ref)


def _experts(block_expert, n_used, x_pad, w1, w3, w2):
    npad, d = x_pad.shape
    de = w1.shape[2]
    nb = npad // MOE_ROWS
    return pl.pallas_call(
        _expert_kernel,
        grid_spec=pltpu.PrefetchScalarGridSpec(
            num_scalar_prefetch=2,
            grid=(nb,),
            in_specs=[pl.BlockSpec((MOE_ROWS, d), lambda i, be, nu: (i, 0)),
                      pl.BlockSpec((1, d, de), lambda i, be, nu: (be[i], 0, 0)),
                      pl.BlockSpec((1, d, de), lambda i, be, nu: (be[i], 0, 0)),
                      pl.BlockSpec((1, de, d), lambda i, be, nu: (be[i], 0, 0))],
            out_specs=pl.BlockSpec((MOE_ROWS, d), lambda i, be, nu: (i, 0))),
        out_shape=jax.ShapeDtypeStruct((npad, d), F32),
        compiler_params=_cparams(("arbitrary",)),
        name="moe_experts",
    )(block_expert, n_used, x_pad, w1, w3, w2)


def _combine_kernel(dest_ref, yp_ref, rec_ref, x1_ref, mod_ref, g_ref, b_ref, o_ref, buf_ref, sem, *, tm):
    base = pl.program_id(0) * tm

    def issue(r, carry):
        for k in range(2):
            d = dest_ref[2 * (base + r) + k]
            _row_copy(yp_ref.at[pl.ds(d, 1), :], buf_ref.at[k, pl.ds(r, 1), :], sem).start()
        return carry

    lax.fori_loop(0, tm, issue, 0)

    def drain(r, carry):
        _row_copy(yp_ref.at[pl.ds(0, 1), :], buf_ref.at[0, pl.ds(0, 1), :], sem).wait()
        return carry

    lax.fori_loop(0, 2 * tm, drain, 0)

    rec = rec_ref[...]
    y = rec[:, 2:3] * buf_ref[0] + rec[:, 3:4] * buf_ref[1]
    y = (1.0 + mod_ref[0, 5:6, :]) * y
    o_ref[...] = _layer_norm(ALPHA * x1_ref[...] + y) * g_ref[...] + b_ref[...]


def _combine(dest, y_pad, rec, x1, mod, ln_g, ln_b, seq):
    nt, d = x1.shape
    tm = min(256, seq)
    return pl.pallas_call(
        functools.partial(_combine_kernel, tm=tm),
        grid_spec=pltpu.PrefetchScalarGridSpec(
            num_scalar_prefetch=1,
            grid=(nt // tm,),
            in_specs=[pl.BlockSpec(memory_space=pl.ANY),
                      pl.BlockSpec((tm, LANE), lambda i, dest: (i, 0)),
                      pl.BlockSpec((tm, d), lambda i, dest: (i, 0)),
                      pl.BlockSpec((1, 6, d), lambda i, dest: (i * tm // seq, 0, 0)),
                      pl.BlockSpec((1, d), lambda i, dest: (0, 0)),
                      pl.BlockSpec((1, d), lambda i, dest: (0, 0))],
            out_specs=pl.BlockSpec((tm, d), lambda i, dest: (i, 0)),
            scratch_shapes=[pltpu.VMEM((2, tm, d), F32), pltpu.SemaphoreType.DMA(())]),
        out_shape=jax.ShapeDtypeStruct((nt, d), F32),
        compiler_params=_cparams(("arbitrary",)),
        name="moe_combine",
    )(dest, y_pad, rec, x1, mod, ln_g, ln_b)


def _rel_bucket(dist):
    n = jnp.maximum(dist, 0)
    max_exact = REL_BUCKETS // 2
    nf = jnp.maximum(n, 1).astype(F32)
    large = max_exact + (jnp.log(nf / max_exact) / math.log(REL_MAX_DIST / max_exact)
                         * (REL_BUCKETS - max_exact)).astype(jnp.int32)
    large = jnp.minimum(large, REL_BUCKETS - 1)
    return jnp.where(n < max_exact, n, large)


def _bias_tables(rel_bias, seq):
    bucket_onehot = (_rel_bucket(jnp.arange(LANE))[:, None] == jnp.arange(REL_BUCKETS)).astype(F32)
    tab_d = jnp.einsum('db,hb->hd', bucket_onehot, rel_bias,
                       precision=lax.Precision.HIGHEST)
    tok = np.arange(QT)[None, :]
    key = np.arange(LANE)[:, None]
    far = tab_d[:, LANE - 1]
    cols = NSA_REP * QT

    def transposed(dist):
        idx = jnp.asarray(np.clip(dist, 0, LANE - 1).astype(np.int32))
        onehot = (idx[..., None] == jnp.arange(LANE, dtype=jnp.int32)).astype(F32)
        t = jnp.einsum('ijd,hd->hij', onehot, tab_d, precision=lax.Precision.HIGHEST)
        t = (t - far[:, None, None]) * LOG2E
        t = t.reshape(NSA_GROUPS, NSA_REP, LANE, QT).transpose(0, 2, 1, 3)
        return t.reshape(NSA_GROUPS, LANE, cols)

    t0t = transposed(tok - key)
    t1t = transposed(tok - key + QT)

    ns = seq // CMP_STRIDE
    d_c = tok - CMP_STRIDE * key + (CMP_STRIDE * CMP_PAD - (CMP_BLOCK - 1))
    seen = np.tile(d_c >= 0, (1, NSA_REP))
    recent = jnp.where(seen[None], transposed(d_c), NEG)
    tt = jnp.concatenate([jnp.zeros((NSA_GROUPS, ns, cols), F32), recent,
                          jnp.full((NSA_GROUPS, ns, cols), NEG, F32)], axis=1)

    rho = np.arange(WINDOW + QT)[:, None]
    tok_w = np.tile(np.arange(QT), NSA_REP)[None, :]
    band = (rho > tok_w) & (rho <= tok_w + WINDOW)
    rows = jnp.concatenate([jnp.zeros((NSA_GROUPS, WINDOW - QT, NSA_REP * QT), F32), t1t, t0t], axis=1)
    wb = jnp.where(band[None], rows, NEG)

    zeros = lambda n: jnp.zeros((NSA_GROUPS, n * QT, cols), F32)
    negs = lambda n: jnp.full((NSA_GROUPS, n * QT, cols), NEG, F32)
    diag = jnp.where(np.tile(tok >= key, (1, NSA_REP))[None], t0t, NEG)
    nd = jnp.stack([jnp.concatenate(([zeros(u - 1), t1t] if u else []) + [diag, negs(SEL_SUB - 1 - u)], axis=1)
                    for u in range(SEL_SUB)], axis=1)
    npv = jnp.concatenate([zeros(SEL_SUB - 1), t1t], axis=1)
    return nd, npv, wb, tt


def _overlap_matrix(seq):
    ns = seq // CMP_STRIDE
    nslc = seq // SLC_BLOCK
    ov = np.zeros((LANE, ns), np.float32)
    cs = np.arange(ns - 1) * CMP_STRIDE
    ss = np.arange(nslc) * SLC_BLOCK
    ov[:nslc, :ns - 1] = ((cs[None, :] < ss[:, None] + SLC_BLOCK) & (cs[None, :] + CMP_BLOCK > ss[:, None]))
    return jnp.asarray(ov, BF16)


def _reorder_cols(a):
    lead = a.shape[:-1]
    gate = a[..., MAIN_COLS:MAIN_COLS + GATE_COLS]
    per = GATE_COLS // NSA_GROUPS
    gate_blocks = []
    for g in range(NSA_GROUPS):
        gate_blocks.append(gate[..., g * per:(g + 1) * per])
        gate_blocks.append(jnp.zeros(lead + (LANE - per,), a.dtype))
    pad = jnp.zeros(lead + ((CB_MGH - CB_GATE - NSA_GROUPS) * LANE,), a.dtype)
    return jnp.concatenate([a[..., :MAIN_COLS]] + gate_blocks + [pad, a[..., MAIN_COLS + GATE_COLS:]], axis=-1)


def kernel(x, c, ada_w, ada_b, w_in, b_in, hg_lb_logits, hg_norm_w, cmp_pos_k, cmp_w1_k, cmp_b1_k, cmp_w2_k, cmp_pos_v, cmp_w1_v, cmp_b1_v, cmp_w2_v, rel_bias, w_br_hg, w_br_nsa, w_out, ln1_g, ln1_b, router_grp_w, router_grp_b, router_exp_w, router_exp_b, exp_w1, exp_w3, exp_w2, ln2_g, ln2_b):
    batch, seq, d = x.shape
    nt = batch * seq
    assert d == D_MODEL and seq % 1024 == 0 and seq // SLC_BLOCK <= LANE
    l = 0
    x2 = x.reshape(nt, d)

    c_pad = jnp.zeros((8, d), F32).at[:batch].set(c)
    mod = _adaln(c_pad, ada_w[l], ada_b[l][None])[:batch].reshape(batch, 6, d)

    proj = _inproj(x2, mod, _reorder_cols(w_in[l]).astype(BF16), _reorder_cols(b_in[l])[None], seq)

    lb_all = jnp.cumsum(jax.nn.softmax(hg_lb_logits.astype(F32), axis=0), axis=0)
    o_h = _hgrn(proj, lb_all[l][None], hg_norm_w[l][None], batch, seq)

    half = CMP_STRIDE * LANE
    pos = jnp.stack([cmp_pos_k[l].reshape(2, half), cmp_pos_v[l].reshape(2, half)])
    kv_cols = proj[CB_KC:CB_KC + 2 * NSA_GROUPS].reshape(2 * NSA_GROUPS, nt // CMP_STRIDE, half)
    kvc = _compress(kv_cols, pos,
                    jnp.stack([cmp_w1_k[l], cmp_w1_v[l]]).astype(BF16),
                    jnp.stack([cmp_b1_k[l], cmp_b1_v[l]])[:, None, :],
                    jnp.stack([cmp_w2_k[l], cmp_w2_v[l]]).astype(BF16), batch, seq)

    nd, npv, wb, tt = _bias_tables(rel_bias, seq)
    o_c, sel_t = _cmp_sel(proj, kvc, tt, _overlap_matrix(seq), batch, seq)
    o_w = _window(proj, wb, batch, seq)
    o_s = _selected(proj, sel_t, nd, npv, batch, seq)

    w_r = jnp.zeros((d, LANE), F32).at[:, :N_GROUPS].set(router_grp_w[l])
    w_r = w_r.at[:, N_GROUPS:N_GROUPS + N_EXPERTS].set(router_exp_w[l])
    b_r = jnp.zeros((1, LANE), F32).at[0, :N_GROUPS].set(router_grp_b[l])
    b_r = b_r.at[0, N_GROUPS:N_GROUPS + N_EXPERTS].set(router_exp_b[l])
    x1, h2, logits = _merge(o_h, o_c, o_s, o_w, proj, x2, mod,
                            w_br_hg[l].astype(BF16), w_br_nsa[l].astype(BF16), w_out[l].astype(BF16),
                            ln1_g[l][None], ln1_b[l][None], w_r, b_r, seq)

    rec, cnt = _route(logits)
    counts = cnt[0, N_GROUPS:N_GROUPS + N_EXPERTS].astype(jnp.int32)
    padded = (counts + MOE_ROWS - 1) // MOE_ROWS * MOE_ROWS
    pend = jnp.cumsum(padded)
    pstart = pend - padded
    n_assign = 2 * nt
    nb = n_assign // MOE_ROWS + N_EXPERTS
    expert = rec[:, 0:2].astype(jnp.int32)
    dest = (pstart[expert] + rec[:, 4:6].astype(jnp.int32)).reshape(-1)
    block_start = jnp.arange(nb, dtype=jnp.int32) * MOE_ROWS
    block_expert = jnp.minimum(jnp.sum(pend[None, :] <= block_start[:, None], axis=1),
                               N_EXPERTS - 1).astype(jnp.int32)
    n_used = (pend[-1:] // MOE_ROWS).astype(jnp.int32)

    x_pad = _dispatch(dest, h2, jnp.zeros((nb * MOE_ROWS, d), F32))
    y_pad = _experts(block_expert, n_used, x_pad,
                     exp_w1[l].astype(BF16), exp_w3[l].astype(BF16), exp_w2[l].astype(BF16))
    out = _combine(dest, y_pad, rec, x1, mod, ln2_g[l][None], ln2_b[l][None], seq)
    return out.reshape(batch, seq, d)
```

```python
import functools
import math

import numpy as np
import jax
import jax.numpy as jnp
from jax import lax
from jax.experimental import pallas as pl
from jax.experimental.pallas import tpu as pltpu

F32 = jnp.float32
BF16 = jnp.bfloat16

D_MODEL = 1024
HG_HEADS = 8
HG_DK = 128
HG_DV = 128
HG_CHUNK = 32
NSA_HEADS = 8
NSA_GROUPS = 2
NSA_REP = NSA_HEADS // NSA_GROUPS
NSA_DK = 128
CMP_BLOCK = 32
CMP_STRIDE = 16
SLC_BLOCK = 64
SLC_TOPK = 16
WINDOW = 512
FORCE_SCORE = 1e4
REL_BUCKETS = 32
REL_MAX_DIST = 128
N_GROUPS = 4
EXP_PER_GROUP = 8
N_EXPERTS = N_GROUPS * EXP_PER_GROUP
D_EXPERT = D_MODEL // 2
DEPTH = 1
ALPHA = (2 * DEPTH) ** 0.25

LANE = 128
QT = 128
NEG = -1e30
SCALE = NSA_DK ** -0.5
LOG2E = math.log2(math.e)
SCALE_LOG2 = SCALE * LOG2E
KEY_BLK = 512
SEL_ROWS = 16
SEL_SUB = KEY_BLK // QT
WIN_SUB = 2
CMP_SUB = 4
CMP_PAD = 120
VMEM_LIMIT = 56 * 1024 * 1024

CB_HQ, CB_HF, CB_HI, CB_HG = 0, 8, 16, 24
CB_NQ = 32
CB_KC, CB_VC, CB_KS, CB_VS, CB_KW, CB_VW = 40, 42, 44, 46, 48, 50
CB_GATE = 52
CB_MGH, CB_MGN = 56, 64
NCB = 72
MAIN_COLS = 52 * LANE
GATE_COLS = 3 * NSA_HEADS

MERGE_SPLIT = 2
MOE_ROWS = 512
ROW_DMA_UNROLL = 8


def _cparams(sem):
    return pltpu.CompilerParams(dimension_semantics=sem, vmem_limit_bytes=VMEM_LIMIT)


def _dot(a, b):
    return jnp.dot(a, b, preferred_element_type=F32)


def _dot_nt(a, b):
    return lax.dot_general(a, b, (((1,), (1,)), ((), ())), preferred_element_type=F32)


def _dot_tn(a, b):
    return lax.dot_general(a, b, (((0,), (0,)), ((), ())), preferred_element_type=F32)


def _split3(x):
    hi = x.astype(BF16)
    r = x - hi.astype(F32)
    mid = r.astype(BF16)
    lo = (r - mid.astype(F32)).astype(BF16)
    return hi, mid, lo


def _dot01(m01, x):
    hi, mid, lo = _split3(x)
    return _dot(m01, hi) + _dot(m01, mid) + _dot(m01, lo)


def _dot01_r(x, m01):
    hi, mid, lo = _split3(x)
    return _dot(hi, m01) + _dot(mid, m01) + _dot(lo, m01)


def _layer_norm(x, eps=1e-5):
    mu = jnp.mean(x, axis=-1, keepdims=True)
    xc = x - mu
    var = jnp.mean(xc * xc, axis=-1, keepdims=True)
    return xc * lax.rsqrt(var + eps)


def _adaln_kernel(c_ref, w_ref, b_ref, o_ref):
    c = c_ref[...]
    ca = c * jax.nn.sigmoid(c)
    o_ref[...] = jnp.dot(ca, w_ref[...], precision=lax.Precision.HIGHEST,
                         preferred_element_type=F32) + b_ref[...]


def _adaln(c_pad, w, b):
    rows, d = c_pad.shape
    n = w.shape[1]
    return pl.pallas_call(
        _adaln_kernel,
        grid=(n // d,),
        in_specs=[pl.BlockSpec((rows, d), lambda j: (0, 0)),
                  pl.BlockSpec((d, d), lambda j: (0, j)),
                  pl.BlockSpec((1, d), lambda j: (0, j))],
        out_specs=pl.BlockSpec((rows, d), lambda j: (0, j)),
        out_shape=jax.ShapeDtypeStruct((rows, n), F32),
        compiler_params=_cparams(("arbitrary",)),
        name="adaln",
    )(c_pad, w, b)


def _inproj_kernel(x_ref, mod_ref, w_ref, b_ref, o_ref, hn_ref, *, ncb_tile):
    @pl.when(pl.program_id(1) == 0)
    def _():
        hn = _layer_norm(x_ref[...])
        sh = mod_ref[0, 0:1, :]
        sc = mod_ref[0, 1:2, :]
        hn_ref[...] = (hn * (1.0 + sc) + sh).astype(BF16)

    res = _dot(hn_ref[...], w_ref[...]) + b_ref[...]
    for c in range(ncb_tile):
        o_ref[c] = res[:, c * LANE:(c + 1) * LANE]


def _inproj(x2, mod, w, b, seq):
    nt, d = x2.shape
    tm = min(2048, seq)
    ncb_tile = 6
    tn = ncb_tile * LANE
    return pl.pallas_call(
        functools.partial(_inproj_kernel, ncb_tile=ncb_tile),
        grid=(nt // tm, NCB // ncb_tile),
        in_specs=[pl.BlockSpec((tm, d), lambda i, j: (i, 0)),
                  pl.BlockSpec((1, 6, d), lambda i, j: (i * tm // seq, 0, 0)),
                  pl.BlockSpec((d, tn), lambda i, j: (0, j)),
                  pl.BlockSpec((1, tn), lambda i, j: (0, j))],
        out_specs=pl.BlockSpec((ncb_tile, tm, LANE), lambda i, j: (j, i, 0)),
        out_shape=jax.ShapeDtypeStruct((NCB, nt, LANE), F32),
        scratch_shapes=[pltpu.VMEM((tm, d), BF16)],
        compiler_params=_cparams(("arbitrary", "arbitrary")),
        name="inproj",
    )(x2, mod, w, b)


def _hgrn_kernel(q_ref, f_ref, v_ref, g_ref, lb_ref, nw_ref, o_ref, st_ref, *, rows):
    @pl.when(pl.program_id(1) == 0)
    def _():
        st_ref[...] = jnp.zeros_like(st_ref)

    ri = lax.broadcasted_iota(jnp.int32, (LANE, LANE), 0)
    ci = lax.broadcasted_iota(jnp.int32, (LANE, LANE), 1)
    same = (ri // HG_CHUNK) == (ci // HG_CHUNK)
    tril = same & (ci <= ri)
    cum_m = jnp.where(tril, 1.0, 0.0).astype(BF16)
    tot_m = jnp.where(same, 1.0, 0.0).astype(BF16)
    per = LANE // HG_CHUNK

    heads = range(HG_HEADS)
    hs = [slice(h * LANE, (h + 1) * LANE) for h in heads]

    def wide(ref, r0):
        return jnp.concatenate([ref[h, pl.ds(r0, LANE), :] for h in heads], axis=1)

    def body(i, carry):
        r0 = pl.multiple_of(i * LANE, LANE)
        lb = lb_ref[...]
        f = lb + (1.0 - lb) * jax.nn.sigmoid(wide(f_ref, r0))
        lf = jnp.log(f)
        k = 1.0 - f
        hi, mid, lo = _split3(lf)
        b = _dot(cum_m, hi) + _dot(cum_m, mid) + _dot(cum_m, lo)
        bl = _dot(tot_m, hi) + _dot(tot_m, mid) + _dot(tot_m, lo)
        q_in = (wide(q_ref, r0) * jnp.exp(b)).astype(BF16)
        k_in = (k * jnp.exp(-b)).astype(BF16)
        k_end = (k * jnp.exp(bl - b)).astype(BF16)
        vb = wide(v_ref, r0).astype(BF16)
        dec = [jnp.exp(bl[c * HG_CHUNK:c * HG_CHUNK + 1, :]) for c in range(per)]
        chunk = [slice(c * HG_CHUNK, (c + 1) * HG_CHUNK) for c in range(per)]

        att = [jnp.where(tril, _dot_nt(q_in[:, hs[h]], k_in[:, hs[h]]), 0.0).astype(BF16) for h in heads]
        upd = [[_dot_tn(vb[chunk[c], hs[h]], k_end[chunk[c], hs[h]]) for c in range(per)] for h in heads]
        intra = [_dot(att[h], vb[:, hs[h]]) for h in heads]

        st = [st_ref[h] for h in heads]
        inter = [[] for _ in heads]
        for c in range(per):
            for h in heads:
                inter[h].append(_dot_nt(q_in[chunk[c], hs[h]], st[h].astype(BF16)))
                st[h] = dec[c][:, hs[h]] * st[h] + upd[h][c]
        for h in heads:
            st_ref[h] = st[h]

        o = [intra[h] + jnp.concatenate(inter[h], axis=0) for h in heads]
        scale = [lax.rsqrt(jnp.mean(o[h] * o[h], axis=-1, keepdims=True) + 1e-6) for h in heads]
        on = jnp.concatenate([o[h] * scale[h] for h in heads], axis=1)
        on = on * nw_ref[...] * jax.nn.sigmoid(wide(g_ref, r0))
        o_ref[pl.ds(r0, LANE), :] = on.astype(BF16)
        return carry

    lax.fori_loop(0, rows // LANE, body, 0)


def _hgrn(proj, lb, nw, batch, seq):
    nt = proj.shape[1]
    tb = min(512, seq)
    nblk = seq // tb

    def slab(cb0):
        return pl.BlockSpec((HG_HEADS, tb, LANE), lambda b, t: (cb0 // HG_HEADS, b * nblk + t, 0))

    vec = pl.BlockSpec((1, HG_HEADS * LANE), lambda b, t: (0, 0))
    return pl.pallas_call(
        functools.partial(_hgrn_kernel, rows=tb),
        grid=(batch, nblk),
        in_specs=[slab(CB_HQ), slab(CB_HF), slab(CB_HI), slab(CB_HG), vec, vec],
        out_specs=pl.BlockSpec((tb, HG_HEADS * HG_DV), lambda b, t: (b * nblk + t, 0)),
        out_shape=jax.ShapeDtypeStruct((nt, HG_HEADS * HG_DV), BF16),
        scratch_shapes=[pltpu.VMEM((HG_HEADS, HG_DV, HG_DK), F32)],
        compiler_params=_cparams(("arbitrary", "arbitrary")),
        name="hgrn2",
    )(proj, proj, proj, proj, lb, nw)


def _compress_kernel(sub_ref, pos_ref, w1_ref, b1_ref, w2_ref, o_ref, *, ns):
    half = CMP_STRIDE * LANE
    sub = sub_ref[0]
    xa = (sub + pos_ref[0, 0:1, :]).astype(BF16)
    xb = (sub + pos_ref[0, 1:2, :]).astype(BF16)
    p0 = _dot(xa, w1_ref[0, 0:half, :])
    p1 = _dot(xb, w1_ref[0, half:2 * half, :])
    h = p0 + pltpu.roll(p1, ns - 1, axis=0) + b1_ref[0]
    a = h * jax.nn.sigmoid(h)
    out = _dot(a.astype(BF16), w2_ref[0])
    row = lax.broadcasted_iota(jnp.int32, out.shape, 0)
    out = jnp.where(row < ns - 1, out, 0.0)
    o_ref[0, 0, 0] = out


def _compress(proj16, pos, w1, b1, w2, batch, seq):
    ns = seq // CMP_STRIDE
    np_rows = ns
    width = CMP_STRIDE * LANE
    return pl.pallas_call(
        functools.partial(_compress_kernel, ns=ns),
        grid=(2, batch, NSA_GROUPS),
        in_specs=[pl.BlockSpec((1, ns, width), lambda s, b, g: (NSA_GROUPS * s + g, b, 0)),
                  pl.BlockSpec((1, 2, width), lambda s, b, g: (s, 0, 0)),
                  pl.BlockSpec((1, 2 * width, LANE), lambda s, b, g: (s, 0, 0)),
                  pl.BlockSpec((1, 1, LANE), lambda s, b, g: (s, 0, 0)),
                  pl.BlockSpec((1, LANE, LANE), lambda s, b, g: (s, 0, 0))],
        out_specs=pl.BlockSpec((1, 1, 1, np_rows, LANE), lambda s, b, g: (s, b, g, 0, 0)),
        out_shape=jax.ShapeDtypeStruct((2, batch, NSA_GROUPS, np_rows, LANE), F32),
        compiler_params=_cparams(("arbitrary", "arbitrary", "arbitrary")),
        name="nsa_compress",
    )(proj16, pos, w1, b1, w2)


def _cmp_sel_kernel(q0, q1, q2, q3, kc_ref, vc_ref, gate_ref, tt_ref, ov_ref, oc_ref, sel_ref,
                    kb_ref, vt_ref, *, np_rows):
    step = pl.program_id(2)
    cols = NSA_REP * QT
    ns = np_rows

    @pl.when(step == 0)
    def _():
        kb_ref[...] = (kc_ref[0, 0, 0] * SCALE_LOG2).astype(BF16)
        for c in range(np_rows // LANE):
            vt_ref[:, c * LANE:(c + 1) * LANE] = vc_ref[0, 0, 0, c * LANE:(c + 1) * LANE, :].T.astype(BF16)

    subs = range(CMP_SUB)
    tis = [step * CMP_SUB + u for u in subs]
    kb = kb_ref[...]
    qts = [_q_transposed((q0, q1, q2, q3), u) for u in subs]
    bias = [tt_ref[0, pl.ds(pl.multiple_of(ns + CMP_PAD - tis[u] * (QT // CMP_STRIDE), 8), np_rows), :]
            for u in subs]
    ss = [_dot(kb, qts[u]) + bias[u] for u in subs]
    ms = [jnp.max(ss[u], axis=0, keepdims=True) for u in subs]
    ps = [jnp.exp2(ss[u] - ms[u]) for u in subs]
    ls = [jnp.sum(ps[u], axis=0, keepdims=True) for u in subs]
    invs = [jnp.where(ms[u] > 0.5 * NEG, 1.0 / ls[u], 0.0) for u in subs]
    pn = [ps[u] * invs[u] for u in subs]
    vt = vt_ref[...]
    os_ = [_dot(vt, pn[u].astype(BF16)) for u in subs]
    for u in subs:
        rows = slice(u * QT, (u + 1) * QT)
        gt = jax.nn.sigmoid(gate_ref[0, rows, :])
        for r in range(NSA_REP):
            oc_ref[rows, r * LANE:(r + 1) * LANE] = gt[:, 3 * r:3 * r + 1] * os_[u][:, r * QT:(r + 1) * QT].T

    def group_sum(p):
        tot = p[:, 0:QT]
        for r in range(1, NSA_REP):
            tot = tot + p[:, r * QT:(r + 1) * QT]
        return tot

    imp = jnp.concatenate([_dot01(ov_ref[...], group_sum(pn[u])) for u in subs], axis=1)

    width = CMP_SUB * QT
    jj = lax.broadcasted_iota(jnp.int32, (LANE, width), 0)
    tok = step * width + lax.broadcasted_iota(jnp.int32, (LANE, width), 1)
    cur = tok // SLC_BLOCK
    forced = (jj == 0) | (jj == cur) | (jj == cur - 1)
    score = jnp.where(jj <= cur, jnp.where(forced, FORCE_SCORE, imp), -1.0)
    selb = jnp.full((LANE, width), NEG, F32)
    jf = jj.astype(F32)
    for _ in range(SLC_TOPK):
        mval = jnp.max(score, axis=0, keepdims=True)
        first = jnp.min(jnp.where(score == mval, jf, float(LANE)), axis=0, keepdims=True)
        pick = jf == first
        selb = jnp.where(pick, 0.0, selb)
        score = jnp.where(pick, -jnp.inf, score)
    sel_ref[0, 0] = selb


def _q_specs(nsteps, rows=QT):
    return [pl.BlockSpec((1, rows, LANE),
                         functools.partial(lambda b, g, t, r: (CB_NQ + NSA_REP * g + r, b * nsteps + t, 0), r=r))
            for r in range(NSA_REP)]


def _cmp_sel(proj, kvc, tt, ov_t, batch, seq):
    nt = proj.shape[1]
    rows = CMP_SUB * QT
    nsteps = seq // rows
    cols = NSA_REP * QT
    np_rows = kvc.shape[3]
    kv_spec = lambda s: pl.BlockSpec((1, 1, 1, np_rows, LANE), lambda b, g, t: (s, b, g, 0, 0))
    return pl.pallas_call(
        functools.partial(_cmp_sel_kernel, np_rows=np_rows),
        grid=(batch, NSA_GROUPS, nsteps),
        in_specs=_q_specs(nsteps, rows) + [
            kv_spec(0), kv_spec(1),
            pl.BlockSpec((1, rows, LANE), lambda b, g, t: (CB_GATE + g, b * nsteps + t, 0)),
            pl.BlockSpec((1, tt.shape[1], cols), lambda b, g, t: (g, 0, 0)),
            pl.BlockSpec((LANE, np_rows), lambda b, g, t: (0, 0))],
        out_specs=[pl.BlockSpec((rows, cols), lambda b, g, t: (b * nsteps + t, g)),
                   pl.BlockSpec((1, 1, LANE, rows), lambda b, g, t: (b, g, 0, t))],
        out_shape=[jax.ShapeDtypeStruct((nt, NSA_HEADS * LANE), F32),
                   jax.ShapeDtypeStruct((batch, NSA_GROUPS, LANE, seq), F32)],
        scratch_shapes=[pltpu.VMEM((np_rows, LANE), BF16), pltpu.VMEM((LANE, np_rows), BF16)],
        compiler_params=_cparams(("arbitrary", "arbitrary", "arbitrary")),
        name="nsa_cmp_select",
    )(proj, proj, proj, proj, kvc, kvc, proj, tt, ov_t)


def _q_transposed(q_refs, sub=0):
    return jnp.concatenate([r[0, sub * QT:(sub + 1) * QT, :].T for r in q_refs], axis=1).astype(BF16)


def _softmax_step(s, vt, state):
    m, l, acc = state
    m_new = jnp.maximum(m, jnp.max(s, axis=0, keepdims=True))
    alpha = jnp.exp2(m - m_new)
    p = jnp.exp2(s - m_new)
    l = alpha * l + jnp.sum(p, axis=0, keepdims=True)
    acc = alpha * acc + _dot(vt, p.astype(BF16))
    return m_new, l, acc


def _finish_t(state, gate_ref, branch, o_ref, sub=0, others=()):
    _, l, acc = state
    o = acc / jnp.where(l == 0.0, 1.0, l)
    rows = slice(sub * QT, (sub + 1) * QT)
    gt = jax.nn.sigmoid(gate_ref[0, rows, :])
    for r in range(NSA_REP):
        col = 3 * r + branch
        lanes = slice(r * LANE, (r + 1) * LANE)
        val = gt[:, col:col + 1] * o[:, r * QT:(r + 1) * QT].T
        for other in others:
            val = val + other[rows, lanes]
        o_ref[rows, lanes] = val.astype(o_ref.dtype)


def _init_state(cols):
    return (jnp.full((1, cols), NEG, F32), jnp.zeros((1, cols), F32), jnp.zeros((LANE, cols), F32))


def _win_kernel(q0, q1, q2, q3, k_ref, v_ref, gate_ref, wb_ref, o_ref, kb_ref, vt_ref, *, seq):
    step = pl.program_id(2)
    cols = NSA_REP * QT
    wt = WINDOW // QT

    @pl.when(step == 0)
    def _():
        kb_ref[0:WINDOW, :] = jnp.zeros((WINDOW, LANE), BF16)
        for i in range(wt):
            vt_ref[i] = jnp.zeros((LANE, QT), BF16)

        def fill(i, carry):
            r0 = pl.multiple_of(i * QT, QT)
            kb_ref[pl.ds(WINDOW + r0, QT), :] = (k_ref[0, pl.ds(r0, QT), :] * SCALE_LOG2).astype(BF16)
            vt_ref[wt + i] = v_ref[0, pl.ds(r0, QT), :].T.astype(BF16)
            return carry

        lax.fori_loop(0, seq // QT, fill, 0)

    rho = lax.broadcasted_iota(jnp.int32, (WINDOW + QT, cols), 0)
    subs = range(WIN_SUB)
    tis = [step * WIN_SUB + sub for sub in subs]
    qts = [_q_transposed((q0, q1, q2, q3), sub) for sub in subs]
    ss = [_dot(kb_ref[pl.ds(pl.multiple_of(tis[u] * QT, QT), WINDOW + QT), :], qts[u]) + wb_ref[0]
          for u in subs]
    ss = [jnp.where(rho >= (wt - tis[u]) * QT, ss[u], NEG) for u in subs]
    ms = [jnp.max(ss[u], axis=0, keepdims=True) for u in subs]
    ps = [jnp.exp2(ss[u] - ms[u]) for u in subs]
    ls = [jnp.sum(ps[u], axis=0, keepdims=True) for u in subs]
    pbs = [ps[u].astype(BF16) for u in subs]
    accs = [_dot(vt_ref[tis[u]], pbs[u][0:QT]) for u in subs]
    for d in range(1, wt + 1):
        accs = [accs[u] + _dot(vt_ref[tis[u] + d], pbs[u][d * QT:(d + 1) * QT]) for u in subs]
    for u in subs:
        _finish_t((ms[u], ls[u], accs[u]), gate_ref, 2, o_ref, u)


def _window(proj, wb, batch, seq):
    nt = proj.shape[1]
    rows = WIN_SUB * QT
    nqt = seq // rows
    cols = NSA_REP * QT
    slab = lambda cb0: pl.BlockSpec((1, seq, LANE), lambda b, g, t: (cb0 + g, b, 0))
    return pl.pallas_call(
        functools.partial(_win_kernel, seq=seq),
        grid=(batch, NSA_GROUPS, nqt),
        in_specs=_q_specs(nqt, rows) + [
            slab(CB_KW), slab(CB_VW),
            pl.BlockSpec((1, rows, LANE), lambda b, g, t: (CB_GATE + g, b * nqt + t, 0)),
            pl.BlockSpec((1, WINDOW + QT, cols), lambda b, g, t: (g, 0, 0))],
        out_specs=pl.BlockSpec((rows, cols), lambda b, g, t: (b * nqt + t, g)),
        out_shape=jax.ShapeDtypeStruct((nt, NSA_HEADS * LANE), F32),
        scratch_shapes=[pltpu.VMEM((seq + WINDOW, LANE), BF16),
                        pltpu.VMEM((seq // QT + WINDOW // QT, LANE, QT), BF16)],
        compiler_params=_cparams(("arbitrary", "arbitrary", "arbitrary")),
        name="nsa_window",
    )(proj, proj, proj, proj, proj, proj, proj, wb)


def _softmax_steps(ss, vt, states):
    n = range(len(ss))
    m_new = [jnp.maximum(states[u][0], jnp.max(ss[u], axis=0, keepdims=True)) for u in n]
    alpha = [jnp.exp2(states[u][0] - m_new[u]) for u in n]
    p = [jnp.exp2(ss[u] - m_new[u]) for u in n]
    l = [alpha[u] * states[u][1] + jnp.sum(p[u], axis=0, keepdims=True) for u in n]
    pv = [_dot(vt, p[u].astype(BF16)) for u in n]
    return [(m_new[u], l[u], alpha[u] * states[u][2] + pv[u]) for u in n]


def _sel_kernel(q0, q1, q2, q3, k_ref, v_ref, sel_ref, gate_ref, nd_ref, npv_ref, oc_ref, ow_ref, o_ref,
                kb_ref, vt_ref, qa_ref, m_ref, l_ref, acc_ref, *, seq):
    step = pl.program_id(2)
    cols = NSA_REP * QT
    spb = KEY_BLK // SLC_BLOCK
    subs = range(SEL_SUB)

    @pl.when(step == 0)
    def _():
        blk = lax.broadcasted_iota(jnp.int32, (KEY_BLK, LANE), 0) // SLC_BLOCK
        onehot = jnp.where(blk == lax.broadcasted_iota(jnp.int32, (KEY_BLK, LANE), 1), 1.0, 0.0).astype(BF16)

        def fill(i, carry):
            r0 = pl.multiple_of(i * KEY_BLK, KEY_BLK)
            kb_ref[pl.ds(r0, KEY_BLK), 0:LANE] = (k_ref[0, pl.ds(r0, KEY_BLK), :] * SCALE_LOG2).astype(BF16)
            kb_ref[pl.ds(r0, KEY_BLK), LANE:2 * LANE] = onehot
            for c in range(SEL_SUB):
                vt_ref[i, :, c * QT:(c + 1) * QT] = v_ref[0, pl.ds(r0 + c * QT, QT), :].T.astype(BF16)
            return carry

        lax.fori_loop(0, seq // KEY_BLK, fill, 0)
        qa_ref[...] = jnp.zeros_like(qa_ref)

    for u in subs:
        qa_ref[u, 0:LANE, :] = _q_transposed((q0, q1, q2, q3), u)

    def block(kb):
        r0 = pl.multiple_of(kb * KEY_BLK, KEY_BLK)
        return kb_ref[pl.ds(r0, KEY_BLK), :], vt_ref[kb]

    pad_rows = jnp.zeros((SEL_ROWS - spb, cols), F32)

    def scores(kb, k):
        for u in subs:
            rows = sel_ref[0, 0, pl.ds(pl.multiple_of(kb * spb, spb), spb), u * QT:(u + 1) * QT]
            rows = jnp.concatenate([rows] * NSA_REP, axis=1)
            qa_ref[u, LANE:LANE + SEL_ROWS, :] = jnp.concatenate([rows, pad_rows], axis=0).astype(BF16)
        return [_dot(k, qa_ref[u]) for u in subs]

    k, vt = block(step)
    ss = scores(step, k)
    ss = [ss[u] + nd_ref[0, u] for u in subs]
    states = _softmax_steps(ss, vt, [_init_state(cols) for _ in subs])

    prev = jnp.maximum(step - 1, 0)
    k, vt = block(prev)
    ss = scores(prev, k)
    ss[0] = ss[0] + npv_ref[0]
    exists = (step + jnp.zeros((KEY_BLK, cols), jnp.int32)) >= 1
    ss = [jnp.where(exists, ss[u], NEG) for u in subs]
    states = _softmax_steps(ss, vt, states)
    for u in subs:
        m_ref[u], l_ref[u], acc_ref[u] = states[u]

    def body(kb, carry):
        k, vt = block(kb)
        st = _softmax_steps(scores(kb, k), vt, [(m_ref[u], l_ref[u], acc_ref[u]) for u in subs])
        for u in subs:
            m_ref[u], l_ref[u], acc_ref[u] = st[u]
        return carry

    lax.fori_loop(0, jnp.maximum(step - 1, 0), body, 0)
    for u in subs:
        _finish_t((m_ref[u], l_ref[u], acc_ref[u]), gate_ref, 1, o_ref, u, others=(oc_ref, ow_ref))


def _selected(proj, sel_t, nd, npv, o_c, o_w, batch, seq):
    nt = proj.shape[1]
    rows = SEL_SUB * QT
    nsteps = seq // rows
    cols = NSA_REP * QT
    slab = lambda cb0: pl.BlockSpec((1, seq, LANE), lambda b, g, t: (cb0 + g, b, 0))
    return pl.pallas_call(
        functools.partial(_sel_kernel, seq=seq),
        grid=(batch, NSA_GROUPS, nsteps),
        in_specs=_q_specs(nsteps, rows) + [
            slab(CB_KS), slab(CB_VS),
            pl.BlockSpec((1, 1, LANE, rows), lambda b, g, t: (b, g, 0, t)),
            pl.BlockSpec((1, rows, LANE), lambda b, g, t: (CB_GATE + g, b * nsteps + t, 0)),
            pl.BlockSpec((1, SEL_SUB, KEY_BLK, cols), lambda b, g, t: (g, 0, 0, 0)),
            pl.BlockSpec((1, KEY_BLK, cols), lambda b, g, t: (g, 0, 0)),
            pl.BlockSpec((rows, cols), lambda b, g, t: (b * nsteps + t, g)),
            pl.BlockSpec((rows, cols), lambda b, g, t: (b * nsteps + t, g))],
        out_specs=pl.BlockSpec((rows, cols), lambda b, g, t: (b * nsteps + t, g)),
        out_shape=jax.ShapeDtypeStruct((nt, NSA_HEADS * LANE), BF16),
        scratch_shapes=[pltpu.VMEM((seq, 2 * LANE), BF16),
                        pltpu.VMEM((seq // KEY_BLK, LANE, KEY_BLK), BF16),
                        pltpu.VMEM((SEL_SUB, 2 * LANE, cols), BF16),
                        pltpu.VMEM((SEL_SUB, 1, cols), F32), pltpu.VMEM((SEL_SUB, 1, cols), F32),
                        pltpu.VMEM((SEL_SUB, LANE, cols), F32)],
        compiler_params=_cparams(("arbitrary", "arbitrary", "arbitrary")),
        name="nsa_selected",
    )(proj, proj, proj, proj, proj, proj, sel_t, proj, nd, npv, o_c, o_w)


def _merge_kernel(oh_ref, on_ref, mgh_ref, mgn_ref, x_ref, mod_ref,
                  wh_ref, wn_ref, wo_ref, g_ref, b_ref, wr_ref, br_ref,
                  x1_ref, h2_ref, lg_ref):
    nblk = D_MODEL // LANE
    tm = x_ref.shape[0]
    halves = [slice(s * tm // MERGE_SPLIT, (s + 1) * tm // MERGE_SPLIT) for s in range(MERGE_SPLIT)]
    a_h = [_dot(oh_ref[rs, :], wh_ref[...]) for rs in halves]
    a_n = [_dot(on_ref[rs, :], wn_ref[...]) for rs in halves]
    gh = [jnp.concatenate([mgh_ref[c, rs, :] for c in range(nblk)], axis=-1) for rs in halves]
    gn = [jnp.concatenate([mgn_ref[c, rs, :] for c in range(nblk)], axis=-1) for rs in halves]
    merged = [(jax.nn.sigmoid(gh[s]) * a_h[s] + jax.nn.sigmoid(gn[s]) * a_n[s]).astype(BF16)
              for s in range(MERGE_SPLIT)]
    y = [(1.0 + mod_ref[0, 2:3, :]) * _dot(merged[s], wo_ref[...]) for s in range(MERGE_SPLIT)]
    x1 = [_layer_norm(ALPHA * x_ref[rs, :] + y[s]) * g_ref[...] + b_ref[...] for s, rs in enumerate(halves)]
    h2 = [_layer_norm(x1[s]) * (1.0 + mod_ref[0, 4:5, :]) + mod_ref[0, 3:4, :] for s in range(MERGE_SPLIT)]
    lg = [jnp.dot(h2[s], wr_ref[...], precision=lax.Precision.HIGHEST, preferred_element_type=F32) + br_ref[...]
          for s in range(MERGE_SPLIT)]
    for s, rs in enumerate(halves):
        x1_ref[rs, :] = x1[s]
        h2_ref[rs, :] = h2[s]
        lg_ref[rs, :] = lg[s]


def _merge(o_h, o_n, proj, x2, mod, w_h, w_n, w_o, ln_g, ln_b, w_r, b_r, seq):
    nt, d = x2.shape
    tm = min(512, seq)
    nblk = d // LANE
    row = lambda w: pl.BlockSpec((tm, w), lambda i: (i, 0))
    full = lambda a: pl.BlockSpec(a.shape, lambda i: (0,) * a.ndim)
    return pl.pallas_call(
        _merge_kernel,
        grid=(nt // tm,),
        in_specs=[row(d), row(d),
                  pl.BlockSpec((nblk, tm, LANE), lambda i: (CB_MGH // nblk, i, 0)),
                  pl.BlockSpec((nblk, tm, LANE), lambda i: (CB_MGN // nblk, i, 0)),
                  row(d),
                  pl.BlockSpec((1, 6, d), lambda i: (i * tm // seq, 0, 0)),
                  full(w_h), full(w_n), full(w_o), full(ln_g), full(ln_b), full(w_r), full(b_r)],
        out_specs=[row(d), row(d), row(LANE)],
        out_shape=[jax.ShapeDtypeStruct((nt, d), F32),
                   jax.ShapeDtypeStruct((nt, d), F32),
                   jax.ShapeDtypeStruct((nt, LANE), F32)],
        compiler_params=_cparams(("arbitrary",)),
        name="merge_outproj",
    )(o_h, o_n, proj, proj, x2, mod, w_h, w_n, w_o, ln_g, ln_b, w_r, b_r)


def _route_kernel(lg_ref, rec_ref, cnt_ref, carry_ref, *, tm):
    @pl.when(pl.program_id(0) == 0)
    def _():
        carry_ref[...] = jnp.zeros_like(carry_ref)

    lg = lg_ref[...]
    lane = lax.broadcasted_iota(jnp.int32, (tm, LANE), 1).astype(F32)
    far = float(LANE)
    gmask = lane < N_GROUPS
    gl = jnp.where(gmask, lg, -jnp.inf)
    gmax = jnp.max(gl, axis=-1, keepdims=True)
    gsum = jnp.sum(jnp.where(gmask, jnp.exp(gl - gmax), 0.0), axis=-1, keepdims=True)
    grp_p = 1.0 / gsum
    gidx = jnp.min(jnp.where(gl == gmax, lane, far), axis=-1, keepdims=True)
    lo = N_GROUPS + EXP_PER_GROUP * gidx
    emask = (lane >= lo) & (lane < lo + EXP_PER_GROUP)
    el = jnp.where(emask, lg, -jnp.inf)
    m1 = jnp.max(el, axis=-1, keepdims=True)
    i1 = jnp.min(jnp.where(el == m1, lane, far), axis=-1, keepdims=True)
    el2 = jnp.where(lane == i1, -jnp.inf, el)
    m2 = jnp.max(el2, axis=-1, keepdims=True)
    i2 = jnp.min(jnp.where(emask & (lane != i1) & (el2 == m2), lane, far), axis=-1, keepdims=True)
    e = jnp.exp(m2 - m1)
    w0 = grp_p / (1.0 + e)
    w1 = grp_p * e / (1.0 + e)

    oh0 = lane == i1
    oh1 = lane == i2
    f0 = jnp.where(oh0, 1.0, 0.0)
    f1 = jnp.where(oh1, 1.0, 0.0)
    ri = lax.broadcasted_iota(jnp.int32, (tm, tm), 0)
    ci = lax.broadcasted_iota(jnp.int32, (tm, tm), 1)
    before = jnp.where(ci < ri, 1.0, 0.0).astype(BF16)
    cum0 = _dot(before, f0.astype(BF16))
    cum1 = _dot(before, f1.astype(BF16))
    tot0 = jnp.sum(f0, axis=0, keepdims=True)
    tot1 = jnp.sum(f1, axis=0, keepdims=True)
    carry = carry_ref[...]
    rank0 = jnp.sum(jnp.where(oh0, carry + cum0, 0.0), axis=-1, keepdims=True)
    rank1 = jnp.sum(jnp.where(oh1, carry + tot0 + cum1, 0.0), axis=-1, keepdims=True)
    carry = carry + tot0 + tot1
    carry_ref[...] = carry
    cnt_ref[...] = carry

    rec = jnp.where(lane == 0, i1 - N_GROUPS, 0.0)
    rec = jnp.where(lane == 1, i2 - N_GROUPS, rec)
    rec = jnp.where(lane == 2, w0, rec)
    rec = jnp.where(lane == 3, w1, rec)
    rec = jnp.where(lane == 4, rank0, rec)
    rec = jnp.where(lane == 5, rank1, rec)
    rec_ref[...] = rec


def _route(logits):
    nt = logits.shape[0]
    tm = min(512, nt)
    return pl.pallas_call(
        functools.partial(_route_kernel, tm=tm),
        grid=(nt // tm,),
        in_specs=[pl.BlockSpec((tm, LANE), lambda i: (i, 0))],
        out_specs=[pl.BlockSpec((tm, LANE), lambda i: (i, 0)),
                   pl.BlockSpec((1, LANE), lambda i: (0, 0))],
        out_shape=[jax.ShapeDtypeStruct((nt, LANE), F32),
                   jax.ShapeDtypeStruct((1, LANE), F32)],
        scratch_shapes=[pltpu.VMEM((1, LANE), F32)],
        compiler_params=_cparams(("arbitrary",)),
        name="moe_route",
    )(logits)


def _row_copy(src, dst, sem):
    return pltpu.make_async_copy(src, dst, sem)


def _dispatch_kernel(dest_ref, h_ref, zero_ref, xp_ref, sem, *, tm):
    del zero_ref
    base = pl.program_id(0) * tm

    def issue(r, carry):
        for k in range(2):
            d = dest_ref[2 * (base + r) + k]
            _row_copy(h_ref.at[pl.ds(r, 1), :], xp_ref.at[pl.ds(d, 1), :], sem).start()
        return carry

    lax.fori_loop(0, tm, issue, 0, unroll=ROW_DMA_UNROLL)
    for _ in range(2):
        _row_copy(h_ref, xp_ref.at[pl.ds(0, tm), :], sem).wait()


def _dispatch(dest, h2, x_pad0):
    nt, d = h2.shape
    tm = min(256, nt)
    return pl.pallas_call(
        functools.partial(_dispatch_kernel, tm=tm),
        grid_spec=pltpu.PrefetchScalarGridSpec(
            num_scalar_prefetch=1,
            grid=(nt // tm,),
            in_specs=[pl.BlockSpec((tm, d), lambda i, dest: (i, 0)),
                      pl.BlockSpec(memory_space=pl.ANY)],
            out_specs=pl.BlockSpec(memory_space=pl.ANY),
            scratch_shapes=[pltpu.SemaphoreType.DMA(())]),
        out_shape=jax.ShapeDtypeStruct(x_pad0.shape, x_pad0.dtype),
        input_output_aliases={2: 0},
        compiler_params=_cparams(("arbitrary",)),
        name="moe_dispatch",
    )(dest, h2, x_pad0)


def _expert_kernel(be_ref, nu_ref, x_ref, w1_ref, w3_ref, w2_ref, y_ref):
    del be_ref
    i = pl.program_id(0)

    @pl.when(i < nu_ref[0])
    def _():
        xb = x_ref[...].astype(BF16)
        a = _dot(xb, w1_ref[0])
        b = _dot(xb, w3_ref[0])
        hmid = (a * jax.nn.sigmoid(a) * b).astype(BF16)
        y_ref[...] = _dot(hmid, w2_ref[0])

    @pl.when(i >= nu_ref[0])
    def _():
        y_ref[...] = jnp.zeros_like(y_ref)


def _experts(block_expert, n_used, x_pad, w1, w3, w2):
    npad, d = x_pad.shape
    de = w1.shape[2]
    nb = npad // MOE_ROWS
    return pl.pallas_call(
        _expert_kernel,
        grid_spec=pltpu.PrefetchScalarGridSpec(
            num_scalar_prefetch=2,
            grid=(nb,),
            in_specs=[pl.BlockSpec((MOE_ROWS, d), lambda i, be, nu: (i, 0)),
                      pl.BlockSpec((1, d, de), lambda i, be, nu: (be[i], 0, 0)),
                      pl.BlockSpec((1, d, de), lambda i, be, nu: (be[i], 0, 0)),
                      pl.BlockSpec((1, de, d), lambda i, be, nu: (be[i], 0, 0))],
            out_specs=pl.BlockSpec((MOE_ROWS, d), lambda i, be, nu: (i, 0))),
        out_shape=jax.ShapeDtypeStruct((npad, d), F32),
        compiler_params=_cparams(("arbitrary",)),
        name="moe_experts",
    )(block_expert, n_used, x_pad, w1, w3, w2)


def _combine_kernel(dest_ref, yp_ref, rec_ref, x1_ref, mod_ref, g_ref, b_ref, o_ref, buf_ref, sem, *, tm):
    base = pl.program_id(0) * tm

    def issue(r, carry):
        for k in range(2):
            d = dest_ref[2 * (base + r) + k]
            _row_copy(yp_ref.at[pl.ds(d, 1), :], buf_ref.at[k, pl.ds(r, 1), :], sem).start()
        return carry

    lax.fori_loop(0, tm, issue, 0, unroll=ROW_DMA_UNROLL)
    for k in range(2):
        _row_copy(yp_ref.at[pl.ds(0, tm), :], buf_ref.at[k], sem).wait()

    rec = rec_ref[...]
    y = rec[:, 2:3] * buf_ref[0] + rec[:, 3:4] * buf_ref[1]
    y = (1.0 + mod_ref[0, 5:6, :]) * y
    o_ref[...] = _layer_norm(ALPHA * x1_ref[...] + y) * g_ref[...] + b_ref[...]


def _combine(dest, y_pad, rec, x1, mod, ln_g, ln_b, seq):
    nt, d = x1.shape
    tm = min(256, seq)
    return pl.pallas_call(
        functools.partial(_combine_kernel, tm=tm),
        grid_spec=pltpu.PrefetchScalarGridSpec(
            num_scalar_prefetch=1,
            grid=(nt // tm,),
            in_specs=[pl.BlockSpec(memory_space=pl.ANY),
                      pl.BlockSpec((tm, LANE), lambda i, dest: (i, 0)),
                      pl.BlockSpec((tm, d), lambda i, dest: (i, 0)),
                      pl.BlockSpec((1, 6, d), lambda i, dest: (i * tm // seq, 0, 0)),
                      pl.BlockSpec((1, d), lambda i, dest: (0, 0)),
                      pl.BlockSpec((1, d), lambda i, dest: (0, 0))],
            out_specs=pl.BlockSpec((tm, d), lambda i, dest: (i, 0)),
            scratch_shapes=[pltpu.VMEM((2, tm, d), F32), pltpu.SemaphoreType.DMA(())]),
        out_shape=jax.ShapeDtypeStruct((nt, d), F32),
        compiler_params=_cparams(("arbitrary",)),
        name="moe_combine",
    )(dest, y_pad, rec, x1, mod, ln_g, ln_b)


def _rel_bucket(dist):
    n = jnp.maximum(dist, 0)
    max_exact = REL_BUCKETS // 2
    nf = jnp.maximum(n, 1).astype(F32)
    large = max_exact + (jnp.log(nf / max_exact) / math.log(REL_MAX_DIST / max_exact)
                         * (REL_BUCKETS - max_exact)).astype(jnp.int32)
    large = jnp.minimum(large, REL_BUCKETS - 1)
    return jnp.where(n < max_exact, n, large)


def _bias_tables(rel_bias, seq):
    bucket_onehot = (_rel_bucket(jnp.arange(LANE))[:, None] == jnp.arange(REL_BUCKETS)).astype(F32)
    tab_d = jnp.einsum('db,hb->hd', bucket_onehot, rel_bias,
                       precision=lax.Precision.HIGHEST)
    tok = np.arange(QT)[None, :]
    key = np.arange(LANE)[:, None]
    far = tab_d[:, LANE - 1]
    cols = NSA_REP * QT

    def transposed(dist):
        idx = jnp.asarray(np.clip(dist, 0, LANE - 1).astype(np.int32))
        onehot = (idx[..., None] == jnp.arange(LANE, dtype=jnp.int32)).astype(F32)
        t = jnp.einsum('ijd,hd->hij', onehot, tab_d, precision=lax.Precision.HIGHEST)
        t = (t - far[:, None, None]) * LOG2E
        t = t.reshape(NSA_GROUPS, NSA_REP, LANE, QT).transpose(0, 2, 1, 3)
        return t.reshape(NSA_GROUPS, LANE, cols)

    t0t = transposed(tok - key)
    t1t = transposed(tok - key + QT)

    ns = seq // CMP_STRIDE
    d_c = tok - CMP_STRIDE * key + (CMP_STRIDE * CMP_PAD - (CMP_BLOCK - 1))
    seen = np.tile(d_c >= 0, (1, NSA_REP))
    recent = jnp.where(seen[None], transposed(d_c), NEG)
    tt = jnp.concatenate([jnp.zeros((NSA_GROUPS, ns, cols), F32), recent,
                          jnp.full((NSA_GROUPS, ns, cols), NEG, F32)], axis=1)

    rho = np.arange(WINDOW + QT)[:, None]
    tok_w = np.tile(np.arange(QT), NSA_REP)[None, :]
    band = (rho > tok_w) & (rho <= tok_w + WINDOW)
    rows = jnp.concatenate([jnp.zeros((NSA_GROUPS, WINDOW - QT, NSA_REP * QT), F32), t1t, t0t], axis=1)
    wb = jnp.where(band[None], rows, NEG)

    zeros = lambda n: jnp.zeros((NSA_GROUPS, n * QT, cols), F32)
    negs = lambda n: jnp.full((NSA_GROUPS, n * QT, cols), NEG, F32)
    diag = jnp.where(np.tile(tok >= key, (1, NSA_REP))[None], t0t, NEG)
    nd = jnp.stack([jnp.concatenate(([zeros(u - 1), t1t] if u else []) + [diag, negs(SEL_SUB - 1 - u)], axis=1)
                    for u in range(SEL_SUB)], axis=1)
    npv = jnp.concatenate([zeros(SEL_SUB - 1), t1t], axis=1)
    return nd, npv, wb, tt


def _overlap_matrix(seq):
    ns = seq // CMP_STRIDE
    nslc = seq // SLC_BLOCK
    ov = np.zeros((LANE, ns), np.float32)
    cs = np.arange(ns - 1) * CMP_STRIDE
    ss = np.arange(nslc) * SLC_BLOCK
    ov[:nslc, :ns - 1] = ((cs[None, :] < ss[:, None] + SLC_BLOCK) & (cs[None, :] + CMP_BLOCK > ss[:, None]))
    return jnp.asarray(ov, BF16)


def _reorder_cols(a):
    lead = a.shape[:-1]
    gate = a[..., MAIN_COLS:MAIN_COLS + GATE_COLS]
    per = GATE_COLS // NSA_GROUPS
    gate_blocks = []
    for g in range(NSA_GROUPS):
        gate_blocks.append(gate[..., g * per:(g + 1) * per])
        gate_blocks.append(jnp.zeros(lead + (LANE - per,), a.dtype))
    pad = jnp.zeros(lead + ((CB_MGH - CB_GATE - NSA_GROUPS) * LANE,), a.dtype)
    return jnp.concatenate([a[..., :MAIN_COLS]] + gate_blocks + [pad, a[..., MAIN_COLS + GATE_COLS:]], axis=-1)


def kernel(x, c, ada_w, ada_b, w_in, b_in, hg_lb_logits, hg_norm_w, cmp_pos_k, cmp_w1_k, cmp_b1_k, cmp_w2_k, cmp_pos_v, cmp_w1_v, cmp_b1_v, cmp_w2_v, rel_bias, w_br_hg, w_br_nsa, w_out, ln1_g, ln1_b, router_grp_w, router_grp_b, router_exp_w, router_exp_b, exp_w1, exp_w3, exp_w2, ln2_g, ln2_b):
    batch, seq, d = x.shape
    nt = batch * seq
    assert d == D_MODEL and seq % 1024 == 0 and seq // SLC_BLOCK <= LANE
    l = 0
    x2 = x.reshape(nt, d)

    c_pad = jnp.zeros((8, d), F32).at[:batch].set(c)
    mod = _adaln(c_pad, ada_w[l], ada_b[l][None])[:batch].reshape(batch, 6, d)

    proj = _inproj(x2, mod, _reorder_cols(w_in[l]).astype(BF16), _reorder_cols(b_in[l])[None], seq)

    lb_all = jnp.cumsum(jax.nn.softmax(hg_lb_logits.astype(F32), axis=0), axis=0)
    o_h = _hgrn(proj, lb_all[l][None], hg_norm_w[l][None], batch, seq)

    half = CMP_STRIDE * LANE
    pos = jnp.stack([cmp_pos_k[l].reshape(2, half), cmp_pos_v[l].reshape(2, half)])
    kv_cols = proj[CB_KC:CB_KC + 2 * NSA_GROUPS].reshape(2 * NSA_GROUPS, nt // CMP_STRIDE, half)
    kvc = _compress(kv_cols, pos,
                    jnp.stack([cmp_w1_k[l], cmp_w1_v[l]]).astype(BF16),
                    jnp.stack([cmp_b1_k[l], cmp_b1_v[l]])[:, None, :],
                    jnp.stack([cmp_w2_k[l], cmp_w2_v[l]]).astype(BF16), batch, seq)

    nd, npv, wb, tt = _bias_tables(rel_bias, seq)
    o_c, sel_t = _cmp_sel(proj, kvc, tt, _overlap_matrix(seq), batch, seq)
    o_w = _window(proj, wb, batch, seq)
    o_n = _selected(proj, sel_t, nd, npv, o_c, o_w, batch, seq)

    w_r = jnp.zeros((d, LANE), F32).at[:, :N_GROUPS].set(router_grp_w[l])
    w_r = w_r.at[:, N_GROUPS:N_GROUPS + N_EXPERTS].set(router_exp_w[l])
    b_r = jnp.zeros((1, LANE), F32).at[0, :N_GROUPS].set(router_grp_b[l])
    b_r = b_r.at[0, N_GROUPS:N_GROUPS + N_EXPERTS].set(router_exp_b[l])
    x1, h2, logits = _merge(o_h, o_n, proj, x2, mod,
                            w_br_hg[l].astype(BF16), w_br_nsa[l].astype(BF16), w_out[l].astype(BF16),
                            ln1_g[l][None], ln1_b[l][None], w_r, b_r, seq)

    rec, cnt = _route(logits)
    counts = cnt[0, N_GROUPS:N_GROUPS + N_EXPERTS].astype(jnp.int32)
    padded = (counts + MOE_ROWS - 1) // MOE_ROWS * MOE_ROWS
    pend = jnp.cumsum(padded)
    pstart = pend - padded
    n_assign = 2 * nt
    nb = n_assign // MOE_ROWS + N_EXPERTS
    expert = rec[:, 0:2].astype(jnp.int32)
    dest = (pstart[expert] + rec[:, 4:6].astype(jnp.int32)).reshape(-1)
    block_start = jnp.arange(nb, dtype=jnp.int32) * MOE_ROWS
    block_expert = jnp.minimum(jnp.sum(pend[None, :] <= block_start[:, None], axis=1),
                               N_EXPERTS - 1).astype(jnp.int32)
    n_used = (pend[-1:] // MOE_ROWS).astype(jnp.int32)

    x_pad = _dispatch(dest, h2, jnp.zeros((nb * MOE_ROWS, d), F32))
    y_pad = _experts(block_expert, n_used, x_pad,
                     exp_w1[l].astype(BF16), exp_w3[l].astype(BF16), exp_w2[l].astype(BF16))
    out = _combine(dest, y_pad, rec, x1, mod, ln2_g[l][None], ln2_b[l][None], seq)
    return out.reshape(batch, seq, d)
```

```python
import functools
import math

import numpy as np
import jax
import jax.numpy as jnp
from jax import lax
from jax.experimental import pallas as pl
from jax.experimental.pallas import tpu as pltpu

F32 = jnp.float32
BF16 = jnp.bfloat16

D_MODEL = 1024
HG_HEADS = 8
HG_DK = 128
HG_DV = 128
HG_CHUNK = 32
NSA_HEADS = 8
NSA_GROUPS = 2
NSA_REP = NSA_HEADS // NSA_GROUPS
NSA_DK = 128
CMP_BLOCK = 32
CMP_STRIDE = 16
SLC_BLOCK = 64
SLC_TOPK = 16
WINDOW = 512
FORCE_SCORE = 1e4
REL_BUCKETS = 32
REL_MAX_DIST = 128
N_GROUPS = 4
EXP_PER_GROUP = 8
N_EXPERTS = N_GROUPS * EXP_PER_GROUP
D_EXPERT = D_MODEL // 2
DEPTH = 1
ALPHA = (2 * DEPTH) ** 0.25

LANE = 128
QT = 128
NEG = -1e30
SCALE = NSA_DK ** -0.5
LOG2E = math.log2(math.e)
SCALE_LOG2 = SCALE * LOG2E
KEY_BLK = 512
SEL_ROWS = 16
SEL_SUB = KEY_BLK // QT
WIN_SUB = 2
CMP_SUB = 4
CMP_PAD = 120
VMEM_LIMIT = 56 * 1024 * 1024

CB_HQ, CB_HF, CB_HI, CB_HG = 0, 8, 16, 24
CB_NQ = 32
CB_KC, CB_VC, CB_KS, CB_VS, CB_KW, CB_VW = 40, 42, 44, 46, 48, 50
CB_GATE = 52
CB_MGH, CB_MGN = 56, 64
NCB = 72
MAIN_COLS = 52 * LANE
GATE_COLS = 3 * NSA_HEADS

MERGE_SPLIT = 2
ROW_TILES = D_MODEL // LANE
MOE_ROWS = 512
ROW_DMA_UNROLL = 8


def _cparams(sem):
    return pltpu.CompilerParams(dimension_semantics=sem, vmem_limit_bytes=VMEM_LIMIT)


def _dot(a, b):
    return jnp.dot(a, b, preferred_element_type=F32)


def _dot_nt(a, b):
    return lax.dot_general(a, b, (((1,), (1,)), ((), ())), preferred_element_type=F32)


def _dot_tn(a, b):
    return lax.dot_general(a, b, (((0,), (0,)), ((), ())), preferred_element_type=F32)


def _split3(x):
    hi = x.astype(BF16)
    r = x - hi.astype(F32)
    mid = r.astype(BF16)
    lo = (r - mid.astype(F32)).astype(BF16)
    return hi, mid, lo


def _dot01(m01, x):
    hi, mid, lo = _split3(x)
    return _dot(m01, hi) + _dot(m01, mid) + _dot(m01, lo)


def _dot01_r(x, m01):
    hi, mid, lo = _split3(x)
    return _dot(hi, m01) + _dot(mid, m01) + _dot(lo, m01)


def _layer_norm(x, eps=1e-5):
    mu = jnp.mean(x, axis=-1, keepdims=True)
    xc = x - mu
    var = jnp.mean(xc * xc, axis=-1, keepdims=True)
    return xc * lax.rsqrt(var + eps)


def _adaln_kernel(c_ref, w_ref, b_ref, o_ref):
    c = c_ref[...]
    ca = c * jax.nn.sigmoid(c)
    o_ref[...] = jnp.dot(ca, w_ref[...], precision=lax.Precision.HIGHEST,
                         preferred_element_type=F32) + b_ref[...]


def _adaln(c_pad, w, b):
    rows, d = c_pad.shape
    n = w.shape[1]
    return pl.pallas_call(
        _adaln_kernel,
        grid=(n // d,),
        in_specs=[pl.BlockSpec((rows, d), lambda j: (0, 0)),
                  pl.BlockSpec((d, d), lambda j: (0, j)),
                  pl.BlockSpec((1, d), lambda j: (0, j))],
        out_specs=pl.BlockSpec((rows, d), lambda j: (0, j)),
        out_shape=jax.ShapeDtypeStruct((rows, n), F32),
        compiler_params=_cparams(("arbitrary",)),
        name="adaln",
    )(c_pad, w, b)


def _inproj_kernel(x_ref, mod_ref, w_ref, b_ref, o_ref, hn_ref, *, ncb_tile):
    @pl.when(pl.program_id(1) == 0)
    def _():
        hn = _layer_norm(x_ref[...])
        sh = mod_ref[0, 0:1, :]
        sc = mod_ref[0, 1:2, :]
        hn_ref[...] = (hn * (1.0 + sc) + sh).astype(BF16)

    res = _dot(hn_ref[...], w_ref[...]) + b_ref[...]
    for c in range(ncb_tile):
        o_ref[c] = res[:, c * LANE:(c + 1) * LANE]


def _inproj(x2, mod, w, b, seq):
    nt, d = x2.shape
    tm = min(2048, seq)
    ncb_tile = 6
    tn = ncb_tile * LANE
    return pl.pallas_call(
        functools.partial(_inproj_kernel, ncb_tile=ncb_tile),
        grid=(nt // tm, NCB // ncb_tile),
        in_specs=[pl.BlockSpec((tm, d), lambda i, j: (i, 0)),
                  pl.BlockSpec((1, 6, d), lambda i, j: (i * tm // seq, 0, 0)),
                  pl.BlockSpec((d, tn), lambda i, j: (0, j)),
                  pl.BlockSpec((1, tn), lambda i, j: (0, j))],
        out_specs=pl.BlockSpec((ncb_tile, tm, LANE), lambda i, j: (j, i, 0)),
        out_shape=jax.ShapeDtypeStruct((NCB, nt, LANE), F32),
        scratch_shapes=[pltpu.VMEM((tm, d), BF16)],
        compiler_params=_cparams(("arbitrary", "arbitrary")),
        name="inproj",
    )(x2, mod, w, b)


def _hgrn_kernel(q_ref, f_ref, v_ref, g_ref, lb_ref, nw_ref, o_ref, st_ref, *, rows):
    @pl.when(pl.program_id(1) == 0)
    def _():
        st_ref[...] = jnp.zeros_like(st_ref)

    ri = lax.broadcasted_iota(jnp.int32, (LANE, LANE), 0)
    ci = lax.broadcasted_iota(jnp.int32, (LANE, LANE), 1)
    same = (ri // HG_CHUNK) == (ci // HG_CHUNK)
    tril = same & (ci <= ri)
    cum_m = jnp.where(tril, 1.0, 0.0).astype(BF16)
    tot_m = jnp.where(same, 1.0, 0.0).astype(BF16)
    per = LANE // HG_CHUNK

    heads = range(HG_HEADS)
    hs = [slice(h * LANE, (h + 1) * LANE) for h in heads]

    def wide(ref, r0):
        return jnp.concatenate([ref[h, pl.ds(r0, LANE), :] for h in heads], axis=1)

    def body(i, carry):
        r0 = pl.multiple_of(i * LANE, LANE)
        lb = lb_ref[...]
        f = lb + (1.0 - lb) * jax.nn.sigmoid(wide(f_ref, r0))
        lf = jnp.log(f)
        k = 1.0 - f
        hi, mid, lo = _split3(lf)
        b = _dot(cum_m, hi) + _dot(cum_m, mid) + _dot(cum_m, lo)
        bl = _dot(tot_m, hi) + _dot(tot_m, mid) + _dot(tot_m, lo)
        q_in = (wide(q_ref, r0) * jnp.exp(b)).astype(BF16)
        k_in = (k * jnp.exp(-b)).astype(BF16)
        k_end = (k * jnp.exp(bl - b)).astype(BF16)
        vb = wide(v_ref, r0).astype(BF16)
        dec = [jnp.exp(bl[c * HG_CHUNK:c * HG_CHUNK + 1, :]) for c in range(per)]
        chunk = [slice(c * HG_CHUNK, (c + 1) * HG_CHUNK) for c in range(per)]

        att = [jnp.where(tril, _dot_nt(q_in[:, hs[h]], k_in[:, hs[h]]), 0.0).astype(BF16) for h in heads]
        upd = [[_dot_tn(vb[chunk[c], hs[h]], k_end[chunk[c], hs[h]]) for c in range(per)] for h in heads]
        intra = [_dot(att[h], vb[:, hs[h]]) for h in heads]

        st = [st_ref[h] for h in heads]
        inter = [[] for _ in heads]
        for c in range(per):
            for h in heads:
                inter[h].append(_dot_nt(q_in[chunk[c], hs[h]], st[h].astype(BF16)))
                st[h] = dec[c][:, hs[h]] * st[h] + upd[h][c]
        for h in heads:
            st_ref[h] = st[h]

        o = [intra[h] + jnp.concatenate(inter[h], axis=0) for h in heads]
        scale = [lax.rsqrt(jnp.mean(o[h] * o[h], axis=-1, keepdims=True) + 1e-6) for h in heads]
        on = jnp.concatenate([o[h] * scale[h] for h in heads], axis=1)
        on = on * nw_ref[...] * jax.nn.sigmoid(wide(g_ref, r0))
        o_ref[pl.ds(r0, LANE), :] = on.astype(BF16)
        return carry

    lax.fori_loop(0, rows // LANE, body, 0)


def _hgrn(proj, lb, nw, batch, seq):
    nt = proj.shape[1]
    tb = min(512, seq)
    nblk = seq // tb

    def slab(cb0):
        return pl.BlockSpec((HG_HEADS, tb, LANE), lambda b, t: (cb0 // HG_HEADS, b * nblk + t, 0))

    vec = pl.BlockSpec((1, HG_HEADS * LANE), lambda b, t: (0, 0))
    return pl.pallas_call(
        functools.partial(_hgrn_kernel, rows=tb),
        grid=(batch, nblk),
        in_specs=[slab(CB_HQ), slab(CB_HF), slab(CB_HI), slab(CB_HG), vec, vec],
        out_specs=pl.BlockSpec((tb, HG_HEADS * HG_DV), lambda b, t: (b * nblk + t, 0)),
        out_shape=jax.ShapeDtypeStruct((nt, HG_HEADS * HG_DV), BF16),
        scratch_shapes=[pltpu.VMEM((HG_HEADS, HG_DV, HG_DK), F32)],
        compiler_params=_cparams(("arbitrary", "arbitrary")),
        name="hgrn2",
    )(proj, proj, proj, proj, lb, nw)


def _compress_kernel(x_ref, pos_ref, w1_ref, b1_ref, w2_ref, o_ref, *, ns):
    p0 = jnp.zeros((ns, LANE), F32)
    p1 = jnp.zeros((ns, LANE), F32)
    for j in range(CMP_STRIDE):
        tok = x_ref[0, pl.ds(j, ns, stride=CMP_STRIDE), :]
        rows = slice(j * LANE, (j + 1) * LANE)
        late = slice((CMP_STRIDE + j) * LANE, (CMP_STRIDE + j + 1) * LANE)
        p0 = p0 + _dot((tok + pos_ref[0, j:j + 1, :]).astype(BF16), w1_ref[0, rows, :])
        p1 = p1 + _dot((tok + pos_ref[0, CMP_STRIDE + j:CMP_STRIDE + j + 1, :]).astype(BF16), w1_ref[0, late, :])
    h = p0 + pltpu.roll(p1, ns - 1, axis=0) + b1_ref[0]
    a = h * jax.nn.sigmoid(h)
    out = _dot(a.astype(BF16), w2_ref[0])
    row = lax.broadcasted_iota(jnp.int32, out.shape, 0)
    out = jnp.where(row < ns - 1, out, 0.0)
    o_ref[0, 0, 0] = out


def _compress(proj, pos, w1, b1, w2, batch, seq):
    ns = seq // CMP_STRIDE
    np_rows = ns
    width = CMP_STRIDE * LANE
    return pl.pallas_call(
        functools.partial(_compress_kernel, ns=ns),
        grid=(2, batch, NSA_GROUPS),
        in_specs=[pl.BlockSpec((1, seq, LANE), lambda s, b, g: (CB_KC + NSA_GROUPS * s + g, b, 0)),
                  pl.BlockSpec((1, CMP_BLOCK, LANE), lambda s, b, g: (s, 0, 0)),
                  pl.BlockSpec((1, 2 * width, LANE), lambda s, b, g: (s, 0, 0)),
                  pl.BlockSpec((1, 1, LANE), lambda s, b, g: (s, 0, 0)),
                  pl.BlockSpec((1, LANE, LANE), lambda s, b, g: (s, 0, 0))],
        out_specs=pl.BlockSpec((1, 1, 1, np_rows, LANE), lambda s, b, g: (s, b, g, 0, 0)),
        out_shape=jax.ShapeDtypeStruct((2, batch, NSA_GROUPS, np_rows, LANE), F32),
        compiler_params=_cparams(("arbitrary", "arbitrary", "arbitrary")),
        name="nsa_compress",
    )(proj, pos, w1, b1, w2)


def _cmp_sel_kernel(q0, q1, q2, q3, kc_ref, vc_ref, gate_ref, tt_ref, ov_ref, oc_ref, sel_ref,
                    kb_ref, vt_ref, *, np_rows):
    step = pl.program_id(2)
    cols = NSA_REP * QT
    ns = np_rows

    @pl.when(step == 0)
    def _():
        kb_ref[...] = (kc_ref[0, 0, 0] * SCALE_LOG2).astype(BF16)
        for c in range(np_rows // LANE):
            vt_ref[:, c * LANE:(c + 1) * LANE] = vc_ref[0, 0, 0, c * LANE:(c + 1) * LANE, :].T.astype(BF16)

    subs = range(CMP_SUB)
    tis = [step * CMP_SUB + u for u in subs]
    kb = kb_ref[...]
    qts = [_q_transposed((q0, q1, q2, q3), u) for u in subs]
    bias = [tt_ref[0, pl.ds(pl.multiple_of(ns + CMP_PAD - tis[u] * (QT // CMP_STRIDE), 8), np_rows), :]
            for u in subs]
    ss = [_dot(kb, qts[u]) + bias[u] for u in subs]
    ms = [jnp.max(ss[u], axis=0, keepdims=True) for u in subs]
    ps = [jnp.exp2(ss[u] - ms[u]) for u in subs]
    ls = [jnp.sum(ps[u], axis=0, keepdims=True) for u in subs]
    invs = [jnp.where(ms[u] > 0.5 * NEG, 1.0 / ls[u], 0.0) for u in subs]
    pn = [ps[u] * invs[u] for u in subs]
    vt = vt_ref[...]
    os_ = [_dot(vt, pn[u].astype(BF16)) for u in subs]
    for u in subs:
        rows = slice(u * QT, (u + 1) * QT)
        gt = jax.nn.sigmoid(gate_ref[0, rows, :])
        for r in range(NSA_REP):
            oc_ref[rows, r * LANE:(r + 1) * LANE] = gt[:, 3 * r:3 * r + 1] * os_[u][:, r * QT:(r + 1) * QT].T

    def group_sum(p):
        tot = p[:, 0:QT]
        for r in range(1, NSA_REP):
            tot = tot + p[:, r * QT:(r + 1) * QT]
        return tot

    imp = jnp.concatenate([_dot01(ov_ref[...], group_sum(pn[u])) for u in subs], axis=1)

    width = CMP_SUB * QT
    jj = lax.broadcasted_iota(jnp.int32, (LANE, width), 0)
    tok = step * width + lax.broadcasted_iota(jnp.int32, (LANE, width), 1)
    cur = tok // SLC_BLOCK
    forced = (jj == 0) | (jj == cur) | (jj == cur - 1)
    score = jnp.where(jj <= cur, jnp.where(forced, FORCE_SCORE, imp), -1.0)
    selb = jnp.full((LANE, width), NEG, F32)
    jf = jj.astype(F32)
    for _ in range(SLC_TOPK):
        mval = jnp.max(score, axis=0, keepdims=True)
        first = jnp.min(jnp.where(score == mval, jf, float(LANE)), axis=0, keepdims=True)
        pick = jf == first
        selb = jnp.where(pick, 0.0, selb)
        score = jnp.where(pick, -jnp.inf, score)
    sel_ref[0, 0] = selb


def _q_specs(nsteps, rows=QT):
    return [pl.BlockSpec((1, rows, LANE),
                         functools.partial(lambda b, g, t, r: (CB_NQ + NSA_REP * g + r, b * nsteps + t, 0), r=r))
            for r in range(NSA_REP)]


def _cmp_sel(proj, kvc, tt, ov_t, batch, seq):
    nt = proj.shape[1]
    rows = CMP_SUB * QT
    nsteps = seq // rows
    cols = NSA_REP * QT
    np_rows = kvc.shape[3]
    kv_spec = lambda s: pl.BlockSpec((1, 1, 1, np_rows, LANE), lambda b, g, t: (s, b, g, 0, 0))
    return pl.pallas_call(
        functools.partial(_cmp_sel_kernel, np_rows=np_rows),
        grid=(batch, NSA_GROUPS, nsteps),
        in_specs=_q_specs(nsteps, rows) + [
            kv_spec(0), kv_spec(1),
            pl.BlockSpec((1, rows, LANE), lambda b, g, t: (CB_GATE + g, b * nsteps + t, 0)),
            pl.BlockSpec((1, tt.shape[1], cols), lambda b, g, t: (g, 0, 0)),
            pl.BlockSpec((LANE, np_rows), lambda b, g, t: (0, 0))],
        out_specs=[pl.BlockSpec((rows, cols), lambda b, g, t: (b * nsteps + t, g)),
                   pl.BlockSpec((1, 1, LANE, rows), lambda b, g, t: (b, g, 0, t))],
        out_shape=[jax.ShapeDtypeStruct((nt, NSA_HEADS * LANE), F32),
                   jax.ShapeDtypeStruct((batch, NSA_GROUPS, LANE, seq), F32)],
        scratch_shapes=[pltpu.VMEM((np_rows, LANE), BF16), pltpu.VMEM((LANE, np_rows), BF16)],
        compiler_params=_cparams(("arbitrary", "arbitrary", "arbitrary")),
        name="nsa_cmp_select",
    )(proj, proj, proj, proj, kvc, kvc, proj, tt, ov_t)


def _q_transposed(q_refs, sub=0):
    return jnp.concatenate([r[0, sub * QT:(sub + 1) * QT, :].T for r in q_refs], axis=1).astype(BF16)


def _softmax_step(s, vt, state):
    m, l, acc = state
    m_new = jnp.maximum(m, jnp.max(s, axis=0, keepdims=True))
    alpha = jnp.exp2(m - m_new)
    p = jnp.exp2(s - m_new)
    l = alpha * l + jnp.sum(p, axis=0, keepdims=True)
    acc = alpha * acc + _dot(vt, p.astype(BF16))
    return m_new, l, acc


def _finish_t(state, gate_ref, branch, o_ref, sub=0, others=()):
    _, l, acc = state
    o = acc / jnp.where(l == 0.0, 1.0, l)
    rows = slice(sub * QT, (sub + 1) * QT)
    gt = jax.nn.sigmoid(gate_ref[0, rows, :])
    for r in range(NSA_REP):
        col = 3 * r + branch
        lanes = slice(r * LANE, (r + 1) * LANE)
        val = gt[:, col:col + 1] * o[:, r * QT:(r + 1) * QT].T
        for other in others:
            val = val + other[rows, lanes]
        o_ref[rows, lanes] = val.astype(o_ref.dtype)


def _init_state(cols):
    return (jnp.full((1, cols), NEG, F32), jnp.zeros((1, cols), F32), jnp.zeros((LANE, cols), F32))


def _win_kernel(q0, q1, q2, q3, k_ref, v_ref, gate_ref, wb_ref, o_ref, kb_ref, vt_ref, *, seq):
    step = pl.program_id(2)
    cols = NSA_REP * QT
    wt = WINDOW // QT

    @pl.when(step == 0)
    def _():
        kb_ref[0:WINDOW, :] = jnp.zeros((WINDOW, LANE), BF16)
        for i in range(wt):
            vt_ref[i] = jnp.zeros((LANE, QT), BF16)

        def fill(i, carry):
            r0 = pl.multiple_of(i * QT, QT)
            kb_ref[pl.ds(WINDOW + r0, QT), :] = (k_ref[0, pl.ds(r0, QT), :] * SCALE_LOG2).astype(BF16)
            vt_ref[wt + i] = v_ref[0, pl.ds(r0, QT), :].T.astype(BF16)
            return carry

        lax.fori_loop(0, seq // QT, fill, 0)

    rho = lax.broadcasted_iota(jnp.int32, (WINDOW + QT, cols), 0)
    subs = range(WIN_SUB)
    tis = [step * WIN_SUB + sub for sub in subs]
    qts = [_q_transposed((q0, q1, q2, q3), sub) for sub in subs]
    ss = [_dot(kb_ref[pl.ds(pl.multiple_of(tis[u] * QT, QT), WINDOW + QT), :], qts[u]) + wb_ref[0]
          for u in subs]
    ss = [jnp.where(rho >= (wt - tis[u]) * QT, ss[u], NEG) for u in subs]
    ms = [jnp.max(ss[u], axis=0, keepdims=True) for u in subs]
    ps = [jnp.exp2(ss[u] - ms[u]) for u in subs]
    ls = [jnp.sum(ps[u], axis=0, keepdims=True) for u in subs]
    pbs = [ps[u].astype(BF16) for u in subs]
    accs = [_dot(vt_ref[tis[u]], pbs[u][0:QT]) for u in subs]
    for d in range(1, wt + 1):
        accs = [accs[u] + _dot(vt_ref[tis[u] + d], pbs[u][d * QT:(d + 1) * QT]) for u in subs]
    for u in subs:
        _finish_t((ms[u], ls[u], accs[u]), gate_ref, 2, o_ref, u)


def _window(proj, wb, batch, seq):
    nt = proj.shape[1]
    rows = WIN_SUB * QT
    nqt = seq // rows
    cols = NSA_REP * QT
    slab = lambda cb0: pl.BlockSpec((1, seq, LANE), lambda b, g, t: (cb0 + g, b, 0))
    return pl.pallas_call(
        functools.partial(_win_kernel, seq=seq),
        grid=(batch, NSA_GROUPS, nqt),
        in_specs=_q_specs(nqt, rows) + [
            slab(CB_KW), slab(CB_VW),
            pl.BlockSpec((1, rows, LANE), lambda b, g, t: (CB_GATE + g, b * nqt + t, 0)),
            pl.BlockSpec((1, WINDOW + QT, cols), lambda b, g, t: (g, 0, 0))],
        out_specs=pl.BlockSpec((rows, cols), lambda b, g, t: (b * nqt + t, g)),
        out_shape=jax.ShapeDtypeStruct((nt, NSA_HEADS * LANE), F32),
        scratch_shapes=[pltpu.VMEM((seq + WINDOW, LANE), BF16),
                        pltpu.VMEM((seq // QT + WINDOW // QT, LANE, QT), BF16)],
        compiler_params=_cparams(("arbitrary", "arbitrary", "arbitrary")),
        name="nsa_window",
    )(proj, proj, proj, proj, proj, proj, proj, wb)


def _softmax_steps(ss, vt, states):
    n = range(len(ss))
    m_new = [jnp.maximum(states[u][0], jnp.max(ss[u], axis=0, keepdims=True)) for u in n]
    alpha = [jnp.exp2(states[u][0] - m_new[u]) for u in n]
    p = [jnp.exp2(ss[u] - m_new[u]) for u in n]
    l = [alpha[u] * states[u][1] + jnp.sum(p[u], axis=0, keepdims=True) for u in n]
    pv = [_dot(vt, p[u].astype(BF16)) for u in n]
    return [(m_new[u], l[u], alpha[u] * states[u][2] + pv[u]) for u in n]


def _sel_kernel(q0, q1, q2, q3, k_ref, v_ref, sel_ref, gate_ref, nd_ref, npv_ref, oc_ref, ow_ref, o_ref,
                kb_ref, vt_ref, qa_ref, m_ref, l_ref, acc_ref, *, seq):
    step = pl.program_id(2)
    cols = NSA_REP * QT
    spb = KEY_BLK // SLC_BLOCK
    subs = range(SEL_SUB)

    @pl.when(step == 0)
    def _():
        blk = lax.broadcasted_iota(jnp.int32, (KEY_BLK, LANE), 0) // SLC_BLOCK
        onehot = jnp.where(blk == lax.broadcasted_iota(jnp.int32, (KEY_BLK, LANE), 1), 1.0, 0.0).astype(BF16)

        def fill(i, carry):
            r0 = pl.multiple_of(i * KEY_BLK, KEY_BLK)
            kb_ref[pl.ds(r0, KEY_BLK), 0:LANE] = (k_ref[0, pl.ds(r0, KEY_BLK), :] * SCALE_LOG2).astype(BF16)
            kb_ref[pl.ds(r0, KEY_BLK), LANE:2 * LANE] = onehot
            for c in range(SEL_SUB):
                vt_ref[i, :, c * QT:(c + 1) * QT] = v_ref[0, pl.ds(r0 + c * QT, QT), :].T.astype(BF16)
            return carry

        lax.fori_loop(0, seq // KEY_BLK, fill, 0)
        qa_ref[...] = jnp.zeros_like(qa_ref)

    for u in subs:
        qa_ref[u, 0:LANE, :] = _q_transposed((q0, q1, q2, q3), u)

    def block(kb):
        r0 = pl.multiple_of(kb * KEY_BLK, KEY_BLK)
        return kb_ref[pl.ds(r0, KEY_BLK), :], vt_ref[kb]

    pad_rows = jnp.zeros((SEL_ROWS - spb, cols), F32)

    def scores(kb, k):
        for u in subs:
            rows = sel_ref[0, 0, pl.ds(pl.multiple_of(kb * spb, spb), spb), u * QT:(u + 1) * QT]
            rows = jnp.concatenate([rows] * NSA_REP, axis=1)
            qa_ref[u, LANE:LANE + SEL_ROWS, :] = jnp.concatenate([rows, pad_rows], axis=0).astype(BF16)
        return [_dot(k, qa_ref[u]) for u in subs]

    k, vt = block(step)
    ss = scores(step, k)
    ss = [ss[u] + nd_ref[0, u] for u in subs]
    states = _softmax_steps(ss, vt, [_init_state(cols) for _ in subs])

    prev = jnp.maximum(step - 1, 0)
    k, vt = block(prev)
    ss = scores(prev, k)
    ss[0] = ss[0] + npv_ref[0]
    exists = (step + jnp.zeros((KEY_BLK, cols), jnp.int32)) >= 1
    ss = [jnp.where(exists, ss[u], NEG) for u in subs]
    states = _softmax_steps(ss, vt, states)
    for u in subs:
        m_ref[u], l_ref[u], acc_ref[u] = states[u]

    def body(kb, carry):
        k, vt = block(kb)
        st = _softmax_steps(scores(kb, k), vt, [(m_ref[u], l_ref[u], acc_ref[u]) for u in subs])
        for u in subs:
            m_ref[u], l_ref[u], acc_ref[u] = st[u]
        return carry

    lax.fori_loop(0, jnp.maximum(step - 1, 0), body, 0)
    for u in subs:
        _finish_t((m_ref[u], l_ref[u], acc_ref[u]), gate_ref, 1, o_ref, u, others=(oc_ref, ow_ref))


def _selected(proj, sel_t, nd, npv, o_c, o_w, batch, seq):
    nt = proj.shape[1]
    rows = SEL_SUB * QT
    nsteps = seq // rows
    cols = NSA_REP * QT
    slab = lambda cb0: pl.BlockSpec((1, seq, LANE), lambda b, g, t: (cb0 + g, b, 0))
    return pl.pallas_call(
        functools.partial(_sel_kernel, seq=seq),
        grid=(batch, NSA_GROUPS, nsteps),
        in_specs=_q_specs(nsteps, rows) + [
            slab(CB_KS), slab(CB_VS),
            pl.BlockSpec((1, 1, LANE, rows), lambda b, g, t: (b, g, 0, t)),
            pl.BlockSpec((1, rows, LANE), lambda b, g, t: (CB_GATE + g, b * nsteps + t, 0)),
            pl.BlockSpec((1, SEL_SUB, KEY_BLK, cols), lambda b, g, t: (g, 0, 0, 0)),
            pl.BlockSpec((1, KEY_BLK, cols), lambda b, g, t: (g, 0, 0)),
            pl.BlockSpec((rows, cols), lambda b, g, t: (b * nsteps + t, g)),
            pl.BlockSpec((rows, cols), lambda b, g, t: (b * nsteps + t, g))],
        out_specs=pl.BlockSpec((rows, cols), lambda b, g, t: (b * nsteps + t, g)),
        out_shape=jax.ShapeDtypeStruct((nt, NSA_HEADS * LANE), BF16),
        scratch_shapes=[pltpu.VMEM((seq, 2 * LANE), BF16),
                        pltpu.VMEM((seq // KEY_BLK, LANE, KEY_BLK), BF16),
                        pltpu.VMEM((SEL_SUB, 2 * LANE, cols), BF16),
                        pltpu.VMEM((SEL_SUB, 1, cols), F32), pltpu.VMEM((SEL_SUB, 1, cols), F32),
                        pltpu.VMEM((SEL_SUB, LANE, cols), F32)],
        compiler_params=_cparams(("arbitrary", "arbitrary", "arbitrary")),
        name="nsa_selected",
    )(proj, proj, proj, proj, proj, proj, sel_t, proj, nd, npv, o_c, o_w)


def _merge_kernel(oh_ref, on_ref, mgh_ref, mgn_ref, x_ref, mod_ref,
                  wh_ref, wn_ref, wo_ref, g_ref, b_ref, wr_ref, br_ref,
                  x1_ref, h2_ref, lg_ref):
    nblk = D_MODEL // LANE
    tm = x_ref.shape[0]
    halves = [slice(s * tm // MERGE_SPLIT, (s + 1) * tm // MERGE_SPLIT) for s in range(MERGE_SPLIT)]
    a_h = [_dot(oh_ref[rs, :], wh_ref[...]) for rs in halves]
    a_n = [_dot(on_ref[rs, :], wn_ref[...]) for rs in halves]
    gh = [jnp.concatenate([mgh_ref[c, rs, :] for c in range(nblk)], axis=-1) for rs in halves]
    gn = [jnp.concatenate([mgn_ref[c, rs, :] for c in range(nblk)], axis=-1) for rs in halves]
    merged = [(jax.nn.sigmoid(gh[s]) * a_h[s] + jax.nn.sigmoid(gn[s]) * a_n[s]).astype(BF16)
              for s in range(MERGE_SPLIT)]
    y = [(1.0 + mod_ref[0, 2:3, :]) * _dot(merged[s], wo_ref[...]) for s in range(MERGE_SPLIT)]
    x1 = [_layer_norm(ALPHA * x_ref[rs, :] + y[s]) * g_ref[...] + b_ref[...] for s, rs in enumerate(halves)]
    h2 = [_layer_norm(x1[s]) * (1.0 + mod_ref[0, 4:5, :]) + mod_ref[0, 3:4, :] for s in range(MERGE_SPLIT)]
    lg = [jnp.dot(h2[s], wr_ref[...], precision=lax.Precision.HIGHEST, preferred_element_type=F32) + br_ref[...]
          for s in range(MERGE_SPLIT)]
    for s, rs in enumerate(halves):
        x1_ref[rs, :] = x1[s]
        _store_rows(h2_ref, h2[s], rs.start)
        lg_ref[rs, :] = lg[s]


def _merge(o_h, o_n, proj, x2, mod, w_h, w_n, w_o, ln_g, ln_b, w_r, b_r, seq):
    nt, d = x2.shape
    tm = min(512, seq)
    nblk = d // LANE
    row = lambda w: pl.BlockSpec((tm, w), lambda i: (i, 0))
    full = lambda a: pl.BlockSpec(a.shape, lambda i: (0,) * a.ndim)
    return pl.pallas_call(
        _merge_kernel,
        grid=(nt // tm,),
        in_specs=[row(d), row(d),
                  pl.BlockSpec((nblk, tm, LANE), lambda i: (CB_MGH // nblk, i, 0)),
                  pl.BlockSpec((nblk, tm, LANE), lambda i: (CB_MGN // nblk, i, 0)),
                  row(d),
                  pl.BlockSpec((1, 6, d), lambda i: (i * tm // seq, 0, 0)),
                  full(w_h), full(w_n), full(w_o), full(ln_g), full(ln_b), full(w_r), full(b_r)],
        out_specs=[row(d), pl.BlockSpec((tm * ROW_TILES, LANE), lambda i: (i, 0)), row(LANE)],
        out_shape=[jax.ShapeDtypeStruct((nt, d), F32),
                   jax.ShapeDtypeStruct((nt * ROW_TILES, LANE), F32),
                   jax.ShapeDtypeStruct((nt, LANE), F32)],
        compiler_params=_cparams(("arbitrary",)),
        name="merge_outproj",
    )(o_h, o_n, proj, proj, x2, mod, w_h, w_n, w_o, ln_g, ln_b, w_r, b_r)


def _route_kernel(lg_ref, rec_ref, cnt_ref, carry_ref, *, tm):
    @pl.when(pl.program_id(0) == 0)
    def _():
        carry_ref[...] = jnp.zeros_like(carry_ref)

    lg = lg_ref[...]
    lane = lax.broadcasted_iota(jnp.int32, (tm, LANE), 1).astype(F32)
    far = float(LANE)
    gmask = lane < N_GROUPS
    gl = jnp.where(gmask, lg, -jnp.inf)
    gmax = jnp.max(gl, axis=-1, keepdims=True)
    gsum = jnp.sum(jnp.where(gmask, jnp.exp(gl - gmax), 0.0), axis=-1, keepdims=True)
    grp_p = 1.0 / gsum
    gidx = jnp.min(jnp.where(gl == gmax, lane, far), axis=-1, keepdims=True)
    lo = N_GROUPS + EXP_PER_GROUP * gidx
    emask = (lane >= lo) & (lane < lo + EXP_PER_GROUP)
    el = jnp.where(emask, lg, -jnp.inf)
    m1 = jnp.max(el, axis=-1, keepdims=True)
    i1 = jnp.min(jnp.where(el == m1, lane, far), axis=-1, keepdims=True)
    el2 = jnp.where(lane == i1, -jnp.inf, el)
    m2 = jnp.max(el2, axis=-1, keepdims=True)
    i2 = jnp.min(jnp.where(emask & (lane != i1) & (el2 == m2), lane, far), axis=-1, keepdims=True)
    e = jnp.exp(m2 - m1)
    w0 = grp_p / (1.0 + e)
    w1 = grp_p * e / (1.0 + e)

    oh0 = lane == i1
    oh1 = lane == i2
    f0 = jnp.where(oh0, 1.0, 0.0)
    f1 = jnp.where(oh1, 1.0, 0.0)
    ri = lax.broadcasted_iota(jnp.int32, (tm, tm), 0)
    ci = lax.broadcasted_iota(jnp.int32, (tm, tm), 1)
    before = jnp.where(ci < ri, 1.0, 0.0).astype(BF16)
    cum0 = _dot(before, f0.astype(BF16))
    cum1 = _dot(before, f1.astype(BF16))
    tot0 = jnp.sum(f0, axis=0, keepdims=True)
    tot1 = jnp.sum(f1, axis=0, keepdims=True)
    carry = carry_ref[...]
    rank0 = jnp.sum(jnp.where(oh0, carry + cum0, 0.0), axis=-1, keepdims=True)
    rank1 = jnp.sum(jnp.where(oh1, carry + tot0 + cum1, 0.0), axis=-1, keepdims=True)
    carry = carry + tot0 + tot1
    carry_ref[...] = carry
    cnt_ref[...] = carry

    rec = jnp.where(lane == 0, i1 - N_GROUPS, 0.0)
    rec = jnp.where(lane == 1, i2 - N_GROUPS, rec)
    rec = jnp.where(lane == 2, w0, rec)
    rec = jnp.where(lane == 3, w1, rec)
    rec = jnp.where(lane == 4, rank0, rec)
    rec = jnp.where(lane == 5, rank1, rec)
    rec_ref[...] = rec


def _route(logits):
    nt = logits.shape[0]
    tm = min(512, nt)
    return pl.pallas_call(
        functools.partial(_route_kernel, tm=tm),
        grid=(nt // tm,),
        in_specs=[pl.BlockSpec((tm, LANE), lambda i: (i, 0))],
        out_specs=[pl.BlockSpec((tm, LANE), lambda i: (i, 0)),
                   pl.BlockSpec((1, LANE), lambda i: (0, 0))],
        out_shape=[jax.ShapeDtypeStruct((nt, LANE), F32),
                   jax.ShapeDtypeStruct((1, LANE), F32)],
        scratch_shapes=[pltpu.VMEM((1, LANE), F32)],
        compiler_params=_cparams(("arbitrary",)),
        name="moe_route",
    )(logits)


def _row_copy(src, dst, sem):
    return pltpu.make_async_copy(src, dst, sem)


def _tile_of(r):
    return pl.ds(pl.multiple_of(r * ROW_TILES, ROW_TILES), ROW_TILES)


def _load_rows(ref, n, lead=()):
    return jnp.concatenate([ref[lead + (pl.ds(c, n, stride=ROW_TILES), slice(None))]
                            for c in range(ROW_TILES)], axis=1)


def _store_rows(ref, val, row0=0):
    n = val.shape[0]
    for c in range(ROW_TILES):
        ref[pl.ds(row0 * ROW_TILES + c, n, stride=ROW_TILES), :] = val[:, c * LANE:(c + 1) * LANE]


def _dispatch_kernel(dest_ref, h_ref, zero_ref, xp_ref, sem, *, tm):
    del zero_ref
    base = pl.program_id(0) * tm

    def issue(r, carry):
        for k in range(2):
            d = dest_ref[2 * (base + r) + k]
            _row_copy(h_ref.at[_tile_of(r), :], xp_ref.at[_tile_of(d), :], sem).start()
        return carry

    lax.fori_loop(0, tm, issue, 0, unroll=ROW_DMA_UNROLL)
    for _ in range(2):
        _row_copy(h_ref, xp_ref.at[pl.ds(0, tm * ROW_TILES), :], sem).wait()


def _dispatch(dest, h2, x_pad0):
    nt = h2.shape[0] // ROW_TILES
    tm = min(256, nt)
    return pl.pallas_call(
        functools.partial(_dispatch_kernel, tm=tm),
        grid_spec=pltpu.PrefetchScalarGridSpec(
            num_scalar_prefetch=1,
            grid=(nt // tm,),
            in_specs=[pl.BlockSpec((tm * ROW_TILES, LANE), lambda i, dest: (i, 0)),
                      pl.BlockSpec(memory_space=pl.ANY)],
            out_specs=pl.BlockSpec(memory_space=pl.ANY),
            scratch_shapes=[pltpu.SemaphoreType.DMA(())]),
        out_shape=jax.ShapeDtypeStruct(x_pad0.shape, x_pad0.dtype),
        input_output_aliases={2: 0},
        compiler_params=_cparams(("arbitrary",)),
        name="moe_dispatch",
    )(dest, h2, x_pad0)


def _expert_kernel(be_ref, nu_ref, x_ref, w1_ref, w3_ref, w2_ref, y_ref):
    del be_ref
    i = pl.program_id(0)

    @pl.when(i < nu_ref[0])
    def _():
        xb = _load_rows(x_ref, MOE_ROWS).astype(BF16)
        a = _dot(xb, w1_ref[0].astype(BF16))
        b = _dot(xb, w3_ref[0].astype(BF16))
        hmid = (a * jax.nn.sigmoid(a) * b).astype(BF16)
        _store_rows(y_ref, _dot(hmid, w2_ref[0].astype(BF16)))

    @pl.when(i >= nu_ref[0])
    def _():
        y_ref[...] = jnp.zeros_like(y_ref)


def _experts(block_expert, n_used, x_pad, w1, w3, w2):
    d, de = w1.shape[1], w1.shape[2]
    nb = x_pad.shape[0] // (MOE_ROWS * ROW_TILES)
    return pl.pallas_call(
        _expert_kernel,
        grid_spec=pltpu.PrefetchScalarGridSpec(
            num_scalar_prefetch=2,
            grid=(nb,),
            in_specs=[pl.BlockSpec((MOE_ROWS * ROW_TILES, LANE), lambda i, be, nu: (i, 0)),
                      pl.BlockSpec((1, d, de), lambda i, be, nu: (be[i], 0, 0)),
                      pl.BlockSpec((1, d, de), lambda i, be, nu: (be[i], 0, 0)),
                      pl.BlockSpec((1, de, d), lambda i, be, nu: (be[i], 0, 0))],
            out_specs=pl.BlockSpec((MOE_ROWS * ROW_TILES, LANE), lambda i, be, nu: (i, 0))),
        out_shape=jax.ShapeDtypeStruct(x_pad.shape, F32),
        compiler_params=_cparams(("arbitrary",)),
        name="moe_experts",
    )(block_expert, n_used, x_pad, w1, w3, w2)


def _combine_kernel(dest_ref, yp_ref, rec_ref, x1_ref, mod_ref, g_ref, b_ref, o_ref, buf_ref, sem, *, tm):
    base = pl.program_id(0) * tm

    def issue(r, carry):
        for k in range(2):
            d = dest_ref[2 * (base + r) + k]
            _row_copy(yp_ref.at[_tile_of(d), :], buf_ref.at[k, _tile_of(r), :], sem).start()
        return carry

    lax.fori_loop(0, tm, issue, 0, unroll=ROW_DMA_UNROLL)
    for k in range(2):
        _row_copy(yp_ref.at[pl.ds(0, tm * ROW_TILES), :], buf_ref.at[k], sem).wait()

    rec = rec_ref[...]
    y = rec[:, 2:3] * _load_rows(buf_ref, tm, (0,)) + rec[:, 3:4] * _load_rows(buf_ref, tm, (1,))
    y = (1.0 + mod_ref[0, 5:6, :]) * y
    o_ref[...] = _layer_norm(ALPHA * x1_ref[...] + y) * g_ref[...] + b_ref[...]


def _combine(dest, y_pad, rec, x1, mod, ln_g, ln_b, seq):
    nt, d = x1.shape
    tm = min(256, seq)
    return pl.pallas_call(
        functools.partial(_combine_kernel, tm=tm),
        grid_spec=pltpu.PrefetchScalarGridSpec(
            num_scalar_prefetch=1,
            grid=(nt // tm,),
            in_specs=[pl.BlockSpec(memory_space=pl.ANY),
                      pl.BlockSpec((tm, LANE), lambda i, dest: (i, 0)),
                      pl.BlockSpec((tm, d), lambda i, dest: (i, 0)),
                      pl.BlockSpec((1, 6, d), lambda i, dest: (i * tm // seq, 0, 0)),
                      pl.BlockSpec((1, d), lambda i, dest: (0, 0)),
                      pl.BlockSpec((1, d), lambda i, dest: (0, 0))],
            out_specs=pl.BlockSpec((tm, d), lambda i, dest: (i, 0)),
            scratch_shapes=[pltpu.VMEM((2, tm * ROW_TILES, LANE), F32), pltpu.SemaphoreType.DMA(())]),
        out_shape=jax.ShapeDtypeStruct((nt, d), F32),
        compiler_params=_cparams(("arbitrary",)),
        name="moe_combine",
    )(dest, y_pad, rec, x1, mod, ln_g, ln_b)


def _rel_bucket(dist):
    n = jnp.maximum(dist, 0)
    max_exact = REL_BUCKETS // 2
    nf = jnp.maximum(n, 1).astype(F32)
    large = max_exact + (jnp.log(nf / max_exact) / math.log(REL_MAX_DIST / max_exact)
                         * (REL_BUCKETS - max_exact)).astype(jnp.int32)
    large = jnp.minimum(large, REL_BUCKETS - 1)
    return jnp.where(n < max_exact, n, large)


def _bias_tables(rel_bias, seq):
    bucket_onehot = (_rel_bucket(jnp.arange(LANE))[:, None] == jnp.arange(REL_BUCKETS)).astype(F32)
    tab_d = jnp.einsum('db,hb->hd', bucket_onehot, rel_bias,
                       precision=lax.Precision.HIGHEST)
    tok = np.arange(QT)[None, :]
    key = np.arange(LANE)[:, None]
    far = tab_d[:, LANE - 1]
    cols = NSA_REP * QT

    def transposed(dist):
        idx = jnp.asarray(np.clip(dist, 0, LANE - 1).astype(np.int32))
        onehot = (idx[..., None] == jnp.arange(LANE, dtype=jnp.int32)).astype(F32)
        t = jnp.einsum('ijd,hd->hij', onehot, tab_d, precision=lax.Precision.HIGHEST)
        t = (t - far[:, None, None]) * LOG2E
        t = t.reshape(NSA_GROUPS, NSA_REP, LANE, QT).transpose(0, 2, 1, 3)
        return t.reshape(NSA_GROUPS, LANE, cols)

    t0t = transposed(tok - key)
    t1t = transposed(tok - key + QT)

    ns = seq // CMP_STRIDE
    d_c = tok - CMP_STRIDE * key + (CMP_STRIDE * CMP_PAD - (CMP_BLOCK - 1))
    seen = np.tile(d_c >= 0, (1, NSA_REP))
    recent = jnp.where(seen[None], transposed(d_c), NEG)
    tt = jnp.concatenate([jnp.zeros((NSA_GROUPS, ns, cols), F32), recent,
                          jnp.full((NSA_GROUPS, ns, cols), NEG, F32)], axis=1)

    rho = np.arange(WINDOW + QT)[:, None]
    tok_w = np.tile(np.arange(QT), NSA_REP)[None, :]
    band = (rho > tok_w) & (rho <= tok_w + WINDOW)
    rows = jnp.concatenate([jnp.zeros((NSA_GROUPS, WINDOW - QT, NSA_REP * QT), F32), t1t, t0t], axis=1)
    wb = jnp.where(band[None], rows, NEG)

    zeros = lambda n: jnp.zeros((NSA_GROUPS, n * QT, cols), F32)
    negs = lambda n: jnp.full((NSA_GROUPS, n * QT, cols), NEG, F32)
    diag = jnp.where(np.tile(tok >= key, (1, NSA_REP))[None], t0t, NEG)
    nd = jnp.stack([jnp.concatenate(([zeros(u - 1), t1t] if u else []) + [diag, negs(SEL_SUB - 1 - u)], axis=1)
                    for u in range(SEL_SUB)], axis=1)
    npv = jnp.concatenate([zeros(SEL_SUB - 1), t1t], axis=1)
    return nd, npv, wb, tt


def _overlap_matrix(seq):
    ns = seq // CMP_STRIDE
    nslc = seq // SLC_BLOCK
    ov = np.zeros((LANE, ns), np.float32)
    cs = np.arange(ns - 1) * CMP_STRIDE
    ss = np.arange(nslc) * SLC_BLOCK
    ov[:nslc, :ns - 1] = ((cs[None, :] < ss[:, None] + SLC_BLOCK) & (cs[None, :] + CMP_BLOCK > ss[:, None]))
    return jnp.asarray(ov, BF16)


def _reorder_cols(a):
    lead = a.shape[:-1]
    gate = a[..., MAIN_COLS:MAIN_COLS + GATE_COLS]
    per = GATE_COLS // NSA_GROUPS
    gate_blocks = []
    for g in range(NSA_GROUPS):
        gate_blocks.append(gate[..., g * per:(g + 1) * per])
        gate_blocks.append(jnp.zeros(lead + (LANE - per,), a.dtype))
    pad = jnp.zeros(lead + ((CB_MGH - CB_GATE - NSA_GROUPS) * LANE,), a.dtype)
    return jnp.concatenate([a[..., :MAIN_COLS]] + gate_blocks + [pad, a[..., MAIN_COLS + GATE_COLS:]], axis=-1)


def kernel(x, c, ada_w, ada_b, w_in, b_in, hg_lb_logits, hg_norm_w, cmp_pos_k, cmp_w1_k, cmp_b1_k, cmp_w2_k, cmp_pos_v, cmp_w1_v, cmp_b1_v, cmp_w2_v, rel_bias, w_br_hg, w_br_nsa, w_out, ln1_g, ln1_b, router_grp_w, router_grp_b, router_exp_w, router_exp_b, exp_w1, exp_w3, exp_w2, ln2_g, ln2_b):
    batch, seq, d = x.shape
    nt = batch * seq
    assert d == D_MODEL and seq % 1024 == 0 and seq // SLC_BLOCK <= LANE
    l = 0
    x2 = x.reshape(nt, d)

    c_pad = jnp.zeros((8, d), F32).at[:batch].set(c)
    mod = _adaln(c_pad, ada_w[l], ada_b[l][None])[:batch].reshape(batch, 6, d)

    proj = _inproj(x2, mod, _reorder_cols(w_in[l]).astype(BF16), _reorder_cols(b_in[l])[None], seq)

    lb_all = jnp.cumsum(jax.nn.softmax(hg_lb_logits.astype(F32), axis=0), axis=0)
    o_h = _hgrn(proj, lb_all[l][None], hg_norm_w[l][None], batch, seq)

    kvc = _compress(proj, jnp.stack([cmp_pos_k[l], cmp_pos_v[l]]),
                    jnp.stack([cmp_w1_k[l], cmp_w1_v[l]]).astype(BF16),
                    jnp.stack([cmp_b1_k[l], cmp_b1_v[l]])[:, None, :],
                    jnp.stack([cmp_w2_k[l], cmp_w2_v[l]]).astype(BF16), batch, seq)

    nd, npv, wb, tt = _bias_tables(rel_bias, seq)
    o_c, sel_t = _cmp_sel(proj, kvc, tt, _overlap_matrix(seq), batch, seq)
    o_w = _window(proj, wb, batch, seq)
    o_n = _selected(proj, sel_t, nd, npv, o_c, o_w, batch, seq)

    w_r = jnp.zeros((d, LANE), F32).at[:, :N_GROUPS].set(router_grp_w[l])
    w_r = w_r.at[:, N_GROUPS:N_GROUPS + N_EXPERTS].set(router_exp_w[l])
    b_r = jnp.zeros((1, LANE), F32).at[0, :N_GROUPS].set(router_grp_b[l])
    b_r = b_r.at[0, N_GROUPS:N_GROUPS + N_EXPERTS].set(router_exp_b[l])
    x1, h2, logits = _merge(o_h, o_n, proj, x2, mod,
                            w_br_hg[l].astype(BF16), w_br_nsa[l].astype(BF16), w_out[l].astype(BF16),
                            ln1_g[l][None], ln1_b[l][None], w_r, b_r, seq)

    rec, cnt = _route(logits)
    counts = cnt[0, N_GROUPS:N_GROUPS + N_EXPERTS].astype(jnp.int32)
    padded = (counts + MOE_ROWS - 1) // MOE_ROWS * MOE_ROWS
    pend = jnp.cumsum(padded)
    pstart = pend - padded
    n_assign = 2 * nt
    nb = n_assign // MOE_ROWS + N_EXPERTS
    expert = rec[:, 0:2].astype(jnp.int32)
    dest = (pstart[expert] + rec[:, 4:6].astype(jnp.int32)).reshape(-1)
    block_start = jnp.arange(nb, dtype=jnp.int32) * MOE_ROWS
    block_expert = jnp.minimum(jnp.sum(pend[None, :] <= block_start[:, None], axis=1),
                               N_EXPERTS - 1).astype(jnp.int32)
    n_used = (pend[-1:] // MOE_ROWS).astype(jnp.int32)

    x_pad = _dispatch(dest, h2, jnp.zeros((nb * MOE_ROWS * ROW_TILES, LANE), F32))
    y_pad = _experts(block_expert, n_used, x_pad, exp_w1[l], exp_w3[l], exp_w2[l])
    out = _combine(dest, y_pad, rec, x1, mod, ln2_g[l][None], ln2_b[l][None], seq)
    return out.reshape(batch, seq, d)
```

```python
import functools
import math

import numpy as np
import jax
import jax.numpy as jnp
from jax import lax
from jax.experimental import pallas as pl
from jax.experimental.pallas import tpu as pltpu

F32 = jnp.float32
BF16 = jnp.bfloat16

D_MODEL = 1024
HG_HEADS = 8
HG_DK = 128
HG_DV = 128
HG_CHUNK = 32
HG_SUPER = 256
NSA_HEADS = 8
NSA_GROUPS = 2
NSA_REP = NSA_HEADS // NSA_GROUPS
NSA_DK = 128
CMP_BLOCK = 32
CMP_STRIDE = 16
SLC_BLOCK = 64
SLC_TOPK = 16
WINDOW = 512
FORCE_SCORE = 1e4
REL_BUCKETS = 32
REL_MAX_DIST = 128
N_GROUPS = 4
EXP_PER_GROUP = 8
N_EXPERTS = N_GROUPS * EXP_PER_GROUP
D_EXPERT = D_MODEL // 2
DEPTH = 1
ALPHA = (2 * DEPTH) ** 0.25

LANE = 128
QT = 128
NEG = -1e30
SCALE = NSA_DK ** -0.5
LOG2E = math.log2(math.e)
SCALE_LOG2 = SCALE * LOG2E
KEY_BLK = 512
SEL_ROWS = 16
SEL_SUB = KEY_BLK // QT
WIN_SUB = 4
CMP_SUB = 4
CMP_PAD = 120
VMEM_LIMIT = 56 * 1024 * 1024

CB_HQ, CB_HF, CB_HI, CB_HG = 0, 8, 16, 24
CB_NQ = 32
CB_KC, CB_VC, CB_KS, CB_VS, CB_KW, CB_VW = 40, 42, 44, 46, 48, 50
CB_GATE = 52
CB_MGH, CB_MGN = 56, 64
NCB = 72
MAIN_COLS = 52 * LANE
GATE_COLS = 3 * NSA_HEADS

MERGE_SPLIT = 2
ROW_TILES = D_MODEL // LANE
MOE_ROWS = 512
ROW_DMA_UNROLL = 8


def _cparams(sem):
    return pltpu.CompilerParams(dimension_semantics=sem, vmem_limit_bytes=VMEM_LIMIT)


def _dot(a, b):
    return jnp.dot(a, b, preferred_element_type=F32)


def _dot_nt(a, b):
    return lax.dot_general(a, b, (((1,), (1,)), ((), ())), preferred_element_type=F32)


def _dot_tn(a, b):
    return lax.dot_general(a, b, (((0,), (0,)), ((), ())), preferred_element_type=F32)


def _split3(x):
    hi = x.astype(BF16)
    r = x - hi.astype(F32)
    mid = r.astype(BF16)
    lo = (r - mid.astype(F32)).astype(BF16)
    return hi, mid, lo


def _dot01(m01, x):
    hi, mid, lo = _split3(x)
    return _dot(m01, hi) + _dot(m01, mid) + _dot(m01, lo)


def _dot01_r(x, m01):
    hi, mid, lo = _split3(x)
    return _dot(hi, m01) + _dot(mid, m01) + _dot(lo, m01)


def _layer_norm(x, eps=1e-5):
    mu = jnp.mean(x, axis=-1, keepdims=True)
    xc = x - mu
    var = jnp.mean(xc * xc, axis=-1, keepdims=True)
    return xc * lax.rsqrt(var + eps)


def _adaln_kernel(c_ref, w_ref, b_ref, o_ref):
    c = c_ref[...]
    ca = c * jax.nn.sigmoid(c)
    o_ref[...] = jnp.dot(ca, w_ref[...], precision=lax.Precision.HIGHEST,
                         preferred_element_type=F32) + b_ref[...]


def _adaln(c_pad, w, b):
    rows, d = c_pad.shape
    n = w.shape[1]
    return pl.pallas_call(
        _adaln_kernel,
        grid=(n // d,),
        in_specs=[pl.BlockSpec((rows, d), lambda j: (0, 0)),
                  pl.BlockSpec((d, d), lambda j: (0, j)),
                  pl.BlockSpec((1, d), lambda j: (0, j))],
        out_specs=pl.BlockSpec((rows, d), lambda j: (0, j)),
        out_shape=jax.ShapeDtypeStruct((rows, n), F32),
        compiler_params=_cparams(("arbitrary",)),
        name="adaln",
    )(c_pad, w, b)


def _inproj_kernel(x_ref, mod_ref, w_ref, b_ref, o_ref, hn_ref, *, ncb_tile):
    @pl.when(pl.program_id(1) == 0)
    def _():
        hn = _layer_norm(x_ref[...])
        sh = mod_ref[0, 0:1, :]
        sc = mod_ref[0, 1:2, :]
        hn_ref[...] = (hn * (1.0 + sc) + sh).astype(BF16)

    res = _dot(hn_ref[...], w_ref[...]) + b_ref[...]
    for c in range(ncb_tile):
        o_ref[c] = res[:, c * LANE:(c + 1) * LANE]


def _inproj(x2, mod, w, b, seq):
    nt, d = x2.shape
    tm = min(2048, seq)
    ncb_tile = 6
    tn = ncb_tile * LANE
    return pl.pallas_call(
        functools.partial(_inproj_kernel, ncb_tile=ncb_tile),
        grid=(nt // tm, NCB // ncb_tile),
        in_specs=[pl.BlockSpec((tm, d), lambda i, j: (i, 0)),
                  pl.BlockSpec((1, 6, d), lambda i, j: (i * tm // seq, 0, 0)),
                  pl.BlockSpec((d, tn), lambda i, j: (0, j)),
                  pl.BlockSpec((1, tn), lambda i, j: (0, j))],
        out_specs=pl.BlockSpec((ncb_tile, tm, LANE), lambda i, j: (j, i, 0)),
        out_shape=jax.ShapeDtypeStruct((NCB, nt, LANE), F32),
        scratch_shapes=[pltpu.VMEM((tm, d), BF16)],
        compiler_params=_cparams(("arbitrary", "arbitrary")),
        name="inproj",
    )(x2, mod, w, b)


def _hgrn_kernel(q_ref, f_ref, v_ref, g_ref, lb_ref, nw_ref, o_ref, st_ref, *, rows):
    @pl.when(pl.program_id(1) == 0)
    def _():
        st_ref[...] = jnp.zeros_like(st_ref)

    sup = HG_SUPER
    ri = lax.broadcasted_iota(jnp.int32, (sup, sup), 0)
    ci = lax.broadcasted_iota(jnp.int32, (sup, sup), 1)
    same = (ri // HG_CHUNK) == (ci // HG_CHUNK)
    cum_m = jnp.where(same & (ci <= ri), 1.0, 0.0).astype(BF16)
    tot_m = jnp.where(same, 1.0, 0.0).astype(BF16)
    rt = lax.broadcasted_iota(jnp.int32, (LANE, LANE), 0)
    ct = lax.broadcasted_iota(jnp.int32, (LANE, LANE), 1)
    tril = ((rt // HG_CHUNK) == (ct // HG_CHUNK)) & (ct <= rt)
    per = sup // HG_CHUNK
    groups = [slice(g * LANE, (g + 1) * LANE) for g in range(sup // LANE)]

    heads = range(HG_HEADS)
    hs = [slice(h * LANE, (h + 1) * LANE) for h in heads]

    def wide(ref, r0):
        return jnp.concatenate([ref[h, pl.ds(r0, sup), :] for h in heads], axis=1)

    def body(i, carry):
        r0 = pl.multiple_of(i * sup, sup)
        lb = lb_ref[...]
        f = lb + (1.0 - lb) * jax.nn.sigmoid(wide(f_ref, r0))
        lf = jnp.log(f)
        k = 1.0 - f
        hi, mid, lo = _split3(lf)
        b = _dot(cum_m, hi) + _dot(cum_m, mid) + _dot(cum_m, lo)
        bl = _dot(tot_m, hi) + _dot(tot_m, mid) + _dot(tot_m, lo)
        q_in = (wide(q_ref, r0) * jnp.exp(b)).astype(BF16)
        k_in = (k * jnp.exp(-b)).astype(BF16)
        k_end = (k * jnp.exp(bl - b)).astype(BF16)
        vb = wide(v_ref, r0).astype(BF16)
        dec = [jnp.exp(bl[c * HG_CHUNK:c * HG_CHUNK + 1, :]) for c in range(per)]
        chunk = [slice(c * HG_CHUNK, (c + 1) * HG_CHUNK) for c in range(per)]

        att = [[jnp.where(tril, _dot_nt(q_in[g, hs[h]], k_in[g, hs[h]]), 0.0).astype(BF16) for g in groups]
               for h in heads]
        upd = [[_dot_tn(vb[chunk[c], hs[h]], k_end[chunk[c], hs[h]]) for c in range(per)] for h in heads]
        intra = [[_dot(att[h][n], vb[g, hs[h]]) for n, g in enumerate(groups)] for h in heads]

        st = [st_ref[h] for h in heads]
        inter = [[] for _ in heads]
        for c in range(per):
            for h in heads:
                inter[h].append(_dot_nt(q_in[chunk[c], hs[h]], st[h].astype(BF16)))
                st[h] = dec[c][:, hs[h]] * st[h] + upd[h][c]
        for h in heads:
            st_ref[h] = st[h]

        o = [jnp.concatenate(intra[h], axis=0) + jnp.concatenate(inter[h], axis=0) for h in heads]
        scale = [lax.rsqrt(jnp.mean(o[h] * o[h], axis=-1, keepdims=True) + 1e-6) for h in heads]
        on = jnp.concatenate([o[h] * scale[h] for h in heads], axis=1)
        on = on * nw_ref[...] * jax.nn.sigmoid(wide(g_ref, r0))
        o_ref[pl.ds(r0, sup), :] = on.astype(BF16)
        return carry

    lax.fori_loop(0, rows // sup, body, 0)


def _hgrn(proj, lb, nw, batch, seq):
    nt = proj.shape[1]
    tb = min(512, seq)
    nblk = seq // tb

    def slab(cb0):
        return pl.BlockSpec((HG_HEADS, tb, LANE), lambda b, t: (cb0 // HG_HEADS, b * nblk + t, 0))

    vec = pl.BlockSpec((1, HG_HEADS * LANE), lambda b, t: (0, 0))
    return pl.pallas_call(
        functools.partial(_hgrn_kernel, rows=tb),
        grid=(batch, nblk),
        in_specs=[slab(CB_HQ), slab(CB_HF), slab(CB_HI), slab(CB_HG), vec, vec],
        out_specs=pl.BlockSpec((tb, HG_HEADS * HG_DV), lambda b, t: (b * nblk + t, 0)),
        out_shape=jax.ShapeDtypeStruct((nt, HG_HEADS * HG_DV), BF16),
        scratch_shapes=[pltpu.VMEM((HG_HEADS, HG_DV, HG_DK), F32)],
        compiler_params=_cparams(("arbitrary", "arbitrary")),
        name="hgrn2",
    )(proj, proj, proj, proj, lb, nw)


def _compress_kernel(x_ref, pos_ref, w1_ref, b1_ref, w2_ref, o_ref, *, ns):
    p0 = jnp.zeros((ns, LANE), F32)
    p1 = jnp.zeros((ns, LANE), F32)
    for j in range(CMP_STRIDE):
        tok = x_ref[0, pl.ds(j, ns, stride=CMP_STRIDE), :]
        rows = slice(j * LANE, (j + 1) * LANE)
        late = slice((CMP_STRIDE + j) * LANE, (CMP_STRIDE + j + 1) * LANE)
        p0 = p0 + _dot((tok + pos_ref[0, j:j + 1, :]).astype(BF16), w1_ref[0, rows, :])
        p1 = p1 + _dot((tok + pos_ref[0, CMP_STRIDE + j:CMP_STRIDE + j + 1, :]).astype(BF16), w1_ref[0, late, :])
    h = p0 + pltpu.roll(p1, ns - 1, axis=0) + b1_ref[0]
    a = h * jax.nn.sigmoid(h)
    out = _dot(a.astype(BF16), w2_ref[0])
    row = lax.broadcasted_iota(jnp.int32, out.shape, 0)
    out = jnp.where(row < ns - 1, out, 0.0)
    o_ref[0, 0, 0] = out


def _compress(proj, pos, w1, b1, w2, batch, seq):
    ns = seq // CMP_STRIDE
    np_rows = ns
    width = CMP_STRIDE * LANE
    return pl.pallas_call(
        functools.partial(_compress_kernel, ns=ns),
        grid=(2, batch, NSA_GROUPS),
        in_specs=[pl.BlockSpec((1, seq, LANE), lambda s, b, g: (CB_KC + NSA_GROUPS * s + g, b, 0)),
                  pl.BlockSpec((1, CMP_BLOCK, LANE), lambda s, b, g: (s, 0, 0)),
                  pl.BlockSpec((1, 2 * width, LANE), lambda s, b, g: (s, 0, 0)),
                  pl.BlockSpec((1, 1, LANE), lambda s, b, g: (s, 0, 0)),
                  pl.BlockSpec((1, LANE, LANE), lambda s, b, g: (s, 0, 0))],
        out_specs=pl.BlockSpec((1, 1, 1, np_rows, LANE), lambda s, b, g: (s, b, g, 0, 0)),
        out_shape=jax.ShapeDtypeStruct((2, batch, NSA_GROUPS, np_rows, LANE), F32),
        compiler_params=_cparams(("arbitrary", "arbitrary", "arbitrary")),
        name="nsa_compress",
    )(proj, pos, w1, b1, w2)


def _cmp_sel_kernel(q0, q1, q2, q3, kc_ref, vc_ref, gate_ref, tt_ref, ov_ref, oc_ref, sel_ref,
                    kb_ref, vt_ref, *, np_rows):
    step = pl.program_id(2)
    cols = NSA_REP * QT
    ns = np_rows

    @pl.when(step == 0)
    def _():
        kb_ref[...] = (kc_ref[0, 0, 0] * SCALE_LOG2).astype(BF16)
        for c in range(np_rows // LANE):
            vt_ref[:, c * LANE:(c + 1) * LANE] = vc_ref[0, 0, 0, c * LANE:(c + 1) * LANE, :].T.astype(BF16)

    subs = range(CMP_SUB)
    tis = [step * CMP_SUB + u for u in subs]
    kb = kb_ref[...]
    qts = [_q_transposed((q0, q1, q2, q3), u) for u in subs]
    bias = [tt_ref[0, pl.ds(pl.multiple_of(ns + CMP_PAD - tis[u] * (QT // CMP_STRIDE), 8), np_rows), :]
            for u in subs]
    ss = [_dot(kb, qts[u]) + bias[u] for u in subs]
    ms = [jnp.max(ss[u], axis=0, keepdims=True) for u in subs]
    ps = [jnp.exp2(ss[u] - ms[u]) for u in subs]
    ls = [jnp.sum(ps[u], axis=0, keepdims=True) for u in subs]
    invs = [jnp.where(ms[u] > 0.5 * NEG, 1.0 / ls[u], 0.0) for u in subs]
    pn = [ps[u] * invs[u] for u in subs]
    vt = vt_ref[...]
    os_ = [_dot(vt, pn[u].astype(BF16)) for u in subs]
    for u in subs:
        rows = slice(u * QT, (u + 1) * QT)
        gt = jax.nn.sigmoid(gate_ref[0, rows, :])
        for r in range(NSA_REP):
            oc_ref[rows, r * LANE:(r + 1) * LANE] = gt[:, 3 * r:3 * r + 1] * os_[u][:, r * QT:(r + 1) * QT].T

    def group_sum(p):
        tot = p[:, 0:QT]
        for r in range(1, NSA_REP):
            tot = tot + p[:, r * QT:(r + 1) * QT]
        return tot

    imp = jnp.concatenate([_dot01(ov_ref[...], group_sum(pn[u])) for u in subs], axis=1)

    width = CMP_SUB * QT
    jj = lax.broadcasted_iota(jnp.int32, (LANE, width), 0)
    tok = step * width + lax.broadcasted_iota(jnp.int32, (LANE, width), 1)
    cur = tok // SLC_BLOCK
    forced = (jj == 0) | (jj == cur) | (jj == cur - 1)
    score = jnp.where(jj <= cur, jnp.where(forced, FORCE_SCORE, imp), -1.0)
    selb = jnp.full((LANE, width), NEG, F32)
    jf = jj.astype(F32)
    for _ in range(SLC_TOPK):
        mval = jnp.max(score, axis=0, keepdims=True)
        first = jnp.min(jnp.where(score == mval, jf, float(LANE)), axis=0, keepdims=True)
        pick = jf == first
        selb = jnp.where(pick, 0.0, selb)
        score = jnp.where(pick, -jnp.inf, score)
    sel_ref[0, 0] = selb


def _q_specs(nsteps, rows=QT):
    return [pl.BlockSpec((1, rows, LANE),
                         functools.partial(lambda b, g, t, r: (CB_NQ + NSA_REP * g + r, b * nsteps + t, 0), r=r))
            for r in range(NSA_REP)]


def _cmp_sel(proj, kvc, tt, ov_t, batch, seq):
    nt = proj.shape[1]
    rows = CMP_SUB * QT
    nsteps = seq // rows
    cols = NSA_REP * QT
    np_rows = kvc.shape[3]
    kv_spec = lambda s: pl.BlockSpec((1, 1, 1, np_rows, LANE), lambda b, g, t: (s, b, g, 0, 0))
    return pl.pallas_call(
        functools.partial(_cmp_sel_kernel, np_rows=np_rows),
        grid=(batch, NSA_GROUPS, nsteps),
        in_specs=_q_specs(nsteps, rows) + [
            kv_spec(0), kv_spec(1),
            pl.BlockSpec((1, rows, LANE), lambda b, g, t: (CB_GATE + g, b * nsteps + t, 0)),
            pl.BlockSpec((1, tt.shape[1], cols), lambda b, g, t: (g, 0, 0)),
            pl.BlockSpec((LANE, np_rows), lambda b, g, t: (0, 0))],
        out_specs=[pl.BlockSpec((rows, cols), lambda b, g, t: (b * nsteps + t, g)),
                   pl.BlockSpec((1, 1, LANE, rows), lambda b, g, t: (b, g, 0, t))],
        out_shape=[jax.ShapeDtypeStruct((nt, NSA_HEADS * LANE), F32),
                   jax.ShapeDtypeStruct((batch, NSA_GROUPS, LANE, seq), F32)],
        scratch_shapes=[pltpu.VMEM((np_rows, LANE), BF16), pltpu.VMEM((LANE, np_rows), BF16)],
        compiler_params=_cparams(("arbitrary", "arbitrary", "arbitrary")),
        name="nsa_cmp_select",
    )(proj, proj, proj, proj, kvc, kvc, proj, tt, ov_t)


def _q_transposed(q_refs, sub=0):
    return jnp.concatenate([r[0, sub * QT:(sub + 1) * QT, :].T for r in q_refs], axis=1).astype(BF16)


def _softmax_step(s, vt, state):
    m, l, acc = state
    m_new = jnp.maximum(m, jnp.max(s, axis=0, keepdims=True))
    alpha = jnp.exp2(m - m_new)
    p = jnp.exp2(s - m_new)
    l = alpha * l + jnp.sum(p, axis=0, keepdims=True)
    acc = alpha * acc + _dot(vt, p.astype(BF16))
    return m_new, l, acc


def _finish_t(state, gate_ref, branch, o_ref, sub=0, others=()):
    _, l, acc = state
    o = acc / jnp.where(l == 0.0, 1.0, l)
    rows = slice(sub * QT, (sub + 1) * QT)
    gt = jax.nn.sigmoid(gate_ref[0, rows, :])
    for r in range(NSA_REP):
        col = 3 * r + branch
        lanes = slice(r * LANE, (r + 1) * LANE)
        val = gt[:, col:col + 1] * o[:, r * QT:(r + 1) * QT].T
        for other in others:
            val = val + other[rows, lanes]
        o_ref[rows, lanes] = val.astype(o_ref.dtype)


def _init_state(cols):
    return (jnp.full((1, cols), NEG, F32), jnp.zeros((1, cols), F32), jnp.zeros((LANE, cols), F32))


def _win_kernel(q0, q1, q2, q3, k_ref, v_ref, gate_ref, wb_ref, o_ref, kb_ref, vt_ref, *, seq):
    step = pl.program_id(2)
    cols = NSA_REP * QT
    wt = WINDOW // QT

    @pl.when(step == 0)
    def _():
        kb_ref[0:WINDOW, :] = jnp.zeros((WINDOW, LANE), BF16)
        for i in range(wt):
            vt_ref[i] = jnp.zeros((LANE, QT), BF16)

        def fill(i, carry):
            r0 = pl.multiple_of(i * QT, QT)
            kb_ref[pl.ds(WINDOW + r0, QT), :] = (k_ref[0, pl.ds(r0, QT), :] * SCALE_LOG2).astype(BF16)
            vt_ref[wt + i] = v_ref[0, pl.ds(r0, QT), :].T.astype(BF16)
            return carry

        lax.fori_loop(0, seq // QT, fill, 0)

    rho = lax.broadcasted_iota(jnp.int32, (WINDOW + QT, cols), 0)
    subs = range(WIN_SUB)
    tis = [step * WIN_SUB + sub for sub in subs]
    qts = [_q_transposed((q0, q1, q2, q3), sub) for sub in subs]
    ss = [_dot(kb_ref[pl.ds(pl.multiple_of(tis[u] * QT, QT), WINDOW + QT), :], qts[u]) + wb_ref[0]
          for u in subs]
    ss = [jnp.where(rho >= (wt - tis[u]) * QT, ss[u], NEG) for u in subs]
    ms = [jnp.max(ss[u], axis=0, keepdims=True) for u in subs]
    ps = [jnp.exp2(ss[u] - ms[u]) for u in subs]
    ls = [jnp.sum(ps[u], axis=0, keepdims=True) for u in subs]
    pbs = [ps[u].astype(BF16) for u in subs]
    accs = [_dot(vt_ref[tis[u]], pbs[u][0:QT]) for u in subs]
    for d in range(1, wt + 1):
        accs = [accs[u] + _dot(vt_ref[tis[u] + d], pbs[u][d * QT:(d + 1) * QT]) for u in subs]
    for u in subs:
        _finish_t((ms[u], ls[u], accs[u]), gate_ref, 2, o_ref, u)


def _window(proj, wb, batch, seq):
    nt = proj.shape[1]
    rows = WIN_SUB * QT
    nqt = seq // rows
    cols = NSA_REP * QT
    slab = lambda cb0: pl.BlockSpec((1, seq, LANE), lambda b, g, t: (cb0 + g, b, 0))
    return pl.pallas_call(
        functools.partial(_win_kernel, seq=seq),
        grid=(batch, NSA_GROUPS, nqt),
        in_specs=_q_specs(nqt, rows) + [
            slab(CB_KW), slab(CB_VW),
            pl.BlockSpec((1, rows, LANE), lambda b, g, t: (CB_GATE + g, b * nqt + t, 0)),
            pl.BlockSpec((1, WINDOW + QT, cols), lambda b, g, t: (g, 0, 0))],
        out_specs=pl.BlockSpec((rows, cols), lambda b, g, t: (b * nqt + t, g)),
        out_shape=jax.ShapeDtypeStruct((nt, NSA_HEADS * LANE), F32),
        scratch_shapes=[pltpu.VMEM((seq + WINDOW, LANE), BF16),
                        pltpu.VMEM((seq // QT + WINDOW // QT, LANE, QT), BF16)],
        compiler_params=_cparams(("arbitrary", "arbitrary", "arbitrary")),
        name="nsa_window",
    )(proj, proj, proj, proj, proj, proj, proj, wb)


def _softmax_steps(ss, vt, states):
    n = range(len(ss))
    m_new = [jnp.maximum(states[u][0], jnp.max(ss[u], axis=0, keepdims=True)) for u in n]
    alpha = [jnp.exp2(states[u][0] - m_new[u]) for u in n]
    p = [jnp.exp2(ss[u] - m_new[u]) for u in n]
    l = [alpha[u] * states[u][1] + jnp.sum(p[u], axis=0, keepdims=True) for u in n]
    pv = [_dot(vt, p[u].astype(BF16)) for u in n]
    return [(m_new[u], l[u], alpha[u] * states[u][2] + pv[u]) for u in n]


def _sel_kernel(q0, q1, q2, q3, k_ref, v_ref, sel_ref, gate_ref, nd_ref, npv_ref, oc_ref, ow_ref, o_ref,
                kb_ref, vt_ref, qa_ref, m_ref, l_ref, acc_ref, *, seq):
    step = pl.program_id(2)
    cols = NSA_REP * QT
    spb = KEY_BLK // SLC_BLOCK
    subs = range(SEL_SUB)

    @pl.when(step == 0)
    def _():
        blk = lax.broadcasted_iota(jnp.int32, (KEY_BLK, LANE), 0) // SLC_BLOCK
        onehot = jnp.where(blk == lax.broadcasted_iota(jnp.int32, (KEY_BLK, LANE), 1), 1.0, 0.0).astype(BF16)

        def fill(i, carry):
            r0 = pl.multiple_of(i * KEY_BLK, KEY_BLK)
            kb_ref[pl.ds(r0, KEY_BLK), 0:LANE] = (k_ref[0, pl.ds(r0, KEY_BLK), :] * SCALE_LOG2).astype(BF16)
            kb_ref[pl.ds(r0, KEY_BLK), LANE:2 * LANE] = onehot
            for c in range(SEL_SUB):
                vt_ref[i, :, c * QT:(c + 1) * QT] = v_ref[0, pl.ds(r0 + c * QT, QT), :].T.astype(BF16)
            return carry

        lax.fori_loop(0, seq // KEY_BLK, fill, 0)
        qa_ref[...] = jnp.zeros_like(qa_ref)

    for u in subs:
        qa_ref[u, 0:LANE, :] = _q_transposed((q0, q1, q2, q3), u)

    def block(kb):
        r0 = pl.multiple_of(kb * KEY_BLK, KEY_BLK)
        return kb_ref[pl.ds(r0, KEY_BLK), :], vt_ref[kb]

    pad_rows = jnp.zeros((SEL_ROWS - spb, cols), F32)

    def scores(kb, k):
        for u in subs:
            rows = sel_ref[0, 0, pl.ds(pl.multiple_of(kb * spb, spb), spb), u * QT:(u + 1) * QT]
            rows = jnp.concatenate([rows] * NSA_REP, axis=1)
            qa_ref[u, LANE:LANE + SEL_ROWS, :] = jnp.concatenate([rows, pad_rows], axis=0).astype(BF16)
        return [_dot(k, qa_ref[u]) for u in subs]

    k, vt = block(step)
    ss = scores(step, k)
    ss = [ss[u] + nd_ref[0, u] for u in subs]
    states = _softmax_steps(ss, vt, [_init_state(cols) for _ in subs])

    prev = jnp.maximum(step - 1, 0)
    k, vt = block(prev)
    ss = scores(prev, k)
    ss[0] = ss[0] + npv_ref[0]
    exists = (step + jnp.zeros((KEY_BLK, cols), jnp.int32)) >= 1
    ss = [jnp.where(exists, ss[u], NEG) for u in subs]
    states = _softmax_steps(ss, vt, states)
    for u in subs:
        m_ref[u], l_ref[u], acc_ref[u] = states[u]

    def body(kb, carry):
        k, vt = block(kb)
        st = _softmax_steps(scores(kb, k), vt, [(m_ref[u], l_ref[u], acc_ref[u]) for u in subs])
        for u in subs:
            m_ref[u], l_ref[u], acc_ref[u] = st[u]
        return carry

    lax.fori_loop(0, jnp.maximum(step - 1, 0), body, 0)
    for u in subs:
        _finish_t((m_ref[u], l_ref[u], acc_ref[u]), gate_ref, 1, o_ref, u, others=(oc_ref, ow_ref))


def _selected(proj, sel_t, nd, npv, o_c, o_w, batch, seq):
    nt = proj.shape[1]
    rows = SEL_SUB * QT
    nsteps = seq // rows
    cols = NSA_REP * QT
    slab = lambda cb0: pl.BlockSpec((1, seq, LANE), lambda b, g, t: (cb0 + g, b, 0))
    return pl.pallas_call(
        functools.partial(_sel_kernel, seq=seq),
        grid=(batch, NSA_GROUPS, nsteps),
        in_specs=_q_specs(nsteps, rows) + [
            slab(CB_KS), slab(CB_VS),
            pl.BlockSpec((1, 1, LANE, rows), lambda b, g, t: (b, g, 0, t)),
            pl.BlockSpec((1, rows, LANE), lambda b, g, t: (CB_GATE + g, b * nsteps + t, 0)),
            pl.BlockSpec((1, SEL_SUB, KEY_BLK, cols), lambda b, g, t: (g, 0, 0, 0)),
            pl.BlockSpec((1, KEY_BLK, cols), lambda b, g, t: (g, 0, 0)),
            pl.BlockSpec((rows, cols), lambda b, g, t: (b * nsteps + t, g)),
            pl.BlockSpec((rows, cols), lambda b, g, t: (b * nsteps + t, g))],
        out_specs=pl.BlockSpec((rows, cols), lambda b, g, t: (b * nsteps + t, g)),
        out_shape=jax.ShapeDtypeStruct((nt, NSA_HEADS * LANE), BF16),
        scratch_shapes=[pltpu.VMEM((seq, 2 * LANE), BF16),
                        pltpu.VMEM((seq // KEY_BLK, LANE, KEY_BLK), BF16),
                        pltpu.VMEM((SEL_SUB, 2 * LANE, cols), BF16),
                        pltpu.VMEM((SEL_SUB, 1, cols), F32), pltpu.VMEM((SEL_SUB, 1, cols), F32),
                        pltpu.VMEM((SEL_SUB, LANE, cols), F32)],
        compiler_params=_cparams(("arbitrary", "arbitrary", "arbitrary")),
        name="nsa_selected",
    )(proj, proj, proj, proj, proj, proj, sel_t, proj, nd, npv, o_c, o_w)


def _merge_kernel(oh_ref, on_ref, mgh_ref, mgn_ref, x_ref, mod_ref,
                  wh_ref, wn_ref, wo_ref, g_ref, b_ref, wr_ref, br_ref,
                  x1_ref, h2_ref, lg_ref):
    nblk = D_MODEL // LANE
    tm = x_ref.shape[0]
    halves = [slice(s * tm // MERGE_SPLIT, (s + 1) * tm // MERGE_SPLIT) for s in range(MERGE_SPLIT)]
    a_h = [_dot(oh_ref[rs, :], wh_ref[...]) for rs in halves]
    a_n = [_dot(on_ref[rs, :], wn_ref[...]) for rs in halves]
    gh = [jnp.concatenate([mgh_ref[c, rs, :] for c in range(nblk)], axis=-1) for rs in halves]
    gn = [jnp.concatenate([mgn_ref[c, rs, :] for c in range(nblk)], axis=-1) for rs in halves]
    merged = [(jax.nn.sigmoid(gh[s]) * a_h[s] + jax.nn.sigmoid(gn[s]) * a_n[s]).astype(BF16)
              for s in range(MERGE_SPLIT)]
    y = [(1.0 + mod_ref[0, 2:3, :]) * _dot(merged[s], wo_ref[...]) for s in range(MERGE_SPLIT)]
    x1 = [_layer_norm(ALPHA * x_ref[rs, :] + y[s]) * g_ref[...] + b_ref[...] for s, rs in enumerate(halves)]
    h2 = [_layer_norm(x1[s]) * (1.0 + mod_ref[0, 4:5, :]) + mod_ref[0, 3:4, :] for s in range(MERGE_SPLIT)]
    lg = [jnp.dot(h2[s], wr_ref[...], precision=lax.Precision.HIGHEST, preferred_element_type=F32) + br_ref[...]
          for s in range(MERGE_SPLIT)]
    for s, rs in enumerate(halves):
        x1_ref[rs, :] = x1[s]
        _store_rows(h2_ref, h2[s], rs.start)
        lg_ref[rs, :] = lg[s]


def _merge(o_h, o_n, proj, x2, mod, w_h, w_n, w_o, ln_g, ln_b, w_r, b_r, seq):
    nt, d = x2.shape
    tm = min(512, seq)
    nblk = d // LANE
    row = lambda w: pl.BlockSpec((tm, w), lambda i: (i, 0))
    full = lambda a: pl.BlockSpec(a.shape, lambda i: (0,) * a.ndim)
    return pl.pallas_call(
        _merge_kernel,
        grid=(nt // tm,),
        in_specs=[row(d), row(d),
                  pl.BlockSpec((nblk, tm, LANE), lambda i: (CB_MGH // nblk, i, 0)),
                  pl.BlockSpec((nblk, tm, LANE), lambda i: (CB_MGN // nblk, i, 0)),
                  row(d),
                  pl.BlockSpec((1, 6, d), lambda i: (i * tm // seq, 0, 0)),
                  full(w_h), full(w_n), full(w_o), full(ln_g), full(ln_b), full(w_r), full(b_r)],
        out_specs=[row(d), pl.BlockSpec((tm * ROW_TILES, LANE), lambda i: (i, 0)), row(LANE)],
        out_shape=[jax.ShapeDtypeStruct((nt, d), F32),
                   jax.ShapeDtypeStruct((nt * ROW_TILES, LANE), F32),
                   jax.ShapeDtypeStruct((nt, LANE), F32)],
        compiler_params=_cparams(("arbitrary",)),
        name="merge_outproj",
    )(o_h, o_n, proj, proj, x2, mod, w_h, w_n, w_o, ln_g, ln_b, w_r, b_r)


def _route_kernel(lg_ref, rec_ref, cnt_ref, rect_ref, carry_ref, *, tm):
    @pl.when(pl.program_id(0) == 0)
    def _():
        carry_ref[...] = jnp.zeros_like(carry_ref)

    lg = lg_ref[...]
    lane = lax.broadcasted_iota(jnp.int32, (tm, LANE), 1).astype(F32)
    far = float(LANE)
    gmask = lane < N_GROUPS
    gl = jnp.where(gmask, lg, -jnp.inf)
    gmax = jnp.max(gl, axis=-1, keepdims=True)
    gsum = jnp.sum(jnp.where(gmask, jnp.exp(gl - gmax), 0.0), axis=-1, keepdims=True)
    grp_p = 1.0 / gsum
    gidx = jnp.min(jnp.where(gl == gmax, lane, far), axis=-1, keepdims=True)
    lo = N_GROUPS + EXP_PER_GROUP * gidx
    emask = (lane >= lo) & (lane < lo + EXP_PER_GROUP)
    el = jnp.where(emask, lg, -jnp.inf)
    m1 = jnp.max(el, axis=-1, keepdims=True)
    i1 = jnp.min(jnp.where(el == m1, lane, far), axis=-1, keepdims=True)
    el2 = jnp.where(lane == i1, -jnp.inf, el)
    m2 = jnp.max(el2, axis=-1, keepdims=True)
    i2 = jnp.min(jnp.where(emask & (lane != i1) & (el2 == m2), lane, far), axis=-1, keepdims=True)
    e = jnp.exp(m2 - m1)
    w0 = grp_p / (1.0 + e)
    w1 = grp_p * e / (1.0 + e)

    oh0 = lane == i1
    oh1 = lane == i2
    f0 = jnp.where(oh0, 1.0, 0.0)
    f1 = jnp.where(oh1, 1.0, 0.0)
    ri = lax.broadcasted_iota(jnp.int32, (tm, tm), 0)
    ci = lax.broadcasted_iota(jnp.int32, (tm, tm), 1)
    before = jnp.where(ci < ri, 1.0, 0.0).astype(BF16)
    cum0 = _dot(before, f0.astype(BF16))
    cum1 = _dot(before, f1.astype(BF16))
    tot0 = jnp.sum(f0, axis=0, keepdims=True)
    tot1 = jnp.sum(f1, axis=0, keepdims=True)
    carry = carry_ref[...]
    rank0 = jnp.sum(jnp.where(oh0, carry + cum0, 0.0), axis=-1, keepdims=True)
    rank1 = jnp.sum(jnp.where(oh1, carry + tot0 + cum1, 0.0), axis=-1, keepdims=True)
    carry = carry + tot0 + tot1
    carry_ref[...] = carry
    cnt_ref[...] = carry

    rec = jnp.where(lane == 0, i1 - N_GROUPS, 0.0)
    rec = jnp.where(lane == 1, i2 - N_GROUPS, rec)
    rec = jnp.where(lane == 2, w0, rec)
    rec = jnp.where(lane == 3, w1, rec)
    rec = jnp.where(lane == 4, rank0, rec)
    rec = jnp.where(lane == 5, rank1, rec)
    rec_ref[...] = rec
    rect_ref[...] = jnp.concatenate([rec[c * LANE:(c + 1) * LANE, :].T[0:8, :] for c in range(tm // LANE)], axis=1)


def _route(logits):
    nt = logits.shape[0]
    tm = min(512, nt)
    return pl.pallas_call(
        functools.partial(_route_kernel, tm=tm),
        grid=(nt // tm,),
        in_specs=[pl.BlockSpec((tm, LANE), lambda i: (i, 0))],
        out_specs=[pl.BlockSpec((tm, LANE), lambda i: (i, 0)),
                   pl.BlockSpec((1, LANE), lambda i: (0, 0)),
                   pl.BlockSpec((8, tm), lambda i: (0, i))],
        out_shape=[jax.ShapeDtypeStruct((nt, LANE), F32),
                   jax.ShapeDtypeStruct((1, LANE), F32),
                   jax.ShapeDtypeStruct((8, nt), F32)],
        scratch_shapes=[pltpu.VMEM((1, LANE), F32)],
        compiler_params=_cparams(("arbitrary",)),
        name="moe_route",
    )(logits)


def _row_copy(src, dst, sem):
    return pltpu.make_async_copy(src, dst, sem)


def _tile_of(r):
    return pl.ds(pl.multiple_of(r * ROW_TILES, ROW_TILES), ROW_TILES)


def _load_rows(ref, n, lead=()):
    return jnp.concatenate([ref[lead + (pl.ds(c, n, stride=ROW_TILES), slice(None))]
                            for c in range(ROW_TILES)], axis=1)


def _store_rows(ref, val, row0=0):
    n = val.shape[0]
    for c in range(ROW_TILES):
        ref[pl.ds(row0 * ROW_TILES + c, n, stride=ROW_TILES), :] = val[:, c * LANE:(c + 1) * LANE]


def _dispatch_kernel(dest0_ref, dest1_ref, zb_ref, h_ref, xp_ref, z_ref, sem, zsem, *, tm):
    step = pl.program_id(0)
    base = step * tm
    blk = MOE_ROWS * ROW_TILES

    @pl.when(step == 0)
    def _():
        z_ref[...] = jnp.zeros_like(z_ref)

        def zero_copy(j):
            b = jnp.maximum(zb_ref[j], 0)
            return _row_copy(z_ref, xp_ref.at[pl.ds(pl.multiple_of(b * blk, blk), blk), :], zsem)

        def start(j, carry):
            @pl.when(zb_ref[j] >= 0)
            def _():
                zero_copy(j).start()
            return carry

        def wait(j, carry):
            @pl.when(zb_ref[j] >= 0)
            def _():
                zero_copy(j).wait()
            return carry

        lax.fori_loop(0, 2 * N_EXPERTS, start, 0)
        lax.fori_loop(0, 2 * N_EXPERTS, wait, 0)

    def issue(r, carry):
        for dest_ref in (dest0_ref, dest1_ref):
            d = dest_ref[base + r]
            _row_copy(h_ref.at[_tile_of(r), :], xp_ref.at[_tile_of(d), :], sem).start()
        return carry

    lax.fori_loop(0, tm, issue, 0, unroll=ROW_DMA_UNROLL)
    for _ in range(2):
        _row_copy(h_ref, xp_ref.at[pl.ds(0, tm * ROW_TILES), :], sem).wait()


def _dispatch(dest0, dest1, zero_blocks, h2, n_blocks):
    nt = h2.shape[0] // ROW_TILES
    tm = min(256, nt)
    return pl.pallas_call(
        functools.partial(_dispatch_kernel, tm=tm),
        grid_spec=pltpu.PrefetchScalarGridSpec(
            num_scalar_prefetch=3,
            grid=(nt // tm,),
            in_specs=[pl.BlockSpec((tm * ROW_TILES, LANE), lambda i, *_: (i, 0))],
            out_specs=pl.BlockSpec(memory_space=pl.ANY),
            scratch_shapes=[pltpu.VMEM((MOE_ROWS * ROW_TILES, LANE), F32),
                            pltpu.SemaphoreType.DMA(()), pltpu.SemaphoreType.DMA(())]),
        out_shape=jax.ShapeDtypeStruct((n_blocks * MOE_ROWS * ROW_TILES, LANE), F32),
        compiler_params=_cparams(("arbitrary",)),
        name="moe_dispatch",
    )(dest0, dest1, zero_blocks, h2)


def _expert_kernel(be_ref, nu_ref, x_ref, w1_ref, w3_ref, w2_ref, y_ref):
    del be_ref
    i = pl.program_id(0)

    @pl.when(i < nu_ref[0])
    def _():
        xb = _load_rows(x_ref, MOE_ROWS).astype(BF16)
        a = _dot(xb, w1_ref[0].astype(BF16))
        b = _dot(xb, w3_ref[0].astype(BF16))
        hmid = (a * jax.nn.sigmoid(a) * b).astype(BF16)
        _store_rows(y_ref, _dot(hmid, w2_ref[0].astype(BF16)))

    @pl.when(i >= nu_ref[0])
    def _():
        y_ref[...] = jnp.zeros_like(y_ref)


def _experts(block_expert, n_used, x_pad, w1, w3, w2):
    d, de = w1.shape[1], w1.shape[2]
    nb = x_pad.shape[0] // (MOE_ROWS * ROW_TILES)
    return pl.pallas_call(
        _expert_kernel,
        grid_spec=pltpu.PrefetchScalarGridSpec(
            num_scalar_prefetch=2,
            grid=(nb,),
            in_specs=[pl.BlockSpec((MOE_ROWS * ROW_TILES, LANE), lambda i, be, nu: (i, 0)),
                      pl.BlockSpec((1, d, de), lambda i, be, nu: (be[i], 0, 0)),
                      pl.BlockSpec((1, d, de), lambda i, be, nu: (be[i], 0, 0)),
                      pl.BlockSpec((1, de, d), lambda i, be, nu: (be[i], 0, 0))],
            out_specs=pl.BlockSpec((MOE_ROWS * ROW_TILES, LANE), lambda i, be, nu: (i, 0))),
        out_shape=jax.ShapeDtypeStruct(x_pad.shape, F32),
        compiler_params=_cparams(("arbitrary",)),
        name="moe_experts",
    )(block_expert, n_used, x_pad, w1, w3, w2)


def _combine_kernel(dest0_ref, dest1_ref, yp_ref, rec_ref, x1_ref, mod_ref, g_ref, b_ref, o_ref,
                    buf_ref, sem, *, tm):
    step = pl.program_id(0)
    slot = step % 2

    def fetch(tile, to_slot):
        def issue(r, carry):
            for k, dest_ref in enumerate((dest0_ref, dest1_ref)):
                d = dest_ref[tile * tm + r]
                _row_copy(yp_ref.at[_tile_of(d), :], buf_ref.at[to_slot, k, _tile_of(r), :],
                          sem.at[to_slot]).start()
            return carry

        lax.fori_loop(0, tm, issue, 0, unroll=ROW_DMA_UNROLL)

    @pl.when(step == 0)
    def _():
        fetch(0, 0)

    @pl.when(step + 1 < pl.num_programs(0))
    def _():
        fetch(step + 1, 1 - slot)

    for k in range(2):
        _row_copy(yp_ref.at[pl.ds(0, tm * ROW_TILES), :], buf_ref.at[slot, k], sem.at[slot]).wait()

    rec = rec_ref[...]
    y = rec[:, 2:3] * _load_rows(buf_ref, tm, (slot, 0)) + rec[:, 3:4] * _load_rows(buf_ref, tm, (slot, 1))
    y = (1.0 + mod_ref[0, 5:6, :]) * y
    o_ref[...] = _layer_norm(ALPHA * x1_ref[...] + y) * g_ref[...] + b_ref[...]


def _combine(dest0, dest1, y_pad, rec, x1, mod, ln_g, ln_b, seq):
    nt, d = x1.shape
    tm = min(256, seq)
    return pl.pallas_call(
        functools.partial(_combine_kernel, tm=tm),
        grid_spec=pltpu.PrefetchScalarGridSpec(
            num_scalar_prefetch=2,
            grid=(nt // tm,),
            in_specs=[pl.BlockSpec(memory_space=pl.ANY),
                      pl.BlockSpec((tm, LANE), lambda i, *_: (i, 0)),
                      pl.BlockSpec((tm, d), lambda i, *_: (i, 0)),
                      pl.BlockSpec((1, 6, d), lambda i, *_: (i * tm // seq, 0, 0)),
                      pl.BlockSpec((1, d), lambda i, *_: (0, 0)),
                      pl.BlockSpec((1, d), lambda i, *_: (0, 0))],
            out_specs=pl.BlockSpec((tm, d), lambda i, *_: (i, 0)),
            scratch_shapes=[pltpu.VMEM((2, 2, tm * ROW_TILES, LANE), F32), pltpu.SemaphoreType.DMA((2,))]),
        out_shape=jax.ShapeDtypeStruct((nt, d), F32),
        compiler_params=_cparams(("arbitrary",)),
        name="moe_combine",
    )(dest0, dest1, y_pad, rec, x1, mod, ln_g, ln_b)


def _rel_bucket(dist):
    n = jnp.maximum(dist, 0)
    max_exact = REL_BUCKETS // 2
    nf = jnp.maximum(n, 1).astype(F32)
    large = max_exact + (jnp.log(nf / max_exact) / math.log(REL_MAX_DIST / max_exact)
                         * (REL_BUCKETS - max_exact)).astype(jnp.int32)
    large = jnp.minimum(large, REL_BUCKETS - 1)
    return jnp.where(n < max_exact, n, large)


def _bias_tables(rel_bias, seq):
    bucket_onehot = (_rel_bucket(jnp.arange(LANE))[:, None] == jnp.arange(REL_BUCKETS)).astype(F32)
    tab_d = jnp.einsum('db,hb->hd', bucket_onehot, rel_bias,
                       precision=lax.Precision.HIGHEST)
    tok = np.arange(QT)[None, :]
    key = np.arange(LANE)[:, None]
    far = tab_d[:, LANE - 1]
    cols = NSA_REP * QT

    def transposed(dist):
        idx = jnp.asarray(np.clip(dist, 0, LANE - 1).astype(np.int32))
        onehot = (idx[..., None] == jnp.arange(LANE, dtype=jnp.int32)).astype(F32)
        t = jnp.einsum('ijd,hd->hij', onehot, tab_d, precision=lax.Precision.HIGHEST)
        t = (t - far[:, None, None]) * LOG2E
        t = t.reshape(NSA_GROUPS, NSA_REP, LANE, QT).transpose(0, 2, 1, 3)
        return t.reshape(NSA_GROUPS, LANE, cols)

    t0t = transposed(tok - key)
    t1t = transposed(tok - key + QT)

    ns = seq // CMP_STRIDE
    d_c = tok - CMP_STRIDE * key + (CMP_STRIDE * CMP_PAD - (CMP_BLOCK - 1))
    seen = np.tile(d_c >= 0, (1, NSA_REP))
    recent = jnp.where(seen[None], transposed(d_c), NEG)
    tt = jnp.concatenate([jnp.zeros((NSA_GROUPS, ns, cols), F32), recent,
                          jnp.full((NSA_GROUPS, ns, cols), NEG, F32)], axis=1)

    rho = np.arange(WINDOW + QT)[:, None]
    tok_w = np.tile(np.arange(QT), NSA_REP)[None, :]
    band = (rho > tok_w) & (rho <= tok_w + WINDOW)
    rows = jnp.concatenate([jnp.zeros((NSA_GROUPS, WINDOW - QT, NSA_REP * QT), F32), t1t, t0t], axis=1)
    wb = jnp.where(band[None], rows, NEG)

    zeros = lambda n: jnp.zeros((NSA_GROUPS, n * QT, cols), F32)
    negs = lambda n: jnp.full((NSA_GROUPS, n * QT, cols), NEG, F32)
    diag = jnp.where(np.tile(tok >= key, (1, NSA_REP))[None], t0t, NEG)
    nd = jnp.stack([jnp.concatenate(([zeros(u - 1), t1t] if u else []) + [diag, negs(SEL_SUB - 1 - u)], axis=1)
                    for u in range(SEL_SUB)], axis=1)
    npv = jnp.concatenate([zeros(SEL_SUB - 1), t1t], axis=1)
    return nd, npv, wb, tt


def _overlap_matrix(seq):
    ns = seq // CMP_STRIDE
    nslc = seq // SLC_BLOCK
    ov = np.zeros((LANE, ns), np.float32)
    cs = np.arange(ns - 1) * CMP_STRIDE
    ss = np.arange(nslc) * SLC_BLOCK
    ov[:nslc, :ns - 1] = ((cs[None, :] < ss[:, None] + SLC_BLOCK) & (cs[None, :] + CMP_BLOCK > ss[:, None]))
    return jnp.asarray(ov, BF16)


def _reorder_cols(a):
    lead = a.shape[:-1]
    gate = a[..., MAIN_COLS:MAIN_COLS + GATE_COLS]
    per = GATE_COLS // NSA_GROUPS
    gate_blocks = []
    for g in range(NSA_GROUPS):
        gate_blocks.append(gate[..., g * per:(g + 1) * per])
        gate_blocks.append(jnp.zeros(lead + (LANE - per,), a.dtype))
    pad = jnp.zeros(lead + ((CB_MGH - CB_GATE - NSA_GROUPS) * LANE,), a.dtype)
    return jnp.concatenate([a[..., :MAIN_COLS]] + gate_blocks + [pad, a[..., MAIN_COLS + GATE_COLS:]], axis=-1)


def kernel(x, c, ada_w, ada_b, w_in, b_in, hg_lb_logits, hg_norm_w, cmp_pos_k, cmp_w1_k, cmp_b1_k, cmp_w2_k, cmp_pos_v, cmp_w1_v, cmp_b1_v, cmp_w2_v, rel_bias, w_br_hg, w_br_nsa, w_out, ln1_g, ln1_b, router_grp_w, router_grp_b, router_exp_w, router_exp_b, exp_w1, exp_w3, exp_w2, ln2_g, ln2_b):
    batch, seq, d = x.shape
    nt = batch * seq
    assert d == D_MODEL and seq % 1024 == 0 and seq // SLC_BLOCK <= LANE
    l = 0
    x2 = x.reshape(nt, d)

    c_pad = jnp.zeros((8, d), F32).at[:batch].set(c)
    mod = _adaln(c_pad, ada_w[l], ada_b[l][None])[:batch].reshape(batch, 6, d)

    proj = _inproj(x2, mod, _reorder_cols(w_in[l]).astype(BF16), _reorder_cols(b_in[l])[None], seq)

    lb_all = jnp.cumsum(jax.nn.softmax(hg_lb_logits.astype(F32), axis=0), axis=0)
    o_h = _hgrn(proj, lb_all[l][None], hg_norm_w[l][None], batch, seq)

    kvc = _compress(proj, jnp.stack([cmp_pos_k[l], cmp_pos_v[l]]),
                    jnp.stack([cmp_w1_k[l], cmp_w1_v[l]]).astype(BF16),
                    jnp.stack([cmp_b1_k[l], cmp_b1_v[l]])[:, None, :],
                    jnp.stack([cmp_w2_k[l], cmp_w2_v[l]]).astype(BF16), batch, seq)

    nd, npv, wb, tt = _bias_tables(rel_bias, seq)
    o_c, sel_t = _cmp_sel(proj, kvc, tt, _overlap_matrix(seq), batch, seq)
    o_w = _window(proj, wb, batch, seq)
    o_n = _selected(proj, sel_t, nd, npv, o_c, o_w, batch, seq)

    w_r = jnp.zeros((d, LANE), F32).at[:, :N_GROUPS].set(router_grp_w[l])
    w_r = w_r.at[:, N_GROUPS:N_GROUPS + N_EXPERTS].set(router_exp_w[l])
    b_r = jnp.zeros((1, LANE), F32).at[0, :N_GROUPS].set(router_grp_b[l])
    b_r = b_r.at[0, N_GROUPS:N_GROUPS + N_EXPERTS].set(router_exp_b[l])
    x1, h2, logits = _merge(o_h, o_n, proj, x2, mod,
                            w_br_hg[l].astype(BF16), w_br_nsa[l].astype(BF16), w_out[l].astype(BF16),
                            ln1_g[l][None], ln1_b[l][None], w_r, b_r, seq)

    rec, cnt, rec_t = _route(logits)
    counts = cnt[0, N_GROUPS:N_GROUPS + N_EXPERTS].astype(jnp.int32)
    padded = (counts + MOE_ROWS - 1) // MOE_ROWS * MOE_ROWS
    pend = jnp.cumsum(padded)
    pstart = pend - padded
    n_assign = 2 * nt
    nb = n_assign // MOE_ROWS + N_EXPERTS
    slots = rec_t.astype(jnp.int32)
    expert_ids = jnp.arange(N_EXPERTS, dtype=jnp.int32)[:, None]
    slot_base = lambda e: jnp.sum(jnp.where(e[None, :] == expert_ids, pstart[:, None], 0), axis=0)
    dest0 = slot_base(slots[0]) + slots[4]
    dest1 = slot_base(slots[1]) + slots[5]
    block_start = jnp.arange(nb, dtype=jnp.int32) * MOE_ROWS
    block_expert = jnp.minimum(jnp.sum(pend[None, :] <= block_start[:, None], axis=1),
                               N_EXPERTS - 1).astype(jnp.int32)
    n_used = pend[-1] // MOE_ROWS
    spare = n_used + jnp.arange(N_EXPERTS, dtype=jnp.int32)
    zero_blocks = jnp.concatenate([jnp.where(padded > 0, pend // MOE_ROWS - 1, -1),
                                   jnp.where(spare < nb, spare, -1)]).astype(jnp.int32)

    x_pad = _dispatch(dest0, dest1, zero_blocks, h2, nb)
    y_pad = _experts(block_expert, n_used[None].astype(jnp.int32), x_pad, exp_w1[l], exp_w3[l], exp_w2[l])
    out = _combine(dest0, dest1, y_pad, rec, x1, mod, ln2_g[l][None], ln2_b[l][None], seq)
    return out.reshape(batch, seq, d)
```

```python
import functools
import math

import numpy as np
import jax
import jax.numpy as jnp
from jax import lax
from jax.experimental import pallas as pl
from jax.experimental.pallas import tpu as pltpu

F32 = jnp.float32
BF16 = jnp.bfloat16

D_MODEL = 1024
HG_HEADS = 8
HG_DK = 128
HG_DV = 128
HG_CHUNK = 32
HG_SUPER = 256
NSA_HEADS = 8
NSA_GROUPS = 2
NSA_REP = NSA_HEADS // NSA_GROUPS
NSA_DK = 128
CMP_BLOCK = 32
CMP_STRIDE = 16
SLC_BLOCK = 64
SLC_TOPK = 16
WINDOW = 512
FORCE_SCORE = 1e4
REL_BUCKETS = 32
REL_MAX_DIST = 128
N_GROUPS = 4
EXP_PER_GROUP = 8
N_EXPERTS = N_GROUPS * EXP_PER_GROUP
D_EXPERT = D_MODEL // 2
DEPTH = 1
ALPHA = (2 * DEPTH) ** 0.25

LANE = 128
QT = 128
NEG = -1e30
SCALE = NSA_DK ** -0.5
LOG2E = math.log2(math.e)
SCALE_LOG2 = SCALE * LOG2E
KEY_BLK = 512
BOUND_SLACK = 1.0 + 2.0 ** -10
MAX_REF_GAP = 64.0
SEL_ROWS = 16
SEL_SUB = KEY_BLK // QT
WIN_SUB = 4
CMP_SUB = 4
CMP_PAD = 120
VMEM_LIMIT = 56 * 1024 * 1024

CB_HQ, CB_HF, CB_HI, CB_HG = 0, 8, 16, 24
CB_NQ = 32
CB_KC, CB_VC, CB_KS, CB_VS, CB_KW, CB_VW = 40, 42, 44, 46, 48, 50
CB_GATE = 52
CB_MGH, CB_MGN = 56, 64
NCB = 72
MAIN_COLS = 52 * LANE
GATE_COLS = 3 * NSA_HEADS

MERGE_SPLIT = 2
ROW_TILES = D_MODEL // LANE
MOE_ROWS = 512
ROW_DMA_UNROLL = 8


def _cparams(sem):
    return pltpu.CompilerParams(dimension_semantics=sem, vmem_limit_bytes=VMEM_LIMIT)


def _dot(a, b):
    return jnp.dot(a, b, preferred_element_type=F32)


def _dot_nt(a, b):
    return lax.dot_general(a, b, (((1,), (1,)), ((), ())), preferred_element_type=F32)


def _dot_tn(a, b):
    return lax.dot_general(a, b, (((0,), (0,)), ((), ())), preferred_element_type=F32)


def _split3(x):
    hi = x.astype(BF16)
    r = x - hi.astype(F32)
    mid = r.astype(BF16)
    lo = (r - mid.astype(F32)).astype(BF16)
    return hi, mid, lo


def _dot01(m01, x):
    hi, mid, lo = _split3(x)
    return _dot(m01, hi) + _dot(m01, mid) + _dot(m01, lo)


def _dot01_r(x, m01):
    hi, mid, lo = _split3(x)
    return _dot(hi, m01) + _dot(mid, m01) + _dot(lo, m01)


def _layer_norm(x, eps=1e-5):
    mu = jnp.mean(x, axis=-1, keepdims=True)
    xc = x - mu
    var = jnp.mean(xc * xc, axis=-1, keepdims=True)
    return xc * lax.rsqrt(var + eps)


def _adaln_kernel(c_ref, w_ref, b_ref, o_ref):
    c = c_ref[...]
    ca = c * jax.nn.sigmoid(c)
    o_ref[...] = jnp.dot(ca, w_ref[...], precision=lax.Precision.HIGHEST,
                         preferred_element_type=F32) + b_ref[...]


def _adaln(c_pad, w, b):
    rows, d = c_pad.shape
    n = w.shape[1]
    return pl.pallas_call(
        _adaln_kernel,
        grid=(n // d,),
        in_specs=[pl.BlockSpec((rows, d), lambda j: (0, 0)),
                  pl.BlockSpec((d, d), lambda j: (0, j)),
                  pl.BlockSpec((1, d), lambda j: (0, j))],
        out_specs=pl.BlockSpec((rows, d), lambda j: (0, j)),
        out_shape=jax.ShapeDtypeStruct((rows, n), F32),
        compiler_params=_cparams(("arbitrary",)),
        name="adaln",
    )(c_pad, w, b)


def _inproj_kernel(x_ref, mod_ref, w_ref, b_ref, o_ref, hn_ref, *, ncb_tile):
    @pl.when(pl.program_id(1) == 0)
    def _():
        hn = _layer_norm(x_ref[...])
        sh = mod_ref[0, 0:1, :]
        sc = mod_ref[0, 1:2, :]
        hn_ref[...] = (hn * (1.0 + sc) + sh).astype(BF16)

    res = _dot(hn_ref[...], w_ref[...]) + b_ref[...]
    for c in range(ncb_tile):
        o_ref[c] = res[:, c * LANE:(c + 1) * LANE]


def _inproj(x2, mod, w, b, seq):
    nt, d = x2.shape
    tm = min(2048, seq)
    ncb_tile = 6
    tn = ncb_tile * LANE
    return pl.pallas_call(
        functools.partial(_inproj_kernel, ncb_tile=ncb_tile),
        grid=(nt // tm, NCB // ncb_tile),
        in_specs=[pl.BlockSpec((tm, d), lambda i, j: (i, 0)),
                  pl.BlockSpec((1, 6, d), lambda i, j: (i * tm // seq, 0, 0)),
                  pl.BlockSpec((d, tn), lambda i, j: (0, j)),
                  pl.BlockSpec((1, tn), lambda i, j: (0, j))],
        out_specs=pl.BlockSpec((ncb_tile, tm, LANE), lambda i, j: (j, i, 0)),
        out_shape=jax.ShapeDtypeStruct((NCB, nt, LANE), F32),
        scratch_shapes=[pltpu.VMEM((tm, d), BF16)],
        compiler_params=_cparams(("arbitrary", "arbitrary")),
        name="inproj",
    )(x2, mod, w, b)


def _hgrn_kernel(q_ref, f_ref, v_ref, g_ref, lb_ref, nw_ref, o_ref, st_ref, *, rows):
    @pl.when(pl.program_id(1) == 0)
    def _():
        st_ref[...] = jnp.zeros_like(st_ref)

    sup = HG_SUPER
    ri = lax.broadcasted_iota(jnp.int32, (sup, sup), 0)
    ci = lax.broadcasted_iota(jnp.int32, (sup, sup), 1)
    same = (ri // HG_CHUNK) == (ci // HG_CHUNK)
    cum_m = jnp.where(same & (ci <= ri), 1.0, 0.0).astype(BF16)
    tot_m = jnp.where(same, 1.0, 0.0).astype(BF16)
    rt = lax.broadcasted_iota(jnp.int32, (LANE, LANE), 0)
    ct = lax.broadcasted_iota(jnp.int32, (LANE, LANE), 1)
    tril = ((rt // HG_CHUNK) == (ct // HG_CHUNK)) & (ct <= rt)
    per = sup // HG_CHUNK
    groups = [slice(g * LANE, (g + 1) * LANE) for g in range(sup // LANE)]

    heads = range(HG_HEADS)
    hs = [slice(h * LANE, (h + 1) * LANE) for h in heads]

    def wide(ref, r0):
        return jnp.concatenate([ref[h, pl.ds(r0, sup), :] for h in heads], axis=1)

    def body(i, carry):
        r0 = pl.multiple_of(i * sup, sup)
        lb = lb_ref[...]
        f = lb + (1.0 - lb) * jax.nn.sigmoid(wide(f_ref, r0))
        lf = jnp.log(f)
        k = 1.0 - f
        hi, mid, lo = _split3(lf)
        b = _dot(cum_m, hi) + _dot(cum_m, mid) + _dot(cum_m, lo)
        bl = _dot(tot_m, hi) + _dot(tot_m, mid) + _dot(tot_m, lo)
        q_in = (wide(q_ref, r0) * jnp.exp(b)).astype(BF16)
        k_in = (k * jnp.exp(-b)).astype(BF16)
        k_end = (k * jnp.exp(bl - b)).astype(BF16)
        vb = wide(v_ref, r0).astype(BF16)
        dec = [jnp.exp(bl[c * HG_CHUNK:c * HG_CHUNK + 1, :]) for c in range(per)]
        chunk = [slice(c * HG_CHUNK, (c + 1) * HG_CHUNK) for c in range(per)]

        att = [[jnp.where(tril, _dot_nt(q_in[g, hs[h]], k_in[g, hs[h]]), 0.0).astype(BF16) for g in groups]
               for h in heads]
        upd = [[_dot_tn(vb[chunk[c], hs[h]], k_end[chunk[c], hs[h]]) for c in range(per)] for h in heads]
        intra = [[_dot(att[h][n], vb[g, hs[h]]) for n, g in enumerate(groups)] for h in heads]

        st = [st_ref[h] for h in heads]
        inter = [[] for _ in heads]
        for c in range(per):
            for h in heads:
                inter[h].append(_dot_nt(q_in[chunk[c], hs[h]], st[h].astype(BF16)))
                st[h] = dec[c][:, hs[h]] * st[h] + upd[h][c]
        for h in heads:
            st_ref[h] = st[h]

        o = [jnp.concatenate(intra[h], axis=0) + jnp.concatenate(inter[h], axis=0) for h in heads]
        scale = [lax.rsqrt(jnp.mean(o[h] * o[h], axis=-1, keepdims=True) + 1e-6) for h in heads]
        on = jnp.concatenate([o[h] * scale[h] for h in heads], axis=1)
        on = on * nw_ref[...] * jax.nn.sigmoid(wide(g_ref, r0))
        o_ref[pl.ds(r0, sup), :] = on.astype(BF16)
        return carry

    lax.fori_loop(0, rows // sup, body, 0)


def _hgrn(proj, lb, nw, batch, seq):
    nt = proj.shape[1]
    tb = min(512, seq)
    nblk = seq // tb

    def slab(cb0):
        return pl.BlockSpec((HG_HEADS, tb, LANE), lambda b, t: (cb0 // HG_HEADS, b * nblk + t, 0))

    vec = pl.BlockSpec((1, HG_HEADS * LANE), lambda b, t: (0, 0))
    return pl.pallas_call(
        functools.partial(_hgrn_kernel, rows=tb),
        grid=(batch, nblk),
        in_specs=[slab(CB_HQ), slab(CB_HF), slab(CB_HI), slab(CB_HG), vec, vec],
        out_specs=pl.BlockSpec((tb, HG_HEADS * HG_DV), lambda b, t: (b * nblk + t, 0)),
        out_shape=jax.ShapeDtypeStruct((nt, HG_HEADS * HG_DV), BF16),
        scratch_shapes=[pltpu.VMEM((HG_HEADS, HG_DV, HG_DK), F32)],
        compiler_params=_cparams(("arbitrary", "arbitrary")),
        name="hgrn2",
    )(proj, proj, proj, proj, lb, nw)


def _compress_kernel(x_ref, pos_ref, w1_ref, b1_ref, w2_ref, o_ref, *, ns):
    p0 = jnp.zeros((ns, LANE), F32)
    p1 = jnp.zeros((ns, LANE), F32)
    for j in range(CMP_STRIDE):
        tok = x_ref[0, pl.ds(j, ns, stride=CMP_STRIDE), :]
        rows = slice(j * LANE, (j + 1) * LANE)
        late = slice((CMP_STRIDE + j) * LANE, (CMP_STRIDE + j + 1) * LANE)
        p0 = p0 + _dot((tok + pos_ref[0, j:j + 1, :]).astype(BF16), w1_ref[0, rows, :])
        p1 = p1 + _dot((tok + pos_ref[0, CMP_STRIDE + j:CMP_STRIDE + j + 1, :]).astype(BF16), w1_ref[0, late, :])
    h = p0 + pltpu.roll(p1, ns - 1, axis=0) + b1_ref[0]
    a = h * jax.nn.sigmoid(h)
    out = _dot(a.astype(BF16), w2_ref[0])
    row = lax.broadcasted_iota(jnp.int32, out.shape, 0)
    out = jnp.where(row < ns - 1, out, 0.0)
    o_ref[0, 0, 0] = out


def _compress(proj, pos, w1, b1, w2, batch, seq):
    ns = seq // CMP_STRIDE
    np_rows = ns
    width = CMP_STRIDE * LANE
    return pl.pallas_call(
        functools.partial(_compress_kernel, ns=ns),
        grid=(2, batch, NSA_GROUPS),
        in_specs=[pl.BlockSpec((1, seq, LANE), lambda s, b, g: (CB_KC + NSA_GROUPS * s + g, b, 0)),
                  pl.BlockSpec((1, CMP_BLOCK, LANE), lambda s, b, g: (s, 0, 0)),
                  pl.BlockSpec((1, 2 * width, LANE), lambda s, b, g: (s, 0, 0)),
                  pl.BlockSpec((1, 1, LANE), lambda s, b, g: (s, 0, 0)),
                  pl.BlockSpec((1, LANE, LANE), lambda s, b, g: (s, 0, 0))],
        out_specs=pl.BlockSpec((1, 1, 1, np_rows, LANE), lambda s, b, g: (s, b, g, 0, 0)),
        out_shape=jax.ShapeDtypeStruct((2, batch, NSA_GROUPS, np_rows, LANE), F32),
        compiler_params=_cparams(("arbitrary", "arbitrary", "arbitrary")),
        name="nsa_compress",
    )(proj, pos, w1, b1, w2)


def _cmp_sel_kernel(q0, q1, q2, q3, kc_ref, vc_ref, gate_ref, tt_ref, ov_ref, oc_ref, sel_ref,
                    kb_ref, vt_ref, *, np_rows):
    step = pl.program_id(2)
    cols = NSA_REP * QT
    ns = np_rows

    @pl.when(step == 0)
    def _():
        kb_ref[...] = (kc_ref[0, 0, 0] * SCALE_LOG2).astype(BF16)
        for c in range(np_rows // LANE):
            vt_ref[:, c * LANE:(c + 1) * LANE] = vc_ref[0, 0, 0, c * LANE:(c + 1) * LANE, :].T.astype(BF16)

    subs = range(CMP_SUB)
    tis = [step * CMP_SUB + u for u in subs]
    kb = kb_ref[...]
    qts = [_q_transposed((q0, q1, q2, q3), u) for u in subs]
    bias = [tt_ref[0, pl.ds(pl.multiple_of(ns + CMP_PAD - tis[u] * (QT // CMP_STRIDE), 8), np_rows), :]
            for u in subs]
    ss = [_dot(kb, qts[u]) + bias[u] for u in subs]
    ms = [jnp.max(ss[u], axis=0, keepdims=True) for u in subs]
    ps = [jnp.exp2(ss[u] - ms[u]) for u in subs]
    ls = [jnp.sum(ps[u], axis=0, keepdims=True) for u in subs]
    invs = [jnp.where(ms[u] > 0.5 * NEG, 1.0 / ls[u], 0.0) for u in subs]
    pn = [ps[u] * invs[u] for u in subs]
    vt = vt_ref[...]
    os_ = [_dot(vt, pn[u].astype(BF16)) for u in subs]
    for u in subs:
        rows = slice(u * QT, (u + 1) * QT)
        gt = jax.nn.sigmoid(gate_ref[0, rows, :])
        for r in range(NSA_REP):
            oc_ref[rows, r * LANE:(r + 1) * LANE] = gt[:, 3 * r:3 * r + 1] * os_[u][:, r * QT:(r + 1) * QT].T

    def group_sum(p):
        tot = p[:, 0:QT]
        for r in range(1, NSA_REP):
            tot = tot + p[:, r * QT:(r + 1) * QT]
        return tot

    imp = jnp.concatenate([_dot01(ov_ref[...], group_sum(pn[u])) for u in subs], axis=1)

    width = CMP_SUB * QT
    jj = lax.broadcasted_iota(jnp.int32, (LANE, width), 0)
    tok = step * width + lax.broadcasted_iota(jnp.int32, (LANE, width), 1)
    cur = tok // SLC_BLOCK
    forced = (jj == 0) | (jj == cur) | (jj == cur - 1)
    score = jnp.where(jj <= cur, jnp.where(forced, FORCE_SCORE, imp), -1.0)
    selb = jnp.full((LANE, width), NEG, F32)
    jf = jj.astype(F32)
    for _ in range(SLC_TOPK):
        mval = jnp.max(score, axis=0, keepdims=True)
        first = jnp.min(jnp.where(score == mval, jf, float(LANE)), axis=0, keepdims=True)
        pick = jf == first
        selb = jnp.where(pick, 0.0, selb)
        score = jnp.where(pick, -jnp.inf, score)
    sel_ref[0, 0] = selb


def _q_specs(nsteps, rows=QT):
    return [pl.BlockSpec((1, rows, LANE),
                         functools.partial(lambda b, g, t, r: (CB_NQ + NSA_REP * g + r, b * nsteps + t, 0), r=r))
            for r in range(NSA_REP)]


def _cmp_sel(proj, kvc, tt, ov_t, batch, seq):
    nt = proj.shape[1]
    rows = CMP_SUB * QT
    nsteps = seq // rows
    cols = NSA_REP * QT
    np_rows = kvc.shape[3]
    kv_spec = lambda s: pl.BlockSpec((1, 1, 1, np_rows, LANE), lambda b, g, t: (s, b, g, 0, 0))
    return pl.pallas_call(
        functools.partial(_cmp_sel_kernel, np_rows=np_rows),
        grid=(batch, NSA_GROUPS, nsteps),
        in_specs=_q_specs(nsteps, rows) + [
            kv_spec(0), kv_spec(1),
            pl.BlockSpec((1, rows, LANE), lambda b, g, t: (CB_GATE + g, b * nsteps + t, 0)),
            pl.BlockSpec((1, tt.shape[1], cols), lambda b, g, t: (g, 0, 0)),
            pl.BlockSpec((LANE, np_rows), lambda b, g, t: (0, 0))],
        out_specs=[pl.BlockSpec((rows, cols), lambda b, g, t: (b * nsteps + t, g)),
                   pl.BlockSpec((1, 1, LANE, rows), lambda b, g, t: (b, g, 0, t))],
        out_shape=[jax.ShapeDtypeStruct((nt, NSA_HEADS * LANE), F32),
                   jax.ShapeDtypeStruct((batch, NSA_GROUPS, LANE, seq), F32)],
        scratch_shapes=[pltpu.VMEM((np_rows, LANE), BF16), pltpu.VMEM((LANE, np_rows), BF16)],
        compiler_params=_cparams(("arbitrary", "arbitrary", "arbitrary")),
        name="nsa_cmp_select",
    )(proj, proj, proj, proj, kvc, kvc, proj, tt, ov_t)


def _q_transposed(q_refs, sub=0):
    return jnp.concatenate([r[0, sub * QT:(sub + 1) * QT, :].T for r in q_refs], axis=1).astype(BF16)


def _softmax_step(s, vt, state):
    m, l, acc = state
    m_new = jnp.maximum(m, jnp.max(s, axis=0, keepdims=True))
    alpha = jnp.exp2(m - m_new)
    p = jnp.exp2(s - m_new)
    l = alpha * l + jnp.sum(p, axis=0, keepdims=True)
    acc = alpha * acc + _dot(vt, p.astype(BF16))
    return m_new, l, acc


def _finish_t(state, gate_ref, branch, o_ref, sub=0, others=()):
    _, l, acc = state
    o = acc / jnp.where(l == 0.0, 1.0, l)
    rows = slice(sub * QT, (sub + 1) * QT)
    gt = jax.nn.sigmoid(gate_ref[0, rows, :])
    for r in range(NSA_REP):
        col = 3 * r + branch
        lanes = slice(r * LANE, (r + 1) * LANE)
        val = gt[:, col:col + 1] * o[:, r * QT:(r + 1) * QT].T
        for other in others:
            val = val + other[rows, lanes]
        o_ref[rows, lanes] = val.astype(o_ref.dtype)


def _init_state(cols):
    return (jnp.full((1, cols), NEG, F32), jnp.zeros((1, cols), F32), jnp.zeros((LANE, cols), F32))


def _win_kernel(q0, q1, q2, q3, k_ref, v_ref, gate_ref, wb_ref, o_ref, kb_ref, vt_ref, *, seq):
    step = pl.program_id(2)
    cols = NSA_REP * QT
    wt = WINDOW // QT

    @pl.when(step == 0)
    def _():
        kb_ref[0:WINDOW, :] = jnp.zeros((WINDOW, LANE), BF16)
        for i in range(wt):
            vt_ref[i] = jnp.zeros((LANE, QT), BF16)

        def fill(i, carry):
            r0 = pl.multiple_of(i * QT, QT)
            kb_ref[pl.ds(WINDOW + r0, QT), :] = (k_ref[0, pl.ds(r0, QT), :] * SCALE_LOG2).astype(BF16)
            vt_ref[wt + i] = v_ref[0, pl.ds(r0, QT), :].T.astype(BF16)
            return carry

        lax.fori_loop(0, seq // QT, fill, 0)

    rho = lax.broadcasted_iota(jnp.int32, (WINDOW + QT, cols), 0)
    subs = range(WIN_SUB)
    tis = [step * WIN_SUB + sub for sub in subs]
    qts = [_q_transposed((q0, q1, q2, q3), sub) for sub in subs]
    ss = [_dot(kb_ref[pl.ds(pl.multiple_of(tis[u] * QT, QT), WINDOW + QT), :], qts[u]) + wb_ref[0]
          for u in subs]
    ss = [jnp.where(rho >= (wt - tis[u]) * QT, ss[u], NEG) for u in subs]
    ms = [jnp.max(ss[u], axis=0, keepdims=True) for u in subs]
    ps = [jnp.exp2(ss[u] - ms[u]) for u in subs]
    ls = [jnp.sum(ps[u], axis=0, keepdims=True) for u in subs]
    pbs = [ps[u].astype(BF16) for u in subs]
    accs = [_dot(vt_ref[tis[u]], pbs[u][0:QT]) for u in subs]
    for d in range(1, wt + 1):
        accs = [accs[u] + _dot(vt_ref[tis[u] + d], pbs[u][d * QT:(d + 1) * QT]) for u in subs]
    for u in subs:
        _finish_t((ms[u], ls[u], accs[u]), gate_ref, 2, o_ref, u)


def _window(proj, wb, batch, seq):
    nt = proj.shape[1]
    rows = WIN_SUB * QT
    nqt = seq // rows
    cols = NSA_REP * QT
    slab = lambda cb0: pl.BlockSpec((1, seq, LANE), lambda b, g, t: (cb0 + g, b, 0))
    return pl.pallas_call(
        functools.partial(_win_kernel, seq=seq),
        grid=(batch, NSA_GROUPS, nqt),
        in_specs=_q_specs(nqt, rows) + [
            slab(CB_KW), slab(CB_VW),
            pl.BlockSpec((1, rows, LANE), lambda b, g, t: (CB_GATE + g, b * nqt + t, 0)),
            pl.BlockSpec((1, WINDOW + QT, cols), lambda b, g, t: (g, 0, 0))],
        out_specs=pl.BlockSpec((rows, cols), lambda b, g, t: (b * nqt + t, g)),
        out_shape=jax.ShapeDtypeStruct((nt, NSA_HEADS * LANE), F32),
        scratch_shapes=[pltpu.VMEM((seq + WINDOW, LANE), BF16),
                        pltpu.VMEM((seq // QT + WINDOW // QT, LANE, QT), BF16)],
        compiler_params=_cparams(("arbitrary", "arbitrary", "arbitrary")),
        name="nsa_window",
    )(proj, proj, proj, proj, proj, proj, proj, wb)


def _softmax_steps(ss, vt, states):
    n = range(len(ss))
    m_new = [jnp.maximum(states[u][0], jnp.max(ss[u], axis=0, keepdims=True)) for u in n]
    alpha = [jnp.exp2(states[u][0] - m_new[u]) for u in n]
    p = [jnp.exp2(ss[u] - m_new[u]) for u in n]
    l = [alpha[u] * states[u][1] + jnp.sum(p[u], axis=0, keepdims=True) for u in n]
    pv = [_dot(vt, p[u].astype(BF16)) for u in n]
    return [(m_new[u], l[u], alpha[u] * states[u][2] + pv[u]) for u in n]


def _sel_kernel(q0, q1, q2, q3, k_ref, v_ref, sel_ref, gate_ref, nd_ref, npv_ref, oc_ref, ow_ref, o_ref,
                kb_ref, vt_ref, qa_ref, kn_ref, m_ref, l_ref, acc_ref, *, seq):
    step = pl.program_id(2)
    cols = NSA_REP * QT
    spb = KEY_BLK // SLC_BLOCK
    subs = range(SEL_SUB)

    @pl.when(step == 0)
    def _():
        blk = lax.broadcasted_iota(jnp.int32, (KEY_BLK, LANE), 0) // SLC_BLOCK
        onehot = jnp.where(blk == lax.broadcasted_iota(jnp.int32, (KEY_BLK, LANE), 1), 1.0, 0.0).astype(BF16)

        def fill(i, kmax2):
            r0 = pl.multiple_of(i * KEY_BLK, KEY_BLK)
            kblk = (k_ref[0, pl.ds(r0, KEY_BLK), :] * SCALE_LOG2).astype(BF16)
            kb_ref[pl.ds(r0, KEY_BLK), 0:LANE] = kblk
            kb_ref[pl.ds(r0, KEY_BLK), LANE:2 * LANE] = onehot
            for c in range(SEL_SUB):
                vt_ref[i, :, c * QT:(c + 1) * QT] = v_ref[0, pl.ds(r0 + c * QT, QT), :].T.astype(BF16)
            n2 = jnp.sum(jnp.square(kblk.astype(F32)), axis=1, keepdims=True)
            return jnp.maximum(kmax2, jnp.max(n2, axis=0, keepdims=True))

        kmax2 = lax.fori_loop(0, seq // KEY_BLK, fill, jnp.zeros((1, 1), F32))
        kn_ref[...] = jnp.sqrt(kmax2)
        qa_ref[...] = jnp.zeros_like(qa_ref)

    qts = [_q_transposed((q0, q1, q2, q3), u) for u in subs]
    for u in subs:
        qa_ref[u, 0:LANE, :] = qts[u]
    bounds = [jnp.sqrt(jnp.sum(jnp.square(qts[u].astype(F32)), axis=0, keepdims=True)) * kn_ref[...]
              * BOUND_SLACK + 1.0 for u in subs]

    def block(kb):
        r0 = pl.multiple_of(kb * KEY_BLK, KEY_BLK)
        return kb_ref[pl.ds(r0, KEY_BLK), :], vt_ref[kb]

    pad_rows = jnp.zeros((SEL_ROWS - spb, cols), F32)

    def scores(kb, k):
        for u in subs:
            rows = sel_ref[0, 0, pl.ds(pl.multiple_of(kb * spb, spb), spb), u * QT:(u + 1) * QT]
            rows = jnp.concatenate([rows] * NSA_REP, axis=1)
            qa_ref[u, LANE:LANE + SEL_ROWS, :] = jnp.concatenate([rows, pad_rows], axis=0).astype(BF16)
        return [_dot(k, qa_ref[u]) for u in subs]

    k, vt = block(step)
    ss = scores(step, k)
    ss = [ss[u] + nd_ref[0, u] for u in subs]
    states = _softmax_steps(ss, vt, [_init_state(cols) for _ in subs])

    prev = jnp.maximum(step - 1, 0)
    k, vt = block(prev)
    ss = scores(prev, k)
    ss[0] = ss[0] + npv_ref[0]
    exists = (step + jnp.zeros((KEY_BLK, cols), jnp.int32)) >= 1
    ss = [jnp.where(exists, ss[u], NEG) for u in subs]
    states = _softmax_steps(ss, vt, states)
    for u in subs:
        m_ref[u], l_ref[u], acc_ref[u] = states[u]

    n_old = jnp.maximum(step - 1, 0)
    m_fix = [jnp.maximum(states[u][0], bounds[u]) for u in subs]
    gap = jnp.max(jnp.concatenate([m_fix[u] - states[u][0] for u in subs], axis=1))
    fixed_ok = gap < MAX_REF_GAP

    @pl.when(fixed_ok)
    def _():
        for u in subs:
            alpha = jnp.exp2(m_ref[u] - m_fix[u])
            l_ref[u] = alpha * l_ref[u]
            acc_ref[u] = alpha * acc_ref[u]
            m_ref[u] = m_fix[u]

        def body(kb, carry):
            k, vt = block(kb)
            ss = scores(kb, k)
            ps = [jnp.exp2(ss[u] - m_ref[u]) for u in subs]
            for u in subs:
                l_ref[u] = l_ref[u] + jnp.sum(ps[u], axis=0, keepdims=True)
            pv = [_dot(vt, ps[u].astype(BF16)) for u in subs]
            for u in subs:
                acc_ref[u] = acc_ref[u] + pv[u]
            return carry

        lax.fori_loop(0, n_old, body, 0)

    @pl.when(jnp.logical_not(fixed_ok))
    def _():
        def body(kb, carry):
            k, vt = block(kb)
            st = _softmax_steps(scores(kb, k), vt, [(m_ref[u], l_ref[u], acc_ref[u]) for u in subs])
            for u in subs:
                m_ref[u], l_ref[u], acc_ref[u] = st[u]
            return carry

        lax.fori_loop(0, n_old, body, 0)

    for u in subs:
        _finish_t((m_ref[u], l_ref[u], acc_ref[u]), gate_ref, 1, o_ref, u, others=(oc_ref, ow_ref))


def _selected(proj, sel_t, nd, npv, o_c, o_w, batch, seq):
    nt = proj.shape[1]
    rows = SEL_SUB * QT
    nsteps = seq // rows
    cols = NSA_REP * QT
    slab = lambda cb0: pl.BlockSpec((1, seq, LANE), lambda b, g, t: (cb0 + g, b, 0))
    return pl.pallas_call(
        functools.partial(_sel_kernel, seq=seq),
        grid=(batch, NSA_GROUPS, nsteps),
        in_specs=_q_specs(nsteps, rows) + [
            slab(CB_KS), slab(CB_VS),
            pl.BlockSpec((1, 1, LANE, rows), lambda b, g, t: (b, g, 0, t)),
            pl.BlockSpec((1, rows, LANE), lambda b, g, t: (CB_GATE + g, b * nsteps + t, 0)),
            pl.BlockSpec((1, SEL_SUB, KEY_BLK, cols), lambda b, g, t: (g, 0, 0, 0)),
            pl.BlockSpec((1, KEY_BLK, cols), lambda b, g, t: (g, 0, 0)),
            pl.BlockSpec((rows, cols), lambda b, g, t: (b * nsteps + t, g)),
            pl.BlockSpec((rows, cols), lambda b, g, t: (b * nsteps + t, g))],
        out_specs=pl.BlockSpec((rows, cols), lambda b, g, t: (b * nsteps + t, g)),
        out_shape=jax.ShapeDtypeStruct((nt, NSA_HEADS * LANE), BF16),
        scratch_shapes=[pltpu.VMEM((seq, 2 * LANE), BF16),
                        pltpu.VMEM((seq // KEY_BLK, LANE, KEY_BLK), BF16),
                        pltpu.VMEM((SEL_SUB, 2 * LANE, cols), BF16),
                        pltpu.VMEM((1, 1), F32),
                        pltpu.VMEM((SEL_SUB, 1, cols), F32), pltpu.VMEM((SEL_SUB, 1, cols), F32),
                        pltpu.VMEM((SEL_SUB, LANE, cols), F32)],
        compiler_params=_cparams(("arbitrary", "arbitrary", "arbitrary")),
        name="nsa_selected",
    )(proj, proj, proj, proj, proj, proj, sel_t, proj, nd, npv, o_c, o_w)


def _merge_kernel(oh_ref, on_ref, mgh_ref, mgn_ref, x_ref, mod_ref,
                  wh_ref, wn_ref, wo_ref, g_ref, b_ref, wr_ref, br_ref,
                  x1_ref, h2_ref, lg_ref):
    nblk = D_MODEL // LANE
    tm = x_ref.shape[0]
    halves = [slice(s * tm // MERGE_SPLIT, (s + 1) * tm // MERGE_SPLIT) for s in range(MERGE_SPLIT)]
    a_h = [_dot(oh_ref[rs, :], wh_ref[...]) for rs in halves]
    a_n = [_dot(on_ref[rs, :], wn_ref[...]) for rs in halves]
    gh = [jnp.concatenate([mgh_ref[c, rs, :] for c in range(nblk)], axis=-1) for rs in halves]
    gn = [jnp.concatenate([mgn_ref[c, rs, :] for c in range(nblk)], axis=-1) for rs in halves]
    merged = [(jax.nn.sigmoid(gh[s]) * a_h[s] + jax.nn.sigmoid(gn[s]) * a_n[s]).astype(BF16)
              for s in range(MERGE_SPLIT)]
    y = [(1.0 + mod_ref[0, 2:3, :]) * _dot(merged[s], wo_ref[...]) for s in range(MERGE_SPLIT)]
    x1 = [_layer_norm(ALPHA * x_ref[rs, :] + y[s]) * g_ref[...] + b_ref[...] for s, rs in enumerate(halves)]
    h2 = [_layer_norm(x1[s]) * (1.0 + mod_ref[0, 4:5, :]) + mod_ref[0, 3:4, :] for s in range(MERGE_SPLIT)]
    lg = [jnp.dot(h2[s], wr_ref[...], precision=lax.Precision.HIGHEST, preferred_element_type=F32) + br_ref[...]
          for s in range(MERGE_SPLIT)]
    for s, rs in enumerate(halves):
        x1_ref[rs, :] = x1[s]
        _store_rows(h2_ref, h2[s], rs.start)
        lg_ref[rs, :] = lg[s]


def _merge(o_h, o_n, proj, x2, mod, w_h, w_n, w_o, ln_g, ln_b, w_r, b_r, seq):
    nt, d = x2.shape
    tm = min(512, seq)
    nblk = d // LANE
    row = lambda w: pl.BlockSpec((tm, w), lambda i: (i, 0))
    full = lambda a: pl.BlockSpec(a.shape, lambda i: (0,) * a.ndim)
    return pl.pallas_call(
        _merge_kernel,
        grid=(nt // tm,),
        in_specs=[row(d), row(d),
                  pl.BlockSpec((nblk, tm, LANE), lambda i: (CB_MGH // nblk, i, 0)),
                  pl.BlockSpec((nblk, tm, LANE), lambda i: (CB_MGN // nblk, i, 0)),
                  row(d),
                  pl.BlockSpec((1, 6, d), lambda i: (i * tm // seq, 0, 0)),
                  full(w_h), full(w_n), full(w_o), full(ln_g), full(ln_b), full(w_r), full(b_r)],
        out_specs=[row(d), pl.BlockSpec((tm * ROW_TILES, LANE), lambda i: (i, 0)), row(LANE)],
        out_shape=[jax.ShapeDtypeStruct((nt, d), F32),
                   jax.ShapeDtypeStruct((nt * ROW_TILES, LANE), F32),
                   jax.ShapeDtypeStruct((nt, LANE), F32)],
        compiler_params=_cparams(("arbitrary",)),
        name="merge_outproj",
    )(o_h, o_n, proj, proj, x2, mod, w_h, w_n, w_o, ln_g, ln_b, w_r, b_r)


def _route_kernel(lg_ref, rec_ref, cnt_ref, rect_ref, carry_ref, *, tm):
    @pl.when(pl.program_id(0) == 0)
    def _():
        carry_ref[...] = jnp.zeros_like(carry_ref)

    lg = lg_ref[...]
    lane = lax.broadcasted_iota(jnp.int32, (tm, LANE), 1).astype(F32)
    far = float(LANE)
    gmask = lane < N_GROUPS
    gl = jnp.where(gmask, lg, -jnp.inf)
    gmax = jnp.max(gl, axis=-1, keepdims=True)
    gsum = jnp.sum(jnp.where(gmask, jnp.exp(gl - gmax), 0.0), axis=-1, keepdims=True)
    grp_p = 1.0 / gsum
    gidx = jnp.min(jnp.where(gl == gmax, lane, far), axis=-1, keepdims=True)
    lo = N_GROUPS + EXP_PER_GROUP * gidx
    emask = (lane >= lo) & (lane < lo + EXP_PER_GROUP)
    el = jnp.where(emask, lg, -jnp.inf)
    m1 = jnp.max(el, axis=-1, keepdims=True)
    i1 = jnp.min(jnp.where(el == m1, lane, far), axis=-1, keepdims=True)
    el2 = jnp.where(lane == i1, -jnp.inf, el)
    m2 = jnp.max(el2, axis=-1, keepdims=True)
    i2 = jnp.min(jnp.where(emask & (lane != i1) & (el2 == m2), lane, far), axis=-1, keepdims=True)
    e = jnp.exp(m2 - m1)
    w0 = grp_p / (1.0 + e)
    w1 = grp_p * e / (1.0 + e)

    oh0 = lane == i1
    oh1 = lane == i2
    f0 = jnp.where(oh0, 1.0, 0.0)
    f1 = jnp.where(oh1, 1.0, 0.0)
    ri = lax.broadcasted_iota(jnp.int32, (tm, tm), 0)
    ci = lax.broadcasted_iota(jnp.int32, (tm, tm), 1)
    before = jnp.where(ci < ri, 1.0, 0.0).astype(BF16)
    cum0 = _dot(before, f0.astype(BF16))
    cum1 = _dot(before, f1.astype(BF16))
    tot0 = jnp.sum(f0, axis=0, keepdims=True)
    tot1 = jnp.sum(f1, axis=0, keepdims=True)
    carry = carry_ref[...]
    rank0 = jnp.sum(jnp.where(oh0, carry + cum0, 0.0), axis=-1, keepdims=True)
    rank1 = jnp.sum(jnp.where(oh1, carry + tot0 + cum1, 0.0), axis=-1, keepdims=True)
    carry = carry + tot0 + tot1
    carry_ref[...] = carry
    cnt_ref[...] = carry

    rec = jnp.where(lane == 0, i1 - N_GROUPS, 0.0)
    rec = jnp.where(lane == 1, i2 - N_GROUPS, rec)
    rec = jnp.where(lane == 2, w0, rec)
    rec = jnp.where(lane == 3, w1, rec)
    rec = jnp.where(lane == 4, rank0, rec)
    rec = jnp.where(lane == 5, rank1, rec)
    rec_ref[...] = rec
    rect_ref[...] = jnp.concatenate([rec[c * LANE:(c + 1) * LANE, :].T[0:8, :] for c in range(tm // LANE)], axis=1)


def _route(logits):
    nt = logits.shape[0]
    tm = min(512, nt)
    return pl.pallas_call(
        functools.partial(_route_kernel, tm=tm),
        grid=(nt // tm,),
        in_specs=[pl.BlockSpec((tm, LANE), lambda i: (i, 0))],
        out_specs=[pl.BlockSpec((tm, LANE), lambda i: (i, 0)),
                   pl.BlockSpec((1, LANE), lambda i: (0, 0)),
                   pl.BlockSpec((8, tm), lambda i: (0, i))],
        out_shape=[jax.ShapeDtypeStruct((nt, LANE), F32),
                   jax.ShapeDtypeStruct((1, LANE), F32),
                   jax.ShapeDtypeStruct((8, nt), F32)],
        scratch_shapes=[pltpu.VMEM((1, LANE), F32)],
        compiler_params=_cparams(("arbitrary",)),
        name="moe_route",
    )(logits)


def _row_copy(src, dst, sem):
    return pltpu.make_async_copy(src, dst, sem)


def _tile_of(r):
    return pl.ds(pl.multiple_of(r * ROW_TILES, ROW_TILES), ROW_TILES)


def _load_rows(ref, n, lead=()):
    return jnp.concatenate([ref[lead + (pl.ds(c, n, stride=ROW_TILES), slice(None))]
                            for c in range(ROW_TILES)], axis=1)


def _store_rows(ref, val, row0=0):
    n = val.shape[0]
    for c in range(ROW_TILES):
        ref[pl.ds(row0 * ROW_TILES + c, n, stride=ROW_TILES), :] = val[:, c * LANE:(c + 1) * LANE]


def _dispatch_kernel(dest0_ref, dest1_ref, zb_ref, h_ref, xp_ref, z_ref, sem, zsem, *, tm):
    step = pl.program_id(0)
    base = step * tm
    blk = MOE_ROWS * ROW_TILES

    @pl.when(step == 0)
    def _():
        z_ref[...] = jnp.zeros_like(z_ref)

        def zero_copy(j):
            b = jnp.maximum(zb_ref[j], 0)
            return _row_copy(z_ref, xp_ref.at[pl.ds(pl.multiple_of(b * blk, blk), blk), :], zsem)

        def start(j, carry):
            @pl.when(zb_ref[j] >= 0)
            def _():
                zero_copy(j).start()
            return carry

        def wait(j, carry):
            @pl.when(zb_ref[j] >= 0)
            def _():
                zero_copy(j).wait()
            return carry

        lax.fori_loop(0, 2 * N_EXPERTS, start, 0)
        lax.fori_loop(0, 2 * N_EXPERTS, wait, 0)

    def issue(r, carry):
        for dest_ref in (dest0_ref, dest1_ref):
            d = dest_ref[base + r]
            _row_copy(h_ref.at[_tile_of(r), :], xp_ref.at[_tile_of(d), :], sem).start()
        return carry

    lax.fori_loop(0, tm, issue, 0, unroll=ROW_DMA_UNROLL)
    for _ in range(2):
        _row_copy(h_ref, xp_ref.at[pl.ds(0, tm * ROW_TILES), :], sem).wait()


def _dispatch(dest0, dest1, zero_blocks, h2, n_blocks):
    nt = h2.shape[0] // ROW_TILES
    tm = min(256, nt)
    return pl.pallas_call(
        functools.partial(_dispatch_kernel, tm=tm),
        grid_spec=pltpu.PrefetchScalarGridSpec(
            num_scalar_prefetch=3,
            grid=(nt // tm,),
            in_specs=[pl.BlockSpec((tm * ROW_TILES, LANE), lambda i, *_: (i, 0))],
            out_specs=pl.BlockSpec(memory_space=pl.ANY),
            scratch_shapes=[pltpu.VMEM((MOE_ROWS * ROW_TILES, LANE), F32),
                            pltpu.SemaphoreType.DMA(()), pltpu.SemaphoreType.DMA(())]),
        out_shape=jax.ShapeDtypeStruct((n_blocks * MOE_ROWS * ROW_TILES, LANE), F32),
        compiler_params=_cparams(("arbitrary",)),
        name="moe_dispatch",
    )(dest0, dest1, zero_blocks, h2)


def _expert_kernel(be_ref, nu_ref, x_ref, w1_ref, w3_ref, w2_ref, y_ref, w1b_ref, w3b_ref, w2b_ref):
    i = pl.program_id(0)

    @pl.when((i < nu_ref[0]) & ((i == 0) | (be_ref[i] != be_ref[jnp.maximum(i - 1, 0)])))
    def _():
        w1b_ref[...] = w1_ref[0].astype(BF16)
        w3b_ref[...] = w3_ref[0].astype(BF16)
        w2b_ref[...] = w2_ref[0].astype(BF16)

    @pl.when(i < nu_ref[0])
    def _():
        xb = _load_rows(x_ref, MOE_ROWS).astype(BF16)
        a = _dot(xb, w1b_ref[...])
        b = _dot(xb, w3b_ref[...])
        hmid = (a * jax.nn.sigmoid(a) * b).astype(BF16)
        _store_rows(y_ref, _dot(hmid, w2b_ref[...]))

    @pl.when(i >= nu_ref[0])
    def _():
        y_ref[...] = jnp.zeros_like(y_ref)


def _experts(block_expert, n_used, x_pad, w1, w3, w2):
    d, de = w1.shape[1], w1.shape[2]
    nb = x_pad.shape[0] // (MOE_ROWS * ROW_TILES)
    return pl.pallas_call(
        _expert_kernel,
        grid_spec=pltpu.PrefetchScalarGridSpec(
            num_scalar_prefetch=2,
            grid=(nb,),
            in_specs=[pl.BlockSpec((MOE_ROWS * ROW_TILES, LANE), lambda i, be, nu: (i, 0)),
                      pl.BlockSpec((1, d, de), lambda i, be, nu: (be[i], 0, 0)),
                      pl.BlockSpec((1, d, de), lambda i, be, nu: (be[i], 0, 0)),
                      pl.BlockSpec((1, de, d), lambda i, be, nu: (be[i], 0, 0))],
            out_specs=pl.BlockSpec((MOE_ROWS * ROW_TILES, LANE), lambda i, be, nu: (i, 0)),
            scratch_shapes=[pltpu.VMEM((d, de), BF16), pltpu.VMEM((d, de), BF16), pltpu.VMEM((de, d), BF16)]),
        out_shape=jax.ShapeDtypeStruct(x_pad.shape, F32),
        compiler_params=_cparams(("arbitrary",)),
        name="moe_experts",
    )(block_expert, n_used, x_pad, w1, w3, w2)


def _combine_kernel(dest0_ref, dest1_ref, yp_ref, rec_ref, x1_ref, mod_ref, g_ref, b_ref, o_ref,
                    buf_ref, sem, *, tm):
    step = pl.program_id(0)
    slot = step % 2

    def fetch(tile, to_slot):
        def issue(r, carry):
            for k, dest_ref in enumerate((dest0_ref, dest1_ref)):
                d = dest_ref[tile * tm + r]
                _row_copy(yp_ref.at[_tile_of(d), :], buf_ref.at[to_slot, k, _tile_of(r), :],
                          sem.at[to_slot]).start()
            return carry

        lax.fori_loop(0, tm, issue, 0, unroll=ROW_DMA_UNROLL)

    @pl.when(step == 0)
    def _():
        fetch(0, 0)

    @pl.when(step + 1 < pl.num_programs(0))
    def _():
        fetch(step + 1, 1 - slot)

    for k in range(2):
        _row_copy(yp_ref.at[pl.ds(0, tm * ROW_TILES), :], buf_ref.at[slot, k], sem.at[slot]).wait()

    rec = rec_ref[...]
    y = rec[:, 2:3] * _load_rows(buf_ref, tm, (slot, 0)) + rec[:, 3:4] * _load_rows(buf_ref, tm, (slot, 1))
    y = (1.0 + mod_ref[0, 5:6, :]) * y
    o_ref[...] = _layer_norm(ALPHA * x1_ref[...] + y) * g_ref[...] + b_ref[...]


def _combine(dest0, dest1, y_pad, rec, x1, mod, ln_g, ln_b, seq):
    nt, d = x1.shape
    tm = min(256, seq)
    return pl.pallas_call(
        functools.partial(_combine_kernel, tm=tm),
        grid_spec=pltpu.PrefetchScalarGridSpec(
            num_scalar_prefetch=2,
            grid=(nt // tm,),
            in_specs=[pl.BlockSpec(memory_space=pl.ANY),
                      pl.BlockSpec((tm, LANE), lambda i, *_: (i, 0)),
                      pl.BlockSpec((tm, d), lambda i, *_: (i, 0)),
                      pl.BlockSpec((1, 6, d), lambda i, *_: (i * tm // seq, 0, 0)),
                      pl.BlockSpec((1, d), lambda i, *_: (0, 0)),
                      pl.BlockSpec((1, d), lambda i, *_: (0, 0))],
            out_specs=pl.BlockSpec((tm, d), lambda i, *_: (i, 0)),
            scratch_shapes=[pltpu.VMEM((2, 2, tm * ROW_TILES, LANE), F32), pltpu.SemaphoreType.DMA((2,))]),
        out_shape=jax.ShapeDtypeStruct((nt, d), F32),
        compiler_params=_cparams(("arbitrary",)),
        name="moe_combine",
    )(dest0, dest1, y_pad, rec, x1, mod, ln_g, ln_b)


def _rel_bucket(dist):
    n = jnp.maximum(dist, 0)
    max_exact = REL_BUCKETS // 2
    nf = jnp.maximum(n, 1).astype(F32)
    large = max_exact + (jnp.log(nf / max_exact) / math.log(REL_MAX_DIST / max_exact)
                         * (REL_BUCKETS - max_exact)).astype(jnp.int32)
    large = jnp.minimum(large, REL_BUCKETS - 1)
    return jnp.where(n < max_exact, n, large)


def _bias_tables(rel_bias, seq):
    bucket_onehot = (_rel_bucket(jnp.arange(LANE))[:, None] == jnp.arange(REL_BUCKETS)).astype(F32)
    tab_d = jnp.einsum('db,hb->hd', bucket_onehot, rel_bias,
                       precision=lax.Precision.HIGHEST)
    tok = np.arange(QT)[None, :]
    key = np.arange(LANE)[:, None]
    far = tab_d[:, LANE - 1]
    cols = NSA_REP * QT

    def transposed(dist):
        idx = jnp.asarray(np.clip(dist, 0, LANE - 1).astype(np.int32))
        onehot = (idx[..., None] == jnp.arange(LANE, dtype=jnp.int32)).astype(F32)
        t = jnp.einsum('ijd,hd->hij', onehot, tab_d, precision=lax.Precision.HIGHEST)
        t = (t - far[:, None, None]) * LOG2E
        t = t.reshape(NSA_GROUPS, NSA_REP, LANE, QT).transpose(0, 2, 1, 3)
        return t.reshape(NSA_GROUPS, LANE, cols)

    t0t = transposed(tok - key)
    t1t = transposed(tok - key + QT)

    ns = seq // CMP_STRIDE
    d_c = tok - CMP_STRIDE * key + (CMP_STRIDE * CMP_PAD - (CMP_BLOCK - 1))
    seen = np.tile(d_c >= 0, (1, NSA_REP))
    recent = jnp.where(seen[None], transposed(d_c), NEG)
    tt = jnp.concatenate([jnp.zeros((NSA_GROUPS, ns, cols), F32), recent,
                          jnp.full((NSA_GROUPS, ns, cols), NEG, F32)], axis=1)

    rho = np.arange(WINDOW + QT)[:, None]
    tok_w = np.tile(np.arange(QT), NSA_REP)[None, :]
    band = (rho > tok_w) & (rho <= tok_w + WINDOW)
    rows = jnp.concatenate([jnp.zeros((NSA_GROUPS, WINDOW - QT, NSA_REP * QT), F32), t1t, t0t], axis=1)
    wb = jnp.where(band[None], rows, NEG)

    zeros = lambda n: jnp.zeros((NSA_GROUPS, n * QT, cols), F32)
    negs = lambda n: jnp.full((NSA_GROUPS, n * QT, cols), NEG, F32)
    diag = jnp.where(np.tile(tok >= key, (1, NSA_REP))[None], t0t, NEG)
    nd = jnp.stack([jnp.concatenate(([zeros(u - 1), t1t] if u else []) + [diag, negs(SEL_SUB - 1 - u)], axis=1)
                    for u in range(SEL_SUB)], axis=1)
    npv = jnp.concatenate([zeros(SEL_SUB - 1), t1t], axis=1)
    return nd, npv, wb, tt


def _overlap_matrix(seq):
    ns = seq // CMP_STRIDE
    nslc = seq // SLC_BLOCK
    ov = np.zeros((LANE, ns), np.float32)
    cs = np.arange(ns - 1) * CMP_STRIDE
    ss = np.arange(nslc) * SLC_BLOCK
    ov[:nslc, :ns - 1] = ((cs[None, :] < ss[:, None] + SLC_BLOCK) & (cs[None, :] + CMP_BLOCK > ss[:, None]))
    return jnp.asarray(ov, BF16)


def _reorder_cols(a):
    lead = a.shape[:-1]
    gate = a[..., MAIN_COLS:MAIN_COLS + GATE_COLS]
    per = GATE_COLS // NSA_GROUPS
    gate_blocks = []
    for g in range(NSA_GROUPS):
        gate_blocks.append(gate[..., g * per:(g + 1) * per])
        gate_blocks.append(jnp.zeros(lead + (LANE - per,), a.dtype))
    pad = jnp.zeros(lead + ((CB_MGH - CB_GATE - NSA_GROUPS) * LANE,), a.dtype)
    return jnp.concatenate([a[..., :MAIN_COLS]] + gate_blocks + [pad, a[..., MAIN_COLS + GATE_COLS:]], axis=-1)


def kernel(x, c, ada_w, ada_b, w_in, b_in, hg_lb_logits, hg_norm_w, cmp_pos_k, cmp_w1_k, cmp_b1_k, cmp_w2_k, cmp_pos_v, cmp_w1_v, cmp_b1_v, cmp_w2_v, rel_bias, w_br_hg, w_br_nsa, w_out, ln1_g, ln1_b, router_grp_w, router_grp_b, router_exp_w, router_exp_b, exp_w1, exp_w3, exp_w2, ln2_g, ln2_b):
    batch, seq, d = x.shape
    nt = batch * seq
    assert d == D_MODEL and seq % 1024 == 0 and seq // SLC_BLOCK <= LANE
    l = 0
    x2 = x.reshape(nt, d)

    c_pad = jnp.zeros((8, d), F32).at[:batch].set(c)
    mod = _adaln(c_pad, ada_w[l], ada_b[l][None])[:batch].reshape(batch, 6, d)

    proj = _inproj(x2, mod, _reorder_cols(w_in[l]).astype(BF16), _reorder_cols(b_in[l])[None], seq)

    lb_all = jnp.cumsum(jax.nn.softmax(hg_lb_logits.astype(F32), axis=0), axis=0)
    o_h = _hgrn(proj, lb_all[l][None], hg_norm_w[l][None], batch, seq)

    kvc = _compress(proj, jnp.stack([cmp_pos_k[l], cmp_pos_v[l]]),
                    jnp.stack([cmp_w1_k[l], cmp_w1_v[l]]).astype(BF16),
                    jnp.stack([cmp_b1_k[l], cmp_b1_v[l]])[:, None, :],
                    jnp.stack([cmp_w2_k[l], cmp_w2_v[l]]).astype(BF16), batch, seq)

    nd, npv, wb, tt = _bias_tables(rel_bias, seq)
    o_c, sel_t = _cmp_sel(proj, kvc, tt, _overlap_matrix(seq), batch, seq)
    o_w = _window(proj, wb, batch, seq)
    o_n = _selected(proj, sel_t, nd, npv, o_c, o_w, batch, seq)

    w_r = jnp.zeros((d, LANE), F32).at[:, :N_GROUPS].set(router_grp_w[l])
    w_r = w_r.at[:, N_GROUPS:N_GROUPS + N_EXPERTS].set(router_exp_w[l])
    b_r = jnp.zeros((1, LANE), F32).at[0, :N_GROUPS].set(router_grp_b[l])
    b_r = b_r.at[0, N_GROUPS:N_GROUPS + N_EXPERTS].set(router_exp_b[l])
    x1, h2, logits = _merge(o_h, o_n, proj, x2, mod,
                            w_br_hg[l].astype(BF16), w_br_nsa[l].astype(BF16), w_out[l].astype(BF16),
                            ln1_g[l][None], ln1_b[l][None], w_r, b_r, seq)

    rec, cnt, rec_t = _route(logits)
    counts = cnt[0, N_GROUPS:N_GROUPS + N_EXPERTS].astype(jnp.int32)
    padded = (counts + MOE_ROWS - 1) // MOE_ROWS * MOE_ROWS
    pend = jnp.cumsum(padded)
    pstart = pend - padded
    n_assign = 2 * nt
    nb = n_assign // MOE_ROWS + N_EXPERTS
    slots = rec_t.astype(jnp.int32)
    expert_ids = jnp.arange(N_EXPERTS, dtype=jnp.int32)[:, None]
    slot_base = lambda e: jnp.sum(jnp.where(e[None, :] == expert_ids, pstart[:, None], 0), axis=0)
    dest0 = slot_base(slots[0]) + slots[4]
    dest1 = slot_base(slots[1]) + slots[5]
    block_start = jnp.arange(nb, dtype=jnp.int32) * MOE_ROWS
    block_expert = jnp.minimum(jnp.sum(pend[None, :] <= block_start[:, None], axis=1),
                               N_EXPERTS - 1).astype(jnp.int32)
    n_used = pend[-1] // MOE_ROWS
    spare = n_used + jnp.arange(N_EXPERTS, dtype=jnp.int32)
    zero_blocks = jnp.concatenate([jnp.where(padded > 0, pend // MOE_ROWS - 1, -1),
                                   jnp.where(spare < nb, spare, -1)]).astype(jnp.int32)

    x_pad = _dispatch(dest0, dest1, zero_blocks, h2, nb)
    y_pad = _experts(block_expert, n_used[None].astype(jnp.int32), x_pad, exp_w1[l], exp_w3[l], exp_w2[l])
    out = _combine(dest0, dest1, y_pad, rec, x1, mod, ln2_g[l][None], ln2_b[l][None], seq)
    return out.reshape(batch, seq, d)
```

```python
import functools
import math

import numpy as np
import jax
import jax.numpy as jnp
from jax import lax
from jax.experimental import pallas as pl
from jax.experimental.pallas import tpu as pltpu

F32 = jnp.float32
BF16 = jnp.bfloat16

D_MODEL = 1024
HG_HEADS = 8
HG_DK = 128
HG_DV = 128
HG_CHUNK = 32
HG_SUPER = 256
NSA_HEADS = 8
NSA_GROUPS = 2
NSA_REP = NSA_HEADS // NSA_GROUPS
NSA_DK = 128
CMP_BLOCK = 32
CMP_STRIDE = 16
SLC_BLOCK = 64
SLC_TOPK = 16
WINDOW = 512
FORCE_SCORE = 1e4
N_FORCED = 3
REL_BUCKETS = 32
REL_MAX_DIST = 128
N_GROUPS = 4
EXP_PER_GROUP = 8
N_EXPERTS = N_GROUPS * EXP_PER_GROUP
D_EXPERT = D_MODEL // 2
DEPTH = 1
ALPHA = (2 * DEPTH) ** 0.25

LANE = 128
QT = 128
NEG = -1e30
SCALE = NSA_DK ** -0.5
LOG2E = math.log2(math.e)
SCALE_LOG2 = SCALE * LOG2E
KEY_BLK = 512
BOUND_SLACK = 1.0 + 2.0 ** -10
MAX_REF_GAP = 64.0
SEL_ROWS = 16
SEL_SUB = KEY_BLK // QT
WIN_SUB = 4
CMP_SUB = 4
CMP_PAD = 120
VMEM_LIMIT = 56 * 1024 * 1024

CB_HQ, CB_HF, CB_HI, CB_HG = 0, 8, 16, 24
CB_NQ = 32
CB_KC, CB_VC, CB_KS, CB_VS, CB_KW, CB_VW = 40, 42, 44, 46, 48, 50
CB_GATE = 52
CB_MGH, CB_MGN = 56, 64
NCB = 72
MAIN_COLS = 52 * LANE
GATE_COLS = 3 * NSA_HEADS

MERGE_SPLIT = 2
ROW_TILES = D_MODEL // LANE
MOE_ROWS = 512
ROW_DMA_UNROLL = 8


def _cparams(sem):
    return pltpu.CompilerParams(dimension_semantics=sem, vmem_limit_bytes=VMEM_LIMIT)


def _dot(a, b):
    return jnp.dot(a, b, preferred_element_type=F32)


def _dot_nt(a, b):
    return lax.dot_general(a, b, (((1,), (1,)), ((), ())), preferred_element_type=F32)


def _dot_tn(a, b):
    return lax.dot_general(a, b, (((0,), (0,)), ((), ())), preferred_element_type=F32)


def _split3(x):
    hi = x.astype(BF16)
    r = x - hi.astype(F32)
    mid = r.astype(BF16)
    lo = (r - mid.astype(F32)).astype(BF16)
    return hi, mid, lo


def _dot01(m01, x):
    hi, mid, lo = _split3(x)
    return _dot(m01, hi) + _dot(m01, mid) + _dot(m01, lo)


def _dot01_r(x, m01):
    hi, mid, lo = _split3(x)
    return _dot(hi, m01) + _dot(mid, m01) + _dot(lo, m01)


def _layer_norm(x, eps=1e-5):
    mu = jnp.mean(x, axis=-1, keepdims=True)
    xc = x - mu
    var = jnp.mean(xc * xc, axis=-1, keepdims=True)
    return xc * lax.rsqrt(var + eps)


def _adaln_kernel(c_ref, w_ref, b_ref, o_ref):
    c = c_ref[...]
    ca = c * jax.nn.sigmoid(c)
    o_ref[...] = jnp.dot(ca, w_ref[...], precision=lax.Precision.HIGHEST,
                         preferred_element_type=F32) + b_ref[...]


def _adaln(c_pad, w, b):
    rows, d = c_pad.shape
    n = w.shape[1]
    return pl.pallas_call(
        _adaln_kernel,
        grid=(n // d,),
        in_specs=[pl.BlockSpec((rows, d), lambda j: (0, 0)),
                  pl.BlockSpec((d, d), lambda j: (0, j)),
                  pl.BlockSpec((1, d), lambda j: (0, j))],
        out_specs=pl.BlockSpec((rows, d), lambda j: (0, j)),
        out_shape=jax.ShapeDtypeStruct((rows, n), F32),
        compiler_params=_cparams(("arbitrary",)),
        name="adaln",
    )(c_pad, w, b)


def _inproj_kernel(x_ref, mod_ref, w_ref, b_ref, o_ref, hn_ref, *, ncb_tile):
    @pl.when(pl.program_id(1) == 0)
    def _():
        hn = _layer_norm(x_ref[...])
        sh = mod_ref[0, 0:1, :]
        sc = mod_ref[0, 1:2, :]
        hn_ref[...] = (hn * (1.0 + sc) + sh).astype(BF16)

    res = _dot(hn_ref[...], w_ref[...]) + b_ref[...]
    for c in range(ncb_tile):
        o_ref[c] = res[:, c * LANE:(c + 1) * LANE]


def _inproj(x2, mod, w, b, seq):
    nt, d = x2.shape
    tm = min(2048, seq)
    ncb_tile = 6
    tn = ncb_tile * LANE
    return pl.pallas_call(
        functools.partial(_inproj_kernel, ncb_tile=ncb_tile),
        grid=(nt // tm, NCB // ncb_tile),
        in_specs=[pl.BlockSpec((tm, d), lambda i, j: (i, 0)),
                  pl.BlockSpec((1, 6, d), lambda i, j: (i * tm // seq, 0, 0)),
                  pl.BlockSpec((d, tn), lambda i, j: (0, j)),
                  pl.BlockSpec((1, tn), lambda i, j: (0, j))],
        out_specs=pl.BlockSpec((ncb_tile, tm, LANE), lambda i, j: (j, i, 0)),
        out_shape=jax.ShapeDtypeStruct((NCB, nt, LANE), F32),
        scratch_shapes=[pltpu.VMEM((tm, d), BF16)],
        compiler_params=_cparams(("arbitrary", "arbitrary")),
        name="inproj",
    )(x2, mod, w, b)


def _hgrn_kernel(q_ref, f_ref, v_ref, g_ref, lb_ref, nw_ref, o_ref, st_ref, *, rows):
    @pl.when(pl.program_id(1) == 0)
    def _():
        st_ref[...] = jnp.zeros_like(st_ref)

    sup = HG_SUPER
    ri = lax.broadcasted_iota(jnp.int32, (sup, sup), 0)
    ci = lax.broadcasted_iota(jnp.int32, (sup, sup), 1)
    same = (ri // HG_CHUNK) == (ci // HG_CHUNK)
    cum_m = jnp.where(same & (ci <= ri), 1.0, 0.0).astype(BF16)
    rt = lax.broadcasted_iota(jnp.int32, (LANE, LANE), 0)
    ct = lax.broadcasted_iota(jnp.int32, (LANE, LANE), 1)
    tril = ((rt // HG_CHUNK) == (ct // HG_CHUNK)) & (ct <= rt)
    per = sup // HG_CHUNK
    groups = [slice(g * LANE, (g + 1) * LANE) for g in range(sup // LANE)]

    heads = range(HG_HEADS)
    hs = [slice(h * LANE, (h + 1) * LANE) for h in heads]

    def wide(ref, r0):
        return jnp.concatenate([ref[h, pl.ds(r0, sup), :] for h in heads], axis=1)

    def body(i, carry):
        r0 = pl.multiple_of(i * sup, sup)
        lb = lb_ref[...]
        f = lb + (1.0 - lb) * jax.nn.sigmoid(wide(f_ref, r0))
        lf = jnp.log(f)
        k = 1.0 - f
        hi, mid, lo = _split3(lf)
        b = _dot(cum_m, hi) + _dot(cum_m, mid) + _dot(cum_m, lo)
        chunk = [slice(c * HG_CHUNK, (c + 1) * HG_CHUNK) for c in range(per)]
        dec = [jnp.exp(b[(c + 1) * HG_CHUNK - 1:(c + 1) * HG_CHUNK, :]) for c in range(per)]
        dec_rows = jnp.concatenate([jnp.broadcast_to(dec[c], (HG_CHUNK, dec[c].shape[1])) for c in range(per)],
                                   axis=0)
        q_in = (wide(q_ref, r0) * jnp.exp(b)).astype(BF16)
        k_dec = k * jnp.exp(-b)
        k_in = k_dec.astype(BF16)
        k_end = (k_dec * dec_rows).astype(BF16)
        vb = wide(v_ref, r0).astype(BF16)

        att = [[jnp.where(tril, _dot_nt(q_in[g, hs[h]], k_in[g, hs[h]]), 0.0).astype(BF16) for g in groups]
               for h in heads]
        upd = [[_dot_tn(vb[chunk[c], hs[h]], k_end[chunk[c], hs[h]]) for c in range(per)] for h in heads]
        intra = [[_dot(att[h][n], vb[g, hs[h]]) for n, g in enumerate(groups)] for h in heads]

        st = [st_ref[h] for h in heads]
        inter = [[] for _ in heads]
        for c in range(per):
            for h in heads:
                inter[h].append(_dot_nt(q_in[chunk[c], hs[h]], st[h].astype(BF16)))
                st[h] = dec[c][:, hs[h]] * st[h] + upd[h][c]
        for h in heads:
            st_ref[h] = st[h]

        o = [jnp.concatenate(intra[h], axis=0) + jnp.concatenate(inter[h], axis=0) for h in heads]
        scale = [lax.rsqrt(jnp.mean(o[h] * o[h], axis=-1, keepdims=True) + 1e-6) for h in heads]
        on = jnp.concatenate([o[h] * scale[h] for h in heads], axis=1)
        on = on * nw_ref[...] * jax.nn.sigmoid(wide(g_ref, r0))
        o_ref[pl.ds(r0, sup), :] = on.astype(BF16)
        return carry

    lax.fori_loop(0, rows // sup, body, 0)


def _hgrn(proj, lb, nw, batch, seq):
    nt = proj.shape[1]
    tb = min(512, seq)
    nblk = seq // tb

    def slab(cb0):
        return pl.BlockSpec((HG_HEADS, tb, LANE), lambda b, t: (cb0 // HG_HEADS, b * nblk + t, 0))

    vec = pl.BlockSpec((1, HG_HEADS * LANE), lambda b, t: (0, 0))
    return pl.pallas_call(
        functools.partial(_hgrn_kernel, rows=tb),
        grid=(batch, nblk),
        in_specs=[slab(CB_HQ), slab(CB_HF), slab(CB_HI), slab(CB_HG), vec, vec],
        out_specs=pl.BlockSpec((tb, HG_HEADS * HG_DV), lambda b, t: (b * nblk + t, 0)),
        out_shape=jax.ShapeDtypeStruct((nt, HG_HEADS * HG_DV), BF16),
        scratch_shapes=[pltpu.VMEM((HG_HEADS, HG_DV, HG_DK), F32)],
        compiler_params=_cparams(("arbitrary", "arbitrary")),
        name="hgrn2",
    )(proj, proj, proj, proj, lb, nw)


def _compress_kernel(x_ref, pos_ref, w1_ref, b1_ref, w2_ref, o_ref, *, ns):
    p0 = jnp.zeros((ns, LANE), F32)
    p1 = jnp.zeros((ns, LANE), F32)
    for j in range(CMP_STRIDE):
        tok = x_ref[0, pl.ds(j, ns, stride=CMP_STRIDE), :]
        rows = slice(j * LANE, (j + 1) * LANE)
        late = slice((CMP_STRIDE + j) * LANE, (CMP_STRIDE + j + 1) * LANE)
        p0 = p0 + _dot((tok + pos_ref[0, j:j + 1, :]).astype(BF16), w1_ref[0, rows, :])
        p1 = p1 + _dot((tok + pos_ref[0, CMP_STRIDE + j:CMP_STRIDE + j + 1, :]).astype(BF16), w1_ref[0, late, :])
    h = p0 + pltpu.roll(p1, ns - 1, axis=0) + b1_ref[0]
    a = h * jax.nn.sigmoid(h)
    out = _dot(a.astype(BF16), w2_ref[0])
    row = lax.broadcasted_iota(jnp.int32, out.shape, 0)
    out = jnp.where(row < ns - 1, out, 0.0)
    o_ref[0, 0, 0] = out


def _compress(proj, pos, w1, b1, w2, batch, seq):
    ns = seq // CMP_STRIDE
    np_rows = ns
    width = CMP_STRIDE * LANE
    return pl.pallas_call(
        functools.partial(_compress_kernel, ns=ns),
        grid=(2, batch, NSA_GROUPS),
        in_specs=[pl.BlockSpec((1, seq, LANE), lambda s, b, g: (CB_KC + NSA_GROUPS * s + g, b, 0)),
                  pl.BlockSpec((1, CMP_BLOCK, LANE), lambda s, b, g: (s, 0, 0)),
                  pl.BlockSpec((1, 2 * width, LANE), lambda s, b, g: (s, 0, 0)),
                  pl.BlockSpec((1, 1, LANE), lambda s, b, g: (s, 0, 0)),
                  pl.BlockSpec((1, LANE, LANE), lambda s, b, g: (s, 0, 0))],
        out_specs=pl.BlockSpec((1, 1, 1, np_rows, LANE), lambda s, b, g: (s, b, g, 0, 0)),
        out_shape=jax.ShapeDtypeStruct((2, batch, NSA_GROUPS, np_rows, LANE), F32),
        compiler_params=_cparams(("arbitrary", "arbitrary", "arbitrary")),
        name="nsa_compress",
    )(proj, pos, w1, b1, w2)


def _cmp_sel_kernel(q0, q1, q2, q3, kc_ref, vc_ref, gate_ref, tt_ref, ov_ref, oc_ref, sel_ref,
                    kb_ref, vt_ref, *, np_rows):
    step = pl.program_id(2)
    cols = NSA_REP * QT
    ns = np_rows

    @pl.when(step == 0)
    def _():
        kb_ref[...] = (kc_ref[0, 0, 0] * SCALE_LOG2).astype(BF16)
        for c in range(np_rows // LANE):
            vt_ref[:, c * LANE:(c + 1) * LANE] = vc_ref[0, 0, 0, c * LANE:(c + 1) * LANE, :].T.astype(BF16)

    subs = range(CMP_SUB)
    tis = [step * CMP_SUB + u for u in subs]
    kb = kb_ref[...]
    qts = [_q_transposed((q0, q1, q2, q3), u) for u in subs]
    bias = [tt_ref[0, pl.ds(pl.multiple_of(ns + CMP_PAD - tis[u] * (QT // CMP_STRIDE), 8), np_rows), :]
            for u in subs]
    ss = [_dot(kb, qts[u]) + bias[u] for u in subs]
    ms = [jnp.max(ss[u], axis=0, keepdims=True) for u in subs]
    ps = [jnp.exp2(ss[u] - ms[u]) for u in subs]
    ls = [jnp.sum(ps[u], axis=0, keepdims=True) for u in subs]
    invs = [jnp.where(ms[u] > 0.5 * NEG, 1.0 / ls[u], 0.0) for u in subs]
    pn = [ps[u] * invs[u] for u in subs]
    vt = vt_ref[...]
    os_ = [_dot(vt, pn[u].astype(BF16)) for u in subs]
    for u in subs:
        rows = slice(u * QT, (u + 1) * QT)
        gt = jax.nn.sigmoid(gate_ref[0, rows, :])
        for r in range(NSA_REP):
            oc_ref[rows, r * LANE:(r + 1) * LANE] = gt[:, 3 * r:3 * r + 1] * os_[u][:, r * QT:(r + 1) * QT].T

    def group_sum(p):
        tot = p[:, 0:QT]
        for r in range(1, NSA_REP):
            tot = tot + p[:, r * QT:(r + 1) * QT]
        return tot

    imp = jnp.concatenate([_dot01(ov_ref[...], group_sum(pn[u])) for u in subs], axis=1)

    width = CMP_SUB * QT
    jj = lax.broadcasted_iota(jnp.int32, (LANE, width), 0)
    tok = step * width + lax.broadcasted_iota(jnp.int32, (LANE, width), 1)
    cur = tok // SLC_BLOCK
    forced = (jj == 0) | (jj == cur) | (jj == cur - 1)
    assert FORCE_SCORE > NSA_REP
    score = jnp.where(forced, -jnp.inf, jnp.where(jj <= cur, imp, -1.0))
    selb = jnp.where(forced, 0.0, NEG)
    jf = jj.astype(F32)
    for _ in range(SLC_TOPK - N_FORCED):
        mval = jnp.max(score, axis=0, keepdims=True)
        first = jnp.min(jnp.where(score == mval, jf, float(LANE)), axis=0, keepdims=True)
        pick = jf == first
        selb = jnp.where(pick, 0.0, selb)
        score = jnp.where(pick, -jnp.inf, score)
    sel_ref[0, 0] = selb


def _q_specs(nsteps, rows=QT):
    return [pl.BlockSpec((1, rows, LANE),
                         functools.partial(lambda b, g, t, r: (CB_NQ + NSA_REP * g + r, b * nsteps + t, 0), r=r))
            for r in range(NSA_REP)]


def _cmp_sel(proj, kvc, tt, ov_t, batch, seq):
    nt = proj.shape[1]
    rows = CMP_SUB * QT
    nsteps = seq // rows
    cols = NSA_REP * QT
    np_rows = kvc.shape[3]
    kv_spec = lambda s: pl.BlockSpec((1, 1, 1, np_rows, LANE), lambda b, g, t: (s, b, g, 0, 0))
    return pl.pallas_call(
        functools.partial(_cmp_sel_kernel, np_rows=np_rows),
        grid=(batch, NSA_GROUPS, nsteps),
        in_specs=_q_specs(nsteps, rows) + [
            kv_spec(0), kv_spec(1),
            pl.BlockSpec((1, rows, LANE), lambda b, g, t: (CB_GATE + g, b * nsteps + t, 0)),
            pl.BlockSpec((1, tt.shape[1], cols), lambda b, g, t: (g, 0, 0)),
            pl.BlockSpec((LANE, np_rows), lambda b, g, t: (0, 0))],
        out_specs=[pl.BlockSpec((rows, cols), lambda b, g, t: (b * nsteps + t, g)),
                   pl.BlockSpec((1, 1, LANE, rows), lambda b, g, t: (b, g, 0, t))],
        out_shape=[jax.ShapeDtypeStruct((nt, NSA_HEADS * LANE), F32),
                   jax.ShapeDtypeStruct((batch, NSA_GROUPS, LANE, seq), F32)],
        scratch_shapes=[pltpu.VMEM((np_rows, LANE), BF16), pltpu.VMEM((LANE, np_rows), BF16)],
        compiler_params=_cparams(("arbitrary", "arbitrary", "arbitrary")),
        name="nsa_cmp_select",
    )(proj, proj, proj, proj, kvc, kvc, proj, tt, ov_t)


def _q_transposed(q_refs, sub=0):
    return jnp.concatenate([r[0, sub * QT:(sub + 1) * QT, :].T for r in q_refs], axis=1).astype(BF16)


def _softmax_step(s, vt, state):
    m, l, acc = state
    m_new = jnp.maximum(m, jnp.max(s, axis=0, keepdims=True))
    alpha = jnp.exp2(m - m_new)
    p = jnp.exp2(s - m_new)
    l = alpha * l + jnp.sum(p, axis=0, keepdims=True)
    acc = alpha * acc + _dot(vt, p.astype(BF16))
    return m_new, l, acc


def _finish_t(state, gate_ref, branch, o_ref, sub=0, others=()):
    _, l, acc = state
    o = acc / jnp.where(l == 0.0, 1.0, l)
    rows = slice(sub * QT, (sub + 1) * QT)
    gt = jax.nn.sigmoid(gate_ref[0, rows, :])
    for r in range(NSA_REP):
        col = 3 * r + branch
        lanes = slice(r * LANE, (r + 1) * LANE)
        val = gt[:, col:col + 1] * o[:, r * QT:(r + 1) * QT].T
        for other in others:
            val = val + other[rows, lanes]
        o_ref[rows, lanes] = val.astype(o_ref.dtype)


def _init_state(cols):
    return (jnp.full((1, cols), NEG, F32), jnp.zeros((1, cols), F32), jnp.zeros((LANE, cols), F32))


def _win_kernel(q0, q1, q2, q3, k_ref, v_ref, gate_ref, wb_ref, o_ref, kb_ref, vt_ref, *, seq):
    step = pl.program_id(2)
    cols = NSA_REP * QT
    wt = WINDOW // QT

    @pl.when(step == 0)
    def _():
        kb_ref[0:WINDOW, :] = jnp.zeros((WINDOW, LANE), BF16)
        for i in range(wt):
            vt_ref[i] = jnp.zeros((LANE, QT), BF16)

        def fill(i, carry):
            r0 = pl.multiple_of(i * QT, QT)
            kb_ref[pl.ds(WINDOW + r0, QT), :] = (k_ref[0, pl.ds(r0, QT), :] * SCALE_LOG2).astype(BF16)
            vt_ref[wt + i] = v_ref[0, pl.ds(r0, QT), :].T.astype(BF16)
            return carry

        lax.fori_loop(0, seq // QT, fill, 0)

    rho = lax.broadcasted_iota(jnp.int32, (WINDOW + QT, cols), 0)
    subs = range(WIN_SUB)
    tis = [step * WIN_SUB + sub for sub in subs]
    qts = [_q_transposed((q0, q1, q2, q3), sub) for sub in subs]
    ss = [_dot(kb_ref[pl.ds(pl.multiple_of(tis[u] * QT, QT), WINDOW + QT), :], qts[u]) + wb_ref[0]
          for u in subs]
    ss = [jnp.where(rho >= (wt - tis[u]) * QT, ss[u], NEG) for u in subs]
    ms = [jnp.max(ss[u], axis=0, keepdims=True) for u in subs]
    ps = [jnp.exp2(ss[u] - ms[u]) for u in subs]
    ls = [jnp.sum(ps[u], axis=0, keepdims=True) for u in subs]
    pbs = [ps[u].astype(BF16) for u in subs]
    accs = [_dot(vt_ref[tis[u]], pbs[u][0:QT]) for u in subs]
    for d in range(1, wt + 1):
        accs = [accs[u] + _dot(vt_ref[tis[u] + d], pbs[u][d * QT:(d + 1) * QT]) for u in subs]
    for u in subs:
        _finish_t((ms[u], ls[u], accs[u]), gate_ref, 2, o_ref, u)


def _window(proj, wb, batch, seq):
    nt = proj.shape[1]
    rows = WIN_SUB * QT
    nqt = seq // rows
    cols = NSA_REP * QT
    slab = lambda cb0: pl.BlockSpec((1, seq, LANE), lambda b, g, t: (cb0 + g, b, 0))
    return pl.pallas_call(
        functools.partial(_win_kernel, seq=seq),
        grid=(batch, NSA_GROUPS, nqt),
        in_specs=_q_specs(nqt, rows) + [
            slab(CB_KW), slab(CB_VW),
            pl.BlockSpec((1, rows, LANE), lambda b, g, t: (CB_GATE + g, b * nqt + t, 0)),
            pl.BlockSpec((1, WINDOW + QT, cols), lambda b, g, t: (g, 0, 0))],
        out_specs=pl.BlockSpec((rows, cols), lambda b, g, t: (b * nqt + t, g)),
        out_shape=jax.ShapeDtypeStruct((nt, NSA_HEADS * LANE), F32),
        scratch_shapes=[pltpu.VMEM((seq + WINDOW, LANE), BF16),
                        pltpu.VMEM((seq // QT + WINDOW // QT, LANE, QT), BF16)],
        compiler_params=_cparams(("arbitrary", "arbitrary", "arbitrary")),
        name="nsa_window",
    )(proj, proj, proj, proj, proj, proj, proj, wb)


def _softmax_steps(ss, vt, states):
    n = range(len(ss))
    m_new = [jnp.maximum(states[u][0], jnp.max(ss[u], axis=0, keepdims=True)) for u in n]
    alpha = [jnp.exp2(states[u][0] - m_new[u]) for u in n]
    p = [jnp.exp2(ss[u] - m_new[u]) for u in n]
    l = [alpha[u] * states[u][1] + jnp.sum(p[u], axis=0, keepdims=True) for u in n]
    pv = [_dot(vt, p[u].astype(BF16)) for u in n]
    return [(m_new[u], l[u], alpha[u] * states[u][2] + pv[u]) for u in n]


def _sel_kernel(q0, q1, q2, q3, k_ref, v_ref, sel_ref, gate_ref, nd_ref, npv_ref, oc_ref, ow_ref, o_ref,
                kb_ref, vt_ref, qa_ref, kn_ref, m_ref, l_ref, acc_ref, *, seq):
    step = pl.program_id(2)
    cols = NSA_REP * QT
    spb = KEY_BLK // SLC_BLOCK
    subs = range(SEL_SUB)

    @pl.when(step == 0)
    def _():
        blk = lax.broadcasted_iota(jnp.int32, (KEY_BLK, LANE), 0) // SLC_BLOCK
        onehot = jnp.where(blk == lax.broadcasted_iota(jnp.int32, (KEY_BLK, LANE), 1), 1.0, 0.0).astype(BF16)

        def fill(i, kmax2):
            r0 = pl.multiple_of(i * KEY_BLK, KEY_BLK)
            kblk = (k_ref[0, pl.ds(r0, KEY_BLK), :] * SCALE_LOG2).astype(BF16)
            kb_ref[pl.ds(r0, KEY_BLK), 0:LANE] = kblk
            kb_ref[pl.ds(r0, KEY_BLK), LANE:2 * LANE] = onehot
            for c in range(SEL_SUB):
                vt_ref[i, :, c * QT:(c + 1) * QT] = v_ref[0, pl.ds(r0 + c * QT, QT), :].T.astype(BF16)
            n2 = jnp.sum(jnp.square(kblk.astype(F32)), axis=1, keepdims=True)
            return jnp.maximum(kmax2, jnp.max(n2, axis=0, keepdims=True))

        kmax2 = lax.fori_loop(0, seq // KEY_BLK, fill, jnp.zeros((1, 1), F32))
        kn_ref[...] = jnp.sqrt(kmax2)
        qa_ref[...] = jnp.zeros_like(qa_ref)

    qts = [_q_transposed((q0, q1, q2, q3), u) for u in subs]
    for u in subs:
        qa_ref[u, 0:LANE, :] = qts[u]
    bounds = [jnp.sqrt(jnp.sum(jnp.square(qts[u].astype(F32)), axis=0, keepdims=True)) * kn_ref[...]
              * BOUND_SLACK + 1.0 for u in subs]

    def block(kb):
        r0 = pl.multiple_of(kb * KEY_BLK, KEY_BLK)
        return kb_ref[pl.ds(r0, KEY_BLK), :], vt_ref[kb]

    pad_rows = jnp.zeros((SEL_ROWS - spb, cols), F32)

    def scores(kb, k):
        for u in subs:
            rows = sel_ref[0, 0, pl.ds(pl.multiple_of(kb * spb, spb), spb), u * QT:(u + 1) * QT]
            rows = jnp.concatenate([rows] * NSA_REP, axis=1)
            qa_ref[u, LANE:LANE + SEL_ROWS, :] = jnp.concatenate([rows, pad_rows], axis=0).astype(BF16)
        return [_dot(k, qa_ref[u]) for u in subs]

    k, vt = block(step)
    ss = scores(step, k)
    ss = [ss[u] + nd_ref[0, u] for u in subs]
    states = _softmax_steps(ss, vt, [_init_state(cols) for _ in subs])

    prev = jnp.maximum(step - 1, 0)
    k, vt = block(prev)
    ss = scores(prev, k)
    ss[0] = ss[0] + npv_ref[0]
    exists = (step + jnp.zeros((KEY_BLK, cols), jnp.int32)) >= 1
    ss = [jnp.where(exists, ss[u], NEG) for u in subs]
    states = _softmax_steps(ss, vt, states)
    for u in subs:
        m_ref[u], l_ref[u], acc_ref[u] = states[u]

    n_old = jnp.maximum(step - 1, 0)
    m_fix = [jnp.maximum(states[u][0], bounds[u]) for u in subs]
    gap = jnp.max(jnp.concatenate([m_fix[u] - states[u][0] for u in subs], axis=1))
    fixed_ok = gap < MAX_REF_GAP

    @pl.when(fixed_ok)
    def _():
        for u in subs:
            alpha = jnp.exp2(m_ref[u] - m_fix[u])
            l_ref[u] = alpha * l_ref[u]
            acc_ref[u] = alpha * acc_ref[u]
            m_ref[u] = m_fix[u]

        def body(kb, carry):
            k, vt = block(kb)
            ss = scores(kb, k)
            ps = [jnp.exp2(ss[u] - m_ref[u]) for u in subs]
            for u in subs:
                l_ref[u] = l_ref[u] + jnp.sum(ps[u], axis=0, keepdims=True)
            pv = [_dot(vt, ps[u].astype(BF16)) for u in subs]
            for u in subs:
                acc_ref[u] = acc_ref[u] + pv[u]
            return carry

        lax.fori_loop(0, n_old, body, 0)

    @pl.when(jnp.logical_not(fixed_ok))
    def _():
        def body(kb, carry):
            k, vt = block(kb)
            st = _softmax_steps(scores(kb, k), vt, [(m_ref[u], l_ref[u], acc_ref[u]) for u in subs])
            for u in subs:
                m_ref[u], l_ref[u], acc_ref[u] = st[u]
            return carry

        lax.fori_loop(0, n_old, body, 0)

    for u in subs:
        _finish_t((m_ref[u], l_ref[u], acc_ref[u]), gate_ref, 1, o_ref, u, others=(oc_ref, ow_ref))


def _selected(proj, sel_t, nd, npv, o_c, o_w, batch, seq):
    nt = proj.shape[1]
    rows = SEL_SUB * QT
    nsteps = seq // rows
    cols = NSA_REP * QT
    slab = lambda cb0: pl.BlockSpec((1, seq, LANE), lambda b, g, t: (cb0 + g, b, 0))
    return pl.pallas_call(
        functools.partial(_sel_kernel, seq=seq),
        grid=(batch, NSA_GROUPS, nsteps),
        in_specs=_q_specs(nsteps, rows) + [
            slab(CB_KS), slab(CB_VS),
            pl.BlockSpec((1, 1, LANE, rows), lambda b, g, t: (b, g, 0, t)),
            pl.BlockSpec((1, rows, LANE), lambda b, g, t: (CB_GATE + g, b * nsteps + t, 0)),
            pl.BlockSpec((1, SEL_SUB, KEY_BLK, cols), lambda b, g, t: (g, 0, 0, 0)),
            pl.BlockSpec((1, KEY_BLK, cols), lambda b, g, t: (g, 0, 0)),
            pl.BlockSpec((rows, cols), lambda b, g, t: (b * nsteps + t, g)),
            pl.BlockSpec((rows, cols), lambda b, g, t: (b * nsteps + t, g))],
        out_specs=pl.BlockSpec((rows, cols), lambda b, g, t: (b * nsteps + t, g)),
        out_shape=jax.ShapeDtypeStruct((nt, NSA_HEADS * LANE), BF16),
        scratch_shapes=[pltpu.VMEM((seq, 2 * LANE), BF16),
                        pltpu.VMEM((seq // KEY_BLK, LANE, KEY_BLK), BF16),
                        pltpu.VMEM((SEL_SUB, 2 * LANE, cols), BF16),
                        pltpu.VMEM((1, 1), F32),
                        pltpu.VMEM((SEL_SUB, 1, cols), F32), pltpu.VMEM((SEL_SUB, 1, cols), F32),
                        pltpu.VMEM((SEL_SUB, LANE, cols), F32)],
        compiler_params=_cparams(("arbitrary", "arbitrary", "arbitrary")),
        name="nsa_selected",
    )(proj, proj, proj, proj, proj, proj, sel_t, proj, nd, npv, o_c, o_w)


def _merge_kernel(oh_ref, on_ref, mgh_ref, mgn_ref, x_ref, mod_ref,
                  wh_ref, wn_ref, wo_ref, g_ref, b_ref, wr_ref, br_ref,
                  x1_ref, h2_ref, lg_ref):
    nblk = D_MODEL // LANE
    tm = x_ref.shape[0]
    halves = [slice(s * tm // MERGE_SPLIT, (s + 1) * tm // MERGE_SPLIT) for s in range(MERGE_SPLIT)]
    a_h = [_dot(oh_ref[rs, :], wh_ref[...]) for rs in halves]
    a_n = [_dot(on_ref[rs, :], wn_ref[...]) for rs in halves]
    gh = [jnp.concatenate([mgh_ref[c, rs, :] for c in range(nblk)], axis=-1) for rs in halves]
    gn = [jnp.concatenate([mgn_ref[c, rs, :] for c in range(nblk)], axis=-1) for rs in halves]
    merged = [(jax.nn.sigmoid(gh[s]) * a_h[s] + jax.nn.sigmoid(gn[s]) * a_n[s]).astype(BF16)
              for s in range(MERGE_SPLIT)]
    y = [(1.0 + mod_ref[0, 2:3, :]) * _dot(merged[s], wo_ref[...]) for s in range(MERGE_SPLIT)]
    x1 = [_layer_norm(ALPHA * x_ref[rs, :] + y[s]) * g_ref[...] + b_ref[...] for s, rs in enumerate(halves)]
    h2 = [_layer_norm(x1[s]) * (1.0 + mod_ref[0, 4:5, :]) + mod_ref[0, 3:4, :] for s in range(MERGE_SPLIT)]
    lg = [jnp.dot(h2[s], wr_ref[...], precision=lax.Precision.HIGHEST, preferred_element_type=F32) + br_ref[...]
          for s in range(MERGE_SPLIT)]
    for s, rs in enumerate(halves):
        x1_ref[rs, :] = x1[s]
        _store_rows(h2_ref, h2[s], rs.start)
        lg_ref[rs, :] = lg[s]


def _merge(o_h, o_n, proj, x2, mod, w_h, w_n, w_o, ln_g, ln_b, w_r, b_r, seq):
    nt, d = x2.shape
    tm = min(512, seq)
    nblk = d // LANE
    row = lambda w: pl.BlockSpec((tm, w), lambda i: (i, 0))
    full = lambda a: pl.BlockSpec(a.shape, lambda i: (0,) * a.ndim)
    return pl.pallas_call(
        _merge_kernel,
        grid=(nt // tm,),
        in_specs=[row(d), row(d),
                  pl.BlockSpec((nblk, tm, LANE), lambda i: (CB_MGH // nblk, i, 0)),
                  pl.BlockSpec((nblk, tm, LANE), lambda i: (CB_MGN // nblk, i, 0)),
                  row(d),
                  pl.BlockSpec((1, 6, d), lambda i: (i * tm // seq, 0, 0)),
                  full(w_h), full(w_n), full(w_o), full(ln_g), full(ln_b), full(w_r), full(b_r)],
        out_specs=[row(d), pl.BlockSpec((tm * ROW_TILES, LANE), lambda i: (i, 0)), row(LANE)],
        out_shape=[jax.ShapeDtypeStruct((nt, d), F32),
                   jax.ShapeDtypeStruct((nt * ROW_TILES, LANE), F32),
                   jax.ShapeDtypeStruct((nt, LANE), F32)],
        compiler_params=_cparams(("arbitrary",)),
        name="merge_outproj",
    )(o_h, o_n, proj, proj, x2, mod, w_h, w_n, w_o, ln_g, ln_b, w_r, b_r)


def _route_kernel(lg_ref, rec_ref, cnt_ref, rect_ref, carry_ref, *, tm):
    @pl.when(pl.program_id(0) == 0)
    def _():
        carry_ref[...] = jnp.zeros_like(carry_ref)

    lg = lg_ref[...]
    lane = lax.broadcasted_iota(jnp.int32, (tm, LANE), 1).astype(F32)
    far = float(LANE)
    gmask = lane < N_GROUPS
    gl = jnp.where(gmask, lg, -jnp.inf)
    gmax = jnp.max(gl, axis=-1, keepdims=True)
    gsum = jnp.sum(jnp.where(gmask, jnp.exp(gl - gmax), 0.0), axis=-1, keepdims=True)
    grp_p = 1.0 / gsum
    gidx = jnp.min(jnp.where(gl == gmax, lane, far), axis=-1, keepdims=True)
    lo = N_GROUPS + EXP_PER_GROUP * gidx
    emask = (lane >= lo) & (lane < lo + EXP_PER_GROUP)
    el = jnp.where(emask, lg, -jnp.inf)
    m1 = jnp.max(el, axis=-1, keepdims=True)
    i1 = jnp.min(jnp.where(el == m1, lane, far), axis=-1, keepdims=True)
    el2 = jnp.where(lane == i1, -jnp.inf, el)
    m2 = jnp.max(el2, axis=-1, keepdims=True)
    i2 = jnp.min(jnp.where(emask & (lane != i1) & (el2 == m2), lane, far), axis=-1, keepdims=True)
    e = jnp.exp(m2 - m1)
    w0 = grp_p / (1.0 + e)
    w1 = grp_p * e / (1.0 + e)

    oh0 = lane == i1
    oh1 = lane == i2
    f0 = jnp.where(oh0, 1.0, 0.0)
    f1 = jnp.where(oh1, 1.0, 0.0)
    ri = lax.broadcasted_iota(jnp.int32, (tm, tm), 0)
    ci = lax.broadcasted_iota(jnp.int32, (tm, tm), 1)
    before = jnp.where(ci < ri, 1.0, 0.0).astype(BF16)
    cum0 = _dot(before, f0.astype(BF16))
    cum1 = _dot(before, f1.astype(BF16))
    tot0 = jnp.sum(f0, axis=0, keepdims=True)
    tot1 = jnp.sum(f1, axis=0, keepdims=True)
    carry = carry_ref[...]
    rank0 = jnp.sum(jnp.where(oh0, carry + cum0, 0.0), axis=-1, keepdims=True)
    rank1 = jnp.sum(jnp.where(oh1, carry + tot0 + cum1, 0.0), axis=-1, keepdims=True)
    carry = carry + tot0 + tot1
    carry_ref[...] = carry
    cnt_ref[...] = carry

    rec = jnp.where(lane == 0, i1 - N_GROUPS, 0.0)
    rec = jnp.where(lane == 1, i2 - N_GROUPS, rec)
    rec = jnp.where(lane == 2, w0, rec)
    rec = jnp.where(lane == 3, w1, rec)
    rec = jnp.where(lane == 4, rank0, rec)
    rec = jnp.where(lane == 5, rank1, rec)
    rec_ref[...] = rec
    rect_ref[...] = jnp.concatenate([rec[c * LANE:(c + 1) * LANE, :].T[0:8, :] for c in range(tm // LANE)], axis=1)


def _route(logits):
    nt = logits.shape[0]
    tm = min(512, nt)
    return pl.pallas_call(
        functools.partial(_route_kernel, tm=tm),
        grid=(nt // tm,),
        in_specs=[pl.BlockSpec((tm, LANE), lambda i: (i, 0))],
        out_specs=[pl.BlockSpec((tm, LANE), lambda i: (i, 0)),
                   pl.BlockSpec((1, LANE), lambda i: (0, 0)),
                   pl.BlockSpec((8, tm), lambda i: (0, i))],
        out_shape=[jax.ShapeDtypeStruct((nt, LANE), F32),
                   jax.ShapeDtypeStruct((1, LANE), F32),
                   jax.ShapeDtypeStruct((8, nt), F32)],
        scratch_shapes=[pltpu.VMEM((1, LANE), F32)],
        compiler_params=_cparams(("arbitrary",)),
        name="moe_route",
    )(logits)


def _row_copy(src, dst, sem):
    return pltpu.make_async_copy(src, dst, sem)


def _tile_of(r):
    return pl.ds(pl.multiple_of(r * ROW_TILES, ROW_TILES), ROW_TILES)


def _load_rows(ref, n, lead=()):
    return jnp.concatenate([ref[lead + (pl.ds(c, n, stride=ROW_TILES), slice(None))]
                            for c in range(ROW_TILES)], axis=1)


def _store_rows(ref, val, row0=0):
    n = val.shape[0]
    for c in range(ROW_TILES):
        ref[pl.ds(row0 * ROW_TILES + c, n, stride=ROW_TILES), :] = val[:, c * LANE:(c + 1) * LANE]


def _dispatch_kernel(dest0_ref, dest1_ref, zb_ref, h_ref, xp_ref, z_ref, sem, zsem, *, tm):
    step = pl.program_id(0)
    base = step * tm
    blk = MOE_ROWS * ROW_TILES

    @pl.when(step == 0)
    def _():
        z_ref[...] = jnp.zeros_like(z_ref)

        def zero_copy(j):
            b = jnp.maximum(zb_ref[j], 0)
            return _row_copy(z_ref, xp_ref.at[pl.ds(pl.multiple_of(b * blk, blk), blk), :], zsem)

        def start(j, carry):
            @pl.when(zb_ref[j] >= 0)
            def _():
                zero_copy(j).start()
            return carry

        def wait(j, carry):
            @pl.when(zb_ref[j] >= 0)
            def _():
                zero_copy(j).wait()
            return carry

        lax.fori_loop(0, 2 * N_EXPERTS, start, 0)
        lax.fori_loop(0, 2 * N_EXPERTS, wait, 0)

    def issue(r, carry):
        for dest_ref in (dest0_ref, dest1_ref):
            d = dest_ref[base + r]
            _row_copy(h_ref.at[_tile_of(r), :], xp_ref.at[_tile_of(d), :], sem).start()
        return carry

    lax.fori_loop(0, tm, issue, 0, unroll=ROW_DMA_UNROLL)
    for _ in range(2):
        _row_copy(h_ref, xp_ref.at[pl.ds(0, tm * ROW_TILES), :], sem).wait()


def _dispatch(dest0, dest1, zero_blocks, h2, n_blocks):
    nt = h2.shape[0] // ROW_TILES
    tm = min(256, nt)
    return pl.pallas_call(
        functools.partial(_dispatch_kernel, tm=tm),
        grid_spec=pltpu.PrefetchScalarGridSpec(
            num_scalar_prefetch=3,
            grid=(nt // tm,),
            in_specs=[pl.BlockSpec((tm * ROW_TILES, LANE), lambda i, *_: (i, 0))],
            out_specs=pl.BlockSpec(memory_space=pl.ANY),
            scratch_shapes=[pltpu.VMEM((MOE_ROWS * ROW_TILES, LANE), F32),
                            pltpu.SemaphoreType.DMA(()), pltpu.SemaphoreType.DMA(())]),
        out_shape=jax.ShapeDtypeStruct((n_blocks * MOE_ROWS * ROW_TILES, LANE), F32),
        compiler_params=_cparams(("arbitrary",)),
        name="moe_dispatch",
    )(dest0, dest1, zero_blocks, h2)


def _expert_kernel(be_ref, nu_ref, x_ref, w1_ref, w3_ref, w2_ref, y_ref, w1b_ref, w3b_ref, w2b_ref):
    i = pl.program_id(0)

    @pl.when((i < nu_ref[0]) & ((i == 0) | (be_ref[i] != be_ref[jnp.maximum(i - 1, 0)])))
    def _():
        w1b_ref[...] = w1_ref[0].astype(BF16)
        w3b_ref[...] = w3_ref[0].astype(BF16)
        w2b_ref[...] = w2_ref[0].astype(BF16)

    @pl.when(i < nu_ref[0])
    def _():
        xb = _load_rows(x_ref, MOE_ROWS).astype(BF16)
        a = _dot(xb, w1b_ref[...])
        b = _dot(xb, w3b_ref[...])
        hmid = (a * jax.nn.sigmoid(a) * b).astype(BF16)
        _store_rows(y_ref, _dot(hmid, w2b_ref[...]))

    @pl.when(i >= nu_ref[0])
    def _():
        y_ref[...] = jnp.zeros_like(y_ref)


def _experts(block_expert, n_used, x_pad, w1, w3, w2):
    d, de = w1.shape[1], w1.shape[2]
    nb = x_pad.shape[0] // (MOE_ROWS * ROW_TILES)
    return pl.pallas_call(
        _expert_kernel,
        grid_spec=pltpu.PrefetchScalarGridSpec(
            num_scalar_prefetch=2,
            grid=(nb,),
            in_specs=[pl.BlockSpec((MOE_ROWS * ROW_TILES, LANE), lambda i, be, nu: (i, 0)),
                      pl.BlockSpec((1, d, de), lambda i, be, nu: (be[i], 0, 0)),
                      pl.BlockSpec((1, d, de), lambda i, be, nu: (be[i], 0, 0)),
                      pl.BlockSpec((1, de, d), lambda i, be, nu: (be[i], 0, 0))],
            out_specs=pl.BlockSpec((MOE_ROWS * ROW_TILES, LANE), lambda i, be, nu: (i, 0)),
            scratch_shapes=[pltpu.VMEM((d, de), BF16), pltpu.VMEM((d, de), BF16), pltpu.VMEM((de, d), BF16)]),
        out_shape=jax.ShapeDtypeStruct(x_pad.shape, F32),
        compiler_params=_cparams(("arbitrary",)),
        name="moe_experts",
    )(block_expert, n_used, x_pad, w1, w3, w2)


def _combine_kernel(dest0_ref, dest1_ref, yp_ref, rec_ref, x1_ref, mod_ref, g_ref, b_ref, o_ref,
                    buf_ref, sem, *, tm):
    step = pl.program_id(0)
    slot = step % 2

    def fetch(tile, to_slot):
        def issue(r, carry):
            for k, dest_ref in enumerate((dest0_ref, dest1_ref)):
                d = dest_ref[tile * tm + r]
                _row_copy(yp_ref.at[_tile_of(d), :], buf_ref.at[to_slot, k, _tile_of(r), :],
                          sem.at[to_slot]).start()
            return carry

        lax.fori_loop(0, tm, issue, 0, unroll=ROW_DMA_UNROLL)

    @pl.when(step == 0)
    def _():
        fetch(0, 0)

    @pl.when(step + 1 < pl.num_programs(0))
    def _():
        fetch(step + 1, 1 - slot)

    for k in range(2):
        _row_copy(yp_ref.at[pl.ds(0, tm * ROW_TILES), :], buf_ref.at[slot, k], sem.at[slot]).wait()

    rec = rec_ref[...]
    y = rec[:, 2:3] * _load_rows(buf_ref, tm, (slot, 0)) + rec[:, 3:4] * _load_rows(buf_ref, tm, (slot, 1))
    y = (1.0 + mod_ref[0, 5:6, :]) * y
    o_ref[...] = _layer_norm(ALPHA * x1_ref[...] + y) * g_ref[...] + b_ref[...]


def _combine(dest0, dest1, y_pad, rec, x1, mod, ln_g, ln_b, seq):
    nt, d = x1.shape
    tm = min(256, seq)
    return pl.pallas_call(
        functools.partial(_combine_kernel, tm=tm),
        grid_spec=pltpu.PrefetchScalarGridSpec(
            num_scalar_prefetch=2,
            grid=(nt // tm,),
            in_specs=[pl.BlockSpec(memory_space=pl.ANY),
                      pl.BlockSpec((tm, LANE), lambda i, *_: (i, 0)),
                      pl.BlockSpec((tm, d), lambda i, *_: (i, 0)),
                      pl.BlockSpec((1, 6, d), lambda i, *_: (i * tm // seq, 0, 0)),
                      pl.BlockSpec((1, d), lambda i, *_: (0, 0)),
                      pl.BlockSpec((1, d), lambda i, *_: (0, 0))],
            out_specs=pl.BlockSpec((tm, d), lambda i, *_: (i, 0)),
            scratch_shapes=[pltpu.VMEM((2, 2, tm * ROW_TILES, LANE), F32), pltpu.SemaphoreType.DMA((2,))]),
        out_shape=jax.ShapeDtypeStruct((nt, d), F32),
        compiler_params=_cparams(("arbitrary",)),
        name="moe_combine",
    )(dest0, dest1, y_pad, rec, x1, mod, ln_g, ln_b)


def _rel_bucket(dist):
    n = jnp.maximum(dist, 0)
    max_exact = REL_BUCKETS // 2
    nf = jnp.maximum(n, 1).astype(F32)
    large = max_exact + (jnp.log(nf / max_exact) / math.log(REL_MAX_DIST / max_exact)
                         * (REL_BUCKETS - max_exact)).astype(jnp.int32)
    large = jnp.minimum(large, REL_BUCKETS - 1)
    return jnp.where(n < max_exact, n, large)


def _bias_tables(rel_bias, seq):
    bucket_onehot = (_rel_bucket(jnp.arange(LANE))[:, None] == jnp.arange(REL_BUCKETS)).astype(F32)
    tab_d = jnp.einsum('db,hb->hd', bucket_onehot, rel_bias,
                       precision=lax.Precision.HIGHEST)
    tok = np.arange(QT)[None, :]
    key = np.arange(LANE)[:, None]
    far = tab_d[:, LANE - 1]
    cols = NSA_REP * QT

    def transposed(dist):
        idx = jnp.asarray(np.clip(dist, 0, LANE - 1).astype(np.int32))
        onehot = (idx[..., None] == jnp.arange(LANE, dtype=jnp.int32)).astype(F32)
        t = jnp.einsum('ijd,hd->hij', onehot, tab_d, precision=lax.Precision.HIGHEST)
        t = (t - far[:, None, None]) * LOG2E
        t = t.reshape(NSA_GROUPS, NSA_REP, LANE, QT).transpose(0, 2, 1, 3)
        return t.reshape(NSA_GROUPS, LANE, cols)

    t0t = transposed(tok - key)
    t1t = transposed(tok - key + QT)

    ns = seq // CMP_STRIDE
    d_c = tok - CMP_STRIDE * key + (CMP_STRIDE * CMP_PAD - (CMP_BLOCK - 1))
    seen = np.tile(d_c >= 0, (1, NSA_REP))
    recent = jnp.where(seen[None], transposed(d_c), NEG)
    tt = jnp.concatenate([jnp.zeros((NSA_GROUPS, ns, cols), F32), recent,
                          jnp.full((NSA_GROUPS, ns, cols), NEG, F32)], axis=1)

    rho = np.arange(WINDOW + QT)[:, None]
    tok_w = np.tile(np.arange(QT), NSA_REP)[None, :]
    band = (rho > tok_w) & (rho <= tok_w + WINDOW)
    rows = jnp.concatenate([jnp.zeros((NSA_GROUPS, WINDOW - QT, NSA_REP * QT), F32), t1t, t0t], axis=1)
    wb = jnp.where(band[None], rows, NEG)

    zeros = lambda n: jnp.zeros((NSA_GROUPS, n * QT, cols), F32)
    negs = lambda n: jnp.full((NSA_GROUPS, n * QT, cols), NEG, F32)
    diag = jnp.where(np.tile(tok >= key, (1, NSA_REP))[None], t0t, NEG)
    nd = jnp.stack([jnp.concatenate(([zeros(u - 1), t1t] if u else []) + [diag, negs(SEL_SUB - 1 - u)], axis=1)
                    for u in range(SEL_SUB)], axis=1)
    npv = jnp.concatenate([zeros(SEL_SUB - 1), t1t], axis=1)
    return nd, npv, wb, tt


def _overlap_matrix(seq):
    ns = seq // CMP_STRIDE
    nslc = seq // SLC_BLOCK
    ov = np.zeros((LANE, ns), np.float32)
    cs = np.arange(ns - 1) * CMP_STRIDE
    ss = np.arange(nslc) * SLC_BLOCK
    ov[:nslc, :ns - 1] = ((cs[None, :] < ss[:, None] + SLC_BLOCK) & (cs[None, :] + CMP_BLOCK > ss[:, None]))
    return jnp.asarray(ov, BF16)


def _reorder_cols(a):
    lead = a.shape[:-1]
    gate = a[..., MAIN_COLS:MAIN_COLS + GATE_COLS]
    per = GATE_COLS // NSA_GROUPS
    gate_blocks = []
    for g in range(NSA_GROUPS):
        gate_blocks.append(gate[..., g * per:(g + 1) * per])
        gate_blocks.append(jnp.zeros(lead + (LANE - per,), a.dtype))
    pad = jnp.zeros(lead + ((CB_MGH - CB_GATE - NSA_GROUPS) * LANE,), a.dtype)
    return jnp.concatenate([a[..., :MAIN_COLS]] + gate_blocks + [pad, a[..., MAIN_COLS + GATE_COLS:]], axis=-1)


def kernel(x, c, ada_w, ada_b, w_in, b_in, hg_lb_logits, hg_norm_w, cmp_pos_k, cmp_w1_k, cmp_b1_k, cmp_w2_k, cmp_pos_v, cmp_w1_v, cmp_b1_v, cmp_w2_v, rel_bias, w_br_hg, w_br_nsa, w_out, ln1_g, ln1_b, router_grp_w, router_grp_b, router_exp_w, router_exp_b, exp_w1, exp_w3, exp_w2, ln2_g, ln2_b):
    batch, seq, d = x.shape
    nt = batch * seq
    assert d == D_MODEL and seq % 1024 == 0 and seq // SLC_BLOCK <= LANE
    l = 0
    x2 = x.reshape(nt, d)

    c_pad = jnp.zeros((8, d), F32).at[:batch].set(c)
    mod = _adaln(c_pad, ada_w[l], ada_b[l][None])[:batch].reshape(batch, 6, d)

    proj = _inproj(x2, mod, _reorder_cols(w_in[l]).astype(BF16), _reorder_cols(b_in[l])[None], seq)

    lb_all = jnp.cumsum(jax.nn.softmax(hg_lb_logits.astype(F32), axis=0), axis=0)
    o_h = _hgrn(proj, lb_all[l][None], hg_norm_w[l][None], batch, seq)

    kvc = _compress(proj, jnp.stack([cmp_pos_k[l], cmp_pos_v[l]]),
                    jnp.stack([cmp_w1_k[l], cmp_w1_v[l]]).astype(BF16),
                    jnp.stack([cmp_b1_k[l], cmp_b1_v[l]])[:, None, :],
                    jnp.stack([cmp_w2_k[l], cmp_w2_v[l]]).astype(BF16), batch, seq)

    nd, npv, wb, tt = _bias_tables(rel_bias, seq)
    o_c, sel_t = _cmp_sel(proj, kvc, tt, _overlap_matrix(seq), batch, seq)
    o_w = _window(proj, wb, batch, seq)
    o_n = _selected(proj, sel_t, nd, npv, o_c, o_w, batch, seq)

    w_r = jnp.zeros((d, LANE), F32).at[:, :N_GROUPS].set(router_grp_w[l])
    w_r = w_r.at[:, N_GROUPS:N_GROUPS + N_EXPERTS].set(router_exp_w[l])
    b_r = jnp.zeros((1, LANE), F32).at[0, :N_GROUPS].set(router_grp_b[l])
    b_r = b_r.at[0, N_GROUPS:N_GROUPS + N_EXPERTS].set(router_exp_b[l])
    x1, h2, logits = _merge(o_h, o_n, proj, x2, mod,
                            w_br_hg[l].astype(BF16), w_br_nsa[l].astype(BF16), w_out[l].astype(BF16),
                            ln1_g[l][None], ln1_b[l][None], w_r, b_r, seq)

    rec, cnt, rec_t = _route(logits)
    counts = cnt[0, N_GROUPS:N_GROUPS + N_EXPERTS].astype(jnp.int32)
    padded = (counts + MOE_ROWS - 1) // MOE_ROWS * MOE_ROWS
    pend = jnp.cumsum(padded)
    pstart = pend - padded
    n_assign = 2 * nt
    nb = n_assign // MOE_ROWS + N_EXPERTS
    slots = rec_t.astype(jnp.int32)
    expert_ids = jnp.arange(N_EXPERTS, dtype=jnp.int32)[:, None]
    slot_base = lambda e: jnp.sum(jnp.where(e[None, :] == expert_ids, pstart[:, None], 0), axis=0)
    dest0 = slot_base(slots[0]) + slots[4]
    dest1 = slot_base(slots[1]) + slots[5]
    block_start = jnp.arange(nb, dtype=jnp.int32) * MOE_ROWS
    block_expert = jnp.minimum(jnp.sum(pend[None, :] <= block_start[:, None], axis=1),
                               N_EXPERTS - 1).astype(jnp.int32)
    n_used = pend[-1] // MOE_ROWS
    spare = n_used + jnp.arange(N_EXPERTS, dtype=jnp.int32)
    zero_blocks = jnp.concatenate([jnp.where(padded > 0, pend // MOE_ROWS - 1, -1),
                                   jnp.where(spare < nb, spare, -1)]).astype(jnp.int32)

    x_pad = _dispatch(dest0, dest1, zero_blocks, h2, nb)
    y_pad = _experts(block_expert, n_used[None].astype(jnp.int32), x_pad, exp_w1[l], exp_w3[l], exp_w2[l])
    out = _combine(dest0, dest1, y_pad, rec, x1, mod, ln2_g[l][None], ln2_b[l][None], seq)
    return out.reshape(batch, seq, d)
```

```python
import functools
import math

import numpy as np
import jax
import jax.numpy as jnp
from jax import lax
from jax.experimental import pallas as pl
from jax.experimental.pallas import tpu as pltpu

F32 = jnp.float32
BF16 = jnp.bfloat16

D_MODEL = 1024
HG_HEADS = 8
HG_DK = 128
HG_DV = 128
HG_CHUNK = 32
HG_SUPER = 256
NSA_HEADS = 8
NSA_GROUPS = 2
NSA_REP = NSA_HEADS // NSA_GROUPS
NSA_DK = 128
CMP_BLOCK = 32
CMP_STRIDE = 16
SLC_BLOCK = 64
SLC_TOPK = 16
WINDOW = 512
FORCE_SCORE = 1e4
N_FORCED = 3
REL_BUCKETS = 32
REL_MAX_DIST = 128
N_GROUPS = 4
EXP_PER_GROUP = 8
N_EXPERTS = N_GROUPS * EXP_PER_GROUP
D_EXPERT = D_MODEL // 2
DEPTH = 1
ALPHA = (2 * DEPTH) ** 0.25

LANE = 128
QT = 128
NEG = -1e30
SCALE = NSA_DK ** -0.5
LOG2E = math.log2(math.e)
SCALE_LOG2 = SCALE * LOG2E
KEY_BLK = 512
BOUND_SLACK = 1.0 + 2.0 ** -10
MAX_REF_GAP = 64.0
SEL_ROWS = 16
SEL_SUB = KEY_BLK // QT
WIN_SUB = 4
CMP_SUB = 4
CMP_PAD = 120
VMEM_LIMIT = 56 * 1024 * 1024

CB_HQ, CB_HF, CB_HI, CB_HG = 0, 8, 16, 24
CB_NQ = 32
CB_KC, CB_VC, CB_KS, CB_VS, CB_KW, CB_VW = 40, 42, 44, 46, 48, 50
CB_GATE = 52
CB_MGH, CB_MGN = 56, 64
NCB = 72
MAIN_COLS = 52 * LANE
GATE_COLS = 3 * NSA_HEADS

MERGE_SPLIT = 2
ROW_TILES = D_MODEL // LANE
MOE_ROWS = 512
ROW_DMA_UNROLL = 8


def _cparams(sem):
    return pltpu.CompilerParams(dimension_semantics=sem, vmem_limit_bytes=VMEM_LIMIT)


def _dot(a, b):
    return jnp.dot(a, b, preferred_element_type=F32)


def _dot_nt(a, b):
    return lax.dot_general(a, b, (((1,), (1,)), ((), ())), preferred_element_type=F32)


def _dot_tn(a, b):
    return lax.dot_general(a, b, (((0,), (0,)), ((), ())), preferred_element_type=F32)


def _split3(x):
    hi = x.astype(BF16)
    r = x - hi.astype(F32)
    mid = r.astype(BF16)
    lo = (r - mid.astype(F32)).astype(BF16)
    return hi, mid, lo


def _dot01(m01, x):
    hi, mid, lo = _split3(x)
    return _dot(m01, hi) + _dot(m01, mid) + _dot(m01, lo)


def _dot01_r(x, m01):
    hi, mid, lo = _split3(x)
    return _dot(hi, m01) + _dot(mid, m01) + _dot(lo, m01)


def _layer_norm(x, eps=1e-5):
    mu = jnp.mean(x, axis=-1, keepdims=True)
    xc = x - mu
    var = jnp.mean(xc * xc, axis=-1, keepdims=True)
    return xc * lax.rsqrt(var + eps)


def _adaln_kernel(c_ref, w_ref, b_ref, o_ref):
    c = c_ref[...]
    ca = c * jax.nn.sigmoid(c)
    o_ref[...] = jnp.dot(ca, w_ref[...], precision=lax.Precision.HIGHEST,
                         preferred_element_type=F32) + b_ref[...]


def _adaln(c_pad, w, b):
    rows, d = c_pad.shape
    n = w.shape[1]
    return pl.pallas_call(
        _adaln_kernel,
        grid=(n // d,),
        in_specs=[pl.BlockSpec((rows, d), lambda j: (0, 0)),
                  pl.BlockSpec((d, d), lambda j: (0, j)),
                  pl.BlockSpec((1, d), lambda j: (0, j))],
        out_specs=pl.BlockSpec((rows, d), lambda j: (0, j)),
        out_shape=jax.ShapeDtypeStruct((rows, n), F32),
        compiler_params=_cparams(("arbitrary",)),
        name="adaln",
    )(c_pad, w, b)


def _inproj_kernel(x_ref, mod_ref, w_ref, b_ref, o_ref, hn_ref, *, ncb_tile):
    @pl.when(pl.program_id(1) == 0)
    def _():
        hn = _layer_norm(x_ref[...])
        sh = mod_ref[0, 0:1, :]
        sc = mod_ref[0, 1:2, :]
        hn_ref[...] = (hn * (1.0 + sc) + sh).astype(BF16)

    res = _dot(hn_ref[...], w_ref[...]) + b_ref[...]
    for c in range(ncb_tile):
        o_ref[c] = res[:, c * LANE:(c + 1) * LANE]


def _inproj(x2, mod, w, b, seq):
    nt, d = x2.shape
    tm = min(2048, seq)
    ncb_tile = 6
    tn = ncb_tile * LANE
    return pl.pallas_call(
        functools.partial(_inproj_kernel, ncb_tile=ncb_tile),
        grid=(nt // tm, NCB // ncb_tile),
        in_specs=[pl.BlockSpec((tm, d), lambda i, j: (i, 0)),
                  pl.BlockSpec((1, 6, d), lambda i, j: (i * tm // seq, 0, 0)),
                  pl.BlockSpec((d, tn), lambda i, j: (0, j)),
                  pl.BlockSpec((1, tn), lambda i, j: (0, j))],
        out_specs=pl.BlockSpec((ncb_tile, tm, LANE), lambda i, j: (j, i, 0)),
        out_shape=jax.ShapeDtypeStruct((NCB, nt, LANE), F32),
        scratch_shapes=[pltpu.VMEM((tm, d), BF16)],
        compiler_params=_cparams(("arbitrary", "arbitrary")),
        name="inproj",
    )(x2, mod, w, b)


def _hgrn_kernel(q_ref, f_ref, v_ref, g_ref, lb_ref, nw_ref, o_ref, st_ref, *, rows):
    @pl.when(pl.program_id(1) == 0)
    def _():
        st_ref[...] = jnp.zeros_like(st_ref)

    sup = HG_SUPER
    ri = lax.broadcasted_iota(jnp.int32, (sup, sup), 0)
    ci = lax.broadcasted_iota(jnp.int32, (sup, sup), 1)
    same = (ri // HG_CHUNK) == (ci // HG_CHUNK)
    cum_m = jnp.where(same & (ci <= ri), 1.0, 0.0).astype(BF16)
    rt = lax.broadcasted_iota(jnp.int32, (LANE, LANE), 0)
    ct = lax.broadcasted_iota(jnp.int32, (LANE, LANE), 1)
    tril = ((rt // HG_CHUNK) == (ct // HG_CHUNK)) & (ct <= rt)
    per = sup // HG_CHUNK
    groups = [slice(g * LANE, (g + 1) * LANE) for g in range(sup // LANE)]

    heads = range(HG_HEADS)
    hs = [slice(h * LANE, (h + 1) * LANE) for h in heads]

    def wide(ref, r0):
        return jnp.concatenate([ref[h, pl.ds(r0, sup), :] for h in heads], axis=1)

    def body(i, carry):
        r0 = pl.multiple_of(i * sup, sup)
        lb = lb_ref[...]
        f = lb + (1.0 - lb) * jax.nn.sigmoid(wide(f_ref, r0))
        lf = jnp.log(f)
        k = 1.0 - f
        hi = lf.astype(BF16)
        lo = (lf - hi.astype(F32)).astype(BF16)
        b = _dot(cum_m, hi) + _dot(cum_m, lo)
        chunk = [slice(c * HG_CHUNK, (c + 1) * HG_CHUNK) for c in range(per)]
        dec = [jnp.exp(b[(c + 1) * HG_CHUNK - 1:(c + 1) * HG_CHUNK, :]) for c in range(per)]
        dec_rows = jnp.concatenate([jnp.broadcast_to(dec[c], (HG_CHUNK, dec[c].shape[1])) for c in range(per)],
                                   axis=0)
        q_in = (wide(q_ref, r0) * jnp.exp(b)).astype(BF16)
        k_dec = k * jnp.exp(-b)
        k_in = k_dec.astype(BF16)
        k_end = (k_dec * dec_rows).astype(BF16)
        vb = wide(v_ref, r0).astype(BF16)

        att = [[jnp.where(tril, _dot_nt(q_in[g, hs[h]], k_in[g, hs[h]]), 0.0).astype(BF16) for g in groups]
               for h in heads]
        upd = [[_dot_tn(vb[chunk[c], hs[h]], k_end[chunk[c], hs[h]]) for c in range(per)] for h in heads]
        intra = [[_dot(att[h][n], vb[g, hs[h]]) for n, g in enumerate(groups)] for h in heads]

        st = [st_ref[h] for h in heads]
        inter = [[] for _ in heads]
        for c in range(per):
            for h in heads:
                inter[h].append(_dot_nt(q_in[chunk[c], hs[h]], st[h].astype(BF16)))
                st[h] = dec[c][:, hs[h]] * st[h] + upd[h][c]
        for h in heads:
            st_ref[h] = st[h]

        o = [jnp.concatenate(intra[h], axis=0) + jnp.concatenate(inter[h], axis=0) for h in heads]
        scale = [lax.rsqrt(jnp.mean(o[h] * o[h], axis=-1, keepdims=True) + 1e-6) for h in heads]
        on = jnp.concatenate([o[h] * scale[h] for h in heads], axis=1)
        on = on * nw_ref[...] * jax.nn.sigmoid(wide(g_ref, r0))
        o_ref[pl.ds(r0, sup), :] = on.astype(BF16)
        return carry

    lax.fori_loop(0, rows // sup, body, 0)


def _hgrn(proj, lb, nw, batch, seq):
    nt = proj.shape[1]
    tb = min(512, seq)
    nblk = seq // tb

    def slab(cb0):
        return pl.BlockSpec((HG_HEADS, tb, LANE), lambda b, t: (cb0 // HG_HEADS, b * nblk + t, 0))

    vec = pl.BlockSpec((1, HG_HEADS * LANE), lambda b, t: (0, 0))
    return pl.pallas_call(
        functools.partial(_hgrn_kernel, rows=tb),
        grid=(batch, nblk),
        in_specs=[slab(CB_HQ), slab(CB_HF), slab(CB_HI), slab(CB_HG), vec, vec],
        out_specs=pl.BlockSpec((tb, HG_HEADS * HG_DV), lambda b, t: (b * nblk + t, 0)),
        out_shape=jax.ShapeDtypeStruct((nt, HG_HEADS * HG_DV), BF16),
        scratch_shapes=[pltpu.VMEM((HG_HEADS, HG_DV, HG_DK), F32)],
        compiler_params=_cparams(("arbitrary", "arbitrary")),
        name="hgrn2",
    )(proj, proj, proj, proj, lb, nw)


def _compress_kernel(x_ref, pos_ref, w1_ref, b1_ref, w2_ref, o_ref, *, ns):
    p0 = jnp.zeros((ns, LANE), F32)
    p1 = jnp.zeros((ns, LANE), F32)
    for j in range(CMP_STRIDE):
        tok = x_ref[0, pl.ds(j, ns, stride=CMP_STRIDE), :]
        rows = slice(j * LANE, (j + 1) * LANE)
        late = slice((CMP_STRIDE + j) * LANE, (CMP_STRIDE + j + 1) * LANE)
        p0 = p0 + _dot((tok + pos_ref[0, j:j + 1, :]).astype(BF16), w1_ref[0, rows, :])
        p1 = p1 + _dot((tok + pos_ref[0, CMP_STRIDE + j:CMP_STRIDE + j + 1, :]).astype(BF16), w1_ref[0, late, :])
    h = p0 + pltpu.roll(p1, ns - 1, axis=0) + b1_ref[0]
    a = h * jax.nn.sigmoid(h)
    out = _dot(a.astype(BF16), w2_ref[0])
    row = lax.broadcasted_iota(jnp.int32, out.shape, 0)
    out = jnp.where(row < ns - 1, out, 0.0)
    o_ref[0, 0, 0] = out


def _compress(proj, pos, w1, b1, w2, batch, seq):
    ns = seq // CMP_STRIDE
    np_rows = ns
    width = CMP_STRIDE * LANE
    return pl.pallas_call(
        functools.partial(_compress_kernel, ns=ns),
        grid=(2, batch, NSA_GROUPS),
        in_specs=[pl.BlockSpec((1, seq, LANE), lambda s, b, g: (CB_KC + NSA_GROUPS * s + g, b, 0)),
                  pl.BlockSpec((1, CMP_BLOCK, LANE), lambda s, b, g: (s, 0, 0)),
                  pl.BlockSpec((1, 2 * width, LANE), lambda s, b, g: (s, 0, 0)),
                  pl.BlockSpec((1, 1, LANE), lambda s, b, g: (s, 0, 0)),
                  pl.BlockSpec((1, LANE, LANE), lambda s, b, g: (s, 0, 0))],
        out_specs=pl.BlockSpec((1, 1, 1, np_rows, LANE), lambda s, b, g: (s, b, g, 0, 0)),
        out_shape=jax.ShapeDtypeStruct((2, batch, NSA_GROUPS, np_rows, LANE), F32),
        compiler_params=_cparams(("arbitrary", "arbitrary", "arbitrary")),
        name="nsa_compress",
    )(proj, pos, w1, b1, w2)


def _cmp_sel_kernel(q0, q1, q2, q3, kc_ref, vc_ref, gate_ref, tt_ref, ov_ref, oc_ref, sel_ref,
                    kb_ref, vt_ref, imp_ref, *, np_rows):
    step = pl.program_id(2)
    cols = NSA_REP * QT
    ns = np_rows

    @pl.when(step == 0)
    def _():
        kb_ref[...] = (kc_ref[0, 0, 0] * SCALE_LOG2).astype(BF16)
        for c in range(np_rows // LANE):
            vt_ref[:, c * LANE:(c + 1) * LANE] = vc_ref[0, 0, 0, c * LANE:(c + 1) * LANE, :].T.astype(BF16)

    subs = range(CMP_SUB)
    tis = [step * CMP_SUB + u for u in subs]
    qts = [_q_transposed((q0, q1, q2, q3), u) for u in subs]

    def group_sum(p):
        tot = p[:, 0:QT]
        for r in range(1, NSA_REP):
            tot = tot + p[:, r * QT:(r + 1) * QT]
        return tot

    def attend(nrows):
        kb = kb_ref[0:nrows, :]
        bias = [tt_ref[0, pl.ds(pl.multiple_of(ns + CMP_PAD - tis[u] * (QT // CMP_STRIDE), 8), nrows), :]
                for u in subs]
        ss = [_dot(kb, qts[u]) + bias[u] for u in subs]
        ms = [jnp.max(ss[u], axis=0, keepdims=True) for u in subs]
        ps = [jnp.exp2(ss[u] - ms[u]) for u in subs]
        ls = [jnp.sum(ps[u], axis=0, keepdims=True) for u in subs]
        invs = [jnp.where(ms[u] > 0.5 * NEG, 1.0 / ls[u], 0.0) for u in subs]
        pn = [ps[u] * invs[u] for u in subs]
        vt = vt_ref[:, 0:nrows]
        os_ = [_dot(vt, pn[u].astype(BF16)) for u in subs]
        for u in subs:
            rows = slice(u * QT, (u + 1) * QT)
            gt = jax.nn.sigmoid(gate_ref[0, rows, :])
            for r in range(NSA_REP):
                oc_ref[rows, r * LANE:(r + 1) * LANE] = gt[:, 3 * r:3 * r + 1] * os_[u][:, r * QT:(r + 1) * QT].T
            imp_ref[:, rows] = _dot01(ov_ref[:, 0:nrows], group_sum(pn[u]))

    visible = (step + 1) * CMP_SUB * (QT // CMP_STRIDE)
    ngroups = np_rows // LANE
    for g in range(1, ngroups + 1):
        upper = visible <= g * LANE if g < ngroups else True

        @pl.when((visible > (g - 1) * LANE) & upper)
        def _():
            attend(g * LANE)

    imp = imp_ref[...]
    width = CMP_SUB * QT
    jj = lax.broadcasted_iota(jnp.int32, (LANE, width), 0)
    tok = step * width + lax.broadcasted_iota(jnp.int32, (LANE, width), 1)
    cur = tok // SLC_BLOCK
    forced = (jj == 0) | (jj == cur) | (jj == cur - 1)
    assert FORCE_SCORE > NSA_REP
    score = jnp.where(forced, -jnp.inf, jnp.where(jj <= cur, imp, -1.0))
    selb = jnp.where(forced, 0.0, NEG)
    jf = jj.astype(F32)
    for _ in range(SLC_TOPK - N_FORCED):
        mval = jnp.max(score, axis=0, keepdims=True)
        first = jnp.min(jnp.where(score == mval, jf, float(LANE)), axis=0, keepdims=True)
        pick = jf == first
        selb = jnp.where(pick, 0.0, selb)
        score = jnp.where(pick, -jnp.inf, score)
    sel_ref[0, 0] = selb


def _q_specs(nsteps, rows=QT):
    return [pl.BlockSpec((1, rows, LANE),
                         functools.partial(lambda b, g, t, r: (CB_NQ + NSA_REP * g + r, b * nsteps + t, 0), r=r))
            for r in range(NSA_REP)]


def _cmp_sel(proj, kvc, tt, ov_t, batch, seq):
    nt = proj.shape[1]
    rows = CMP_SUB * QT
    nsteps = seq // rows
    cols = NSA_REP * QT
    np_rows = kvc.shape[3]
    kv_spec = lambda s: pl.BlockSpec((1, 1, 1, np_rows, LANE), lambda b, g, t: (s, b, g, 0, 0))
    return pl.pallas_call(
        functools.partial(_cmp_sel_kernel, np_rows=np_rows),
        grid=(batch, NSA_GROUPS, nsteps),
        in_specs=_q_specs(nsteps, rows) + [
            kv_spec(0), kv_spec(1),
            pl.BlockSpec((1, rows, LANE), lambda b, g, t: (CB_GATE + g, b * nsteps + t, 0)),
            pl.BlockSpec((1, tt.shape[1], cols), lambda b, g, t: (g, 0, 0)),
            pl.BlockSpec((LANE, np_rows), lambda b, g, t: (0, 0))],
        out_specs=[pl.BlockSpec((rows, cols), lambda b, g, t: (b * nsteps + t, g)),
                   pl.BlockSpec((1, 1, LANE, rows), lambda b, g, t: (b, g, 0, t))],
        out_shape=[jax.ShapeDtypeStruct((nt, NSA_HEADS * LANE), F32),
                   jax.ShapeDtypeStruct((batch, NSA_GROUPS, LANE, seq), F32)],
        scratch_shapes=[pltpu.VMEM((np_rows, LANE), BF16), pltpu.VMEM((LANE, np_rows), BF16),
                        pltpu.VMEM((LANE, rows), F32)],
        compiler_params=_cparams(("arbitrary", "arbitrary", "arbitrary")),
        name="nsa_cmp_select",
    )(proj, proj, proj, proj, kvc, kvc, proj, tt, ov_t)


def _q_transposed(q_refs, sub=0):
    return jnp.concatenate([r[0, sub * QT:(sub + 1) * QT, :].T for r in q_refs], axis=1).astype(BF16)


def _softmax_step(s, vt, state):
    m, l, acc = state
    m_new = jnp.maximum(m, jnp.max(s, axis=0, keepdims=True))
    alpha = jnp.exp2(m - m_new)
    p = jnp.exp2(s - m_new)
    l = alpha * l + jnp.sum(p, axis=0, keepdims=True)
    acc = alpha * acc + _dot(vt, p.astype(BF16))
    return m_new, l, acc


def _finish_t(state, gate_ref, branch, o_ref, sub=0, others=()):
    _, l, acc = state
    o = acc / jnp.where(l == 0.0, 1.0, l)
    rows = slice(sub * QT, (sub + 1) * QT)
    gt = jax.nn.sigmoid(gate_ref[0, rows, :])
    for r in range(NSA_REP):
        col = 3 * r + branch
        lanes = slice(r * LANE, (r + 1) * LANE)
        val = gt[:, col:col + 1] * o[:, r * QT:(r + 1) * QT].T
        for other in others:
            val = val + other[rows, lanes]
        o_ref[rows, lanes] = val.astype(o_ref.dtype)


def _init_state(cols):
    return (jnp.full((1, cols), NEG, F32), jnp.zeros((1, cols), F32), jnp.zeros((LANE, cols), F32))


def _win_kernel(q0, q1, q2, q3, k_ref, v_ref, gate_ref, wb_ref, o_ref, kb_ref, vt_ref, *, seq):
    step = pl.program_id(2)
    cols = NSA_REP * QT
    wt = WINDOW // QT

    lane = lax.broadcasted_iota(jnp.int32, (QT, LANE), 1)
    pad_mark = jnp.where(lane == 0, 1.0, 0.0).astype(BF16)

    @pl.when(step == 0)
    def _():
        for i in range(wt):
            kb_ref[i * QT:(i + 1) * QT, 0:LANE] = jnp.zeros((QT, LANE), BF16)
            kb_ref[i * QT:(i + 1) * QT, LANE:2 * LANE] = pad_mark
            vt_ref[i] = jnp.zeros((LANE, QT), BF16)

        def fill(i, carry):
            r0 = pl.multiple_of(i * QT, QT)
            kb_ref[pl.ds(WINDOW + r0, QT), 0:LANE] = (k_ref[0, pl.ds(r0, QT), :] * SCALE_LOG2).astype(BF16)
            kb_ref[pl.ds(WINDOW + r0, QT), LANE:2 * LANE] = jnp.zeros((QT, LANE), BF16)
            vt_ref[wt + i] = v_ref[0, pl.ds(r0, QT), :].T.astype(BF16)
            return carry

        lax.fori_loop(0, seq // QT, fill, 0)

    subs = range(WIN_SUB)
    tis = [step * WIN_SUB + sub for sub in subs]
    row = lax.broadcasted_iota(jnp.int32, (LANE, cols), 0)
    pad_rows = jnp.where(row == 0, NEG, 0.0).astype(BF16)
    qts = [jnp.concatenate([_q_transposed((q0, q1, q2, q3), sub), pad_rows], axis=0) for sub in subs]
    ss = [_dot(kb_ref[pl.ds(pl.multiple_of(tis[u] * QT, QT), WINDOW + QT), :], qts[u]) + wb_ref[0]
          for u in subs]
    ms = [jnp.max(ss[u], axis=0, keepdims=True) for u in subs]
    ps = [jnp.exp2(ss[u] - ms[u]) for u in subs]
    ls = [jnp.sum(ps[u], axis=0, keepdims=True) for u in subs]
    pbs = [ps[u].astype(BF16) for u in subs]
    accs = [_dot(vt_ref[tis[u]], pbs[u][0:QT]) for u in subs]
    for d in range(1, wt + 1):
        accs = [accs[u] + _dot(vt_ref[tis[u] + d], pbs[u][d * QT:(d + 1) * QT]) for u in subs]
    for u in subs:
        _finish_t((ms[u], ls[u], accs[u]), gate_ref, 2, o_ref, u)


def _window(proj, wb, batch, seq):
    nt = proj.shape[1]
    rows = WIN_SUB * QT
    nqt = seq // rows
    cols = NSA_REP * QT
    slab = lambda cb0: pl.BlockSpec((1, seq, LANE), lambda b, g, t: (cb0 + g, b, 0))
    return pl.pallas_call(
        functools.partial(_win_kernel, seq=seq),
        grid=(batch, NSA_GROUPS, nqt),
        in_specs=_q_specs(nqt, rows) + [
            slab(CB_KW), slab(CB_VW),
            pl.BlockSpec((1, rows, LANE), lambda b, g, t: (CB_GATE + g, b * nqt + t, 0)),
            pl.BlockSpec((1, WINDOW + QT, cols), lambda b, g, t: (g, 0, 0))],
        out_specs=pl.BlockSpec((rows, cols), lambda b, g, t: (b * nqt + t, g)),
        out_shape=jax.ShapeDtypeStruct((nt, NSA_HEADS * LANE), F32),
        scratch_shapes=[pltpu.VMEM((seq + WINDOW, 2 * LANE), BF16),
                        pltpu.VMEM((seq // QT + WINDOW // QT, LANE, QT), BF16)],
        compiler_params=_cparams(("arbitrary", "arbitrary", "arbitrary")),
        name="nsa_window",
    )(proj, proj, proj, proj, proj, proj, proj, wb)


def _softmax_steps(ss, vt, states):
    n = range(len(ss))
    m_new = [jnp.maximum(states[u][0], jnp.max(ss[u], axis=0, keepdims=True)) for u in n]
    alpha = [jnp.exp2(states[u][0] - m_new[u]) for u in n]
    p = [jnp.exp2(ss[u] - m_new[u]) for u in n]
    l = [alpha[u] * states[u][1] + jnp.sum(p[u], axis=0, keepdims=True) for u in n]
    vts = vt if isinstance(vt, (list, tuple)) else [vt] * len(ss)
    pv = [_dot(vts[u], p[u].astype(BF16)) for u in n]
    return [(m_new[u], l[u], alpha[u] * states[u][2] + pv[u]) for u in n]


def _sel_kernel(q0, q1, q2, q3, k_ref, v_ref, sel_ref, gate_ref, nd_ref, npv_ref, oc_ref, ow_ref, o_ref,
                kb_ref, vt_ref, qa_ref, kn_ref, m_ref, l_ref, acc_ref, *, seq):
    step = pl.program_id(2)
    cols = NSA_REP * QT
    spb = KEY_BLK // SLC_BLOCK
    subs = range(SEL_SUB)

    @pl.when(step == 0)
    def _():
        blk = lax.broadcasted_iota(jnp.int32, (KEY_BLK, LANE), 0) // SLC_BLOCK
        onehot = jnp.where(blk == lax.broadcasted_iota(jnp.int32, (KEY_BLK, LANE), 1), 1.0, 0.0).astype(BF16)

        def fill(i, kmax2):
            r0 = pl.multiple_of(i * KEY_BLK, KEY_BLK)
            kblk = (k_ref[0, pl.ds(r0, KEY_BLK), :] * SCALE_LOG2).astype(BF16)
            kb_ref[pl.ds(r0, KEY_BLK), 0:LANE] = kblk
            kb_ref[pl.ds(r0, KEY_BLK), LANE:2 * LANE] = onehot
            for c in range(SEL_SUB):
                vt_ref[i, :, c * QT:(c + 1) * QT] = v_ref[0, pl.ds(r0 + c * QT, QT), :].T.astype(BF16)
            n2 = jnp.sum(jnp.square(kblk.astype(F32)), axis=1, keepdims=True)
            return jnp.maximum(kmax2, jnp.max(n2, axis=0, keepdims=True))

        kmax2 = lax.fori_loop(0, seq // KEY_BLK, fill, jnp.zeros((1, 1), F32))
        kn_ref[...] = jnp.sqrt(kmax2)
        qa_ref[...] = jnp.zeros_like(qa_ref)

    qts = [_q_transposed((q0, q1, q2, q3), u) for u in subs]
    for u in subs:
        qa_ref[u, 0:LANE, :] = qts[u]
    bounds = [jnp.sqrt(jnp.sum(jnp.square(qts[u].astype(F32)), axis=0, keepdims=True)) * kn_ref[...]
              * BOUND_SLACK + 1.0 for u in subs]

    def block(kb):
        r0 = pl.multiple_of(kb * KEY_BLK, KEY_BLK)
        return kb_ref[pl.ds(r0, KEY_BLK), :], vt_ref[kb]

    pad_rows = jnp.zeros((SEL_ROWS - spb, cols), F32)

    def scores(kb, k, rows_of=lambda u: KEY_BLK):
        for u in subs:
            rows = sel_ref[0, 0, pl.ds(pl.multiple_of(kb * spb, spb), spb), u * QT:(u + 1) * QT]
            rows = jnp.concatenate([rows] * NSA_REP, axis=1)
            qa_ref[u, LANE:LANE + SEL_ROWS, :] = jnp.concatenate([rows, pad_rows], axis=0).astype(BF16)
        return [_dot(k[0:rows_of(u)], qa_ref[u]) for u in subs]

    seen = lambda u: (u + 1) * QT
    k, vt = block(step)
    ss = scores(step, k, seen)
    ss = [ss[u] + nd_ref[0, u, 0:seen(u), :] for u in subs]
    states = _softmax_steps(ss, [vt[:, 0:seen(u)] for u in subs], [_init_state(cols) for _ in subs])

    prev = jnp.maximum(step - 1, 0)
    k, vt = block(prev)
    ss = scores(prev, k)
    ss[0] = ss[0] + npv_ref[0]
    exists = (step + jnp.zeros((KEY_BLK, cols), jnp.int32)) >= 1
    ss = [jnp.where(exists, ss[u], NEG) for u in subs]
    states = _softmax_steps(ss, vt, states)
    for u in subs:
        m_ref[u], l_ref[u], acc_ref[u] = states[u]

    n_old = jnp.maximum(step - 1, 0)
    m_fix = [jnp.maximum(states[u][0], bounds[u]) for u in subs]
    gap = jnp.max(jnp.concatenate([m_fix[u] - states[u][0] for u in subs], axis=1))
    fixed_ok = gap < MAX_REF_GAP

    @pl.when(fixed_ok)
    def _():
        for u in subs:
            alpha = jnp.exp2(m_ref[u] - m_fix[u])
            l_ref[u] = alpha * l_ref[u]
            acc_ref[u] = alpha * acc_ref[u]
            m_ref[u] = m_fix[u]

        def body(kb, carry):
            k, vt = block(kb)
            ss = scores(kb, k)
            ps = [jnp.exp2(ss[u] - m_ref[u]) for u in subs]
            for u in subs:
                l_ref[u] = l_ref[u] + jnp.sum(ps[u], axis=0, keepdims=True)
            pv = [_dot(vt, ps[u].astype(BF16)) for u in subs]
            for u in subs:
                acc_ref[u] = acc_ref[u] + pv[u]
            return carry

        lax.fori_loop(0, n_old, body, 0)

    @pl.when(jnp.logical_not(fixed_ok))
    def _():
        def body(kb, carry):
            k, vt = block(kb)
            st = _softmax_steps(scores(kb, k), vt, [(m_ref[u], l_ref[u], acc_ref[u]) for u in subs])
            for u in subs:
                m_ref[u], l_ref[u], acc_ref[u] = st[u]
            return carry

        lax.fori_loop(0, n_old, body, 0)

    for u in subs:
        _finish_t((m_ref[u], l_ref[u], acc_ref[u]), gate_ref, 1, o_ref, u, others=(oc_ref, ow_ref))


def _selected(proj, sel_t, nd, npv, o_c, o_w, batch, seq):
    nt = proj.shape[1]
    rows = SEL_SUB * QT
    nsteps = seq // rows
    cols = NSA_REP * QT
    slab = lambda cb0: pl.BlockSpec((1, seq, LANE), lambda b, g, t: (cb0 + g, b, 0))
    return pl.pallas_call(
        functools.partial(_sel_kernel, seq=seq),
        grid=(batch, NSA_GROUPS, nsteps),
        in_specs=_q_specs(nsteps, rows) + [
            slab(CB_KS), slab(CB_VS),
            pl.BlockSpec((1, 1, LANE, rows), lambda b, g, t: (b, g, 0, t)),
            pl.BlockSpec((1, rows, LANE), lambda b, g, t: (CB_GATE + g, b * nsteps + t, 0)),
            pl.BlockSpec((1, SEL_SUB, KEY_BLK, cols), lambda b, g, t: (g, 0, 0, 0)),
            pl.BlockSpec((1, KEY_BLK, cols), lambda b, g, t: (g, 0, 0)),
            pl.BlockSpec((rows, cols), lambda b, g, t: (b * nsteps + t, g)),
            pl.BlockSpec((rows, cols), lambda b, g, t: (b * nsteps + t, g))],
        out_specs=pl.BlockSpec((rows, cols), lambda b, g, t: (b * nsteps + t, g)),
        out_shape=jax.ShapeDtypeStruct((nt, NSA_HEADS * LANE), BF16),
        scratch_shapes=[pltpu.VMEM((seq, 2 * LANE), BF16),
                        pltpu.VMEM((seq // KEY_BLK, LANE, KEY_BLK), BF16),
                        pltpu.VMEM((SEL_SUB, 2 * LANE, cols), BF16),
                        pltpu.VMEM((1, 1), F32),
                        pltpu.VMEM((SEL_SUB, 1, cols), F32), pltpu.VMEM((SEL_SUB, 1, cols), F32),
                        pltpu.VMEM((SEL_SUB, LANE, cols), F32)],
        compiler_params=_cparams(("arbitrary", "arbitrary", "arbitrary")),
        name="nsa_selected",
    )(proj, proj, proj, proj, proj, proj, sel_t, proj, nd, npv, o_c, o_w)


def _merge_kernel(oh_ref, on_ref, mgh_ref, mgn_ref, x_ref, mod_ref,
                  wh_ref, wn_ref, wo_ref, g_ref, b_ref, wr_ref, br_ref,
                  x1_ref, h2_ref, lg_ref):
    nblk = D_MODEL // LANE
    tm = x_ref.shape[0]
    halves = [slice(s * tm // MERGE_SPLIT, (s + 1) * tm // MERGE_SPLIT) for s in range(MERGE_SPLIT)]
    a_h = [_dot(oh_ref[rs, :], wh_ref[...]) for rs in halves]
    a_n = [_dot(on_ref[rs, :], wn_ref[...]) for rs in halves]
    gh = [jnp.concatenate([mgh_ref[c, rs, :] for c in range(nblk)], axis=-1) for rs in halves]
    gn = [jnp.concatenate([mgn_ref[c, rs, :] for c in range(nblk)], axis=-1) for rs in halves]
    merged = [(jax.nn.sigmoid(gh[s]) * a_h[s] + jax.nn.sigmoid(gn[s]) * a_n[s]).astype(BF16)
              for s in range(MERGE_SPLIT)]
    y = [(1.0 + mod_ref[0, 2:3, :]) * _dot(merged[s], wo_ref[...]) for s in range(MERGE_SPLIT)]
    x1 = [_layer_norm(ALPHA * x_ref[rs, :] + y[s]) * g_ref[...] + b_ref[...] for s, rs in enumerate(halves)]
    h2 = [_layer_norm(x1[s]) * (1.0 + mod_ref[0, 4:5, :]) + mod_ref[0, 3:4, :] for s in range(MERGE_SPLIT)]
    lg = [jnp.dot(h2[s], wr_ref[...], precision=lax.Precision.HIGHEST, preferred_element_type=F32) + br_ref[...]
          for s in range(MERGE_SPLIT)]
    for s, rs in enumerate(halves):
        x1_ref[rs, :] = x1[s]
        _store_rows(h2_ref, h2[s], rs.start)
        lg_ref[rs, :] = lg[s]


def _merge(o_h, o_n, proj, x2, mod, w_h, w_n, w_o, ln_g, ln_b, w_r, b_r, seq):
    nt, d = x2.shape
    tm = min(512, seq)
    nblk = d // LANE
    row = lambda w: pl.BlockSpec((tm, w), lambda i: (i, 0))
    full = lambda a: pl.BlockSpec(a.shape, lambda i: (0,) * a.ndim)
    return pl.pallas_call(
        _merge_kernel,
        grid=(nt // tm,),
        in_specs=[row(d), row(d),
                  pl.BlockSpec((nblk, tm, LANE), lambda i: (CB_MGH // nblk, i, 0)),
                  pl.BlockSpec((nblk, tm, LANE), lambda i: (CB_MGN // nblk, i, 0)),
                  row(d),
                  pl.BlockSpec((1, 6, d), lambda i: (i * tm // seq, 0, 0)),
                  full(w_h), full(w_n), full(w_o), full(ln_g), full(ln_b), full(w_r), full(b_r)],
        out_specs=[row(d), pl.BlockSpec((tm * ROW_TILES, LANE), lambda i: (i, 0)), row(LANE)],
        out_shape=[jax.ShapeDtypeStruct((nt, d), F32),
                   jax.ShapeDtypeStruct((nt * ROW_TILES, LANE), F32),
                   jax.ShapeDtypeStruct((nt, LANE), F32)],
        compiler_params=_cparams(("arbitrary",)),
        name="merge_outproj",
    )(o_h, o_n, proj, proj, x2, mod, w_h, w_n, w_o, ln_g, ln_b, w_r, b_r)


def _route_kernel(lg_ref, rec_ref, cnt_ref, rect_ref, carry_ref, *, tm):
    @pl.when(pl.program_id(0) == 0)
    def _():
        carry_ref[...] = jnp.zeros_like(carry_ref)

    lg = lg_ref[...]
    lane = lax.broadcasted_iota(jnp.int32, (tm, LANE), 1).astype(F32)
    far = float(LANE)
    gmask = lane < N_GROUPS
    gl = jnp.where(gmask, lg, -jnp.inf)
    gmax = jnp.max(gl, axis=-1, keepdims=True)
    gsum = jnp.sum(jnp.where(gmask, jnp.exp(gl - gmax), 0.0), axis=-1, keepdims=True)
    grp_p = 1.0 / gsum
    gidx = jnp.min(jnp.where(gl == gmax, lane, far), axis=-1, keepdims=True)
    lo = N_GROUPS + EXP_PER_GROUP * gidx
    emask = (lane >= lo) & (lane < lo + EXP_PER_GROUP)
    el = jnp.where(emask, lg, -jnp.inf)
    m1 = jnp.max(el, axis=-1, keepdims=True)
    i1 = jnp.min(jnp.where(el == m1, lane, far), axis=-1, keepdims=True)
    el2 = jnp.where(lane == i1, -jnp.inf, el)
    m2 = jnp.max(el2, axis=-1, keepdims=True)
    i2 = jnp.min(jnp.where(emask & (lane != i1) & (el2 == m2), lane, far), axis=-1, keepdims=True)
    e = jnp.exp(m2 - m1)
    w0 = grp_p / (1.0 + e)
    w1 = grp_p * e / (1.0 + e)

    oh0 = lane == i1
    oh1 = lane == i2
    f0 = jnp.where(oh0, 1.0, 0.0)
    f1 = jnp.where(oh1, 1.0, 0.0)
    ri = lax.broadcasted_iota(jnp.int32, (tm, tm), 0)
    ci = lax.broadcasted_iota(jnp.int32, (tm, tm), 1)
    before = jnp.where(ci < ri, 1.0, 0.0).astype(BF16)
    cum0 = _dot(before, f0.astype(BF16))
    cum1 = _dot(before, f1.astype(BF16))
    tot0 = jnp.sum(f0, axis=0, keepdims=True)
    tot1 = jnp.sum(f1, axis=0, keepdims=True)
    carry = carry_ref[...]
    rank0 = jnp.sum(jnp.where(oh0, carry + cum0, 0.0), axis=-1, keepdims=True)
    rank1 = jnp.sum(jnp.where(oh1, carry + tot0 + cum1, 0.0), axis=-1, keepdims=True)
    carry = carry + tot0 + tot1
    carry_ref[...] = carry
    cnt_ref[...] = carry

    rec = jnp.where(lane == 0, i1 - N_GROUPS, 0.0)
    rec = jnp.where(lane == 1, i2 - N_GROUPS, rec)
    rec = jnp.where(lane == 2, w0, rec)
    rec = jnp.where(lane == 3, w1, rec)
    rec = jnp.where(lane == 4, rank0, rec)
    rec = jnp.where(lane == 5, rank1, rec)
    rec_ref[...] = rec
    rect_ref[...] = jnp.concatenate([rec[c * LANE:(c + 1) * LANE, :].T[0:8, :] for c in range(tm // LANE)], axis=1)


def _route(logits):
    nt = logits.shape[0]
    tm = min(512, nt)
    return pl.pallas_call(
        functools.partial(_route_kernel, tm=tm),
        grid=(nt // tm,),
        in_specs=[pl.BlockSpec((tm, LANE), lambda i: (i, 0))],
        out_specs=[pl.BlockSpec((tm, LANE), lambda i: (i, 0)),
                   pl.BlockSpec((1, LANE), lambda i: (0, 0)),
                   pl.BlockSpec((8, tm), lambda i: (0, i))],
        out_shape=[jax.ShapeDtypeStruct((nt, LANE), F32),
                   jax.ShapeDtypeStruct((1, LANE), F32),
                   jax.ShapeDtypeStruct((8, nt), F32)],
        scratch_shapes=[pltpu.VMEM((1, LANE), F32)],
        compiler_params=_cparams(("arbitrary",)),
        name="moe_route",
    )(logits)


def _row_copy(src, dst, sem):
    return pltpu.make_async_copy(src, dst, sem)


def _tile_of(r):
    return pl.ds(pl.multiple_of(r * ROW_TILES, ROW_TILES), ROW_TILES)


def _load_rows(ref, n, lead=()):
    return jnp.concatenate([ref[lead + (pl.ds(c, n, stride=ROW_TILES), slice(None))]
                            for c in range(ROW_TILES)], axis=1)


def _store_rows(ref, val, row0=0):
    n = val.shape[0]
    for c in range(ROW_TILES):
        ref[pl.ds(row0 * ROW_TILES + c, n, stride=ROW_TILES), :] = val[:, c * LANE:(c + 1) * LANE]


def _dispatch_kernel(dest0_ref, dest1_ref, zb_ref, h_ref, xp_ref, z_ref, sem, zsem, *, tm):
    step = pl.program_id(0)
    base = step * tm
    blk = MOE_ROWS * ROW_TILES

    @pl.when(step == 0)
    def _():
        z_ref[...] = jnp.zeros_like(z_ref)

        def zero_copy(j):
            b = jnp.maximum(zb_ref[j], 0)
            return _row_copy(z_ref, xp_ref.at[pl.ds(pl.multiple_of(b * blk, blk), blk), :], zsem)

        def start(j, carry):
            @pl.when(zb_ref[j] >= 0)
            def _():
                zero_copy(j).start()
            return carry

        def wait(j, carry):
            @pl.when(zb_ref[j] >= 0)
            def _():
                zero_copy(j).wait()
            return carry

        lax.fori_loop(0, 2 * N_EXPERTS, start, 0)
        lax.fori_loop(0, 2 * N_EXPERTS, wait, 0)

    def issue(r, carry):
        for dest_ref in (dest0_ref, dest1_ref):
            d = dest_ref[base + r]
            _row_copy(h_ref.at[_tile_of(r), :], xp_ref.at[_tile_of(d), :], sem).start()
        return carry

    lax.fori_loop(0, tm, issue, 0, unroll=ROW_DMA_UNROLL)
    for _ in range(2):
        _row_copy(h_ref, xp_ref.at[pl.ds(0, tm * ROW_TILES), :], sem).wait()


def _dispatch(dest0, dest1, zero_blocks, h2, n_blocks):
    nt = h2.shape[0] // ROW_TILES
    tm = min(256, nt)
    return pl.pallas_call(
        functools.partial(_dispatch_kernel, tm=tm),
        grid_spec=pltpu.PrefetchScalarGridSpec(
            num_scalar_prefetch=3,
            grid=(nt // tm,),
            in_specs=[pl.BlockSpec((tm * ROW_TILES, LANE), lambda i, *_: (i, 0))],
            out_specs=pl.BlockSpec(memory_space=pl.ANY),
            scratch_shapes=[pltpu.VMEM((MOE_ROWS * ROW_TILES, LANE), F32),
                            pltpu.SemaphoreType.DMA(()), pltpu.SemaphoreType.DMA(())]),
        out_shape=jax.ShapeDtypeStruct((n_blocks * MOE_ROWS * ROW_TILES, LANE), F32),
        compiler_params=_cparams(("arbitrary",)),
        name="moe_dispatch",
    )(dest0, dest1, zero_blocks, h2)


def _expert_kernel(be_ref, nu_ref, x_ref, w1_ref, w3_ref, w2_ref, y_ref, w1b_ref, w3b_ref, w2b_ref):
    i = pl.program_id(0)

    @pl.when((i < nu_ref[0]) & ((i == 0) | (be_ref[i] != be_ref[jnp.maximum(i - 1, 0)])))
    def _():
        w1b_ref[...] = w1_ref[0].astype(BF16)
        w3b_ref[...] = w3_ref[0].astype(BF16)
        w2b_ref[...] = w2_ref[0].astype(BF16)

    @pl.when(i < nu_ref[0])
    def _():
        xb = _load_rows(x_ref, MOE_ROWS).astype(BF16)
        a = _dot(xb, w1b_ref[...])
        b = _dot(xb, w3b_ref[...])
        hmid = (a * jax.nn.sigmoid(a) * b).astype(BF16)
        _store_rows(y_ref, _dot(hmid, w2b_ref[...]))

    @pl.when(i >= nu_ref[0])
    def _():
        y_ref[...] = jnp.zeros_like(y_ref)


def _experts(block_expert, n_used, x_pad, w1, w3, w2):
    d, de = w1.shape[1], w1.shape[2]
    nb = x_pad.shape[0] // (MOE_ROWS * ROW_TILES)
    return pl.pallas_call(
        _expert_kernel,
        grid_spec=pltpu.PrefetchScalarGridSpec(
            num_scalar_prefetch=2,
            grid=(nb,),
            in_specs=[pl.BlockSpec((MOE_ROWS * ROW_TILES, LANE), lambda i, be, nu: (i, 0)),
                      pl.BlockSpec((1, d, de), lambda i, be, nu: (be[i], 0, 0)),
                      pl.BlockSpec((1, d, de), lambda i, be, nu: (be[i], 0, 0)),
                      pl.BlockSpec((1, de, d), lambda i, be, nu: (be[i], 0, 0))],
            out_specs=pl.BlockSpec((MOE_ROWS * ROW_TILES, LANE), lambda i, be, nu: (i, 0)),
            scratch_shapes=[pltpu.VMEM((d, de), BF16), pltpu.VMEM((d, de), BF16), pltpu.VMEM((de, d), BF16)]),
        out_shape=jax.ShapeDtypeStruct(x_pad.shape, F32),
        compiler_params=_cparams(("arbitrary",)),
        name="moe_experts",
    )(block_expert, n_used, x_pad, w1, w3, w2)


def _combine_kernel(dest0_ref, dest1_ref, yp_ref, rec_ref, x1_ref, mod_ref, g_ref, b_ref, o_ref,
                    buf_ref, sem, *, tm):
    step = pl.program_id(0)
    slot = step % 2

    def fetch(tile, to_slot):
        def issue(r, carry):
            for k, dest_ref in enumerate((dest0_ref, dest1_ref)):
                d = dest_ref[tile * tm + r]
                _row_copy(yp_ref.at[_tile_of(d), :], buf_ref.at[to_slot, k, _tile_of(r), :],
                          sem.at[to_slot]).start()
            return carry

        lax.fori_loop(0, tm, issue, 0, unroll=ROW_DMA_UNROLL)

    @pl.when(step == 0)
    def _():
        fetch(0, 0)

    @pl.when(step + 1 < pl.num_programs(0))
    def _():
        fetch(step + 1, 1 - slot)

    for k in range(2):
        _row_copy(yp_ref.at[pl.ds(0, tm * ROW_TILES), :], buf_ref.at[slot, k], sem.at[slot]).wait()

    rec = rec_ref[...]
    y = rec[:, 2:3] * _load_rows(buf_ref, tm, (slot, 0)) + rec[:, 3:4] * _load_rows(buf_ref, tm, (slot, 1))
    y = (1.0 + mod_ref[0, 5:6, :]) * y
    o_ref[...] = _layer_norm(ALPHA * x1_ref[...] + y) * g_ref[...] + b_ref[...]


def _combine(dest0, dest1, y_pad, rec, x1, mod, ln_g, ln_b, seq):
    nt, d = x1.shape
    tm = min(256, seq)
    return pl.pallas_call(
        functools.partial(_combine_kernel, tm=tm),
        grid_spec=pltpu.PrefetchScalarGridSpec(
            num_scalar_prefetch=2,
            grid=(nt // tm,),
            in_specs=[pl.BlockSpec(memory_space=pl.ANY),
                      pl.BlockSpec((tm, LANE), lambda i, *_: (i, 0)),
                      pl.BlockSpec((tm, d), lambda i, *_: (i, 0)),
                      pl.BlockSpec((1, 6, d), lambda i, *_: (i * tm // seq, 0, 0)),
                      pl.BlockSpec((1, d), lambda i, *_: (0, 0)),
                      pl.BlockSpec((1, d), lambda i, *_: (0, 0))],
            out_specs=pl.BlockSpec((tm, d), lambda i, *_: (i, 0)),
            scratch_shapes=[pltpu.VMEM((2, 2, tm * ROW_TILES, LANE), F32), pltpu.SemaphoreType.DMA((2,))]),
        out_shape=jax.ShapeDtypeStruct((nt, d), F32),
        compiler_params=_cparams(("arbitrary",)),
        name="moe_combine",
    )(dest0, dest1, y_pad, rec, x1, mod, ln_g, ln_b)


def _rel_bucket(dist):
    n = jnp.maximum(dist, 0)
    max_exact = REL_BUCKETS // 2
    nf = jnp.maximum(n, 1).astype(F32)
    large = max_exact + (jnp.log(nf / max_exact) / math.log(REL_MAX_DIST / max_exact)
                         * (REL_BUCKETS - max_exact)).astype(jnp.int32)
    large = jnp.minimum(large, REL_BUCKETS - 1)
    return jnp.where(n < max_exact, n, large)


def _bias_tables(rel_bias, seq):
    bucket_onehot = (_rel_bucket(jnp.arange(LANE))[:, None] == jnp.arange(REL_BUCKETS)).astype(F32)
    tab_d = jnp.einsum('db,hb->hd', bucket_onehot, rel_bias,
                       precision=lax.Precision.HIGHEST)
    tok = np.arange(QT)[None, :]
    key = np.arange(LANE)[:, None]
    far = tab_d[:, LANE - 1]
    cols = NSA_REP * QT

    def transposed(dist):
        idx = jnp.asarray(np.clip(dist, 0, LANE - 1).astype(np.int32))
        onehot = (idx[..., None] == jnp.arange(LANE, dtype=jnp.int32)).astype(F32)
        t = jnp.einsum('ijd,hd->hij', onehot, tab_d, precision=lax.Precision.HIGHEST)
        t = (t - far[:, None, None]) * LOG2E
        t = t.reshape(NSA_GROUPS, NSA_REP, LANE, QT).transpose(0, 2, 1, 3)
        return t.reshape(NSA_GROUPS, LANE, cols)

    t0t = transposed(tok - key)
    t1t = transposed(tok - key + QT)

    ns = seq // CMP_STRIDE
    d_c = tok - CMP_STRIDE * key + (CMP_STRIDE * CMP_PAD - (CMP_BLOCK - 1))
    seen = np.tile(d_c >= 0, (1, NSA_REP))
    recent = jnp.where(seen[None], transposed(d_c), NEG)
    tt = jnp.concatenate([jnp.zeros((NSA_GROUPS, ns, cols), F32), recent,
                          jnp.full((NSA_GROUPS, ns, cols), NEG, F32)], axis=1)

    rho = np.arange(WINDOW + QT)[:, None]
    tok_w = np.tile(np.arange(QT), NSA_REP)[None, :]
    band = (rho > tok_w) & (rho <= tok_w + WINDOW)
    rows = jnp.concatenate([jnp.zeros((NSA_GROUPS, WINDOW - QT, NSA_REP * QT), F32), t1t, t0t], axis=1)
    wb = jnp.where(band[None], rows, NEG)

    zeros = lambda n: jnp.zeros((NSA_GROUPS, n * QT, cols), F32)
    negs = lambda n: jnp.full((NSA_GROUPS, n * QT, cols), NEG, F32)
    diag = jnp.where(np.tile(tok >= key, (1, NSA_REP))[None], t0t, NEG)
    nd = jnp.stack([jnp.concatenate(([zeros(u - 1), t1t] if u else []) + [diag, negs(SEL_SUB - 1 - u)], axis=1)
                    for u in range(SEL_SUB)], axis=1)
    npv = jnp.concatenate([zeros(SEL_SUB - 1), t1t], axis=1)
    return nd, npv, wb, tt


def _overlap_matrix(seq):
    ns = seq // CMP_STRIDE
    nslc = seq // SLC_BLOCK
    ov = np.zeros((LANE, ns), np.float32)
    cs = np.arange(ns - 1) * CMP_STRIDE
    ss = np.arange(nslc) * SLC_BLOCK
    ov[:nslc, :ns - 1] = ((cs[None, :] < ss[:, None] + SLC_BLOCK) & (cs[None, :] + CMP_BLOCK > ss[:, None]))
    return jnp.asarray(ov, BF16)


def _reorder_cols(a):
    lead = a.shape[:-1]
    gate = a[..., MAIN_COLS:MAIN_COLS + GATE_COLS]
    per = GATE_COLS // NSA_GROUPS
    gate_blocks = []
    for g in range(NSA_GROUPS):
        gate_blocks.append(gate[..., g * per:(g + 1) * per])
        gate_blocks.append(jnp.zeros(lead + (LANE - per,), a.dtype))
    pad = jnp.zeros(lead + ((CB_MGH - CB_GATE - NSA_GROUPS) * LANE,), a.dtype)
    return jnp.concatenate([a[..., :MAIN_COLS]] + gate_blocks + [pad, a[..., MAIN_COLS + GATE_COLS:]], axis=-1)


def kernel(x, c, ada_w, ada_b, w_in, b_in, hg_lb_logits, hg_norm_w, cmp_pos_k, cmp_w1_k, cmp_b1_k, cmp_w2_k, cmp_pos_v, cmp_w1_v, cmp_b1_v, cmp_w2_v, rel_bias, w_br_hg, w_br_nsa, w_out, ln1_g, ln1_b, router_grp_w, router_grp_b, router_exp_w, router_exp_b, exp_w1, exp_w3, exp_w2, ln2_g, ln2_b):
    batch, seq, d = x.shape
    nt = batch * seq
    assert d == D_MODEL and seq % 1024 == 0 and seq // SLC_BLOCK <= LANE
    l = 0
    x2 = x.reshape(nt, d)

    c_pad = jnp.zeros((8, d), F32).at[:batch].set(c)
    mod = _adaln(c_pad, ada_w[l], ada_b[l][None])[:batch].reshape(batch, 6, d)

    proj = _inproj(x2, mod, _reorder_cols(w_in[l]).astype(BF16), _reorder_cols(b_in[l])[None], seq)

    lb_all = jnp.cumsum(jax.nn.softmax(hg_lb_logits.astype(F32), axis=0), axis=0)
    o_h = _hgrn(proj, lb_all[l][None], hg_norm_w[l][None], batch, seq)

    kvc = _compress(proj, jnp.stack([cmp_pos_k[l], cmp_pos_v[l]]),
                    jnp.stack([cmp_w1_k[l], cmp_w1_v[l]]).astype(BF16),
                    jnp.stack([cmp_b1_k[l], cmp_b1_v[l]])[:, None, :],
                    jnp.stack([cmp_w2_k[l], cmp_w2_v[l]]).astype(BF16), batch, seq)

    nd, npv, wb, tt = _bias_tables(rel_bias, seq)
    o_c, sel_t = _cmp_sel(proj, kvc, tt, _overlap_matrix(seq), batch, seq)
    o_w = _window(proj, wb, batch, seq)
    o_n = _selected(proj, sel_t, nd, npv, o_c, o_w, batch, seq)

    w_r = jnp.zeros((d, LANE), F32).at[:, :N_GROUPS].set(router_grp_w[l])
    w_r = w_r.at[:, N_GROUPS:N_GROUPS + N_EXPERTS].set(router_exp_w[l])
    b_r = jnp.zeros((1, LANE), F32).at[0, :N_GROUPS].set(router_grp_b[l])
    b_r = b_r.at[0, N_GROUPS:N_GROUPS + N_EXPERTS].set(router_exp_b[l])
    x1, h2, logits = _merge(o_h, o_n, proj, x2, mod,
                            w_br_hg[l].astype(BF16), w_br_nsa[l].astype(BF16), w_out[l].astype(BF16),
                            ln1_g[l][None], ln1_b[l][None], w_r, b_r, seq)

    rec, cnt, rec_t = _route(logits)
    counts = cnt[0, N_GROUPS:N_GROUPS + N_EXPERTS].astype(jnp.int32)
    padded = (counts + MOE_ROWS - 1) // MOE_ROWS * MOE_ROWS
    pend = jnp.cumsum(padded)
    pstart = pend - padded
    n_assign = 2 * nt
    nb = n_assign // MOE_ROWS + N_EXPERTS
    slots = rec_t.astype(jnp.int32)
    expert_ids = jnp.arange(N_EXPERTS, dtype=jnp.int32)[:, None]
    slot_base = lambda e: jnp.sum(jnp.where(e[None, :] == expert_ids, pstart[:, None], 0), axis=0)
    dest0 = slot_base(slots[0]) + slots[4]
    dest1 = slot_base(slots[1]) + slots[5]
    block_start = jnp.arange(nb, dtype=jnp.int32) * MOE_ROWS
    block_expert = jnp.minimum(jnp.sum(pend[None, :] <= block_start[:, None], axis=1),
                               N_EXPERTS - 1).astype(jnp.int32)
    n_used = pend[-1] // MOE_ROWS
    spare = n_used + jnp.arange(N_EXPERTS, dtype=jnp.int32)
    zero_blocks = jnp.concatenate([jnp.where(padded > 0, pend // MOE_ROWS - 1, -1),
                                   jnp.where(spare < nb, spare, -1)]).astype(jnp.int32)

    x_pad = _dispatch(dest0, dest1, zero_blocks, h2, nb)
    y_pad = _experts(block_expert, n_used[None].astype(jnp.int32), x_pad, exp_w1[l], exp_w3[l], exp_w2[l])
    out = _combine(dest0, dest1, y_pad, rec, x1, mod, ln2_g[l][None], ln2_b[l][None], seq)
    return out.reshape(batch, seq, d)
```

```python
import functools
import math

import numpy as np
import jax
import jax.numpy as jnp
from jax import lax
from jax.experimental import pallas as pl
from jax.experimental.pallas import tpu as pltpu

F32 = jnp.float32
BF16 = jnp.bfloat16

D_MODEL = 1024
HG_HEADS = 8
HG_DK = 128
HG_DV = 128
HG_CHUNK = 32
HG_SUPER = 256
NSA_HEADS = 8
NSA_GROUPS = 2
NSA_REP = NSA_HEADS // NSA_GROUPS
NSA_DK = 128
CMP_BLOCK = 32
CMP_STRIDE = 16
SLC_BLOCK = 64
SLC_TOPK = 16
WINDOW = 512
FORCE_SCORE = 1e4
N_FORCED = 3
REL_BUCKETS = 32
REL_MAX_DIST = 128
N_GROUPS = 4
EXP_PER_GROUP = 8
N_EXPERTS = N_GROUPS * EXP_PER_GROUP
D_EXPERT = D_MODEL // 2
DEPTH = 1
ALPHA = (2 * DEPTH) ** 0.25

LANE = 128
QT = 128
NEG = -1e30
SCALE = NSA_DK ** -0.5
LOG2E = math.log2(math.e)
SCALE_LOG2 = SCALE * LOG2E
KEY_BLK = 512
BOUND_SLACK = 1.0 + 2.0 ** -10
MAX_REF_GAP = 64.0
SEL_ROWS = 16
SEL_SUB = KEY_BLK // QT
WIN_SUB = 4
CMP_SUB = 4
CMP_PAD = 120
VMEM_LIMIT = 56 * 1024 * 1024

CB_HQ, CB_HF, CB_HI, CB_HG = 0, 8, 16, 24
CB_NQ = 32
CB_KC, CB_VC, CB_KS, CB_VS, CB_KW, CB_VW = 40, 42, 44, 46, 48, 50
CB_GATE = 52
CB_MGH, CB_MGN = 56, 64
NCB = 72
MAIN_COLS = 52 * LANE
GATE_COLS = 3 * NSA_HEADS

MERGE_SPLIT = 2
ROW_TILES = D_MODEL // LANE
MOE_ROWS = 512
ROW_DMA_UNROLL = 8


def _cparams(sem):
    return pltpu.CompilerParams(dimension_semantics=sem, vmem_limit_bytes=VMEM_LIMIT)


def _dot(a, b):
    return jnp.dot(a, b, preferred_element_type=F32)


def _dot_nt(a, b):
    return lax.dot_general(a, b, (((1,), (1,)), ((), ())), preferred_element_type=F32)


def _dot_tn(a, b):
    return lax.dot_general(a, b, (((0,), (0,)), ((), ())), preferred_element_type=F32)


def _split3(x):
    hi = x.astype(BF16)
    r = x - hi.astype(F32)
    mid = r.astype(BF16)
    lo = (r - mid.astype(F32)).astype(BF16)
    return hi, mid, lo


def _dot01(m01, x):
    hi, mid, lo = _split3(x)
    return _dot(m01, hi) + _dot(m01, mid) + _dot(m01, lo)


def _dot01_r(x, m01):
    hi, mid, lo = _split3(x)
    return _dot(hi, m01) + _dot(mid, m01) + _dot(lo, m01)


def _layer_norm(x, eps=1e-5):
    mu = jnp.mean(x, axis=-1, keepdims=True)
    xc = x - mu
    var = jnp.mean(xc * xc, axis=-1, keepdims=True)
    return xc * lax.rsqrt(var + eps)


def _adaln_kernel(c_ref, w_ref, b_ref, o_ref):
    c = c_ref[...]
    ca = c * jax.nn.sigmoid(c)
    o_ref[...] = jnp.dot(ca, w_ref[...], precision=lax.Precision.HIGHEST,
                         preferred_element_type=F32) + b_ref[...]


def _adaln(c_pad, w, b):
    rows, d = c_pad.shape
    n = w.shape[1]
    return pl.pallas_call(
        _adaln_kernel,
        grid=(n // d,),
        in_specs=[pl.BlockSpec((rows, d), lambda j: (0, 0)),
                  pl.BlockSpec((d, d), lambda j: (0, j)),
                  pl.BlockSpec((1, d), lambda j: (0, j))],
        out_specs=pl.BlockSpec((rows, d), lambda j: (0, j)),
        out_shape=jax.ShapeDtypeStruct((rows, n), F32),
        compiler_params=_cparams(("arbitrary",)),
        name="adaln",
    )(c_pad, w, b)


def _inproj_kernel(x_ref, mod_ref, w_ref, b_ref, o_ref, hn_ref, *, ncb_tile):
    @pl.when(pl.program_id(1) == 0)
    def _():
        hn = _layer_norm(x_ref[...])
        sh = mod_ref[0, 0:1, :]
        sc = mod_ref[0, 1:2, :]
        hn_ref[...] = (hn * (1.0 + sc) + sh).astype(BF16)

    res = _dot(hn_ref[...], w_ref[...]) + b_ref[...]
    for c in range(ncb_tile):
        o_ref[c] = res[:, c * LANE:(c + 1) * LANE]


def _inproj(x2, mod, w, b, seq):
    nt, d = x2.shape
    tm = min(2048, seq)
    ncb_tile = 8
    tn = ncb_tile * LANE
    return pl.pallas_call(
        functools.partial(_inproj_kernel, ncb_tile=ncb_tile),
        grid=(nt // tm, NCB // ncb_tile),
        in_specs=[pl.BlockSpec((tm, d), lambda i, j: (i, 0)),
                  pl.BlockSpec((1, 6, d), lambda i, j: (i * tm // seq, 0, 0)),
                  pl.BlockSpec((d, tn), lambda i, j: (0, j)),
                  pl.BlockSpec((1, tn), lambda i, j: (0, j))],
        out_specs=pl.BlockSpec((ncb_tile, tm, LANE), lambda i, j: (j, i, 0)),
        out_shape=jax.ShapeDtypeStruct((NCB, nt, LANE), F32),
        scratch_shapes=[pltpu.VMEM((tm, d), BF16)],
        compiler_params=_cparams(("arbitrary", "arbitrary")),
        name="inproj",
    )(x2, mod, w, b)


def _hgrn_kernel(q_ref, f_ref, v_ref, g_ref, lb_ref, nw_ref, o_ref, st_ref, *, rows):
    @pl.when(pl.program_id(1) == 0)
    def _():
        st_ref[...] = jnp.zeros_like(st_ref)

    sup = HG_SUPER
    ri = lax.broadcasted_iota(jnp.int32, (sup, sup), 0)
    ci = lax.broadcasted_iota(jnp.int32, (sup, sup), 1)
    same = (ri // HG_CHUNK) == (ci // HG_CHUNK)
    cum_m = jnp.where(same & (ci <= ri), 1.0, 0.0).astype(BF16)
    rt = lax.broadcasted_iota(jnp.int32, (LANE, LANE), 0)
    ct = lax.broadcasted_iota(jnp.int32, (LANE, LANE), 1)
    tril = ((rt // HG_CHUNK) == (ct // HG_CHUNK)) & (ct <= rt)
    per = sup // HG_CHUNK
    groups = [slice(g * LANE, (g + 1) * LANE) for g in range(sup // LANE)]

    heads = range(HG_HEADS)
    hs = [slice(h * LANE, (h + 1) * LANE) for h in heads]

    def wide(ref, r0):
        return jnp.concatenate([ref[h, pl.ds(r0, sup), :] for h in heads], axis=1)

    def body(i, carry):
        r0 = pl.multiple_of(i * sup, sup)
        lb = lb_ref[...]
        f = lb + (1.0 - lb) * jax.nn.sigmoid(wide(f_ref, r0))
        lf = jnp.log(f)
        k = 1.0 - f
        hi = lf.astype(BF16)
        lo = (lf - hi.astype(F32)).astype(BF16)
        b = _dot(cum_m, hi) + _dot(cum_m, lo)
        chunk = [slice(c * HG_CHUNK, (c + 1) * HG_CHUNK) for c in range(per)]
        dec = [jnp.exp(b[(c + 1) * HG_CHUNK - 1:(c + 1) * HG_CHUNK, :]) for c in range(per)]
        dec_rows = jnp.concatenate([jnp.broadcast_to(dec[c], (HG_CHUNK, dec[c].shape[1])) for c in range(per)],
                                   axis=0)
        q_in = (wide(q_ref, r0) * jnp.exp(b)).astype(BF16)
        k_dec = k * jnp.exp(-b)
        k_in = k_dec.astype(BF16)
        k_end = (k_dec * dec_rows).astype(BF16)
        vb = wide(v_ref, r0).astype(BF16)

        att = [[jnp.where(tril, _dot_nt(q_in[g, hs[h]], k_in[g, hs[h]]), 0.0).astype(BF16) for g in groups]
               for h in heads]
        upd = [[_dot_tn(vb[chunk[c], hs[h]], k_end[chunk[c], hs[h]]) for c in range(per)] for h in heads]
        intra = [[_dot(att[h][n], vb[g, hs[h]]) for n, g in enumerate(groups)] for h in heads]

        st = [st_ref[h] for h in heads]
        inter = [[] for _ in heads]
        for c in range(per):
            for h in heads:
                inter[h].append(_dot_nt(q_in[chunk[c], hs[h]], st[h].astype(BF16)))
                st[h] = dec[c][:, hs[h]] * st[h] + upd[h][c]
        for h in heads:
            st_ref[h] = st[h]

        o = [jnp.concatenate(intra[h], axis=0) + jnp.concatenate(inter[h], axis=0) for h in heads]
        scale = [lax.rsqrt(jnp.mean(o[h] * o[h], axis=-1, keepdims=True) + 1e-6) for h in heads]
        on = jnp.concatenate([o[h] * scale[h] for h in heads], axis=1)
        on = on * nw_ref[...] * jax.nn.sigmoid(wide(g_ref, r0))
        o_ref[pl.ds(r0, sup), :] = on.astype(BF16)
        return carry

    lax.fori_loop(0, rows // sup, body, 0)


def _hgrn(proj, lb, nw, batch, seq):
    nt = proj.shape[1]
    tb = min(512, seq)
    nblk = seq // tb

    def slab(cb0):
        return pl.BlockSpec((HG_HEADS, tb, LANE), lambda b, t: (cb0 // HG_HEADS, b * nblk + t, 0))

    vec = pl.BlockSpec((1, HG_HEADS * LANE), lambda b, t: (0, 0))
    return pl.pallas_call(
        functools.partial(_hgrn_kernel, rows=tb),
        grid=(batch, nblk),
        in_specs=[slab(CB_HQ), slab(CB_HF), slab(CB_HI), slab(CB_HG), vec, vec],
        out_specs=pl.BlockSpec((tb, HG_HEADS * HG_DV), lambda b, t: (b * nblk + t, 0)),
        out_shape=jax.ShapeDtypeStruct((nt, HG_HEADS * HG_DV), BF16),
        scratch_shapes=[pltpu.VMEM((HG_HEADS, HG_DV, HG_DK), F32)],
        compiler_params=_cparams(("arbitrary", "arbitrary")),
        name="hgrn2",
    )(proj, proj, proj, proj, lb, nw)


def _compress_kernel(x_ref, pos_ref, w1_ref, b1_ref, w2_ref, o_ref, *, ns):
    p0 = jnp.zeros((ns, LANE), F32)
    p1 = jnp.zeros((ns, LANE), F32)
    for j in range(CMP_STRIDE):
        tok = x_ref[0, pl.ds(j, ns, stride=CMP_STRIDE), :]
        rows = slice(j * LANE, (j + 1) * LANE)
        late = slice((CMP_STRIDE + j) * LANE, (CMP_STRIDE + j + 1) * LANE)
        p0 = p0 + _dot((tok + pos_ref[0, j:j + 1, :]).astype(BF16), w1_ref[0, rows, :])
        p1 = p1 + _dot((tok + pos_ref[0, CMP_STRIDE + j:CMP_STRIDE + j + 1, :]).astype(BF16), w1_ref[0, late, :])
    h = p0 + pltpu.roll(p1, ns - 1, axis=0) + b1_ref[0]
    a = h * jax.nn.sigmoid(h)
    out = _dot(a.astype(BF16), w2_ref[0])
    row = lax.broadcasted_iota(jnp.int32, out.shape, 0)
    out = jnp.where(row < ns - 1, out, 0.0)
    o_ref[0, 0, 0] = out


def _compress(proj, pos, w1, b1, w2, batch, seq):
    ns = seq // CMP_STRIDE
    np_rows = ns
    width = CMP_STRIDE * LANE
    return pl.pallas_call(
        functools.partial(_compress_kernel, ns=ns),
        grid=(2, batch, NSA_GROUPS),
        in_specs=[pl.BlockSpec((1, seq, LANE), lambda s, b, g: (CB_KC + NSA_GROUPS * s + g, b, 0)),
                  pl.BlockSpec((1, CMP_BLOCK, LANE), lambda s, b, g: (s, 0, 0)),
                  pl.BlockSpec((1, 2 * width, LANE), lambda s, b, g: (s, 0, 0)),
                  pl.BlockSpec((1, 1, LANE), lambda s, b, g: (s, 0, 0)),
                  pl.BlockSpec((1, LANE, LANE), lambda s, b, g: (s, 0, 0))],
        out_specs=pl.BlockSpec((1, 1, 1, np_rows, LANE), lambda s, b, g: (s, b, g, 0, 0)),
        out_shape=jax.ShapeDtypeStruct((2, batch, NSA_GROUPS, np_rows, LANE), F32),
        compiler_params=_cparams(("arbitrary", "arbitrary", "arbitrary")),
        name="nsa_compress",
    )(proj, pos, w1, b1, w2)


def _cmp_sel_kernel(q0, q1, q2, q3, kc_ref, vc_ref, gate_ref, tt_ref, ov_ref, oc_ref, sel_ref,
                    kb_ref, vt_ref, imp_ref, *, np_rows):
    step = pl.program_id(2)
    cols = NSA_REP * QT
    ns = np_rows

    @pl.when(step == 0)
    def _():
        kb_ref[...] = (kc_ref[0, 0, 0] * SCALE_LOG2).astype(BF16)
        for c in range(np_rows // LANE):
            vt_ref[:, c * LANE:(c + 1) * LANE] = vc_ref[0, 0, 0, c * LANE:(c + 1) * LANE, :].T.astype(BF16)

    subs = range(CMP_SUB)
    tis = [step * CMP_SUB + u for u in subs]
    qts = [_q_transposed((q0, q1, q2, q3), u) for u in subs]

    def group_sum(p):
        tot = p[:, 0:QT]
        for r in range(1, NSA_REP):
            tot = tot + p[:, r * QT:(r + 1) * QT]
        return tot

    def attend(nrows):
        kb = kb_ref[0:nrows, :]
        bias = [tt_ref[0, pl.ds(pl.multiple_of(ns + CMP_PAD - tis[u] * (QT // CMP_STRIDE), 8), nrows), :]
                for u in subs]
        ss = [_dot(kb, qts[u]) + bias[u] for u in subs]
        ms = [jnp.max(ss[u], axis=0, keepdims=True) for u in subs]
        ps = [jnp.exp2(ss[u] - ms[u]) for u in subs]
        ls = [jnp.sum(ps[u], axis=0, keepdims=True) for u in subs]
        invs = [jnp.where(ms[u] > 0.5 * NEG, 1.0 / ls[u], 0.0) for u in subs]
        pn = [ps[u] * invs[u] for u in subs]
        vt = vt_ref[:, 0:nrows]
        os_ = [_dot(vt, pn[u].astype(BF16)) for u in subs]
        for u in subs:
            rows = slice(u * QT, (u + 1) * QT)
            gt = jax.nn.sigmoid(gate_ref[0, rows, :])
            for r in range(NSA_REP):
                oc_ref[rows, r * LANE:(r + 1) * LANE] = gt[:, 3 * r:3 * r + 1] * os_[u][:, r * QT:(r + 1) * QT].T
            imp_ref[:, rows] = _dot01(ov_ref[:, 0:nrows], group_sum(pn[u]))

    visible = (step + 1) * CMP_SUB * (QT // CMP_STRIDE)
    ngroups = np_rows // LANE
    for g in range(1, ngroups + 1):
        upper = visible <= g * LANE if g < ngroups else True

        @pl.when((visible > (g - 1) * LANE) & upper)
        def _():
            attend(g * LANE)

    imp = imp_ref[...]
    width = CMP_SUB * QT
    jj = lax.broadcasted_iota(jnp.int32, (LANE, width), 0)
    tok = step * width + lax.broadcasted_iota(jnp.int32, (LANE, width), 1)
    cur = tok // SLC_BLOCK
    forced = (jj == 0) | (jj == cur) | (jj == cur - 1)
    assert FORCE_SCORE > NSA_REP
    score = jnp.where(forced, -jnp.inf, jnp.where(jj <= cur, imp, -1.0))
    selb = jnp.where(forced, 0.0, NEG)
    jf = jj.astype(F32)
    for _ in range(SLC_TOPK - N_FORCED):
        mval = jnp.max(score, axis=0, keepdims=True)
        first = jnp.min(jnp.where(score == mval, jf, float(LANE)), axis=0, keepdims=True)
        pick = jf == first
        selb = jnp.where(pick, 0.0, selb)
        score = jnp.where(pick, -jnp.inf, score)
    sel_ref[0, 0] = selb


def _q_specs(nsteps, rows=QT):
    return [pl.BlockSpec((1, rows, LANE),
                         functools.partial(lambda b, g, t, r: (CB_NQ + NSA_REP * g + r, b * nsteps + t, 0), r=r))
            for r in range(NSA_REP)]


def _cmp_sel(proj, kvc, tt, ov_t, batch, seq):
    nt = proj.shape[1]
    rows = CMP_SUB * QT
    nsteps = seq // rows
    cols = NSA_REP * QT
    np_rows = kvc.shape[3]
    kv_spec = lambda s: pl.BlockSpec((1, 1, 1, np_rows, LANE), lambda b, g, t: (s, b, g, 0, 0))
    return pl.pallas_call(
        functools.partial(_cmp_sel_kernel, np_rows=np_rows),
        grid=(batch, NSA_GROUPS, nsteps),
        in_specs=_q_specs(nsteps, rows) + [
            kv_spec(0), kv_spec(1),
            pl.BlockSpec((1, rows, LANE), lambda b, g, t: (CB_GATE + g, b * nsteps + t, 0)),
            pl.BlockSpec((1, tt.shape[1], cols), lambda b, g, t: (g, 0, 0)),
            pl.BlockSpec((LANE, np_rows), lambda b, g, t: (0, 0))],
        out_specs=[pl.BlockSpec((rows, cols), lambda b, g, t: (b * nsteps + t, g)),
                   pl.BlockSpec((1, 1, LANE, rows), lambda b, g, t: (b, g, 0, t))],
        out_shape=[jax.ShapeDtypeStruct((nt, NSA_HEADS * LANE), F32),
                   jax.ShapeDtypeStruct((batch, NSA_GROUPS, LANE, seq), F32)],
        scratch_shapes=[pltpu.VMEM((np_rows, LANE), BF16), pltpu.VMEM((LANE, np_rows), BF16),
                        pltpu.VMEM((LANE, rows), F32)],
        compiler_params=_cparams(("arbitrary", "arbitrary", "arbitrary")),
        name="nsa_cmp_select",
    )(proj, proj, proj, proj, kvc, kvc, proj, tt, ov_t)


def _q_transposed(q_refs, sub=0):
    return jnp.concatenate([r[0, sub * QT:(sub + 1) * QT, :].T for r in q_refs], axis=1).astype(BF16)


def _softmax_step(s, vt, state):
    m, l, acc = state
    m_new = jnp.maximum(m, jnp.max(s, axis=0, keepdims=True))
    alpha = jnp.exp2(m - m_new)
    p = jnp.exp2(s - m_new)
    l = alpha * l + jnp.sum(p, axis=0, keepdims=True)
    acc = alpha * acc + _dot(vt, p.astype(BF16))
    return m_new, l, acc


def _finish_t(state, gate_ref, branch, o_ref, sub=0, others=()):
    _, l, acc = state
    o = acc / jnp.where(l == 0.0, 1.0, l)
    rows = slice(sub * QT, (sub + 1) * QT)
    gt = jax.nn.sigmoid(gate_ref[0, rows, :])
    for r in range(NSA_REP):
        col = 3 * r + branch
        lanes = slice(r * LANE, (r + 1) * LANE)
        val = gt[:, col:col + 1] * o[:, r * QT:(r + 1) * QT].T
        for other in others:
            val = val + other[rows, lanes]
        o_ref[rows, lanes] = val.astype(o_ref.dtype)


def _init_state(cols):
    return (jnp.full((1, cols), NEG, F32), jnp.zeros((1, cols), F32), jnp.zeros((LANE, cols), F32))


def _win_kernel(q0, q1, q2, q3, k_ref, v_ref, gate_ref, wb_ref, o_ref, kb_ref, vt_ref, *, seq):
    step = pl.program_id(2)
    cols = NSA_REP * QT
    wt = WINDOW // QT

    lane = lax.broadcasted_iota(jnp.int32, (QT, LANE), 1)
    pad_mark = jnp.where(lane == 0, 1.0, 0.0).astype(BF16)

    @pl.when(step == 0)
    def _():
        for i in range(wt):
            kb_ref[i * QT:(i + 1) * QT, 0:LANE] = jnp.zeros((QT, LANE), BF16)
            kb_ref[i * QT:(i + 1) * QT, LANE:2 * LANE] = pad_mark
            vt_ref[i] = jnp.zeros((LANE, QT), BF16)

        def fill(i, carry):
            r0 = pl.multiple_of(i * QT, QT)
            kb_ref[pl.ds(WINDOW + r0, QT), 0:LANE] = (k_ref[0, pl.ds(r0, QT), :] * SCALE_LOG2).astype(BF16)
            kb_ref[pl.ds(WINDOW + r0, QT), LANE:2 * LANE] = jnp.zeros((QT, LANE), BF16)
            vt_ref[wt + i] = v_ref[0, pl.ds(r0, QT), :].T.astype(BF16)
            return carry

        lax.fori_loop(0, seq // QT, fill, 0)

    subs = range(WIN_SUB)
    tis = [step * WIN_SUB + sub for sub in subs]
    row = lax.broadcasted_iota(jnp.int32, (LANE, cols), 0)
    pad_rows = jnp.where(row == 0, NEG, 0.0).astype(BF16)
    qts = [jnp.concatenate([_q_transposed((q0, q1, q2, q3), sub), pad_rows], axis=0) for sub in subs]
    ss = [_dot(kb_ref[pl.ds(pl.multiple_of(tis[u] * QT, QT), WINDOW + QT), :], qts[u]) + wb_ref[0]
          for u in subs]
    ms = [jnp.max(ss[u], axis=0, keepdims=True) for u in subs]
    ps = [jnp.exp2(ss[u] - ms[u]) for u in subs]
    ls = [jnp.sum(ps[u], axis=0, keepdims=True) for u in subs]
    pbs = [ps[u].astype(BF16) for u in subs]
    accs = [_dot(vt_ref[tis[u]], pbs[u][0:QT]) for u in subs]
    for d in range(1, wt + 1):
        accs = [accs[u] + _dot(vt_ref[tis[u] + d], pbs[u][d * QT:(d + 1) * QT]) for u in subs]
    for u in subs:
        _finish_t((ms[u], ls[u], accs[u]), gate_ref, 2, o_ref, u)


def _window(proj, wb, batch, seq):
    nt = proj.shape[1]
    rows = WIN_SUB * QT
    nqt = seq // rows
    cols = NSA_REP * QT
    slab = lambda cb0: pl.BlockSpec((1, seq, LANE), lambda b, g, t: (cb0 + g, b, 0))
    return pl.pallas_call(
        functools.partial(_win_kernel, seq=seq),
        grid=(batch, NSA_GROUPS, nqt),
        in_specs=_q_specs(nqt, rows) + [
            slab(CB_KW), slab(CB_VW),
            pl.BlockSpec((1, rows, LANE), lambda b, g, t: (CB_GATE + g, b * nqt + t, 0)),
            pl.BlockSpec((1, WINDOW + QT, cols), lambda b, g, t: (g, 0, 0))],
        out_specs=pl.BlockSpec((rows, cols), lambda b, g, t: (b * nqt + t, g)),
        out_shape=jax.ShapeDtypeStruct((nt, NSA_HEADS * LANE), F32),
        scratch_shapes=[pltpu.VMEM((seq + WINDOW, 2 * LANE), BF16),
                        pltpu.VMEM((seq // QT + WINDOW // QT, LANE, QT), BF16)],
        compiler_params=_cparams(("arbitrary", "arbitrary", "arbitrary")),
        name="nsa_window",
    )(proj, proj, proj, proj, proj, proj, proj, wb)


def _softmax_steps(ss, vt, states):
    n = range(len(ss))
    m_new = [jnp.maximum(states[u][0], jnp.max(ss[u], axis=0, keepdims=True)) for u in n]
    alpha = [jnp.exp2(states[u][0] - m_new[u]) for u in n]
    p = [jnp.exp2(ss[u] - m_new[u]) for u in n]
    l = [alpha[u] * states[u][1] + jnp.sum(p[u], axis=0, keepdims=True) for u in n]
    vts = vt if isinstance(vt, (list, tuple)) else [vt] * len(ss)
    pv = [_dot(vts[u], p[u].astype(BF16)) for u in n]
    return [(m_new[u], l[u], alpha[u] * states[u][2] + pv[u]) for u in n]


def _sel_kernel(q0, q1, q2, q3, k_ref, v_ref, sel_ref, gate_ref, nd_ref, npv_ref, oc_ref, ow_ref, o_ref,
                kb_ref, vt_ref, qa_ref, kn_ref, m_ref, l_ref, acc_ref, *, seq):
    step = pl.program_id(2)
    cols = NSA_REP * QT
    spb = KEY_BLK // SLC_BLOCK
    subs = range(SEL_SUB)

    @pl.when(step == 0)
    def _():
        blk = lax.broadcasted_iota(jnp.int32, (KEY_BLK, LANE), 0) // SLC_BLOCK
        onehot = jnp.where(blk == lax.broadcasted_iota(jnp.int32, (KEY_BLK, LANE), 1), 1.0, 0.0).astype(BF16)

        def fill(i, kmax2):
            r0 = pl.multiple_of(i * KEY_BLK, KEY_BLK)
            kblk = (k_ref[0, pl.ds(r0, KEY_BLK), :] * SCALE_LOG2).astype(BF16)
            kb_ref[pl.ds(r0, KEY_BLK), 0:LANE] = kblk
            kb_ref[pl.ds(r0, KEY_BLK), LANE:2 * LANE] = onehot
            for c in range(SEL_SUB):
                vt_ref[i, :, c * QT:(c + 1) * QT] = v_ref[0, pl.ds(r0 + c * QT, QT), :].T.astype(BF16)
            n2 = jnp.sum(jnp.square(kblk.astype(F32)), axis=1, keepdims=True)
            return jnp.maximum(kmax2, jnp.max(n2, axis=0, keepdims=True))

        kmax2 = lax.fori_loop(0, seq // KEY_BLK, fill, jnp.zeros((1, 1), F32))
        kn_ref[...] = jnp.sqrt(kmax2)
        qa_ref[...] = jnp.zeros_like(qa_ref)

    qts = [_q_transposed((q0, q1, q2, q3), u) for u in subs]
    for u in subs:
        qa_ref[u, 0:LANE, :] = qts[u]
    bounds = [jnp.sqrt(jnp.sum(jnp.square(qts[u].astype(F32)), axis=0, keepdims=True)) * kn_ref[...]
              * BOUND_SLACK + 1.0 for u in subs]

    def block(kb):
        r0 = pl.multiple_of(kb * KEY_BLK, KEY_BLK)
        return kb_ref[pl.ds(r0, KEY_BLK), :], vt_ref[kb]

    pad_rows = jnp.zeros((SEL_ROWS - spb, cols), F32)

    def scores(kb, k, rows_of=lambda u: KEY_BLK):
        for u in subs:
            rows = sel_ref[0, 0, pl.ds(pl.multiple_of(kb * spb, spb), spb), u * QT:(u + 1) * QT]
            rows = jnp.concatenate([rows] * NSA_REP, axis=1)
            qa_ref[u, LANE:LANE + SEL_ROWS, :] = jnp.concatenate([rows, pad_rows], axis=0).astype(BF16)
        return [_dot(k[0:rows_of(u)], qa_ref[u]) for u in subs]

    seen = lambda u: (u + 1) * QT
    k, vt = block(step)
    ss = scores(step, k, seen)
    ss = [ss[u] + nd_ref[0, u, 0:seen(u), :] for u in subs]
    states = _softmax_steps(ss, [vt[:, 0:seen(u)] for u in subs], [_init_state(cols) for _ in subs])

    for u in subs:
        m_ref[u], l_ref[u], acc_ref[u] = states[u]

    def prev_scores():
        prev = jnp.maximum(step - 1, 0)
        k, vt = block(prev)
        ss = scores(prev, k)
        ss[0] = ss[0] + npv_ref[0]
        exists = (step + jnp.zeros((KEY_BLK, cols), jnp.int32)) >= 1
        return [jnp.where(exists, ss[u], NEG) for u in subs], vt

    n_old = jnp.maximum(step - 1, 0)
    bounds[0] = bounds[0] + jnp.max(npv_ref[0], axis=0, keepdims=True)
    m_fix = [jnp.maximum(states[u][0], bounds[u]) for u in subs]
    gap = jnp.max(jnp.concatenate([m_fix[u] - states[u][0] for u in subs], axis=1))
    fixed_ok = gap < MAX_REF_GAP

    @pl.when(fixed_ok)
    def _():
        for u in subs:
            alpha = jnp.exp2(m_ref[u] - m_fix[u])
            l_ref[u] = alpha * l_ref[u]
            acc_ref[u] = alpha * acc_ref[u]
            m_ref[u] = m_fix[u]

        def fixed_step(ss, vt):
            ps = [jnp.exp2(ss[u] - m_ref[u]) for u in subs]
            for u in subs:
                l_ref[u] = l_ref[u] + jnp.sum(ps[u], axis=0, keepdims=True)
            pv = [_dot(vt, ps[u].astype(BF16)) for u in subs]
            for u in subs:
                acc_ref[u] = acc_ref[u] + pv[u]

        fixed_step(*prev_scores())

        def body(kb, carry):
            k, vt = block(kb)
            fixed_step(scores(kb, k), vt)
            return carry

        lax.fori_loop(0, n_old, body, 0)

    @pl.when(jnp.logical_not(fixed_ok))
    def _():
        def online_step(ss, vt):
            st = _softmax_steps(ss, vt, [(m_ref[u], l_ref[u], acc_ref[u]) for u in subs])
            for u in subs:
                m_ref[u], l_ref[u], acc_ref[u] = st[u]

        online_step(*prev_scores())

        def body(kb, carry):
            k, vt = block(kb)
            online_step(scores(kb, k), vt)
            return carry

        lax.fori_loop(0, n_old, body, 0)

    for u in subs:
        _finish_t((m_ref[u], l_ref[u], acc_ref[u]), gate_ref, 1, o_ref, u, others=(oc_ref, ow_ref))


def _selected(proj, sel_t, nd, npv, o_c, o_w, batch, seq):
    nt = proj.shape[1]
    rows = SEL_SUB * QT
    nsteps = seq // rows
    cols = NSA_REP * QT
    slab = lambda cb0: pl.BlockSpec((1, seq, LANE), lambda b, g, t: (cb0 + g, b, 0))
    return pl.pallas_call(
        functools.partial(_sel_kernel, seq=seq),
        grid=(batch, NSA_GROUPS, nsteps),
        in_specs=_q_specs(nsteps, rows) + [
            slab(CB_KS), slab(CB_VS),
            pl.BlockSpec((1, 1, LANE, rows), lambda b, g, t: (b, g, 0, t)),
            pl.BlockSpec((1, rows, LANE), lambda b, g, t: (CB_GATE + g, b * nsteps + t, 0)),
            pl.BlockSpec((1, SEL_SUB, KEY_BLK, cols), lambda b, g, t: (g, 0, 0, 0)),
            pl.BlockSpec((1, KEY_BLK, cols), lambda b, g, t: (g, 0, 0)),
            pl.BlockSpec((rows, cols), lambda b, g, t: (b * nsteps + t, g)),
            pl.BlockSpec((rows, cols), lambda b, g, t: (b * nsteps + t, g))],
        out_specs=pl.BlockSpec((rows, cols), lambda b, g, t: (b * nsteps + t, g)),
        out_shape=jax.ShapeDtypeStruct((nt, NSA_HEADS * LANE), BF16),
        scratch_shapes=[pltpu.VMEM((seq, 2 * LANE), BF16),
                        pltpu.VMEM((seq // KEY_BLK, LANE, KEY_BLK), BF16),
                        pltpu.VMEM((SEL_SUB, 2 * LANE, cols), BF16),
                        pltpu.VMEM((1, 1), F32),
                        pltpu.VMEM((SEL_SUB, 1, cols), F32), pltpu.VMEM((SEL_SUB, 1, cols), F32),
                        pltpu.VMEM((SEL_SUB, LANE, cols), F32)],
        compiler_params=_cparams(("arbitrary", "arbitrary", "arbitrary")),
        name="nsa_selected",
    )(proj, proj, proj, proj, proj, proj, sel_t, proj, nd, npv, o_c, o_w)


def _merge_kernel(oh_ref, on_ref, mgh_ref, mgn_ref, x_ref, mod_ref,
                  wh_ref, wn_ref, wo_ref, g_ref, b_ref, wr_ref, br_ref,
                  x1_ref, h2_ref, lg_ref):
    nblk = D_MODEL // LANE
    tm = x_ref.shape[0]
    halves = [slice(s * tm // MERGE_SPLIT, (s + 1) * tm // MERGE_SPLIT) for s in range(MERGE_SPLIT)]
    a_h = [_dot(oh_ref[rs, :], wh_ref[...]) for rs in halves]
    a_n = [_dot(on_ref[rs, :], wn_ref[...]) for rs in halves]
    gh = [jnp.concatenate([mgh_ref[c, rs, :] for c in range(nblk)], axis=-1) for rs in halves]
    gn = [jnp.concatenate([mgn_ref[c, rs, :] for c in range(nblk)], axis=-1) for rs in halves]
    merged = [(jax.nn.sigmoid(gh[s]) * a_h[s] + jax.nn.sigmoid(gn[s]) * a_n[s]).astype(BF16)
              for s in range(MERGE_SPLIT)]
    y = [(1.0 + mod_ref[0, 2:3, :]) * _dot(merged[s], wo_ref[...]) for s in range(MERGE_SPLIT)]
    x1 = [_layer_norm(ALPHA * x_ref[rs, :] + y[s]) * g_ref[...] + b_ref[...] for s, rs in enumerate(halves)]
    h2 = [_layer_norm(x1[s]) * (1.0 + mod_ref[0, 4:5, :]) + mod_ref[0, 3:4, :] for s in range(MERGE_SPLIT)]
    lg = [jnp.dot(h2[s], wr_ref[...], precision=lax.Precision.HIGHEST, preferred_element_type=F32) + br_ref[...]
          for s in range(MERGE_SPLIT)]
    for s, rs in enumerate(halves):
        x1_ref[rs, :] = x1[s]
        _store_rows(h2_ref, h2[s], rs.start)
        lg_ref[rs, :] = lg[s]


def _merge(o_h, o_n, proj, x2, mod, w_h, w_n, w_o, ln_g, ln_b, w_r, b_r, seq):
    nt, d = x2.shape
    tm = min(512, seq)
    nblk = d // LANE
    row = lambda w: pl.BlockSpec((tm, w), lambda i: (i, 0))
    full = lambda a: pl.BlockSpec(a.shape, lambda i: (0,) * a.ndim)
    return pl.pallas_call(
        _merge_kernel,
        grid=(nt // tm,),
        in_specs=[row(d), row(d),
                  pl.BlockSpec((nblk, tm, LANE), lambda i: (CB_MGH // nblk, i, 0)),
                  pl.BlockSpec((nblk, tm, LANE), lambda i: (CB_MGN // nblk, i, 0)),
                  row(d),
                  pl.BlockSpec((1, 6, d), lambda i: (i * tm // seq, 0, 0)),
                  full(w_h), full(w_n), full(w_o), full(ln_g), full(ln_b), full(w_r), full(b_r)],
        out_specs=[row(d), pl.BlockSpec((tm * ROW_TILES, LANE), lambda i: (i, 0)), row(LANE)],
        out_shape=[jax.ShapeDtypeStruct((nt, d), F32),
                   jax.ShapeDtypeStruct((nt * ROW_TILES, LANE), F32),
                   jax.ShapeDtypeStruct((nt, LANE), F32)],
        compiler_params=_cparams(("arbitrary",)),
        name="merge_outproj",
    )(o_h, o_n, proj, proj, x2, mod, w_h, w_n, w_o, ln_g, ln_b, w_r, b_r)


def _route_kernel(lg_ref, rec_ref, cnt_ref, rect_ref, carry_ref, *, tm):
    @pl.when(pl.program_id(0) == 0)
    def _():
        carry_ref[...] = jnp.zeros_like(carry_ref)

    lg = lg_ref[...]
    lane = lax.broadcasted_iota(jnp.int32, (tm, LANE), 1).astype(F32)
    far = float(LANE)
    gmask = lane < N_GROUPS
    gl = jnp.where(gmask, lg, -jnp.inf)
    gmax = jnp.max(gl, axis=-1, keepdims=True)
    gsum = jnp.sum(jnp.where(gmask, jnp.exp(gl - gmax), 0.0), axis=-1, keepdims=True)
    grp_p = 1.0 / gsum
    gidx = jnp.min(jnp.where(gl == gmax, lane, far), axis=-1, keepdims=True)
    lo = N_GROUPS + EXP_PER_GROUP * gidx
    emask = (lane >= lo) & (lane < lo + EXP_PER_GROUP)
    el = jnp.where(emask, lg, -jnp.inf)
    m1 = jnp.max(el, axis=-1, keepdims=True)
    i1 = jnp.min(jnp.where(el == m1, lane, far), axis=-1, keepdims=True)
    el2 = jnp.where(lane == i1, -jnp.inf, el)
    m2 = jnp.max(el2, axis=-1, keepdims=True)
    i2 = jnp.min(jnp.where(emask & (lane != i1) & (el2 == m2), lane, far), axis=-1, keepdims=True)
    e = jnp.exp(m2 - m1)
    w0 = grp_p / (1.0 + e)
    w1 = grp_p * e / (1.0 + e)

    oh0 = lane == i1
    oh1 = lane == i2
    f0 = jnp.where(oh0, 1.0, 0.0)
    f1 = jnp.where(oh1, 1.0, 0.0)
    ri = lax.broadcasted_iota(jnp.int32, (tm, tm), 0)
    ci = lax.broadcasted_iota(jnp.int32, (tm, tm), 1)
    before = jnp.where(ci < ri, 1.0, 0.0).astype(BF16)
    cum0 = _dot(before, f0.astype(BF16))
    cum1 = _dot(before, f1.astype(BF16))
    tot0 = jnp.sum(f0, axis=0, keepdims=True)
    tot1 = jnp.sum(f1, axis=0, keepdims=True)
    carry = carry_ref[...]
    rank0 = jnp.sum(jnp.where(oh0, carry + cum0, 0.0), axis=-1, keepdims=True)
    rank1 = jnp.sum(jnp.where(oh1, carry + tot0 + cum1, 0.0), axis=-1, keepdims=True)
    carry = carry + tot0 + tot1
    carry_ref[...] = carry
    cnt_ref[...] = carry

    rec = jnp.where(lane == 0, i1 - N_GROUPS, 0.0)
    rec = jnp.where(lane == 1, i2 - N_GROUPS, rec)
    rec = jnp.where(lane == 2, w0, rec)
    rec = jnp.where(lane == 3, w1, rec)
    rec = jnp.where(lane == 4, rank0, rec)
    rec = jnp.where(lane == 5, rank1, rec)
    rec_ref[...] = rec
    rect_ref[...] = jnp.concatenate([rec[c * LANE:(c + 1) * LANE, :].T[0:8, :] for c in range(tm // LANE)], axis=1)


def _route(logits):
    nt = logits.shape[0]
    tm = min(512, nt)
    return pl.pallas_call(
        functools.partial(_route_kernel, tm=tm),
        grid=(nt // tm,),
        in_specs=[pl.BlockSpec((tm, LANE), lambda i: (i, 0))],
        out_specs=[pl.BlockSpec((tm, LANE), lambda i: (i, 0)),
                   pl.BlockSpec((1, LANE), lambda i: (0, 0)),
                   pl.BlockSpec((8, tm), lambda i: (0, i))],
        out_shape=[jax.ShapeDtypeStruct((nt, LANE), F32),
                   jax.ShapeDtypeStruct((1, LANE), F32),
                   jax.ShapeDtypeStruct((8, nt), F32)],
        scratch_shapes=[pltpu.VMEM((1, LANE), F32)],
        compiler_params=_cparams(("arbitrary",)),
        name="moe_route",
    )(logits)


def _row_copy(src, dst, sem):
    return pltpu.make_async_copy(src, dst, sem)


def _tile_of(r):
    return pl.ds(pl.multiple_of(r * ROW_TILES, ROW_TILES), ROW_TILES)


def _load_rows(ref, n, lead=()):
    return jnp.concatenate([ref[lead + (pl.ds(c, n, stride=ROW_TILES), slice(None))]
                            for c in range(ROW_TILES)], axis=1)


def _store_rows(ref, val, row0=0):
    n = val.shape[0]
    for c in range(ROW_TILES):
        ref[pl.ds(row0 * ROW_TILES + c, n, stride=ROW_TILES), :] = val[:, c * LANE:(c + 1) * LANE]


def _dispatch_kernel(dest0_ref, dest1_ref, zb_ref, h_ref, xp_ref, z_ref, sem, zsem, *, tm):
    step = pl.program_id(0)
    base = step * tm
    blk = MOE_ROWS * ROW_TILES

    @pl.when(step == 0)
    def _():
        z_ref[...] = jnp.zeros_like(z_ref)

        def zero_copy(j):
            b = jnp.maximum(zb_ref[j], 0)
            return _row_copy(z_ref, xp_ref.at[pl.ds(pl.multiple_of(b * blk, blk), blk), :], zsem)

        def start(j, carry):
            @pl.when(zb_ref[j] >= 0)
            def _():
                zero_copy(j).start()
            return carry

        def wait(j, carry):
            @pl.when(zb_ref[j] >= 0)
            def _():
                zero_copy(j).wait()
            return carry

        lax.fori_loop(0, 2 * N_EXPERTS, start, 0)
        lax.fori_loop(0, 2 * N_EXPERTS, wait, 0)

    def issue(r, carry):
        for dest_ref in (dest0_ref, dest1_ref):
            d = dest_ref[base + r]
            _row_copy(h_ref.at[_tile_of(r), :], xp_ref.at[_tile_of(d), :], sem).start()
        return carry

    lax.fori_loop(0, tm, issue, 0, unroll=ROW_DMA_UNROLL)
    for _ in range(2):
        _row_copy(h_ref, xp_ref.at[pl.ds(0, tm * ROW_TILES), :], sem).wait()


def _dispatch(dest0, dest1, zero_blocks, h2, n_blocks):
    nt = h2.shape[0] // ROW_TILES
    tm = min(256, nt)
    return pl.pallas_call(
        functools.partial(_dispatch_kernel, tm=tm),
        grid_spec=pltpu.PrefetchScalarGridSpec(
            num_scalar_prefetch=3,
            grid=(nt // tm,),
            in_specs=[pl.BlockSpec((tm * ROW_TILES, LANE), lambda i, *_: (i, 0))],
            out_specs=pl.BlockSpec(memory_space=pl.ANY),
            scratch_shapes=[pltpu.VMEM((MOE_ROWS * ROW_TILES, LANE), F32),
                            pltpu.SemaphoreType.DMA(()), pltpu.SemaphoreType.DMA(())]),
        out_shape=jax.ShapeDtypeStruct((n_blocks * MOE_ROWS * ROW_TILES, LANE), F32),
        compiler_params=_cparams(("arbitrary",)),
        name="moe_dispatch",
    )(dest0, dest1, zero_blocks, h2)


def _expert_kernel(be_ref, nu_ref, x_ref, w1_ref, w3_ref, w2_ref, y_ref, w1b_ref, w3b_ref, w2b_ref):
    i = pl.program_id(0)

    @pl.when((i < nu_ref[0]) & ((i == 0) | (be_ref[i] != be_ref[jnp.maximum(i - 1, 0)])))
    def _():
        w1b_ref[...] = w1_ref[0].astype(BF16)
        w3b_ref[...] = w3_ref[0].astype(BF16)
        w2b_ref[...] = w2_ref[0].astype(BF16)

    @pl.when(i < nu_ref[0])
    def _():
        xb = _load_rows(x_ref, MOE_ROWS).astype(BF16)
        a = _dot(xb, w1b_ref[...])
        b = _dot(xb, w3b_ref[...])
        hmid = (a * jax.nn.sigmoid(a) * b).astype(BF16)
        _store_rows(y_ref, _dot(hmid, w2b_ref[...]))

    @pl.when(i >= nu_ref[0])
    def _():
        y_ref[...] = jnp.zeros_like(y_ref)


def _experts(block_expert, n_used, x_pad, w1, w3, w2):
    d, de = w1.shape[1], w1.shape[2]
    nb = x_pad.shape[0] // (MOE_ROWS * ROW_TILES)
    return pl.pallas_call(
        _expert_kernel,
        grid_spec=pltpu.PrefetchScalarGridSpec(
            num_scalar_prefetch=2,
            grid=(nb,),
            in_specs=[pl.BlockSpec((MOE_ROWS * ROW_TILES, LANE), lambda i, be, nu: (i, 0)),
                      pl.BlockSpec((1, d, de), lambda i, be, nu: (be[i], 0, 0)),
                      pl.BlockSpec((1, d, de), lambda i, be, nu: (be[i], 0, 0)),
                      pl.BlockSpec((1, de, d), lambda i, be, nu: (be[i], 0, 0))],
            out_specs=pl.BlockSpec((MOE_ROWS * ROW_TILES, LANE), lambda i, be, nu: (i, 0)),
            scratch_shapes=[pltpu.VMEM((d, de), BF16), pltpu.VMEM((d, de), BF16), pltpu.VMEM((de, d), BF16)]),
        out_shape=jax.ShapeDtypeStruct(x_pad.shape, F32),
        compiler_params=_cparams(("arbitrary",)),
        name="moe_experts",
    )(block_expert, n_used, x_pad, w1, w3, w2)


def _combine_kernel(dest0_ref, dest1_ref, yp_ref, rec_ref, x1_ref, mod_ref, g_ref, b_ref, o_ref,
                    buf_ref, sem, *, tm):
    step = pl.program_id(0)
    slot = step % 2

    def fetch(tile, to_slot):
        def issue(r, carry):
            for k, dest_ref in enumerate((dest0_ref, dest1_ref)):
                d = dest_ref[tile * tm + r]
                _row_copy(yp_ref.at[_tile_of(d), :], buf_ref.at[to_slot, k, _tile_of(r), :],
                          sem.at[to_slot]).start()
            return carry

        lax.fori_loop(0, tm, issue, 0, unroll=ROW_DMA_UNROLL)

    @pl.when(step == 0)
    def _():
        fetch(0, 0)

    @pl.when(step + 1 < pl.num_programs(0))
    def _():
        fetch(step + 1, 1 - slot)

    for k in range(2):
        _row_copy(yp_ref.at[pl.ds(0, tm * ROW_TILES), :], buf_ref.at[slot, k], sem.at[slot]).wait()

    rec = rec_ref[...]
    y = rec[:, 2:3] * _load_rows(buf_ref, tm, (slot, 0)) + rec[:, 3:4] * _load_rows(buf_ref, tm, (slot, 1))
    y = (1.0 + mod_ref[0, 5:6, :]) * y
    o_ref[...] = _layer_norm(ALPHA * x1_ref[...] + y) * g_ref[...] + b_ref[...]


def _combine(dest0, dest1, y_pad, rec, x1, mod, ln_g, ln_b, seq):
    nt, d = x1.shape
    tm = min(256, seq)
    return pl.pallas_call(
        functools.partial(_combine_kernel, tm=tm),
        grid_spec=pltpu.PrefetchScalarGridSpec(
            num_scalar_prefetch=2,
            grid=(nt // tm,),
            in_specs=[pl.BlockSpec(memory_space=pl.ANY),
                      pl.BlockSpec((tm, LANE), lambda i, *_: (i, 0)),
                      pl.BlockSpec((tm, d), lambda i, *_: (i, 0)),
                      pl.BlockSpec((1, 6, d), lambda i, *_: (i * tm // seq, 0, 0)),
                      pl.BlockSpec((1, d), lambda i, *_: (0, 0)),
                      pl.BlockSpec((1, d), lambda i, *_: (0, 0))],
            out_specs=pl.BlockSpec((tm, d), lambda i, *_: (i, 0)),
            scratch_shapes=[pltpu.VMEM((2, 2, tm * ROW_TILES, LANE), F32), pltpu.SemaphoreType.DMA((2,))]),
        out_shape=jax.ShapeDtypeStruct((nt, d), F32),
        compiler_params=_cparams(("arbitrary",)),
        name="moe_combine",
    )(dest0, dest1, y_pad, rec, x1, mod, ln_g, ln_b)


def _rel_bucket(dist):
    n = jnp.maximum(dist, 0)
    max_exact = REL_BUCKETS // 2
    nf = jnp.maximum(n, 1).astype(F32)
    large = max_exact + (jnp.log(nf / max_exact) / math.log(REL_MAX_DIST / max_exact)
                         * (REL_BUCKETS - max_exact)).astype(jnp.int32)
    large = jnp.minimum(large, REL_BUCKETS - 1)
    return jnp.where(n < max_exact, n, large)


def _bias_tables(rel_bias, seq):
    bucket_onehot = (_rel_bucket(jnp.arange(LANE))[:, None] == jnp.arange(REL_BUCKETS)).astype(F32)
    tab_d = jnp.einsum('db,hb->hd', bucket_onehot, rel_bias,
                       precision=lax.Precision.HIGHEST)
    tok = np.arange(QT)[None, :]
    key = np.arange(LANE)[:, None]
    far = tab_d[:, LANE - 1]
    cols = NSA_REP * QT

    def transposed(dist):
        idx = jnp.asarray(np.clip(dist, 0, LANE - 1).astype(np.int32))
        onehot = (idx[..., None] == jnp.arange(LANE, dtype=jnp.int32)).astype(F32)
        t = jnp.einsum('ijd,hd->hij', onehot, tab_d, precision=lax.Precision.HIGHEST)
        t = (t - far[:, None, None]) * LOG2E
        t = t.reshape(NSA_GROUPS, NSA_REP, LANE, QT).transpose(0, 2, 1, 3)
        return t.reshape(NSA_GROUPS, LANE, cols)

    t0t = transposed(tok - key)
    t1t = transposed(tok - key + QT)

    ns = seq // CMP_STRIDE
    d_c = tok - CMP_STRIDE * key + (CMP_STRIDE * CMP_PAD - (CMP_BLOCK - 1))
    seen = np.tile(d_c >= 0, (1, NSA_REP))
    recent = jnp.where(seen[None], transposed(d_c), NEG)
    tt = jnp.concatenate([jnp.zeros((NSA_GROUPS, ns, cols), F32), recent,
                          jnp.full((NSA_GROUPS, ns, cols), NEG, F32)], axis=1)

    rho = np.arange(WINDOW + QT)[:, None]
    tok_w = np.tile(np.arange(QT), NSA_REP)[None, :]
    band = (rho > tok_w) & (rho <= tok_w + WINDOW)
    rows = jnp.concatenate([jnp.zeros((NSA_GROUPS, WINDOW - QT, NSA_REP * QT), F32), t1t, t0t], axis=1)
    wb = jnp.where(band[None], rows, NEG)

    zeros = lambda n: jnp.zeros((NSA_GROUPS, n * QT, cols), F32)
    negs = lambda n: jnp.full((NSA_GROUPS, n * QT, cols), NEG, F32)
    diag = jnp.where(np.tile(tok >= key, (1, NSA_REP))[None], t0t, NEG)
    nd = jnp.stack([jnp.concatenate(([zeros(u - 1), t1t] if u else []) + [diag, negs(SEL_SUB - 1 - u)], axis=1)
                    for u in range(SEL_SUB)], axis=1)
    npv = jnp.concatenate([zeros(SEL_SUB - 1), t1t], axis=1)
    return nd, npv, wb, tt


def _overlap_matrix(seq):
    ns = seq // CMP_STRIDE
    nslc = seq // SLC_BLOCK
    ov = np.zeros((LANE, ns), np.float32)
    cs = np.arange(ns - 1) * CMP_STRIDE
    ss = np.arange(nslc) * SLC_BLOCK
    ov[:nslc, :ns - 1] = ((cs[None, :] < ss[:, None] + SLC_BLOCK) & (cs[None, :] + CMP_BLOCK > ss[:, None]))
    return jnp.asarray(ov, BF16)


def _reorder_cols(a):
    lead = a.shape[:-1]
    gate = a[..., MAIN_COLS:MAIN_COLS + GATE_COLS]
    per = GATE_COLS // NSA_GROUPS
    gate_blocks = []
    for g in range(NSA_GROUPS):
        gate_blocks.append(gate[..., g * per:(g + 1) * per])
        gate_blocks.append(jnp.zeros(lead + (LANE - per,), a.dtype))
    pad = jnp.zeros(lead + ((CB_MGH - CB_GATE - NSA_GROUPS) * LANE,), a.dtype)
    return jnp.concatenate([a[..., :MAIN_COLS]] + gate_blocks + [pad, a[..., MAIN_COLS + GATE_COLS:]], axis=-1)


def kernel(x, c, ada_w, ada_b, w_in, b_in, hg_lb_logits, hg_norm_w, cmp_pos_k, cmp_w1_k, cmp_b1_k, cmp_w2_k, cmp_pos_v, cmp_w1_v, cmp_b1_v, cmp_w2_v, rel_bias, w_br_hg, w_br_nsa, w_out, ln1_g, ln1_b, router_grp_w, router_grp_b, router_exp_w, router_exp_b, exp_w1, exp_w3, exp_w2, ln2_g, ln2_b):
    batch, seq, d = x.shape
    nt = batch * seq
    assert d == D_MODEL and seq % 1024 == 0 and seq // SLC_BLOCK <= LANE
    l = 0
    x2 = x.reshape(nt, d)

    c_pad = jnp.zeros((8, d), F32).at[:batch].set(c)
    mod = _adaln(c_pad, ada_w[l], ada_b[l][None])[:batch].reshape(batch, 6, d)

    proj = _inproj(x2, mod, _reorder_cols(w_in[l]).astype(BF16), _reorder_cols(b_in[l])[None], seq)

    lb_all = jnp.cumsum(jax.nn.softmax(hg_lb_logits.astype(F32), axis=0), axis=0)
    o_h = _hgrn(proj, lb_all[l][None], hg_norm_w[l][None], batch, seq)

    kvc = _compress(proj, jnp.stack([cmp_pos_k[l], cmp_pos_v[l]]),
                    jnp.stack([cmp_w1_k[l], cmp_w1_v[l]]).astype(BF16),
                    jnp.stack([cmp_b1_k[l], cmp_b1_v[l]])[:, None, :],
                    jnp.stack([cmp_w2_k[l], cmp_w2_v[l]]).astype(BF16), batch, seq)

    nd, npv, wb, tt = _bias_tables(rel_bias, seq)
    o_c, sel_t = _cmp_sel(proj, kvc, tt, _overlap_matrix(seq), batch, seq)
    o_w = _window(proj, wb, batch, seq)
    o_n = _selected(proj, sel_t, nd, npv, o_c, o_w, batch, seq)

    w_r = jnp.zeros((d, LANE), F32).at[:, :N_GROUPS].set(router_grp_w[l])
    w_r = w_r.at[:, N_GROUPS:N_GROUPS + N_EXPERTS].set(router_exp_w[l])
    b_r = jnp.zeros((1, LANE), F32).at[0, :N_GROUPS].set(router_grp_b[l])
    b_r = b_r.at[0, N_GROUPS:N_GROUPS + N_EXPERTS].set(router_exp_b[l])
    x1, h2, logits = _merge(o_h, o_n, proj, x2, mod,
                            w_br_hg[l].astype(BF16), w_br_nsa[l].astype(BF16), w_out[l].astype(BF16),
                            ln1_g[l][None], ln1_b[l][None], w_r, b_r, seq)

    rec, cnt, rec_t = _route(logits)
    counts = cnt[0, N_GROUPS:N_GROUPS + N_EXPERTS].astype(jnp.int32)
    padded = (counts + MOE_ROWS - 1) // MOE_ROWS * MOE_ROWS
    pend = jnp.cumsum(padded)
    pstart = pend - padded
    n_assign = 2 * nt
    nb = n_assign // MOE_ROWS + N_EXPERTS
    slots = rec_t.astype(jnp.int32)
    expert_ids = jnp.arange(N_EXPERTS, dtype=jnp.int32)[:, None]
    slot_base = lambda e: jnp.sum(jnp.where(e[None, :] == expert_ids, pstart[:, None], 0), axis=0)
    dest0 = slot_base(slots[0]) + slots[4]
    dest1 = slot_base(slots[1]) + slots[5]
    block_start = jnp.arange(nb, dtype=jnp.int32) * MOE_ROWS
    block_expert = jnp.minimum(jnp.sum(pend[None, :] <= block_start[:, None], axis=1),
                               N_EXPERTS - 1).astype(jnp.int32)
    n_used = pend[-1] // MOE_ROWS
    spare = n_used + jnp.arange(N_EXPERTS, dtype=jnp.int32)
    zero_blocks = jnp.concatenate([jnp.where(padded > 0, pend // MOE_ROWS - 1, -1),
                                   jnp.where(spare < nb, spare, -1)]).astype(jnp.int32)

    x_pad = _dispatch(dest0, dest1, zero_blocks, h2, nb)
    y_pad = _experts(block_expert, n_used[None].astype(jnp.int32), x_pad, exp_w1[l], exp_w3[l], exp_w2[l])
    out = _combine(dest0, dest1, y_pad, rec, x1, mod, ln2_g[l][None], ln2_b[l][None], seq)
    return out.reshape(batch, seq, d)
```

```python
import functools
import math

import numpy as np
import jax
import jax.numpy as jnp
from jax import lax
from jax.experimental import pallas as pl
from jax.experimental.pallas import tpu as pltpu

F32 = jnp.float32
BF16 = jnp.bfloat16

D_MODEL = 1024
HG_HEADS = 8
HG_DK = 128
HG_DV = 128
HG_CHUNK = 32
HG_SUPER = 256
NSA_HEADS = 8
NSA_GROUPS = 2
NSA_REP = NSA_HEADS // NSA_GROUPS
NSA_DK = 128
CMP_BLOCK = 32
CMP_STRIDE = 16
SLC_BLOCK = 64
SLC_TOPK = 16
WINDOW = 512
FORCE_SCORE = 1e4
N_FORCED = 3
REL_BUCKETS = 32
REL_MAX_DIST = 128
N_GROUPS = 4
EXP_PER_GROUP = 8
N_EXPERTS = N_GROUPS * EXP_PER_GROUP
D_EXPERT = D_MODEL // 2
DEPTH = 1
ALPHA = (2 * DEPTH) ** 0.25

LANE = 128
QT = 128
NEG = -1e30
SCALE = NSA_DK ** -0.5
LOG2E = math.log2(math.e)
SCALE_LOG2 = SCALE * LOG2E
KEY_BLK = 512
BOUND_SLACK = 1.0 + 2.0 ** -10
MAX_REF_GAP = 64.0
SEL_ROWS = 16
SEL_SUB = KEY_BLK // QT
WIN_SUB = 4
CMP_SUB = 4
CMP_PAD = 120
VMEM_LIMIT = 56 * 1024 * 1024

CB_HQ, CB_HF, CB_HI, CB_HG = 0, 8, 16, 24
CB_NQ = 32
CB_KC, CB_VC, CB_KS, CB_VS, CB_KW, CB_VW = 40, 42, 44, 46, 48, 50
CB_GATE = 52
CB_MGH, CB_MGN = 56, 64
NCB = 72
MAIN_COLS = 52 * LANE
GATE_COLS = 3 * NSA_HEADS

INPROJ_ROWS = 2048
INPROJ_COL_BLOCKS = 8
HGRN_ROWS = 512
MERGE_ROWS = 512
ROUTE_ROWS = 512
MOE_TOKEN_TILE = 256
MERGE_SPLIT = 2
ROW_TILES = D_MODEL // LANE
MOE_ROWS = 512
ROW_DMA_UNROLL = 8


def _cparams(sem):
    return pltpu.CompilerParams(dimension_semantics=sem, vmem_limit_bytes=VMEM_LIMIT)


def _dot(a, b):
    return jnp.dot(a, b, preferred_element_type=F32)


def _dot_nt(a, b):
    return lax.dot_general(a, b, (((1,), (1,)), ((), ())), preferred_element_type=F32)


def _dot_tn(a, b):
    return lax.dot_general(a, b, (((0,), (0,)), ((), ())), preferred_element_type=F32)


def _split3(x):
    hi = x.astype(BF16)
    r = x - hi.astype(F32)
    mid = r.astype(BF16)
    lo = (r - mid.astype(F32)).astype(BF16)
    return hi, mid, lo


def _dot01(m01, x):
    hi, mid, lo = _split3(x)
    return _dot(m01, hi) + _dot(m01, mid) + _dot(m01, lo)


def _layer_norm(x, eps=1e-5):
    mu = jnp.mean(x, axis=-1, keepdims=True)
    xc = x - mu
    var = jnp.mean(xc * xc, axis=-1, keepdims=True)
    return xc * lax.rsqrt(var + eps)


def _adaln_kernel(c_ref, w_ref, b_ref, o_ref):
    c = c_ref[...]
    ca = c * jax.nn.sigmoid(c)
    o_ref[...] = jnp.dot(ca, w_ref[...], precision=lax.Precision.HIGHEST,
                         preferred_element_type=F32) + b_ref[...]


def _adaln(c_pad, w, b):
    rows, d = c_pad.shape
    n = w.shape[1]
    return pl.pallas_call(
        _adaln_kernel,
        grid=(n // d,),
        in_specs=[pl.BlockSpec((rows, d), lambda j: (0, 0)),
                  pl.BlockSpec((d, d), lambda j: (0, j)),
                  pl.BlockSpec((1, d), lambda j: (0, j))],
        out_specs=pl.BlockSpec((rows, d), lambda j: (0, j)),
        out_shape=jax.ShapeDtypeStruct((rows, n), F32),
        compiler_params=_cparams(("arbitrary",)),
        name="adaln",
    )(c_pad, w, b)


def _inproj_kernel(x_ref, mod_ref, w_ref, b_ref, o_ref, hn_ref, *, ncb_tile):
    @pl.when(pl.program_id(1) == 0)
    def _():
        hn = _layer_norm(x_ref[...])
        sh = mod_ref[0, 0:1, :]
        sc = mod_ref[0, 1:2, :]
        hn_ref[...] = (hn * (1.0 + sc) + sh).astype(BF16)

    res = _dot(hn_ref[...], w_ref[...]) + b_ref[...]
    for c in range(ncb_tile):
        o_ref[c] = res[:, c * LANE:(c + 1) * LANE]


def _inproj(x2, mod, w, b, seq):
    nt, d = x2.shape
    tm = min(INPROJ_ROWS, seq)
    ncb_tile = INPROJ_COL_BLOCKS
    tn = ncb_tile * LANE
    return pl.pallas_call(
        functools.partial(_inproj_kernel, ncb_tile=ncb_tile),
        grid=(nt // tm, NCB // ncb_tile),
        in_specs=[pl.BlockSpec((tm, d), lambda i, j: (i, 0)),
                  pl.BlockSpec((1, 6, d), lambda i, j: (i * tm // seq, 0, 0)),
                  pl.BlockSpec((d, tn), lambda i, j: (0, j)),
                  pl.BlockSpec((1, tn), lambda i, j: (0, j))],
        out_specs=pl.BlockSpec((ncb_tile, tm, LANE), lambda i, j: (j, i, 0)),
        out_shape=jax.ShapeDtypeStruct((NCB, nt, LANE), F32),
        scratch_shapes=[pltpu.VMEM((tm, d), BF16)],
        compiler_params=_cparams(("arbitrary", "arbitrary")),
        name="inproj",
    )(x2, mod, w, b)


def _hgrn_kernel(q_ref, f_ref, v_ref, g_ref, lb_ref, nw_ref, o_ref, st_ref, *, rows):
    @pl.when(pl.program_id(1) == 0)
    def _():
        st_ref[...] = jnp.zeros_like(st_ref)

    sup = HG_SUPER
    ri = lax.broadcasted_iota(jnp.int32, (sup, sup), 0)
    ci = lax.broadcasted_iota(jnp.int32, (sup, sup), 1)
    same = (ri // HG_CHUNK) == (ci // HG_CHUNK)
    cum_m = jnp.where(same & (ci <= ri), 1.0, 0.0).astype(BF16)
    rt = lax.broadcasted_iota(jnp.int32, (LANE, LANE), 0)
    ct = lax.broadcasted_iota(jnp.int32, (LANE, LANE), 1)
    tril = ((rt // HG_CHUNK) == (ct // HG_CHUNK)) & (ct <= rt)
    per = sup // HG_CHUNK
    groups = [slice(g * LANE, (g + 1) * LANE) for g in range(sup // LANE)]

    heads = range(HG_HEADS)
    hs = [slice(h * LANE, (h + 1) * LANE) for h in heads]

    def wide(ref, r0):
        return jnp.concatenate([ref[h, pl.ds(r0, sup), :] for h in heads], axis=1)

    def body(i, carry):
        r0 = pl.multiple_of(i * sup, sup)
        lb = lb_ref[...]
        f = lb + (1.0 - lb) * jax.nn.sigmoid(wide(f_ref, r0))
        lf = jnp.log(f)
        k = 1.0 - f
        hi = lf.astype(BF16)
        lo = (lf - hi.astype(F32)).astype(BF16)
        b = _dot(cum_m, hi) + _dot(cum_m, lo)
        chunk = [slice(c * HG_CHUNK, (c + 1) * HG_CHUNK) for c in range(per)]
        dec = [jnp.exp(b[(c + 1) * HG_CHUNK - 1:(c + 1) * HG_CHUNK, :]) for c in range(per)]
        dec_rows = jnp.concatenate([jnp.broadcast_to(dec[c], (HG_CHUNK, dec[c].shape[1])) for c in range(per)],
                                   axis=0)
        q_in = (wide(q_ref, r0) * jnp.exp(b)).astype(BF16)
        k_dec = k * jnp.exp(-b)
        k_in = k_dec.astype(BF16)
        k_end = (k_dec * dec_rows).astype(BF16)
        vb = wide(v_ref, r0).astype(BF16)

        att = [[jnp.where(tril, _dot_nt(q_in[g, hs[h]], k_in[g, hs[h]]), 0.0).astype(BF16) for g in groups]
               for h in heads]
        upd = [[_dot_tn(vb[chunk[c], hs[h]], k_end[chunk[c], hs[h]]) for c in range(per)] for h in heads]
        intra = [[_dot(att[h][n], vb[g, hs[h]]) for n, g in enumerate(groups)] for h in heads]

        st = [st_ref[h] for h in heads]
        inter = [[] for _ in heads]
        for c in range(per):
            for h in heads:
                inter[h].append(_dot_nt(q_in[chunk[c], hs[h]], st[h].astype(BF16)))
                st[h] = dec[c][:, hs[h]] * st[h] + upd[h][c]
        for h in heads:
            st_ref[h] = st[h]

        o = [jnp.concatenate(intra[h], axis=0) + jnp.concatenate(inter[h], axis=0) for h in heads]
        scale = [lax.rsqrt(jnp.mean(o[h] * o[h], axis=-1, keepdims=True) + 1e-6) for h in heads]
        on = jnp.concatenate([o[h] * scale[h] for h in heads], axis=1)
        on = on * nw_ref[...] * jax.nn.sigmoid(wide(g_ref, r0))
        o_ref[pl.ds(r0, sup), :] = on.astype(BF16)
        return carry

    lax.fori_loop(0, rows // sup, body, 0)


def _hgrn(proj, lb, nw, batch, seq):
    nt = proj.shape[1]
    tb = min(HGRN_ROWS, seq)
    nblk = seq // tb

    def slab(cb0):
        return pl.BlockSpec((HG_HEADS, tb, LANE), lambda b, t: (cb0 // HG_HEADS, b * nblk + t, 0))

    vec = pl.BlockSpec((1, HG_HEADS * LANE), lambda b, t: (0, 0))
    return pl.pallas_call(
        functools.partial(_hgrn_kernel, rows=tb),
        grid=(batch, nblk),
        in_specs=[slab(CB_HQ), slab(CB_HF), slab(CB_HI), slab(CB_HG), vec, vec],
        out_specs=pl.BlockSpec((tb, HG_HEADS * HG_DV), lambda b, t: (b * nblk + t, 0)),
        out_shape=jax.ShapeDtypeStruct((nt, HG_HEADS * HG_DV), BF16),
        scratch_shapes=[pltpu.VMEM((HG_HEADS, HG_DV, HG_DK), F32)],
        compiler_params=_cparams(("arbitrary", "arbitrary")),
        name="hgrn2",
    )(proj, proj, proj, proj, lb, nw)


def _compress_kernel(x_ref, pos_ref, w1_ref, b1_ref, w2_ref, o_ref, *, ns):
    p0 = jnp.zeros((ns, LANE), F32)
    p1 = jnp.zeros((ns, LANE), F32)
    for j in range(CMP_STRIDE):
        tok = x_ref[0, pl.ds(j, ns, stride=CMP_STRIDE), :]
        rows = slice(j * LANE, (j + 1) * LANE)
        late = slice((CMP_STRIDE + j) * LANE, (CMP_STRIDE + j + 1) * LANE)
        p0 = p0 + _dot((tok + pos_ref[0, j:j + 1, :]).astype(BF16), w1_ref[0, rows, :])
        p1 = p1 + _dot((tok + pos_ref[0, CMP_STRIDE + j:CMP_STRIDE + j + 1, :]).astype(BF16), w1_ref[0, late, :])
    h = p0 + pltpu.roll(p1, ns - 1, axis=0) + b1_ref[0]
    a = h * jax.nn.sigmoid(h)
    out = _dot(a.astype(BF16), w2_ref[0])
    row = lax.broadcasted_iota(jnp.int32, out.shape, 0)
    out = jnp.where(row < ns - 1, out, 0.0)
    o_ref[0, 0, 0] = out


def _compress(proj, pos, w1, b1, w2, batch, seq):
    ns = seq // CMP_STRIDE
    np_rows = ns
    width = CMP_STRIDE * LANE
    return pl.pallas_call(
        functools.partial(_compress_kernel, ns=ns),
        grid=(2, batch, NSA_GROUPS),
        in_specs=[pl.BlockSpec((1, seq, LANE), lambda s, b, g: (CB_KC + NSA_GROUPS * s + g, b, 0)),
                  pl.BlockSpec((1, CMP_BLOCK, LANE), lambda s, b, g: (s, 0, 0)),
                  pl.BlockSpec((1, 2 * width, LANE), lambda s, b, g: (s, 0, 0)),
                  pl.BlockSpec((1, 1, LANE), lambda s, b, g: (s, 0, 0)),
                  pl.BlockSpec((1, LANE, LANE), lambda s, b, g: (s, 0, 0))],
        out_specs=pl.BlockSpec((1, 1, 1, np_rows, LANE), lambda s, b, g: (s, b, g, 0, 0)),
        out_shape=jax.ShapeDtypeStruct((2, batch, NSA_GROUPS, np_rows, LANE), F32),
        compiler_params=_cparams(("arbitrary", "arbitrary", "arbitrary")),
        name="nsa_compress",
    )(proj, pos, w1, b1, w2)


def _cmp_sel_kernel(q0, q1, q2, q3, kc_ref, vc_ref, gate_ref, tt_ref, ov_ref, oc_ref, sel_ref,
                    kb_ref, vt_ref, imp_ref, *, np_rows):
    step = pl.program_id(2)
    cols = NSA_REP * QT
    ns = np_rows

    @pl.when(step == 0)
    def _():
        kb_ref[...] = (kc_ref[0, 0, 0] * SCALE_LOG2).astype(BF16)
        for c in range(np_rows // LANE):
            vt_ref[:, c * LANE:(c + 1) * LANE] = vc_ref[0, 0, 0, c * LANE:(c + 1) * LANE, :].T.astype(BF16)

    subs = range(CMP_SUB)
    tis = [step * CMP_SUB + u for u in subs]
    qts = [_q_transposed((q0, q1, q2, q3), u) for u in subs]

    def group_sum(p):
        tot = p[:, 0:QT]
        for r in range(1, NSA_REP):
            tot = tot + p[:, r * QT:(r + 1) * QT]
        return tot

    def attend(nrows):
        kb = kb_ref[0:nrows, :]
        bias = [tt_ref[0, pl.ds(pl.multiple_of(ns + CMP_PAD - tis[u] * (QT // CMP_STRIDE), 8), nrows), :]
                for u in subs]
        ss = [_dot(kb, qts[u]) + bias[u] for u in subs]
        ms = [jnp.max(ss[u], axis=0, keepdims=True) for u in subs]
        ps = [jnp.exp2(ss[u] - ms[u]) for u in subs]
        ls = [jnp.sum(ps[u], axis=0, keepdims=True) for u in subs]
        invs = [jnp.where(ms[u] > 0.5 * NEG, 1.0 / ls[u], 0.0) for u in subs]
        pn = [ps[u] * invs[u] for u in subs]
        vt = vt_ref[:, 0:nrows]
        os_ = [_dot(vt, pn[u].astype(BF16)) for u in subs]
        for u in subs:
            rows = slice(u * QT, (u + 1) * QT)
            gt = jax.nn.sigmoid(gate_ref[0, rows, :])
            for r in range(NSA_REP):
                oc_ref[rows, r * LANE:(r + 1) * LANE] = gt[:, 3 * r:3 * r + 1] * os_[u][:, r * QT:(r + 1) * QT].T
            imp_ref[:, rows] = _dot01(ov_ref[:, 0:nrows], group_sum(pn[u]))

    visible = (step + 1) * CMP_SUB * (QT // CMP_STRIDE)
    ngroups = np_rows // LANE
    for g in range(1, ngroups + 1):
        upper = visible <= g * LANE if g < ngroups else True

        @pl.when((visible > (g - 1) * LANE) & upper)
        def _():
            attend(g * LANE)

    imp = imp_ref[...]
    width = CMP_SUB * QT
    jj = lax.broadcasted_iota(jnp.int32, (LANE, width), 0)
    tok = step * width + lax.broadcasted_iota(jnp.int32, (LANE, width), 1)
    cur = tok // SLC_BLOCK
    forced = (jj == 0) | (jj == cur) | (jj == cur - 1)
    assert FORCE_SCORE > NSA_REP
    score = jnp.where(forced, -jnp.inf, jnp.where(jj <= cur, imp, -1.0))
    selb = jnp.where(forced, 0.0, NEG)
    jf = jj.astype(F32)
    for _ in range(SLC_TOPK - N_FORCED):
        mval = jnp.max(score, axis=0, keepdims=True)
        first = jnp.min(jnp.where(score == mval, jf, float(LANE)), axis=0, keepdims=True)
        pick = jf == first
        selb = jnp.where(pick, 0.0, selb)
        score = jnp.where(pick, -jnp.inf, score)
    sel_ref[0, 0] = selb


def _q_specs(nsteps, rows=QT):
    return [pl.BlockSpec((1, rows, LANE),
                         functools.partial(lambda b, g, t, r: (CB_NQ + NSA_REP * g + r, b * nsteps + t, 0), r=r))
            for r in range(NSA_REP)]


def _cmp_sel(proj, kvc, tt, ov_t, batch, seq):
    nt = proj.shape[1]
    rows = CMP_SUB * QT
    nsteps = seq // rows
    cols = NSA_REP * QT
    np_rows = kvc.shape[3]
    kv_spec = lambda s: pl.BlockSpec((1, 1, 1, np_rows, LANE), lambda b, g, t: (s, b, g, 0, 0))
    return pl.pallas_call(
        functools.partial(_cmp_sel_kernel, np_rows=np_rows),
        grid=(batch, NSA_GROUPS, nsteps),
        in_specs=_q_specs(nsteps, rows) + [
            kv_spec(0), kv_spec(1),
            pl.BlockSpec((1, rows, LANE), lambda b, g, t: (CB_GATE + g, b * nsteps + t, 0)),
            pl.BlockSpec((1, tt.shape[1], cols), lambda b, g, t: (g, 0, 0)),
            pl.BlockSpec((LANE, np_rows), lambda b, g, t: (0, 0))],
        out_specs=[pl.BlockSpec((rows, cols), lambda b, g, t: (b * nsteps + t, g)),
                   pl.BlockSpec((1, 1, LANE, rows), lambda b, g, t: (b, g, 0, t))],
        out_shape=[jax.ShapeDtypeStruct((nt, NSA_HEADS * LANE), F32),
                   jax.ShapeDtypeStruct((batch, NSA_GROUPS, LANE, seq), F32)],
        scratch_shapes=[pltpu.VMEM((np_rows, LANE), BF16), pltpu.VMEM((LANE, np_rows), BF16),
                        pltpu.VMEM((LANE, rows), F32)],
        compiler_params=_cparams(("arbitrary", "arbitrary", "arbitrary")),
        name="nsa_cmp_select",
    )(proj, proj, proj, proj, kvc, kvc, proj, tt, ov_t)


def _q_transposed(q_refs, sub=0):
    return jnp.concatenate([r[0, sub * QT:(sub + 1) * QT, :].T for r in q_refs], axis=1).astype(BF16)


def _finish_t(state, gate_ref, branch, o_ref, sub=0, others=()):
    _, l, acc = state
    o = acc / jnp.where(l == 0.0, 1.0, l)
    rows = slice(sub * QT, (sub + 1) * QT)
    gt = jax.nn.sigmoid(gate_ref[0, rows, :])
    for r in range(NSA_REP):
        col = 3 * r + branch
        lanes = slice(r * LANE, (r + 1) * LANE)
        val = gt[:, col:col + 1] * o[:, r * QT:(r + 1) * QT].T
        for other in others:
            val = val + other[rows, lanes]
        o_ref[rows, lanes] = val.astype(o_ref.dtype)


def _init_state(cols):
    return (jnp.full((1, cols), NEG, F32), jnp.zeros((1, cols), F32), jnp.zeros((LANE, cols), F32))


def _win_kernel(q0, q1, q2, q3, k_ref, v_ref, gate_ref, wb_ref, o_ref, kb_ref, vt_ref, *, seq):
    step = pl.program_id(2)
    cols = NSA_REP * QT
    wt = WINDOW // QT

    lane = lax.broadcasted_iota(jnp.int32, (QT, LANE), 1)
    pad_mark = jnp.where(lane == 0, 1.0, 0.0).astype(BF16)

    @pl.when(step == 0)
    def _():
        for i in range(wt):
            kb_ref[i * QT:(i + 1) * QT, 0:LANE] = jnp.zeros((QT, LANE), BF16)
            kb_ref[i * QT:(i + 1) * QT, LANE:2 * LANE] = pad_mark
            vt_ref[i] = jnp.zeros((LANE, QT), BF16)

        def fill(i, carry):
            r0 = pl.multiple_of(i * QT, QT)
            kb_ref[pl.ds(WINDOW + r0, QT), 0:LANE] = (k_ref[0, pl.ds(r0, QT), :] * SCALE_LOG2).astype(BF16)
            kb_ref[pl.ds(WINDOW + r0, QT), LANE:2 * LANE] = jnp.zeros((QT, LANE), BF16)
            vt_ref[wt + i] = v_ref[0, pl.ds(r0, QT), :].T.astype(BF16)
            return carry

        lax.fori_loop(0, seq // QT, fill, 0)

    subs = range(WIN_SUB)
    tis = [step * WIN_SUB + sub for sub in subs]
    row = lax.broadcasted_iota(jnp.int32, (LANE, cols), 0)
    pad_rows = jnp.where(row == 0, NEG, 0.0).astype(BF16)
    qts = [jnp.concatenate([_q_transposed((q0, q1, q2, q3), sub), pad_rows], axis=0) for sub in subs]
    ss = [_dot(kb_ref[pl.ds(pl.multiple_of(tis[u] * QT, QT), WINDOW + QT), :], qts[u]) + wb_ref[0]
          for u in subs]
    ms = [jnp.max(ss[u], axis=0, keepdims=True) for u in subs]
    ps = [jnp.exp2(ss[u] - ms[u]) for u in subs]
    ls = [jnp.sum(ps[u], axis=0, keepdims=True) for u in subs]
    pbs = [ps[u].astype(BF16) for u in subs]
    accs = [_dot(vt_ref[tis[u]], pbs[u][0:QT]) for u in subs]
    for d in range(1, wt + 1):
        accs = [accs[u] + _dot(vt_ref[tis[u] + d], pbs[u][d * QT:(d + 1) * QT]) for u in subs]
    for u in subs:
        _finish_t((ms[u], ls[u], accs[u]), gate_ref, 2, o_ref, u)


def _window(proj, wb, batch, seq):
    nt = proj.shape[1]
    rows = WIN_SUB * QT
    nqt = seq // rows
    cols = NSA_REP * QT
    slab = lambda cb0: pl.BlockSpec((1, seq, LANE), lambda b, g, t: (cb0 + g, b, 0))
    return pl.pallas_call(
        functools.partial(_win_kernel, seq=seq),
        grid=(batch, NSA_GROUPS, nqt),
        in_specs=_q_specs(nqt, rows) + [
            slab(CB_KW), slab(CB_VW),
            pl.BlockSpec((1, rows, LANE), lambda b, g, t: (CB_GATE + g, b * nqt + t, 0)),
            pl.BlockSpec((1, WINDOW + QT, cols), lambda b, g, t: (g, 0, 0))],
        out_specs=pl.BlockSpec((rows, cols), lambda b, g, t: (b * nqt + t, g)),
        out_shape=jax.ShapeDtypeStruct((nt, NSA_HEADS * LANE), F32),
        scratch_shapes=[pltpu.VMEM((seq + WINDOW, 2 * LANE), BF16),
                        pltpu.VMEM((seq // QT + WINDOW // QT, LANE, QT), BF16)],
        compiler_params=_cparams(("arbitrary", "arbitrary", "arbitrary")),
        name="nsa_window",
    )(proj, proj, proj, proj, proj, proj, proj, wb)


def _softmax_steps(ss, vt, states):
    n = range(len(ss))
    m_new = [jnp.maximum(states[u][0], jnp.max(ss[u], axis=0, keepdims=True)) for u in n]
    alpha = [jnp.exp2(states[u][0] - m_new[u]) for u in n]
    p = [jnp.exp2(ss[u] - m_new[u]) for u in n]
    l = [alpha[u] * states[u][1] + jnp.sum(p[u], axis=0, keepdims=True) for u in n]
    vts = vt if isinstance(vt, (list, tuple)) else [vt] * len(ss)
    pv = [_dot(vts[u], p[u].astype(BF16)) for u in n]
    return [(m_new[u], l[u], alpha[u] * states[u][2] + pv[u]) for u in n]


def _sel_kernel(q0, q1, q2, q3, k_ref, v_ref, sel_ref, gate_ref, nd_ref, npv_ref, oc_ref, ow_ref, o_ref,
                kb_ref, vt_ref, qa_ref, kn_ref, m_ref, l_ref, acc_ref, *, seq):
    step = pl.program_id(2)
    cols = NSA_REP * QT
    spb = KEY_BLK // SLC_BLOCK
    subs = range(SEL_SUB)

    @pl.when(step == 0)
    def _():
        blk = lax.broadcasted_iota(jnp.int32, (KEY_BLK, LANE), 0) // SLC_BLOCK
        onehot = jnp.where(blk == lax.broadcasted_iota(jnp.int32, (KEY_BLK, LANE), 1), 1.0, 0.0).astype(BF16)

        def fill(i, kmax2):
            r0 = pl.multiple_of(i * KEY_BLK, KEY_BLK)
            kblk = (k_ref[0, pl.ds(r0, KEY_BLK), :] * SCALE_LOG2).astype(BF16)
            kb_ref[pl.ds(r0, KEY_BLK), 0:LANE] = kblk
            kb_ref[pl.ds(r0, KEY_BLK), LANE:2 * LANE] = onehot
            for c in range(SEL_SUB):
                vt_ref[i, :, c * QT:(c + 1) * QT] = v_ref[0, pl.ds(r0 + c * QT, QT), :].T.astype(BF16)
            n2 = jnp.sum(jnp.square(kblk.astype(F32)), axis=1, keepdims=True)
            return jnp.maximum(kmax2, jnp.max(n2, axis=0, keepdims=True))

        kmax2 = lax.fori_loop(0, seq // KEY_BLK, fill, jnp.zeros((1, 1), F32))
        kn_ref[...] = jnp.sqrt(kmax2)
        qa_ref[...] = jnp.zeros_like(qa_ref)

    qts = [_q_transposed((q0, q1, q2, q3), u) for u in subs]
    for u in subs:
        qa_ref[u, 0:LANE, :] = qts[u]
    bounds = [jnp.sqrt(jnp.sum(jnp.square(qts[u].astype(F32)), axis=0, keepdims=True)) * kn_ref[...]
              * BOUND_SLACK + 1.0 for u in subs]

    def block(kb):
        r0 = pl.multiple_of(kb * KEY_BLK, KEY_BLK)
        return kb_ref[pl.ds(r0, KEY_BLK), :], vt_ref[kb]

    pad_rows = jnp.zeros((SEL_ROWS - spb, cols), F32)

    def scores(kb, k, rows_of=lambda u: KEY_BLK):
        for u in subs:
            rows = sel_ref[0, 0, pl.ds(pl.multiple_of(kb * spb, spb), spb), u * QT:(u + 1) * QT]
            rows = jnp.concatenate([rows] * NSA_REP, axis=1)
            qa_ref[u, LANE:LANE + SEL_ROWS, :] = jnp.concatenate([rows, pad_rows], axis=0).astype(BF16)
        return [_dot(k[0:rows_of(u)], qa_ref[u]) for u in subs]

    seen = lambda u: (u + 1) * QT
    k, vt = block(step)
    ss = scores(step, k, seen)
    ss = [ss[u] + nd_ref[0, u, 0:seen(u), :] for u in subs]
    states = _softmax_steps(ss, [vt[:, 0:seen(u)] for u in subs], [_init_state(cols) for _ in subs])

    for u in subs:
        m_ref[u], l_ref[u], acc_ref[u] = states[u]

    def prev_scores():
        k, vt = block(step - 1)
        ss = scores(step - 1, k)
        ss[0] = ss[0] + npv_ref[0]
        return ss, vt

    n_old = jnp.maximum(step - 1, 0)
    bounds[0] = bounds[0] + jnp.max(npv_ref[0], axis=0, keepdims=True)
    m_fix = [jnp.maximum(states[u][0], bounds[u]) for u in subs]
    gap = jnp.max(jnp.concatenate([m_fix[u] - states[u][0] for u in subs], axis=1))
    fixed_ok = gap < MAX_REF_GAP

    @pl.when(fixed_ok)
    def _():
        for u in subs:
            alpha = jnp.exp2(m_ref[u] - m_fix[u])
            l_ref[u] = alpha * l_ref[u]
            acc_ref[u] = alpha * acc_ref[u]
            m_ref[u] = m_fix[u]

        def fixed_step(ss, vt):
            ps = [jnp.exp2(ss[u] - m_ref[u]) for u in subs]
            for u in subs:
                l_ref[u] = l_ref[u] + jnp.sum(ps[u], axis=0, keepdims=True)
            pv = [_dot(vt, ps[u].astype(BF16)) for u in subs]
            for u in subs:
                acc_ref[u] = acc_ref[u] + pv[u]

        @pl.when(step >= 1)
        def _():
            fixed_step(*prev_scores())

        def body(kb, carry):
            k, vt = block(kb)
            fixed_step(scores(kb, k), vt)
            return carry

        lax.fori_loop(0, n_old, body, 0)

    @pl.when(jnp.logical_not(fixed_ok))
    def _():
        def online_step(ss, vt):
            st = _softmax_steps(ss, vt, [(m_ref[u], l_ref[u], acc_ref[u]) for u in subs])
            for u in subs:
                m_ref[u], l_ref[u], acc_ref[u] = st[u]

        @pl.when(step >= 1)
        def _():
            online_step(*prev_scores())

        def body(kb, carry):
            k, vt = block(kb)
            online_step(scores(kb, k), vt)
            return carry

        lax.fori_loop(0, n_old, body, 0)

    for u in subs:
        _finish_t((m_ref[u], l_ref[u], acc_ref[u]), gate_ref, 1, o_ref, u, others=(oc_ref, ow_ref))


def _selected(proj, sel_t, nd, npv, o_c, o_w, batch, seq):
    nt = proj.shape[1]
    rows = SEL_SUB * QT
    nsteps = seq // rows
    cols = NSA_REP * QT
    slab = lambda cb0: pl.BlockSpec((1, seq, LANE), lambda b, g, t: (cb0 + g, b, 0))
    return pl.pallas_call(
        functools.partial(_sel_kernel, seq=seq),
        grid=(batch, NSA_GROUPS, nsteps),
        in_specs=_q_specs(nsteps, rows) + [
            slab(CB_KS), slab(CB_VS),
            pl.BlockSpec((1, 1, LANE, rows), lambda b, g, t: (b, g, 0, t)),
            pl.BlockSpec((1, rows, LANE), lambda b, g, t: (CB_GATE + g, b * nsteps + t, 0)),
            pl.BlockSpec((1, SEL_SUB, KEY_BLK, cols), lambda b, g, t: (g, 0, 0, 0)),
            pl.BlockSpec((1, KEY_BLK, cols), lambda b, g, t: (g, 0, 0)),
            pl.BlockSpec((rows, cols), lambda b, g, t: (b * nsteps + t, g)),
            pl.BlockSpec((rows, cols), lambda b, g, t: (b * nsteps + t, g))],
        out_specs=pl.BlockSpec((rows, cols), lambda b, g, t: (b * nsteps + t, g)),
        out_shape=jax.ShapeDtypeStruct((nt, NSA_HEADS * LANE), BF16),
        scratch_shapes=[pltpu.VMEM((seq, 2 * LANE), BF16),
                        pltpu.VMEM((seq // KEY_BLK, LANE, KEY_BLK), BF16),
                        pltpu.VMEM((SEL_SUB, 2 * LANE, cols), BF16),
                        pltpu.VMEM((1, 1), F32),
                        pltpu.VMEM((SEL_SUB, 1, cols), F32), pltpu.VMEM((SEL_SUB, 1, cols), F32),
                        pltpu.VMEM((SEL_SUB, LANE, cols), F32)],
        compiler_params=_cparams(("arbitrary", "arbitrary", "arbitrary")),
        name="nsa_selected",
    )(proj, proj, proj, proj, proj, proj, sel_t, proj, nd, npv, o_c, o_w)


def _merge_kernel(oh_ref, on_ref, mgh_ref, mgn_ref, x_ref, mod_ref,
                  wh_ref, wn_ref, wo_ref, g_ref, b_ref, wr_ref, br_ref,
                  x1_ref, h2_ref, lg_ref):
    nblk = D_MODEL // LANE
    tm = x_ref.shape[0]
    halves = [slice(s * tm // MERGE_SPLIT, (s + 1) * tm // MERGE_SPLIT) for s in range(MERGE_SPLIT)]
    a_h = [_dot(oh_ref[rs, :], wh_ref[...]) for rs in halves]
    a_n = [_dot(on_ref[rs, :], wn_ref[...]) for rs in halves]
    gh = [jnp.concatenate([mgh_ref[c, rs, :] for c in range(nblk)], axis=-1) for rs in halves]
    gn = [jnp.concatenate([mgn_ref[c, rs, :] for c in range(nblk)], axis=-1) for rs in halves]
    merged = [(jax.nn.sigmoid(gh[s]) * a_h[s] + jax.nn.sigmoid(gn[s]) * a_n[s]).astype(BF16)
              for s in range(MERGE_SPLIT)]
    y = [(1.0 + mod_ref[0, 2:3, :]) * _dot(merged[s], wo_ref[...]) for s in range(MERGE_SPLIT)]
    x1 = [_layer_norm(ALPHA * x_ref[rs, :] + y[s]) * g_ref[...] + b_ref[...] for s, rs in enumerate(halves)]
    h2 = [_layer_norm(x1[s]) * (1.0 + mod_ref[0, 4:5, :]) + mod_ref[0, 3:4, :] for s in range(MERGE_SPLIT)]
    lg = [jnp.dot(h2[s], wr_ref[...], precision=lax.Precision.HIGHEST, preferred_element_type=F32) + br_ref[...]
          for s in range(MERGE_SPLIT)]
    for s, rs in enumerate(halves):
        x1_ref[rs, :] = x1[s]
        _store_rows(h2_ref, h2[s], rs.start)
        lg_ref[rs, :] = lg[s]


def _merge(o_h, o_n, proj, x2, mod, w_h, w_n, w_o, ln_g, ln_b, w_r, b_r, seq):
    nt, d = x2.shape
    tm = min(MERGE_ROWS, seq)
    nblk = d // LANE
    row = lambda w: pl.BlockSpec((tm, w), lambda i: (i, 0))
    full = lambda a: pl.BlockSpec(a.shape, lambda i: (0,) * a.ndim)
    return pl.pallas_call(
        _merge_kernel,
        grid=(nt // tm,),
        in_specs=[row(d), row(d),
                  pl.BlockSpec((nblk, tm, LANE), lambda i: (CB_MGH // nblk, i, 0)),
                  pl.BlockSpec((nblk, tm, LANE), lambda i: (CB_MGN // nblk, i, 0)),
                  row(d),
                  pl.BlockSpec((1, 6, d), lambda i: (i * tm // seq, 0, 0)),
                  full(w_h), full(w_n), full(w_o), full(ln_g), full(ln_b), full(w_r), full(b_r)],
        out_specs=[row(d), pl.BlockSpec((tm * ROW_TILES, LANE), lambda i: (i, 0)), row(LANE)],
        out_shape=[jax.ShapeDtypeStruct((nt, d), F32),
                   jax.ShapeDtypeStruct((nt * ROW_TILES, LANE), F32),
                   jax.ShapeDtypeStruct((nt, LANE), F32)],
        compiler_params=_cparams(("arbitrary",)),
        name="merge_outproj",
    )(o_h, o_n, proj, proj, x2, mod, w_h, w_n, w_o, ln_g, ln_b, w_r, b_r)


def _route_kernel(lg_ref, rec_ref, cnt_ref, rect_ref, carry_ref, *, tm):
    @pl.when(pl.program_id(0) == 0)
    def _():
        carry_ref[...] = jnp.zeros_like(carry_ref)

    lg = lg_ref[...]
    lane = lax.broadcasted_iota(jnp.int32, (tm, LANE), 1).astype(F32)
    far = float(LANE)
    gmask = lane < N_GROUPS
    gl = jnp.where(gmask, lg, -jnp.inf)
    gmax = jnp.max(gl, axis=-1, keepdims=True)
    gsum = jnp.sum(jnp.where(gmask, jnp.exp(gl - gmax), 0.0), axis=-1, keepdims=True)
    grp_p = 1.0 / gsum
    gidx = jnp.min(jnp.where(gl == gmax, lane, far), axis=-1, keepdims=True)
    lo = N_GROUPS + EXP_PER_GROUP * gidx
    emask = (lane >= lo) & (lane < lo + EXP_PER_GROUP)
    el = jnp.where(emask, lg, -jnp.inf)
    m1 = jnp.max(el, axis=-1, keepdims=True)
    i1 = jnp.min(jnp.where(el == m1, lane, far), axis=-1, keepdims=True)
    el2 = jnp.where(lane == i1, -jnp.inf, el)
    m2 = jnp.max(el2, axis=-1, keepdims=True)
    i2 = jnp.min(jnp.where(emask & (lane != i1) & (el2 == m2), lane, far), axis=-1, keepdims=True)
    e = jnp.exp(m2 - m1)
    w0 = grp_p / (1.0 + e)
    w1 = grp_p * e / (1.0 + e)

    oh0 = lane == i1
    oh1 = lane == i2
    f0 = jnp.where(oh0, 1.0, 0.0)
    f1 = jnp.where(oh1, 1.0, 0.0)
    ri = lax.broadcasted_iota(jnp.int32, (tm, tm), 0)
    ci = lax.broadcasted_iota(jnp.int32, (tm, tm), 1)
    before = jnp.where(ci < ri, 1.0, 0.0).astype(BF16)
    cum0 = _dot(before, f0.astype(BF16))
    cum1 = _dot(before, f1.astype(BF16))
    tot0 = jnp.sum(f0, axis=0, keepdims=True)
    tot1 = jnp.sum(f1, axis=0, keepdims=True)
    carry = carry_ref[...]
    rank0 = jnp.sum(jnp.where(oh0, carry + cum0, 0.0), axis=-1, keepdims=True)
    rank1 = jnp.sum(jnp.where(oh1, carry + tot0 + cum1, 0.0), axis=-1, keepdims=True)
    carry = carry + tot0 + tot1
    carry_ref[...] = carry
    cnt_ref[...] = carry

    rec = jnp.where(lane == 0, i1 - N_GROUPS, 0.0)
    rec = jnp.where(lane == 1, i2 - N_GROUPS, rec)
    rec = jnp.where(lane == 2, w0, rec)
    rec = jnp.where(lane == 3, w1, rec)
    rec = jnp.where(lane == 4, rank0, rec)
    rec = jnp.where(lane == 5, rank1, rec)
    rec_ref[...] = rec
    rect_ref[...] = jnp.concatenate([rec[c * LANE:(c + 1) * LANE, :].T[0:8, :] for c in range(tm // LANE)], axis=1)


def _route(logits):
    nt = logits.shape[0]
    tm = min(ROUTE_ROWS, nt)
    return pl.pallas_call(
        functools.partial(_route_kernel, tm=tm),
        grid=(nt // tm,),
        in_specs=[pl.BlockSpec((tm, LANE), lambda i: (i, 0))],
        out_specs=[pl.BlockSpec((tm, LANE), lambda i: (i, 0)),
                   pl.BlockSpec((1, LANE), lambda i: (0, 0)),
                   pl.BlockSpec((8, tm), lambda i: (0, i))],
        out_shape=[jax.ShapeDtypeStruct((nt, LANE), F32),
                   jax.ShapeDtypeStruct((1, LANE), F32),
                   jax.ShapeDtypeStruct((8, nt), F32)],
        scratch_shapes=[pltpu.VMEM((1, LANE), F32)],
        compiler_params=_cparams(("arbitrary",)),
        name="moe_route",
    )(logits)


def _row_copy(src, dst, sem):
    return pltpu.make_async_copy(src, dst, sem)


def _tile_of(r):
    return pl.ds(pl.multiple_of(r * ROW_TILES, ROW_TILES), ROW_TILES)


def _load_rows(ref, n, lead=()):
    return jnp.concatenate([ref[lead + (pl.ds(c, n, stride=ROW_TILES), slice(None))]
                            for c in range(ROW_TILES)], axis=1)


def _store_rows(ref, val, row0=0):
    n = val.shape[0]
    for c in range(ROW_TILES):
        ref[pl.ds(row0 * ROW_TILES + c, n, stride=ROW_TILES), :] = val[:, c * LANE:(c + 1) * LANE]


def _dispatch_kernel(dest0_ref, dest1_ref, zb_ref, h_ref, xp_ref, z_ref, sem, zsem, *, tm):
    step = pl.program_id(0)
    base = step * tm
    blk = MOE_ROWS * ROW_TILES

    @pl.when(step == 0)
    def _():
        z_ref[...] = jnp.zeros_like(z_ref)

        def zero_copy(j):
            b = jnp.maximum(zb_ref[j], 0)
            return _row_copy(z_ref, xp_ref.at[pl.ds(pl.multiple_of(b * blk, blk), blk), :], zsem)

        def start(j, carry):
            @pl.when(zb_ref[j] >= 0)
            def _():
                zero_copy(j).start()
            return carry

        def wait(j, carry):
            @pl.when(zb_ref[j] >= 0)
            def _():
                zero_copy(j).wait()
            return carry

        lax.fori_loop(0, 2 * N_EXPERTS, start, 0)
        lax.fori_loop(0, 2 * N_EXPERTS, wait, 0)

    def issue(r, carry):
        for dest_ref in (dest0_ref, dest1_ref):
            d = dest_ref[base + r]
            _row_copy(h_ref.at[_tile_of(r), :], xp_ref.at[_tile_of(d), :], sem).start()
        return carry

    lax.fori_loop(0, tm, issue, 0, unroll=ROW_DMA_UNROLL)
    for _ in range(2):
        _row_copy(h_ref, xp_ref.at[pl.ds(0, tm * ROW_TILES), :], sem).wait()


def _dispatch(dest0, dest1, zero_blocks, h2, n_blocks):
    nt = h2.shape[0] // ROW_TILES
    tm = min(MOE_TOKEN_TILE, nt)
    return pl.pallas_call(
        functools.partial(_dispatch_kernel, tm=tm),
        grid_spec=pltpu.PrefetchScalarGridSpec(
            num_scalar_prefetch=3,
            grid=(nt // tm,),
            in_specs=[pl.BlockSpec((tm * ROW_TILES, LANE), lambda i, *_: (i, 0))],
            out_specs=pl.BlockSpec(memory_space=pl.ANY),
            scratch_shapes=[pltpu.VMEM((MOE_ROWS * ROW_TILES, LANE), F32),
                            pltpu.SemaphoreType.DMA(()), pltpu.SemaphoreType.DMA(())]),
        out_shape=jax.ShapeDtypeStruct((n_blocks * MOE_ROWS * ROW_TILES, LANE), F32),
        compiler_params=_cparams(("arbitrary",)),
        name="moe_dispatch",
    )(dest0, dest1, zero_blocks, h2)


def _expert_kernel(be_ref, nu_ref, x_ref, w1_ref, w3_ref, w2_ref, y_ref, w1b_ref, w3b_ref, w2b_ref):
    i = pl.program_id(0)

    @pl.when((i < nu_ref[0]) & ((i == 0) | (be_ref[i] != be_ref[jnp.maximum(i - 1, 0)])))
    def _():
        w1b_ref[...] = w1_ref[0].astype(BF16)
        w3b_ref[...] = w3_ref[0].astype(BF16)
        w2b_ref[...] = w2_ref[0].astype(BF16)

    @pl.when(i < nu_ref[0])
    def _():
        xb = _load_rows(x_ref, MOE_ROWS).astype(BF16)
        a = _dot(xb, w1b_ref[...])
        b = _dot(xb, w3b_ref[...])
        hmid = (a * jax.nn.sigmoid(a) * b).astype(BF16)
        _store_rows(y_ref, _dot(hmid, w2b_ref[...]))

    @pl.when(i >= nu_ref[0])
    def _():
        y_ref[...] = jnp.zeros_like(y_ref)


def _experts(block_expert, n_used, x_pad, w1, w3, w2):
    d, de = w1.shape[1], w1.shape[2]
    nb = x_pad.shape[0] // (MOE_ROWS * ROW_TILES)
    return pl.pallas_call(
        _expert_kernel,
        grid_spec=pltpu.PrefetchScalarGridSpec(
            num_scalar_prefetch=2,
            grid=(nb,),
            in_specs=[pl.BlockSpec((MOE_ROWS * ROW_TILES, LANE), lambda i, be, nu: (i, 0)),
                      pl.BlockSpec((1, d, de), lambda i, be, nu: (be[i], 0, 0)),
                      pl.BlockSpec((1, d, de), lambda i, be, nu: (be[i], 0, 0)),
                      pl.BlockSpec((1, de, d), lambda i, be, nu: (be[i], 0, 0))],
            out_specs=pl.BlockSpec((MOE_ROWS * ROW_TILES, LANE), lambda i, be, nu: (i, 0)),
            scratch_shapes=[pltpu.VMEM((d, de), BF16), pltpu.VMEM((d, de), BF16), pltpu.VMEM((de, d), BF16)]),
        out_shape=jax.ShapeDtypeStruct(x_pad.shape, F32),
        compiler_params=_cparams(("arbitrary",)),
        name="moe_experts",
    )(block_expert, n_used, x_pad, w1, w3, w2)


def _combine_kernel(dest0_ref, dest1_ref, yp_ref, rec_ref, x1_ref, mod_ref, g_ref, b_ref, o_ref,
                    buf_ref, sem, *, tm):
    step = pl.program_id(0)
    slot = step % 2

    def fetch(tile, to_slot):
        def issue(r, carry):
            for k, dest_ref in enumerate((dest0_ref, dest1_ref)):
                d = dest_ref[tile * tm + r]
                _row_copy(yp_ref.at[_tile_of(d), :], buf_ref.at[to_slot, k, _tile_of(r), :],
                          sem.at[to_slot]).start()
            return carry

        lax.fori_loop(0, tm, issue, 0, unroll=ROW_DMA_UNROLL)

    @pl.when(step == 0)
    def _():
        fetch(0, 0)

    @pl.when(step + 1 < pl.num_programs(0))
    def _():
        fetch(step + 1, 1 - slot)

    for k in range(2):
        _row_copy(yp_ref.at[pl.ds(0, tm * ROW_TILES), :], buf_ref.at[slot, k], sem.at[slot]).wait()

    rec = rec_ref[...]
    y = rec[:, 2:3] * _load_rows(buf_ref, tm, (slot, 0)) + rec[:, 3:4] * _load_rows(buf_ref, tm, (slot, 1))
    y = (1.0 + mod_ref[0, 5:6, :]) * y
    o_ref[...] = _layer_norm(ALPHA * x1_ref[...] + y) * g_ref[...] + b_ref[...]


def _combine(dest0, dest1, y_pad, rec, x1, mod, ln_g, ln_b, seq):
    nt, d = x1.shape
    tm = min(MOE_TOKEN_TILE, seq)
    return pl.pallas_call(
        functools.partial(_combine_kernel, tm=tm),
        grid_spec=pltpu.PrefetchScalarGridSpec(
            num_scalar_prefetch=2,
            grid=(nt // tm,),
            in_specs=[pl.BlockSpec(memory_space=pl.ANY),
                      pl.BlockSpec((tm, LANE), lambda i, *_: (i, 0)),
                      pl.BlockSpec((tm, d), lambda i, *_: (i, 0)),
                      pl.BlockSpec((1, 6, d), lambda i, *_: (i * tm // seq, 0, 0)),
                      pl.BlockSpec((1, d), lambda i, *_: (0, 0)),
                      pl.BlockSpec((1, d), lambda i, *_: (0, 0))],
            out_specs=pl.BlockSpec((tm, d), lambda i, *_: (i, 0)),
            scratch_shapes=[pltpu.VMEM((2, 2, tm * ROW_TILES, LANE), F32), pltpu.SemaphoreType.DMA((2,))]),
        out_shape=jax.ShapeDtypeStruct((nt, d), F32),
        compiler_params=_cparams(("arbitrary",)),
        name="moe_combine",
    )(dest0, dest1, y_pad, rec, x1, mod, ln_g, ln_b)


def _rel_bucket(dist):
    n = jnp.maximum(dist, 0)
    max_exact = REL_BUCKETS // 2
    nf = jnp.maximum(n, 1).astype(F32)
    large = max_exact + (jnp.log(nf / max_exact) / math.log(REL_MAX_DIST / max_exact)
                         * (REL_BUCKETS - max_exact)).astype(jnp.int32)
    large = jnp.minimum(large, REL_BUCKETS - 1)
    return jnp.where(n < max_exact, n, large)


def _bias_tables(rel_bias, seq):
    bucket_onehot = (_rel_bucket(jnp.arange(LANE))[:, None] == jnp.arange(REL_BUCKETS)).astype(F32)
    tab_d = jnp.einsum('db,hb->hd', bucket_onehot, rel_bias,
                       precision=lax.Precision.HIGHEST)
    tok = np.arange(QT)[None, :]
    key = np.arange(LANE)[:, None]
    far = tab_d[:, LANE - 1]
    cols = NSA_REP * QT

    def transposed(dist):
        idx = jnp.asarray(np.clip(dist, 0, LANE - 1).astype(np.int32))
        onehot = (idx[..., None] == jnp.arange(LANE, dtype=jnp.int32)).astype(F32)
        t = jnp.einsum('ijd,hd->hij', onehot, tab_d, precision=lax.Precision.HIGHEST)
        t = (t - far[:, None, None]) * LOG2E
        t = t.reshape(NSA_GROUPS, NSA_REP, LANE, QT).transpose(0, 2, 1, 3)
        return t.reshape(NSA_GROUPS, LANE, cols)

    t0t = transposed(tok - key)
    t1t = transposed(tok - key + QT)

    ns = seq // CMP_STRIDE
    d_c = tok - CMP_STRIDE * key + (CMP_STRIDE * CMP_PAD - (CMP_BLOCK - 1))
    seen = np.tile(d_c >= 0, (1, NSA_REP))
    recent = jnp.where(seen[None], transposed(d_c), NEG)
    tt = jnp.concatenate([jnp.zeros((NSA_GROUPS, ns, cols), F32), recent,
                          jnp.full((NSA_GROUPS, ns, cols), NEG, F32)], axis=1)

    rho = np.arange(WINDOW + QT)[:, None]
    tok_w = np.tile(np.arange(QT), NSA_REP)[None, :]
    band = (rho > tok_w) & (rho <= tok_w + WINDOW)
    rows = jnp.concatenate([jnp.zeros((NSA_GROUPS, WINDOW - QT, NSA_REP * QT), F32), t1t, t0t], axis=1)
    wb = jnp.where(band[None], rows, NEG)

    zeros = lambda n: jnp.zeros((NSA_GROUPS, n * QT, cols), F32)
    negs = lambda n: jnp.full((NSA_GROUPS, n * QT, cols), NEG, F32)
    diag = jnp.where(np.tile(tok >= key, (1, NSA_REP))[None], t0t, NEG)
    nd = jnp.stack([jnp.concatenate(([zeros(u - 1), t1t] if u else []) + [diag, negs(SEL_SUB - 1 - u)], axis=1)
                    for u in range(SEL_SUB)], axis=1)
    npv = jnp.concatenate([zeros(SEL_SUB - 1), t1t], axis=1)
    return nd, npv, wb, tt


def _overlap_matrix(seq):
    ns = seq // CMP_STRIDE
    nslc = seq // SLC_BLOCK
    ov = np.zeros((LANE, ns), np.float32)
    cs = np.arange(ns - 1) * CMP_STRIDE
    ss = np.arange(nslc) * SLC_BLOCK
    ov[:nslc, :ns - 1] = ((cs[None, :] < ss[:, None] + SLC_BLOCK) & (cs[None, :] + CMP_BLOCK > ss[:, None]))
    return jnp.asarray(ov, BF16)


def _reorder_cols(a):
    lead = a.shape[:-1]
    gate = a[..., MAIN_COLS:MAIN_COLS + GATE_COLS]
    per = GATE_COLS // NSA_GROUPS
    gate_blocks = []
    for g in range(NSA_GROUPS):
        gate_blocks.append(gate[..., g * per:(g + 1) * per])
        gate_blocks.append(jnp.zeros(lead + (LANE - per,), a.dtype))
    pad = jnp.zeros(lead + ((CB_MGH - CB_GATE - NSA_GROUPS) * LANE,), a.dtype)
    return jnp.concatenate([a[..., :MAIN_COLS]] + gate_blocks + [pad, a[..., MAIN_COLS + GATE_COLS:]], axis=-1)


def kernel(x, c, ada_w, ada_b, w_in, b_in, hg_lb_logits, hg_norm_w, cmp_pos_k, cmp_w1_k, cmp_b1_k, cmp_w2_k, cmp_pos_v, cmp_w1_v, cmp_b1_v, cmp_w2_v, rel_bias, w_br_hg, w_br_nsa, w_out, ln1_g, ln1_b, router_grp_w, router_grp_b, router_exp_w, router_exp_b, exp_w1, exp_w3, exp_w2, ln2_g, ln2_b):
    batch, seq, d = x.shape
    nt = batch * seq
    assert d == D_MODEL and seq % INPROJ_ROWS == 0 and seq // SLC_BLOCK <= LANE
    l = 0
    x2 = x.reshape(nt, d)

    c_pad = jnp.zeros((8, d), F32).at[:batch].set(c)
    mod = _adaln(c_pad, ada_w[l], ada_b[l][None])[:batch].reshape(batch, 6, d)

    proj = _inproj(x2, mod, _reorder_cols(w_in[l].astype(BF16)), _reorder_cols(b_in[l])[None], seq)

    lb_all = jnp.cumsum(jax.nn.softmax(hg_lb_logits.astype(F32), axis=0), axis=0)
    o_h = _hgrn(proj, lb_all[l][None], hg_norm_w[l][None], batch, seq)

    kvc = _compress(proj, jnp.stack([cmp_pos_k[l], cmp_pos_v[l]]),
                    jnp.stack([cmp_w1_k[l], cmp_w1_v[l]]).astype(BF16),
                    jnp.stack([cmp_b1_k[l], cmp_b1_v[l]])[:, None, :],
                    jnp.stack([cmp_w2_k[l], cmp_w2_v[l]]).astype(BF16), batch, seq)

    nd, npv, wb, tt = _bias_tables(rel_bias, seq)
    o_c, sel_t = _cmp_sel(proj, kvc, tt, _overlap_matrix(seq), batch, seq)
    o_w = _window(proj, wb, batch, seq)
    o_n = _selected(proj, sel_t, nd, npv, o_c, o_w, batch, seq)

    w_r = jnp.zeros((d, LANE), F32).at[:, :N_GROUPS].set(router_grp_w[l])
    w_r = w_r.at[:, N_GROUPS:N_GROUPS + N_EXPERTS].set(router_exp_w[l])
    b_r = jnp.zeros((1, LANE), F32).at[0, :N_GROUPS].set(router_grp_b[l])
    b_r = b_r.at[0, N_GROUPS:N_GROUPS + N_EXPERTS].set(router_exp_b[l])
    x1, h2, logits = _merge(o_h, o_n, proj, x2, mod,
                            w_br_hg[l].astype(BF16), w_br_nsa[l].astype(BF16), w_out[l].astype(BF16),
                            ln1_g[l][None], ln1_b[l][None], w_r, b_r, seq)

    rec, cnt, rec_t = _route(logits)
    counts = cnt[0, N_GROUPS:N_GROUPS + N_EXPERTS].astype(jnp.int32)
    padded = (counts + MOE_ROWS - 1) // MOE_ROWS * MOE_ROWS
    pend = jnp.cumsum(padded)
    pstart = pend - padded
    n_assign = 2 * nt
    nb = n_assign // MOE_ROWS + N_EXPERTS
    slots = rec_t.astype(jnp.int32)
    expert_ids = jnp.arange(N_EXPERTS, dtype=jnp.int32)[:, None]
    slot_base = lambda e: jnp.sum(jnp.where(e[None, :] == expert_ids, pstart[:, None], 0), axis=0)
    dest0 = slot_base(slots[0]) + slots[4]
    dest1 = slot_base(slots[1]) + slots[5]
    block_start = jnp.arange(nb, dtype=jnp.int32) * MOE_ROWS
    block_expert = jnp.minimum(jnp.sum(pend[None, :] <= block_start[:, None], axis=1),
                               N_EXPERTS - 1).astype(jnp.int32)
    n_used = pend[-1] // MOE_ROWS
    spare = n_used + jnp.arange(N_EXPERTS, dtype=jnp.int32)
    zero_blocks = jnp.concatenate([jnp.where(padded > 0, pend // MOE_ROWS - 1, -1),
                                   jnp.where(spare < nb, spare, -1)]).astype(jnp.int32)

    x_pad = _dispatch(dest0, dest1, zero_blocks, h2, nb)
    y_pad = _experts(block_expert, n_used[None].astype(jnp.int32), x_pad, exp_w1[l], exp_w3[l], exp_w2[l])
    out = _combine(dest0, dest1, y_pad, rec, x1, mod, ln2_g[l][None], ln2_b[l][None], seq)
    return out.reshape(batch, seq, d)
```

```python
import functools
import math

import numpy as np
import jax
import jax.numpy as jnp
from jax import lax
from jax.experimental import pallas as pl
from jax.experimental.pallas import tpu as pltpu

F32 = jnp.float32
BF16 = jnp.bfloat16

D_MODEL = 1024
HG_HEADS = 8
HG_DK = 128
HG_DV = 128
HG_CHUNK = 32
HG_SUPER = 256
NSA_HEADS = 8
NSA_GROUPS = 2
NSA_REP = NSA_HEADS // NSA_GROUPS
NSA_DK = 128
CMP_BLOCK = 32
CMP_STRIDE = 16
SLC_BLOCK = 64
SLC_TOPK = 16
WINDOW = 512
FORCE_SCORE = 1e4
N_FORCED = 3
REL_BUCKETS = 32
REL_MAX_DIST = 128
N_GROUPS = 4
EXP_PER_GROUP = 8
N_EXPERTS = N_GROUPS * EXP_PER_GROUP
D_EXPERT = D_MODEL // 2
DEPTH = 1
ALPHA = (2 * DEPTH) ** 0.25

LANE = 128
QT = 128
NEG = -1e30
SCALE = NSA_DK ** -0.5
LOG2E = math.log2(math.e)
SCALE_LOG2 = SCALE * LOG2E
KEY_BLK = 512
BOUND_SLACK = 1.0 + 2.0 ** -10
MAX_REF_GAP = 64.0
SEL_ROWS = 16
SEL_SUB = KEY_BLK // QT
WIN_SUB = 4
CMP_SUB = 4
CMP_PAD = 120
VMEM_LIMIT = 56 * 1024 * 1024

CB_HQ, CB_HF, CB_HI, CB_HG = 0, 8, 16, 24
CB_NQ = 32
CB_KC, CB_VC, CB_KS, CB_VS, CB_KW, CB_VW = 40, 42, 44, 46, 48, 50
CB_GATE = 52
CB_MGH, CB_MGN = 56, 64
NCB = 72
MAIN_COLS = 52 * LANE
GATE_COLS = 3 * NSA_HEADS

INPROJ_ROWS = 2048
INPROJ_COL_BLOCKS = 8
HGRN_ROWS = 512
MERGE_ROWS = 512
ROUTE_ROWS = 512
MOE_TOKEN_TILE = 256
MERGE_SPLIT = 2
ROW_TILES = D_MODEL // LANE
MOE_ROWS = 512
ROW_DMA_UNROLL = 8


def _cparams(sem):
    return pltpu.CompilerParams(dimension_semantics=sem, vmem_limit_bytes=VMEM_LIMIT)


def _dot(a, b):
    return jnp.dot(a, b, preferred_element_type=F32)


def _dot_nt(a, b):
    return lax.dot_general(a, b, (((1,), (1,)), ((), ())), preferred_element_type=F32)


def _dot_tn(a, b):
    return lax.dot_general(a, b, (((0,), (0,)), ((), ())), preferred_element_type=F32)


def _split3(x):
    hi = x.astype(BF16)
    r = x - hi.astype(F32)
    mid = r.astype(BF16)
    lo = (r - mid.astype(F32)).astype(BF16)
    return hi, mid, lo


def _dot01(m01, x):
    hi, mid, lo = _split3(x)
    return _dot(m01, hi) + _dot(m01, mid) + _dot(m01, lo)


def _layer_norm(x, eps=1e-5):
    mu = jnp.mean(x, axis=-1, keepdims=True)
    xc = x - mu
    var = jnp.mean(xc * xc, axis=-1, keepdims=True)
    return xc * lax.rsqrt(var + eps)


def _adaln_kernel(c_ref, w_ref, b_ref, o_ref):
    c = c_ref[...]
    ca = c * jax.nn.sigmoid(c)
    o_ref[...] = jnp.dot(ca, w_ref[...], precision=lax.Precision.HIGHEST,
                         preferred_element_type=F32) + b_ref[...]


def _adaln(c_pad, w, b):
    rows, d = c_pad.shape
    n = w.shape[1]
    return pl.pallas_call(
        _adaln_kernel,
        grid=(n // d,),
        in_specs=[pl.BlockSpec((rows, d), lambda j: (0, 0)),
                  pl.BlockSpec((d, d), lambda j: (0, j)),
                  pl.BlockSpec((1, d), lambda j: (0, j))],
        out_specs=pl.BlockSpec((rows, d), lambda j: (0, j)),
        out_shape=jax.ShapeDtypeStruct((rows, n), F32),
        compiler_params=_cparams(("arbitrary",)),
        name="adaln",
    )(c_pad, w, b)


def _inproj_kernel(x_ref, mod_ref, w_ref, b_ref, o_ref, hn_ref, *, ncb_tile):
    @pl.when(pl.program_id(1) == 0)
    def _():
        hn = _layer_norm(x_ref[...])
        sh = mod_ref[0, 0:1, :]
        sc = mod_ref[0, 1:2, :]
        hn_ref[...] = (hn * (1.0 + sc) + sh).astype(BF16)

    res = _dot(hn_ref[...], w_ref[...]) + b_ref[...]
    for c in range(ncb_tile):
        o_ref[c] = res[:, c * LANE:(c + 1) * LANE]


def _inproj(x2, mod, w, b, seq):
    nt, d = x2.shape
    tm = min(INPROJ_ROWS, seq)
    ncb_tile = INPROJ_COL_BLOCKS
    tn = ncb_tile * LANE
    return pl.pallas_call(
        functools.partial(_inproj_kernel, ncb_tile=ncb_tile),
        grid=(nt // tm, NCB // ncb_tile),
        in_specs=[pl.BlockSpec((tm, d), lambda i, j: (i, 0)),
                  pl.BlockSpec((1, 6, d), lambda i, j: (i * tm // seq, 0, 0)),
                  pl.BlockSpec((d, tn), lambda i, j: (0, j)),
                  pl.BlockSpec((1, tn), lambda i, j: (0, j))],
        out_specs=pl.BlockSpec((ncb_tile, tm, LANE), lambda i, j: (j, i, 0)),
        out_shape=jax.ShapeDtypeStruct((NCB, nt, LANE), F32),
        scratch_shapes=[pltpu.VMEM((tm, d), BF16)],
        compiler_params=_cparams(("arbitrary", "arbitrary")),
        name="inproj",
    )(x2, mod, w, b)


def _hgrn_kernel(q_ref, f_ref, v_ref, g_ref, lb_ref, nw_ref, o_ref, st_ref, *, rows):
    @pl.when(pl.program_id(1) == 0)
    def _():
        st_ref[...] = jnp.zeros_like(st_ref)

    sup = HG_SUPER
    ri = lax.broadcasted_iota(jnp.int32, (sup, sup), 0)
    ci = lax.broadcasted_iota(jnp.int32, (sup, sup), 1)
    same = (ri // HG_CHUNK) == (ci // HG_CHUNK)
    cum_m = jnp.where(same & (ci <= ri), 1.0, 0.0).astype(BF16)
    rt = lax.broadcasted_iota(jnp.int32, (LANE, LANE), 0)
    ct = lax.broadcasted_iota(jnp.int32, (LANE, LANE), 1)
    tril = ((rt // HG_CHUNK) == (ct // HG_CHUNK)) & (ct <= rt)
    per = sup // HG_CHUNK
    groups = [slice(g * LANE, (g + 1) * LANE) for g in range(sup // LANE)]

    heads = range(HG_HEADS)
    hs = [slice(h * LANE, (h + 1) * LANE) for h in heads]

    def wide(ref, r0):
        return jnp.concatenate([ref[h, pl.ds(r0, sup), :] for h in heads], axis=1)

    def body(i, carry):
        r0 = pl.multiple_of(i * sup, sup)
        lb = lb_ref[...]
        f = lb + (1.0 - lb) * jax.nn.sigmoid(wide(f_ref, r0))
        lf = jnp.log(f)
        k = 1.0 - f
        hi = lf.astype(BF16)
        lo = (lf - hi.astype(F32)).astype(BF16)
        b = _dot(cum_m, hi) + _dot(cum_m, lo)
        chunk = [slice(c * HG_CHUNK, (c + 1) * HG_CHUNK) for c in range(per)]
        dec = [jnp.exp(b[(c + 1) * HG_CHUNK - 1:(c + 1) * HG_CHUNK, :]) for c in range(per)]
        dec_rows = jnp.concatenate([jnp.broadcast_to(dec[c], (HG_CHUNK, dec[c].shape[1])) for c in range(per)],
                                   axis=0)
        q_in = (wide(q_ref, r0) * jnp.exp(b)).astype(BF16)
        k_dec = k * jnp.exp(-b)
        k_in = k_dec.astype(BF16)
        k_end = (k_dec * dec_rows).astype(BF16)
        vb = wide(v_ref, r0).astype(BF16)

        att = [[jnp.where(tril, _dot_nt(q_in[g, hs[h]], k_in[g, hs[h]]), 0.0).astype(BF16) for g in groups]
               for h in heads]
        upd = [[_dot_tn(vb[chunk[c], hs[h]], k_end[chunk[c], hs[h]]) for c in range(per)] for h in heads]
        intra = [[_dot(att[h][n], vb[g, hs[h]]) for n, g in enumerate(groups)] for h in heads]

        st = [st_ref[h] for h in heads]
        inter = [[] for _ in heads]
        for c in range(per):
            for h in heads:
                inter[h].append(_dot_nt(q_in[chunk[c], hs[h]], st[h].astype(BF16)))
                st[h] = dec[c][:, hs[h]] * st[h] + upd[h][c]
        for h in heads:
            st_ref[h] = st[h]

        o = [jnp.concatenate(intra[h], axis=0) + jnp.concatenate(inter[h], axis=0) for h in heads]
        scale = [lax.rsqrt(jnp.mean(o[h] * o[h], axis=-1, keepdims=True) + 1e-6) for h in heads]
        on = jnp.concatenate([o[h] * scale[h] for h in heads], axis=1)
        on = on * nw_ref[...] * jax.nn.sigmoid(wide(g_ref, r0))
        o_ref[pl.ds(r0, sup), :] = on.astype(BF16)
        return carry

    lax.fori_loop(0, rows // sup, body, 0)


def _hgrn(proj, lb, nw, batch, seq):
    nt = proj.shape[1]
    tb = min(HGRN_ROWS, seq)
    nblk = seq // tb

    def slab(cb0):
        return pl.BlockSpec((HG_HEADS, tb, LANE), lambda b, t: (cb0 // HG_HEADS, b * nblk + t, 0))

    vec = pl.BlockSpec((1, HG_HEADS * LANE), lambda b, t: (0, 0))
    return pl.pallas_call(
        functools.partial(_hgrn_kernel, rows=tb),
        grid=(batch, nblk),
        in_specs=[slab(CB_HQ), slab(CB_HF), slab(CB_HI), slab(CB_HG), vec, vec],
        out_specs=pl.BlockSpec((tb, HG_HEADS * HG_DV), lambda b, t: (b * nblk + t, 0)),
        out_shape=jax.ShapeDtypeStruct((nt, HG_HEADS * HG_DV), BF16),
        scratch_shapes=[pltpu.VMEM((HG_HEADS, HG_DV, HG_DK), F32)],
        compiler_params=_cparams(("arbitrary", "arbitrary")),
        name="hgrn2",
    )(proj, proj, proj, proj, lb, nw)


def _compress_kernel(x_ref, pos_ref, w1_ref, b1_ref, w2_ref, o_ref, *, ns):
    p0 = jnp.zeros((ns, LANE), F32)
    p1 = jnp.zeros((ns, LANE), F32)
    for j in range(CMP_STRIDE):
        tok = x_ref[0, pl.ds(j, ns, stride=CMP_STRIDE), :]
        rows = slice(j * LANE, (j + 1) * LANE)
        late = slice((CMP_STRIDE + j) * LANE, (CMP_STRIDE + j + 1) * LANE)
        p0 = p0 + _dot((tok + pos_ref[0, j:j + 1, :]).astype(BF16), w1_ref[0, rows, :])
        p1 = p1 + _dot((tok + pos_ref[0, CMP_STRIDE + j:CMP_STRIDE + j + 1, :]).astype(BF16), w1_ref[0, late, :])
    h = p0 + pltpu.roll(p1, ns - 1, axis=0) + b1_ref[0]
    a = h * jax.nn.sigmoid(h)
    out = _dot(a.astype(BF16), w2_ref[0])
    row = lax.broadcasted_iota(jnp.int32, out.shape, 0)
    out = jnp.where(row < ns - 1, out, 0.0)
    o_ref[0, 0, 0] = out


def _compress(proj, pos, w1, b1, w2, batch, seq):
    ns = seq // CMP_STRIDE
    np_rows = ns
    width = CMP_STRIDE * LANE
    return pl.pallas_call(
        functools.partial(_compress_kernel, ns=ns),
        grid=(2, batch, NSA_GROUPS),
        in_specs=[pl.BlockSpec((1, seq, LANE), lambda s, b, g: (CB_KC + NSA_GROUPS * s + g, b, 0)),
                  pl.BlockSpec((1, CMP_BLOCK, LANE), lambda s, b, g: (s, 0, 0)),
                  pl.BlockSpec((1, 2 * width, LANE), lambda s, b, g: (s, 0, 0)),
                  pl.BlockSpec((1, 1, LANE), lambda s, b, g: (s, 0, 0)),
                  pl.BlockSpec((1, LANE, LANE), lambda s, b, g: (s, 0, 0))],
        out_specs=pl.BlockSpec((1, 1, 1, np_rows, LANE), lambda s, b, g: (s, b, g, 0, 0)),
        out_shape=jax.ShapeDtypeStruct((2, batch, NSA_GROUPS, np_rows, LANE), F32),
        compiler_params=_cparams(("arbitrary", "arbitrary", "arbitrary")),
        name="nsa_compress",
    )(proj, pos, w1, b1, w2)


def _cmp_sel_kernel(q0, q1, q2, q3, kc_ref, vc_ref, gate_ref, tt_ref, ov_ref, oc_ref, sel_ref,
                    kb_ref, vt_ref, imp_ref, *, np_rows):
    step = pl.program_id(2)
    cols = NSA_REP * QT
    ns = np_rows

    @pl.when(step == 0)
    def _():
        kb_ref[...] = (kc_ref[0, 0, 0] * SCALE_LOG2).astype(BF16)
        for c in range(np_rows // LANE):
            vt_ref[:, c * LANE:(c + 1) * LANE] = vc_ref[0, 0, 0, c * LANE:(c + 1) * LANE, :].T.astype(BF16)

    subs = range(CMP_SUB)
    tis = [step * CMP_SUB + u for u in subs]
    qts = [_q_transposed((q0, q1, q2, q3), u) for u in subs]

    def group_sum(p):
        tot = p[:, 0:QT]
        for r in range(1, NSA_REP):
            tot = tot + p[:, r * QT:(r + 1) * QT]
        return tot

    def attend(nrows):
        kb = kb_ref[0:nrows, :]
        bias = [tt_ref[0, pl.ds(pl.multiple_of(ns + CMP_PAD - tis[u] * (QT // CMP_STRIDE), 8), nrows), :]
                for u in subs]
        ss = [_dot(kb, qts[u]) + bias[u] for u in subs]
        ms = [jnp.max(ss[u], axis=0, keepdims=True) for u in subs]
        ps = [jnp.exp2(ss[u] - ms[u]) for u in subs]
        ls = [jnp.sum(ps[u], axis=0, keepdims=True) for u in subs]
        invs = [jnp.where(ms[u] > 0.5 * NEG, 1.0 / ls[u], 0.0) for u in subs]
        pn = [ps[u] * invs[u] for u in subs]
        vt = vt_ref[:, 0:nrows]
        os_ = [_dot(vt, pn[u].astype(BF16)) for u in subs]
        for u in subs:
            rows = slice(u * QT, (u + 1) * QT)
            gt = jax.nn.sigmoid(gate_ref[0, rows, :])
            for r in range(NSA_REP):
                oc_ref[rows, r * LANE:(r + 1) * LANE] = gt[:, 3 * r:3 * r + 1] * os_[u][:, r * QT:(r + 1) * QT].T
            imp_ref[:, rows] = _dot01(ov_ref[:, 0:nrows], group_sum(pn[u]))

    visible = (step + 1) * CMP_SUB * (QT // CMP_STRIDE)
    ngroups = np_rows // LANE
    for g in range(1, ngroups + 1):
        upper = visible <= g * LANE if g < ngroups else True

        @pl.when((visible > (g - 1) * LANE) & upper)
        def _():
            attend(g * LANE)

    imp = imp_ref[...]
    width = CMP_SUB * QT
    jj = lax.broadcasted_iota(jnp.int32, (LANE, width), 0)
    tok = step * width + lax.broadcasted_iota(jnp.int32, (LANE, width), 1)
    cur = tok // SLC_BLOCK
    forced = (jj == 0) | (jj == cur) | (jj == cur - 1)
    assert FORCE_SCORE > NSA_REP
    score = jnp.where(forced, -jnp.inf, jnp.where(jj <= cur, imp, -1.0))
    selb = jnp.where(forced, 0.0, NEG)
    jf = jj.astype(F32)
    for _ in range(SLC_TOPK - N_FORCED):
        mval = jnp.max(score, axis=0, keepdims=True)
        first = jnp.min(jnp.where(score == mval, jf, float(LANE)), axis=0, keepdims=True)
        pick = jf == first
        selb = jnp.where(pick, 0.0, selb)
        score = jnp.where(pick, -jnp.inf, score)
    sel_ref[0, 0] = selb


def _q_specs(nsteps, rows=QT):
    return [pl.BlockSpec((1, rows, LANE),
                         functools.partial(lambda b, g, t, r: (CB_NQ + NSA_REP * g + r, b * nsteps + t, 0), r=r))
            for r in range(NSA_REP)]


def _cmp_sel(proj, kvc, tt, ov_t, batch, seq):
    nt = proj.shape[1]
    rows = CMP_SUB * QT
    nsteps = seq // rows
    cols = NSA_REP * QT
    np_rows = kvc.shape[3]
    kv_spec = lambda s: pl.BlockSpec((1, 1, 1, np_rows, LANE), lambda b, g, t: (s, b, g, 0, 0))
    return pl.pallas_call(
        functools.partial(_cmp_sel_kernel, np_rows=np_rows),
        grid=(batch, NSA_GROUPS, nsteps),
        in_specs=_q_specs(nsteps, rows) + [
            kv_spec(0), kv_spec(1),
            pl.BlockSpec((1, rows, LANE), lambda b, g, t: (CB_GATE + g, b * nsteps + t, 0)),
            pl.BlockSpec((1, tt.shape[1], cols), lambda b, g, t: (g, 0, 0)),
            pl.BlockSpec((LANE, np_rows), lambda b, g, t: (0, 0))],
        out_specs=[pl.BlockSpec((rows, cols), lambda b, g, t: (b * nsteps + t, g)),
                   pl.BlockSpec((1, 1, LANE, rows), lambda b, g, t: (b, g, 0, t))],
        out_shape=[jax.ShapeDtypeStruct((nt, NSA_HEADS * LANE), F32),
                   jax.ShapeDtypeStruct((batch, NSA_GROUPS, LANE, seq), F32)],
        scratch_shapes=[pltpu.VMEM((np_rows, LANE), BF16), pltpu.VMEM((LANE, np_rows), BF16),
                        pltpu.VMEM((LANE, rows), F32)],
        compiler_params=_cparams(("arbitrary", "arbitrary", "arbitrary")),
        name="nsa_cmp_select",
    )(proj, proj, proj, proj, kvc, kvc, proj, tt, ov_t)


def _q_transposed(q_refs, sub=0):
    return jnp.concatenate([r[0, sub * QT:(sub + 1) * QT, :].T for r in q_refs], axis=1).astype(BF16)


def _finish_t(state, gate_ref, branch, o_ref, sub=0, others=()):
    _, l, acc = state
    o = acc / jnp.where(l == 0.0, 1.0, l)
    rows = slice(sub * QT, (sub + 1) * QT)
    gt = jax.nn.sigmoid(gate_ref[0, rows, :])
    for r in range(NSA_REP):
        col = 3 * r + branch
        lanes = slice(r * LANE, (r + 1) * LANE)
        val = gt[:, col:col + 1] * o[:, r * QT:(r + 1) * QT].T
        for other in others:
            val = val + other[rows, lanes]
        o_ref[rows, lanes] = val.astype(o_ref.dtype)


def _init_state(cols):
    return (jnp.full((1, cols), NEG, F32), jnp.zeros((1, cols), F32), jnp.zeros((LANE, cols), F32))


def _win_kernel(q0, q1, q2, q3, k_ref, v_ref, gate_ref, wb_ref, o_ref, kb_ref, vt_ref, *, seq):
    step = pl.program_id(2)
    cols = NSA_REP * QT
    wt = WINDOW // QT

    lane = lax.broadcasted_iota(jnp.int32, (QT, LANE), 1)
    pad_mark = jnp.where(lane == 0, 1.0, 0.0).astype(BF16)

    @pl.when(step == 0)
    def _():
        for i in range(wt):
            kb_ref[i * QT:(i + 1) * QT, 0:LANE] = jnp.zeros((QT, LANE), BF16)
            kb_ref[i * QT:(i + 1) * QT, LANE:2 * LANE] = pad_mark
            vt_ref[i] = jnp.zeros((LANE, QT), BF16)

        def fill(i, carry):
            r0 = pl.multiple_of(i * QT, QT)
            kb_ref[pl.ds(WINDOW + r0, QT), 0:LANE] = (k_ref[0, pl.ds(r0, QT), :] * SCALE_LOG2).astype(BF16)
            kb_ref[pl.ds(WINDOW + r0, QT), LANE:2 * LANE] = jnp.zeros((QT, LANE), BF16)
            vt_ref[wt + i] = v_ref[0, pl.ds(r0, QT), :].T.astype(BF16)
            return carry

        lax.fori_loop(0, seq // QT, fill, 0)

    subs = range(WIN_SUB)
    tis = [step * WIN_SUB + sub for sub in subs]
    row = lax.broadcasted_iota(jnp.int32, (LANE, cols), 0)
    pad_rows = jnp.where(row == 0, NEG, 0.0).astype(BF16)
    qts = [jnp.concatenate([_q_transposed((q0, q1, q2, q3), sub), pad_rows], axis=0) for sub in subs]
    ss = [_dot(kb_ref[pl.ds(pl.multiple_of(tis[u] * QT, QT), WINDOW + QT), :], qts[u]) + wb_ref[0]
          for u in subs]
    ms = [jnp.max(ss[u], axis=0, keepdims=True) for u in subs]
    ps = [jnp.exp2(ss[u] - ms[u]) for u in subs]
    ls = [jnp.sum(ps[u], axis=0, keepdims=True) for u in subs]
    pbs = [ps[u].astype(BF16) for u in subs]
    accs = [_dot(vt_ref[tis[u]], pbs[u][0:QT]) for u in subs]
    for d in range(1, wt + 1):
        accs = [accs[u] + _dot(vt_ref[tis[u] + d], pbs[u][d * QT:(d + 1) * QT]) for u in subs]
    for u in subs:
        _finish_t((ms[u], ls[u], accs[u]), gate_ref, 2, o_ref, u)


def _window(proj, wb, batch, seq):
    nt = proj.shape[1]
    rows = WIN_SUB * QT
    nqt = seq // rows
    cols = NSA_REP * QT
    slab = lambda cb0: pl.BlockSpec((1, seq, LANE), lambda b, g, t: (cb0 + g, b, 0))
    return pl.pallas_call(
        functools.partial(_win_kernel, seq=seq),
        grid=(batch, NSA_GROUPS, nqt),
        in_specs=_q_specs(nqt, rows) + [
            slab(CB_KW), slab(CB_VW),
            pl.BlockSpec((1, rows, LANE), lambda b, g, t: (CB_GATE + g, b * nqt + t, 0)),
            pl.BlockSpec((1, WINDOW + QT, cols), lambda b, g, t: (g, 0, 0))],
        out_specs=pl.BlockSpec((rows, cols), lambda b, g, t: (b * nqt + t, g)),
        out_shape=jax.ShapeDtypeStruct((nt, NSA_HEADS * LANE), F32),
        scratch_shapes=[pltpu.VMEM((seq + WINDOW, 2 * LANE), BF16),
                        pltpu.VMEM((seq // QT + WINDOW // QT, LANE, QT), BF16)],
        compiler_params=_cparams(("arbitrary", "arbitrary", "arbitrary")),
        name="nsa_window",
    )(proj, proj, proj, proj, proj, proj, proj, wb)


def _softmax_steps(ss, vt, states):
    n = range(len(ss))
    m_new = [jnp.maximum(states[u][0], jnp.max(ss[u], axis=0, keepdims=True)) for u in n]
    alpha = [jnp.exp2(states[u][0] - m_new[u]) for u in n]
    p = [jnp.exp2(ss[u] - m_new[u]) for u in n]
    l = [alpha[u] * states[u][1] + jnp.sum(p[u], axis=0, keepdims=True) for u in n]
    vts = vt if isinstance(vt, (list, tuple)) else [vt] * len(ss)
    pv = [_dot(vts[u], p[u].astype(BF16)) for u in n]
    return [(m_new[u], l[u], alpha[u] * states[u][2] + pv[u]) for u in n]


def _sel_kernel(q0, q1, q2, q3, k_ref, v_ref, sel_ref, gate_ref, nd_ref, npv_ref, oc_ref, ow_ref, o_ref,
                kb_ref, vt_ref, qa_ref, kn_ref, m_ref, l_ref, acc_ref, *, seq):
    step = pl.program_id(2)
    cols = NSA_REP * QT
    spb = KEY_BLK // SLC_BLOCK
    subs = range(SEL_SUB)

    @pl.when(step == 0)
    def _():
        blk = lax.broadcasted_iota(jnp.int32, (KEY_BLK, LANE), 0) // SLC_BLOCK
        onehot = jnp.where(blk == lax.broadcasted_iota(jnp.int32, (KEY_BLK, LANE), 1), 1.0, 0.0).astype(BF16)

        def fill(i, kmax2):
            r0 = pl.multiple_of(i * KEY_BLK, KEY_BLK)
            kblk = (k_ref[0, pl.ds(r0, KEY_BLK), :] * SCALE_LOG2).astype(BF16)
            kb_ref[pl.ds(r0, KEY_BLK), 0:LANE] = kblk
            kb_ref[pl.ds(r0, KEY_BLK), LANE:2 * LANE] = onehot
            for c in range(SEL_SUB):
                vt_ref[i, :, c * QT:(c + 1) * QT] = v_ref[0, pl.ds(r0 + c * QT, QT), :].T.astype(BF16)
            n2 = jnp.sum(jnp.square(kblk.astype(F32)), axis=1, keepdims=True)
            return jnp.maximum(kmax2, jnp.max(n2, axis=0, keepdims=True))

        kmax2 = lax.fori_loop(0, seq // KEY_BLK, fill, jnp.zeros((1, 1), F32))
        kn_ref[...] = jnp.sqrt(kmax2)
        qa_ref[...] = jnp.zeros_like(qa_ref)

    qts = [_q_transposed((q0, q1, q2, q3), u) for u in subs]
    for u in subs:
        qa_ref[u, 0:LANE, :] = qts[u]
    bounds = [jnp.sqrt(jnp.sum(jnp.square(qts[u].astype(F32)), axis=0, keepdims=True)) * kn_ref[...]
              * BOUND_SLACK + 1.0 for u in subs]

    def block(kb):
        r0 = pl.multiple_of(kb * KEY_BLK, KEY_BLK)
        return kb_ref[pl.ds(r0, KEY_BLK), :], vt_ref[kb]

    pad_rows = jnp.zeros((SEL_ROWS - spb, cols), F32)

    def scores(kb, k, rows_of=lambda u: KEY_BLK):
        for u in subs:
            rows = sel_ref[0, 0, pl.ds(pl.multiple_of(kb * spb, spb), spb), u * QT:(u + 1) * QT]
            rows = jnp.concatenate([rows] * NSA_REP, axis=1)
            qa_ref[u, LANE:LANE + SEL_ROWS, :] = jnp.concatenate([rows, pad_rows], axis=0).astype(BF16)
        return [_dot(k[0:rows_of(u)], qa_ref[u]) for u in subs]

    seen = lambda u: (u + 1) * QT
    k, vt = block(step)
    ss = scores(step, k, seen)
    ss = [ss[u] + nd_ref[0, u, 0:seen(u), :] for u in subs]
    states = _softmax_steps(ss, [vt[:, 0:seen(u)] for u in subs], [_init_state(cols) for _ in subs])

    for u in subs:
        m_ref[u], l_ref[u], acc_ref[u] = states[u]

    def prev_scores():
        k, vt = block(step - 1)
        ss = scores(step - 1, k)
        ss[0] = ss[0] + npv_ref[0]
        return ss, vt

    n_old = jnp.maximum(step - 1, 0)
    bounds[0] = bounds[0] + jnp.max(npv_ref[0], axis=0, keepdims=True)
    m_fix = [jnp.maximum(states[u][0], bounds[u]) for u in subs]
    gap = jnp.max(jnp.concatenate([m_fix[u] - states[u][0] for u in subs], axis=1))
    fixed_ok = gap < MAX_REF_GAP

    @pl.when(fixed_ok)
    def _():
        for u in subs:
            alpha = jnp.exp2(m_ref[u] - m_fix[u])
            l_ref[u] = alpha * l_ref[u]
            acc_ref[u] = alpha * acc_ref[u]
            m_ref[u] = m_fix[u]

        def fixed_step(ss, vt):
            ps = [jnp.exp2(ss[u] - m_ref[u]) for u in subs]
            for u in subs:
                l_ref[u] = l_ref[u] + jnp.sum(ps[u], axis=0, keepdims=True)
            pv = [_dot(vt, ps[u].astype(BF16)) for u in subs]
            for u in subs:
                acc_ref[u] = acc_ref[u] + pv[u]

        @pl.when(step >= 1)
        def _():
            fixed_step(*prev_scores())

        def body(kb, carry):
            k, vt = block(kb)
            fixed_step(scores(kb, k), vt)
            return carry

        lax.fori_loop(0, n_old, body, 0)

    @pl.when(jnp.logical_not(fixed_ok))
    def _():
        def online_step(ss, vt):
            st = _softmax_steps(ss, vt, [(m_ref[u], l_ref[u], acc_ref[u]) for u in subs])
            for u in subs:
                m_ref[u], l_ref[u], acc_ref[u] = st[u]

        @pl.when(step >= 1)
        def _():
            online_step(*prev_scores())

        def body(kb, carry):
            k, vt = block(kb)
            online_step(scores(kb, k), vt)
            return carry

        lax.fori_loop(0, n_old, body, 0)

    for u in subs:
        _finish_t((m_ref[u], l_ref[u], acc_ref[u]), gate_ref, 1, o_ref, u, others=(oc_ref, ow_ref))


def _selected(proj, sel_t, nd, npv, o_c, o_w, batch, seq):
    nt = proj.shape[1]
    rows = SEL_SUB * QT
    nsteps = seq // rows
    cols = NSA_REP * QT
    slab = lambda cb0: pl.BlockSpec((1, seq, LANE), lambda b, g, t: (cb0 + g, b, 0))
    return pl.pallas_call(
        functools.partial(_sel_kernel, seq=seq),
        grid=(batch, NSA_GROUPS, nsteps),
        in_specs=_q_specs(nsteps, rows) + [
            slab(CB_KS), slab(CB_VS),
            pl.BlockSpec((1, 1, LANE, rows), lambda b, g, t: (b, g, 0, t)),
            pl.BlockSpec((1, rows, LANE), lambda b, g, t: (CB_GATE + g, b * nsteps + t, 0)),
            pl.BlockSpec((1, SEL_SUB, KEY_BLK, cols), lambda b, g, t: (g, 0, 0, 0)),
            pl.BlockSpec((1, KEY_BLK, cols), lambda b, g, t: (g, 0, 0)),
            pl.BlockSpec((rows, cols), lambda b, g, t: (b * nsteps + t, g)),
            pl.BlockSpec((rows, cols), lambda b, g, t: (b * nsteps + t, g))],
        out_specs=pl.BlockSpec((rows, cols), lambda b, g, t: (b * nsteps + t, g)),
        out_shape=jax.ShapeDtypeStruct((nt, NSA_HEADS * LANE), BF16),
        scratch_shapes=[pltpu.VMEM((seq, 2 * LANE), BF16),
                        pltpu.VMEM((seq // KEY_BLK, LANE, KEY_BLK), BF16),
                        pltpu.VMEM((SEL_SUB, 2 * LANE, cols), BF16),
                        pltpu.VMEM((1, 1), F32),
                        pltpu.VMEM((SEL_SUB, 1, cols), F32), pltpu.VMEM((SEL_SUB, 1, cols), F32),
                        pltpu.VMEM((SEL_SUB, LANE, cols), F32)],
        compiler_params=_cparams(("arbitrary", "arbitrary", "arbitrary")),
        name="nsa_selected",
    )(proj, proj, proj, proj, proj, proj, sel_t, proj, nd, npv, o_c, o_w)


def _merge_kernel(oh_ref, on_ref, mgh_ref, mgn_ref, x_ref, mod_ref,
                  wh_ref, wn_ref, wo_ref, g_ref, b_ref, wr_ref, br_ref,
                  x1_ref, h2_ref, lg_ref):
    nblk = D_MODEL // LANE
    tm = x_ref.shape[0]
    halves = [slice(s * tm // MERGE_SPLIT, (s + 1) * tm // MERGE_SPLIT) for s in range(MERGE_SPLIT)]
    a_h = [_dot(oh_ref[rs, :], wh_ref[...]) for rs in halves]
    a_n = [_dot(on_ref[rs, :], wn_ref[...]) for rs in halves]
    gh = [jnp.concatenate([mgh_ref[c, rs, :] for c in range(nblk)], axis=-1) for rs in halves]
    gn = [jnp.concatenate([mgn_ref[c, rs, :] for c in range(nblk)], axis=-1) for rs in halves]
    merged = [(jax.nn.sigmoid(gh[s]) * a_h[s] + jax.nn.sigmoid(gn[s]) * a_n[s]).astype(BF16)
              for s in range(MERGE_SPLIT)]
    y = [(1.0 + mod_ref[0, 2:3, :]) * _dot(merged[s], wo_ref[...]) for s in range(MERGE_SPLIT)]
    x1 = [_layer_norm(ALPHA * x_ref[rs, :] + y[s]) * g_ref[...] + b_ref[...] for s, rs in enumerate(halves)]
    h2 = [_layer_norm(x1[s]) * (1.0 + mod_ref[0, 4:5, :]) + mod_ref[0, 3:4, :] for s in range(MERGE_SPLIT)]
    h_hi = [h2[s].astype(BF16) for s in range(MERGE_SPLIT)]
    h_lo = [(h2[s] - h_hi[s].astype(F32)).astype(BF16) for s in range(MERGE_SPLIT)]
    lg = [_dot(h_hi[s], wr_ref[0]) + _dot(h_hi[s], wr_ref[1]) + _dot(h_lo[s], wr_ref[0]) + br_ref[...]
          for s in range(MERGE_SPLIT)]
    for s, rs in enumerate(halves):
        x1_ref[rs, :] = x1[s]
        _store_rows(h2_ref, h2[s], rs.start)
        lg_ref[rs, :] = lg[s]


def _merge(o_h, o_n, proj, x2, mod, w_h, w_n, w_o, ln_g, ln_b, w_r, b_r, seq):
    nt, d = x2.shape
    tm = min(MERGE_ROWS, seq)
    nblk = d // LANE
    row = lambda w: pl.BlockSpec((tm, w), lambda i: (i, 0))
    full = lambda a: pl.BlockSpec(a.shape, lambda i: (0,) * a.ndim)
    return pl.pallas_call(
        _merge_kernel,
        grid=(nt // tm,),
        in_specs=[row(d), row(d),
                  pl.BlockSpec((nblk, tm, LANE), lambda i: (CB_MGH // nblk, i, 0)),
                  pl.BlockSpec((nblk, tm, LANE), lambda i: (CB_MGN // nblk, i, 0)),
                  row(d),
                  pl.BlockSpec((1, 6, d), lambda i: (i * tm // seq, 0, 0)),
                  full(w_h), full(w_n), full(w_o), full(ln_g), full(ln_b), full(w_r), full(b_r)],
        out_specs=[row(d), pl.BlockSpec((tm * ROW_TILES, LANE), lambda i: (i, 0)), row(LANE)],
        out_shape=[jax.ShapeDtypeStruct((nt, d), F32),
                   jax.ShapeDtypeStruct((nt * ROW_TILES, LANE), F32),
                   jax.ShapeDtypeStruct((nt, LANE), F32)],
        compiler_params=_cparams(("arbitrary",)),
        name="merge_outproj",
    )(o_h, o_n, proj, proj, x2, mod, w_h, w_n, w_o, ln_g, ln_b, w_r, b_r)


def _route_kernel(lg_ref, rec_ref, cnt_ref, rect_ref, carry_ref, *, tm):
    @pl.when(pl.program_id(0) == 0)
    def _():
        carry_ref[...] = jnp.zeros_like(carry_ref)

    lg = lg_ref[...]
    lane = lax.broadcasted_iota(jnp.int32, (tm, LANE), 1).astype(F32)
    far = float(LANE)
    gmask = lane < N_GROUPS
    gl = jnp.where(gmask, lg, -jnp.inf)
    gmax = jnp.max(gl, axis=-1, keepdims=True)
    gsum = jnp.sum(jnp.where(gmask, jnp.exp(gl - gmax), 0.0), axis=-1, keepdims=True)
    grp_p = 1.0 / gsum
    gidx = jnp.min(jnp.where(gl == gmax, lane, far), axis=-1, keepdims=True)
    lo = N_GROUPS + EXP_PER_GROUP * gidx
    emask = (lane >= lo) & (lane < lo + EXP_PER_GROUP)
    el = jnp.where(emask, lg, -jnp.inf)
    m1 = jnp.max(el, axis=-1, keepdims=True)
    i1 = jnp.min(jnp.where(el == m1, lane, far), axis=-1, keepdims=True)
    el2 = jnp.where(lane == i1, -jnp.inf, el)
    m2 = jnp.max(el2, axis=-1, keepdims=True)
    i2 = jnp.min(jnp.where(emask & (lane != i1) & (el2 == m2), lane, far), axis=-1, keepdims=True)
    e = jnp.exp(m2 - m1)
    w0 = grp_p / (1.0 + e)
    w1 = grp_p * e / (1.0 + e)

    oh0 = lane == i1
    oh1 = lane == i2
    f0 = jnp.where(oh0, 1.0, 0.0)
    f1 = jnp.where(oh1, 1.0, 0.0)
    ri = lax.broadcasted_iota(jnp.int32, (tm, tm), 0)
    ci = lax.broadcasted_iota(jnp.int32, (tm, tm), 1)
    before = jnp.where(ci < ri, 1.0, 0.0).astype(BF16)
    cum0 = _dot(before, f0.astype(BF16))
    cum1 = _dot(before, f1.astype(BF16))
    tot0 = jnp.sum(f0, axis=0, keepdims=True)
    tot1 = jnp.sum(f1, axis=0, keepdims=True)
    carry = carry_ref[...]
    rank0 = jnp.sum(jnp.where(oh0, carry + cum0, 0.0), axis=-1, keepdims=True)
    rank1 = jnp.sum(jnp.where(oh1, carry + tot0 + cum1, 0.0), axis=-1, keepdims=True)
    carry = carry + tot0 + tot1
    carry_ref[...] = carry
    cnt_ref[...] = carry

    rec = jnp.where(lane == 0, i1 - N_GROUPS, 0.0)
    rec = jnp.where(lane == 1, i2 - N_GROUPS, rec)
    rec = jnp.where(lane == 2, w0, rec)
    rec = jnp.where(lane == 3, w1, rec)
    rec = jnp.where(lane == 4, rank0, rec)
    rec = jnp.where(lane == 5, rank1, rec)
    rec_ref[...] = rec
    rect_ref[...] = jnp.concatenate([rec[c * LANE:(c + 1) * LANE, :].T[0:8, :] for c in range(tm // LANE)], axis=1)


def _route(logits):
    nt = logits.shape[0]
    tm = min(ROUTE_ROWS, nt)
    return pl.pallas_call(
        functools.partial(_route_kernel, tm=tm),
        grid=(nt // tm,),
        in_specs=[pl.BlockSpec((tm, LANE), lambda i: (i, 0))],
        out_specs=[pl.BlockSpec((tm, LANE), lambda i: (i, 0)),
                   pl.BlockSpec((1, LANE), lambda i: (0, 0)),
                   pl.BlockSpec((8, tm), lambda i: (0, i))],
        out_shape=[jax.ShapeDtypeStruct((nt, LANE), F32),
                   jax.ShapeDtypeStruct((1, LANE), F32),
                   jax.ShapeDtypeStruct((8, nt), F32)],
        scratch_shapes=[pltpu.VMEM((1, LANE), F32)],
        compiler_params=_cparams(("arbitrary",)),
        name="moe_route",
    )(logits)


def _row_copy(src, dst, sem):
    return pltpu.make_async_copy(src, dst, sem)


def _tile_of(r):
    return pl.ds(pl.multiple_of(r * ROW_TILES, ROW_TILES), ROW_TILES)


def _load_rows(ref, n, lead=()):
    return jnp.concatenate([ref[lead + (pl.ds(c, n, stride=ROW_TILES), slice(None))]
                            for c in range(ROW_TILES)], axis=1)


def _store_rows(ref, val, row0=0):
    n = val.shape[0]
    for c in range(ROW_TILES):
        ref[pl.ds(row0 * ROW_TILES + c, n, stride=ROW_TILES), :] = val[:, c * LANE:(c + 1) * LANE]


def _dispatch_kernel(dest0_ref, dest1_ref, zb_ref, h_ref, xp_ref, z_ref, sem, zsem, *, tm):
    step = pl.program_id(0)
    base = step * tm
    blk = MOE_ROWS * ROW_TILES

    @pl.when(step == 0)
    def _():
        z_ref[...] = jnp.zeros_like(z_ref)

        def zero_copy(j):
            b = jnp.maximum(zb_ref[j], 0)
            return _row_copy(z_ref, xp_ref.at[pl.ds(pl.multiple_of(b * blk, blk), blk), :], zsem)

        def start(j, carry):
            @pl.when(zb_ref[j] >= 0)
            def _():
                zero_copy(j).start()
            return carry

        def wait(j, carry):
            @pl.when(zb_ref[j] >= 0)
            def _():
                zero_copy(j).wait()
            return carry

        lax.fori_loop(0, 2 * N_EXPERTS, start, 0)
        lax.fori_loop(0, 2 * N_EXPERTS, wait, 0)

    def issue(r, carry):
        for dest_ref in (dest0_ref, dest1_ref):
            d = dest_ref[base + r]
            _row_copy(h_ref.at[_tile_of(r), :], xp_ref.at[_tile_of(d), :], sem).start()
        return carry

    lax.fori_loop(0, tm, issue, 0, unroll=ROW_DMA_UNROLL)
    for _ in range(2):
        _row_copy(h_ref, xp_ref.at[pl.ds(0, tm * ROW_TILES), :], sem).wait()


def _dispatch(dest0, dest1, zero_blocks, h2, n_blocks):
    nt = h2.shape[0] // ROW_TILES
    tm = min(MOE_TOKEN_TILE, nt)
    return pl.pallas_call(
        functools.partial(_dispatch_kernel, tm=tm),
        grid_spec=pltpu.PrefetchScalarGridSpec(
            num_scalar_prefetch=3,
            grid=(nt // tm,),
            in_specs=[pl.BlockSpec((tm * ROW_TILES, LANE), lambda i, *_: (i, 0))],
            out_specs=pl.BlockSpec(memory_space=pl.ANY),
            scratch_shapes=[pltpu.VMEM((MOE_ROWS * ROW_TILES, LANE), F32),
                            pltpu.SemaphoreType.DMA(()), pltpu.SemaphoreType.DMA(())]),
        out_shape=jax.ShapeDtypeStruct((n_blocks * MOE_ROWS * ROW_TILES, LANE), F32),
        compiler_params=_cparams(("arbitrary",)),
        name="moe_dispatch",
    )(dest0, dest1, zero_blocks, h2)


def _expert_kernel(be_ref, nu_ref, x_ref, w1_ref, w3_ref, w2_ref, y_ref, w1b_ref, w3b_ref, w2b_ref):
    i = pl.program_id(0)

    @pl.when((i < nu_ref[0]) & ((i == 0) | (be_ref[i] != be_ref[jnp.maximum(i - 1, 0)])))
    def _():
        w1b_ref[...] = w1_ref[0].astype(BF16)
        w3b_ref[...] = w3_ref[0].astype(BF16)
        w2b_ref[...] = w2_ref[0].astype(BF16)

    @pl.when(i < nu_ref[0])
    def _():
        xb = _load_rows(x_ref, MOE_ROWS).astype(BF16)
        a = _dot(xb, w1b_ref[...])
        b = _dot(xb, w3b_ref[...])
        hmid = (a * jax.nn.sigmoid(a) * b).astype(BF16)
        _store_rows(y_ref, _dot(hmid, w2b_ref[...]))

    @pl.when(i >= nu_ref[0])
    def _():
        y_ref[...] = jnp.zeros_like(y_ref)


def _experts(block_expert, n_used, x_pad, w1, w3, w2):
    d, de = w1.shape[1], w1.shape[2]
    nb = x_pad.shape[0] // (MOE_ROWS * ROW_TILES)
    return pl.pallas_call(
        _expert_kernel,
        grid_spec=pltpu.PrefetchScalarGridSpec(
            num_scalar_prefetch=2,
            grid=(nb,),
            in_specs=[pl.BlockSpec((MOE_ROWS * ROW_TILES, LANE), lambda i, be, nu: (i, 0)),
                      pl.BlockSpec((1, d, de), lambda i, be, nu: (be[i], 0, 0)),
                      pl.BlockSpec((1, d, de), lambda i, be, nu: (be[i], 0, 0)),
                      pl.BlockSpec((1, de, d), lambda i, be, nu: (be[i], 0, 0))],
            out_specs=pl.BlockSpec((MOE_ROWS * ROW_TILES, LANE), lambda i, be, nu: (i, 0)),
            scratch_shapes=[pltpu.VMEM((d, de), BF16), pltpu.VMEM((d, de), BF16), pltpu.VMEM((de, d), BF16)]),
        out_shape=jax.ShapeDtypeStruct(x_pad.shape, F32),
        compiler_params=_cparams(("arbitrary",)),
        name="moe_experts",
    )(block_expert, n_used, x_pad, w1, w3, w2)


def _combine_kernel(dest0_ref, dest1_ref, yp_ref, rec_ref, x1_ref, mod_ref, g_ref, b_ref, o_ref,
                    buf_ref, sem, *, tm):
    step = pl.program_id(0)
    slot = step % 2

    def fetch(tile, to_slot):
        def issue(r, carry):
            for k, dest_ref in enumerate((dest0_ref, dest1_ref)):
                d = dest_ref[tile * tm + r]
                _row_copy(yp_ref.at[_tile_of(d), :], buf_ref.at[to_slot, k, _tile_of(r), :],
                          sem.at[to_slot]).start()
            return carry

        lax.fori_loop(0, tm, issue, 0, unroll=ROW_DMA_UNROLL)

    @pl.when(step == 0)
    def _():
        fetch(0, 0)

    @pl.when(step + 1 < pl.num_programs(0))
    def _():
        fetch(step + 1, 1 - slot)

    for k in range(2):
        _row_copy(yp_ref.at[pl.ds(0, tm * ROW_TILES), :], buf_ref.at[slot, k], sem.at[slot]).wait()

    rec = rec_ref[...]
    y = rec[:, 2:3] * _load_rows(buf_ref, tm, (slot, 0)) + rec[:, 3:4] * _load_rows(buf_ref, tm, (slot, 1))
    y = (1.0 + mod_ref[0, 5:6, :]) * y
    o_ref[...] = _layer_norm(ALPHA * x1_ref[...] + y) * g_ref[...] + b_ref[...]


def _combine(dest0, dest1, y_pad, rec, x1, mod, ln_g, ln_b, seq):
    nt, d = x1.shape
    tm = min(MOE_TOKEN_TILE, seq)
    return pl.pallas_call(
        functools.partial(_combine_kernel, tm=tm),
        grid_spec=pltpu.PrefetchScalarGridSpec(
            num_scalar_prefetch=2,
            grid=(nt // tm,),
            in_specs=[pl.BlockSpec(memory_space=pl.ANY),
                      pl.BlockSpec((tm, LANE), lambda i, *_: (i, 0)),
                      pl.BlockSpec((tm, d), lambda i, *_: (i, 0)),
                      pl.BlockSpec((1, 6, d), lambda i, *_: (i * tm // seq, 0, 0)),
                      pl.BlockSpec((1, d), lambda i, *_: (0, 0)),
                      pl.BlockSpec((1, d), lambda i, *_: (0, 0))],
            out_specs=pl.BlockSpec((tm, d), lambda i, *_: (i, 0)),
            scratch_shapes=[pltpu.VMEM((2, 2, tm * ROW_TILES, LANE), F32), pltpu.SemaphoreType.DMA((2,))]),
        out_shape=jax.ShapeDtypeStruct((nt, d), F32),
        compiler_params=_cparams(("arbitrary",)),
        name="moe_combine",
    )(dest0, dest1, y_pad, rec, x1, mod, ln_g, ln_b)


def _rel_bucket(dist):
    n = jnp.maximum(dist, 0)
    max_exact = REL_BUCKETS // 2
    nf = jnp.maximum(n, 1).astype(F32)
    large = max_exact + (jnp.log(nf / max_exact) / math.log(REL_MAX_DIST / max_exact)
                         * (REL_BUCKETS - max_exact)).astype(jnp.int32)
    large = jnp.minimum(large, REL_BUCKETS - 1)
    return jnp.where(n < max_exact, n, large)


def _bias_tables(rel_bias, seq):
    bucket_onehot = (_rel_bucket(jnp.arange(LANE))[:, None] == jnp.arange(REL_BUCKETS)).astype(F32)
    tab_d = jnp.einsum('db,hb->hd', bucket_onehot, rel_bias,
                       precision=lax.Precision.HIGHEST)
    tok = np.arange(QT)[None, :]
    key = np.arange(LANE)[:, None]
    far = tab_d[:, LANE - 1]
    cols = NSA_REP * QT

    def transposed(dist):
        idx = jnp.asarray(np.clip(dist, 0, LANE - 1).astype(np.int32))
        onehot = (idx[..., None] == jnp.arange(LANE, dtype=jnp.int32)).astype(F32)
        t = jnp.einsum('ijd,hd->hij', onehot, tab_d, precision=lax.Precision.HIGHEST)
        t = (t - far[:, None, None]) * LOG2E
        t = t.reshape(NSA_GROUPS, NSA_REP, LANE, QT).transpose(0, 2, 1, 3)
        return t.reshape(NSA_GROUPS, LANE, cols)

    t0t = transposed(tok - key)
    t1t = transposed(tok - key + QT)

    ns = seq // CMP_STRIDE
    d_c = tok - CMP_STRIDE * key + (CMP_STRIDE * CMP_PAD - (CMP_BLOCK - 1))
    seen = np.tile(d_c >= 0, (1, NSA_REP))
    recent = jnp.where(seen[None], transposed(d_c), NEG)
    tt = jnp.concatenate([jnp.zeros((NSA_GROUPS, ns, cols), F32), recent,
                          jnp.full((NSA_GROUPS, ns, cols), NEG, F32)], axis=1)

    rho = np.arange(WINDOW + QT)[:, None]
    tok_w = np.tile(np.arange(QT), NSA_REP)[None, :]
    band = (rho > tok_w) & (rho <= tok_w + WINDOW)
    rows = jnp.concatenate([jnp.zeros((NSA_GROUPS, WINDOW - QT, NSA_REP * QT), F32), t1t, t0t], axis=1)
    wb = jnp.where(band[None], rows, NEG)

    zeros = lambda n: jnp.zeros((NSA_GROUPS, n * QT, cols), F32)
    negs = lambda n: jnp.full((NSA_GROUPS, n * QT, cols), NEG, F32)
    diag = jnp.where(np.tile(tok >= key, (1, NSA_REP))[None], t0t, NEG)
    nd = jnp.stack([jnp.concatenate(([zeros(u - 1), t1t] if u else []) + [diag, negs(SEL_SUB - 1 - u)], axis=1)
                    for u in range(SEL_SUB)], axis=1)
    npv = jnp.concatenate([zeros(SEL_SUB - 1), t1t], axis=1)
    return nd, npv, wb, tt


def _overlap_matrix(seq):
    ns = seq // CMP_STRIDE
    nslc = seq // SLC_BLOCK
    ov = np.zeros((LANE, ns), np.float32)
    cs = np.arange(ns - 1) * CMP_STRIDE
    ss = np.arange(nslc) * SLC_BLOCK
    ov[:nslc, :ns - 1] = ((cs[None, :] < ss[:, None] + SLC_BLOCK) & (cs[None, :] + CMP_BLOCK > ss[:, None]))
    return jnp.asarray(ov, BF16)


def _reorder_cols(a):
    lead = a.shape[:-1]
    gate = a[..., MAIN_COLS:MAIN_COLS + GATE_COLS]
    per = GATE_COLS // NSA_GROUPS
    gate_blocks = []
    for g in range(NSA_GROUPS):
        gate_blocks.append(gate[..., g * per:(g + 1) * per])
        gate_blocks.append(jnp.zeros(lead + (LANE - per,), a.dtype))
    pad = jnp.zeros(lead + ((CB_MGH - CB_GATE - NSA_GROUPS) * LANE,), a.dtype)
    return jnp.concatenate([a[..., :MAIN_COLS]] + gate_blocks + [pad, a[..., MAIN_COLS + GATE_COLS:]], axis=-1)


def kernel(x, c, ada_w, ada_b, w_in, b_in, hg_lb_logits, hg_norm_w, cmp_pos_k, cmp_w1_k, cmp_b1_k, cmp_w2_k, cmp_pos_v, cmp_w1_v, cmp_b1_v, cmp_w2_v, rel_bias, w_br_hg, w_br_nsa, w_out, ln1_g, ln1_b, router_grp_w, router_grp_b, router_exp_w, router_exp_b, exp_w1, exp_w3, exp_w2, ln2_g, ln2_b):
    batch, seq, d = x.shape
    nt = batch * seq
    assert d == D_MODEL and seq % INPROJ_ROWS == 0 and seq // SLC_BLOCK <= LANE
    l = 0
    x2 = x.reshape(nt, d)

    c_pad = jnp.zeros((8, d), F32).at[:batch].set(c)
    mod = _adaln(c_pad, ada_w[l], ada_b[l][None])[:batch].reshape(batch, 6, d)

    proj = _inproj(x2, mod, _reorder_cols(w_in[l].astype(BF16)), _reorder_cols(b_in[l])[None], seq)

    lb_all = jnp.cumsum(jax.nn.softmax(hg_lb_logits.astype(F32), axis=0), axis=0)
    o_h = _hgrn(proj, lb_all[l][None], hg_norm_w[l][None], batch, seq)

    kvc = _compress(proj, jnp.stack([cmp_pos_k[l], cmp_pos_v[l]]),
                    jnp.stack([cmp_w1_k[l], cmp_w1_v[l]]).astype(BF16),
                    jnp.stack([cmp_b1_k[l], cmp_b1_v[l]])[:, None, :],
                    jnp.stack([cmp_w2_k[l], cmp_w2_v[l]]).astype(BF16), batch, seq)

    nd, npv, wb, tt = _bias_tables(rel_bias, seq)
    o_c, sel_t = _cmp_sel(proj, kvc, tt, _overlap_matrix(seq), batch, seq)
    o_w = _window(proj, wb, batch, seq)
    o_n = _selected(proj, sel_t, nd, npv, o_c, o_w, batch, seq)

    w_r = jnp.zeros((d, LANE), F32).at[:, :N_GROUPS].set(router_grp_w[l])
    w_r = w_r.at[:, N_GROUPS:N_GROUPS + N_EXPERTS].set(router_exp_w[l])
    w_r_hi = w_r.astype(BF16)
    w_r_lo = (w_r - w_r_hi.astype(F32)).astype(BF16)
    b_r = jnp.zeros((1, LANE), F32).at[0, :N_GROUPS].set(router_grp_b[l])
    b_r = b_r.at[0, N_GROUPS:N_GROUPS + N_EXPERTS].set(router_exp_b[l])
    x1, h2, logits = _merge(o_h, o_n, proj, x2, mod,
                            w_br_hg[l].astype(BF16), w_br_nsa[l].astype(BF16), w_out[l].astype(BF16),
                            ln1_g[l][None], ln1_b[l][None], jnp.stack([w_r_hi, w_r_lo]), b_r, seq)

    rec, cnt, rec_t = _route(logits)
    counts = cnt[0, N_GROUPS:N_GROUPS + N_EXPERTS].astype(jnp.int32)
    padded = (counts + MOE_ROWS - 1) // MOE_ROWS * MOE_ROWS
    pend = jnp.cumsum(padded)
    pstart = pend - padded
    n_assign = 2 * nt
    nb = n_assign // MOE_ROWS + N_EXPERTS
    slots = rec_t.astype(jnp.int32)
    expert_ids = jnp.arange(N_EXPERTS, dtype=jnp.int32)[:, None]
    slot_base = lambda e: jnp.sum(jnp.where(e[None, :] == expert_ids, pstart[:, None], 0), axis=0)
    dest0 = slot_base(slots[0]) + slots[4]
    dest1 = slot_base(slots[1]) + slots[5]
    block_start = jnp.arange(nb, dtype=jnp.int32) * MOE_ROWS
    block_expert = jnp.minimum(jnp.sum(pend[None, :] <= block_start[:, None], axis=1),
                               N_EXPERTS - 1).astype(jnp.int32)
    n_used = pend[-1] // MOE_ROWS
    spare = n_used + jnp.arange(N_EXPERTS, dtype=jnp.int32)
    zero_blocks = jnp.concatenate([jnp.where(padded > 0, pend // MOE_ROWS - 1, -1),
                                   jnp.where(spare < nb, spare, -1)]).astype(jnp.int32)

    x_pad = _dispatch(dest0, dest1, zero_blocks, h2, nb)
    y_pad = _experts(block_expert, n_used[None].astype(jnp.int32), x_pad, exp_w1[l], exp_w3[l], exp_w2[l])
    out = _combine(dest0, dest1, y_pad, rec, x1, mod, ln2_g[l][None], ln2_b[l][None], seq)
    return out.reshape(batch, seq, d)
```

```python
import functools
import math

import numpy as np
import jax
import jax.numpy as jnp
from jax import lax
from jax.experimental import pallas as pl
from jax.experimental.pallas import tpu as pltpu

F32 = jnp.float32
BF16 = jnp.bfloat16

D_MODEL = 1024
HG_HEADS = 8
HG_DK = 128
HG_DV = 128
HG_CHUNK = 32
HG_SUPER = 256
NSA_HEADS = 8
NSA_GROUPS = 2
NSA_REP = NSA_HEADS // NSA_GROUPS
NSA_DK = 128
CMP_BLOCK = 32
CMP_STRIDE = 16
SLC_BLOCK = 64
SLC_TOPK = 16
WINDOW = 512
FORCE_SCORE = 1e4
N_FORCED = 3
REL_BUCKETS = 32
REL_MAX_DIST = 128
N_GROUPS = 4
EXP_PER_GROUP = 8
N_EXPERTS = N_GROUPS * EXP_PER_GROUP
D_EXPERT = D_MODEL // 2
DEPTH = 1
ALPHA = (2 * DEPTH) ** 0.25

LANE = 128
QT = 128
NEG = -1e30
SCALE = NSA_DK ** -0.5
LOG2E = math.log2(math.e)
SCALE_LOG2 = SCALE * LOG2E
KEY_BLK = 512
BOUND_SLACK = 1.0 + 2.0 ** -10
MAX_REF_GAP = 64.0
SEL_ROWS = 16
SEL_SUB = KEY_BLK // QT
WIN_SUB = 4
CMP_SUB = 4
CMP_PAD = 120
VMEM_LIMIT = 56 * 1024 * 1024

CB_HQ, CB_HF, CB_HI, CB_HG = 0, 8, 16, 24
CB_NQ = 32
CB_KC, CB_VC, CB_KS, CB_VS, CB_KW, CB_VW = 40, 42, 44, 46, 48, 50
CB_GATE = 52
CB_MGH, CB_MGN = 56, 64
NCB = 72
MAIN_COLS = 52 * LANE
GATE_COLS = 3 * NSA_HEADS

INPROJ_ROWS = 2048
INPROJ_COL_BLOCKS = 8
HGRN_ROWS = 512
MERGE_ROWS = 512
ROUTE_ROWS = 512
MOE_TOKEN_TILE = 512
MERGE_SPLIT = 2
ROW_TILES = D_MODEL // LANE
MOE_ROWS = 512
ROW_DMA_UNROLL = 8


def _cparams(sem):
    return pltpu.CompilerParams(dimension_semantics=sem, vmem_limit_bytes=VMEM_LIMIT)


def _dot(a, b):
    return jnp.dot(a, b, preferred_element_type=F32)


def _dot_nt(a, b):
    return lax.dot_general(a, b, (((1,), (1,)), ((), ())), preferred_element_type=F32)


def _dot_tn(a, b):
    return lax.dot_general(a, b, (((0,), (0,)), ((), ())), preferred_element_type=F32)


def _split3(x):
    hi = x.astype(BF16)
    r = x - hi.astype(F32)
    mid = r.astype(BF16)
    lo = (r - mid.astype(F32)).astype(BF16)
    return hi, mid, lo


def _dot01(m01, x):
    hi, mid, lo = _split3(x)
    return _dot(m01, hi) + _dot(m01, mid) + _dot(m01, lo)


def _layer_norm(x, eps=1e-5):
    mu = jnp.mean(x, axis=-1, keepdims=True)
    xc = x - mu
    var = jnp.mean(xc * xc, axis=-1, keepdims=True)
    return xc * lax.rsqrt(var + eps)


def _adaln_kernel(c_ref, w_ref, b_ref, o_ref):
    c = c_ref[...]
    ca = c * jax.nn.sigmoid(c)
    o_ref[...] = jnp.dot(ca, w_ref[...], precision=lax.Precision.HIGHEST,
                         preferred_element_type=F32) + b_ref[...]


def _adaln(c_pad, w, b):
    rows, d = c_pad.shape
    n = w.shape[1]
    return pl.pallas_call(
        _adaln_kernel,
        grid=(n // d,),
        in_specs=[pl.BlockSpec((rows, d), lambda j: (0, 0)),
                  pl.BlockSpec((d, d), lambda j: (0, j)),
                  pl.BlockSpec((1, d), lambda j: (0, j))],
        out_specs=pl.BlockSpec((rows, d), lambda j: (0, j)),
        out_shape=jax.ShapeDtypeStruct((rows, n), F32),
        compiler_params=_cparams(("arbitrary",)),
        name="adaln",
    )(c_pad, w, b)


def _inproj_kernel(x_ref, mod_ref, w_ref, b_ref, o_ref, hn_ref, *, ncb_tile):
    @pl.when(pl.program_id(1) == 0)
    def _():
        hn = _layer_norm(x_ref[...])
        sh = mod_ref[0, 0:1, :]
        sc = mod_ref[0, 1:2, :]
        hn_ref[...] = (hn * (1.0 + sc) + sh).astype(BF16)

    res = _dot(hn_ref[...], w_ref[...]) + b_ref[...]
    for c in range(ncb_tile):
        o_ref[c] = res[:, c * LANE:(c + 1) * LANE]


def _inproj(x2, mod, w, b, seq):
    nt, d = x2.shape
    tm = min(INPROJ_ROWS, seq)
    ncb_tile = INPROJ_COL_BLOCKS
    tn = ncb_tile * LANE
    return pl.pallas_call(
        functools.partial(_inproj_kernel, ncb_tile=ncb_tile),
        grid=(nt // tm, NCB // ncb_tile),
        in_specs=[pl.BlockSpec((tm, d), lambda i, j: (i, 0)),
                  pl.BlockSpec((1, 6, d), lambda i, j: (i * tm // seq, 0, 0)),
                  pl.BlockSpec((d, tn), lambda i, j: (0, j)),
                  pl.BlockSpec((1, tn), lambda i, j: (0, j))],
        out_specs=pl.BlockSpec((ncb_tile, tm, LANE), lambda i, j: (j, i, 0)),
        out_shape=jax.ShapeDtypeStruct((NCB, nt, LANE), F32),
        scratch_shapes=[pltpu.VMEM((tm, d), BF16)],
        compiler_params=_cparams(("arbitrary", "arbitrary")),
        name="inproj",
    )(x2, mod, w, b)


def _hgrn_kernel(q_ref, f_ref, v_ref, g_ref, lb_ref, nw_ref, o_ref, st_ref, *, rows):
    @pl.when(pl.program_id(1) == 0)
    def _():
        st_ref[...] = jnp.zeros_like(st_ref)

    sup = HG_SUPER
    ri = lax.broadcasted_iota(jnp.int32, (sup, sup), 0)
    ci = lax.broadcasted_iota(jnp.int32, (sup, sup), 1)
    same = (ri // HG_CHUNK) == (ci // HG_CHUNK)
    cum_m = jnp.where(same & (ci <= ri), 1.0, 0.0).astype(BF16)
    rt = lax.broadcasted_iota(jnp.int32, (LANE, LANE), 0)
    ct = lax.broadcasted_iota(jnp.int32, (LANE, LANE), 1)
    tril = ((rt // HG_CHUNK) == (ct // HG_CHUNK)) & (ct <= rt)
    per = sup // HG_CHUNK
    groups = [slice(g * LANE, (g + 1) * LANE) for g in range(sup // LANE)]

    heads = range(HG_HEADS)
    hs = [slice(h * LANE, (h + 1) * LANE) for h in heads]

    def wide(ref, r0):
        return jnp.concatenate([ref[h, pl.ds(r0, sup), :] for h in heads], axis=1)

    def body(i, carry):
        r0 = pl.multiple_of(i * sup, sup)
        lb = lb_ref[...]
        f = lb + (1.0 - lb) * jax.nn.sigmoid(wide(f_ref, r0))
        lf = jnp.log(f)
        k = 1.0 - f
        hi = lf.astype(BF16)
        lo = (lf - hi.astype(F32)).astype(BF16)
        b = _dot(cum_m, hi) + _dot(cum_m, lo)
        chunk = [slice(c * HG_CHUNK, (c + 1) * HG_CHUNK) for c in range(per)]
        dec = [jnp.exp(b[(c + 1) * HG_CHUNK - 1:(c + 1) * HG_CHUNK, :]) for c in range(per)]
        dec_rows = jnp.concatenate([jnp.broadcast_to(dec[c], (HG_CHUNK, dec[c].shape[1])) for c in range(per)],
                                   axis=0)
        q_in = (wide(q_ref, r0) * jnp.exp(b)).astype(BF16)
        k_dec = k * jnp.exp(-b)
        k_in = k_dec.astype(BF16)
        k_end = (k_dec * dec_rows).astype(BF16)
        vb = wide(v_ref, r0).astype(BF16)

        att = [[jnp.where(tril, _dot_nt(q_in[g, hs[h]], k_in[g, hs[h]]), 0.0).astype(BF16) for g in groups]
               for h in heads]
        upd = [[_dot_tn(vb[chunk[c], hs[h]], k_end[chunk[c], hs[h]]) for c in range(per)] for h in heads]
        intra = [[_dot(att[h][n], vb[g, hs[h]]) for n, g in enumerate(groups)] for h in heads]

        st = [st_ref[h] for h in heads]
        inter = [[] for _ in heads]
        for c in range(per):
            for h in heads:
                inter[h].append(_dot_nt(q_in[chunk[c], hs[h]], st[h].astype(BF16)))
                st[h] = dec[c][:, hs[h]] * st[h] + upd[h][c]
        for h in heads:
            st_ref[h] = st[h]

        o = [jnp.concatenate(intra[h], axis=0) + jnp.concatenate(inter[h], axis=0) for h in heads]
        scale = [lax.rsqrt(jnp.mean(o[h] * o[h], axis=-1, keepdims=True) + 1e-6) for h in heads]
        on = jnp.concatenate([o[h] * scale[h] for h in heads], axis=1)
        on = on * nw_ref[...] * jax.nn.sigmoid(wide(g_ref, r0))
        o_ref[pl.ds(r0, sup), :] = on.astype(BF16)
        return carry

    lax.fori_loop(0, rows // sup, body, 0)


def _hgrn(proj, lb, nw, batch, seq):
    nt = proj.shape[1]
    tb = min(HGRN_ROWS, seq)
    nblk = seq // tb

    def slab(cb0):
        return pl.BlockSpec((HG_HEADS, tb, LANE), lambda b, t: (cb0 // HG_HEADS, b * nblk + t, 0))

    vec = pl.BlockSpec((1, HG_HEADS * LANE), lambda b, t: (0, 0))
    return pl.pallas_call(
        functools.partial(_hgrn_kernel, rows=tb),
        grid=(batch, nblk),
        in_specs=[slab(CB_HQ), slab(CB_HF), slab(CB_HI), slab(CB_HG), vec, vec],
        out_specs=pl.BlockSpec((tb, HG_HEADS * HG_DV), lambda b, t: (b * nblk + t, 0)),
        out_shape=jax.ShapeDtypeStruct((nt, HG_HEADS * HG_DV), BF16),
        scratch_shapes=[pltpu.VMEM((HG_HEADS, HG_DV, HG_DK), F32)],
        compiler_params=_cparams(("arbitrary", "arbitrary")),
        name="hgrn2",
    )(proj, proj, proj, proj, lb, nw)


def _compress_kernel(x_ref, pos_ref, w1_ref, b1_ref, w2_ref, o_ref, *, ns):
    p0 = jnp.zeros((ns, LANE), F32)
    p1 = jnp.zeros((ns, LANE), F32)
    for j in range(CMP_STRIDE):
        tok = x_ref[0, pl.ds(j, ns, stride=CMP_STRIDE), :]
        rows = slice(j * LANE, (j + 1) * LANE)
        late = slice((CMP_STRIDE + j) * LANE, (CMP_STRIDE + j + 1) * LANE)
        p0 = p0 + _dot((tok + pos_ref[0, j:j + 1, :]).astype(BF16), w1_ref[0, rows, :])
        p1 = p1 + _dot((tok + pos_ref[0, CMP_STRIDE + j:CMP_STRIDE + j + 1, :]).astype(BF16), w1_ref[0, late, :])
    h = p0 + pltpu.roll(p1, ns - 1, axis=0) + b1_ref[0]
    a = h * jax.nn.sigmoid(h)
    out = _dot(a.astype(BF16), w2_ref[0])
    row = lax.broadcasted_iota(jnp.int32, out.shape, 0)
    out = jnp.where(row < ns - 1, out, 0.0)
    o_ref[0, 0, 0] = out


def _compress(proj, pos, w1, b1, w2, batch, seq):
    ns = seq // CMP_STRIDE
    np_rows = ns
    width = CMP_STRIDE * LANE
    return pl.pallas_call(
        functools.partial(_compress_kernel, ns=ns),
        grid=(2, batch, NSA_GROUPS),
        in_specs=[pl.BlockSpec((1, seq, LANE), lambda s, b, g: (CB_KC + NSA_GROUPS * s + g, b, 0)),
                  pl.BlockSpec((1, CMP_BLOCK, LANE), lambda s, b, g: (s, 0, 0)),
                  pl.BlockSpec((1, 2 * width, LANE), lambda s, b, g: (s, 0, 0)),
                  pl.BlockSpec((1, 1, LANE), lambda s, b, g: (s, 0, 0)),
                  pl.BlockSpec((1, LANE, LANE), lambda s, b, g: (s, 0, 0))],
        out_specs=pl.BlockSpec((1, 1, 1, np_rows, LANE), lambda s, b, g: (s, b, g, 0, 0)),
        out_shape=jax.ShapeDtypeStruct((2, batch, NSA_GROUPS, np_rows, LANE), F32),
        compiler_params=_cparams(("arbitrary", "arbitrary", "arbitrary")),
        name="nsa_compress",
    )(proj, pos, w1, b1, w2)


def _cmp_sel_kernel(q0, q1, q2, q3, kc_ref, vc_ref, gate_ref, tt_ref, ov_ref, oc_ref, sel_ref,
                    kb_ref, vt_ref, imp_ref, *, np_rows):
    step = pl.program_id(2)
    cols = NSA_REP * QT
    ns = np_rows

    @pl.when(step == 0)
    def _():
        kb_ref[...] = (kc_ref[0, 0, 0] * SCALE_LOG2).astype(BF16)
        for c in range(np_rows // LANE):
            vt_ref[:, c * LANE:(c + 1) * LANE] = vc_ref[0, 0, 0, c * LANE:(c + 1) * LANE, :].T.astype(BF16)

    subs = range(CMP_SUB)
    tis = [step * CMP_SUB + u for u in subs]
    qts = [_q_transposed((q0, q1, q2, q3), u) for u in subs]

    def group_sum(p):
        tot = p[:, 0:QT]
        for r in range(1, NSA_REP):
            tot = tot + p[:, r * QT:(r + 1) * QT]
        return tot

    def attend(nrows):
        kb = kb_ref[0:nrows, :]
        bias = [tt_ref[0, pl.ds(pl.multiple_of(ns + CMP_PAD - tis[u] * (QT // CMP_STRIDE), 8), nrows), :]
                for u in subs]
        ss = [_dot(kb, qts[u]) + bias[u] for u in subs]
        ms = [jnp.max(ss[u], axis=0, keepdims=True) for u in subs]
        ps = [jnp.exp2(ss[u] - ms[u]) for u in subs]
        ls = [jnp.sum(ps[u], axis=0, keepdims=True) for u in subs]
        invs = [jnp.where(ms[u] > 0.5 * NEG, 1.0 / ls[u], 0.0) for u in subs]
        pn = [ps[u] * invs[u] for u in subs]
        vt = vt_ref[:, 0:nrows]
        os_ = [_dot(vt, pn[u].astype(BF16)) for u in subs]
        for u in subs:
            rows = slice(u * QT, (u + 1) * QT)
            gt = jax.nn.sigmoid(gate_ref[0, rows, :])
            for r in range(NSA_REP):
                oc_ref[rows, r * LANE:(r + 1) * LANE] = gt[:, 3 * r:3 * r + 1] * os_[u][:, r * QT:(r + 1) * QT].T
            imp_ref[:, rows] = _dot01(ov_ref[:, 0:nrows], group_sum(pn[u]))

    visible = (step + 1) * CMP_SUB * (QT // CMP_STRIDE)
    ngroups = np_rows // LANE
    for g in range(1, ngroups + 1):
        upper = visible <= g * LANE if g < ngroups else True

        @pl.when((visible > (g - 1) * LANE) & upper)
        def _():
            attend(g * LANE)

    imp = imp_ref[...]
    width = CMP_SUB * QT
    jj = lax.broadcasted_iota(jnp.int32, (LANE, width), 0)
    tok = step * width + lax.broadcasted_iota(jnp.int32, (LANE, width), 1)
    cur = tok // SLC_BLOCK
    forced = (jj == 0) | (jj == cur) | (jj == cur - 1)
    assert FORCE_SCORE > NSA_REP
    score = jnp.where(forced, -jnp.inf, jnp.where(jj <= cur, imp, -1.0))
    selb = jnp.where(forced, 0.0, NEG)
    jf = jj.astype(F32)
    for _ in range(SLC_TOPK - N_FORCED):
        mval = jnp.max(score, axis=0, keepdims=True)
        first = jnp.min(jnp.where(score == mval, jf, float(LANE)), axis=0, keepdims=True)
        pick = jf == first
        selb = jnp.where(pick, 0.0, selb)
        score = jnp.where(pick, -jnp.inf, score)
    sel_ref[0, 0] = selb


def _q_specs(nsteps, rows=QT):
    return [pl.BlockSpec((1, rows, LANE),
                         functools.partial(lambda b, g, t, r: (CB_NQ + NSA_REP * g + r, b * nsteps + t, 0), r=r))
            for r in range(NSA_REP)]


def _cmp_sel(proj, kvc, tt, ov_t, batch, seq):
    nt = proj.shape[1]
    rows = CMP_SUB * QT
    nsteps = seq // rows
    cols = NSA_REP * QT
    np_rows = kvc.shape[3]
    kv_spec = lambda s: pl.BlockSpec((1, 1, 1, np_rows, LANE), lambda b, g, t: (s, b, g, 0, 0))
    return pl.pallas_call(
        functools.partial(_cmp_sel_kernel, np_rows=np_rows),
        grid=(batch, NSA_GROUPS, nsteps),
        in_specs=_q_specs(nsteps, rows) + [
            kv_spec(0), kv_spec(1),
            pl.BlockSpec((1, rows, LANE), lambda b, g, t: (CB_GATE + g, b * nsteps + t, 0)),
            pl.BlockSpec((1, tt.shape[1], cols), lambda b, g, t: (g, 0, 0)),
            pl.BlockSpec((LANE, np_rows), lambda b, g, t: (0, 0))],
        out_specs=[pl.BlockSpec((rows, cols), lambda b, g, t: (b * nsteps + t, g)),
                   pl.BlockSpec((1, 1, LANE, rows), lambda b, g, t: (b, g, 0, t))],
        out_shape=[jax.ShapeDtypeStruct((nt, NSA_HEADS * LANE), F32),
                   jax.ShapeDtypeStruct((batch, NSA_GROUPS, LANE, seq), F32)],
        scratch_shapes=[pltpu.VMEM((np_rows, LANE), BF16), pltpu.VMEM((LANE, np_rows), BF16),
                        pltpu.VMEM((LANE, rows), F32)],
        compiler_params=_cparams(("arbitrary", "arbitrary", "arbitrary")),
        name="nsa_cmp_select",
    )(proj, proj, proj, proj, kvc, kvc, proj, tt, ov_t)


def _q_transposed(q_refs, sub=0):
    return jnp.concatenate([r[0, sub * QT:(sub + 1) * QT, :].T for r in q_refs], axis=1).astype(BF16)


def _finish_t(state, gate_ref, branch, o_ref, sub=0, others=()):
    _, l, acc = state
    o = acc / jnp.where(l == 0.0, 1.0, l)
    rows = slice(sub * QT, (sub + 1) * QT)
    gt = jax.nn.sigmoid(gate_ref[0, rows, :])
    for r in range(NSA_REP):
        col = 3 * r + branch
        lanes = slice(r * LANE, (r + 1) * LANE)
        val = gt[:, col:col + 1] * o[:, r * QT:(r + 1) * QT].T
        for other in others:
            val = val + other[rows, lanes]
        o_ref[rows, lanes] = val.astype(o_ref.dtype)


def _init_state(cols):
    return (jnp.full((1, cols), NEG, F32), jnp.zeros((1, cols), F32), jnp.zeros((LANE, cols), F32))


def _win_kernel(q0, q1, q2, q3, k_ref, v_ref, gate_ref, wb_ref, o_ref, kb_ref, vt_ref, *, seq):
    step = pl.program_id(2)
    cols = NSA_REP * QT
    wt = WINDOW // QT

    lane = lax.broadcasted_iota(jnp.int32, (QT, LANE), 1)
    pad_mark = jnp.where(lane == 0, 1.0, 0.0).astype(BF16)

    @pl.when(step == 0)
    def _():
        for i in range(wt):
            kb_ref[i * QT:(i + 1) * QT, 0:LANE] = jnp.zeros((QT, LANE), BF16)
            kb_ref[i * QT:(i + 1) * QT, LANE:2 * LANE] = pad_mark
            vt_ref[i] = jnp.zeros((LANE, QT), BF16)

        def fill(i, carry):
            r0 = pl.multiple_of(i * QT, QT)
            kb_ref[pl.ds(WINDOW + r0, QT), 0:LANE] = (k_ref[0, pl.ds(r0, QT), :] * SCALE_LOG2).astype(BF16)
            kb_ref[pl.ds(WINDOW + r0, QT), LANE:2 * LANE] = jnp.zeros((QT, LANE), BF16)
            vt_ref[wt + i] = v_ref[0, pl.ds(r0, QT), :].T.astype(BF16)
            return carry

        lax.fori_loop(0, seq // QT, fill, 0)

    subs = range(WIN_SUB)
    tis = [step * WIN_SUB + sub for sub in subs]
    row = lax.broadcasted_iota(jnp.int32, (LANE, cols), 0)
    pad_rows = jnp.where(row == 0, NEG, 0.0).astype(BF16)
    qts = [jnp.concatenate([_q_transposed((q0, q1, q2, q3), sub), pad_rows], axis=0) for sub in subs]
    ss = [_dot(kb_ref[pl.ds(pl.multiple_of(tis[u] * QT, QT), WINDOW + QT), :], qts[u]) + wb_ref[0]
          for u in subs]
    ms = [jnp.max(ss[u], axis=0, keepdims=True) for u in subs]
    ps = [jnp.exp2(ss[u] - ms[u]) for u in subs]
    ls = [jnp.sum(ps[u], axis=0, keepdims=True) for u in subs]
    pbs = [ps[u].astype(BF16) for u in subs]
    accs = [_dot(vt_ref[tis[u]], pbs[u][0:QT]) for u in subs]
    for d in range(1, wt + 1):
        accs = [accs[u] + _dot(vt_ref[tis[u] + d], pbs[u][d * QT:(d + 1) * QT]) for u in subs]
    for u in subs:
        _finish_t((ms[u], ls[u], accs[u]), gate_ref, 2, o_ref, u)


def _window(proj, wb, batch, seq):
    nt = proj.shape[1]
    rows = WIN_SUB * QT
    nqt = seq // rows
    cols = NSA_REP * QT
    slab = lambda cb0: pl.BlockSpec((1, seq, LANE), lambda b, g, t: (cb0 + g, b, 0))
    return pl.pallas_call(
        functools.partial(_win_kernel, seq=seq),
        grid=(batch, NSA_GROUPS, nqt),
        in_specs=_q_specs(nqt, rows) + [
            slab(CB_KW), slab(CB_VW),
            pl.BlockSpec((1, rows, LANE), lambda b, g, t: (CB_GATE + g, b * nqt + t, 0)),
            pl.BlockSpec((1, WINDOW + QT, cols), lambda b, g, t: (g, 0, 0))],
        out_specs=pl.BlockSpec((rows, cols), lambda b, g, t: (b * nqt + t, g)),
        out_shape=jax.ShapeDtypeStruct((nt, NSA_HEADS * LANE), F32),
        scratch_shapes=[pltpu.VMEM((seq + WINDOW, 2 * LANE), BF16),
                        pltpu.VMEM((seq // QT + WINDOW // QT, LANE, QT), BF16)],
        compiler_params=_cparams(("arbitrary", "arbitrary", "arbitrary")),
        name="nsa_window",
    )(proj, proj, proj, proj, proj, proj, proj, wb)


def _softmax_steps(ss, vt, states):
    n = range(len(ss))
    m_new = [jnp.maximum(states[u][0], jnp.max(ss[u], axis=0, keepdims=True)) for u in n]
    alpha = [jnp.exp2(states[u][0] - m_new[u]) for u in n]
    p = [jnp.exp2(ss[u] - m_new[u]) for u in n]
    l = [alpha[u] * states[u][1] + jnp.sum(p[u], axis=0, keepdims=True) for u in n]
    vts = vt if isinstance(vt, (list, tuple)) else [vt] * len(ss)
    pv = [_dot(vts[u], p[u].astype(BF16)) for u in n]
    return [(m_new[u], l[u], alpha[u] * states[u][2] + pv[u]) for u in n]


def _sel_kernel(q0, q1, q2, q3, k_ref, v_ref, sel_ref, gate_ref, nd_ref, npv_ref, oc_ref, ow_ref, o_ref,
                kb_ref, vt_ref, qa_ref, kn_ref, m_ref, l_ref, acc_ref, *, seq):
    step = pl.program_id(2)
    cols = NSA_REP * QT
    spb = KEY_BLK // SLC_BLOCK
    subs = range(SEL_SUB)

    @pl.when(step == 0)
    def _():
        blk = lax.broadcasted_iota(jnp.int32, (KEY_BLK, LANE), 0) // SLC_BLOCK
        onehot = jnp.where(blk == lax.broadcasted_iota(jnp.int32, (KEY_BLK, LANE), 1), 1.0, 0.0).astype(BF16)

        def fill(i, kmax2):
            r0 = pl.multiple_of(i * KEY_BLK, KEY_BLK)
            kblk = (k_ref[0, pl.ds(r0, KEY_BLK), :] * SCALE_LOG2).astype(BF16)
            kb_ref[pl.ds(r0, KEY_BLK), 0:LANE] = kblk
            kb_ref[pl.ds(r0, KEY_BLK), LANE:2 * LANE] = onehot
            for c in range(SEL_SUB):
                vt_ref[i, :, c * QT:(c + 1) * QT] = v_ref[0, pl.ds(r0 + c * QT, QT), :].T.astype(BF16)
            n2 = jnp.sum(jnp.square(kblk.astype(F32)), axis=1, keepdims=True)
            return jnp.maximum(kmax2, jnp.max(n2, axis=0, keepdims=True))

        kmax2 = lax.fori_loop(0, seq // KEY_BLK, fill, jnp.zeros((1, 1), F32))
        kn_ref[...] = jnp.sqrt(kmax2)
        qa_ref[...] = jnp.zeros_like(qa_ref)

    qts = [_q_transposed((q0, q1, q2, q3), u) for u in subs]
    for u in subs:
        qa_ref[u, 0:LANE, :] = qts[u]
    bounds = [jnp.sqrt(jnp.sum(jnp.square(qts[u].astype(F32)), axis=0, keepdims=True)) * kn_ref[...]
              * BOUND_SLACK + 1.0 for u in subs]

    def block(kb):
        r0 = pl.multiple_of(kb * KEY_BLK, KEY_BLK)
        return kb_ref[pl.ds(r0, KEY_BLK), :], vt_ref[kb]

    pad_rows = jnp.zeros((SEL_ROWS - spb, cols), F32)

    def scores(kb, k, rows_of=lambda u: KEY_BLK):
        for u in subs:
            rows = sel_ref[0, 0, pl.ds(pl.multiple_of(kb * spb, spb), spb), u * QT:(u + 1) * QT]
            rows = jnp.concatenate([rows] * NSA_REP, axis=1)
            qa_ref[u, LANE:LANE + SEL_ROWS, :] = jnp.concatenate([rows, pad_rows], axis=0).astype(BF16)
        return [_dot(k[0:rows_of(u)], qa_ref[u]) for u in subs]

    seen = lambda u: (u + 1) * QT
    k, vt = block(step)
    ss = scores(step, k, seen)
    ss = [ss[u] + nd_ref[0, u, 0:seen(u), :] for u in subs]
    states = _softmax_steps(ss, [vt[:, 0:seen(u)] for u in subs], [_init_state(cols) for _ in subs])

    for u in subs:
        m_ref[u], l_ref[u], acc_ref[u] = states[u]

    def prev_scores():
        k, vt = block(step - 1)
        ss = scores(step - 1, k)
        ss[0] = ss[0] + npv_ref[0]
        return ss, vt

    n_old = jnp.maximum(step - 1, 0)
    bounds[0] = bounds[0] + jnp.max(npv_ref[0], axis=0, keepdims=True)
    m_fix = [jnp.maximum(states[u][0], bounds[u]) for u in subs]
    gap = jnp.max(jnp.concatenate([m_fix[u] - states[u][0] for u in subs], axis=1))
    fixed_ok = gap < MAX_REF_GAP

    @pl.when(fixed_ok)
    def _():
        for u in subs:
            alpha = jnp.exp2(m_ref[u] - m_fix[u])
            l_ref[u] = alpha * l_ref[u]
            acc_ref[u] = alpha * acc_ref[u]
            m_ref[u] = m_fix[u]

        def fixed_step(ss, vt):
            ps = [jnp.exp2(ss[u] - m_ref[u]) for u in subs]
            for u in subs:
                l_ref[u] = l_ref[u] + jnp.sum(ps[u], axis=0, keepdims=True)
            pv = [_dot(vt, ps[u].astype(BF16)) for u in subs]
            for u in subs:
                acc_ref[u] = acc_ref[u] + pv[u]

        @pl.when(step >= 1)
        def _():
            fixed_step(*prev_scores())

        def body(kb, carry):
            k, vt = block(kb)
            fixed_step(scores(kb, k), vt)
            return carry

        lax.fori_loop(0, n_old, body, 0)

    @pl.when(jnp.logical_not(fixed_ok))
    def _():
        def online_step(ss, vt):
            st = _softmax_steps(ss, vt, [(m_ref[u], l_ref[u], acc_ref[u]) for u in subs])
            for u in subs:
                m_ref[u], l_ref[u], acc_ref[u] = st[u]

        @pl.when(step >= 1)
        def _():
            online_step(*prev_scores())

        def body(kb, carry):
            k, vt = block(kb)
            online_step(scores(kb, k), vt)
            return carry

        lax.fori_loop(0, n_old, body, 0)

    for u in subs:
        _finish_t((m_ref[u], l_ref[u], acc_ref[u]), gate_ref, 1, o_ref, u, others=(oc_ref, ow_ref))


def _selected(proj, sel_t, nd, npv, o_c, o_w, batch, seq):
    nt = proj.shape[1]
    rows = SEL_SUB * QT
    nsteps = seq // rows
    cols = NSA_REP * QT
    slab = lambda cb0: pl.BlockSpec((1, seq, LANE), lambda b, g, t: (cb0 + g, b, 0))
    return pl.pallas_call(
        functools.partial(_sel_kernel, seq=seq),
        grid=(batch, NSA_GROUPS, nsteps),
        in_specs=_q_specs(nsteps, rows) + [
            slab(CB_KS), slab(CB_VS),
            pl.BlockSpec((1, 1, LANE, rows), lambda b, g, t: (b, g, 0, t)),
            pl.BlockSpec((1, rows, LANE), lambda b, g, t: (CB_GATE + g, b * nsteps + t, 0)),
            pl.BlockSpec((1, SEL_SUB, KEY_BLK, cols), lambda b, g, t: (g, 0, 0, 0)),
            pl.BlockSpec((1, KEY_BLK, cols), lambda b, g, t: (g, 0, 0)),
            pl.BlockSpec((rows, cols), lambda b, g, t: (b * nsteps + t, g)),
            pl.BlockSpec((rows, cols), lambda b, g, t: (b * nsteps + t, g))],
        out_specs=pl.BlockSpec((rows, cols), lambda b, g, t: (b * nsteps + t, g)),
        out_shape=jax.ShapeDtypeStruct((nt, NSA_HEADS * LANE), BF16),
        scratch_shapes=[pltpu.VMEM((seq, 2 * LANE), BF16),
                        pltpu.VMEM((seq // KEY_BLK, LANE, KEY_BLK), BF16),
                        pltpu.VMEM((SEL_SUB, 2 * LANE, cols), BF16),
                        pltpu.VMEM((1, 1), F32),
                        pltpu.VMEM((SEL_SUB, 1, cols), F32), pltpu.VMEM((SEL_SUB, 1, cols), F32),
                        pltpu.VMEM((SEL_SUB, LANE, cols), F32)],
        compiler_params=_cparams(("arbitrary", "arbitrary", "arbitrary")),
        name="nsa_selected",
    )(proj, proj, proj, proj, proj, proj, sel_t, proj, nd, npv, o_c, o_w)


def _merge_kernel(oh_ref, on_ref, mgh_ref, mgn_ref, x_ref, mod_ref,
                  wh_ref, wn_ref, wo_ref, g_ref, b_ref, wr_ref, br_ref,
                  x1_ref, h2_ref, lg_ref):
    nblk = D_MODEL // LANE
    tm = x_ref.shape[0]
    halves = [slice(s * tm // MERGE_SPLIT, (s + 1) * tm // MERGE_SPLIT) for s in range(MERGE_SPLIT)]
    a_h = [_dot(oh_ref[rs, :], wh_ref[...]) for rs in halves]
    a_n = [_dot(on_ref[rs, :], wn_ref[...]) for rs in halves]
    gh = [jnp.concatenate([mgh_ref[c, rs, :] for c in range(nblk)], axis=-1) for rs in halves]
    gn = [jnp.concatenate([mgn_ref[c, rs, :] for c in range(nblk)], axis=-1) for rs in halves]
    merged = [(jax.nn.sigmoid(gh[s]) * a_h[s] + jax.nn.sigmoid(gn[s]) * a_n[s]).astype(BF16)
              for s in range(MERGE_SPLIT)]
    y = [(1.0 + mod_ref[0, 2:3, :]) * _dot(merged[s], wo_ref[...]) for s in range(MERGE_SPLIT)]
    x1 = [_layer_norm(ALPHA * x_ref[rs, :] + y[s]) * g_ref[...] + b_ref[...] for s, rs in enumerate(halves)]
    h2 = [_layer_norm(x1[s]) * (1.0 + mod_ref[0, 4:5, :]) + mod_ref[0, 3:4, :] for s in range(MERGE_SPLIT)]
    h_hi = [h2[s].astype(BF16) for s in range(MERGE_SPLIT)]
    h_lo = [(h2[s] - h_hi[s].astype(F32)).astype(BF16) for s in range(MERGE_SPLIT)]
    lg = [_dot(h_hi[s], wr_ref[0]) + _dot(h_hi[s], wr_ref[1]) + _dot(h_lo[s], wr_ref[0]) + br_ref[...]
          for s in range(MERGE_SPLIT)]
    for s, rs in enumerate(halves):
        x1_ref[rs, :] = x1[s]
        _store_rows(h2_ref, h2[s], rs.start)
        lg_ref[rs, :] = lg[s]


def _merge(o_h, o_n, proj, x2, mod, w_h, w_n, w_o, ln_g, ln_b, w_r, b_r, seq):
    nt, d = x2.shape
    tm = min(MERGE_ROWS, seq)
    nblk = d // LANE
    row = lambda w: pl.BlockSpec((tm, w), lambda i: (i, 0))
    full = lambda a: pl.BlockSpec(a.shape, lambda i: (0,) * a.ndim)
    return pl.pallas_call(
        _merge_kernel,
        grid=(nt // tm,),
        in_specs=[row(d), row(d),
                  pl.BlockSpec((nblk, tm, LANE), lambda i: (CB_MGH // nblk, i, 0)),
                  pl.BlockSpec((nblk, tm, LANE), lambda i: (CB_MGN // nblk, i, 0)),
                  row(d),
                  pl.BlockSpec((1, 6, d), lambda i: (i * tm // seq, 0, 0)),
                  full(w_h), full(w_n), full(w_o), full(ln_g), full(ln_b), full(w_r), full(b_r)],
        out_specs=[row(d), pl.BlockSpec((tm * ROW_TILES, LANE), lambda i: (i, 0)), row(LANE)],
        out_shape=[jax.ShapeDtypeStruct((nt, d), F32),
                   jax.ShapeDtypeStruct((nt * ROW_TILES, LANE), F32),
                   jax.ShapeDtypeStruct((nt, LANE), F32)],
        compiler_params=_cparams(("arbitrary",)),
        name="merge_outproj",
    )(o_h, o_n, proj, proj, x2, mod, w_h, w_n, w_o, ln_g, ln_b, w_r, b_r)


def _route_kernel(lg_ref, rec_ref, cnt_ref, rect_ref, carry_ref, *, tm):
    @pl.when(pl.program_id(0) == 0)
    def _():
        carry_ref[...] = jnp.zeros_like(carry_ref)

    lg = lg_ref[...]
    lane = lax.broadcasted_iota(jnp.int32, (tm, LANE), 1).astype(F32)
    far = float(LANE)
    gmask = lane < N_GROUPS
    gl = jnp.where(gmask, lg, -jnp.inf)
    gmax = jnp.max(gl, axis=-1, keepdims=True)
    gsum = jnp.sum(jnp.where(gmask, jnp.exp(gl - gmax), 0.0), axis=-1, keepdims=True)
    grp_p = 1.0 / gsum
    gidx = jnp.min(jnp.where(gl == gmax, lane, far), axis=-1, keepdims=True)
    lo = N_GROUPS + EXP_PER_GROUP * gidx
    emask = (lane >= lo) & (lane < lo + EXP_PER_GROUP)
    el = jnp.where(emask, lg, -jnp.inf)
    m1 = jnp.max(el, axis=-1, keepdims=True)
    i1 = jnp.min(jnp.where(el == m1, lane, far), axis=-1, keepdims=True)
    el2 = jnp.where(lane == i1, -jnp.inf, el)
    m2 = jnp.max(el2, axis=-1, keepdims=True)
    i2 = jnp.min(jnp.where(emask & (lane != i1) & (el2 == m2), lane, far), axis=-1, keepdims=True)
    e = jnp.exp(m2 - m1)
    w0 = grp_p / (1.0 + e)
    w1 = grp_p * e / (1.0 + e)

    oh0 = lane == i1
    oh1 = lane == i2
    f0 = jnp.where(oh0, 1.0, 0.0)
    f1 = jnp.where(oh1, 1.0, 0.0)
    ri = lax.broadcasted_iota(jnp.int32, (tm, tm), 0)
    ci = lax.broadcasted_iota(jnp.int32, (tm, tm), 1)
    before = jnp.where(ci < ri, 1.0, 0.0).astype(BF16)
    cum0 = _dot(before, f0.astype(BF16))
    cum1 = _dot(before, f1.astype(BF16))
    tot0 = jnp.sum(f0, axis=0, keepdims=True)
    tot1 = jnp.sum(f1, axis=0, keepdims=True)
    carry = carry_ref[...]
    rank0 = jnp.sum(jnp.where(oh0, carry + cum0, 0.0), axis=-1, keepdims=True)
    rank1 = jnp.sum(jnp.where(oh1, carry + tot0 + cum1, 0.0), axis=-1, keepdims=True)
    carry = carry + tot0 + tot1
    carry_ref[...] = carry
    cnt_ref[...] = carry

    rec = jnp.where(lane == 0, i1 - N_GROUPS, 0.0)
    rec = jnp.where(lane == 1, i2 - N_GROUPS, rec)
    rec = jnp.where(lane == 2, w0, rec)
    rec = jnp.where(lane == 3, w1, rec)
    rec = jnp.where(lane == 4, rank0, rec)
    rec = jnp.where(lane == 5, rank1, rec)
    rec_ref[...] = rec
    rect_ref[...] = jnp.concatenate([rec[c * LANE:(c + 1) * LANE, :].T[0:8, :] for c in range(tm // LANE)], axis=1)


def _route(logits):
    nt = logits.shape[0]
    tm = min(ROUTE_ROWS, nt)
    return pl.pallas_call(
        functools.partial(_route_kernel, tm=tm),
        grid=(nt // tm,),
        in_specs=[pl.BlockSpec((tm, LANE), lambda i: (i, 0))],
        out_specs=[pl.BlockSpec((tm, LANE), lambda i: (i, 0)),
                   pl.BlockSpec((1, LANE), lambda i: (0, 0)),
                   pl.BlockSpec((8, tm), lambda i: (0, i))],
        out_shape=[jax.ShapeDtypeStruct((nt, LANE), F32),
                   jax.ShapeDtypeStruct((1, LANE), F32),
                   jax.ShapeDtypeStruct((8, nt), F32)],
        scratch_shapes=[pltpu.VMEM((1, LANE), F32)],
        compiler_params=_cparams(("arbitrary",)),
        name="moe_route",
    )(logits)


def _row_copy(src, dst, sem):
    return pltpu.make_async_copy(src, dst, sem)


def _tile_of(r):
    return pl.ds(pl.multiple_of(r * ROW_TILES, ROW_TILES), ROW_TILES)


def _load_rows(ref, n, lead=()):
    return jnp.concatenate([ref[lead + (pl.ds(c, n, stride=ROW_TILES), slice(None))]
                            for c in range(ROW_TILES)], axis=1)


def _store_rows(ref, val, row0=0):
    n = val.shape[0]
    for c in range(ROW_TILES):
        ref[pl.ds(row0 * ROW_TILES + c, n, stride=ROW_TILES), :] = val[:, c * LANE:(c + 1) * LANE]


def _dispatch_kernel(dest0_ref, dest1_ref, zb_ref, h_ref, xp_ref, z_ref, sem, zsem, *, tm):
    step = pl.program_id(0)
    base = step * tm
    blk = MOE_ROWS * ROW_TILES

    @pl.when(step == 0)
    def _():
        z_ref[...] = jnp.zeros_like(z_ref)

        def zero_copy(j):
            b = jnp.maximum(zb_ref[j], 0)
            return _row_copy(z_ref, xp_ref.at[pl.ds(pl.multiple_of(b * blk, blk), blk), :], zsem)

        def start(j, carry):
            @pl.when(zb_ref[j] >= 0)
            def _():
                zero_copy(j).start()
            return carry

        def wait(j, carry):
            @pl.when(zb_ref[j] >= 0)
            def _():
                zero_copy(j).wait()
            return carry

        lax.fori_loop(0, 2 * N_EXPERTS, start, 0)
        lax.fori_loop(0, 2 * N_EXPERTS, wait, 0)

    def issue(r, carry):
        for dest_ref in (dest0_ref, dest1_ref):
            d = dest_ref[base + r]
            _row_copy(h_ref.at[_tile_of(r), :], xp_ref.at[_tile_of(d), :], sem).start()
        return carry

    lax.fori_loop(0, tm, issue, 0, unroll=ROW_DMA_UNROLL)
    for _ in range(2):
        _row_copy(h_ref, xp_ref.at[pl.ds(0, tm * ROW_TILES), :], sem).wait()


def _dispatch(dest0, dest1, zero_blocks, h2, n_blocks):
    nt = h2.shape[0] // ROW_TILES
    tm = min(MOE_TOKEN_TILE, nt)
    return pl.pallas_call(
        functools.partial(_dispatch_kernel, tm=tm),
        grid_spec=pltpu.PrefetchScalarGridSpec(
            num_scalar_prefetch=3,
            grid=(nt // tm,),
            in_specs=[pl.BlockSpec((tm * ROW_TILES, LANE), lambda i, *_: (i, 0))],
            out_specs=pl.BlockSpec(memory_space=pl.ANY),
            scratch_shapes=[pltpu.VMEM((MOE_ROWS * ROW_TILES, LANE), F32),
                            pltpu.SemaphoreType.DMA(()), pltpu.SemaphoreType.DMA(())]),
        out_shape=jax.ShapeDtypeStruct((n_blocks * MOE_ROWS * ROW_TILES, LANE), F32),
        compiler_params=_cparams(("arbitrary",)),
        name="moe_dispatch",
    )(dest0, dest1, zero_blocks, h2)


def _expert_kernel(be_ref, nu_ref, x_ref, w1_ref, w3_ref, w2_ref, y_ref, w1b_ref, w3b_ref, w2b_ref):
    i = pl.program_id(0)

    @pl.when((i < nu_ref[0]) & ((i == 0) | (be_ref[i] != be_ref[jnp.maximum(i - 1, 0)])))
    def _():
        w1b_ref[...] = w1_ref[0].astype(BF16)
        w3b_ref[...] = w3_ref[0].astype(BF16)
        w2b_ref[...] = w2_ref[0].astype(BF16)

    @pl.when(i < nu_ref[0])
    def _():
        xb = _load_rows(x_ref, MOE_ROWS).astype(BF16)
        a = _dot(xb, w1b_ref[...])
        b = _dot(xb, w3b_ref[...])
        hmid = (a * jax.nn.sigmoid(a) * b).astype(BF16)
        _store_rows(y_ref, _dot(hmid, w2b_ref[...]))

    @pl.when(i >= nu_ref[0])
    def _():
        y_ref[...] = jnp.zeros_like(y_ref)


def _experts(block_expert, n_used, x_pad, w1, w3, w2):
    d, de = w1.shape[1], w1.shape[2]
    nb = x_pad.shape[0] // (MOE_ROWS * ROW_TILES)
    return pl.pallas_call(
        _expert_kernel,
        grid_spec=pltpu.PrefetchScalarGridSpec(
            num_scalar_prefetch=2,
            grid=(nb,),
            in_specs=[pl.BlockSpec((MOE_ROWS * ROW_TILES, LANE), lambda i, be, nu: (i, 0)),
                      pl.BlockSpec((1, d, de), lambda i, be, nu: (be[i], 0, 0)),
                      pl.BlockSpec((1, d, de), lambda i, be, nu: (be[i], 0, 0)),
                      pl.BlockSpec((1, de, d), lambda i, be, nu: (be[i], 0, 0))],
            out_specs=pl.BlockSpec((MOE_ROWS * ROW_TILES, LANE), lambda i, be, nu: (i, 0)),
            scratch_shapes=[pltpu.VMEM((d, de), BF16), pltpu.VMEM((d, de), BF16), pltpu.VMEM((de, d), BF16)]),
        out_shape=jax.ShapeDtypeStruct(x_pad.shape, F32),
        compiler_params=_cparams(("arbitrary",)),
        name="moe_experts",
    )(block_expert, n_used, x_pad, w1, w3, w2)


def _combine_kernel(dest0_ref, dest1_ref, yp_ref, rec_ref, x1_ref, mod_ref, g_ref, b_ref, o_ref,
                    buf_ref, sem, *, tm):
    step = pl.program_id(0)
    slot = step % 2

    def fetch(tile, to_slot):
        def issue(r, carry):
            for k, dest_ref in enumerate((dest0_ref, dest1_ref)):
                d = dest_ref[tile * tm + r]
                _row_copy(yp_ref.at[_tile_of(d), :], buf_ref.at[to_slot, k, _tile_of(r), :],
                          sem.at[to_slot]).start()
            return carry

        lax.fori_loop(0, tm, issue, 0, unroll=ROW_DMA_UNROLL)

    @pl.when(step == 0)
    def _():
        fetch(0, 0)

    @pl.when(step + 1 < pl.num_programs(0))
    def _():
        fetch(step + 1, 1 - slot)

    for k in range(2):
        _row_copy(yp_ref.at[pl.ds(0, tm * ROW_TILES), :], buf_ref.at[slot, k], sem.at[slot]).wait()

    rec = rec_ref[...]
    y = rec[:, 2:3] * _load_rows(buf_ref, tm, (slot, 0)) + rec[:, 3:4] * _load_rows(buf_ref, tm, (slot, 1))
    y = (1.0 + mod_ref[0, 5:6, :]) * y
    o_ref[...] = _layer_norm(ALPHA * x1_ref[...] + y) * g_ref[...] + b_ref[...]


def _combine(dest0, dest1, y_pad, rec, x1, mod, ln_g, ln_b, seq):
    nt, d = x1.shape
    tm = min(MOE_TOKEN_TILE, seq)
    return pl.pallas_call(
        functools.partial(_combine_kernel, tm=tm),
        grid_spec=pltpu.PrefetchScalarGridSpec(
            num_scalar_prefetch=2,
            grid=(nt // tm,),
            in_specs=[pl.BlockSpec(memory_space=pl.ANY),
                      pl.BlockSpec((tm, LANE), lambda i, *_: (i, 0)),
                      pl.BlockSpec((tm, d), lambda i, *_: (i, 0)),
                      pl.BlockSpec((1, 6, d), lambda i, *_: (i * tm // seq, 0, 0)),
                      pl.BlockSpec((1, d), lambda i, *_: (0, 0)),
                      pl.BlockSpec((1, d), lambda i, *_: (0, 0))],
            out_specs=pl.BlockSpec((tm, d), lambda i, *_: (i, 0)),
            scratch_shapes=[pltpu.VMEM((2, 2, tm * ROW_TILES, LANE), F32), pltpu.SemaphoreType.DMA((2,))]),
        out_shape=jax.ShapeDtypeStruct((nt, d), F32),
        compiler_params=_cparams(("arbitrary",)),
        name="moe_combine",
    )(dest0, dest1, y_pad, rec, x1, mod, ln_g, ln_b)


def _rel_bucket(dist):
    n = jnp.maximum(dist, 0)
    max_exact = REL_BUCKETS // 2
    nf = jnp.maximum(n, 1).astype(F32)
    large = max_exact + (jnp.log(nf / max_exact) / math.log(REL_MAX_DIST / max_exact)
                         * (REL_BUCKETS - max_exact)).astype(jnp.int32)
    large = jnp.minimum(large, REL_BUCKETS - 1)
    return jnp.where(n < max_exact, n, large)


def _bias_tables(rel_bias, seq):
    bucket_onehot = (_rel_bucket(jnp.arange(LANE))[:, None] == jnp.arange(REL_BUCKETS)).astype(F32)
    tab_d = jnp.einsum('db,hb->hd', bucket_onehot, rel_bias,
                       precision=lax.Precision.HIGHEST)
    tok = np.arange(QT)[None, :]
    key = np.arange(LANE)[:, None]
    far = tab_d[:, LANE - 1]
    cols = NSA_REP * QT

    def transposed(dist):
        idx = jnp.asarray(np.clip(dist, 0, LANE - 1).astype(np.int32))
        onehot = (idx[..., None] == jnp.arange(LANE, dtype=jnp.int32)).astype(F32)
        t = jnp.einsum('ijd,hd->hij', onehot, tab_d, precision=lax.Precision.HIGHEST)
        t = (t - far[:, None, None]) * LOG2E
        t = t.reshape(NSA_GROUPS, NSA_REP, LANE, QT).transpose(0, 2, 1, 3)
        return t.reshape(NSA_GROUPS, LANE, cols)

    t0t = transposed(tok - key)
    t1t = transposed(tok - key + QT)

    ns = seq // CMP_STRIDE
    d_c = tok - CMP_STRIDE * key + (CMP_STRIDE * CMP_PAD - (CMP_BLOCK - 1))
    seen = np.tile(d_c >= 0, (1, NSA_REP))
    recent = jnp.where(seen[None], transposed(d_c), NEG)
    tt = jnp.concatenate([jnp.zeros((NSA_GROUPS, ns, cols), F32), recent,
                          jnp.full((NSA_GROUPS, ns, cols), NEG, F32)], axis=1)

    rho = np.arange(WINDOW + QT)[:, None]
    tok_w = np.tile(np.arange(QT), NSA_REP)[None, :]
    band = (rho > tok_w) & (rho <= tok_w + WINDOW)
    rows = jnp.concatenate([jnp.zeros((NSA_GROUPS, WINDOW - QT, NSA_REP * QT), F32), t1t, t0t], axis=1)
    wb = jnp.where(band[None], rows, NEG)

    zeros = lambda n: jnp.zeros((NSA_GROUPS, n * QT, cols), F32)
    negs = lambda n: jnp.full((NSA_GROUPS, n * QT, cols), NEG, F32)
    diag = jnp.where(np.tile(tok >= key, (1, NSA_REP))[None], t0t, NEG)
    nd = jnp.stack([jnp.concatenate(([zeros(u - 1), t1t] if u else []) + [diag, negs(SEL_SUB - 1 - u)], axis=1)
                    for u in range(SEL_SUB)], axis=1)
    npv = jnp.concatenate([zeros(SEL_SUB - 1), t1t], axis=1)
    return nd, npv, wb, tt


def _overlap_matrix(seq):
    ns = seq // CMP_STRIDE
    nslc = seq // SLC_BLOCK
    ov = np.zeros((LANE, ns), np.float32)
    cs = np.arange(ns - 1) * CMP_STRIDE
    ss = np.arange(nslc) * SLC_BLOCK
    ov[:nslc, :ns - 1] = ((cs[None, :] < ss[:, None] + SLC_BLOCK) & (cs[None, :] + CMP_BLOCK > ss[:, None]))
    return jnp.asarray(ov, BF16)


def _reorder_cols(a):
    lead = a.shape[:-1]
    gate = a[..., MAIN_COLS:MAIN_COLS + GATE_COLS]
    per = GATE_COLS // NSA_GROUPS
    gate_blocks = []
    for g in range(NSA_GROUPS):
        gate_blocks.append(gate[..., g * per:(g + 1) * per])
        gate_blocks.append(jnp.zeros(lead + (LANE - per,), a.dtype))
    pad = jnp.zeros(lead + ((CB_MGH - CB_GATE - NSA_GROUPS) * LANE,), a.dtype)
    return jnp.concatenate([a[..., :MAIN_COLS]] + gate_blocks + [pad, a[..., MAIN_COLS + GATE_COLS:]], axis=-1)


def kernel(x, c, ada_w, ada_b, w_in, b_in, hg_lb_logits, hg_norm_w, cmp_pos_k, cmp_w1_k, cmp_b1_k, cmp_w2_k, cmp_pos_v, cmp_w1_v, cmp_b1_v, cmp_w2_v, rel_bias, w_br_hg, w_br_nsa, w_out, ln1_g, ln1_b, router_grp_w, router_grp_b, router_exp_w, router_exp_b, exp_w1, exp_w3, exp_w2, ln2_g, ln2_b):
    batch, seq, d = x.shape
    nt = batch * seq
    assert d == D_MODEL and seq % INPROJ_ROWS == 0 and seq // SLC_BLOCK <= LANE
    l = 0
    x2 = x.reshape(nt, d)

    c_pad = jnp.zeros((8, d), F32).at[:batch].set(c)
    mod = _adaln(c_pad, ada_w[l], ada_b[l][None])[:batch].reshape(batch, 6, d)

    proj = _inproj(x2, mod, _reorder_cols(w_in[l].astype(BF16)), _reorder_cols(b_in[l])[None], seq)

    lb_all = jnp.cumsum(jax.nn.softmax(hg_lb_logits.astype(F32), axis=0), axis=0)
    o_h = _hgrn(proj, lb_all[l][None], hg_norm_w[l][None], batch, seq)

    kvc = _compress(proj, jnp.stack([cmp_pos_k[l], cmp_pos_v[l]]),
                    jnp.stack([cmp_w1_k[l], cmp_w1_v[l]]).astype(BF16),
                    jnp.stack([cmp_b1_k[l], cmp_b1_v[l]])[:, None, :],
                    jnp.stack([cmp_w2_k[l], cmp_w2_v[l]]).astype(BF16), batch, seq)

    nd, npv, wb, tt = _bias_tables(rel_bias, seq)
    o_c, sel_t = _cmp_sel(proj, kvc, tt, _overlap_matrix(seq), batch, seq)
    o_w = _window(proj, wb, batch, seq)
    o_n = _selected(proj, sel_t, nd, npv, o_c, o_w, batch, seq)

    w_r = jnp.zeros((d, LANE), F32).at[:, :N_GROUPS].set(router_grp_w[l])
    w_r = w_r.at[:, N_GROUPS:N_GROUPS + N_EXPERTS].set(router_exp_w[l])
    w_r_hi = w_r.astype(BF16)
    w_r_lo = (w_r - w_r_hi.astype(F32)).astype(BF16)
    b_r = jnp.zeros((1, LANE), F32).at[0, :N_GROUPS].set(router_grp_b[l])
    b_r = b_r.at[0, N_GROUPS:N_GROUPS + N_EXPERTS].set(router_exp_b[l])
    x1, h2, logits = _merge(o_h, o_n, proj, x2, mod,
                            w_br_hg[l].astype(BF16), w_br_nsa[l].astype(BF16), w_out[l].astype(BF16),
                            ln1_g[l][None], ln1_b[l][None], jnp.stack([w_r_hi, w_r_lo]), b_r, seq)

    rec, cnt, rec_t = _route(logits)
    counts = cnt[0, N_GROUPS:N_GROUPS + N_EXPERTS].astype(jnp.int32)
    padded = (counts + MOE_ROWS - 1) // MOE_ROWS * MOE_ROWS
    pend = jnp.cumsum(padded)
    pstart = pend - padded
    n_assign = 2 * nt
    nb = n_assign // MOE_ROWS + N_EXPERTS
    slots = rec_t.astype(jnp.int32)
    expert_ids = jnp.arange(N_EXPERTS, dtype=jnp.int32)[:, None]
    slot_base = lambda e: jnp.sum(jnp.where(e[None, :] == expert_ids, pstart[:, None], 0), axis=0)
    dest0 = slot_base(slots[0]) + slots[4]
    dest1 = slot_base(slots[1]) + slots[5]
    block_start = jnp.arange(nb, dtype=jnp.int32) * MOE_ROWS
    block_expert = jnp.minimum(jnp.sum(pend[None, :] <= block_start[:, None], axis=1),
                               N_EXPERTS - 1).astype(jnp.int32)
    n_used = pend[-1] // MOE_ROWS
    spare = n_used + jnp.arange(N_EXPERTS, dtype=jnp.int32)
    zero_blocks = jnp.concatenate([jnp.where(padded > 0, pend // MOE_ROWS - 1, -1),
                                   jnp.where(spare < nb, spare, -1)]).astype(jnp.int32)

    x_pad = _dispatch(dest0, dest1, zero_blocks, h2, nb)
    y_pad = _experts(block_expert, n_used[None].astype(jnp.int32), x_pad, exp_w1[l], exp_w3[l], exp_w2[l])
    out = _combine(dest0, dest1, y_pad, rec, x1, mod, ln2_g[l][None], ln2_b[l][None], seq)
    return out.reshape(batch, seq, d)
```

```python
import functools
import math

import numpy as np
import jax
import jax.numpy as jnp
from jax import lax
from jax.experimental import pallas as pl
from jax.experimental.pallas import tpu as pltpu

F32 = jnp.float32
BF16 = jnp.bfloat16

D_MODEL = 1024
HG_HEADS = 8
HG_DK = 128
HG_DV = 128
HG_CHUNK = 32
HG_SUPER = 256
NSA_HEADS = 8
NSA_GROUPS = 2
NSA_REP = NSA_HEADS // NSA_GROUPS
NSA_DK = 128
CMP_BLOCK = 32
CMP_STRIDE = 16
SLC_BLOCK = 64
SLC_TOPK = 16
WINDOW = 512
FORCE_SCORE = 1e4
N_FORCED = 3
REL_BUCKETS = 32
REL_MAX_DIST = 128
N_GROUPS = 4
EXP_PER_GROUP = 8
N_EXPERTS = N_GROUPS * EXP_PER_GROUP
D_EXPERT = D_MODEL // 2
DEPTH = 1
ALPHA = (2 * DEPTH) ** 0.25

LANE = 128
QT = 128
NEG = -1e30
SCALE = NSA_DK ** -0.5
LOG2E = math.log2(math.e)
SCALE_LOG2 = SCALE * LOG2E
KEY_BLK = 512
BOUND_SLACK = 1.0 + 2.0 ** -10
MAX_REF_GAP = 64.0
SEL_ROWS = 16
SEL_SUB = KEY_BLK // QT
WIN_SUB = 4
CMP_SUB = 4
CMP_PAD = 120
VMEM_LIMIT = 56 * 1024 * 1024

CB_HQ, CB_HF, CB_HI, CB_HG = 0, 8, 16, 24
CB_NQ = 32
CB_KC, CB_VC, CB_KS, CB_VS, CB_KW, CB_VW = 40, 42, 44, 46, 48, 50
CB_GATE = 52
CB_MGH, CB_MGN = 56, 64
NCB = 72
MAIN_COLS = 52 * LANE
GATE_COLS = 3 * NSA_HEADS

INPROJ_ROWS = 2048
INPROJ_COL_BLOCKS = 8
HGRN_ROWS = 512
MERGE_ROWS = 512
ROUTE_ROWS = 512
DISPATCH_TOKENS = 1024
COMBINE_TOKENS = 256
MERGE_SPLIT = 2
ROW_TILES = D_MODEL // LANE
MOE_ROWS = 512
ROW_DMA_UNROLL = 8


def _cparams(sem):
    return pltpu.CompilerParams(dimension_semantics=sem, vmem_limit_bytes=VMEM_LIMIT)


def _dot(a, b):
    return jnp.dot(a, b, preferred_element_type=F32)


def _dot_nt(a, b):
    return lax.dot_general(a, b, (((1,), (1,)), ((), ())), preferred_element_type=F32)


def _dot_tn(a, b):
    return lax.dot_general(a, b, (((0,), (0,)), ((), ())), preferred_element_type=F32)


def _split3(x):
    hi = x.astype(BF16)
    r = x - hi.astype(F32)
    mid = r.astype(BF16)
    lo = (r - mid.astype(F32)).astype(BF16)
    return hi, mid, lo


def _dot01(m01, x):
    hi, mid, lo = _split3(x)
    return _dot(m01, hi) + _dot(m01, mid) + _dot(m01, lo)


def _layer_norm(x, eps=1e-5):
    mu = jnp.mean(x, axis=-1, keepdims=True)
    xc = x - mu
    var = jnp.mean(xc * xc, axis=-1, keepdims=True)
    return xc * lax.rsqrt(var + eps)


def _adaln_kernel(c_ref, w_ref, b_ref, o_ref):
    c = c_ref[...]
    ca = c * jax.nn.sigmoid(c)
    o_ref[...] = jnp.dot(ca, w_ref[...], precision=lax.Precision.HIGHEST,
                         preferred_element_type=F32) + b_ref[...]


def _adaln(c_pad, w, b):
    rows, d = c_pad.shape
    n = w.shape[1]
    return pl.pallas_call(
        _adaln_kernel,
        grid=(n // d,),
        in_specs=[pl.BlockSpec((rows, d), lambda j: (0, 0)),
                  pl.BlockSpec((d, d), lambda j: (0, j)),
                  pl.BlockSpec((1, d), lambda j: (0, j))],
        out_specs=pl.BlockSpec((rows, d), lambda j: (0, j)),
        out_shape=jax.ShapeDtypeStruct((rows, n), F32),
        compiler_params=_cparams(("arbitrary",)),
        name="adaln",
    )(c_pad, w, b)


def _inproj_kernel(x_ref, mod_ref, w_ref, b_ref, o_ref, hn_ref, *, ncb_tile):
    @pl.when(pl.program_id(1) == 0)
    def _():
        hn = _layer_norm(x_ref[...])
        sh = mod_ref[0, 0:1, :]
        sc = mod_ref[0, 1:2, :]
        hn_ref[...] = (hn * (1.0 + sc) + sh).astype(BF16)

    res = _dot(hn_ref[...], w_ref[...]) + b_ref[...]
    for c in range(ncb_tile):
        o_ref[c] = res[:, c * LANE:(c + 1) * LANE]


def _inproj(x2, mod, w, b, seq):
    nt, d = x2.shape
    tm = min(INPROJ_ROWS, seq)
    ncb_tile = INPROJ_COL_BLOCKS
    tn = ncb_tile * LANE
    return pl.pallas_call(
        functools.partial(_inproj_kernel, ncb_tile=ncb_tile),
        grid=(nt // tm, NCB // ncb_tile),
        in_specs=[pl.BlockSpec((tm, d), lambda i, j: (i, 0)),
                  pl.BlockSpec((1, 6, d), lambda i, j: (i * tm // seq, 0, 0)),
                  pl.BlockSpec((d, tn), lambda i, j: (0, j)),
                  pl.BlockSpec((1, tn), lambda i, j: (0, j))],
        out_specs=pl.BlockSpec((ncb_tile, tm, LANE), lambda i, j: (j, i, 0)),
        out_shape=jax.ShapeDtypeStruct((NCB, nt, LANE), F32),
        scratch_shapes=[pltpu.VMEM((tm, d), BF16)],
        compiler_params=_cparams(("arbitrary", "arbitrary")),
        name="inproj",
    )(x2, mod, w, b)


def _hgrn_kernel(q_ref, f_ref, v_ref, g_ref, lb_ref, nw_ref, o_ref, st_ref, *, rows):
    @pl.when(pl.program_id(1) == 0)
    def _():
        st_ref[...] = jnp.zeros_like(st_ref)

    sup = HG_SUPER
    ri = lax.broadcasted_iota(jnp.int32, (sup, sup), 0)
    ci = lax.broadcasted_iota(jnp.int32, (sup, sup), 1)
    same = (ri // HG_CHUNK) == (ci // HG_CHUNK)
    cum_m = jnp.where(same & (ci <= ri), 1.0, 0.0).astype(BF16)
    rt = lax.broadcasted_iota(jnp.int32, (LANE, LANE), 0)
    ct = lax.broadcasted_iota(jnp.int32, (LANE, LANE), 1)
    tril = ((rt // HG_CHUNK) == (ct // HG_CHUNK)) & (ct <= rt)
    per = sup // HG_CHUNK
    groups = [slice(g * LANE, (g + 1) * LANE) for g in range(sup // LANE)]

    heads = range(HG_HEADS)
    hs = [slice(h * LANE, (h + 1) * LANE) for h in heads]

    def wide(ref, r0):
        return jnp.concatenate([ref[h, pl.ds(r0, sup), :] for h in heads], axis=1)

    def body(i, carry):
        r0 = pl.multiple_of(i * sup, sup)
        lb = lb_ref[...]
        f = lb + (1.0 - lb) * jax.nn.sigmoid(wide(f_ref, r0))
        lf = jnp.log(f)
        k = 1.0 - f
        hi = lf.astype(BF16)
        lo = (lf - hi.astype(F32)).astype(BF16)
        b = _dot(cum_m, hi) + _dot(cum_m, lo)
        chunk = [slice(c * HG_CHUNK, (c + 1) * HG_CHUNK) for c in range(per)]
        dec = [jnp.exp(b[(c + 1) * HG_CHUNK - 1:(c + 1) * HG_CHUNK, :]) for c in range(per)]
        dec_rows = jnp.concatenate([jnp.broadcast_to(dec[c], (HG_CHUNK, dec[c].shape[1])) for c in range(per)],
                                   axis=0)
        q_in = (wide(q_ref, r0) * jnp.exp(b)).astype(BF16)
        k_dec = k * jnp.exp(-b)
        k_in = k_dec.astype(BF16)
        k_end = (k_dec * dec_rows).astype(BF16)
        vb = wide(v_ref, r0).astype(BF16)

        att = [[jnp.where(tril, _dot_nt(q_in[g, hs[h]], k_in[g, hs[h]]), 0.0).astype(BF16) for g in groups]
               for h in heads]
        upd = [[_dot_tn(vb[chunk[c], hs[h]], k_end[chunk[c], hs[h]]) for c in range(per)] for h in heads]
        intra = [[_dot(att[h][n], vb[g, hs[h]]) for n, g in enumerate(groups)] for h in heads]

        st = [st_ref[h] for h in heads]
        inter = [[] for _ in heads]
        for c in range(per):
            for h in heads:
                inter[h].append(_dot_nt(q_in[chunk[c], hs[h]], st[h].astype(BF16)))
                st[h] = dec[c][:, hs[h]] * st[h] + upd[h][c]
        for h in heads:
            st_ref[h] = st[h]

        o = [jnp.concatenate(intra[h], axis=0) + jnp.concatenate(inter[h], axis=0) for h in heads]
        scale = [lax.rsqrt(jnp.mean(o[h] * o[h], axis=-1, keepdims=True) + 1e-6) for h in heads]
        on = jnp.concatenate([o[h] * scale[h] for h in heads], axis=1)
        on = on * nw_ref[...] * jax.nn.sigmoid(wide(g_ref, r0))
        o_ref[pl.ds(r0, sup), :] = on.astype(BF16)
        return carry

    lax.fori_loop(0, rows // sup, body, 0)


def _hgrn(proj, lb, nw, batch, seq):
    nt = proj.shape[1]
    tb = min(HGRN_ROWS, seq)
    nblk = seq // tb

    def slab(cb0):
        return pl.BlockSpec((HG_HEADS, tb, LANE), lambda b, t: (cb0 // HG_HEADS, b * nblk + t, 0))

    vec = pl.BlockSpec((1, HG_HEADS * LANE), lambda b, t: (0, 0))
    return pl.pallas_call(
        functools.partial(_hgrn_kernel, rows=tb),
        grid=(batch, nblk),
        in_specs=[slab(CB_HQ), slab(CB_HF), slab(CB_HI), slab(CB_HG), vec, vec],
        out_specs=pl.BlockSpec((tb, HG_HEADS * HG_DV), lambda b, t: (b * nblk + t, 0)),
        out_shape=jax.ShapeDtypeStruct((nt, HG_HEADS * HG_DV), BF16),
        scratch_shapes=[pltpu.VMEM((HG_HEADS, HG_DV, HG_DK), F32)],
        compiler_params=_cparams(("arbitrary", "arbitrary")),
        name="hgrn2",
    )(proj, proj, proj, proj, lb, nw)


def _compress_kernel(x_ref, pos_ref, w1_ref, b1_ref, w2_ref, o_ref, *, ns):
    p0 = jnp.zeros((ns, LANE), F32)
    p1 = jnp.zeros((ns, LANE), F32)
    for j in range(CMP_STRIDE):
        tok = x_ref[0, pl.ds(j, ns, stride=CMP_STRIDE), :]
        rows = slice(j * LANE, (j + 1) * LANE)
        late = slice((CMP_STRIDE + j) * LANE, (CMP_STRIDE + j + 1) * LANE)
        p0 = p0 + _dot((tok + pos_ref[0, j:j + 1, :]).astype(BF16), w1_ref[0, rows, :])
        p1 = p1 + _dot((tok + pos_ref[0, CMP_STRIDE + j:CMP_STRIDE + j + 1, :]).astype(BF16), w1_ref[0, late, :])
    h = p0 + pltpu.roll(p1, ns - 1, axis=0) + b1_ref[0]
    a = h * jax.nn.sigmoid(h)
    out = _dot(a.astype(BF16), w2_ref[0])
    row = lax.broadcasted_iota(jnp.int32, out.shape, 0)
    out = jnp.where(row < ns - 1, out, 0.0)
    o_ref[0, 0, 0] = out


def _compress(proj, pos, w1, b1, w2, batch, seq):
    ns = seq // CMP_STRIDE
    np_rows = ns
    width = CMP_STRIDE * LANE
    return pl.pallas_call(
        functools.partial(_compress_kernel, ns=ns),
        grid=(2, batch, NSA_GROUPS),
        in_specs=[pl.BlockSpec((1, seq, LANE), lambda s, b, g: (CB_KC + NSA_GROUPS * s + g, b, 0)),
                  pl.BlockSpec((1, CMP_BLOCK, LANE), lambda s, b, g: (s, 0, 0)),
                  pl.BlockSpec((1, 2 * width, LANE), lambda s, b, g: (s, 0, 0)),
                  pl.BlockSpec((1, 1, LANE), lambda s, b, g: (s, 0, 0)),
                  pl.BlockSpec((1, LANE, LANE), lambda s, b, g: (s, 0, 0))],
        out_specs=pl.BlockSpec((1, 1, 1, np_rows, LANE), lambda s, b, g: (s, b, g, 0, 0)),
        out_shape=jax.ShapeDtypeStruct((2, batch, NSA_GROUPS, np_rows, LANE), F32),
        compiler_params=_cparams(("arbitrary", "arbitrary", "arbitrary")),
        name="nsa_compress",
    )(proj, pos, w1, b1, w2)


def _cmp_sel_kernel(q0, q1, q2, q3, kc_ref, vc_ref, gate_ref, tt_ref, ov_ref, oc_ref, sel_ref,
                    kb_ref, vt_ref, imp_ref, *, np_rows):
    step = pl.program_id(2)
    cols = NSA_REP * QT
    ns = np_rows

    @pl.when(step == 0)
    def _():
        kb_ref[...] = (kc_ref[0, 0, 0] * SCALE_LOG2).astype(BF16)
        for c in range(np_rows // LANE):
            vt_ref[:, c * LANE:(c + 1) * LANE] = vc_ref[0, 0, 0, c * LANE:(c + 1) * LANE, :].T.astype(BF16)

    subs = range(CMP_SUB)
    tis = [step * CMP_SUB + u for u in subs]
    qts = [_q_transposed((q0, q1, q2, q3), u) for u in subs]

    def group_sum(p):
        tot = p[:, 0:QT]
        for r in range(1, NSA_REP):
            tot = tot + p[:, r * QT:(r + 1) * QT]
        return tot

    def attend(nrows):
        kb = kb_ref[0:nrows, :]
        bias = [tt_ref[0, pl.ds(pl.multiple_of(ns + CMP_PAD - tis[u] * (QT // CMP_STRIDE), 8), nrows), :]
                for u in subs]
        ss = [_dot(kb, qts[u]) + bias[u] for u in subs]
        ms = [jnp.max(ss[u], axis=0, keepdims=True) for u in subs]
        ps = [jnp.exp2(ss[u] - ms[u]) for u in subs]
        ls = [jnp.sum(ps[u], axis=0, keepdims=True) for u in subs]
        invs = [jnp.where(ms[u] > 0.5 * NEG, 1.0 / ls[u], 0.0) for u in subs]
        pn = [ps[u] * invs[u] for u in subs]
        vt = vt_ref[:, 0:nrows]
        os_ = [_dot(vt, pn[u].astype(BF16)) for u in subs]
        for u in subs:
            rows = slice(u * QT, (u + 1) * QT)
            gt = jax.nn.sigmoid(gate_ref[0, rows, :])
            for r in range(NSA_REP):
                oc_ref[rows, r * LANE:(r + 1) * LANE] = gt[:, 3 * r:3 * r + 1] * os_[u][:, r * QT:(r + 1) * QT].T
            imp_ref[:, rows] = _dot01(ov_ref[:, 0:nrows], group_sum(pn[u]))

    visible = (step + 1) * CMP_SUB * (QT // CMP_STRIDE)
    ngroups = np_rows // LANE
    for g in range(1, ngroups + 1):
        upper = visible <= g * LANE if g < ngroups else True

        @pl.when((visible > (g - 1) * LANE) & upper)
        def _():
            attend(g * LANE)

    imp = imp_ref[...]
    width = CMP_SUB * QT
    jj = lax.broadcasted_iota(jnp.int32, (LANE, width), 0)
    tok = step * width + lax.broadcasted_iota(jnp.int32, (LANE, width), 1)
    cur = tok // SLC_BLOCK
    forced = (jj == 0) | (jj == cur) | (jj == cur - 1)
    assert FORCE_SCORE > NSA_REP
    score = jnp.where(forced, -jnp.inf, jnp.where(jj <= cur, imp, -1.0))
    selb = jnp.where(forced, 0.0, NEG)
    jf = jj.astype(F32)
    for _ in range(SLC_TOPK - N_FORCED):
        mval = jnp.max(score, axis=0, keepdims=True)
        first = jnp.min(jnp.where(score == mval, jf, float(LANE)), axis=0, keepdims=True)
        pick = jf == first
        selb = jnp.where(pick, 0.0, selb)
        score = jnp.where(pick, -jnp.inf, score)
    sel_ref[0, 0] = selb


def _q_specs(nsteps, rows=QT):
    return [pl.BlockSpec((1, rows, LANE),
                         functools.partial(lambda b, g, t, r: (CB_NQ + NSA_REP * g + r, b * nsteps + t, 0), r=r))
            for r in range(NSA_REP)]


def _cmp_sel(proj, kvc, tt, ov_t, batch, seq):
    nt = proj.shape[1]
    rows = CMP_SUB * QT
    nsteps = seq // rows
    cols = NSA_REP * QT
    np_rows = kvc.shape[3]
    kv_spec = lambda s: pl.BlockSpec((1, 1, 1, np_rows, LANE), lambda b, g, t: (s, b, g, 0, 0))
    return pl.pallas_call(
        functools.partial(_cmp_sel_kernel, np_rows=np_rows),
        grid=(batch, NSA_GROUPS, nsteps),
        in_specs=_q_specs(nsteps, rows) + [
            kv_spec(0), kv_spec(1),
            pl.BlockSpec((1, rows, LANE), lambda b, g, t: (CB_GATE + g, b * nsteps + t, 0)),
            pl.BlockSpec((1, tt.shape[1], cols), lambda b, g, t: (g, 0, 0)),
            pl.BlockSpec((LANE, np_rows), lambda b, g, t: (0, 0))],
        out_specs=[pl.BlockSpec((rows, cols), lambda b, g, t: (b * nsteps + t, g)),
                   pl.BlockSpec((1, 1, LANE, rows), lambda b, g, t: (b, g, 0, t))],
        out_shape=[jax.ShapeDtypeStruct((nt, NSA_HEADS * LANE), F32),
                   jax.ShapeDtypeStruct((batch, NSA_GROUPS, LANE, seq), F32)],
        scratch_shapes=[pltpu.VMEM((np_rows, LANE), BF16), pltpu.VMEM((LANE, np_rows), BF16),
                        pltpu.VMEM((LANE, rows), F32)],
        compiler_params=_cparams(("arbitrary", "arbitrary", "arbitrary")),
        name="nsa_cmp_select",
    )(proj, proj, proj, proj, kvc, kvc, proj, tt, ov_t)


def _q_transposed(q_refs, sub=0):
    return jnp.concatenate([r[0, sub * QT:(sub + 1) * QT, :].T for r in q_refs], axis=1).astype(BF16)


def _finish_t(state, gate_ref, branch, o_ref, sub=0, others=()):
    _, l, acc = state
    o = acc / jnp.where(l == 0.0, 1.0, l)
    rows = slice(sub * QT, (sub + 1) * QT)
    gt = jax.nn.sigmoid(gate_ref[0, rows, :])
    for r in range(NSA_REP):
        col = 3 * r + branch
        lanes = slice(r * LANE, (r + 1) * LANE)
        val = gt[:, col:col + 1] * o[:, r * QT:(r + 1) * QT].T
        for other in others:
            val = val + other[rows, lanes]
        o_ref[rows, lanes] = val.astype(o_ref.dtype)


def _init_state(cols):
    return (jnp.full((1, cols), NEG, F32), jnp.zeros((1, cols), F32), jnp.zeros((LANE, cols), F32))


def _win_kernel(q0, q1, q2, q3, k_ref, v_ref, gate_ref, wb_ref, o_ref, kb_ref, vt_ref, *, seq):
    step = pl.program_id(2)
    cols = NSA_REP * QT
    wt = WINDOW // QT

    lane = lax.broadcasted_iota(jnp.int32, (QT, LANE), 1)
    pad_mark = jnp.where(lane == 0, 1.0, 0.0).astype(BF16)

    @pl.when(step == 0)
    def _():
        for i in range(wt):
            kb_ref[i * QT:(i + 1) * QT, 0:LANE] = jnp.zeros((QT, LANE), BF16)
            kb_ref[i * QT:(i + 1) * QT, LANE:2 * LANE] = pad_mark
            vt_ref[i] = jnp.zeros((LANE, QT), BF16)

        def fill(i, carry):
            r0 = pl.multiple_of(i * QT, QT)
            kb_ref[pl.ds(WINDOW + r0, QT), 0:LANE] = (k_ref[0, pl.ds(r0, QT), :] * SCALE_LOG2).astype(BF16)
            kb_ref[pl.ds(WINDOW + r0, QT), LANE:2 * LANE] = jnp.zeros((QT, LANE), BF16)
            vt_ref[wt + i] = v_ref[0, pl.ds(r0, QT), :].T.astype(BF16)
            return carry

        lax.fori_loop(0, seq // QT, fill, 0)

    subs = range(WIN_SUB)
    tis = [step * WIN_SUB + sub for sub in subs]
    row = lax.broadcasted_iota(jnp.int32, (LANE, cols), 0)
    pad_rows = jnp.where(row == 0, NEG, 0.0).astype(BF16)
    qts = [jnp.concatenate([_q_transposed((q0, q1, q2, q3), sub), pad_rows], axis=0) for sub in subs]
    ss = [_dot(kb_ref[pl.ds(pl.multiple_of(tis[u] * QT, QT), WINDOW + QT), :], qts[u]) + wb_ref[0]
          for u in subs]
    ms = [jnp.max(ss[u], axis=0, keepdims=True) for u in subs]
    ps = [jnp.exp2(ss[u] - ms[u]) for u in subs]
    ls = [jnp.sum(ps[u], axis=0, keepdims=True) for u in subs]
    pbs = [ps[u].astype(BF16) for u in subs]
    accs = [_dot(vt_ref[tis[u]], pbs[u][0:QT]) for u in subs]
    for d in range(1, wt + 1):
        accs = [accs[u] + _dot(vt_ref[tis[u] + d], pbs[u][d * QT:(d + 1) * QT]) for u in subs]
    for u in subs:
        _finish_t((ms[u], ls[u], accs[u]), gate_ref, 2, o_ref, u)


def _window(proj, wb, batch, seq):
    nt = proj.shape[1]
    rows = WIN_SUB * QT
    nqt = seq // rows
    cols = NSA_REP * QT
    slab = lambda cb0: pl.BlockSpec((1, seq, LANE), lambda b, g, t: (cb0 + g, b, 0))
    return pl.pallas_call(
        functools.partial(_win_kernel, seq=seq),
        grid=(batch, NSA_GROUPS, nqt),
        in_specs=_q_specs(nqt, rows) + [
            slab(CB_KW), slab(CB_VW),
            pl.BlockSpec((1, rows, LANE), lambda b, g, t: (CB_GATE + g, b * nqt + t, 0)),
            pl.BlockSpec((1, WINDOW + QT, cols), lambda b, g, t: (g, 0, 0))],
        out_specs=pl.BlockSpec((rows, cols), lambda b, g, t: (b * nqt + t, g)),
        out_shape=jax.ShapeDtypeStruct((nt, NSA_HEADS * LANE), F32),
        scratch_shapes=[pltpu.VMEM((seq + WINDOW, 2 * LANE), BF16),
                        pltpu.VMEM((seq // QT + WINDOW // QT, LANE, QT), BF16)],
        compiler_params=_cparams(("arbitrary", "arbitrary", "arbitrary")),
        name="nsa_window",
    )(proj, proj, proj, proj, proj, proj, proj, wb)


def _softmax_steps(ss, vt, states):
    n = range(len(ss))
    m_new = [jnp.maximum(states[u][0], jnp.max(ss[u], axis=0, keepdims=True)) for u in n]
    alpha = [jnp.exp2(states[u][0] - m_new[u]) for u in n]
    p = [jnp.exp2(ss[u] - m_new[u]) for u in n]
    l = [alpha[u] * states[u][1] + jnp.sum(p[u], axis=0, keepdims=True) for u in n]
    vts = vt if isinstance(vt, (list, tuple)) else [vt] * len(ss)
    pv = [_dot(vts[u], p[u].astype(BF16)) for u in n]
    return [(m_new[u], l[u], alpha[u] * states[u][2] + pv[u]) for u in n]


def _sel_kernel(q0, q1, q2, q3, k_ref, v_ref, sel_ref, gate_ref, nd_ref, npv_ref, oc_ref, ow_ref, o_ref,
                kb_ref, vt_ref, qa_ref, kn_ref, m_ref, l_ref, acc_ref, *, seq):
    step = pl.program_id(2)
    cols = NSA_REP * QT
    spb = KEY_BLK // SLC_BLOCK
    subs = range(SEL_SUB)

    @pl.when(step == 0)
    def _():
        blk = lax.broadcasted_iota(jnp.int32, (KEY_BLK, LANE), 0) // SLC_BLOCK
        onehot = jnp.where(blk == lax.broadcasted_iota(jnp.int32, (KEY_BLK, LANE), 1), 1.0, 0.0).astype(BF16)

        def fill(i, kmax2):
            r0 = pl.multiple_of(i * KEY_BLK, KEY_BLK)
            kblk = (k_ref[0, pl.ds(r0, KEY_BLK), :] * SCALE_LOG2).astype(BF16)
            kb_ref[pl.ds(r0, KEY_BLK), 0:LANE] = kblk
            kb_ref[pl.ds(r0, KEY_BLK), LANE:2 * LANE] = onehot
            for c in range(SEL_SUB):
                vt_ref[i, :, c * QT:(c + 1) * QT] = v_ref[0, pl.ds(r0 + c * QT, QT), :].T.astype(BF16)
            n2 = jnp.sum(jnp.square(kblk.astype(F32)), axis=1, keepdims=True)
            return jnp.maximum(kmax2, jnp.max(n2, axis=0, keepdims=True))

        kmax2 = lax.fori_loop(0, seq // KEY_BLK, fill, jnp.zeros((1, 1), F32))
        kn_ref[...] = jnp.sqrt(kmax2)
        qa_ref[...] = jnp.zeros_like(qa_ref)

    qts = [_q_transposed((q0, q1, q2, q3), u) for u in subs]
    for u in subs:
        qa_ref[u, 0:LANE, :] = qts[u]
    bounds = [jnp.sqrt(jnp.sum(jnp.square(qts[u].astype(F32)), axis=0, keepdims=True)) * kn_ref[...]
              * BOUND_SLACK + 1.0 for u in subs]

    def block(kb):
        r0 = pl.multiple_of(kb * KEY_BLK, KEY_BLK)
        return kb_ref[pl.ds(r0, KEY_BLK), :], vt_ref[kb]

    pad_rows = jnp.zeros((SEL_ROWS - spb, cols), F32)

    def scores(kb, k, rows_of=lambda u: KEY_BLK):
        for u in subs:
            rows = sel_ref[0, 0, pl.ds(pl.multiple_of(kb * spb, spb), spb), u * QT:(u + 1) * QT]
            rows = jnp.concatenate([rows] * NSA_REP, axis=1)
            qa_ref[u, LANE:LANE + SEL_ROWS, :] = jnp.concatenate([rows, pad_rows], axis=0).astype(BF16)
        return [_dot(k[0:rows_of(u)], qa_ref[u]) for u in subs]

    seen = lambda u: (u + 1) * QT
    k, vt = block(step)
    ss = scores(step, k, seen)
    ss = [ss[u] + nd_ref[0, u, 0:seen(u), :] for u in subs]
    states = _softmax_steps(ss, [vt[:, 0:seen(u)] for u in subs], [_init_state(cols) for _ in subs])

    for u in subs:
        m_ref[u], l_ref[u], acc_ref[u] = states[u]

    def prev_scores():
        k, vt = block(step - 1)
        ss = scores(step - 1, k)
        ss[0] = ss[0] + npv_ref[0]
        return ss, vt

    n_old = jnp.maximum(step - 1, 0)
    bounds[0] = bounds[0] + jnp.max(npv_ref[0], axis=0, keepdims=True)
    m_fix = [jnp.maximum(states[u][0], bounds[u]) for u in subs]
    gap = jnp.max(jnp.concatenate([m_fix[u] - states[u][0] for u in subs], axis=1))
    fixed_ok = gap < MAX_REF_GAP

    @pl.when(fixed_ok)
    def _():
        for u in subs:
            alpha = jnp.exp2(m_ref[u] - m_fix[u])
            l_ref[u] = alpha * l_ref[u]
            acc_ref[u] = alpha * acc_ref[u]
            m_ref[u] = m_fix[u]

        def fixed_step(ss, vt):
            ps = [jnp.exp2(ss[u] - m_ref[u]) for u in subs]
            for u in subs:
                l_ref[u] = l_ref[u] + jnp.sum(ps[u], axis=0, keepdims=True)
            pv = [_dot(vt, ps[u].astype(BF16)) for u in subs]
            for u in subs:
                acc_ref[u] = acc_ref[u] + pv[u]

        @pl.when(step >= 1)
        def _():
            fixed_step(*prev_scores())

        def body(kb, carry):
            k, vt = block(kb)
            fixed_step(scores(kb, k), vt)
            return carry

        lax.fori_loop(0, n_old, body, 0)

    @pl.when(jnp.logical_not(fixed_ok))
    def _():
        def online_step(ss, vt):
            st = _softmax_steps(ss, vt, [(m_ref[u], l_ref[u], acc_ref[u]) for u in subs])
            for u in subs:
                m_ref[u], l_ref[u], acc_ref[u] = st[u]

        @pl.when(step >= 1)
        def _():
            online_step(*prev_scores())

        def body(kb, carry):
            k, vt = block(kb)
            online_step(scores(kb, k), vt)
            return carry

        lax.fori_loop(0, n_old, body, 0)

    for u in subs:
        _finish_t((m_ref[u], l_ref[u], acc_ref[u]), gate_ref, 1, o_ref, u, others=(oc_ref, ow_ref))


def _selected(proj, sel_t, nd, npv, o_c, o_w, batch, seq):
    nt = proj.shape[1]
    rows = SEL_SUB * QT
    nsteps = seq // rows
    cols = NSA_REP * QT
    slab = lambda cb0: pl.BlockSpec((1, seq, LANE), lambda b, g, t: (cb0 + g, b, 0))
    return pl.pallas_call(
        functools.partial(_sel_kernel, seq=seq),
        grid=(batch, NSA_GROUPS, nsteps),
        in_specs=_q_specs(nsteps, rows) + [
            slab(CB_KS), slab(CB_VS),
            pl.BlockSpec((1, 1, LANE, rows), lambda b, g, t: (b, g, 0, t)),
            pl.BlockSpec((1, rows, LANE), lambda b, g, t: (CB_GATE + g, b * nsteps + t, 0)),
            pl.BlockSpec((1, SEL_SUB, KEY_BLK, cols), lambda b, g, t: (g, 0, 0, 0)),
            pl.BlockSpec((1, KEY_BLK, cols), lambda b, g, t: (g, 0, 0)),
            pl.BlockSpec((rows, cols), lambda b, g, t: (b * nsteps + t, g)),
            pl.BlockSpec((rows, cols), lambda b, g, t: (b * nsteps + t, g))],
        out_specs=pl.BlockSpec((rows, cols), lambda b, g, t: (b * nsteps + t, g)),
        out_shape=jax.ShapeDtypeStruct((nt, NSA_HEADS * LANE), BF16),
        scratch_shapes=[pltpu.VMEM((seq, 2 * LANE), BF16),
                        pltpu.VMEM((seq // KEY_BLK, LANE, KEY_BLK), BF16),
                        pltpu.VMEM((SEL_SUB, 2 * LANE, cols), BF16),
                        pltpu.VMEM((1, 1), F32),
                        pltpu.VMEM((SEL_SUB, 1, cols), F32), pltpu.VMEM((SEL_SUB, 1, cols), F32),
                        pltpu.VMEM((SEL_SUB, LANE, cols), F32)],
        compiler_params=_cparams(("arbitrary", "arbitrary", "arbitrary")),
        name="nsa_selected",
    )(proj, proj, proj, proj, proj, proj, sel_t, proj, nd, npv, o_c, o_w)


def _merge_kernel(oh_ref, on_ref, mgh_ref, mgn_ref, x_ref, mod_ref,
                  wh_ref, wn_ref, wo_ref, g_ref, b_ref, wr_ref, br_ref,
                  x1_ref, h2_ref, lg_ref):
    nblk = D_MODEL // LANE
    tm = x_ref.shape[0]
    halves = [slice(s * tm // MERGE_SPLIT, (s + 1) * tm // MERGE_SPLIT) for s in range(MERGE_SPLIT)]
    a_h = [_dot(oh_ref[rs, :], wh_ref[...]) for rs in halves]
    a_n = [_dot(on_ref[rs, :], wn_ref[...]) for rs in halves]
    gh = [jnp.concatenate([mgh_ref[c, rs, :] for c in range(nblk)], axis=-1) for rs in halves]
    gn = [jnp.concatenate([mgn_ref[c, rs, :] for c in range(nblk)], axis=-1) for rs in halves]
    merged = [(jax.nn.sigmoid(gh[s]) * a_h[s] + jax.nn.sigmoid(gn[s]) * a_n[s]).astype(BF16)
              for s in range(MERGE_SPLIT)]
    y = [(1.0 + mod_ref[0, 2:3, :]) * _dot(merged[s], wo_ref[...]) for s in range(MERGE_SPLIT)]
    x1 = [_layer_norm(ALPHA * x_ref[rs, :] + y[s]) * g_ref[...] + b_ref[...] for s, rs in enumerate(halves)]
    h2 = [_layer_norm(x1[s]) * (1.0 + mod_ref[0, 4:5, :]) + mod_ref[0, 3:4, :] for s in range(MERGE_SPLIT)]
    h_hi = [h2[s].astype(BF16) for s in range(MERGE_SPLIT)]
    h_lo = [(h2[s] - h_hi[s].astype(F32)).astype(BF16) for s in range(MERGE_SPLIT)]
    lg = [_dot(h_hi[s], wr_ref[0]) + _dot(h_hi[s], wr_ref[1]) + _dot(h_lo[s], wr_ref[0]) + br_ref[...]
          for s in range(MERGE_SPLIT)]
    for s, rs in enumerate(halves):
        x1_ref[rs, :] = x1[s]
        _store_rows(h2_ref, h2[s], rs.start)
        lg_ref[rs, :] = lg[s]


def _merge(o_h, o_n, proj, x2, mod, w_h, w_n, w_o, ln_g, ln_b, w_r, b_r, seq):
    nt, d = x2.shape
    tm = min(MERGE_ROWS, seq)
    nblk = d // LANE
    row = lambda w: pl.BlockSpec((tm, w), lambda i: (i, 0))
    full = lambda a: pl.BlockSpec(a.shape, lambda i: (0,) * a.ndim)
    return pl.pallas_call(
        _merge_kernel,
        grid=(nt // tm,),
        in_specs=[row(d), row(d),
                  pl.BlockSpec((nblk, tm, LANE), lambda i: (CB_MGH // nblk, i, 0)),
                  pl.BlockSpec((nblk, tm, LANE), lambda i: (CB_MGN // nblk, i, 0)),
                  row(d),
                  pl.BlockSpec((1, 6, d), lambda i: (i * tm // seq, 0, 0)),
                  full(w_h), full(w_n), full(w_o), full(ln_g), full(ln_b), full(w_r), full(b_r)],
        out_specs=[row(d), pl.BlockSpec((tm * ROW_TILES, LANE), lambda i: (i, 0)), row(LANE)],
        out_shape=[jax.ShapeDtypeStruct((nt, d), F32),
                   jax.ShapeDtypeStruct((nt * ROW_TILES, LANE), F32),
                   jax.ShapeDtypeStruct((nt, LANE), F32)],
        compiler_params=_cparams(("arbitrary",)),
        name="merge_outproj",
    )(o_h, o_n, proj, proj, x2, mod, w_h, w_n, w_o, ln_g, ln_b, w_r, b_r)


def _route_kernel(lg_ref, rec_ref, cnt_ref, rect_ref, carry_ref, *, tm):
    @pl.when(pl.program_id(0) == 0)
    def _():
        carry_ref[...] = jnp.zeros_like(carry_ref)

    lg = lg_ref[...]
    lane = lax.broadcasted_iota(jnp.int32, (tm, LANE), 1).astype(F32)
    far = float(LANE)
    gmask = lane < N_GROUPS
    gl = jnp.where(gmask, lg, -jnp.inf)
    gmax = jnp.max(gl, axis=-1, keepdims=True)
    gsum = jnp.sum(jnp.where(gmask, jnp.exp(gl - gmax), 0.0), axis=-1, keepdims=True)
    grp_p = 1.0 / gsum
    gidx = jnp.min(jnp.where(gl == gmax, lane, far), axis=-1, keepdims=True)
    lo = N_GROUPS + EXP_PER_GROUP * gidx
    emask = (lane >= lo) & (lane < lo + EXP_PER_GROUP)
    el = jnp.where(emask, lg, -jnp.inf)
    m1 = jnp.max(el, axis=-1, keepdims=True)
    i1 = jnp.min(jnp.where(el == m1, lane, far), axis=-1, keepdims=True)
    el2 = jnp.where(lane == i1, -jnp.inf, el)
    m2 = jnp.max(el2, axis=-1, keepdims=True)
    i2 = jnp.min(jnp.where(emask & (lane != i1) & (el2 == m2), lane, far), axis=-1, keepdims=True)
    e = jnp.exp(m2 - m1)
    w0 = grp_p / (1.0 + e)
    w1 = grp_p * e / (1.0 + e)

    oh0 = lane == i1
    oh1 = lane == i2
    f0 = jnp.where(oh0, 1.0, 0.0)
    f1 = jnp.where(oh1, 1.0, 0.0)
    ri = lax.broadcasted_iota(jnp.int32, (tm, tm), 0)
    ci = lax.broadcasted_iota(jnp.int32, (tm, tm), 1)
    before = jnp.where(ci < ri, 1.0, 0.0).astype(BF16)
    cum0 = _dot(before, f0.astype(BF16))
    cum1 = _dot(before, f1.astype(BF16))
    tot0 = jnp.sum(f0, axis=0, keepdims=True)
    tot1 = jnp.sum(f1, axis=0, keepdims=True)
    carry = carry_ref[...]
    rank0 = jnp.sum(jnp.where(oh0, carry + cum0, 0.0), axis=-1, keepdims=True)
    rank1 = jnp.sum(jnp.where(oh1, carry + tot0 + cum1, 0.0), axis=-1, keepdims=True)
    carry = carry + tot0 + tot1
    carry_ref[...] = carry
    cnt_ref[...] = carry

    rec = jnp.where(lane == 0, i1 - N_GROUPS, 0.0)
    rec = jnp.where(lane == 1, i2 - N_GROUPS, rec)
    rec = jnp.where(lane == 2, w0, rec)
    rec = jnp.where(lane == 3, w1, rec)
    rec = jnp.where(lane == 4, rank0, rec)
    rec = jnp.where(lane == 5, rank1, rec)
    rec_ref[...] = rec
    rect_ref[...] = jnp.concatenate([rec[c * LANE:(c + 1) * LANE, :].T[0:8, :] for c in range(tm // LANE)], axis=1)


def _route(logits):
    nt = logits.shape[0]
    tm = min(ROUTE_ROWS, nt)
    return pl.pallas_call(
        functools.partial(_route_kernel, tm=tm),
        grid=(nt // tm,),
        in_specs=[pl.BlockSpec((tm, LANE), lambda i: (i, 0))],
        out_specs=[pl.BlockSpec((tm, LANE), lambda i: (i, 0)),
                   pl.BlockSpec((1, LANE), lambda i: (0, 0)),
                   pl.BlockSpec((8, tm), lambda i: (0, i))],
        out_shape=[jax.ShapeDtypeStruct((nt, LANE), F32),
                   jax.ShapeDtypeStruct((1, LANE), F32),
                   jax.ShapeDtypeStruct((8, nt), F32)],
        scratch_shapes=[pltpu.VMEM((1, LANE), F32)],
        compiler_params=_cparams(("arbitrary",)),
        name="moe_route",
    )(logits)


def _row_copy(src, dst, sem):
    return pltpu.make_async_copy(src, dst, sem)


def _tile_of(r):
    return pl.ds(pl.multiple_of(r * ROW_TILES, ROW_TILES), ROW_TILES)


def _load_rows(ref, n, lead=()):
    return jnp.concatenate([ref[lead + (pl.ds(c, n, stride=ROW_TILES), slice(None))]
                            for c in range(ROW_TILES)], axis=1)


def _store_rows(ref, val, row0=0):
    n = val.shape[0]
    for c in range(ROW_TILES):
        ref[pl.ds(row0 * ROW_TILES + c, n, stride=ROW_TILES), :] = val[:, c * LANE:(c + 1) * LANE]


def _dispatch_kernel(dest0_ref, dest1_ref, zb_ref, h_ref, xp_ref, z_ref, sem, zsem, *, tm):
    step = pl.program_id(0)
    base = step * tm
    blk = MOE_ROWS * ROW_TILES

    @pl.when(step == 0)
    def _():
        z_ref[...] = jnp.zeros_like(z_ref)

        def zero_copy(j):
            b = jnp.maximum(zb_ref[j], 0)
            return _row_copy(z_ref, xp_ref.at[pl.ds(pl.multiple_of(b * blk, blk), blk), :], zsem)

        def start(j, carry):
            @pl.when(zb_ref[j] >= 0)
            def _():
                zero_copy(j).start()
            return carry

        def wait(j, carry):
            @pl.when(zb_ref[j] >= 0)
            def _():
                zero_copy(j).wait()
            return carry

        lax.fori_loop(0, 2 * N_EXPERTS, start, 0)
        lax.fori_loop(0, 2 * N_EXPERTS, wait, 0)

    def issue(r, carry):
        for dest_ref in (dest0_ref, dest1_ref):
            d = dest_ref[base + r]
            _row_copy(h_ref.at[_tile_of(r), :], xp_ref.at[_tile_of(d), :], sem).start()
        return carry

    lax.fori_loop(0, tm, issue, 0, unroll=ROW_DMA_UNROLL)
    for _ in range(2):
        _row_copy(h_ref, xp_ref.at[pl.ds(0, tm * ROW_TILES), :], sem).wait()


def _dispatch(dest0, dest1, zero_blocks, h2, n_blocks):
    nt = h2.shape[0] // ROW_TILES
    tm = min(DISPATCH_TOKENS, nt)
    return pl.pallas_call(
        functools.partial(_dispatch_kernel, tm=tm),
        grid_spec=pltpu.PrefetchScalarGridSpec(
            num_scalar_prefetch=3,
            grid=(nt // tm,),
            in_specs=[pl.BlockSpec((tm * ROW_TILES, LANE), lambda i, *_: (i, 0))],
            out_specs=pl.BlockSpec(memory_space=pl.ANY),
            scratch_shapes=[pltpu.VMEM((MOE_ROWS * ROW_TILES, LANE), F32),
                            pltpu.SemaphoreType.DMA(()), pltpu.SemaphoreType.DMA(())]),
        out_shape=jax.ShapeDtypeStruct((n_blocks * MOE_ROWS * ROW_TILES, LANE), F32),
        compiler_params=_cparams(("arbitrary",)),
        name="moe_dispatch",
    )(dest0, dest1, zero_blocks, h2)


def _expert_kernel(be_ref, nu_ref, x_ref, w1_ref, w3_ref, w2_ref, y_ref, w1b_ref, w3b_ref, w2b_ref):
    i = pl.program_id(0)

    @pl.when((i < nu_ref[0]) & ((i == 0) | (be_ref[i] != be_ref[jnp.maximum(i - 1, 0)])))
    def _():
        w1b_ref[...] = w1_ref[0].astype(BF16)
        w3b_ref[...] = w3_ref[0].astype(BF16)
        w2b_ref[...] = w2_ref[0].astype(BF16)

    @pl.when(i < nu_ref[0])
    def _():
        xb = _load_rows(x_ref, MOE_ROWS).astype(BF16)
        a = _dot(xb, w1b_ref[...])
        b = _dot(xb, w3b_ref[...])
        hmid = (a * jax.nn.sigmoid(a) * b).astype(BF16)
        _store_rows(y_ref, _dot(hmid, w2b_ref[...]))

    @pl.when(i >= nu_ref[0])
    def _():
        y_ref[...] = jnp.zeros_like(y_ref)


def _experts(block_expert, n_used, x_pad, w1, w3, w2):
    d, de = w1.shape[1], w1.shape[2]
    nb = x_pad.shape[0] // (MOE_ROWS * ROW_TILES)
    return pl.pallas_call(
        _expert_kernel,
        grid_spec=pltpu.PrefetchScalarGridSpec(
            num_scalar_prefetch=2,
            grid=(nb,),
            in_specs=[pl.BlockSpec((MOE_ROWS * ROW_TILES, LANE), lambda i, be, nu: (i, 0)),
                      pl.BlockSpec((1, d, de), lambda i, be, nu: (be[i], 0, 0)),
                      pl.BlockSpec((1, d, de), lambda i, be, nu: (be[i], 0, 0)),
                      pl.BlockSpec((1, de, d), lambda i, be, nu: (be[i], 0, 0))],
            out_specs=pl.BlockSpec((MOE_ROWS * ROW_TILES, LANE), lambda i, be, nu: (i, 0)),
            scratch_shapes=[pltpu.VMEM((d, de), BF16), pltpu.VMEM((d, de), BF16), pltpu.VMEM((de, d), BF16)]),
        out_shape=jax.ShapeDtypeStruct(x_pad.shape, F32),
        compiler_params=_cparams(("arbitrary",)),
        name="moe_experts",
    )(block_expert, n_used, x_pad, w1, w3, w2)


def _combine_kernel(dest0_ref, dest1_ref, yp_ref, rec_ref, x1_ref, mod_ref, g_ref, b_ref, o_ref,
                    buf_ref, sem, *, tm):
    step = pl.program_id(0)
    slot = step % 2

    def fetch(tile, to_slot):
        def issue(r, carry):
            for k, dest_ref in enumerate((dest0_ref, dest1_ref)):
                d = dest_ref[tile * tm + r]
                _row_copy(yp_ref.at[_tile_of(d), :], buf_ref.at[to_slot, k, _tile_of(r), :],
                          sem.at[to_slot]).start()
            return carry

        lax.fori_loop(0, tm, issue, 0, unroll=ROW_DMA_UNROLL)

    @pl.when(step == 0)
    def _():
        fetch(0, 0)

    @pl.when(step + 1 < pl.num_programs(0))
    def _():
        fetch(step + 1, 1 - slot)

    for k in range(2):
        _row_copy(yp_ref.at[pl.ds(0, tm * ROW_TILES), :], buf_ref.at[slot, k], sem.at[slot]).wait()

    rec = rec_ref[...]
    y = rec[:, 2:3] * _load_rows(buf_ref, tm, (slot, 0)) + rec[:, 3:4] * _load_rows(buf_ref, tm, (slot, 1))
    y = (1.0 + mod_ref[0, 5:6, :]) * y
    o_ref[...] = _layer_norm(ALPHA * x1_ref[...] + y) * g_ref[...] + b_ref[...]


def _combine(dest0, dest1, y_pad, rec, x1, mod, ln_g, ln_b, seq):
    nt, d = x1.shape
    tm = min(COMBINE_TOKENS, seq)
    return pl.pallas_call(
        functools.partial(_combine_kernel, tm=tm),
        grid_spec=pltpu.PrefetchScalarGridSpec(
            num_scalar_prefetch=2,
            grid=(nt // tm,),
            in_specs=[pl.BlockSpec(memory_space=pl.ANY),
                      pl.BlockSpec((tm, LANE), lambda i, *_: (i, 0)),
                      pl.BlockSpec((tm, d), lambda i, *_: (i, 0)),
                      pl.BlockSpec((1, 6, d), lambda i, *_: (i * tm // seq, 0, 0)),
                      pl.BlockSpec((1, d), lambda i, *_: (0, 0)),
                      pl.BlockSpec((1, d), lambda i, *_: (0, 0))],
            out_specs=pl.BlockSpec((tm, d), lambda i, *_: (i, 0)),
            scratch_shapes=[pltpu.VMEM((2, 2, tm * ROW_TILES, LANE), F32), pltpu.SemaphoreType.DMA((2,))]),
        out_shape=jax.ShapeDtypeStruct((nt, d), F32),
        compiler_params=_cparams(("arbitrary",)),
        name="moe_combine",
    )(dest0, dest1, y_pad, rec, x1, mod, ln_g, ln_b)


def _rel_bucket(dist):
    n = jnp.maximum(dist, 0)
    max_exact = REL_BUCKETS // 2
    nf = jnp.maximum(n, 1).astype(F32)
    large = max_exact + (jnp.log(nf / max_exact) / math.log(REL_MAX_DIST / max_exact)
                         * (REL_BUCKETS - max_exact)).astype(jnp.int32)
    large = jnp.minimum(large, REL_BUCKETS - 1)
    return jnp.where(n < max_exact, n, large)


def _bias_tables(rel_bias, seq):
    bucket_onehot = (_rel_bucket(jnp.arange(LANE))[:, None] == jnp.arange(REL_BUCKETS)).astype(F32)
    tab_d = jnp.einsum('db,hb->hd', bucket_onehot, rel_bias,
                       precision=lax.Precision.HIGHEST)
    tok = np.arange(QT)[None, :]
    key = np.arange(LANE)[:, None]
    far = tab_d[:, LANE - 1]
    cols = NSA_REP * QT

    def transposed(dist):
        idx = jnp.asarray(np.clip(dist, 0, LANE - 1).astype(np.int32))
        onehot = (idx[..., None] == jnp.arange(LANE, dtype=jnp.int32)).astype(F32)
        t = jnp.einsum('ijd,hd->hij', onehot, tab_d, precision=lax.Precision.HIGHEST)
        t = (t - far[:, None, None]) * LOG2E
        t = t.reshape(NSA_GROUPS, NSA_REP, LANE, QT).transpose(0, 2, 1, 3)
        return t.reshape(NSA_GROUPS, LANE, cols)

    t0t = transposed(tok - key)
    t1t = transposed(tok - key + QT)

    ns = seq // CMP_STRIDE
    d_c = tok - CMP_STRIDE * key + (CMP_STRIDE * CMP_PAD - (CMP_BLOCK - 1))
    seen = np.tile(d_c >= 0, (1, NSA_REP))
    recent = jnp.where(seen[None], transposed(d_c), NEG)
    tt = jnp.concatenate([jnp.zeros((NSA_GROUPS, ns, cols), F32), recent,
                          jnp.full((NSA_GROUPS, ns, cols), NEG, F32)], axis=1)

    rho = np.arange(WINDOW + QT)[:, None]
    tok_w = np.tile(np.arange(QT), NSA_REP)[None, :]
    band = (rho > tok_w) & (rho <= tok_w + WINDOW)
    rows = jnp.concatenate([jnp.zeros((NSA_GROUPS, WINDOW - QT, NSA_REP * QT), F32), t1t, t0t], axis=1)
    wb = jnp.where(band[None], rows, NEG)

    zeros = lambda n: jnp.zeros((NSA_GROUPS, n * QT, cols), F32)
    negs = lambda n: jnp.full((NSA_GROUPS, n * QT, cols), NEG, F32)
    diag = jnp.where(np.tile(tok >= key, (1, NSA_REP))[None], t0t, NEG)
    nd = jnp.stack([jnp.concatenate(([zeros(u - 1), t1t] if u else []) + [diag, negs(SEL_SUB - 1 - u)], axis=1)
                    for u in range(SEL_SUB)], axis=1)
    npv = jnp.concatenate([zeros(SEL_SUB - 1), t1t], axis=1)
    return nd, npv, wb, tt


def _overlap_matrix(seq):
    ns = seq // CMP_STRIDE
    nslc = seq // SLC_BLOCK
    ov = np.zeros((LANE, ns), np.float32)
    cs = np.arange(ns - 1) * CMP_STRIDE
    ss = np.arange(nslc) * SLC_BLOCK
    ov[:nslc, :ns - 1] = ((cs[None, :] < ss[:, None] + SLC_BLOCK) & (cs[None, :] + CMP_BLOCK > ss[:, None]))
    return jnp.asarray(ov, BF16)


def _reorder_cols(a):
    lead = a.shape[:-1]
    gate = a[..., MAIN_COLS:MAIN_COLS + GATE_COLS]
    per = GATE_COLS // NSA_GROUPS
    gate_blocks = []
    for g in range(NSA_GROUPS):
        gate_blocks.append(gate[..., g * per:(g + 1) * per])
        gate_blocks.append(jnp.zeros(lead + (LANE - per,), a.dtype))
    pad = jnp.zeros(lead + ((CB_MGH - CB_GATE - NSA_GROUPS) * LANE,), a.dtype)
    return jnp.concatenate([a[..., :MAIN_COLS]] + gate_blocks + [pad, a[..., MAIN_COLS + GATE_COLS:]], axis=-1)


def kernel(x, c, ada_w, ada_b, w_in, b_in, hg_lb_logits, hg_norm_w, cmp_pos_k, cmp_w1_k, cmp_b1_k, cmp_w2_k, cmp_pos_v, cmp_w1_v, cmp_b1_v, cmp_w2_v, rel_bias, w_br_hg, w_br_nsa, w_out, ln1_g, ln1_b, router_grp_w, router_grp_b, router_exp_w, router_exp_b, exp_w1, exp_w3, exp_w2, ln2_g, ln2_b):
    batch, seq, d = x.shape
    nt = batch * seq
    assert d == D_MODEL and seq % INPROJ_ROWS == 0 and seq // SLC_BLOCK <= LANE
    l = 0
    x2 = x.reshape(nt, d)

    c_pad = jnp.zeros((8, d), F32).at[:batch].set(c)
    mod = _adaln(c_pad, ada_w[l], ada_b[l][None])[:batch].reshape(batch, 6, d)

    proj = _inproj(x2, mod, _reorder_cols(w_in[l].astype(BF16)), _reorder_cols(b_in[l])[None], seq)

    lb_all = jnp.cumsum(jax.nn.softmax(hg_lb_logits.astype(F32), axis=0), axis=0)
    o_h = _hgrn(proj, lb_all[l][None], hg_norm_w[l][None], batch, seq)

    kvc = _compress(proj, jnp.stack([cmp_pos_k[l], cmp_pos_v[l]]),
                    jnp.stack([cmp_w1_k[l], cmp_w1_v[l]]).astype(BF16),
                    jnp.stack([cmp_b1_k[l], cmp_b1_v[l]])[:, None, :],
                    jnp.stack([cmp_w2_k[l], cmp_w2_v[l]]).astype(BF16), batch, seq)

    nd, npv, wb, tt = _bias_tables(rel_bias, seq)
    o_c, sel_t = _cmp_sel(proj, kvc, tt, _overlap_matrix(seq), batch, seq)
    o_w = _window(proj, wb, batch, seq)
    o_n = _selected(proj, sel_t, nd, npv, o_c, o_w, batch, seq)

    w_r = jnp.zeros((d, LANE), F32).at[:, :N_GROUPS].set(router_grp_w[l])
    w_r = w_r.at[:, N_GROUPS:N_GROUPS + N_EXPERTS].set(router_exp_w[l])
    w_r_hi = w_r.astype(BF16)
    w_r_lo = (w_r - w_r_hi.astype(F32)).astype(BF16)
    b_r = jnp.zeros((1, LANE), F32).at[0, :N_GROUPS].set(router_grp_b[l])
    b_r = b_r.at[0, N_GROUPS:N_GROUPS + N_EXPERTS].set(router_exp_b[l])
    x1, h2, logits = _merge(o_h, o_n, proj, x2, mod,
                            w_br_hg[l].astype(BF16), w_br_nsa[l].astype(BF16), w_out[l].astype(BF16),
                            ln1_g[l][None], ln1_b[l][None], jnp.stack([w_r_hi, w_r_lo]), b_r, seq)

    rec, cnt, rec_t = _route(logits)
    counts = cnt[0, N_GROUPS:N_GROUPS + N_EXPERTS].astype(jnp.int32)
    padded = (counts + MOE_ROWS - 1) // MOE_ROWS * MOE_ROWS
    pend = jnp.cumsum(padded)
    pstart = pend - padded
    n_assign = 2 * nt
    nb = n_assign // MOE_ROWS + N_EXPERTS
    slots = rec_t.astype(jnp.int32)
    expert_ids = jnp.arange(N_EXPERTS, dtype=jnp.int32)[:, None]
    slot_base = lambda e: jnp.sum(jnp.where(e[None, :] == expert_ids, pstart[:, None], 0), axis=0)
    dest0 = slot_base(slots[0]) + slots[4]
    dest1 = slot_base(slots[1]) + slots[5]
    block_start = jnp.arange(nb, dtype=jnp.int32) * MOE_ROWS
    block_expert = jnp.minimum(jnp.sum(pend[None, :] <= block_start[:, None], axis=1),
                               N_EXPERTS - 1).astype(jnp.int32)
    n_used = pend[-1] // MOE_ROWS
    spare = n_used + jnp.arange(N_EXPERTS, dtype=jnp.int32)
    zero_blocks = jnp.concatenate([jnp.where(padded > 0, pend // MOE_ROWS - 1, -1),
                                   jnp.where(spare < nb, spare, -1)]).astype(jnp.int32)

    x_pad = _dispatch(dest0, dest1, zero_blocks, h2, nb)
    y_pad = _experts(block_expert, n_used[None].astype(jnp.int32), x_pad, exp_w1[l], exp_w3[l], exp_w2[l])
    out = _combine(dest0, dest1, y_pad, rec, x1, mod, ln2_g[l][None], ln2_b[l][None], seq)
    return out.reshape(batch, seq, d)
```

```python
import functools
import math

import numpy as np
import jax
import jax.numpy as jnp
from jax import lax
from jax.experimental import pallas as pl
from jax.experimental.pallas import tpu as pltpu

F32 = jnp.float32
BF16 = jnp.bfloat16

D_MODEL = 1024
HG_HEADS = 8
HG_DK = 128
HG_DV = 128
HG_CHUNK = 32
HG_SUPER = 256
NSA_HEADS = 8
NSA_GROUPS = 2
NSA_REP = NSA_HEADS // NSA_GROUPS
NSA_DK = 128
CMP_BLOCK = 32
CMP_STRIDE = 16
SLC_BLOCK = 64
SLC_TOPK = 16
WINDOW = 512
FORCE_SCORE = 1e4
N_FORCED = 3
REL_BUCKETS = 32
REL_MAX_DIST = 128
N_GROUPS = 4
EXP_PER_GROUP = 8
N_EXPERTS = N_GROUPS * EXP_PER_GROUP
D_EXPERT = D_MODEL // 2
DEPTH = 1
ALPHA = (2 * DEPTH) ** 0.25

LANE = 128
QT = 128
NEG = -1e30
SCALE = NSA_DK ** -0.5
LOG2E = math.log2(math.e)
SCALE_LOG2 = SCALE * LOG2E
KEY_BLK = 512
BOUND_SLACK = 1.0 + 2.0 ** -10
MAX_REF_GAP = 64.0
SEL_ROWS = 16
SEL_SUB = KEY_BLK // QT
WIN_SUB = 4
CMP_SUB = 4
CMP_PAD = 120
VMEM_LIMIT = 56 * 1024 * 1024

CB_HQ, CB_HF, CB_HI, CB_HG = 0, 8, 16, 24
CB_NQ = 32
CB_KC, CB_VC, CB_KS, CB_VS, CB_KW, CB_VW = 40, 42, 44, 46, 48, 50
CB_GATE = 52
CB_MGH, CB_MGN = 56, 64
NCB = 72
MAIN_COLS = 52 * LANE
GATE_COLS = 3 * NSA_HEADS

INPROJ_ROWS = 2048
INPROJ_COL_BLOCKS = 8
HGRN_ROWS = 512
MERGE_ROWS = 512
ROUTE_ROWS = 512
DISPATCH_TOKENS = 2048
COMBINE_TOKENS = 256
MERGE_SPLIT = 2
ROW_TILES = D_MODEL // LANE
MOE_ROWS = 512
ROW_DMA_UNROLL = 8


def _cparams(sem):
    return pltpu.CompilerParams(dimension_semantics=sem, vmem_limit_bytes=VMEM_LIMIT)


def _dot(a, b):
    return jnp.dot(a, b, preferred_element_type=F32)


def _dot_nt(a, b):
    return lax.dot_general(a, b, (((1,), (1,)), ((), ())), preferred_element_type=F32)


def _dot_tn(a, b):
    return lax.dot_general(a, b, (((0,), (0,)), ((), ())), preferred_element_type=F32)


def _split3(x):
    hi = x.astype(BF16)
    r = x - hi.astype(F32)
    mid = r.astype(BF16)
    lo = (r - mid.astype(F32)).astype(BF16)
    return hi, mid, lo


def _dot01(m01, x):
    hi, mid, lo = _split3(x)
    return _dot(m01, hi) + _dot(m01, mid) + _dot(m01, lo)


def _layer_norm(x, eps=1e-5):
    mu = jnp.mean(x, axis=-1, keepdims=True)
    xc = x - mu
    var = jnp.mean(xc * xc, axis=-1, keepdims=True)
    return xc * lax.rsqrt(var + eps)


def _adaln_kernel(c_ref, w_ref, b_ref, o_ref):
    c = c_ref[...]
    ca = c * jax.nn.sigmoid(c)
    o_ref[...] = jnp.dot(ca, w_ref[...], precision=lax.Precision.HIGHEST,
                         preferred_element_type=F32) + b_ref[...]


def _adaln(c_pad, w, b):
    rows, d = c_pad.shape
    n = w.shape[1]
    return pl.pallas_call(
        _adaln_kernel,
        grid=(n // d,),
        in_specs=[pl.BlockSpec((rows, d), lambda j: (0, 0)),
                  pl.BlockSpec((d, d), lambda j: (0, j)),
                  pl.BlockSpec((1, d), lambda j: (0, j))],
        out_specs=pl.BlockSpec((rows, d), lambda j: (0, j)),
        out_shape=jax.ShapeDtypeStruct((rows, n), F32),
        compiler_params=_cparams(("arbitrary",)),
        name="adaln",
    )(c_pad, w, b)


def _inproj_kernel(x_ref, mod_ref, w_ref, b_ref, o_ref, hn_ref, *, ncb_tile):
    @pl.when(pl.program_id(1) == 0)
    def _():
        hn = _layer_norm(x_ref[...])
        sh = mod_ref[0, 0:1, :]
        sc = mod_ref[0, 1:2, :]
        hn_ref[...] = (hn * (1.0 + sc) + sh).astype(BF16)

    res = _dot(hn_ref[...], w_ref[...]) + b_ref[...]
    for c in range(ncb_tile):
        o_ref[c] = res[:, c * LANE:(c + 1) * LANE]


def _inproj(x2, mod, w, b, seq):
    nt, d = x2.shape
    tm = min(INPROJ_ROWS, seq)
    ncb_tile = INPROJ_COL_BLOCKS
    tn = ncb_tile * LANE
    return pl.pallas_call(
        functools.partial(_inproj_kernel, ncb_tile=ncb_tile),
        grid=(nt // tm, NCB // ncb_tile),
        in_specs=[pl.BlockSpec((tm, d), lambda i, j: (i, 0)),
                  pl.BlockSpec((1, 6, d), lambda i, j: (i * tm // seq, 0, 0)),
                  pl.BlockSpec((d, tn), lambda i, j: (0, j)),
                  pl.BlockSpec((1, tn), lambda i, j: (0, j))],
        out_specs=pl.BlockSpec((ncb_tile, tm, LANE), lambda i, j: (j, i, 0)),
        out_shape=jax.ShapeDtypeStruct((NCB, nt, LANE), F32),
        scratch_shapes=[pltpu.VMEM((tm, d), BF16)],
        compiler_params=_cparams(("arbitrary", "arbitrary")),
        name="inproj",
    )(x2, mod, w, b)


def _hgrn_kernel(q_ref, f_ref, v_ref, g_ref, lb_ref, nw_ref, o_ref, st_ref, *, rows):
    @pl.when(pl.program_id(1) == 0)
    def _():
        st_ref[...] = jnp.zeros_like(st_ref)

    sup = HG_SUPER
    ri = lax.broadcasted_iota(jnp.int32, (sup, sup), 0)
    ci = lax.broadcasted_iota(jnp.int32, (sup, sup), 1)
    same = (ri // HG_CHUNK) == (ci // HG_CHUNK)
    cum_m = jnp.where(same & (ci <= ri), 1.0, 0.0).astype(BF16)
    rt = lax.broadcasted_iota(jnp.int32, (LANE, LANE), 0)
    ct = lax.broadcasted_iota(jnp.int32, (LANE, LANE), 1)
    tril = ((rt // HG_CHUNK) == (ct // HG_CHUNK)) & (ct <= rt)
    per = sup // HG_CHUNK
    groups = [slice(g * LANE, (g + 1) * LANE) for g in range(sup // LANE)]

    heads = range(HG_HEADS)
    hs = [slice(h * LANE, (h + 1) * LANE) for h in heads]

    def wide(ref, r0):
        return jnp.concatenate([ref[h, pl.ds(r0, sup), :] for h in heads], axis=1)

    def body(i, carry):
        r0 = pl.multiple_of(i * sup, sup)
        lb = lb_ref[...]
        f = lb + (1.0 - lb) * jax.nn.sigmoid(wide(f_ref, r0))
        lf = jnp.log(f)
        k = 1.0 - f
        hi = lf.astype(BF16)
        lo = (lf - hi.astype(F32)).astype(BF16)
        b = _dot(cum_m, hi) + _dot(cum_m, lo)
        chunk = [slice(c * HG_CHUNK, (c + 1) * HG_CHUNK) for c in range(per)]
        dec = [jnp.exp(b[(c + 1) * HG_CHUNK - 1:(c + 1) * HG_CHUNK, :]) for c in range(per)]
        dec_rows = jnp.concatenate([jnp.broadcast_to(dec[c], (HG_CHUNK, dec[c].shape[1])) for c in range(per)],
                                   axis=0)
        q_in = (wide(q_ref, r0) * jnp.exp(b)).astype(BF16)
        k_dec = k * jnp.exp(-b)
        k_in = k_dec.astype(BF16)
        k_end = (k_dec * dec_rows).astype(BF16)
        vb = wide(v_ref, r0).astype(BF16)

        att = [[jnp.where(tril, _dot_nt(q_in[g, hs[h]], k_in[g, hs[h]]), 0.0).astype(BF16) for g in groups]
               for h in heads]
        upd = [[_dot_tn(vb[chunk[c], hs[h]], k_end[chunk[c], hs[h]]) for c in range(per)] for h in heads]
        intra = [[_dot(att[h][n], vb[g, hs[h]]) for n, g in enumerate(groups)] for h in heads]

        st = [st_ref[h] for h in heads]
        inter = [[] for _ in heads]
        for c in range(per):
            for h in heads:
                inter[h].append(_dot_nt(q_in[chunk[c], hs[h]], st[h].astype(BF16)))
                st[h] = dec[c][:, hs[h]] * st[h] + upd[h][c]
        for h in heads:
            st_ref[h] = st[h]

        o = [jnp.concatenate(intra[h], axis=0) + jnp.concatenate(inter[h], axis=0) for h in heads]
        scale = [lax.rsqrt(jnp.mean(o[h] * o[h], axis=-1, keepdims=True) + 1e-6) for h in heads]
        on = jnp.concatenate([o[h] * scale[h] for h in heads], axis=1)
        on = on * nw_ref[...] * jax.nn.sigmoid(wide(g_ref, r0))
        o_ref[pl.ds(r0, sup), :] = on.astype(BF16)
        return carry

    lax.fori_loop(0, rows // sup, body, 0)


def _hgrn(proj, lb, nw, batch, seq):
    nt = proj.shape[1]
    tb = min(HGRN_ROWS, seq)
    nblk = seq // tb

    def slab(cb0):
        return pl.BlockSpec((HG_HEADS, tb, LANE), lambda b, t: (cb0 // HG_HEADS, b * nblk + t, 0))

    vec = pl.BlockSpec((1, HG_HEADS * LANE), lambda b, t: (0, 0))
    return pl.pallas_call(
        functools.partial(_hgrn_kernel, rows=tb),
        grid=(batch, nblk),
        in_specs=[slab(CB_HQ), slab(CB_HF), slab(CB_HI), slab(CB_HG), vec, vec],
        out_specs=pl.BlockSpec((tb, HG_HEADS * HG_DV), lambda b, t: (b * nblk + t, 0)),
        out_shape=jax.ShapeDtypeStruct((nt, HG_HEADS * HG_DV), BF16),
        scratch_shapes=[pltpu.VMEM((HG_HEADS, HG_DV, HG_DK), F32)],
        compiler_params=_cparams(("arbitrary", "arbitrary")),
        name="hgrn2",
    )(proj, proj, proj, proj, lb, nw)


def _compress_kernel(x_ref, pos_ref, w1_ref, b1_ref, w2_ref, o_ref, *, ns):
    p0 = jnp.zeros((ns, LANE), F32)
    p1 = jnp.zeros((ns, LANE), F32)
    for j in range(CMP_STRIDE):
        tok = x_ref[0, pl.ds(j, ns, stride=CMP_STRIDE), :]
        rows = slice(j * LANE, (j + 1) * LANE)
        late = slice((CMP_STRIDE + j) * LANE, (CMP_STRIDE + j + 1) * LANE)
        p0 = p0 + _dot((tok + pos_ref[0, j:j + 1, :]).astype(BF16), w1_ref[0, rows, :])
        p1 = p1 + _dot((tok + pos_ref[0, CMP_STRIDE + j:CMP_STRIDE + j + 1, :]).astype(BF16), w1_ref[0, late, :])
    h = p0 + pltpu.roll(p1, ns - 1, axis=0) + b1_ref[0]
    a = h * jax.nn.sigmoid(h)
    out = _dot(a.astype(BF16), w2_ref[0])
    row = lax.broadcasted_iota(jnp.int32, out.shape, 0)
    out = jnp.where(row < ns - 1, out, 0.0)
    o_ref[0, 0, 0] = out


def _compress(proj, pos, w1, b1, w2, batch, seq):
    ns = seq // CMP_STRIDE
    np_rows = ns
    width = CMP_STRIDE * LANE
    return pl.pallas_call(
        functools.partial(_compress_kernel, ns=ns),
        grid=(2, batch, NSA_GROUPS),
        in_specs=[pl.BlockSpec((1, seq, LANE), lambda s, b, g: (CB_KC + NSA_GROUPS * s + g, b, 0)),
                  pl.BlockSpec((1, CMP_BLOCK, LANE), lambda s, b, g: (s, 0, 0)),
                  pl.BlockSpec((1, 2 * width, LANE), lambda s, b, g: (s, 0, 0)),
                  pl.BlockSpec((1, 1, LANE), lambda s, b, g: (s, 0, 0)),
                  pl.BlockSpec((1, LANE, LANE), lambda s, b, g: (s, 0, 0))],
        out_specs=pl.BlockSpec((1, 1, 1, np_rows, LANE), lambda s, b, g: (s, b, g, 0, 0)),
        out_shape=jax.ShapeDtypeStruct((2, batch, NSA_GROUPS, np_rows, LANE), F32),
        compiler_params=_cparams(("arbitrary", "arbitrary", "arbitrary")),
        name="nsa_compress",
    )(proj, pos, w1, b1, w2)


def _cmp_sel_kernel(q0, q1, q2, q3, kc_ref, vc_ref, gate_ref, tt_ref, ov_ref, oc_ref, sel_ref,
                    kb_ref, vt_ref, imp_ref, *, np_rows):
    step = pl.program_id(2)
    cols = NSA_REP * QT
    ns = np_rows

    @pl.when(step == 0)
    def _():
        kb_ref[...] = (kc_ref[0, 0, 0] * SCALE_LOG2).astype(BF16)
        for c in range(np_rows // LANE):
            vt_ref[:, c * LANE:(c + 1) * LANE] = vc_ref[0, 0, 0, c * LANE:(c + 1) * LANE, :].T.astype(BF16)

    subs = range(CMP_SUB)
    tis = [step * CMP_SUB + u for u in subs]
    qts = [_q_transposed((q0, q1, q2, q3), u) for u in subs]

    def group_sum(p):
        tot = p[:, 0:QT]
        for r in range(1, NSA_REP):
            tot = tot + p[:, r * QT:(r + 1) * QT]
        return tot

    def attend(nrows):
        kb = kb_ref[0:nrows, :]
        bias = [tt_ref[0, pl.ds(pl.multiple_of(ns + CMP_PAD - tis[u] * (QT // CMP_STRIDE), 8), nrows), :]
                for u in subs]
        ss = [_dot(kb, qts[u]) + bias[u] for u in subs]
        ms = [jnp.max(ss[u], axis=0, keepdims=True) for u in subs]
        ps = [jnp.exp2(ss[u] - ms[u]) for u in subs]
        ls = [jnp.sum(ps[u], axis=0, keepdims=True) for u in subs]
        invs = [jnp.where(ms[u] > 0.5 * NEG, 1.0 / ls[u], 0.0) for u in subs]
        pn = [ps[u] * invs[u] for u in subs]
        vt = vt_ref[:, 0:nrows]
        os_ = [_dot(vt, pn[u].astype(BF16)) for u in subs]
        for u in subs:
            rows = slice(u * QT, (u + 1) * QT)
            gt = jax.nn.sigmoid(gate_ref[0, rows, :])
            for r in range(NSA_REP):
                oc_ref[rows, r * LANE:(r + 1) * LANE] = gt[:, 3 * r:3 * r + 1] * os_[u][:, r * QT:(r + 1) * QT].T
            imp_ref[:, rows] = _dot01(ov_ref[:, 0:nrows], group_sum(pn[u]))

    visible = (step + 1) * CMP_SUB * (QT // CMP_STRIDE)
    ngroups = np_rows // LANE
    for g in range(1, ngroups + 1):
        upper = visible <= g * LANE if g < ngroups else True

        @pl.when((visible > (g - 1) * LANE) & upper)
        def _():
            attend(g * LANE)

    imp = imp_ref[...]
    width = CMP_SUB * QT
    jj = lax.broadcasted_iota(jnp.int32, (LANE, width), 0)
    tok = step * width + lax.broadcasted_iota(jnp.int32, (LANE, width), 1)
    cur = tok // SLC_BLOCK
    forced = (jj == 0) | (jj == cur) | (jj == cur - 1)
    assert FORCE_SCORE > NSA_REP
    score = jnp.where(forced, -jnp.inf, jnp.where(jj <= cur, imp, -1.0))
    selb = jnp.where(forced, 0.0, NEG)
    jf = jj.astype(F32)
    for _ in range(SLC_TOPK - N_FORCED):
        mval = jnp.max(score, axis=0, keepdims=True)
        first = jnp.min(jnp.where(score == mval, jf, float(LANE)), axis=0, keepdims=True)
        pick = jf == first
        selb = jnp.where(pick, 0.0, selb)
        score = jnp.where(pick, -jnp.inf, score)
    sel_ref[0, 0] = selb


def _q_specs(nsteps, rows=QT):
    return [pl.BlockSpec((1, rows, LANE),
                         functools.partial(lambda b, g, t, r: (CB_NQ + NSA_REP * g + r, b * nsteps + t, 0), r=r))
            for r in range(NSA_REP)]


def _cmp_sel(proj, kvc, tt, ov_t, batch, seq):
    nt = proj.shape[1]
    rows = CMP_SUB * QT
    nsteps = seq // rows
    cols = NSA_REP * QT
    np_rows = kvc.shape[3]
    kv_spec = lambda s: pl.BlockSpec((1, 1, 1, np_rows, LANE), lambda b, g, t: (s, b, g, 0, 0))
    return pl.pallas_call(
        functools.partial(_cmp_sel_kernel, np_rows=np_rows),
        grid=(batch, NSA_GROUPS, nsteps),
        in_specs=_q_specs(nsteps, rows) + [
            kv_spec(0), kv_spec(1),
            pl.BlockSpec((1, rows, LANE), lambda b, g, t: (CB_GATE + g, b * nsteps + t, 0)),
            pl.BlockSpec((1, tt.shape[1], cols), lambda b, g, t: (g, 0, 0)),
            pl.BlockSpec((LANE, np_rows), lambda b, g, t: (0, 0))],
        out_specs=[pl.BlockSpec((rows, cols), lambda b, g, t: (b * nsteps + t, g)),
                   pl.BlockSpec((1, 1, LANE, rows), lambda b, g, t: (b, g, 0, t))],
        out_shape=[jax.ShapeDtypeStruct((nt, NSA_HEADS * LANE), F32),
                   jax.ShapeDtypeStruct((batch, NSA_GROUPS, LANE, seq), F32)],
        scratch_shapes=[pltpu.VMEM((np_rows, LANE), BF16), pltpu.VMEM((LANE, np_rows), BF16),
                        pltpu.VMEM((LANE, rows), F32)],
        compiler_params=_cparams(("arbitrary", "arbitrary", "arbitrary")),
        name="nsa_cmp_select",
    )(proj, proj, proj, proj, kvc, kvc, proj, tt, ov_t)


def _q_transposed(q_refs, sub=0):
    return jnp.concatenate([r[0, sub * QT:(sub + 1) * QT, :].T for r in q_refs], axis=1).astype(BF16)


def _finish_t(state, gate_ref, branch, o_ref, sub=0, others=()):
    _, l, acc = state
    o = acc / jnp.where(l == 0.0, 1.0, l)
    rows = slice(sub * QT, (sub + 1) * QT)
    gt = jax.nn.sigmoid(gate_ref[0, rows, :])
    for r in range(NSA_REP):
        col = 3 * r + branch
        lanes = slice(r * LANE, (r + 1) * LANE)
        val = gt[:, col:col + 1] * o[:, r * QT:(r + 1) * QT].T
        for other in others:
            val = val + other[rows, lanes]
        o_ref[rows, lanes] = val.astype(o_ref.dtype)


def _init_state(cols):
    return (jnp.full((1, cols), NEG, F32), jnp.zeros((1, cols), F32), jnp.zeros((LANE, cols), F32))


def _win_kernel(q0, q1, q2, q3, k_ref, v_ref, gate_ref, wb_ref, o_ref, kb_ref, vt_ref, *, seq):
    step = pl.program_id(2)
    cols = NSA_REP * QT
    wt = WINDOW // QT

    lane = lax.broadcasted_iota(jnp.int32, (QT, LANE), 1)
    pad_mark = jnp.where(lane == 0, 1.0, 0.0).astype(BF16)

    @pl.when(step == 0)
    def _():
        for i in range(wt):
            kb_ref[i * QT:(i + 1) * QT, 0:LANE] = jnp.zeros((QT, LANE), BF16)
            kb_ref[i * QT:(i + 1) * QT, LANE:2 * LANE] = pad_mark
            vt_ref[i] = jnp.zeros((LANE, QT), BF16)

        def fill(i, carry):
            r0 = pl.multiple_of(i * QT, QT)
            kb_ref[pl.ds(WINDOW + r0, QT), 0:LANE] = (k_ref[0, pl.ds(r0, QT), :] * SCALE_LOG2).astype(BF16)
            kb_ref[pl.ds(WINDOW + r0, QT), LANE:2 * LANE] = jnp.zeros((QT, LANE), BF16)
            vt_ref[wt + i] = v_ref[0, pl.ds(r0, QT), :].T.astype(BF16)
            return carry

        lax.fori_loop(0, seq // QT, fill, 0)

    subs = range(WIN_SUB)
    tis = [step * WIN_SUB + sub for sub in subs]
    row = lax.broadcasted_iota(jnp.int32, (LANE, cols), 0)
    pad_rows = jnp.where(row == 0, NEG, 0.0).astype(BF16)
    qts = [jnp.concatenate([_q_transposed((q0, q1, q2, q3), sub), pad_rows], axis=0) for sub in subs]
    ss = [_dot(kb_ref[pl.ds(pl.multiple_of(tis[u] * QT, QT), WINDOW + QT), :], qts[u]) + wb_ref[0]
          for u in subs]
    ms = [jnp.max(ss[u], axis=0, keepdims=True) for u in subs]
    ps = [jnp.exp2(ss[u] - ms[u]) for u in subs]
    ls = [jnp.sum(ps[u], axis=0, keepdims=True) for u in subs]
    pbs = [ps[u].astype(BF16) for u in subs]
    accs = None
    for d in range(0, wt + 1, 2):
        n = min(2, wt + 1 - d)
        vts = [jnp.concatenate([vt_ref[tis[u] + d + e] for e in range(n)], axis=1) for u in subs]
        part = [_dot(vts[u], pbs[u][d * QT:(d + n) * QT]) for u in subs]
        accs = part if accs is None else [accs[u] + part[u] for u in subs]
    for u in subs:
        _finish_t((ms[u], ls[u], accs[u]), gate_ref, 2, o_ref, u)


def _window(proj, wb, batch, seq):
    nt = proj.shape[1]
    rows = WIN_SUB * QT
    nqt = seq // rows
    cols = NSA_REP * QT
    slab = lambda cb0: pl.BlockSpec((1, seq, LANE), lambda b, g, t: (cb0 + g, b, 0))
    return pl.pallas_call(
        functools.partial(_win_kernel, seq=seq),
        grid=(batch, NSA_GROUPS, nqt),
        in_specs=_q_specs(nqt, rows) + [
            slab(CB_KW), slab(CB_VW),
            pl.BlockSpec((1, rows, LANE), lambda b, g, t: (CB_GATE + g, b * nqt + t, 0)),
            pl.BlockSpec((1, WINDOW + QT, cols), lambda b, g, t: (g, 0, 0))],
        out_specs=pl.BlockSpec((rows, cols), lambda b, g, t: (b * nqt + t, g)),
        out_shape=jax.ShapeDtypeStruct((nt, NSA_HEADS * LANE), F32),
        scratch_shapes=[pltpu.VMEM((seq + WINDOW, 2 * LANE), BF16),
                        pltpu.VMEM((seq // QT + WINDOW // QT, LANE, QT), BF16)],
        compiler_params=_cparams(("arbitrary", "arbitrary", "arbitrary")),
        name="nsa_window",
    )(proj, proj, proj, proj, proj, proj, proj, wb)


def _softmax_steps(ss, vt, states):
    n = range(len(ss))
    m_new = [jnp.maximum(states[u][0], jnp.max(ss[u], axis=0, keepdims=True)) for u in n]
    alpha = [jnp.exp2(states[u][0] - m_new[u]) for u in n]
    p = [jnp.exp2(ss[u] - m_new[u]) for u in n]
    l = [alpha[u] * states[u][1] + jnp.sum(p[u], axis=0, keepdims=True) for u in n]
    vts = vt if isinstance(vt, (list, tuple)) else [vt] * len(ss)
    pv = [_dot(vts[u], p[u].astype(BF16)) for u in n]
    return [(m_new[u], l[u], alpha[u] * states[u][2] + pv[u]) for u in n]


def _sel_kernel(q0, q1, q2, q3, k_ref, v_ref, sel_ref, gate_ref, nd_ref, npv_ref, oc_ref, ow_ref, o_ref,
                kb_ref, vt_ref, qa_ref, kn_ref, m_ref, l_ref, acc_ref, *, seq):
    step = pl.program_id(2)
    cols = NSA_REP * QT
    spb = KEY_BLK // SLC_BLOCK
    subs = range(SEL_SUB)

    @pl.when(step == 0)
    def _():
        blk = lax.broadcasted_iota(jnp.int32, (KEY_BLK, LANE), 0) // SLC_BLOCK
        onehot = jnp.where(blk == lax.broadcasted_iota(jnp.int32, (KEY_BLK, LANE), 1), 1.0, 0.0).astype(BF16)

        def fill(i, kmax2):
            r0 = pl.multiple_of(i * KEY_BLK, KEY_BLK)
            kblk = (k_ref[0, pl.ds(r0, KEY_BLK), :] * SCALE_LOG2).astype(BF16)
            kb_ref[pl.ds(r0, KEY_BLK), 0:LANE] = kblk
            kb_ref[pl.ds(r0, KEY_BLK), LANE:2 * LANE] = onehot
            for c in range(SEL_SUB):
                vt_ref[i, :, c * QT:(c + 1) * QT] = v_ref[0, pl.ds(r0 + c * QT, QT), :].T.astype(BF16)
            n2 = jnp.sum(jnp.square(kblk.astype(F32)), axis=1, keepdims=True)
            return jnp.maximum(kmax2, jnp.max(n2, axis=0, keepdims=True))

        kmax2 = lax.fori_loop(0, seq // KEY_BLK, fill, jnp.zeros((1, 1), F32))
        kn_ref[...] = jnp.sqrt(kmax2)
        qa_ref[...] = jnp.zeros_like(qa_ref)

    qts = [_q_transposed((q0, q1, q2, q3), u) for u in subs]
    for u in subs:
        qa_ref[u, 0:LANE, :] = qts[u]
    bounds = [jnp.sqrt(jnp.sum(jnp.square(qts[u].astype(F32)), axis=0, keepdims=True)) * kn_ref[...]
              * BOUND_SLACK + 1.0 for u in subs]

    def block(kb):
        r0 = pl.multiple_of(kb * KEY_BLK, KEY_BLK)
        return kb_ref[pl.ds(r0, KEY_BLK), :], vt_ref[kb]

    pad_rows = jnp.zeros((SEL_ROWS - spb, cols), F32)

    def scores(kb, k, rows_of=lambda u: KEY_BLK):
        for u in subs:
            rows = sel_ref[0, 0, pl.ds(pl.multiple_of(kb * spb, spb), spb), u * QT:(u + 1) * QT]
            rows = jnp.concatenate([rows] * NSA_REP, axis=1)
            qa_ref[u, LANE:LANE + SEL_ROWS, :] = jnp.concatenate([rows, pad_rows], axis=0).astype(BF16)
        return [_dot(k[0:rows_of(u)], qa_ref[u]) for u in subs]

    seen = lambda u: (u + 1) * QT
    k, vt = block(step)
    ss = scores(step, k, seen)
    ss = [ss[u] + nd_ref[0, u, 0:seen(u), :] for u in subs]
    states = _softmax_steps(ss, [vt[:, 0:seen(u)] for u in subs], [_init_state(cols) for _ in subs])

    for u in subs:
        m_ref[u], l_ref[u], acc_ref[u] = states[u]

    def prev_scores():
        k, vt = block(step - 1)
        ss = scores(step - 1, k)
        ss[0] = ss[0] + npv_ref[0]
        return ss, vt

    n_old = jnp.maximum(step - 1, 0)
    bounds[0] = bounds[0] + jnp.max(npv_ref[0], axis=0, keepdims=True)
    m_fix = [jnp.maximum(states[u][0], bounds[u]) for u in subs]
    gap = jnp.max(jnp.concatenate([m_fix[u] - states[u][0] for u in subs], axis=1))
    fixed_ok = gap < MAX_REF_GAP

    @pl.when(fixed_ok)
    def _():
        for u in subs:
            alpha = jnp.exp2(m_ref[u] - m_fix[u])
            l_ref[u] = alpha * l_ref[u]
            acc_ref[u] = alpha * acc_ref[u]
            m_ref[u] = m_fix[u]

        def fixed_step(ss, vt):
            ps = [jnp.exp2(ss[u] - m_ref[u]) for u in subs]
            for u in subs:
                l_ref[u] = l_ref[u] + jnp.sum(ps[u], axis=0, keepdims=True)
            pv = [_dot(vt, ps[u].astype(BF16)) for u in subs]
            for u in subs:
                acc_ref[u] = acc_ref[u] + pv[u]

        @pl.when(step >= 1)
        def _():
            fixed_step(*prev_scores())

        def body(kb, carry):
            k, vt = block(kb)
            fixed_step(scores(kb, k), vt)
            return carry

        lax.fori_loop(0, n_old, body, 0)

    @pl.when(jnp.logical_not(fixed_ok))
    def _():
        def online_step(ss, vt):
            st = _softmax_steps(ss, vt, [(m_ref[u], l_ref[u], acc_ref[u]) for u in subs])
            for u in subs:
                m_ref[u], l_ref[u], acc_ref[u] = st[u]

        @pl.when(step >= 1)
        def _():
            online_step(*prev_scores())

        def body(kb, carry):
            k, vt = block(kb)
            online_step(scores(kb, k), vt)
            return carry

        lax.fori_loop(0, n_old, body, 0)

    for u in subs:
        _finish_t((m_ref[u], l_ref[u], acc_ref[u]), gate_ref, 1, o_ref, u, others=(oc_ref, ow_ref))


def _selected(proj, sel_t, nd, npv, o_c, o_w, batch, seq):
    nt = proj.shape[1]
    rows = SEL_SUB * QT
    nsteps = seq // rows
    cols = NSA_REP * QT
    slab = lambda cb0: pl.BlockSpec((1, seq, LANE), lambda b, g, t: (cb0 + g, b, 0))
    return pl.pallas_call(
        functools.partial(_sel_kernel, seq=seq),
        grid=(batch, NSA_GROUPS, nsteps),
        in_specs=_q_specs(nsteps, rows) + [
            slab(CB_KS), slab(CB_VS),
            pl.BlockSpec((1, 1, LANE, rows), lambda b, g, t: (b, g, 0, t)),
            pl.BlockSpec((1, rows, LANE), lambda b, g, t: (CB_GATE + g, b * nsteps + t, 0)),
            pl.BlockSpec((1, SEL_SUB, KEY_BLK, cols), lambda b, g, t: (g, 0, 0, 0)),
            pl.BlockSpec((1, KEY_BLK, cols), lambda b, g, t: (g, 0, 0)),
            pl.BlockSpec((rows, cols), lambda b, g, t: (b * nsteps + t, g)),
            pl.BlockSpec((rows, cols), lambda b, g, t: (b * nsteps + t, g))],
        out_specs=pl.BlockSpec((rows, cols), lambda b, g, t: (b * nsteps + t, g)),
        out_shape=jax.ShapeDtypeStruct((nt, NSA_HEADS * LANE), BF16),
        scratch_shapes=[pltpu.VMEM((seq, 2 * LANE), BF16),
                        pltpu.VMEM((seq // KEY_BLK, LANE, KEY_BLK), BF16),
                        pltpu.VMEM((SEL_SUB, 2 * LANE, cols), BF16),
                        pltpu.VMEM((1, 1), F32),
                        pltpu.VMEM((SEL_SUB, 1, cols), F32), pltpu.VMEM((SEL_SUB, 1, cols), F32),
                        pltpu.VMEM((SEL_SUB, LANE, cols), F32)],
        compiler_params=_cparams(("arbitrary", "arbitrary", "arbitrary")),
        name="nsa_selected",
    )(proj, proj, proj, proj, proj, proj, sel_t, proj, nd, npv, o_c, o_w)


def _merge_kernel(oh_ref, on_ref, mgh_ref, mgn_ref, x_ref, mod_ref,
                  wh_ref, wn_ref, wo_ref, g_ref, b_ref, wr_ref, br_ref,
                  x1_ref, h2_ref, lg_ref):
    nblk = D_MODEL // LANE
    tm = x_ref.shape[0]
    halves = [slice(s * tm // MERGE_SPLIT, (s + 1) * tm // MERGE_SPLIT) for s in range(MERGE_SPLIT)]
    a_h = [_dot(oh_ref[rs, :], wh_ref[...]) for rs in halves]
    a_n = [_dot(on_ref[rs, :], wn_ref[...]) for rs in halves]
    gh = [jnp.concatenate([mgh_ref[c, rs, :] for c in range(nblk)], axis=-1) for rs in halves]
    gn = [jnp.concatenate([mgn_ref[c, rs, :] for c in range(nblk)], axis=-1) for rs in halves]
    merged = [(jax.nn.sigmoid(gh[s]) * a_h[s] + jax.nn.sigmoid(gn[s]) * a_n[s]).astype(BF16)
              for s in range(MERGE_SPLIT)]
    y = [(1.0 + mod_ref[0, 2:3, :]) * _dot(merged[s], wo_ref[...]) for s in range(MERGE_SPLIT)]
    x1 = [_layer_norm(ALPHA * x_ref[rs, :] + y[s]) * g_ref[...] + b_ref[...] for s, rs in enumerate(halves)]
    h2 = [_layer_norm(x1[s]) * (1.0 + mod_ref[0, 4:5, :]) + mod_ref[0, 3:4, :] for s in range(MERGE_SPLIT)]
    h_hi = [h2[s].astype(BF16) for s in range(MERGE_SPLIT)]
    h_lo = [(h2[s] - h_hi[s].astype(F32)).astype(BF16) for s in range(MERGE_SPLIT)]
    lg = [_dot(h_hi[s], wr_ref[0]) + _dot(h_hi[s], wr_ref[1]) + _dot(h_lo[s], wr_ref[0]) + br_ref[...]
          for s in range(MERGE_SPLIT)]
    for s, rs in enumerate(halves):
        x1_ref[rs, :] = x1[s]
        _store_rows(h2_ref, h2[s], rs.start)
        lg_ref[rs, :] = lg[s]


def _merge(o_h, o_n, proj, x2, mod, w_h, w_n, w_o, ln_g, ln_b, w_r, b_r, seq):
    nt, d = x2.shape
    tm = min(MERGE_ROWS, seq)
    nblk = d // LANE
    row = lambda w: pl.BlockSpec((tm, w), lambda i: (i, 0))
    full = lambda a: pl.BlockSpec(a.shape, lambda i: (0,) * a.ndim)
    return pl.pallas_call(
        _merge_kernel,
        grid=(nt // tm,),
        in_specs=[row(d), row(d),
                  pl.BlockSpec((nblk, tm, LANE), lambda i: (CB_MGH // nblk, i, 0)),
                  pl.BlockSpec((nblk, tm, LANE), lambda i: (CB_MGN // nblk, i, 0)),
                  row(d),
                  pl.BlockSpec((1, 6, d), lambda i: (i * tm // seq, 0, 0)),
                  full(w_h), full(w_n), full(w_o), full(ln_g), full(ln_b), full(w_r), full(b_r)],
        out_specs=[row(d), pl.BlockSpec((tm * ROW_TILES, LANE), lambda i: (i, 0)), row(LANE)],
        out_shape=[jax.ShapeDtypeStruct((nt, d), F32),
                   jax.ShapeDtypeStruct((nt * ROW_TILES, LANE), F32),
                   jax.ShapeDtypeStruct((nt, LANE), F32)],
        compiler_params=_cparams(("arbitrary",)),
        name="merge_outproj",
    )(o_h, o_n, proj, proj, x2, mod, w_h, w_n, w_o, ln_g, ln_b, w_r, b_r)


def _route_kernel(lg_ref, rec_ref, cnt_ref, rect_ref, carry_ref, *, tm):
    @pl.when(pl.program_id(0) == 0)
    def _():
        carry_ref[...] = jnp.zeros_like(carry_ref)

    lg = lg_ref[...]
    lane = lax.broadcasted_iota(jnp.int32, (tm, LANE), 1).astype(F32)
    far = float(LANE)
    gmask = lane < N_GROUPS
    gl = jnp.where(gmask, lg, -jnp.inf)
    gmax = jnp.max(gl, axis=-1, keepdims=True)
    gsum = jnp.sum(jnp.where(gmask, jnp.exp(gl - gmax), 0.0), axis=-1, keepdims=True)
    grp_p = 1.0 / gsum
    gidx = jnp.min(jnp.where(gl == gmax, lane, far), axis=-1, keepdims=True)
    lo = N_GROUPS + EXP_PER_GROUP * gidx
    emask = (lane >= lo) & (lane < lo + EXP_PER_GROUP)
    el = jnp.where(emask, lg, -jnp.inf)
    m1 = jnp.max(el, axis=-1, keepdims=True)
    i1 = jnp.min(jnp.where(el == m1, lane, far), axis=-1, keepdims=True)
    el2 = jnp.where(lane == i1, -jnp.inf, el)
    m2 = jnp.max(el2, axis=-1, keepdims=True)
    i2 = jnp.min(jnp.where(emask & (lane != i1) & (el2 == m2), lane, far), axis=-1, keepdims=True)
    e = jnp.exp(m2 - m1)
    w0 = grp_p / (1.0 + e)
    w1 = grp_p * e / (1.0 + e)

    oh0 = lane == i1
    oh1 = lane == i2
    f0 = jnp.where(oh0, 1.0, 0.0)
    f1 = jnp.where(oh1, 1.0, 0.0)
    ri = lax.broadcasted_iota(jnp.int32, (tm, tm), 0)
    ci = lax.broadcasted_iota(jnp.int32, (tm, tm), 1)
    before = jnp.where(ci < ri, 1.0, 0.0).astype(BF16)
    cum0 = _dot(before, f0.astype(BF16))
    cum1 = _dot(before, f1.astype(BF16))
    tot0 = jnp.sum(f0, axis=0, keepdims=True)
    tot1 = jnp.sum(f1, axis=0, keepdims=True)
    carry = carry_ref[...]
    rank0 = jnp.sum(jnp.where(oh0, carry + cum0, 0.0), axis=-1, keepdims=True)
    rank1 = jnp.sum(jnp.where(oh1, carry + tot0 + cum1, 0.0), axis=-1, keepdims=True)
    carry = carry + tot0 + tot1
    carry_ref[...] = carry
    cnt_ref[...] = carry

    rec = jnp.where(lane == 0, i1 - N_GROUPS, 0.0)
    rec = jnp.where(lane == 1, i2 - N_GROUPS, rec)
    rec = jnp.where(lane == 2, w0, rec)
    rec = jnp.where(lane == 3, w1, rec)
    rec = jnp.where(lane == 4, rank0, rec)
    rec = jnp.where(lane == 5, rank1, rec)
    rec_ref[...] = rec
    rect_ref[...] = jnp.concatenate([rec[c * LANE:(c + 1) * LANE, :].T[0:8, :] for c in range(tm // LANE)], axis=1)


def _route(logits):
    nt = logits.shape[0]
    tm = min(ROUTE_ROWS, nt)
    return pl.pallas_call(
        functools.partial(_route_kernel, tm=tm),
        grid=(nt // tm,),
        in_specs=[pl.BlockSpec((tm, LANE), lambda i: (i, 0))],
        out_specs=[pl.BlockSpec((tm, LANE), lambda i: (i, 0)),
                   pl.BlockSpec((1, LANE), lambda i: (0, 0)),
                   pl.BlockSpec((8, tm), lambda i: (0, i))],
        out_shape=[jax.ShapeDtypeStruct((nt, LANE), F32),
                   jax.ShapeDtypeStruct((1, LANE), F32),
                   jax.ShapeDtypeStruct((8, nt), F32)],
        scratch_shapes=[pltpu.VMEM((1, LANE), F32)],
        compiler_params=_cparams(("arbitrary",)),
        name="moe_route",
    )(logits)


def _row_copy(src, dst, sem):
    return pltpu.make_async_copy(src, dst, sem)


def _tile_of(r):
    return pl.ds(pl.multiple_of(r * ROW_TILES, ROW_TILES), ROW_TILES)


def _load_rows(ref, n, lead=()):
    return jnp.concatenate([ref[lead + (pl.ds(c, n, stride=ROW_TILES), slice(None))]
                            for c in range(ROW_TILES)], axis=1)


def _store_rows(ref, val, row0=0):
    n = val.shape[0]
    for c in range(ROW_TILES):
        ref[pl.ds(row0 * ROW_TILES + c, n, stride=ROW_TILES), :] = val[:, c * LANE:(c + 1) * LANE]


def _dispatch_kernel(dest0_ref, dest1_ref, zb_ref, h_ref, xp_ref, z_ref, sem, zsem, *, tm):
    step = pl.program_id(0)
    base = step * tm
    blk = MOE_ROWS * ROW_TILES

    @pl.when(step == 0)
    def _():
        z_ref[...] = jnp.zeros_like(z_ref)

        def zero_copy(j):
            b = jnp.maximum(zb_ref[j], 0)
            return _row_copy(z_ref, xp_ref.at[pl.ds(pl.multiple_of(b * blk, blk), blk), :], zsem)

        def start(j, carry):
            @pl.when(zb_ref[j] >= 0)
            def _():
                zero_copy(j).start()
            return carry

        def wait(j, carry):
            @pl.when(zb_ref[j] >= 0)
            def _():
                zero_copy(j).wait()
            return carry

        lax.fori_loop(0, 2 * N_EXPERTS, start, 0)
        lax.fori_loop(0, 2 * N_EXPERTS, wait, 0)

    def issue(r, carry):
        for dest_ref in (dest0_ref, dest1_ref):
            d = dest_ref[base + r]
            _row_copy(h_ref.at[_tile_of(r), :], xp_ref.at[_tile_of(d), :], sem).start()
        return carry

    lax.fori_loop(0, tm, issue, 0, unroll=ROW_DMA_UNROLL)
    for _ in range(2):
        _row_copy(h_ref, xp_ref.at[pl.ds(0, tm * ROW_TILES), :], sem).wait()


def _dispatch(dest0, dest1, zero_blocks, h2, n_blocks):
    nt = h2.shape[0] // ROW_TILES
    tm = min(DISPATCH_TOKENS, nt)
    return pl.pallas_call(
        functools.partial(_dispatch_kernel, tm=tm),
        grid_spec=pltpu.PrefetchScalarGridSpec(
            num_scalar_prefetch=3,
            grid=(nt // tm,),
            in_specs=[pl.BlockSpec((tm * ROW_TILES, LANE), lambda i, *_: (i, 0))],
            out_specs=pl.BlockSpec(memory_space=pl.ANY),
            scratch_shapes=[pltpu.VMEM((MOE_ROWS * ROW_TILES, LANE), F32),
                            pltpu.SemaphoreType.DMA(()), pltpu.SemaphoreType.DMA(())]),
        out_shape=jax.ShapeDtypeStruct((n_blocks * MOE_ROWS * ROW_TILES, LANE), F32),
        compiler_params=_cparams(("arbitrary",)),
        name="moe_dispatch",
    )(dest0, dest1, zero_blocks, h2)


def _expert_kernel(be_ref, nu_ref, x_ref, w1_ref, w3_ref, w2_ref, y_ref, w1b_ref, w3b_ref, w2b_ref):
    i = pl.program_id(0)

    @pl.when((i < nu_ref[0]) & ((i == 0) | (be_ref[i] != be_ref[jnp.maximum(i - 1, 0)])))
    def _():
        w1b_ref[...] = w1_ref[0].astype(BF16)
        w3b_ref[...] = w3_ref[0].astype(BF16)
        w2b_ref[...] = w2_ref[0].astype(BF16)

    @pl.when(i < nu_ref[0])
    def _():
        xb = _load_rows(x_ref, MOE_ROWS).astype(BF16)
        a = _dot(xb, w1b_ref[...])
        b = _dot(xb, w3b_ref[...])
        hmid = (a * jax.nn.sigmoid(a) * b).astype(BF16)
        _store_rows(y_ref, _dot(hmid, w2b_ref[...]))

    @pl.when(i >= nu_ref[0])
    def _():
        y_ref[...] = jnp.zeros_like(y_ref)


def _experts(block_expert, n_used, x_pad, w1, w3, w2):
    d, de = w1.shape[1], w1.shape[2]
    nb = x_pad.shape[0] // (MOE_ROWS * ROW_TILES)
    return pl.pallas_call(
        _expert_kernel,
        grid_spec=pltpu.PrefetchScalarGridSpec(
            num_scalar_prefetch=2,
            grid=(nb,),
            in_specs=[pl.BlockSpec((MOE_ROWS * ROW_TILES, LANE), lambda i, be, nu: (i, 0)),
                      pl.BlockSpec((1, d, de), lambda i, be, nu: (be[i], 0, 0)),
                      pl.BlockSpec((1, d, de), lambda i, be, nu: (be[i], 0, 0)),
                      pl.BlockSpec((1, de, d), lambda i, be, nu: (be[i], 0, 0))],
            out_specs=pl.BlockSpec((MOE_ROWS * ROW_TILES, LANE), lambda i, be, nu: (i, 0)),
            scratch_shapes=[pltpu.VMEM((d, de), BF16), pltpu.VMEM((d, de), BF16), pltpu.VMEM((de, d), BF16)]),
        out_shape=jax.ShapeDtypeStruct(x_pad.shape, F32),
        compiler_params=_cparams(("arbitrary",)),
        name="moe_experts",
    )(block_expert, n_used, x_pad, w1, w3, w2)


def _combine_kernel(dest0_ref, dest1_ref, yp_ref, rec_ref, x1_ref, mod_ref, g_ref, b_ref, o_ref,
                    buf_ref, sem, *, tm):
    step = pl.program_id(0)
    slot = step % 2

    def fetch(tile, to_slot):
        def issue(r, carry):
            for k, dest_ref in enumerate((dest0_ref, dest1_ref)):
                d = dest_ref[tile * tm + r]
                _row_copy(yp_ref.at[_tile_of(d), :], buf_ref.at[to_slot, k, _tile_of(r), :],
                          sem.at[to_slot]).start()
            return carry

        lax.fori_loop(0, tm, issue, 0, unroll=ROW_DMA_UNROLL)

    @pl.when(step == 0)
    def _():
        fetch(0, 0)

    @pl.when(step + 1 < pl.num_programs(0))
    def _():
        fetch(step + 1, 1 - slot)

    for k in range(2):
        _row_copy(yp_ref.at[pl.ds(0, tm * ROW_TILES), :], buf_ref.at[slot, k], sem.at[slot]).wait()

    rec = rec_ref[...]
    y = rec[:, 2:3] * _load_rows(buf_ref, tm, (slot, 0)) + rec[:, 3:4] * _load_rows(buf_ref, tm, (slot, 1))
    y = (1.0 + mod_ref[0, 5:6, :]) * y
    o_ref[...] = _layer_norm(ALPHA * x1_ref[...] + y) * g_ref[...] + b_ref[...]


def _combine(dest0, dest1, y_pad, rec, x1, mod, ln_g, ln_b, seq):
    nt, d = x1.shape
    tm = min(COMBINE_TOKENS, seq)
    return pl.pallas_call(
        functools.partial(_combine_kernel, tm=tm),
        grid_spec=pltpu.PrefetchScalarGridSpec(
            num_scalar_prefetch=2,
            grid=(nt // tm,),
            in_specs=[pl.BlockSpec(memory_space=pl.ANY),
                      pl.BlockSpec((tm, LANE), lambda i, *_: (i, 0)),
                      pl.BlockSpec((tm, d), lambda i, *_: (i, 0)),
                      pl.BlockSpec((1, 6, d), lambda i, *_: (i * tm // seq, 0, 0)),
                      pl.BlockSpec((1, d), lambda i, *_: (0, 0)),
                      pl.BlockSpec((1, d), lambda i, *_: (0, 0))],
            out_specs=pl.BlockSpec((tm, d), lambda i, *_: (i, 0)),
            scratch_shapes=[pltpu.VMEM((2, 2, tm * ROW_TILES, LANE), F32), pltpu.SemaphoreType.DMA((2,))]),
        out_shape=jax.ShapeDtypeStruct((nt, d), F32),
        compiler_params=_cparams(("arbitrary",)),
        name="moe_combine",
    )(dest0, dest1, y_pad, rec, x1, mod, ln_g, ln_b)


def _rel_bucket(dist):
    n = jnp.maximum(dist, 0)
    max_exact = REL_BUCKETS // 2
    nf = jnp.maximum(n, 1).astype(F32)
    large = max_exact + (jnp.log(nf / max_exact) / math.log(REL_MAX_DIST / max_exact)
                         * (REL_BUCKETS - max_exact)).astype(jnp.int32)
    large = jnp.minimum(large, REL_BUCKETS - 1)
    return jnp.where(n < max_exact, n, large)


def _bias_tables(rel_bias, seq):
    bucket_onehot = (_rel_bucket(jnp.arange(LANE))[:, None] == jnp.arange(REL_BUCKETS)).astype(F32)
    tab_d = jnp.einsum('db,hb->hd', bucket_onehot, rel_bias,
                       precision=lax.Precision.HIGHEST)
    tok = np.arange(QT)[None, :]
    key = np.arange(LANE)[:, None]
    far = tab_d[:, LANE - 1]
    cols = NSA_REP * QT

    def transposed(dist):
        idx = jnp.asarray(np.clip(dist, 0, LANE - 1).astype(np.int32))
        onehot = (idx[..., None] == jnp.arange(LANE, dtype=jnp.int32)).astype(F32)
        t = jnp.einsum('ijd,hd->hij', onehot, tab_d, precision=lax.Precision.HIGHEST)
        t = (t - far[:, None, None]) * LOG2E
        t = t.reshape(NSA_GROUPS, NSA_REP, LANE, QT).transpose(0, 2, 1, 3)
        return t.reshape(NSA_GROUPS, LANE, cols)

    t0t = transposed(tok - key)
    t1t = transposed(tok - key + QT)

    ns = seq // CMP_STRIDE
    d_c = tok - CMP_STRIDE * key + (CMP_STRIDE * CMP_PAD - (CMP_BLOCK - 1))
    seen = np.tile(d_c >= 0, (1, NSA_REP))
    recent = jnp.where(seen[None], transposed(d_c), NEG)
    tt = jnp.concatenate([jnp.zeros((NSA_GROUPS, ns, cols), F32), recent,
                          jnp.full((NSA_GROUPS, ns, cols), NEG, F32)], axis=1)

    rho = np.arange(WINDOW + QT)[:, None]
    tok_w = np.tile(np.arange(QT), NSA_REP)[None, :]
    band = (rho > tok_w) & (rho <= tok_w + WINDOW)
    rows = jnp.concatenate([jnp.zeros((NSA_GROUPS, WINDOW - QT, NSA_REP * QT), F32), t1t, t0t], axis=1)
    wb = jnp.where(band[None], rows, NEG)

    zeros = lambda n: jnp.zeros((NSA_GROUPS, n * QT, cols), F32)
    negs = lambda n: jnp.full((NSA_GROUPS, n * QT, cols), NEG, F32)
    diag = jnp.where(np.tile(tok >= key, (1, NSA_REP))[None], t0t, NEG)
    nd = jnp.stack([jnp.concatenate(([zeros(u - 1), t1t] if u else []) + [diag, negs(SEL_SUB - 1 - u)], axis=1)
                    for u in range(SEL_SUB)], axis=1)
    npv = jnp.concatenate([zeros(SEL_SUB - 1), t1t], axis=1)
    return nd, npv, wb, tt


def _overlap_matrix(seq):
    ns = seq // CMP_STRIDE
    nslc = seq // SLC_BLOCK
    ov = np.zeros((LANE, ns), np.float32)
    cs = np.arange(ns - 1) * CMP_STRIDE
    ss = np.arange(nslc) * SLC_BLOCK
    ov[:nslc, :ns - 1] = ((cs[None, :] < ss[:, None] + SLC_BLOCK) & (cs[None, :] + CMP_BLOCK > ss[:, None]))
    return jnp.asarray(ov, BF16)


def _reorder_cols(a):
    lead = a.shape[:-1]
    gate = a[..., MAIN_COLS:MAIN_COLS + GATE_COLS]
    per = GATE_COLS // NSA_GROUPS
    gate_blocks = []
    for g in range(NSA_GROUPS):
        gate_blocks.append(gate[..., g * per:(g + 1) * per])
        gate_blocks.append(jnp.zeros(lead + (LANE - per,), a.dtype))
    pad = jnp.zeros(lead + ((CB_MGH - CB_GATE - NSA_GROUPS) * LANE,), a.dtype)
    return jnp.concatenate([a[..., :MAIN_COLS]] + gate_blocks + [pad, a[..., MAIN_COLS + GATE_COLS:]], axis=-1)


def kernel(x, c, ada_w, ada_b, w_in, b_in, hg_lb_logits, hg_norm_w, cmp_pos_k, cmp_w1_k, cmp_b1_k, cmp_w2_k, cmp_pos_v, cmp_w1_v, cmp_b1_v, cmp_w2_v, rel_bias, w_br_hg, w_br_nsa, w_out, ln1_g, ln1_b, router_grp_w, router_grp_b, router_exp_w, router_exp_b, exp_w1, exp_w3, exp_w2, ln2_g, ln2_b):
    batch, seq, d = x.shape
    nt = batch * seq
    assert d == D_MODEL and seq % INPROJ_ROWS == 0 and seq // SLC_BLOCK <= LANE
    l = 0
    x2 = x.reshape(nt, d)

    c_pad = jnp.zeros((8, d), F32).at[:batch].set(c)
    mod = _adaln(c_pad, ada_w[l], ada_b[l][None])[:batch].reshape(batch, 6, d)

    proj = _inproj(x2, mod, _reorder_cols(w_in[l].astype(BF16)), _reorder_cols(b_in[l])[None], seq)

    lb_all = jnp.cumsum(jax.nn.softmax(hg_lb_logits.astype(F32), axis=0), axis=0)
    o_h = _hgrn(proj, lb_all[l][None], hg_norm_w[l][None], batch, seq)

    kvc = _compress(proj, jnp.stack([cmp_pos_k[l], cmp_pos_v[l]]),
                    jnp.stack([cmp_w1_k[l], cmp_w1_v[l]]).astype(BF16),
                    jnp.stack([cmp_b1_k[l], cmp_b1_v[l]])[:, None, :],
                    jnp.stack([cmp_w2_k[l], cmp_w2_v[l]]).astype(BF16), batch, seq)

    nd, npv, wb, tt = _bias_tables(rel_bias, seq)
    o_c, sel_t = _cmp_sel(proj, kvc, tt, _overlap_matrix(seq), batch, seq)
    o_w = _window(proj, wb, batch, seq)
    o_n = _selected(proj, sel_t, nd, npv, o_c, o_w, batch, seq)

    w_r = jnp.zeros((d, LANE), F32).at[:, :N_GROUPS].set(router_grp_w[l])
    w_r = w_r.at[:, N_GROUPS:N_GROUPS + N_EXPERTS].set(router_exp_w[l])
    w_r_hi = w_r.astype(BF16)
    w_r_lo = (w_r - w_r_hi.astype(F32)).astype(BF16)
    b_r = jnp.zeros((1, LANE), F32).at[0, :N_GROUPS].set(router_grp_b[l])
    b_r = b_r.at[0, N_GROUPS:N_GROUPS + N_EXPERTS].set(router_exp_b[l])
    x1, h2, logits = _merge(o_h, o_n, proj, x2, mod,
                            w_br_hg[l].astype(BF16), w_br_nsa[l].astype(BF16), w_out[l].astype(BF16),
                            ln1_g[l][None], ln1_b[l][None], jnp.stack([w_r_hi, w_r_lo]), b_r, seq)

    rec, cnt, rec_t = _route(logits)
    counts = cnt[0, N_GROUPS:N_GROUPS + N_EXPERTS].astype(jnp.int32)
    padded = (counts + MOE_ROWS - 1) // MOE_ROWS * MOE_ROWS
    pend = jnp.cumsum(padded)
    pstart = pend - padded
    n_assign = 2 * nt
    nb = n_assign // MOE_ROWS + N_EXPERTS
    slots = rec_t.astype(jnp.int32)
    expert_ids = jnp.arange(N_EXPERTS, dtype=jnp.int32)[:, None]
    slot_base = lambda e: jnp.sum(jnp.where(e[None, :] == expert_ids, pstart[:, None], 0), axis=0)
    dest0 = slot_base(slots[0]) + slots[4]
    dest1 = slot_base(slots[1]) + slots[5]
    block_start = jnp.arange(nb, dtype=jnp.int32) * MOE_ROWS
    block_expert = jnp.minimum(jnp.sum(pend[None, :] <= block_start[:, None], axis=1),
                               N_EXPERTS - 1).astype(jnp.int32)
    n_used = pend[-1] // MOE_ROWS
    spare = n_used + jnp.arange(N_EXPERTS, dtype=jnp.int32)
    zero_blocks = jnp.concatenate([jnp.where(padded > 0, pend // MOE_ROWS - 1, -1),
                                   jnp.where(spare < nb, spare, -1)]).astype(jnp.int32)

    x_pad = _dispatch(dest0, dest1, zero_blocks, h2, nb)
    y_pad = _experts(block_expert, n_used[None].astype(jnp.int32), x_pad, exp_w1[l], exp_w3[l], exp_w2[l])
    out = _combine(dest0, dest1, y_pad, rec, x1, mod, ln2_g[l][None], ln2_b[l][None], seq)
    return out.reshape(batch, seq, d)
```

```python
import functools
import math

import numpy as np
import jax
import jax.numpy as jnp
from jax import lax
from jax.experimental import pallas as pl
from jax.experimental.pallas import tpu as pltpu

F32 = jnp.float32
BF16 = jnp.bfloat16

D_MODEL = 1024
HG_HEADS = 8
HG_DK = 128
HG_DV = 128
HG_CHUNK = 32
HG_SUPER = 256
NSA_HEADS = 8
NSA_GROUPS = 2
NSA_REP = NSA_HEADS // NSA_GROUPS
NSA_DK = 128
CMP_BLOCK = 32
CMP_STRIDE = 16
SLC_BLOCK = 64
SLC_TOPK = 16
WINDOW = 512
FORCE_SCORE = 1e4
N_FORCED = 3
REL_BUCKETS = 32
REL_MAX_DIST = 128
N_GROUPS = 4
EXP_PER_GROUP = 8
N_EXPERTS = N_GROUPS * EXP_PER_GROUP
D_EXPERT = D_MODEL // 2
DEPTH = 1
ALPHA = (2 * DEPTH) ** 0.25

LANE = 128
QT = 128
NEG = -1e30
SCALE = NSA_DK ** -0.5
LOG2E = math.log2(math.e)
SCALE_LOG2 = SCALE * LOG2E
KEY_BLK = 512
BOUND_SLACK = 1.0 + 2.0 ** -10
MAX_REF_GAP = 64.0
SEL_ROWS = 16
SEL_SUB = KEY_BLK // QT
WIN_SUB = 4
CMP_SUB = 4
CMP_PAD = 120
VMEM_LIMIT = 56 * 1024 * 1024

CB_HQ, CB_HF, CB_HI, CB_HG = 0, 8, 16, 24
CB_NQ = 32
CB_KC, CB_VC, CB_KS, CB_VS, CB_KW, CB_VW = 40, 42, 44, 46, 48, 50
CB_GATE = 52
CB_MGH, CB_MGN = 56, 64
NCB = 72
MAIN_COLS = 52 * LANE
GATE_COLS = 3 * NSA_HEADS

INPROJ_ROWS = 2048
INPROJ_COL_BLOCKS = 8
HGRN_ROWS = 512
MERGE_ROWS = 512
ROUTE_ROWS = 512
DISPATCH_TOKENS = 2048
COMBINE_TOKENS = 256
MERGE_SPLIT = 2
ROW_TILES = D_MODEL // LANE
MOE_ROWS = 512
ROW_DMA_UNROLL = 8


def _cparams(sem):
    return pltpu.CompilerParams(dimension_semantics=sem, vmem_limit_bytes=VMEM_LIMIT)


def _dot(a, b):
    return jnp.dot(a, b, preferred_element_type=F32)


def _dot_nt(a, b):
    return lax.dot_general(a, b, (((1,), (1,)), ((), ())), preferred_element_type=F32)


def _dot_tn(a, b):
    return lax.dot_general(a, b, (((0,), (0,)), ((), ())), preferred_element_type=F32)


def _split3(x):
    hi = x.astype(BF16)
    r = x - hi.astype(F32)
    mid = r.astype(BF16)
    lo = (r - mid.astype(F32)).astype(BF16)
    return hi, mid, lo


def _dot01(m01, x):
    hi, mid, lo = _split3(x)
    return _dot(m01, hi) + _dot(m01, mid) + _dot(m01, lo)


def _layer_norm(x, eps=1e-5):
    mu = jnp.mean(x, axis=-1, keepdims=True)
    xc = x - mu
    var = jnp.mean(xc * xc, axis=-1, keepdims=True)
    return xc * lax.rsqrt(var + eps)


def _adaln_kernel(c_ref, w_ref, b_ref, o_ref):
    c = c_ref[...]
    ca = c * jax.nn.sigmoid(c)
    o_ref[...] = jnp.dot(ca, w_ref[...], precision=lax.Precision.HIGHEST,
                         preferred_element_type=F32) + b_ref[...]


def _adaln(c_pad, w, b):
    rows, d = c_pad.shape
    n = w.shape[1]
    return pl.pallas_call(
        _adaln_kernel,
        grid=(n // d,),
        in_specs=[pl.BlockSpec((rows, d), lambda j: (0, 0)),
                  pl.BlockSpec((d, d), lambda j: (0, j)),
                  pl.BlockSpec((1, d), lambda j: (0, j))],
        out_specs=pl.BlockSpec((rows, d), lambda j: (0, j)),
        out_shape=jax.ShapeDtypeStruct((rows, n), F32),
        compiler_params=_cparams(("arbitrary",)),
        name="adaln",
    )(c_pad, w, b)


def _inproj_kernel(x_ref, mod_ref, w_ref, b_ref, o_ref, hn_ref, *, ncb_tile):
    @pl.when(pl.program_id(1) == 0)
    def _():
        hn = _layer_norm(x_ref[...])
        sh = mod_ref[0, 0:1, :]
        sc = mod_ref[0, 1:2, :]
        hn_ref[...] = (hn * (1.0 + sc) + sh).astype(BF16)

    res = _dot(hn_ref[...], w_ref[...]) + b_ref[...]
    for c in range(ncb_tile):
        o_ref[c] = res[:, c * LANE:(c + 1) * LANE]


def _inproj(x2, mod, w, b, seq):
    nt, d = x2.shape
    tm = min(INPROJ_ROWS, seq)
    ncb_tile = INPROJ_COL_BLOCKS
    tn = ncb_tile * LANE
    return pl.pallas_call(
        functools.partial(_inproj_kernel, ncb_tile=ncb_tile),
        grid=(nt // tm, NCB // ncb_tile),
        in_specs=[pl.BlockSpec((tm, d), lambda i, j: (i, 0)),
                  pl.BlockSpec((1, 6, d), lambda i, j: (i * tm // seq, 0, 0)),
                  pl.BlockSpec((d, tn), lambda i, j: (0, j)),
                  pl.BlockSpec((1, tn), lambda i, j: (0, j))],
        out_specs=pl.BlockSpec((ncb_tile, tm, LANE), lambda i, j: (j, i, 0)),
        out_shape=jax.ShapeDtypeStruct((NCB, nt, LANE), F32),
        scratch_shapes=[pltpu.VMEM((tm, d), BF16)],
        compiler_params=_cparams(("arbitrary", "arbitrary")),
        name="inproj",
    )(x2, mod, w, b)


def _hgrn_kernel(q_ref, f_ref, v_ref, g_ref, lb_ref, nw_ref, o_ref, st_ref, *, rows):
    @pl.when(pl.program_id(1) == 0)
    def _():
        st_ref[...] = jnp.zeros_like(st_ref)

    sup = HG_SUPER
    ri = lax.broadcasted_iota(jnp.int32, (sup, sup), 0)
    ci = lax.broadcasted_iota(jnp.int32, (sup, sup), 1)
    same = (ri // HG_CHUNK) == (ci // HG_CHUNK)
    cum_m = jnp.where(same & (ci <= ri), 1.0, 0.0).astype(BF16)
    rt = lax.broadcasted_iota(jnp.int32, (LANE, LANE), 0)
    ct = lax.broadcasted_iota(jnp.int32, (LANE, LANE), 1)
    tril = ((rt // HG_CHUNK) == (ct // HG_CHUNK)) & (ct <= rt)
    per = sup // HG_CHUNK
    groups = [slice(g * LANE, (g + 1) * LANE) for g in range(sup // LANE)]

    heads = range(HG_HEADS)
    hs = [slice(h * LANE, (h + 1) * LANE) for h in heads]

    def wide(ref, r0):
        return jnp.concatenate([ref[h, pl.ds(r0, sup), :] for h in heads], axis=1)

    def body(i, carry):
        r0 = pl.multiple_of(i * sup, sup)
        lb = lb_ref[...]
        f = lb + (1.0 - lb) * jax.nn.sigmoid(wide(f_ref, r0))
        lf = jnp.log(f)
        k = 1.0 - f
        hi = lf.astype(BF16)
        lo = (lf - hi.astype(F32)).astype(BF16)
        b = _dot(cum_m, hi) + _dot(cum_m, lo)
        chunk = [slice(c * HG_CHUNK, (c + 1) * HG_CHUNK) for c in range(per)]
        dec = [jnp.exp(b[(c + 1) * HG_CHUNK - 1:(c + 1) * HG_CHUNK, :]) for c in range(per)]
        dec_rows = jnp.concatenate([jnp.broadcast_to(dec[c], (HG_CHUNK, dec[c].shape[1])) for c in range(per)],
                                   axis=0)
        q_in = (wide(q_ref, r0) * jnp.exp(b)).astype(BF16)
        k_dec = k * jnp.exp(-b)
        k_in = k_dec.astype(BF16)
        k_end = (k_dec * dec_rows).astype(BF16)
        vb = wide(v_ref, r0).astype(BF16)

        att = [[jnp.where(tril, _dot_nt(q_in[g, hs[h]], k_in[g, hs[h]]), 0.0).astype(BF16) for g in groups]
               for h in heads]
        upd = [[_dot_tn(vb[chunk[c], hs[h]], k_end[chunk[c], hs[h]]) for c in range(per)] for h in heads]
        intra = [[_dot(att[h][n], vb[g, hs[h]]) for n, g in enumerate(groups)] for h in heads]

        st = [st_ref[h] for h in heads]
        inter = [[] for _ in heads]
        for c in range(per):
            for h in heads:
                inter[h].append(_dot_nt(q_in[chunk[c], hs[h]], st[h].astype(BF16)))
                st[h] = dec[c][:, hs[h]] * st[h] + upd[h][c]
        for h in heads:
            st_ref[h] = st[h]

        o = [jnp.concatenate(intra[h], axis=0) + jnp.concatenate(inter[h], axis=0) for h in heads]
        scale = [lax.rsqrt(jnp.mean(o[h] * o[h], axis=-1, keepdims=True) + 1e-6) for h in heads]
        on = jnp.concatenate([o[h] * scale[h] for h in heads], axis=1)
        on = on * nw_ref[...] * jax.nn.sigmoid(wide(g_ref, r0))
        o_ref[pl.ds(r0, sup), :] = on.astype(BF16)
        return carry

    lax.fori_loop(0, rows // sup, body, 0)


def _hgrn(proj, lb, nw, batch, seq):
    nt = proj.shape[1]
    tb = min(HGRN_ROWS, seq)
    nblk = seq // tb

    def slab(cb0):
        return pl.BlockSpec((HG_HEADS, tb, LANE), lambda b, t: (cb0 // HG_HEADS, b * nblk + t, 0))

    vec = pl.BlockSpec((1, HG_HEADS * LANE), lambda b, t: (0, 0))
    return pl.pallas_call(
        functools.partial(_hgrn_kernel, rows=tb),
        grid=(batch, nblk),
        in_specs=[slab(CB_HQ), slab(CB_HF), slab(CB_HI), slab(CB_HG), vec, vec],
        out_specs=pl.BlockSpec((tb, HG_HEADS * HG_DV), lambda b, t: (b * nblk + t, 0)),
        out_shape=jax.ShapeDtypeStruct((nt, HG_HEADS * HG_DV), BF16),
        scratch_shapes=[pltpu.VMEM((HG_HEADS, HG_DV, HG_DK), F32)],
        compiler_params=_cparams(("arbitrary", "arbitrary")),
        name="hgrn2",
    )(proj, proj, proj, proj, lb, nw)


def _compress_kernel(x_ref, pos_ref, w1_ref, b1_ref, w2_ref, o_ref, *, ns):
    p0 = jnp.zeros((ns, LANE), F32)
    p1 = jnp.zeros((ns, LANE), F32)
    for j in range(CMP_STRIDE):
        tok = x_ref[0, pl.ds(j, ns, stride=CMP_STRIDE), :]
        rows = slice(j * LANE, (j + 1) * LANE)
        late = slice((CMP_STRIDE + j) * LANE, (CMP_STRIDE + j + 1) * LANE)
        p0 = p0 + _dot((tok + pos_ref[0, j:j + 1, :]).astype(BF16), w1_ref[0, rows, :])
        p1 = p1 + _dot((tok + pos_ref[0, CMP_STRIDE + j:CMP_STRIDE + j + 1, :]).astype(BF16), w1_ref[0, late, :])
    h = p0 + pltpu.roll(p1, ns - 1, axis=0) + b1_ref[0]
    a = h * jax.nn.sigmoid(h)
    out = _dot(a.astype(BF16), w2_ref[0])
    row = lax.broadcasted_iota(jnp.int32, out.shape, 0)
    out = jnp.where(row < ns - 1, out, 0.0)
    o_ref[0, 0, 0] = out


def _compress(proj, pos, w1, b1, w2, batch, seq):
    ns = seq // CMP_STRIDE
    np_rows = ns
    width = CMP_STRIDE * LANE
    return pl.pallas_call(
        functools.partial(_compress_kernel, ns=ns),
        grid=(2, batch, NSA_GROUPS),
        in_specs=[pl.BlockSpec((1, seq, LANE), lambda s, b, g: (CB_KC + NSA_GROUPS * s + g, b, 0)),
                  pl.BlockSpec((1, CMP_BLOCK, LANE), lambda s, b, g: (s, 0, 0)),
                  pl.BlockSpec((1, 2 * width, LANE), lambda s, b, g: (s, 0, 0)),
                  pl.BlockSpec((1, 1, LANE), lambda s, b, g: (s, 0, 0)),
                  pl.BlockSpec((1, LANE, LANE), lambda s, b, g: (s, 0, 0))],
        out_specs=pl.BlockSpec((1, 1, 1, np_rows, LANE), lambda s, b, g: (s, b, g, 0, 0)),
        out_shape=jax.ShapeDtypeStruct((2, batch, NSA_GROUPS, np_rows, LANE), F32),
        compiler_params=_cparams(("arbitrary", "arbitrary", "arbitrary")),
        name="nsa_compress",
    )(proj, pos, w1, b1, w2)


def _cmp_sel_kernel(q0, q1, q2, q3, kc_ref, vc_ref, gate_ref, tt_ref, ov_ref, oc_ref, sel_ref,
                    kb_ref, vt_ref, imp_ref, *, np_rows):
    step = pl.program_id(2)
    cols = NSA_REP * QT
    ns = np_rows

    @pl.when(step == 0)
    def _():
        kb_ref[...] = (kc_ref[0, 0, 0] * SCALE_LOG2).astype(BF16)
        for c in range(np_rows // LANE):
            vt_ref[:, c * LANE:(c + 1) * LANE] = vc_ref[0, 0, 0, c * LANE:(c + 1) * LANE, :].T.astype(BF16)

    subs = range(CMP_SUB)
    tis = [step * CMP_SUB + u for u in subs]
    qts = [_q_transposed((q0, q1, q2, q3), u) for u in subs]

    def group_sum(p):
        tot = p[:, 0:QT]
        for r in range(1, NSA_REP):
            tot = tot + p[:, r * QT:(r + 1) * QT]
        return tot

    def attend(nrows):
        kb = kb_ref[0:nrows, :]
        bias = [tt_ref[0, pl.ds(pl.multiple_of(ns + CMP_PAD - tis[u] * (QT // CMP_STRIDE), 8), nrows), :]
                for u in subs]
        ss = [_dot(kb, qts[u]) + bias[u] for u in subs]
        ms = [jnp.max(ss[u], axis=0, keepdims=True) for u in subs]
        ps = [jnp.exp2(ss[u] - ms[u]) for u in subs]
        ls = [jnp.sum(ps[u], axis=0, keepdims=True) for u in subs]
        invs = [jnp.where(ms[u] > 0.5 * NEG, 1.0 / ls[u], 0.0) for u in subs]
        pn = [ps[u] * invs[u] for u in subs]
        vt = vt_ref[:, 0:nrows]
        os_ = [_dot(vt, pn[u].astype(BF16)) for u in subs]
        for u in subs:
            rows = slice(u * QT, (u + 1) * QT)
            gt = jax.nn.sigmoid(gate_ref[0, rows, :])
            for r in range(NSA_REP):
                oc_ref[rows, r * LANE:(r + 1) * LANE] = gt[:, 3 * r:3 * r + 1] * os_[u][:, r * QT:(r + 1) * QT].T
            nblk = nrows * CMP_STRIDE // SLC_BLOCK
            imp_ref[0:nblk, rows] = _dot01(ov_ref[0:nblk, 0:nrows], group_sum(pn[u]))

    width = CMP_SUB * QT

    def choose(nblk):
        assert FORCE_SCORE > NSA_REP and nblk >= SLC_TOPK
        jj = lax.broadcasted_iota(jnp.int32, (nblk, width), 0)
        tok = step * width + lax.broadcasted_iota(jnp.int32, (nblk, width), 1)
        cur = tok // SLC_BLOCK
        forced = (jj == 0) | (jj == cur) | (jj == cur - 1)
        score = jnp.where(forced, -jnp.inf, jnp.where(jj <= cur, imp_ref[0:nblk, :], -1.0))
        jf = jj.astype(F32)
        for _ in range(SLC_TOPK - N_FORCED):
            mval = jnp.max(score, axis=0, keepdims=True)
            first = jnp.min(jnp.where(score == mval, jf, float(LANE)), axis=0, keepdims=True)
            score = jnp.where(jf == first, -jnp.inf, score)
        sel_ref[0, 0, 0:nblk, :] = jnp.where(score == -jnp.inf, 0.0, NEG)
        if nblk < LANE:
            sel_ref[0, 0, nblk:LANE, :] = jnp.full((LANE - nblk, width), NEG, F32)

    visible = (step + 1) * CMP_SUB * (QT // CMP_STRIDE)
    ngroups = np_rows // LANE
    for g in range(1, ngroups + 1):
        upper = visible <= g * LANE if g < ngroups else True

        @pl.when((visible > (g - 1) * LANE) & upper)
        def _():
            attend(g * LANE)
            choose(g * LANE * CMP_STRIDE // SLC_BLOCK)


def _q_specs(nsteps, rows=QT):
    return [pl.BlockSpec((1, rows, LANE),
                         functools.partial(lambda b, g, t, r: (CB_NQ + NSA_REP * g + r, b * nsteps + t, 0), r=r))
            for r in range(NSA_REP)]


def _cmp_sel(proj, kvc, tt, ov_t, batch, seq):
    nt = proj.shape[1]
    rows = CMP_SUB * QT
    nsteps = seq // rows
    cols = NSA_REP * QT
    np_rows = kvc.shape[3]
    kv_spec = lambda s: pl.BlockSpec((1, 1, 1, np_rows, LANE), lambda b, g, t: (s, b, g, 0, 0))
    return pl.pallas_call(
        functools.partial(_cmp_sel_kernel, np_rows=np_rows),
        grid=(batch, NSA_GROUPS, nsteps),
        in_specs=_q_specs(nsteps, rows) + [
            kv_spec(0), kv_spec(1),
            pl.BlockSpec((1, rows, LANE), lambda b, g, t: (CB_GATE + g, b * nsteps + t, 0)),
            pl.BlockSpec((1, tt.shape[1], cols), lambda b, g, t: (g, 0, 0)),
            pl.BlockSpec((LANE, np_rows), lambda b, g, t: (0, 0))],
        out_specs=[pl.BlockSpec((rows, cols), lambda b, g, t: (b * nsteps + t, g)),
                   pl.BlockSpec((1, 1, LANE, rows), lambda b, g, t: (b, g, 0, t))],
        out_shape=[jax.ShapeDtypeStruct((nt, NSA_HEADS * LANE), F32),
                   jax.ShapeDtypeStruct((batch, NSA_GROUPS, LANE, seq), F32)],
        scratch_shapes=[pltpu.VMEM((np_rows, LANE), BF16), pltpu.VMEM((LANE, np_rows), BF16),
                        pltpu.VMEM((LANE, rows), F32)],
        compiler_params=_cparams(("arbitrary", "arbitrary", "arbitrary")),
        name="nsa_cmp_select",
    )(proj, proj, proj, proj, kvc, kvc, proj, tt, ov_t)


def _q_transposed(q_refs, sub=0):
    return jnp.concatenate([r[0, sub * QT:(sub + 1) * QT, :].T for r in q_refs], axis=1).astype(BF16)


def _finish_t(state, gate_ref, branch, o_ref, sub=0, others=()):
    _, l, acc = state
    o = acc / jnp.where(l == 0.0, 1.0, l)
    rows = slice(sub * QT, (sub + 1) * QT)
    gt = jax.nn.sigmoid(gate_ref[0, rows, :])
    for r in range(NSA_REP):
        col = 3 * r + branch
        lanes = slice(r * LANE, (r + 1) * LANE)
        val = gt[:, col:col + 1] * o[:, r * QT:(r + 1) * QT].T
        for other in others:
            val = val + other[rows, lanes]
        o_ref[rows, lanes] = val.astype(o_ref.dtype)


def _init_state(cols):
    return (jnp.full((1, cols), NEG, F32), jnp.zeros((1, cols), F32), jnp.zeros((LANE, cols), F32))


def _win_kernel(q0, q1, q2, q3, k_ref, v_ref, gate_ref, wb_ref, o_ref, kb_ref, vt_ref, *, seq):
    step = pl.program_id(2)
    cols = NSA_REP * QT
    wt = WINDOW // QT

    lane = lax.broadcasted_iota(jnp.int32, (QT, LANE), 1)
    pad_mark = jnp.where(lane == 0, 1.0, 0.0).astype(BF16)

    @pl.when(step == 0)
    def _():
        for i in range(wt):
            kb_ref[i * QT:(i + 1) * QT, 0:LANE] = jnp.zeros((QT, LANE), BF16)
            kb_ref[i * QT:(i + 1) * QT, LANE:2 * LANE] = pad_mark
            vt_ref[i] = jnp.zeros((LANE, QT), BF16)

        def fill(i, carry):
            r0 = pl.multiple_of(i * QT, QT)
            kb_ref[pl.ds(WINDOW + r0, QT), 0:LANE] = (k_ref[0, pl.ds(r0, QT), :] * SCALE_LOG2).astype(BF16)
            kb_ref[pl.ds(WINDOW + r0, QT), LANE:2 * LANE] = jnp.zeros((QT, LANE), BF16)
            vt_ref[wt + i] = v_ref[0, pl.ds(r0, QT), :].T.astype(BF16)
            return carry

        lax.fori_loop(0, seq // QT, fill, 0)

    subs = range(WIN_SUB)
    tis = [step * WIN_SUB + sub for sub in subs]
    row = lax.broadcasted_iota(jnp.int32, (LANE, cols), 0)
    pad_rows = jnp.where(row == 0, NEG, 0.0).astype(BF16)
    qts = [jnp.concatenate([_q_transposed((q0, q1, q2, q3), sub), pad_rows], axis=0) for sub in subs]
    ss = [_dot(kb_ref[pl.ds(pl.multiple_of(tis[u] * QT, QT), WINDOW + QT), :], qts[u]) + wb_ref[0]
          for u in subs]
    ms = [jnp.max(ss[u], axis=0, keepdims=True) for u in subs]
    ps = [jnp.exp2(ss[u] - ms[u]) for u in subs]
    ls = [jnp.sum(ps[u], axis=0, keepdims=True) for u in subs]
    pbs = [ps[u].astype(BF16) for u in subs]
    accs = None
    for d in range(0, wt + 1, 2):
        n = min(2, wt + 1 - d)
        vts = [jnp.concatenate([vt_ref[tis[u] + d + e] for e in range(n)], axis=1) for u in subs]
        part = [_dot(vts[u], pbs[u][d * QT:(d + n) * QT]) for u in subs]
        accs = part if accs is None else [accs[u] + part[u] for u in subs]
    for u in subs:
        _finish_t((ms[u], ls[u], accs[u]), gate_ref, 2, o_ref, u)


def _window(proj, wb, batch, seq):
    nt = proj.shape[1]
    rows = WIN_SUB * QT
    nqt = seq // rows
    cols = NSA_REP * QT
    slab = lambda cb0: pl.BlockSpec((1, seq, LANE), lambda b, g, t: (cb0 + g, b, 0))
    return pl.pallas_call(
        functools.partial(_win_kernel, seq=seq),
        grid=(batch, NSA_GROUPS, nqt),
        in_specs=_q_specs(nqt, rows) + [
            slab(CB_KW), slab(CB_VW),
            pl.BlockSpec((1, rows, LANE), lambda b, g, t: (CB_GATE + g, b * nqt + t, 0)),
            pl.BlockSpec((1, WINDOW + QT, cols), lambda b, g, t: (g, 0, 0))],
        out_specs=pl.BlockSpec((rows, cols), lambda b, g, t: (b * nqt + t, g)),
        out_shape=jax.ShapeDtypeStruct((nt, NSA_HEADS * LANE), F32),
        scratch_shapes=[pltpu.VMEM((seq + WINDOW, 2 * LANE), BF16),
                        pltpu.VMEM((seq // QT + WINDOW // QT, LANE, QT), BF16)],
        compiler_params=_cparams(("arbitrary", "arbitrary", "arbitrary")),
        name="nsa_window",
    )(proj, proj, proj, proj, proj, proj, proj, wb)


def _softmax_steps(ss, vt, states):
    n = range(len(ss))
    m_new = [jnp.maximum(states[u][0], jnp.max(ss[u], axis=0, keepdims=True)) for u in n]
    alpha = [jnp.exp2(states[u][0] - m_new[u]) for u in n]
    p = [jnp.exp2(ss[u] - m_new[u]) for u in n]
    l = [alpha[u] * states[u][1] + jnp.sum(p[u], axis=0, keepdims=True) for u in n]
    vts = vt if isinstance(vt, (list, tuple)) else [vt] * len(ss)
    pv = [_dot(vts[u], p[u].astype(BF16)) for u in n]
    return [(m_new[u], l[u], alpha[u] * states[u][2] + pv[u]) for u in n]


def _sel_kernel(q0, q1, q2, q3, k_ref, v_ref, sel_ref, gate_ref, nd_ref, npv_ref, oc_ref, ow_ref, o_ref,
                kb_ref, vt_ref, qa_ref, kn_ref, m_ref, l_ref, acc_ref, *, seq):
    step = pl.program_id(2)
    cols = NSA_REP * QT
    spb = KEY_BLK // SLC_BLOCK
    subs = range(SEL_SUB)

    @pl.when(step == 0)
    def _():
        blk = lax.broadcasted_iota(jnp.int32, (KEY_BLK, LANE), 0) // SLC_BLOCK
        onehot = jnp.where(blk == lax.broadcasted_iota(jnp.int32, (KEY_BLK, LANE), 1), 1.0, 0.0).astype(BF16)

        def fill(i, kmax2):
            r0 = pl.multiple_of(i * KEY_BLK, KEY_BLK)
            kblk = (k_ref[0, pl.ds(r0, KEY_BLK), :] * SCALE_LOG2).astype(BF16)
            kb_ref[pl.ds(r0, KEY_BLK), 0:LANE] = kblk
            kb_ref[pl.ds(r0, KEY_BLK), LANE:2 * LANE] = onehot
            for c in range(SEL_SUB):
                vt_ref[i, :, c * QT:(c + 1) * QT] = v_ref[0, pl.ds(r0 + c * QT, QT), :].T.astype(BF16)
            n2 = jnp.sum(jnp.square(kblk.astype(F32)), axis=1, keepdims=True)
            return jnp.maximum(kmax2, jnp.max(n2, axis=0, keepdims=True))

        kmax2 = lax.fori_loop(0, seq // KEY_BLK, fill, jnp.zeros((1, 1), F32))
        kn_ref[...] = jnp.sqrt(kmax2)
        qa_ref[...] = jnp.zeros_like(qa_ref)

    qts = [_q_transposed((q0, q1, q2, q3), u) for u in subs]
    for u in subs:
        qa_ref[u, 0:LANE, :] = qts[u]
    bounds = [jnp.sqrt(jnp.sum(jnp.square(qts[u].astype(F32)), axis=0, keepdims=True)) * kn_ref[...]
              * BOUND_SLACK + 1.0 for u in subs]

    def block(kb):
        r0 = pl.multiple_of(kb * KEY_BLK, KEY_BLK)
        return kb_ref[pl.ds(r0, KEY_BLK), :], vt_ref[kb]

    pad_rows = jnp.zeros((SEL_ROWS - spb, cols), F32)

    def scores(kb, k, rows_of=lambda u: KEY_BLK):
        for u in subs:
            rows = sel_ref[0, 0, pl.ds(pl.multiple_of(kb * spb, spb), spb), u * QT:(u + 1) * QT]
            rows = jnp.concatenate([rows] * NSA_REP, axis=1)
            qa_ref[u, LANE:LANE + SEL_ROWS, :] = jnp.concatenate([rows, pad_rows], axis=0).astype(BF16)
        return [_dot(k[0:rows_of(u)], qa_ref[u]) for u in subs]

    seen = lambda u: (u + 1) * QT
    k, vt = block(step)
    ss = scores(step, k, seen)
    ss = [ss[u] + nd_ref[0, u, 0:seen(u), :] for u in subs]
    states = _softmax_steps(ss, [vt[:, 0:seen(u)] for u in subs], [_init_state(cols) for _ in subs])

    for u in subs:
        m_ref[u], l_ref[u], acc_ref[u] = states[u]

    def prev_scores():
        k, vt = block(step - 1)
        ss = scores(step - 1, k)
        ss[0] = ss[0] + npv_ref[0]
        return ss, vt

    n_old = jnp.maximum(step - 1, 0)
    bounds[0] = bounds[0] + jnp.max(npv_ref[0], axis=0, keepdims=True)
    m_fix = [jnp.maximum(states[u][0], bounds[u]) for u in subs]
    gap = jnp.max(jnp.concatenate([m_fix[u] - states[u][0] for u in subs], axis=1))
    fixed_ok = gap < MAX_REF_GAP

    @pl.when(fixed_ok)
    def _():
        for u in subs:
            alpha = jnp.exp2(m_ref[u] - m_fix[u])
            l_ref[u] = alpha * l_ref[u]
            acc_ref[u] = alpha * acc_ref[u]
            m_ref[u] = m_fix[u]

        def fixed_step(ss, vt):
            ps = [jnp.exp2(ss[u] - m_ref[u]) for u in subs]
            for u in subs:
                l_ref[u] = l_ref[u] + jnp.sum(ps[u], axis=0, keepdims=True)
            pv = [_dot(vt, ps[u].astype(BF16)) for u in subs]
            for u in subs:
                acc_ref[u] = acc_ref[u] + pv[u]

        @pl.when(step >= 1)
        def _():
            fixed_step(*prev_scores())

        def body(kb, carry):
            k, vt = block(kb)
            fixed_step(scores(kb, k), vt)
            return carry

        lax.fori_loop(0, n_old, body, 0)

    @pl.when(jnp.logical_not(fixed_ok))
    def _():
        def online_step(ss, vt):
            st = _softmax_steps(ss, vt, [(m_ref[u], l_ref[u], acc_ref[u]) for u in subs])
            for u in subs:
                m_ref[u], l_ref[u], acc_ref[u] = st[u]

        @pl.when(step >= 1)
        def _():
            online_step(*prev_scores())

        def body(kb, carry):
            k, vt = block(kb)
            online_step(scores(kb, k), vt)
            return carry

        lax.fori_loop(0, n_old, body, 0)

    for u in subs:
        _finish_t((m_ref[u], l_ref[u], acc_ref[u]), gate_ref, 1, o_ref, u, others=(oc_ref, ow_ref))


def _selected(proj, sel_t, nd, npv, o_c, o_w, batch, seq):
    nt = proj.shape[1]
    rows = SEL_SUB * QT
    nsteps = seq // rows
    cols = NSA_REP * QT
    slab = lambda cb0: pl.BlockSpec((1, seq, LANE), lambda b, g, t: (cb0 + g, b, 0))
    return pl.pallas_call(
        functools.partial(_sel_kernel, seq=seq),
        grid=(batch, NSA_GROUPS, nsteps),
        in_specs=_q_specs(nsteps, rows) + [
            slab(CB_KS), slab(CB_VS),
            pl.BlockSpec((1, 1, LANE, rows), lambda b, g, t: (b, g, 0, t)),
            pl.BlockSpec((1, rows, LANE), lambda b, g, t: (CB_GATE + g, b * nsteps + t, 0)),
            pl.BlockSpec((1, SEL_SUB, KEY_BLK, cols), lambda b, g, t: (g, 0, 0, 0)),
            pl.BlockSpec((1, KEY_BLK, cols), lambda b, g, t: (g, 0, 0)),
            pl.BlockSpec((rows, cols), lambda b, g, t: (b * nsteps + t, g)),
            pl.BlockSpec((rows, cols), lambda b, g, t: (b * nsteps + t, g))],
        out_specs=pl.BlockSpec((rows, cols), lambda b, g, t: (b * nsteps + t, g)),
        out_shape=jax.ShapeDtypeStruct((nt, NSA_HEADS * LANE), BF16),
        scratch_shapes=[pltpu.VMEM((seq, 2 * LANE), BF16),
                        pltpu.VMEM((seq // KEY_BLK, LANE, KEY_BLK), BF16),
                        pltpu.VMEM((SEL_SUB, 2 * LANE, cols), BF16),
                        pltpu.VMEM((1, 1), F32),
                        pltpu.VMEM((SEL_SUB, 1, cols), F32), pltpu.VMEM((SEL_SUB, 1, cols), F32),
                        pltpu.VMEM((SEL_SUB, LANE, cols), F32)],
        compiler_params=_cparams(("arbitrary", "arbitrary", "arbitrary")),
        name="nsa_selected",
    )(proj, proj, proj, proj, proj, proj, sel_t, proj, nd, npv, o_c, o_w)


def _merge_kernel(oh_ref, on_ref, mgh_ref, mgn_ref, x_ref, mod_ref,
                  wh_ref, wn_ref, wo_ref, g_ref, b_ref, wr_ref, br_ref,
                  x1_ref, h2_ref, lg_ref):
    nblk = D_MODEL // LANE
    tm = x_ref.shape[0]
    halves = [slice(s * tm // MERGE_SPLIT, (s + 1) * tm // MERGE_SPLIT) for s in range(MERGE_SPLIT)]
    a_h = [_dot(oh_ref[rs, :], wh_ref[...]) for rs in halves]
    a_n = [_dot(on_ref[rs, :], wn_ref[...]) for rs in halves]
    gh = [jnp.concatenate([mgh_ref[c, rs, :] for c in range(nblk)], axis=-1) for rs in halves]
    gn = [jnp.concatenate([mgn_ref[c, rs, :] for c in range(nblk)], axis=-1) for rs in halves]
    merged = [(jax.nn.sigmoid(gh[s]) * a_h[s] + jax.nn.sigmoid(gn[s]) * a_n[s]).astype(BF16)
              for s in range(MERGE_SPLIT)]
    y = [(1.0 + mod_ref[0, 2:3, :]) * _dot(merged[s], wo_ref[...]) for s in range(MERGE_SPLIT)]
    x1 = [_layer_norm(ALPHA * x_ref[rs, :] + y[s]) * g_ref[...] + b_ref[...] for s, rs in enumerate(halves)]
    h2 = [_layer_norm(x1[s]) * (1.0 + mod_ref[0, 4:5, :]) + mod_ref[0, 3:4, :] for s in range(MERGE_SPLIT)]
    h_hi = [h2[s].astype(BF16) for s in range(MERGE_SPLIT)]
    h_lo = [(h2[s] - h_hi[s].astype(F32)).astype(BF16) for s in range(MERGE_SPLIT)]
    lg = [_dot(h_hi[s], wr_ref[0]) + _dot(h_hi[s], wr_ref[1]) + _dot(h_lo[s], wr_ref[0]) + br_ref[...]
          for s in range(MERGE_SPLIT)]
    for s, rs in enumerate(halves):
        x1_ref[rs, :] = x1[s]
        _store_rows(h2_ref, h2[s], rs.start)
        lg_ref[rs, :] = lg[s]


def _merge(o_h, o_n, proj, x2, mod, w_h, w_n, w_o, ln_g, ln_b, w_r, b_r, seq):
    nt, d = x2.shape
    tm = min(MERGE_ROWS, seq)
    nblk = d // LANE
    row = lambda w: pl.BlockSpec((tm, w), lambda i: (i, 0))
    full = lambda a: pl.BlockSpec(a.shape, lambda i: (0,) * a.ndim)
    return pl.pallas_call(
        _merge_kernel,
        grid=(nt // tm,),
        in_specs=[row(d), row(d),
                  pl.BlockSpec((nblk, tm, LANE), lambda i: (CB_MGH // nblk, i, 0)),
                  pl.BlockSpec((nblk, tm, LANE), lambda i: (CB_MGN // nblk, i, 0)),
                  row(d),
                  pl.BlockSpec((1, 6, d), lambda i: (i * tm // seq, 0, 0)),
                  full(w_h), full(w_n), full(w_o), full(ln_g), full(ln_b), full(w_r), full(b_r)],
        out_specs=[row(d), pl.BlockSpec((tm * ROW_TILES, LANE), lambda i: (i, 0)), row(LANE)],
        out_shape=[jax.ShapeDtypeStruct((nt, d), F32),
                   jax.ShapeDtypeStruct((nt * ROW_TILES, LANE), F32),
                   jax.ShapeDtypeStruct((nt, LANE), F32)],
        compiler_params=_cparams(("arbitrary",)),
        name="merge_outproj",
    )(o_h, o_n, proj, proj, x2, mod, w_h, w_n, w_o, ln_g, ln_b, w_r, b_r)


def _route_kernel(lg_ref, rec_ref, cnt_ref, rect_ref, carry_ref, *, tm):
    @pl.when(pl.program_id(0) == 0)
    def _():
        carry_ref[...] = jnp.zeros_like(carry_ref)

    lg = lg_ref[...]
    lane = lax.broadcasted_iota(jnp.int32, (tm, LANE), 1).astype(F32)
    far = float(LANE)
    gmask = lane < N_GROUPS
    gl = jnp.where(gmask, lg, -jnp.inf)
    gmax = jnp.max(gl, axis=-1, keepdims=True)
    gsum = jnp.sum(jnp.where(gmask, jnp.exp(gl - gmax), 0.0), axis=-1, keepdims=True)
    grp_p = 1.0 / gsum
    gidx = jnp.min(jnp.where(gl == gmax, lane, far), axis=-1, keepdims=True)
    lo = N_GROUPS + EXP_PER_GROUP * gidx
    emask = (lane >= lo) & (lane < lo + EXP_PER_GROUP)
    el = jnp.where(emask, lg, -jnp.inf)
    m1 = jnp.max(el, axis=-1, keepdims=True)
    i1 = jnp.min(jnp.where(el == m1, lane, far), axis=-1, keepdims=True)
    el2 = jnp.where(lane == i1, -jnp.inf, el)
    m2 = jnp.max(el2, axis=-1, keepdims=True)
    i2 = jnp.min(jnp.where(emask & (lane != i1) & (el2 == m2), lane, far), axis=-1, keepdims=True)
    e = jnp.exp(m2 - m1)
    w0 = grp_p / (1.0 + e)
    w1 = grp_p * e / (1.0 + e)

    oh0 = lane == i1
    oh1 = lane == i2
    f0 = jnp.where(oh0, 1.0, 0.0)
    f1 = jnp.where(oh1, 1.0, 0.0)
    ri = lax.broadcasted_iota(jnp.int32, (tm, tm), 0)
    ci = lax.broadcasted_iota(jnp.int32, (tm, tm), 1)
    before = jnp.where(ci < ri, 1.0, 0.0).astype(BF16)
    cum0 = _dot(before, f0.astype(BF16))
    cum1 = _dot(before, f1.astype(BF16))
    tot0 = jnp.sum(f0, axis=0, keepdims=True)
    tot1 = jnp.sum(f1, axis=0, keepdims=True)
    carry = carry_ref[...]
    rank0 = jnp.sum(jnp.where(oh0, carry + cum0, 0.0), axis=-1, keepdims=True)
    rank1 = jnp.sum(jnp.where(oh1, carry + tot0 + cum1, 0.0), axis=-1, keepdims=True)
    carry = carry + tot0 + tot1
    carry_ref[...] = carry
    cnt_ref[...] = carry

    rec = jnp.where(lane == 0, i1 - N_GROUPS, 0.0)
    rec = jnp.where(lane == 1, i2 - N_GROUPS, rec)
    rec = jnp.where(lane == 2, w0, rec)
    rec = jnp.where(lane == 3, w1, rec)
    rec = jnp.where(lane == 4, rank0, rec)
    rec = jnp.where(lane == 5, rank1, rec)
    rec_ref[...] = rec
    rect_ref[...] = jnp.concatenate([rec[c * LANE:(c + 1) * LANE, :].T[0:8, :] for c in range(tm // LANE)], axis=1)


def _route(logits):
    nt = logits.shape[0]
    tm = min(ROUTE_ROWS, nt)
    return pl.pallas_call(
        functools.partial(_route_kernel, tm=tm),
        grid=(nt // tm,),
        in_specs=[pl.BlockSpec((tm, LANE), lambda i: (i, 0))],
        out_specs=[pl.BlockSpec((tm, LANE), lambda i: (i, 0)),
                   pl.BlockSpec((1, LANE), lambda i: (0, 0)),
                   pl.BlockSpec((8, tm), lambda i: (0, i))],
        out_shape=[jax.ShapeDtypeStruct((nt, LANE), F32),
                   jax.ShapeDtypeStruct((1, LANE), F32),
                   jax.ShapeDtypeStruct((8, nt), F32)],
        scratch_shapes=[pltpu.VMEM((1, LANE), F32)],
        compiler_params=_cparams(("arbitrary",)),
        name="moe_route",
    )(logits)


def _row_copy(src, dst, sem):
    return pltpu.make_async_copy(src, dst, sem)


def _tile_of(r):
    return pl.ds(pl.multiple_of(r * ROW_TILES, ROW_TILES), ROW_TILES)


def _load_rows(ref, n, lead=()):
    return jnp.concatenate([ref[lead + (pl.ds(c, n, stride=ROW_TILES), slice(None))]
                            for c in range(ROW_TILES)], axis=1)


def _store_rows(ref, val, row0=0):
    n = val.shape[0]
    for c in range(ROW_TILES):
        ref[pl.ds(row0 * ROW_TILES + c, n, stride=ROW_TILES), :] = val[:, c * LANE:(c + 1) * LANE]


def _dispatch_kernel(dest0_ref, dest1_ref, zb_ref, h_ref, xp_ref, z_ref, sem, zsem, *, tm):
    step = pl.program_id(0)
    base = step * tm
    blk = MOE_ROWS * ROW_TILES

    @pl.when(step == 0)
    def _():
        z_ref[...] = jnp.zeros_like(z_ref)

        def zero_copy(j):
            b = jnp.maximum(zb_ref[j], 0)
            return _row_copy(z_ref, xp_ref.at[pl.ds(pl.multiple_of(b * blk, blk), blk), :], zsem)

        def start(j, carry):
            @pl.when(zb_ref[j] >= 0)
            def _():
                zero_copy(j).start()
            return carry

        def wait(j, carry):
            @pl.when(zb_ref[j] >= 0)
            def _():
                zero_copy(j).wait()
            return carry

        lax.fori_loop(0, 2 * N_EXPERTS, start, 0)
        lax.fori_loop(0, 2 * N_EXPERTS, wait, 0)

    def issue(r, carry):
        for dest_ref in (dest0_ref, dest1_ref):
            d = dest_ref[base + r]
            _row_copy(h_ref.at[_tile_of(r), :], xp_ref.at[_tile_of(d), :], sem).start()
        return carry

    lax.fori_loop(0, tm, issue, 0, unroll=ROW_DMA_UNROLL)
    for _ in range(2):
        _row_copy(h_ref, xp_ref.at[pl.ds(0, tm * ROW_TILES), :], sem).wait()


def _dispatch(dest0, dest1, zero_blocks, h2, n_blocks):
    nt = h2.shape[0] // ROW_TILES
    tm = min(DISPATCH_TOKENS, nt)
    return pl.pallas_call(
        functools.partial(_dispatch_kernel, tm=tm),
        grid_spec=pltpu.PrefetchScalarGridSpec(
            num_scalar_prefetch=3,
            grid=(nt // tm,),
            in_specs=[pl.BlockSpec((tm * ROW_TILES, LANE), lambda i, *_: (i, 0))],
            out_specs=pl.BlockSpec(memory_space=pl.ANY),
            scratch_shapes=[pltpu.VMEM((MOE_ROWS * ROW_TILES, LANE), F32),
                            pltpu.SemaphoreType.DMA(()), pltpu.SemaphoreType.DMA(())]),
        out_shape=jax.ShapeDtypeStruct((n_blocks * MOE_ROWS * ROW_TILES, LANE), F32),
        compiler_params=_cparams(("arbitrary",)),
        name="moe_dispatch",
    )(dest0, dest1, zero_blocks, h2)


def _expert_kernel(be_ref, nu_ref, x_ref, w1_ref, w3_ref, w2_ref, y_ref, w1b_ref, w3b_ref, w2b_ref):
    i = pl.program_id(0)

    @pl.when((i < nu_ref[0]) & ((i == 0) | (be_ref[i] != be_ref[jnp.maximum(i - 1, 0)])))
    def _():
        w1b_ref[...] = w1_ref[0].astype(BF16)
        w3b_ref[...] = w3_ref[0].astype(BF16)
        w2b_ref[...] = w2_ref[0].astype(BF16)

    @pl.when(i < nu_ref[0])
    def _():
        xb = _load_rows(x_ref, MOE_ROWS).astype(BF16)
        a = _dot(xb, w1b_ref[...])
        b = _dot(xb, w3b_ref[...])
        hmid = (a * jax.nn.sigmoid(a) * b).astype(BF16)
        _store_rows(y_ref, _dot(hmid, w2b_ref[...]))

    @pl.when(i >= nu_ref[0])
    def _():
        y_ref[...] = jnp.zeros_like(y_ref)


def _experts(block_expert, n_used, x_pad, w1, w3, w2):
    d, de = w1.shape[1], w1.shape[2]
    nb = x_pad.shape[0] // (MOE_ROWS * ROW_TILES)
    return pl.pallas_call(
        _expert_kernel,
        grid_spec=pltpu.PrefetchScalarGridSpec(
            num_scalar_prefetch=2,
            grid=(nb,),
            in_specs=[pl.BlockSpec((MOE_ROWS * ROW_TILES, LANE), lambda i, be, nu: (i, 0)),
                      pl.BlockSpec((1, d, de), lambda i, be, nu: (be[i], 0, 0)),
                      pl.BlockSpec((1, d, de), lambda i, be, nu: (be[i], 0, 0)),
                      pl.BlockSpec((1, de, d), lambda i, be, nu: (be[i], 0, 0))],
            out_specs=pl.BlockSpec((MOE_ROWS * ROW_TILES, LANE), lambda i, be, nu: (i, 0)),
            scratch_shapes=[pltpu.VMEM((d, de), BF16), pltpu.VMEM((d, de), BF16), pltpu.VMEM((de, d), BF16)]),
        out_shape=jax.ShapeDtypeStruct(x_pad.shape, F32),
        compiler_params=_cparams(("arbitrary",)),
        name="moe_experts",
    )(block_expert, n_used, x_pad, w1, w3, w2)


def _combine_kernel(dest0_ref, dest1_ref, yp_ref, rec_ref, x1_ref, mod_ref, g_ref, b_ref, o_ref,
                    buf_ref, sem, *, tm):
    step = pl.program_id(0)
    slot = step % 2

    def fetch(tile, to_slot):
        def issue(r, carry):
            for k, dest_ref in enumerate((dest0_ref, dest1_ref)):
                d = dest_ref[tile * tm + r]
                _row_copy(yp_ref.at[_tile_of(d), :], buf_ref.at[to_slot, k, _tile_of(r), :],
                          sem.at[to_slot]).start()
            return carry

        lax.fori_loop(0, tm, issue, 0, unroll=ROW_DMA_UNROLL)

    @pl.when(step == 0)
    def _():
        fetch(0, 0)

    @pl.when(step + 1 < pl.num_programs(0))
    def _():
        fetch(step + 1, 1 - slot)

    for k in range(2):
        _row_copy(yp_ref.at[pl.ds(0, tm * ROW_TILES), :], buf_ref.at[slot, k], sem.at[slot]).wait()

    rec = rec_ref[...]
    y = rec[:, 2:3] * _load_rows(buf_ref, tm, (slot, 0)) + rec[:, 3:4] * _load_rows(buf_ref, tm, (slot, 1))
    y = (1.0 + mod_ref[0, 5:6, :]) * y
    o_ref[...] = _layer_norm(ALPHA * x1_ref[...] + y) * g_ref[...] + b_ref[...]


def _combine(dest0, dest1, y_pad, rec, x1, mod, ln_g, ln_b, seq):
    nt, d = x1.shape
    tm = min(COMBINE_TOKENS, seq)
    return pl.pallas_call(
        functools.partial(_combine_kernel, tm=tm),
        grid_spec=pltpu.PrefetchScalarGridSpec(
            num_scalar_prefetch=2,
            grid=(nt // tm,),
            in_specs=[pl.BlockSpec(memory_space=pl.ANY),
                      pl.BlockSpec((tm, LANE), lambda i, *_: (i, 0)),
                      pl.BlockSpec((tm, d), lambda i, *_: (i, 0)),
                      pl.BlockSpec((1, 6, d), lambda i, *_: (i * tm // seq, 0, 0)),
                      pl.BlockSpec((1, d), lambda i, *_: (0, 0)),
                      pl.BlockSpec((1, d), lambda i, *_: (0, 0))],
            out_specs=pl.BlockSpec((tm, d), lambda i, *_: (i, 0)),
            scratch_shapes=[pltpu.VMEM((2, 2, tm * ROW_TILES, LANE), F32), pltpu.SemaphoreType.DMA((2,))]),
        out_shape=jax.ShapeDtypeStruct((nt, d), F32),
        compiler_params=_cparams(("arbitrary",)),
        name="moe_combine",
    )(dest0, dest1, y_pad, rec, x1, mod, ln_g, ln_b)


def _rel_bucket(dist):
    n = jnp.maximum(dist, 0)
    max_exact = REL_BUCKETS // 2
    nf = jnp.maximum(n, 1).astype(F32)
    large = max_exact + (jnp.log(nf / max_exact) / math.log(REL_MAX_DIST / max_exact)
                         * (REL_BUCKETS - max_exact)).astype(jnp.int32)
    large = jnp.minimum(large, REL_BUCKETS - 1)
    return jnp.where(n < max_exact, n, large)


def _bias_tables(rel_bias, seq):
    bucket_onehot = (_rel_bucket(jnp.arange(LANE))[:, None] == jnp.arange(REL_BUCKETS)).astype(F32)
    tab_d = jnp.einsum('db,hb->hd', bucket_onehot, rel_bias,
                       precision=lax.Precision.HIGHEST)
    tok = np.arange(QT)[None, :]
    key = np.arange(LANE)[:, None]
    far = tab_d[:, LANE - 1]
    cols = NSA_REP * QT

    def transposed(dist):
        idx = jnp.asarray(np.clip(dist, 0, LANE - 1).astype(np.int32))
        onehot = (idx[..., None] == jnp.arange(LANE, dtype=jnp.int32)).astype(F32)
        t = jnp.einsum('ijd,hd->hij', onehot, tab_d, precision=lax.Precision.HIGHEST)
        t = (t - far[:, None, None]) * LOG2E
        t = t.reshape(NSA_GROUPS, NSA_REP, LANE, QT).transpose(0, 2, 1, 3)
        return t.reshape(NSA_GROUPS, LANE, cols)

    t0t = transposed(tok - key)
    t1t = transposed(tok - key + QT)

    ns = seq // CMP_STRIDE
    d_c = tok - CMP_STRIDE * key + (CMP_STRIDE * CMP_PAD - (CMP_BLOCK - 1))
    seen = np.tile(d_c >= 0, (1, NSA_REP))
    recent = jnp.where(seen[None], transposed(d_c), NEG)
    tt = jnp.concatenate([jnp.zeros((NSA_GROUPS, ns, cols), F32), recent,
                          jnp.full((NSA_GROUPS, ns, cols), NEG, F32)], axis=1)

    rho = np.arange(WINDOW + QT)[:, None]
    tok_w = np.tile(np.arange(QT), NSA_REP)[None, :]
    band = (rho > tok_w) & (rho <= tok_w + WINDOW)
    rows = jnp.concatenate([jnp.zeros((NSA_GROUPS, WINDOW - QT, NSA_REP * QT), F32), t1t, t0t], axis=1)
    wb = jnp.where(band[None], rows, NEG)

    zeros = lambda n: jnp.zeros((NSA_GROUPS, n * QT, cols), F32)
    negs = lambda n: jnp.full((NSA_GROUPS, n * QT, cols), NEG, F32)
    diag = jnp.where(np.tile(tok >= key, (1, NSA_REP))[None], t0t, NEG)
    nd = jnp.stack([jnp.concatenate(([zeros(u - 1), t1t] if u else []) + [diag, negs(SEL_SUB - 1 - u)], axis=1)
                    for u in range(SEL_SUB)], axis=1)
    npv = jnp.concatenate([zeros(SEL_SUB - 1), t1t], axis=1)
    return nd, npv, wb, tt


def _overlap_matrix(seq):
    ns = seq // CMP_STRIDE
    nslc = seq // SLC_BLOCK
    ov = np.zeros((LANE, ns), np.float32)
    cs = np.arange(ns - 1) * CMP_STRIDE
    ss = np.arange(nslc) * SLC_BLOCK
    ov[:nslc, :ns - 1] = ((cs[None, :] < ss[:, None] + SLC_BLOCK) & (cs[None, :] + CMP_BLOCK > ss[:, None]))
    return jnp.asarray(ov, BF16)


def _reorder_cols(a):
    lead = a.shape[:-1]
    gate = a[..., MAIN_COLS:MAIN_COLS + GATE_COLS]
    per = GATE_COLS // NSA_GROUPS
    gate_blocks = []
    for g in range(NSA_GROUPS):
        gate_blocks.append(gate[..., g * per:(g + 1) * per])
        gate_blocks.append(jnp.zeros(lead + (LANE - per,), a.dtype))
    pad = jnp.zeros(lead + ((CB_MGH - CB_GATE - NSA_GROUPS) * LANE,), a.dtype)
    return jnp.concatenate([a[..., :MAIN_COLS]] + gate_blocks + [pad, a[..., MAIN_COLS + GATE_COLS:]], axis=-1)


def kernel(x, c, ada_w, ada_b, w_in, b_in, hg_lb_logits, hg_norm_w, cmp_pos_k, cmp_w1_k, cmp_b1_k, cmp_w2_k, cmp_pos_v, cmp_w1_v, cmp_b1_v, cmp_w2_v, rel_bias, w_br_hg, w_br_nsa, w_out, ln1_g, ln1_b, router_grp_w, router_grp_b, router_exp_w, router_exp_b, exp_w1, exp_w3, exp_w2, ln2_g, ln2_b):
    batch, seq, d = x.shape
    nt = batch * seq
    assert d == D_MODEL and seq % INPROJ_ROWS == 0 and seq // SLC_BLOCK <= LANE
    l = 0
    x2 = x.reshape(nt, d)

    c_pad = jnp.zeros((8, d), F32).at[:batch].set(c)
    mod = _adaln(c_pad, ada_w[l], ada_b[l][None])[:batch].reshape(batch, 6, d)

    proj = _inproj(x2, mod, _reorder_cols(w_in[l].astype(BF16)), _reorder_cols(b_in[l])[None], seq)

    lb_all = jnp.cumsum(jax.nn.softmax(hg_lb_logits.astype(F32), axis=0), axis=0)
    o_h = _hgrn(proj, lb_all[l][None], hg_norm_w[l][None], batch, seq)

    kvc = _compress(proj, jnp.stack([cmp_pos_k[l], cmp_pos_v[l]]),
                    jnp.stack([cmp_w1_k[l], cmp_w1_v[l]]).astype(BF16),
                    jnp.stack([cmp_b1_k[l], cmp_b1_v[l]])[:, None, :],
                    jnp.stack([cmp_w2_k[l], cmp_w2_v[l]]).astype(BF16), batch, seq)

    nd, npv, wb, tt = _bias_tables(rel_bias, seq)
    o_c, sel_t = _cmp_sel(proj, kvc, tt, _overlap_matrix(seq), batch, seq)
    o_w = _window(proj, wb, batch, seq)
    o_n = _selected(proj, sel_t, nd, npv, o_c, o_w, batch, seq)

    w_r = jnp.zeros((d, LANE), F32).at[:, :N_GROUPS].set(router_grp_w[l])
    w_r = w_r.at[:, N_GROUPS:N_GROUPS + N_EXPERTS].set(router_exp_w[l])
    w_r_hi = w_r.astype(BF16)
    w_r_lo = (w_r - w_r_hi.astype(F32)).astype(BF16)
    b_r = jnp.zeros((1, LANE), F32).at[0, :N_GROUPS].set(router_grp_b[l])
    b_r = b_r.at[0, N_GROUPS:N_GROUPS + N_EXPERTS].set(router_exp_b[l])
    x1, h2, logits = _merge(o_h, o_n, proj, x2, mod,
                            w_br_hg[l].astype(BF16), w_br_nsa[l].astype(BF16), w_out[l].astype(BF16),
                            ln1_g[l][None], ln1_b[l][None], jnp.stack([w_r_hi, w_r_lo]), b_r, seq)

    rec, cnt, rec_t = _route(logits)
    counts = cnt[0, N_GROUPS:N_GROUPS + N_EXPERTS].astype(jnp.int32)
    padded = (counts + MOE_ROWS - 1) // MOE_ROWS * MOE_ROWS
    pend = jnp.cumsum(padded)
    pstart = pend - padded
    n_assign = 2 * nt
    nb = n_assign // MOE_ROWS + N_EXPERTS
    slots = rec_t.astype(jnp.int32)
    expert_ids = jnp.arange(N_EXPERTS, dtype=jnp.int32)[:, None]
    slot_base = lambda e: jnp.sum(jnp.where(e[None, :] == expert_ids, pstart[:, None], 0), axis=0)
    dest0 = slot_base(slots[0]) + slots[4]
    dest1 = slot_base(slots[1]) + slots[5]
    block_start = jnp.arange(nb, dtype=jnp.int32) * MOE_ROWS
    block_expert = jnp.minimum(jnp.sum(pend[None, :] <= block_start[:, None], axis=1),
                               N_EXPERTS - 1).astype(jnp.int32)
    n_used = pend[-1] // MOE_ROWS
    spare = n_used + jnp.arange(N_EXPERTS, dtype=jnp.int32)
    zero_blocks = jnp.concatenate([jnp.where(padded > 0, pend // MOE_ROWS - 1, -1),
                                   jnp.where(spare < nb, spare, -1)]).astype(jnp.int32)

    x_pad = _dispatch(dest0, dest1, zero_blocks, h2, nb)
    y_pad = _experts(block_expert, n_used[None].astype(jnp.int32), x_pad, exp_w1[l], exp_w3[l], exp_w2[l])
    out = _combine(dest0, dest1, y_pad, rec, x1, mod, ln2_g[l][None], ln2_b[l][None], seq)
    return out.reshape(batch, seq, d)
```

```python
import functools
import math

import numpy as np
import jax
import jax.numpy as jnp
from jax import lax
from jax.experimental import pallas as pl
from jax.experimental.pallas import tpu as pltpu

F32 = jnp.float32
BF16 = jnp.bfloat16

D_MODEL = 1024
HG_HEADS = 8
HG_DK = 128
HG_DV = 128
HG_CHUNK = 32
HG_SUPER = 256
NSA_HEADS = 8
NSA_GROUPS = 2
NSA_REP = NSA_HEADS // NSA_GROUPS
NSA_DK = 128
CMP_BLOCK = 32
CMP_STRIDE = 16
SLC_BLOCK = 64
SLC_TOPK = 16
WINDOW = 512
FORCE_SCORE = 1e4
N_FORCED = 3
REL_BUCKETS = 32
REL_MAX_DIST = 128
N_GROUPS = 4
EXP_PER_GROUP = 8
N_EXPERTS = N_GROUPS * EXP_PER_GROUP
D_EXPERT = D_MODEL // 2
DEPTH = 1
ALPHA = (2 * DEPTH) ** 0.25

LANE = 128
QT = 128
NEG = -1e30
SCALE = NSA_DK ** -0.5
LOG2E = math.log2(math.e)
SCALE_LOG2 = SCALE * LOG2E
KEY_BLK = 512
BOUND_SLACK = 1.0 + 2.0 ** -10
MAX_REF_GAP = 64.0
SEL_ROWS = 16
SEL_SUB = KEY_BLK // QT
WIN_SUB = 4
CMP_SUB = 8
CMP_PAD = 120
VMEM_LIMIT = 56 * 1024 * 1024

CB_HQ, CB_HF, CB_HI, CB_HG = 0, 8, 16, 24
CB_NQ = 32
CB_KC, CB_VC, CB_KS, CB_VS, CB_KW, CB_VW = 40, 42, 44, 46, 48, 50
CB_GATE = 52
CB_MGH, CB_MGN = 56, 64
NCB = 72
MAIN_COLS = 52 * LANE
GATE_COLS = 3 * NSA_HEADS

INPROJ_ROWS = 2048
INPROJ_COL_BLOCKS = 8
HGRN_ROWS = 512
MERGE_ROWS = 512
ROUTE_ROWS = 512
DISPATCH_TOKENS = 2048
COMBINE_TOKENS = 256
MERGE_SPLIT = 2
ROW_TILES = D_MODEL // LANE
MOE_ROWS = 512
ROW_DMA_UNROLL = 8


def _cparams(sem):
    return pltpu.CompilerParams(dimension_semantics=sem, vmem_limit_bytes=VMEM_LIMIT)


def _dot(a, b):
    return jnp.dot(a, b, preferred_element_type=F32)


def _dot_nt(a, b):
    return lax.dot_general(a, b, (((1,), (1,)), ((), ())), preferred_element_type=F32)


def _dot_tn(a, b):
    return lax.dot_general(a, b, (((0,), (0,)), ((), ())), preferred_element_type=F32)


def _split3(x):
    hi = x.astype(BF16)
    r = x - hi.astype(F32)
    mid = r.astype(BF16)
    lo = (r - mid.astype(F32)).astype(BF16)
    return hi, mid, lo


def _dot01(m01, x):
    hi, mid, lo = _split3(x)
    return _dot(m01, hi) + _dot(m01, mid) + _dot(m01, lo)


def _layer_norm(x, eps=1e-5):
    mu = jnp.mean(x, axis=-1, keepdims=True)
    xc = x - mu
    var = jnp.mean(xc * xc, axis=-1, keepdims=True)
    return xc * lax.rsqrt(var + eps)


def _adaln_kernel(c_ref, w_ref, b_ref, o_ref):
    c = c_ref[...]
    ca = c * jax.nn.sigmoid(c)
    o_ref[...] = jnp.dot(ca, w_ref[...], precision=lax.Precision.HIGHEST,
                         preferred_element_type=F32) + b_ref[...]


def _adaln(c_pad, w, b):
    rows, d = c_pad.shape
    n = w.shape[1]
    return pl.pallas_call(
        _adaln_kernel,
        grid=(n // d,),
        in_specs=[pl.BlockSpec((rows, d), lambda j: (0, 0)),
                  pl.BlockSpec((d, d), lambda j: (0, j)),
                  pl.BlockSpec((1, d), lambda j: (0, j))],
        out_specs=pl.BlockSpec((rows, d), lambda j: (0, j)),
        out_shape=jax.ShapeDtypeStruct((rows, n), F32),
        compiler_params=_cparams(("arbitrary",)),
        name="adaln",
    )(c_pad, w, b)


def _inproj_kernel(x_ref, mod_ref, w_ref, b_ref, o_ref, hn_ref, *, ncb_tile):
    @pl.when(pl.program_id(1) == 0)
    def _():
        hn = _layer_norm(x_ref[...])
        sh = mod_ref[0, 0:1, :]
        sc = mod_ref[0, 1:2, :]
        hn_ref[...] = (hn * (1.0 + sc) + sh).astype(BF16)

    res = _dot(hn_ref[...], w_ref[...]) + b_ref[...]
    for c in range(ncb_tile):
        o_ref[c] = res[:, c * LANE:(c + 1) * LANE]


def _inproj(x2, mod, w, b, seq):
    nt, d = x2.shape
    tm = min(INPROJ_ROWS, seq)
    ncb_tile = INPROJ_COL_BLOCKS
    tn = ncb_tile * LANE
    return pl.pallas_call(
        functools.partial(_inproj_kernel, ncb_tile=ncb_tile),
        grid=(nt // tm, NCB // ncb_tile),
        in_specs=[pl.BlockSpec((tm, d), lambda i, j: (i, 0)),
                  pl.BlockSpec((1, 6, d), lambda i, j: (i * tm // seq, 0, 0)),
                  pl.BlockSpec((d, tn), lambda i, j: (0, j)),
                  pl.BlockSpec((1, tn), lambda i, j: (0, j))],
        out_specs=pl.BlockSpec((ncb_tile, tm, LANE), lambda i, j: (j, i, 0)),
        out_shape=jax.ShapeDtypeStruct((NCB, nt, LANE), F32),
        scratch_shapes=[pltpu.VMEM((tm, d), BF16)],
        compiler_params=_cparams(("arbitrary", "arbitrary")),
        name="inproj",
    )(x2, mod, w, b)


def _hgrn_kernel(q_ref, f_ref, v_ref, g_ref, lb_ref, nw_ref, o_ref, st_ref, *, rows):
    @pl.when(pl.program_id(1) == 0)
    def _():
        st_ref[...] = jnp.zeros_like(st_ref)

    sup = HG_SUPER
    ri = lax.broadcasted_iota(jnp.int32, (sup, sup), 0)
    ci = lax.broadcasted_iota(jnp.int32, (sup, sup), 1)
    same = (ri // HG_CHUNK) == (ci // HG_CHUNK)
    cum_m = jnp.where(same & (ci <= ri), 1.0, 0.0).astype(BF16)
    rt = lax.broadcasted_iota(jnp.int32, (LANE, LANE), 0)
    ct = lax.broadcasted_iota(jnp.int32, (LANE, LANE), 1)
    tril = ((rt // HG_CHUNK) == (ct // HG_CHUNK)) & (ct <= rt)
    per = sup // HG_CHUNK
    groups = [slice(g * LANE, (g + 1) * LANE) for g in range(sup // LANE)]

    heads = range(HG_HEADS)
    hs = [slice(h * LANE, (h + 1) * LANE) for h in heads]

    def wide(ref, r0):
        return jnp.concatenate([ref[h, pl.ds(r0, sup), :] for h in heads], axis=1)

    def body(i, carry):
        r0 = pl.multiple_of(i * sup, sup)
        lb = lb_ref[...]
        f = lb + (1.0 - lb) * jax.nn.sigmoid(wide(f_ref, r0))
        lf = jnp.log(f)
        k = 1.0 - f
        hi = lf.astype(BF16)
        lo = (lf - hi.astype(F32)).astype(BF16)
        b = _dot(cum_m, hi) + _dot(cum_m, lo)
        chunk = [slice(c * HG_CHUNK, (c + 1) * HG_CHUNK) for c in range(per)]
        dec = [jnp.exp(b[(c + 1) * HG_CHUNK - 1:(c + 1) * HG_CHUNK, :]) for c in range(per)]
        dec_rows = jnp.concatenate([jnp.broadcast_to(dec[c], (HG_CHUNK, dec[c].shape[1])) for c in range(per)],
                                   axis=0)
        q_in = (wide(q_ref, r0) * jnp.exp(b)).astype(BF16)
        k_dec = k * jnp.exp(-b)
        k_in = k_dec.astype(BF16)
        k_end = (k_dec * dec_rows).astype(BF16)
        vb = wide(v_ref, r0).astype(BF16)

        att = [[jnp.where(tril, _dot_nt(q_in[g, hs[h]], k_in[g, hs[h]]), 0.0).astype(BF16) for g in groups]
               for h in heads]
        upd = [[_dot_tn(vb[chunk[c], hs[h]], k_end[chunk[c], hs[h]]) for c in range(per)] for h in heads]
        intra = [[_dot(att[h][n], vb[g, hs[h]]) for n, g in enumerate(groups)] for h in heads]

        st = [st_ref[h] for h in heads]
        inter = [[] for _ in heads]
        for c in range(per):
            for h in heads:
                inter[h].append(_dot_nt(q_in[chunk[c], hs[h]], st[h].astype(BF16)))
                st[h] = dec[c][:, hs[h]] * st[h] + upd[h][c]
        for h in heads:
            st_ref[h] = st[h]

        o = [jnp.concatenate(intra[h], axis=0) + jnp.concatenate(inter[h], axis=0) for h in heads]
        scale = [lax.rsqrt(jnp.mean(o[h] * o[h], axis=-1, keepdims=True) + 1e-6) for h in heads]
        on = jnp.concatenate([o[h] * scale[h] for h in heads], axis=1)
        on = on * nw_ref[...] * jax.nn.sigmoid(wide(g_ref, r0))
        o_ref[pl.ds(r0, sup), :] = on.astype(BF16)
        return carry

    lax.fori_loop(0, rows // sup, body, 0)


def _hgrn(proj, lb, nw, batch, seq):
    nt = proj.shape[1]
    tb = min(HGRN_ROWS, seq)
    nblk = seq // tb

    def slab(cb0):
        return pl.BlockSpec((HG_HEADS, tb, LANE), lambda b, t: (cb0 // HG_HEADS, b * nblk + t, 0))

    vec = pl.BlockSpec((1, HG_HEADS * LANE), lambda b, t: (0, 0))
    return pl.pallas_call(
        functools.partial(_hgrn_kernel, rows=tb),
        grid=(batch, nblk),
        in_specs=[slab(CB_HQ), slab(CB_HF), slab(CB_HI), slab(CB_HG), vec, vec],
        out_specs=pl.BlockSpec((tb, HG_HEADS * HG_DV), lambda b, t: (b * nblk + t, 0)),
        out_shape=jax.ShapeDtypeStruct((nt, HG_HEADS * HG_DV), BF16),
        scratch_shapes=[pltpu.VMEM((HG_HEADS, HG_DV, HG_DK), F32)],
        compiler_params=_cparams(("arbitrary", "arbitrary")),
        name="hgrn2",
    )(proj, proj, proj, proj, lb, nw)


def _compress_kernel(x_ref, pos_ref, w1_ref, b1_ref, w2_ref, o_ref, *, ns):
    p0 = jnp.zeros((ns, LANE), F32)
    p1 = jnp.zeros((ns, LANE), F32)
    for j in range(CMP_STRIDE):
        tok = x_ref[0, pl.ds(j, ns, stride=CMP_STRIDE), :]
        rows = slice(j * LANE, (j + 1) * LANE)
        late = slice((CMP_STRIDE + j) * LANE, (CMP_STRIDE + j + 1) * LANE)
        p0 = p0 + _dot((tok + pos_ref[0, j:j + 1, :]).astype(BF16), w1_ref[0, rows, :])
        p1 = p1 + _dot((tok + pos_ref[0, CMP_STRIDE + j:CMP_STRIDE + j + 1, :]).astype(BF16), w1_ref[0, late, :])
    h = p0 + pltpu.roll(p1, ns - 1, axis=0) + b1_ref[0]
    a = h * jax.nn.sigmoid(h)
    out = _dot(a.astype(BF16), w2_ref[0])
    row = lax.broadcasted_iota(jnp.int32, out.shape, 0)
    out = jnp.where(row < ns - 1, out, 0.0)
    o_ref[0, 0, 0] = out


def _compress(proj, pos, w1, b1, w2, batch, seq):
    ns = seq // CMP_STRIDE
    np_rows = ns
    width = CMP_STRIDE * LANE
    return pl.pallas_call(
        functools.partial(_compress_kernel, ns=ns),
        grid=(2, batch, NSA_GROUPS),
        in_specs=[pl.BlockSpec((1, seq, LANE), lambda s, b, g: (CB_KC + NSA_GROUPS * s + g, b, 0)),
                  pl.BlockSpec((1, CMP_BLOCK, LANE), lambda s, b, g: (s, 0, 0)),
                  pl.BlockSpec((1, 2 * width, LANE), lambda s, b, g: (s, 0, 0)),
                  pl.BlockSpec((1, 1, LANE), lambda s, b, g: (s, 0, 0)),
                  pl.BlockSpec((1, LANE, LANE), lambda s, b, g: (s, 0, 0))],
        out_specs=pl.BlockSpec((1, 1, 1, np_rows, LANE), lambda s, b, g: (s, b, g, 0, 0)),
        out_shape=jax.ShapeDtypeStruct((2, batch, NSA_GROUPS, np_rows, LANE), F32),
        compiler_params=_cparams(("arbitrary", "arbitrary", "arbitrary")),
        name="nsa_compress",
    )(proj, pos, w1, b1, w2)


def _cmp_sel_kernel(q0, q1, q2, q3, kc_ref, vc_ref, gate_ref, tt_ref, ov_ref, oc_ref, sel_ref,
                    kb_ref, vt_ref, imp_ref, *, np_rows):
    step = pl.program_id(2)
    cols = NSA_REP * QT
    ns = np_rows

    @pl.when(step == 0)
    def _():
        kb_ref[...] = (kc_ref[0, 0, 0] * SCALE_LOG2).astype(BF16)
        for c in range(np_rows // LANE):
            vt_ref[:, c * LANE:(c + 1) * LANE] = vc_ref[0, 0, 0, c * LANE:(c + 1) * LANE, :].T.astype(BF16)

    subs = range(CMP_SUB)
    tis = [step * CMP_SUB + u for u in subs]
    qts = [_q_transposed((q0, q1, q2, q3), u) for u in subs]

    def group_sum(p):
        tot = p[:, 0:QT]
        for r in range(1, NSA_REP):
            tot = tot + p[:, r * QT:(r + 1) * QT]
        return tot

    def attend(nrows):
        kb = kb_ref[0:nrows, :]
        bias = [tt_ref[0, pl.ds(pl.multiple_of(ns + CMP_PAD - tis[u] * (QT // CMP_STRIDE), 8), nrows), :]
                for u in subs]
        ss = [_dot(kb, qts[u]) + bias[u] for u in subs]
        ms = [jnp.max(ss[u], axis=0, keepdims=True) for u in subs]
        ps = [jnp.exp2(ss[u] - ms[u]) for u in subs]
        ls = [jnp.sum(ps[u], axis=0, keepdims=True) for u in subs]
        invs = [jnp.where(ms[u] > 0.5 * NEG, 1.0 / ls[u], 0.0) for u in subs]
        pn = [ps[u] * invs[u] for u in subs]
        vt = vt_ref[:, 0:nrows]
        os_ = [_dot(vt, pn[u].astype(BF16)) for u in subs]
        for u in subs:
            rows = slice(u * QT, (u + 1) * QT)
            gt = jax.nn.sigmoid(gate_ref[0, rows, :])
            for r in range(NSA_REP):
                oc_ref[rows, r * LANE:(r + 1) * LANE] = gt[:, 3 * r:3 * r + 1] * os_[u][:, r * QT:(r + 1) * QT].T
            nblk = nrows * CMP_STRIDE // SLC_BLOCK
            imp_ref[0:nblk, rows] = _dot01(ov_ref[0:nblk, 0:nrows], group_sum(pn[u]))

    width = CMP_SUB * QT

    def choose(nblk):
        assert FORCE_SCORE > NSA_REP and nblk >= SLC_TOPK
        jj = lax.broadcasted_iota(jnp.int32, (nblk, width), 0)
        tok = step * width + lax.broadcasted_iota(jnp.int32, (nblk, width), 1)
        cur = tok // SLC_BLOCK
        forced = (jj == 0) | (jj == cur) | (jj == cur - 1)
        score = jnp.where(forced, -jnp.inf, jnp.where(jj <= cur, imp_ref[0:nblk, :], -1.0))
        jf = jj.astype(F32)
        for _ in range(SLC_TOPK - N_FORCED):
            mval = jnp.max(score, axis=0, keepdims=True)
            first = jnp.min(jnp.where(score == mval, jf, float(LANE)), axis=0, keepdims=True)
            score = jnp.where(jf == first, -jnp.inf, score)
        sel_ref[0, 0, 0:nblk, :] = jnp.where(score == -jnp.inf, 0.0, NEG)
        if nblk < LANE:
            sel_ref[0, 0, nblk:LANE, :] = jnp.full((LANE - nblk, width), NEG, F32)

    visible = (step + 1) * CMP_SUB * (QT // CMP_STRIDE)
    ngroups = np_rows // LANE
    for g in range(1, ngroups + 1):
        upper = visible <= g * LANE if g < ngroups else True

        @pl.when((visible > (g - 1) * LANE) & upper)
        def _():
            attend(g * LANE)
            choose(g * LANE * CMP_STRIDE // SLC_BLOCK)


def _q_specs(nsteps, rows=QT):
    return [pl.BlockSpec((1, rows, LANE),
                         functools.partial(lambda b, g, t, r: (CB_NQ + NSA_REP * g + r, b * nsteps + t, 0), r=r))
            for r in range(NSA_REP)]


def _cmp_sel(proj, kvc, tt, ov_t, batch, seq):
    nt = proj.shape[1]
    rows = CMP_SUB * QT
    nsteps = seq // rows
    cols = NSA_REP * QT
    np_rows = kvc.shape[3]
    kv_spec = lambda s: pl.BlockSpec((1, 1, 1, np_rows, LANE), lambda b, g, t: (s, b, g, 0, 0))
    return pl.pallas_call(
        functools.partial(_cmp_sel_kernel, np_rows=np_rows),
        grid=(batch, NSA_GROUPS, nsteps),
        in_specs=_q_specs(nsteps, rows) + [
            kv_spec(0), kv_spec(1),
            pl.BlockSpec((1, rows, LANE), lambda b, g, t: (CB_GATE + g, b * nsteps + t, 0)),
            pl.BlockSpec((1, tt.shape[1], cols), lambda b, g, t: (g, 0, 0)),
            pl.BlockSpec((LANE, np_rows), lambda b, g, t: (0, 0))],
        out_specs=[pl.BlockSpec((rows, cols), lambda b, g, t: (b * nsteps + t, g)),
                   pl.BlockSpec((1, 1, LANE, rows), lambda b, g, t: (b, g, 0, t))],
        out_shape=[jax.ShapeDtypeStruct((nt, NSA_HEADS * LANE), F32),
                   jax.ShapeDtypeStruct((batch, NSA_GROUPS, LANE, seq), F32)],
        scratch_shapes=[pltpu.VMEM((np_rows, LANE), BF16), pltpu.VMEM((LANE, np_rows), BF16),
                        pltpu.VMEM((LANE, rows), F32)],
        compiler_params=_cparams(("arbitrary", "arbitrary", "arbitrary")),
        name="nsa_cmp_select",
    )(proj, proj, proj, proj, kvc, kvc, proj, tt, ov_t)


def _q_transposed(q_refs, sub=0):
    return jnp.concatenate([r[0, sub * QT:(sub + 1) * QT, :].T for r in q_refs], axis=1).astype(BF16)


def _finish_t(state, gate_ref, branch, o_ref, sub=0, others=()):
    _, l, acc = state
    o = acc / jnp.where(l == 0.0, 1.0, l)
    rows = slice(sub * QT, (sub + 1) * QT)
    gt = jax.nn.sigmoid(gate_ref[0, rows, :])
    for r in range(NSA_REP):
        col = 3 * r + branch
        lanes = slice(r * LANE, (r + 1) * LANE)
        val = gt[:, col:col + 1] * o[:, r * QT:(r + 1) * QT].T
        for other in others:
            val = val + other[rows, lanes]
        o_ref[rows, lanes] = val.astype(o_ref.dtype)


def _init_state(cols):
    return (jnp.full((1, cols), NEG, F32), jnp.zeros((1, cols), F32), jnp.zeros((LANE, cols), F32))


def _win_kernel(q0, q1, q2, q3, k_ref, v_ref, gate_ref, wb_ref, o_ref, kb_ref, vt_ref, *, seq):
    step = pl.program_id(2)
    cols = NSA_REP * QT
    wt = WINDOW // QT

    lane = lax.broadcasted_iota(jnp.int32, (QT, LANE), 1)
    pad_mark = jnp.where(lane == 0, 1.0, 0.0).astype(BF16)

    @pl.when(step == 0)
    def _():
        for i in range(wt):
            kb_ref[i * QT:(i + 1) * QT, 0:LANE] = jnp.zeros((QT, LANE), BF16)
            kb_ref[i * QT:(i + 1) * QT, LANE:2 * LANE] = pad_mark
            vt_ref[i] = jnp.zeros((LANE, QT), BF16)

        def fill(i, carry):
            r0 = pl.multiple_of(i * QT, QT)
            kb_ref[pl.ds(WINDOW + r0, QT), 0:LANE] = (k_ref[0, pl.ds(r0, QT), :] * SCALE_LOG2).astype(BF16)
            kb_ref[pl.ds(WINDOW + r0, QT), LANE:2 * LANE] = jnp.zeros((QT, LANE), BF16)
            vt_ref[wt + i] = v_ref[0, pl.ds(r0, QT), :].T.astype(BF16)
            return carry

        lax.fori_loop(0, seq // QT, fill, 0)

    subs = range(WIN_SUB)
    tis = [step * WIN_SUB + sub for sub in subs]
    row = lax.broadcasted_iota(jnp.int32, (LANE, cols), 0)
    pad_rows = jnp.where(row == 0, NEG, 0.0).astype(BF16)
    qts = [jnp.concatenate([_q_transposed((q0, q1, q2, q3), sub), pad_rows], axis=0) for sub in subs]
    ss = [_dot(kb_ref[pl.ds(pl.multiple_of(tis[u] * QT, QT), WINDOW + QT), :], qts[u]) + wb_ref[0]
          for u in subs]
    ms = [jnp.max(ss[u], axis=0, keepdims=True) for u in subs]
    ps = [jnp.exp2(ss[u] - ms[u]) for u in subs]
    ls = [jnp.sum(ps[u], axis=0, keepdims=True) for u in subs]
    pbs = [ps[u].astype(BF16) for u in subs]
    accs = None
    for d in range(0, wt + 1, 2):
        n = min(2, wt + 1 - d)
        vts = [jnp.concatenate([vt_ref[tis[u] + d + e] for e in range(n)], axis=1) for u in subs]
        part = [_dot(vts[u], pbs[u][d * QT:(d + n) * QT]) for u in subs]
        accs = part if accs is None else [accs[u] + part[u] for u in subs]
    for u in subs:
        _finish_t((ms[u], ls[u], accs[u]), gate_ref, 2, o_ref, u)


def _window(proj, wb, batch, seq):
    nt = proj.shape[1]
    rows = WIN_SUB * QT
    nqt = seq // rows
    cols = NSA_REP * QT
    slab = lambda cb0: pl.BlockSpec((1, seq, LANE), lambda b, g, t: (cb0 + g, b, 0))
    return pl.pallas_call(
        functools.partial(_win_kernel, seq=seq),
        grid=(batch, NSA_GROUPS, nqt),
        in_specs=_q_specs(nqt, rows) + [
            slab(CB_KW), slab(CB_VW),
            pl.BlockSpec((1, rows, LANE), lambda b, g, t: (CB_GATE + g, b * nqt + t, 0)),
            pl.BlockSpec((1, WINDOW + QT, cols), lambda b, g, t: (g, 0, 0))],
        out_specs=pl.BlockSpec((rows, cols), lambda b, g, t: (b * nqt + t, g)),
        out_shape=jax.ShapeDtypeStruct((nt, NSA_HEADS * LANE), F32),
        scratch_shapes=[pltpu.VMEM((seq + WINDOW, 2 * LANE), BF16),
                        pltpu.VMEM((seq // QT + WINDOW // QT, LANE, QT), BF16)],
        compiler_params=_cparams(("arbitrary", "arbitrary", "arbitrary")),
        name="nsa_window",
    )(proj, proj, proj, proj, proj, proj, proj, wb)


def _softmax_steps(ss, vt, states):
    n = range(len(ss))
    m_new = [jnp.maximum(states[u][0], jnp.max(ss[u], axis=0, keepdims=True)) for u in n]
    alpha = [jnp.exp2(states[u][0] - m_new[u]) for u in n]
    p = [jnp.exp2(ss[u] - m_new[u]) for u in n]
    l = [alpha[u] * states[u][1] + jnp.sum(p[u], axis=0, keepdims=True) for u in n]
    vts = vt if isinstance(vt, (list, tuple)) else [vt] * len(ss)
    pv = [_dot(vts[u], p[u].astype(BF16)) for u in n]
    return [(m_new[u], l[u], alpha[u] * states[u][2] + pv[u]) for u in n]


def _sel_kernel(q0, q1, q2, q3, k_ref, v_ref, sel_ref, gate_ref, nd_ref, npv_ref, oc_ref, ow_ref, o_ref,
                kb_ref, vt_ref, qa_ref, kn_ref, m_ref, l_ref, acc_ref, *, seq):
    step = pl.program_id(2)
    cols = NSA_REP * QT
    spb = KEY_BLK // SLC_BLOCK
    subs = range(SEL_SUB)

    @pl.when(step == 0)
    def _():
        blk = lax.broadcasted_iota(jnp.int32, (KEY_BLK, LANE), 0) // SLC_BLOCK
        onehot = jnp.where(blk == lax.broadcasted_iota(jnp.int32, (KEY_BLK, LANE), 1), 1.0, 0.0).astype(BF16)

        def fill(i, kmax2):
            r0 = pl.multiple_of(i * KEY_BLK, KEY_BLK)
            kblk = (k_ref[0, pl.ds(r0, KEY_BLK), :] * SCALE_LOG2).astype(BF16)
            kb_ref[pl.ds(r0, KEY_BLK), 0:LANE] = kblk
            kb_ref[pl.ds(r0, KEY_BLK), LANE:2 * LANE] = onehot
            for c in range(SEL_SUB):
                vt_ref[i, :, c * QT:(c + 1) * QT] = v_ref[0, pl.ds(r0 + c * QT, QT), :].T.astype(BF16)
            n2 = jnp.sum(jnp.square(kblk.astype(F32)), axis=1, keepdims=True)
            return jnp.maximum(kmax2, jnp.max(n2, axis=0, keepdims=True))

        kmax2 = lax.fori_loop(0, seq // KEY_BLK, fill, jnp.zeros((1, 1), F32))
        kn_ref[...] = jnp.sqrt(kmax2)
        qa_ref[...] = jnp.zeros_like(qa_ref)

    qts = [_q_transposed((q0, q1, q2, q3), u) for u in subs]
    for u in subs:
        qa_ref[u, 0:LANE, :] = qts[u]
    bounds = [jnp.sqrt(jnp.sum(jnp.square(qts[u].astype(F32)), axis=0, keepdims=True)) * kn_ref[...]
              * BOUND_SLACK + 1.0 for u in subs]

    def block(kb):
        r0 = pl.multiple_of(kb * KEY_BLK, KEY_BLK)
        return kb_ref[pl.ds(r0, KEY_BLK), :], vt_ref[kb]

    pad_rows = jnp.zeros((SEL_ROWS - spb, cols), F32)

    def scores(kb, k, rows_of=lambda u: KEY_BLK):
        for u in subs:
            rows = sel_ref[0, 0, pl.ds(pl.multiple_of(kb * spb, spb), spb), u * QT:(u + 1) * QT]
            rows = jnp.concatenate([rows] * NSA_REP, axis=1)
            qa_ref[u, LANE:LANE + SEL_ROWS, :] = jnp.concatenate([rows, pad_rows], axis=0).astype(BF16)
        return [_dot(k[0:rows_of(u)], qa_ref[u]) for u in subs]

    seen = lambda u: (u + 1) * QT
    k, vt = block(step)
    ss = scores(step, k, seen)
    ss = [ss[u] + nd_ref[0, u, 0:seen(u), :] for u in subs]
    states = _softmax_steps(ss, [vt[:, 0:seen(u)] for u in subs], [_init_state(cols) for _ in subs])

    for u in subs:
        m_ref[u], l_ref[u], acc_ref[u] = states[u]

    def prev_scores():
        k, vt = block(step - 1)
        ss = scores(step - 1, k)
        ss[0] = ss[0] + npv_ref[0]
        return ss, vt

    n_old = jnp.maximum(step - 1, 0)
    bounds[0] = bounds[0] + jnp.max(npv_ref[0], axis=0, keepdims=True)
    m_fix = [jnp.maximum(states[u][0], bounds[u]) for u in subs]
    gap = jnp.max(jnp.concatenate([m_fix[u] - states[u][0] for u in subs], axis=1))
    fixed_ok = gap < MAX_REF_GAP

    @pl.when(fixed_ok)
    def _():
        for u in subs:
            alpha = jnp.exp2(m_ref[u] - m_fix[u])
            l_ref[u] = alpha * l_ref[u]
            acc_ref[u] = alpha * acc_ref[u]
            m_ref[u] = m_fix[u]

        def fixed_step(ss, vt):
            ps = [jnp.exp2(ss[u] - m_ref[u]) for u in subs]
            for u in subs:
                l_ref[u] = l_ref[u] + jnp.sum(ps[u], axis=0, keepdims=True)
            pv = [_dot(vt, ps[u].astype(BF16)) for u in subs]
            for u in subs:
                acc_ref[u] = acc_ref[u] + pv[u]

        @pl.when(step >= 1)
        def _():
            fixed_step(*prev_scores())

        def body(kb, carry):
            k, vt = block(kb)
            fixed_step(scores(kb, k), vt)
            return carry

        lax.fori_loop(0, n_old, body, 0)

    @pl.when(jnp.logical_not(fixed_ok))
    def _():
        def online_step(ss, vt):
            st = _softmax_steps(ss, vt, [(m_ref[u], l_ref[u], acc_ref[u]) for u in subs])
            for u in subs:
                m_ref[u], l_ref[u], acc_ref[u] = st[u]

        @pl.when(step >= 1)
        def _():
            online_step(*prev_scores())

        def body(kb, carry):
            k, vt = block(kb)
            online_step(scores(kb, k), vt)
            return carry

        lax.fori_loop(0, n_old, body, 0)

    for u in subs:
        _finish_t((m_ref[u], l_ref[u], acc_ref[u]), gate_ref, 1, o_ref, u, others=(oc_ref, ow_ref))


def _selected(proj, sel_t, nd, npv, o_c, o_w, batch, seq):
    nt = proj.shape[1]
    rows = SEL_SUB * QT
    nsteps = seq // rows
    cols = NSA_REP * QT
    slab = lambda cb0: pl.BlockSpec((1, seq, LANE), lambda b, g, t: (cb0 + g, b, 0))
    return pl.pallas_call(
        functools.partial(_sel_kernel, seq=seq),
        grid=(batch, NSA_GROUPS, nsteps),
        in_specs=_q_specs(nsteps, rows) + [
            slab(CB_KS), slab(CB_VS),
            pl.BlockSpec((1, 1, LANE, rows), lambda b, g, t: (b, g, 0, t)),
            pl.BlockSpec((1, rows, LANE), lambda b, g, t: (CB_GATE + g, b * nsteps + t, 0)),
            pl.BlockSpec((1, SEL_SUB, KEY_BLK, cols), lambda b, g, t: (g, 0, 0, 0)),
            pl.BlockSpec((1, KEY_BLK, cols), lambda b, g, t: (g, 0, 0)),
            pl.BlockSpec((rows, cols), lambda b, g, t: (b * nsteps + t, g)),
            pl.BlockSpec((rows, cols), lambda b, g, t: (b * nsteps + t, g))],
        out_specs=pl.BlockSpec((rows, cols), lambda b, g, t: (b * nsteps + t, g)),
        out_shape=jax.ShapeDtypeStruct((nt, NSA_HEADS * LANE), BF16),
        scratch_shapes=[pltpu.VMEM((seq, 2 * LANE), BF16),
                        pltpu.VMEM((seq // KEY_BLK, LANE, KEY_BLK), BF16),
                        pltpu.VMEM((SEL_SUB, 2 * LANE, cols), BF16),
                        pltpu.VMEM((1, 1), F32),
                        pltpu.VMEM((SEL_SUB, 1, cols), F32), pltpu.VMEM((SEL_SUB, 1, cols), F32),
                        pltpu.VMEM((SEL_SUB, LANE, cols), F32)],
        compiler_params=_cparams(("arbitrary", "arbitrary", "arbitrary")),
        name="nsa_selected",
    )(proj, proj, proj, proj, proj, proj, sel_t, proj, nd, npv, o_c, o_w)


def _merge_kernel(oh_ref, on_ref, mgh_ref, mgn_ref, x_ref, mod_ref,
                  wh_ref, wn_ref, wo_ref, g_ref, b_ref, wr_ref, br_ref,
                  x1_ref, h2_ref, lg_ref):
    nblk = D_MODEL // LANE
    tm = x_ref.shape[0]
    halves = [slice(s * tm // MERGE_SPLIT, (s + 1) * tm // MERGE_SPLIT) for s in range(MERGE_SPLIT)]
    a_h = [_dot(oh_ref[rs, :], wh_ref[...]) for rs in halves]
    a_n = [_dot(on_ref[rs, :], wn_ref[...]) for rs in halves]
    gh = [jnp.concatenate([mgh_ref[c, rs, :] for c in range(nblk)], axis=-1) for rs in halves]
    gn = [jnp.concatenate([mgn_ref[c, rs, :] for c in range(nblk)], axis=-1) for rs in halves]
    merged = [(jax.nn.sigmoid(gh[s]) * a_h[s] + jax.nn.sigmoid(gn[s]) * a_n[s]).astype(BF16)
              for s in range(MERGE_SPLIT)]
    y = [(1.0 + mod_ref[0, 2:3, :]) * _dot(merged[s], wo_ref[...]) for s in range(MERGE_SPLIT)]
    x1 = [_layer_norm(ALPHA * x_ref[rs, :] + y[s]) * g_ref[...] + b_ref[...] for s, rs in enumerate(halves)]
    h2 = [_layer_norm(x1[s]) * (1.0 + mod_ref[0, 4:5, :]) + mod_ref[0, 3:4, :] for s in range(MERGE_SPLIT)]
    h_hi = [h2[s].astype(BF16) for s in range(MERGE_SPLIT)]
    h_lo = [(h2[s] - h_hi[s].astype(F32)).astype(BF16) for s in range(MERGE_SPLIT)]
    lg = [_dot(h_hi[s], wr_ref[0]) + _dot(h_hi[s], wr_ref[1]) + _dot(h_lo[s], wr_ref[0]) + br_ref[...]
          for s in range(MERGE_SPLIT)]
    for s, rs in enumerate(halves):
        x1_ref[rs, :] = x1[s]
        _store_rows(h2_ref, h2[s], rs.start)
        lg_ref[rs, :] = lg[s]


def _merge(o_h, o_n, proj, x2, mod, w_h, w_n, w_o, ln_g, ln_b, w_r, b_r, seq):
    nt, d = x2.shape
    tm = min(MERGE_ROWS, seq)
    nblk = d // LANE
    row = lambda w: pl.BlockSpec((tm, w), lambda i: (i, 0))
    full = lambda a: pl.BlockSpec(a.shape, lambda i: (0,) * a.ndim)
    return pl.pallas_call(
        _merge_kernel,
        grid=(nt // tm,),
        in_specs=[row(d), row(d),
                  pl.BlockSpec((nblk, tm, LANE), lambda i: (CB_MGH // nblk, i, 0)),
                  pl.BlockSpec((nblk, tm, LANE), lambda i: (CB_MGN // nblk, i, 0)),
                  row(d),
                  pl.BlockSpec((1, 6, d), lambda i: (i * tm // seq, 0, 0)),
                  full(w_h), full(w_n), full(w_o), full(ln_g), full(ln_b), full(w_r), full(b_r)],
        out_specs=[row(d), pl.BlockSpec((tm * ROW_TILES, LANE), lambda i: (i, 0)), row(LANE)],
        out_shape=[jax.ShapeDtypeStruct((nt, d), F32),
                   jax.ShapeDtypeStruct((nt * ROW_TILES, LANE), F32),
                   jax.ShapeDtypeStruct((nt, LANE), F32)],
        compiler_params=_cparams(("arbitrary",)),
        name="merge_outproj",
    )(o_h, o_n, proj, proj, x2, mod, w_h, w_n, w_o, ln_g, ln_b, w_r, b_r)


def _route_kernel(lg_ref, rec_ref, cnt_ref, rect_ref, carry_ref, *, tm):
    @pl.when(pl.program_id(0) == 0)
    def _():
        carry_ref[...] = jnp.zeros_like(carry_ref)

    lg = lg_ref[...]
    lane = lax.broadcasted_iota(jnp.int32, (tm, LANE), 1).astype(F32)
    far = float(LANE)
    gmask = lane < N_GROUPS
    gl = jnp.where(gmask, lg, -jnp.inf)
    gmax = jnp.max(gl, axis=-1, keepdims=True)
    gsum = jnp.sum(jnp.where(gmask, jnp.exp(gl - gmax), 0.0), axis=-1, keepdims=True)
    grp_p = 1.0 / gsum
    gidx = jnp.min(jnp.where(gl == gmax, lane, far), axis=-1, keepdims=True)
    lo = N_GROUPS + EXP_PER_GROUP * gidx
    emask = (lane >= lo) & (lane < lo + EXP_PER_GROUP)
    el = jnp.where(emask, lg, -jnp.inf)
    m1 = jnp.max(el, axis=-1, keepdims=True)
    i1 = jnp.min(jnp.where(el == m1, lane, far), axis=-1, keepdims=True)
    el2 = jnp.where(lane == i1, -jnp.inf, el)
    m2 = jnp.max(el2, axis=-1, keepdims=True)
    i2 = jnp.min(jnp.where(emask & (lane != i1) & (el2 == m2), lane, far), axis=-1, keepdims=True)
    e = jnp.exp(m2 - m1)
    w0 = grp_p / (1.0 + e)
    w1 = grp_p * e / (1.0 + e)

    oh0 = lane == i1
    oh1 = lane == i2
    f0 = jnp.where(oh0, 1.0, 0.0)
    f1 = jnp.where(oh1, 1.0, 0.0)
    ri = lax.broadcasted_iota(jnp.int32, (tm, tm), 0)
    ci = lax.broadcasted_iota(jnp.int32, (tm, tm), 1)
    before = jnp.where(ci < ri, 1.0, 0.0).astype(BF16)
    cum0 = _dot(before, f0.astype(BF16))
    cum1 = _dot(before, f1.astype(BF16))
    tot0 = jnp.sum(f0, axis=0, keepdims=True)
    tot1 = jnp.sum(f1, axis=0, keepdims=True)
    carry = carry_ref[...]
    rank0 = jnp.sum(jnp.where(oh0, carry + cum0, 0.0), axis=-1, keepdims=True)
    rank1 = jnp.sum(jnp.where(oh1, carry + tot0 + cum1, 0.0), axis=-1, keepdims=True)
    carry = carry + tot0 + tot1
    carry_ref[...] = carry
    cnt_ref[...] = carry

    rec = jnp.where(lane == 0, i1 - N_GROUPS, 0.0)
    rec = jnp.where(lane == 1, i2 - N_GROUPS, rec)
    rec = jnp.where(lane == 2, w0, rec)
    rec = jnp.where(lane == 3, w1, rec)
    rec = jnp.where(lane == 4, rank0, rec)
    rec = jnp.where(lane == 5, rank1, rec)
    rec_ref[...] = rec
    rect_ref[...] = jnp.concatenate([rec[c * LANE:(c + 1) * LANE, :].T[0:8, :] for c in range(tm // LANE)], axis=1)


def _route(logits):
    nt = logits.shape[0]
    tm = min(ROUTE_ROWS, nt)
    return pl.pallas_call(
        functools.partial(_route_kernel, tm=tm),
        grid=(nt // tm,),
        in_specs=[pl.BlockSpec((tm, LANE), lambda i: (i, 0))],
        out_specs=[pl.BlockSpec((tm, LANE), lambda i: (i, 0)),
                   pl.BlockSpec((1, LANE), lambda i: (0, 0)),
                   pl.BlockSpec((8, tm), lambda i: (0, i))],
        out_shape=[jax.ShapeDtypeStruct((nt, LANE), F32),
                   jax.ShapeDtypeStruct((1, LANE), F32),
                   jax.ShapeDtypeStruct((8, nt), F32)],
        scratch_shapes=[pltpu.VMEM((1, LANE), F32)],
        compiler_params=_cparams(("arbitrary",)),
        name="moe_route",
    )(logits)


def _row_copy(src, dst, sem):
    return pltpu.make_async_copy(src, dst, sem)


def _tile_of(r):
    return pl.ds(pl.multiple_of(r * ROW_TILES, ROW_TILES), ROW_TILES)


def _load_rows(ref, n, lead=()):
    return jnp.concatenate([ref[lead + (pl.ds(c, n, stride=ROW_TILES), slice(None))]
                            for c in range(ROW_TILES)], axis=1)


def _store_rows(ref, val, row0=0):
    n = val.shape[0]
    for c in range(ROW_TILES):
        ref[pl.ds(row0 * ROW_TILES + c, n, stride=ROW_TILES), :] = val[:, c * LANE:(c + 1) * LANE]


def _dispatch_kernel(dest0_ref, dest1_ref, zb_ref, h_ref, xp_ref, z_ref, sem, zsem, *, tm):
    step = pl.program_id(0)
    base = step * tm
    blk = MOE_ROWS * ROW_TILES

    @pl.when(step == 0)
    def _():
        z_ref[...] = jnp.zeros_like(z_ref)

        def zero_copy(j):
            b = jnp.maximum(zb_ref[j], 0)
            return _row_copy(z_ref, xp_ref.at[pl.ds(pl.multiple_of(b * blk, blk), blk), :], zsem)

        def start(j, carry):
            @pl.when(zb_ref[j] >= 0)
            def _():
                zero_copy(j).start()
            return carry

        def wait(j, carry):
            @pl.when(zb_ref[j] >= 0)
            def _():
                zero_copy(j).wait()
            return carry

        lax.fori_loop(0, 2 * N_EXPERTS, start, 0)
        lax.fori_loop(0, 2 * N_EXPERTS, wait, 0)

    def issue(r, carry):
        for dest_ref in (dest0_ref, dest1_ref):
            d = dest_ref[base + r]
            _row_copy(h_ref.at[_tile_of(r), :], xp_ref.at[_tile_of(d), :], sem).start()
        return carry

    lax.fori_loop(0, tm, issue, 0, unroll=ROW_DMA_UNROLL)
    for _ in range(2):
        _row_copy(h_ref, xp_ref.at[pl.ds(0, tm * ROW_TILES), :], sem).wait()


def _dispatch(dest0, dest1, zero_blocks, h2, n_blocks):
    nt = h2.shape[0] // ROW_TILES
    tm = min(DISPATCH_TOKENS, nt)
    return pl.pallas_call(
        functools.partial(_dispatch_kernel, tm=tm),
        grid_spec=pltpu.PrefetchScalarGridSpec(
            num_scalar_prefetch=3,
            grid=(nt // tm,),
            in_specs=[pl.BlockSpec((tm * ROW_TILES, LANE), lambda i, *_: (i, 0))],
            out_specs=pl.BlockSpec(memory_space=pl.ANY),
            scratch_shapes=[pltpu.VMEM((MOE_ROWS * ROW_TILES, LANE), F32),
                            pltpu.SemaphoreType.DMA(()), pltpu.SemaphoreType.DMA(())]),
        out_shape=jax.ShapeDtypeStruct((n_blocks * MOE_ROWS * ROW_TILES, LANE), F32),
        compiler_params=_cparams(("arbitrary",)),
        name="moe_dispatch",
    )(dest0, dest1, zero_blocks, h2)


def _expert_kernel(be_ref, nu_ref, x_ref, w1_ref, w3_ref, w2_ref, y_ref, w1b_ref, w3b_ref, w2b_ref):
    i = pl.program_id(0)

    @pl.when((i < nu_ref[0]) & ((i == 0) | (be_ref[i] != be_ref[jnp.maximum(i - 1, 0)])))
    def _():
        w1b_ref[...] = w1_ref[0].astype(BF16)
        w3b_ref[...] = w3_ref[0].astype(BF16)
        w2b_ref[...] = w2_ref[0].astype(BF16)

    @pl.when(i < nu_ref[0])
    def _():
        xb = _load_rows(x_ref, MOE_ROWS).astype(BF16)
        a = _dot(xb, w1b_ref[...])
        b = _dot(xb, w3b_ref[...])
        hmid = (a * jax.nn.sigmoid(a) * b).astype(BF16)
        _store_rows(y_ref, _dot(hmid, w2b_ref[...]))

    @pl.when(i >= nu_ref[0])
    def _():
        y_ref[...] = jnp.zeros_like(y_ref)


def _experts(block_expert, n_used, x_pad, w1, w3, w2):
    d, de = w1.shape[1], w1.shape[2]
    nb = x_pad.shape[0] // (MOE_ROWS * ROW_TILES)
    return pl.pallas_call(
        _expert_kernel,
        grid_spec=pltpu.PrefetchScalarGridSpec(
            num_scalar_prefetch=2,
            grid=(nb,),
            in_specs=[pl.BlockSpec((MOE_ROWS * ROW_TILES, LANE), lambda i, be, nu: (i, 0)),
                      pl.BlockSpec((1, d, de), lambda i, be, nu: (be[i], 0, 0)),
                      pl.BlockSpec((1, d, de), lambda i, be, nu: (be[i], 0, 0)),
                      pl.BlockSpec((1, de, d), lambda i, be, nu: (be[i], 0, 0))],
            out_specs=pl.BlockSpec((MOE_ROWS * ROW_TILES, LANE), lambda i, be, nu: (i, 0)),
            scratch_shapes=[pltpu.VMEM((d, de), BF16), pltpu.VMEM((d, de), BF16), pltpu.VMEM((de, d), BF16)]),
        out_shape=jax.ShapeDtypeStruct(x_pad.shape, F32),
        compiler_params=_cparams(("arbitrary",)),
        name="moe_experts",
    )(block_expert, n_used, x_pad, w1, w3, w2)


def _combine_kernel(dest0_ref, dest1_ref, yp_ref, rec_ref, x1_ref, mod_ref, g_ref, b_ref, o_ref,
                    buf_ref, sem, *, tm):
    step = pl.program_id(0)
    slot = step % 2

    def fetch(tile, to_slot):
        def issue(r, carry):
            for k, dest_ref in enumerate((dest0_ref, dest1_ref)):
                d = dest_ref[tile * tm + r]
                _row_copy(yp_ref.at[_tile_of(d), :], buf_ref.at[to_slot, k, _tile_of(r), :],
                          sem.at[to_slot]).start()
            return carry

        lax.fori_loop(0, tm, issue, 0, unroll=ROW_DMA_UNROLL)

    @pl.when(step == 0)
    def _():
        fetch(0, 0)

    @pl.when(step + 1 < pl.num_programs(0))
    def _():
        fetch(step + 1, 1 - slot)

    for k in range(2):
        _row_copy(yp_ref.at[pl.ds(0, tm * ROW_TILES), :], buf_ref.at[slot, k], sem.at[slot]).wait()

    rec = rec_ref[...]
    y = rec[:, 2:3] * _load_rows(buf_ref, tm, (slot, 0)) + rec[:, 3:4] * _load_rows(buf_ref, tm, (slot, 1))
    y = (1.0 + mod_ref[0, 5:6, :]) * y
    o_ref[...] = _layer_norm(ALPHA * x1_ref[...] + y) * g_ref[...] + b_ref[...]


def _combine(dest0, dest1, y_pad, rec, x1, mod, ln_g, ln_b, seq):
    nt, d = x1.shape
    tm = min(COMBINE_TOKENS, seq)
    return pl.pallas_call(
        functools.partial(_combine_kernel, tm=tm),
        grid_spec=pltpu.PrefetchScalarGridSpec(
            num_scalar_prefetch=2,
            grid=(nt // tm,),
            in_specs=[pl.BlockSpec(memory_space=pl.ANY),
                      pl.BlockSpec((tm, LANE), lambda i, *_: (i, 0)),
                      pl.BlockSpec((tm, d), lambda i, *_: (i, 0)),
                      pl.BlockSpec((1, 6, d), lambda i, *_: (i * tm // seq, 0, 0)),
                      pl.BlockSpec((1, d), lambda i, *_: (0, 0)),
                      pl.BlockSpec((1, d), lambda i, *_: (0, 0))],
            out_specs=pl.BlockSpec((tm, d), lambda i, *_: (i, 0)),
            scratch_shapes=[pltpu.VMEM((2, 2, tm * ROW_TILES, LANE), F32), pltpu.SemaphoreType.DMA((2,))]),
        out_shape=jax.ShapeDtypeStruct((nt, d), F32),
        compiler_params=_cparams(("arbitrary",)),
        name="moe_combine",
    )(dest0, dest1, y_pad, rec, x1, mod, ln_g, ln_b)


def _rel_bucket(dist):
    n = jnp.maximum(dist, 0)
    max_exact = REL_BUCKETS // 2
    nf = jnp.maximum(n, 1).astype(F32)
    large = max_exact + (jnp.log(nf / max_exact) / math.log(REL_MAX_DIST / max_exact)
                         * (REL_BUCKETS - max_exact)).astype(jnp.int32)
    large = jnp.minimum(large, REL_BUCKETS - 1)
    return jnp.where(n < max_exact, n, large)


def _bias_tables(rel_bias, seq):
    bucket_onehot = (_rel_bucket(jnp.arange(LANE))[:, None] == jnp.arange(REL_BUCKETS)).astype(F32)
    tab_d = jnp.einsum('db,hb->hd', bucket_onehot, rel_bias,
                       precision=lax.Precision.HIGHEST)
    tok = np.arange(QT)[None, :]
    key = np.arange(LANE)[:, None]
    far = tab_d[:, LANE - 1]
    cols = NSA_REP * QT

    def transposed(dist):
        idx = jnp.asarray(np.clip(dist, 0, LANE - 1).astype(np.int32))
        onehot = (idx[..., None] == jnp.arange(LANE, dtype=jnp.int32)).astype(F32)
        t = jnp.einsum('ijd,hd->hij', onehot, tab_d, precision=lax.Precision.HIGHEST)
        t = (t - far[:, None, None]) * LOG2E
        t = t.reshape(NSA_GROUPS, NSA_REP, LANE, QT).transpose(0, 2, 1, 3)
        return t.reshape(NSA_GROUPS, LANE, cols)

    t0t = transposed(tok - key)
    t1t = transposed(tok - key + QT)

    ns = seq // CMP_STRIDE
    d_c = tok - CMP_STRIDE * key + (CMP_STRIDE * CMP_PAD - (CMP_BLOCK - 1))
    seen = np.tile(d_c >= 0, (1, NSA_REP))
    recent = jnp.where(seen[None], transposed(d_c), NEG)
    tt = jnp.concatenate([jnp.zeros((NSA_GROUPS, ns, cols), F32), recent,
                          jnp.full((NSA_GROUPS, ns, cols), NEG, F32)], axis=1)

    rho = np.arange(WINDOW + QT)[:, None]
    tok_w = np.tile(np.arange(QT), NSA_REP)[None, :]
    band = (rho > tok_w) & (rho <= tok_w + WINDOW)
    rows = jnp.concatenate([jnp.zeros((NSA_GROUPS, WINDOW - QT, NSA_REP * QT), F32), t1t, t0t], axis=1)
    wb = jnp.where(band[None], rows, NEG)

    zeros = lambda n: jnp.zeros((NSA_GROUPS, n * QT, cols), F32)
    negs = lambda n: jnp.full((NSA_GROUPS, n * QT, cols), NEG, F32)
    diag = jnp.where(np.tile(tok >= key, (1, NSA_REP))[None], t0t, NEG)
    nd = jnp.stack([jnp.concatenate(([zeros(u - 1), t1t] if u else []) + [diag, negs(SEL_SUB - 1 - u)], axis=1)
                    for u in range(SEL_SUB)], axis=1)
    npv = jnp.concatenate([zeros(SEL_SUB - 1), t1t], axis=1)
    return nd, npv, wb, tt


def _overlap_matrix(seq):
    ns = seq // CMP_STRIDE
    nslc = seq // SLC_BLOCK
    ov = np.zeros((LANE, ns), np.float32)
    cs = np.arange(ns - 1) * CMP_STRIDE
    ss = np.arange(nslc) * SLC_BLOCK
    ov[:nslc, :ns - 1] = ((cs[None, :] < ss[:, None] + SLC_BLOCK) & (cs[None, :] + CMP_BLOCK > ss[:, None]))
    return jnp.asarray(ov, BF16)


def _reorder_cols(a):
    lead = a.shape[:-1]
    gate = a[..., MAIN_COLS:MAIN_COLS + GATE_COLS]
    per = GATE_COLS // NSA_GROUPS
    gate_blocks = []
    for g in range(NSA_GROUPS):
        gate_blocks.append(gate[..., g * per:(g + 1) * per])
        gate_blocks.append(jnp.zeros(lead + (LANE - per,), a.dtype))
    pad = jnp.zeros(lead + ((CB_MGH - CB_GATE - NSA_GROUPS) * LANE,), a.dtype)
    return jnp.concatenate([a[..., :MAIN_COLS]] + gate_blocks + [pad, a[..., MAIN_COLS + GATE_COLS:]], axis=-1)


def kernel(x, c, ada_w, ada_b, w_in, b_in, hg_lb_logits, hg_norm_w, cmp_pos_k, cmp_w1_k, cmp_b1_k, cmp_w2_k, cmp_pos_v, cmp_w1_v, cmp_b1_v, cmp_w2_v, rel_bias, w_br_hg, w_br_nsa, w_out, ln1_g, ln1_b, router_grp_w, router_grp_b, router_exp_w, router_exp_b, exp_w1, exp_w3, exp_w2, ln2_g, ln2_b):
    batch, seq, d = x.shape
    nt = batch * seq
    assert d == D_MODEL and seq % INPROJ_ROWS == 0 and seq // SLC_BLOCK <= LANE
    l = 0
    x2 = x.reshape(nt, d)

    c_pad = jnp.zeros((8, d), F32).at[:batch].set(c)
    mod = _adaln(c_pad, ada_w[l], ada_b[l][None])[:batch].reshape(batch, 6, d)

    proj = _inproj(x2, mod, _reorder_cols(w_in[l].astype(BF16)), _reorder_cols(b_in[l])[None], seq)

    lb_all = jnp.cumsum(jax.nn.softmax(hg_lb_logits.astype(F32), axis=0), axis=0)
    o_h = _hgrn(proj, lb_all[l][None], hg_norm_w[l][None], batch, seq)

    kvc = _compress(proj, jnp.stack([cmp_pos_k[l], cmp_pos_v[l]]),
                    jnp.stack([cmp_w1_k[l], cmp_w1_v[l]]).astype(BF16),
                    jnp.stack([cmp_b1_k[l], cmp_b1_v[l]])[:, None, :],
                    jnp.stack([cmp_w2_k[l], cmp_w2_v[l]]).astype(BF16), batch, seq)

    nd, npv, wb, tt = _bias_tables(rel_bias, seq)
    o_c, sel_t = _cmp_sel(proj, kvc, tt, _overlap_matrix(seq), batch, seq)
    o_w = _window(proj, wb, batch, seq)
    o_n = _selected(proj, sel_t, nd, npv, o_c, o_w, batch, seq)

    w_r = jnp.zeros((d, LANE), F32).at[:, :N_GROUPS].set(router_grp_w[l])
    w_r = w_r.at[:, N_GROUPS:N_GROUPS + N_EXPERTS].set(router_exp_w[l])
    w_r_hi = w_r.astype(BF16)
    w_r_lo = (w_r - w_r_hi.astype(F32)).astype(BF16)
    b_r = jnp.zeros((1, LANE), F32).at[0, :N_GROUPS].set(router_grp_b[l])
    b_r = b_r.at[0, N_GROUPS:N_GROUPS + N_EXPERTS].set(router_exp_b[l])
    x1, h2, logits = _merge(o_h, o_n, proj, x2, mod,
                            w_br_hg[l].astype(BF16), w_br_nsa[l].astype(BF16), w_out[l].astype(BF16),
                            ln1_g[l][None], ln1_b[l][None], jnp.stack([w_r_hi, w_r_lo]), b_r, seq)

    rec, cnt, rec_t = _route(logits)
    counts = cnt[0, N_GROUPS:N_GROUPS + N_EXPERTS].astype(jnp.int32)
    padded = (counts + MOE_ROWS - 1) // MOE_ROWS * MOE_ROWS
    pend = jnp.cumsum(padded)
    pstart = pend - padded
    n_assign = 2 * nt
    nb = n_assign // MOE_ROWS + N_EXPERTS
    slots = rec_t.astype(jnp.int32)
    expert_ids = jnp.arange(N_EXPERTS, dtype=jnp.int32)[:, None]
    slot_base = lambda e: jnp.sum(jnp.where(e[None, :] == expert_ids, pstart[:, None], 0), axis=0)
    dest0 = slot_base(slots[0]) + slots[4]
    dest1 = slot_base(slots[1]) + slots[5]
    block_start = jnp.arange(nb, dtype=jnp.int32) * MOE_ROWS
    block_expert = jnp.minimum(jnp.sum(pend[None, :] <= block_start[:, None], axis=1),
                               N_EXPERTS - 1).astype(jnp.int32)
    n_used = pend[-1] // MOE_ROWS
    spare = n_used + jnp.arange(N_EXPERTS, dtype=jnp.int32)
    zero_blocks = jnp.concatenate([jnp.where(padded > 0, pend // MOE_ROWS - 1, -1),
                                   jnp.where(spare < nb, spare, -1)]).astype(jnp.int32)

    x_pad = _dispatch(dest0, dest1, zero_blocks, h2, nb)
    y_pad = _experts(block_expert, n_used[None].astype(jnp.int32), x_pad, exp_w1[l], exp_w3[l], exp_w2[l])
    out = _combine(dest0, dest1, y_pad, rec, x1, mod, ln2_g[l][None], ln2_b[l][None], seq)
    return out.reshape(batch, seq, d)
```

```python
import functools
import math

import numpy as np
import jax
import jax.numpy as jnp
from jax import lax
from jax.experimental import pallas as pl
from jax.experimental.pallas import tpu as pltpu

F32 = jnp.float32
BF16 = jnp.bfloat16

D_MODEL = 1024
HG_HEADS = 8
HG_DK = 128
HG_DV = 128
HG_CHUNK = 32
HG_SUPER = 256
NSA_HEADS = 8
NSA_GROUPS = 2
NSA_REP = NSA_HEADS // NSA_GROUPS
NSA_DK = 128
CMP_BLOCK = 32
CMP_STRIDE = 16
SLC_BLOCK = 64
SLC_TOPK = 16
WINDOW = 512
FORCE_SCORE = 1e4
N_FORCED = 3
REL_BUCKETS = 32
REL_MAX_DIST = 128
N_GROUPS = 4
EXP_PER_GROUP = 8
N_EXPERTS = N_GROUPS * EXP_PER_GROUP
D_EXPERT = D_MODEL // 2
DEPTH = 1
ALPHA = (2 * DEPTH) ** 0.25

LANE = 128
QT = 128
NEG = -1e30
SCALE = NSA_DK ** -0.5
LOG2E = math.log2(math.e)
SCALE_LOG2 = SCALE * LOG2E
KEY_BLK = 512
BOUND_SLACK = 1.0 + 2.0 ** -10
MAX_REF_GAP = 64.0
SEL_ROWS = 16
SEL_SUB = KEY_BLK // QT
WIN_SUB = 8
CMP_SUB = 8
CMP_PAD = 120
VMEM_LIMIT = 56 * 1024 * 1024

CB_HQ, CB_HF, CB_HI, CB_HG = 0, 8, 16, 24
CB_NQ = 32
CB_KC, CB_VC, CB_KS, CB_VS, CB_KW, CB_VW = 40, 42, 44, 46, 48, 50
CB_GATE = 52
CB_MGH, CB_MGN = 56, 64
NCB = 72
MAIN_COLS = 52 * LANE
GATE_COLS = 3 * NSA_HEADS

INPROJ_ROWS = 2048
INPROJ_COL_BLOCKS = 8
HGRN_ROWS = 512
MERGE_ROWS = 512
ROUTE_ROWS = 512
DISPATCH_TOKENS = 2048
COMBINE_TOKENS = 256
MERGE_SPLIT = 2
ROW_TILES = D_MODEL // LANE
MOE_ROWS = 512
ROW_DMA_UNROLL = 8


def _cparams(sem):
    return pltpu.CompilerParams(dimension_semantics=sem, vmem_limit_bytes=VMEM_LIMIT)


def _dot(a, b):
    return jnp.dot(a, b, preferred_element_type=F32)


def _dot_nt(a, b):
    return lax.dot_general(a, b, (((1,), (1,)), ((), ())), preferred_element_type=F32)


def _dot_tn(a, b):
    return lax.dot_general(a, b, (((0,), (0,)), ((), ())), preferred_element_type=F32)


def _split3(x):
    hi = x.astype(BF16)
    r = x - hi.astype(F32)
    mid = r.astype(BF16)
    lo = (r - mid.astype(F32)).astype(BF16)
    return hi, mid, lo


def _dot01(m01, x):
    hi, mid, lo = _split3(x)
    return _dot(m01, hi) + _dot(m01, mid) + _dot(m01, lo)


def _layer_norm(x, eps=1e-5):
    mu = jnp.mean(x, axis=-1, keepdims=True)
    xc = x - mu
    var = jnp.mean(xc * xc, axis=-1, keepdims=True)
    return xc * lax.rsqrt(var + eps)


def _adaln_kernel(c_ref, w_ref, b_ref, o_ref):
    c = c_ref[...]
    ca = c * jax.nn.sigmoid(c)
    o_ref[...] = jnp.dot(ca, w_ref[...], precision=lax.Precision.HIGHEST,
                         preferred_element_type=F32) + b_ref[...]


def _adaln(c_pad, w, b):
    rows, d = c_pad.shape
    n = w.shape[1]
    return pl.pallas_call(
        _adaln_kernel,
        grid=(n // d,),
        in_specs=[pl.BlockSpec((rows, d), lambda j: (0, 0)),
                  pl.BlockSpec((d, d), lambda j: (0, j)),
                  pl.BlockSpec((1, d), lambda j: (0, j))],
        out_specs=pl.BlockSpec((rows, d), lambda j: (0, j)),
        out_shape=jax.ShapeDtypeStruct((rows, n), F32),
        compiler_params=_cparams(("arbitrary",)),
        name="adaln",
    )(c_pad, w, b)


def _inproj_kernel(x_ref, mod_ref, w_ref, b_ref, o_ref, hn_ref, *, ncb_tile):
    @pl.when(pl.program_id(1) == 0)
    def _():
        hn = _layer_norm(x_ref[...])
        sh = mod_ref[0, 0:1, :]
        sc = mod_ref[0, 1:2, :]
        hn_ref[...] = (hn * (1.0 + sc) + sh).astype(BF16)

    res = _dot(hn_ref[...], w_ref[...]) + b_ref[...]
    for c in range(ncb_tile):
        o_ref[c] = res[:, c * LANE:(c + 1) * LANE]


def _inproj(x2, mod, w, b, seq):
    nt, d = x2.shape
    tm = min(INPROJ_ROWS, seq)
    ncb_tile = INPROJ_COL_BLOCKS
    tn = ncb_tile * LANE
    return pl.pallas_call(
        functools.partial(_inproj_kernel, ncb_tile=ncb_tile),
        grid=(nt // tm, NCB // ncb_tile),
        in_specs=[pl.BlockSpec((tm, d), lambda i, j: (i, 0)),
                  pl.BlockSpec((1, 6, d), lambda i, j: (i * tm // seq, 0, 0)),
                  pl.BlockSpec((d, tn), lambda i, j: (0, j)),
                  pl.BlockSpec((1, tn), lambda i, j: (0, j))],
        out_specs=pl.BlockSpec((ncb_tile, tm, LANE), lambda i, j: (j, i, 0)),
        out_shape=jax.ShapeDtypeStruct((NCB, nt, LANE), F32),
        scratch_shapes=[pltpu.VMEM((tm, d), BF16)],
        compiler_params=_cparams(("arbitrary", "arbitrary")),
        name="inproj",
    )(x2, mod, w, b)


def _hgrn_kernel(q_ref, f_ref, v_ref, g_ref, lb_ref, nw_ref, o_ref, st_ref, *, rows):
    @pl.when(pl.program_id(1) == 0)
    def _():
        st_ref[...] = jnp.zeros_like(st_ref)

    sup = HG_SUPER
    ri = lax.broadcasted_iota(jnp.int32, (sup, sup), 0)
    ci = lax.broadcasted_iota(jnp.int32, (sup, sup), 1)
    same = (ri // HG_CHUNK) == (ci // HG_CHUNK)
    cum_m = jnp.where(same & (ci <= ri), 1.0, 0.0).astype(BF16)
    rt = lax.broadcasted_iota(jnp.int32, (LANE, LANE), 0)
    ct = lax.broadcasted_iota(jnp.int32, (LANE, LANE), 1)
    tril = ((rt // HG_CHUNK) == (ct // HG_CHUNK)) & (ct <= rt)
    per = sup // HG_CHUNK
    groups = [slice(g * LANE, (g + 1) * LANE) for g in range(sup // LANE)]

    heads = range(HG_HEADS)
    hs = [slice(h * LANE, (h + 1) * LANE) for h in heads]

    def wide(ref, r0):
        return jnp.concatenate([ref[h, pl.ds(r0, sup), :] for h in heads], axis=1)

    def body(i, carry):
        r0 = pl.multiple_of(i * sup, sup)
        lb = lb_ref[...]
        f = lb + (1.0 - lb) * jax.nn.sigmoid(wide(f_ref, r0))
        lf = jnp.log(f)
        k = 1.0 - f
        hi = lf.astype(BF16)
        lo = (lf - hi.astype(F32)).astype(BF16)
        b = _dot(cum_m, hi) + _dot(cum_m, lo)
        chunk = [slice(c * HG_CHUNK, (c + 1) * HG_CHUNK) for c in range(per)]
        dec = [jnp.exp(b[(c + 1) * HG_CHUNK - 1:(c + 1) * HG_CHUNK, :]) for c in range(per)]
        dec_rows = jnp.concatenate([jnp.broadcast_to(dec[c], (HG_CHUNK, dec[c].shape[1])) for c in range(per)],
                                   axis=0)
        q_in = (wide(q_ref, r0) * jnp.exp(b)).astype(BF16)
        k_dec = k * jnp.exp(-b)
        k_in = k_dec.astype(BF16)
        k_end = (k_dec * dec_rows).astype(BF16)
        vb = wide(v_ref, r0).astype(BF16)

        att = [[jnp.where(tril, _dot_nt(q_in[g, hs[h]], k_in[g, hs[h]]), 0.0).astype(BF16) for g in groups]
               for h in heads]
        upd = [[_dot_tn(vb[chunk[c], hs[h]], k_end[chunk[c], hs[h]]) for c in range(per)] for h in heads]
        intra = [[_dot(att[h][n], vb[g, hs[h]]) for n, g in enumerate(groups)] for h in heads]

        st = [st_ref[h] for h in heads]
        inter = [[] for _ in heads]
        for c in range(per):
            for h in heads:
                inter[h].append(_dot_nt(q_in[chunk[c], hs[h]], st[h].astype(BF16)))
                st[h] = dec[c][:, hs[h]] * st[h] + upd[h][c]
        for h in heads:
            st_ref[h] = st[h]

        o = [jnp.concatenate(intra[h], axis=0) + jnp.concatenate(inter[h], axis=0) for h in heads]
        scale = [lax.rsqrt(jnp.mean(o[h] * o[h], axis=-1, keepdims=True) + 1e-6) for h in heads]
        on = jnp.concatenate([o[h] * scale[h] for h in heads], axis=1)
        on = on * nw_ref[...] * jax.nn.sigmoid(wide(g_ref, r0))
        o_ref[pl.ds(r0, sup), :] = on.astype(BF16)
        return carry

    lax.fori_loop(0, rows // sup, body, 0)


def _hgrn(proj, lb, nw, batch, seq):
    nt = proj.shape[1]
    tb = min(HGRN_ROWS, seq)
    nblk = seq // tb

    def slab(cb0):
        return pl.BlockSpec((HG_HEADS, tb, LANE), lambda b, t: (cb0 // HG_HEADS, b * nblk + t, 0))

    vec = pl.BlockSpec((1, HG_HEADS * LANE), lambda b, t: (0, 0))
    return pl.pallas_call(
        functools.partial(_hgrn_kernel, rows=tb),
        grid=(batch, nblk),
        in_specs=[slab(CB_HQ), slab(CB_HF), slab(CB_HI), slab(CB_HG), vec, vec],
        out_specs=pl.BlockSpec((tb, HG_HEADS * HG_DV), lambda b, t: (b * nblk + t, 0)),
        out_shape=jax.ShapeDtypeStruct((nt, HG_HEADS * HG_DV), BF16),
        scratch_shapes=[pltpu.VMEM((HG_HEADS, HG_DV, HG_DK), F32)],
        compiler_params=_cparams(("arbitrary", "arbitrary")),
        name="hgrn2",
    )(proj, proj, proj, proj, lb, nw)


def _compress_kernel(x_ref, pos_ref, w1_ref, b1_ref, w2_ref, o_ref, *, ns):
    p0 = jnp.zeros((ns, LANE), F32)
    p1 = jnp.zeros((ns, LANE), F32)
    for j in range(CMP_STRIDE):
        tok = x_ref[0, pl.ds(j, ns, stride=CMP_STRIDE), :]
        rows = slice(j * LANE, (j + 1) * LANE)
        late = slice((CMP_STRIDE + j) * LANE, (CMP_STRIDE + j + 1) * LANE)
        p0 = p0 + _dot((tok + pos_ref[0, j:j + 1, :]).astype(BF16), w1_ref[0, rows, :])
        p1 = p1 + _dot((tok + pos_ref[0, CMP_STRIDE + j:CMP_STRIDE + j + 1, :]).astype(BF16), w1_ref[0, late, :])
    h = p0 + pltpu.roll(p1, ns - 1, axis=0) + b1_ref[0]
    a = h * jax.nn.sigmoid(h)
    out = _dot(a.astype(BF16), w2_ref[0])
    row = lax.broadcasted_iota(jnp.int32, out.shape, 0)
    out = jnp.where(row < ns - 1, out, 0.0)
    o_ref[0, 0, 0] = out


def _compress(proj, pos, w1, b1, w2, batch, seq):
    ns = seq // CMP_STRIDE
    np_rows = ns
    width = CMP_STRIDE * LANE
    return pl.pallas_call(
        functools.partial(_compress_kernel, ns=ns),
        grid=(2, batch, NSA_GROUPS),
        in_specs=[pl.BlockSpec((1, seq, LANE), lambda s, b, g: (CB_KC + NSA_GROUPS * s + g, b, 0)),
                  pl.BlockSpec((1, CMP_BLOCK, LANE), lambda s, b, g: (s, 0, 0)),
                  pl.BlockSpec((1, 2 * width, LANE), lambda s, b, g: (s, 0, 0)),
                  pl.BlockSpec((1, 1, LANE), lambda s, b, g: (s, 0, 0)),
                  pl.BlockSpec((1, LANE, LANE), lambda s, b, g: (s, 0, 0))],
        out_specs=pl.BlockSpec((1, 1, 1, np_rows, LANE), lambda s, b, g: (s, b, g, 0, 0)),
        out_shape=jax.ShapeDtypeStruct((2, batch, NSA_GROUPS, np_rows, LANE), F32),
        compiler_params=_cparams(("arbitrary", "arbitrary", "arbitrary")),
        name="nsa_compress",
    )(proj, pos, w1, b1, w2)


def _cmp_sel_kernel(q0, q1, q2, q3, kc_ref, vc_ref, gate_ref, tt_ref, ov_ref, oc_ref, sel_ref,
                    kb_ref, vt_ref, imp_ref, *, np_rows):
    step = pl.program_id(2)
    cols = NSA_REP * QT
    ns = np_rows

    @pl.when(step == 0)
    def _():
        kb_ref[...] = (kc_ref[0, 0, 0] * SCALE_LOG2).astype(BF16)
        for c in range(np_rows // LANE):
            vt_ref[:, c * LANE:(c + 1) * LANE] = vc_ref[0, 0, 0, c * LANE:(c + 1) * LANE, :].T.astype(BF16)

    subs = range(CMP_SUB)
    tis = [step * CMP_SUB + u for u in subs]
    qts = [_q_transposed((q0, q1, q2, q3), u) for u in subs]

    def group_sum(p):
        tot = p[:, 0:QT]
        for r in range(1, NSA_REP):
            tot = tot + p[:, r * QT:(r + 1) * QT]
        return tot

    def attend(nrows):
        kb = kb_ref[0:nrows, :]
        bias = [tt_ref[0, pl.ds(pl.multiple_of(ns + CMP_PAD - tis[u] * (QT // CMP_STRIDE), 8), nrows), :]
                for u in subs]
        ss = [_dot(kb, qts[u]) + bias[u] for u in subs]
        ms = [jnp.max(ss[u], axis=0, keepdims=True) for u in subs]
        ps = [jnp.exp2(ss[u] - ms[u]) for u in subs]
        ls = [jnp.sum(ps[u], axis=0, keepdims=True) for u in subs]
        invs = [jnp.where(ms[u] > 0.5 * NEG, 1.0 / ls[u], 0.0) for u in subs]
        pn = [ps[u] * invs[u] for u in subs]
        vt = vt_ref[:, 0:nrows]
        os_ = [_dot(vt, pn[u].astype(BF16)) for u in subs]
        for u in subs:
            rows = slice(u * QT, (u + 1) * QT)
            gt = jax.nn.sigmoid(gate_ref[0, rows, :])
            for r in range(NSA_REP):
                oc_ref[rows, r * LANE:(r + 1) * LANE] = gt[:, 3 * r:3 * r + 1] * os_[u][:, r * QT:(r + 1) * QT].T
            nblk = nrows * CMP_STRIDE // SLC_BLOCK
            imp_ref[0:nblk, rows] = _dot01(ov_ref[0:nblk, 0:nrows], group_sum(pn[u]))

    width = CMP_SUB * QT

    def choose(nblk):
        assert FORCE_SCORE > NSA_REP and nblk >= SLC_TOPK
        jj = lax.broadcasted_iota(jnp.int32, (nblk, width), 0)
        tok = step * width + lax.broadcasted_iota(jnp.int32, (nblk, width), 1)
        cur = tok // SLC_BLOCK
        forced = (jj == 0) | (jj == cur) | (jj == cur - 1)
        score = jnp.where(forced, -jnp.inf, jnp.where(jj <= cur, imp_ref[0:nblk, :], -1.0))
        jf = jj.astype(F32)
        for _ in range(SLC_TOPK - N_FORCED):
            mval = jnp.max(score, axis=0, keepdims=True)
            first = jnp.min(jnp.where(score == mval, jf, float(LANE)), axis=0, keepdims=True)
            score = jnp.where(jf == first, -jnp.inf, score)
        sel_ref[0, 0, 0:nblk, :] = jnp.where(score == -jnp.inf, 0.0, NEG)
        if nblk < LANE:
            sel_ref[0, 0, nblk:LANE, :] = jnp.full((LANE - nblk, width), NEG, F32)

    visible = (step + 1) * CMP_SUB * (QT // CMP_STRIDE)
    ngroups = np_rows // LANE
    for g in range(1, ngroups + 1):
        upper = visible <= g * LANE if g < ngroups else True

        @pl.when((visible > (g - 1) * LANE) & upper)
        def _():
            attend(g * LANE)
            choose(g * LANE * CMP_STRIDE // SLC_BLOCK)


def _q_specs(nsteps, rows=QT):
    return [pl.BlockSpec((1, rows, LANE),
                         functools.partial(lambda b, g, t, r: (CB_NQ + NSA_REP * g + r, b * nsteps + t, 0), r=r))
            for r in range(NSA_REP)]


def _cmp_sel(proj, kvc, tt, ov_t, batch, seq):
    nt = proj.shape[1]
    rows = CMP_SUB * QT
    nsteps = seq // rows
    cols = NSA_REP * QT
    np_rows = kvc.shape[3]
    kv_spec = lambda s: pl.BlockSpec((1, 1, 1, np_rows, LANE), lambda b, g, t: (s, b, g, 0, 0))
    return pl.pallas_call(
        functools.partial(_cmp_sel_kernel, np_rows=np_rows),
        grid=(batch, NSA_GROUPS, nsteps),
        in_specs=_q_specs(nsteps, rows) + [
            kv_spec(0), kv_spec(1),
            pl.BlockSpec((1, rows, LANE), lambda b, g, t: (CB_GATE + g, b * nsteps + t, 0)),
            pl.BlockSpec((1, tt.shape[1], cols), lambda b, g, t: (g, 0, 0)),
            pl.BlockSpec((LANE, np_rows), lambda b, g, t: (0, 0))],
        out_specs=[pl.BlockSpec((rows, cols), lambda b, g, t: (b * nsteps + t, g)),
                   pl.BlockSpec((1, 1, LANE, rows), lambda b, g, t: (b, g, 0, t))],
        out_shape=[jax.ShapeDtypeStruct((nt, NSA_HEADS * LANE), F32),
                   jax.ShapeDtypeStruct((batch, NSA_GROUPS, LANE, seq), F32)],
        scratch_shapes=[pltpu.VMEM((np_rows, LANE), BF16), pltpu.VMEM((LANE, np_rows), BF16),
                        pltpu.VMEM((LANE, rows), F32)],
        compiler_params=_cparams(("arbitrary", "arbitrary", "arbitrary")),
        name="nsa_cmp_select",
    )(proj, proj, proj, proj, kvc, kvc, proj, tt, ov_t)


def _q_transposed(q_refs, sub=0):
    return jnp.concatenate([r[0, sub * QT:(sub + 1) * QT, :].T for r in q_refs], axis=1).astype(BF16)


def _finish_t(state, gate_ref, branch, o_ref, sub=0, others=()):
    _, l, acc = state
    o = acc / jnp.where(l == 0.0, 1.0, l)
    rows = slice(sub * QT, (sub + 1) * QT)
    gt = jax.nn.sigmoid(gate_ref[0, rows, :])
    for r in range(NSA_REP):
        col = 3 * r + branch
        lanes = slice(r * LANE, (r + 1) * LANE)
        val = gt[:, col:col + 1] * o[:, r * QT:(r + 1) * QT].T
        for other in others:
            val = val + other[rows, lanes]
        o_ref[rows, lanes] = val.astype(o_ref.dtype)


def _init_state(cols):
    return (jnp.full((1, cols), NEG, F32), jnp.zeros((1, cols), F32), jnp.zeros((LANE, cols), F32))


def _win_kernel(q0, q1, q2, q3, k_ref, v_ref, gate_ref, wb_ref, o_ref, kb_ref, vt_ref, *, seq):
    step = pl.program_id(2)
    cols = NSA_REP * QT
    wt = WINDOW // QT

    lane = lax.broadcasted_iota(jnp.int32, (QT, LANE), 1)
    pad_mark = jnp.where(lane == 0, 1.0, 0.0).astype(BF16)

    @pl.when(step == 0)
    def _():
        for i in range(wt):
            kb_ref[i * QT:(i + 1) * QT, 0:LANE] = jnp.zeros((QT, LANE), BF16)
            kb_ref[i * QT:(i + 1) * QT, LANE:2 * LANE] = pad_mark
            vt_ref[i] = jnp.zeros((LANE, QT), BF16)

        def fill(i, carry):
            r0 = pl.multiple_of(i * QT, QT)
            kb_ref[pl.ds(WINDOW + r0, QT), 0:LANE] = (k_ref[0, pl.ds(r0, QT), :] * SCALE_LOG2).astype(BF16)
            kb_ref[pl.ds(WINDOW + r0, QT), LANE:2 * LANE] = jnp.zeros((QT, LANE), BF16)
            vt_ref[wt + i] = v_ref[0, pl.ds(r0, QT), :].T.astype(BF16)
            return carry

        lax.fori_loop(0, seq // QT, fill, 0)

    subs = range(WIN_SUB)
    tis = [step * WIN_SUB + sub for sub in subs]
    row = lax.broadcasted_iota(jnp.int32, (LANE, cols), 0)
    pad_rows = jnp.where(row == 0, NEG, 0.0).astype(BF16)
    qts = [jnp.concatenate([_q_transposed((q0, q1, q2, q3), sub), pad_rows], axis=0) for sub in subs]
    ss = [_dot(kb_ref[pl.ds(pl.multiple_of(tis[u] * QT, QT), WINDOW + QT), :], qts[u]) + wb_ref[0]
          for u in subs]
    ms = [jnp.max(ss[u], axis=0, keepdims=True) for u in subs]
    ps = [jnp.exp2(ss[u] - ms[u]) for u in subs]
    ls = [jnp.sum(ps[u], axis=0, keepdims=True) for u in subs]
    pbs = [ps[u].astype(BF16) for u in subs]
    accs = None
    for d in range(0, wt + 1, 2):
        n = min(2, wt + 1 - d)
        vts = [jnp.concatenate([vt_ref[tis[u] + d + e] for e in range(n)], axis=1) for u in subs]
        part = [_dot(vts[u], pbs[u][d * QT:(d + n) * QT]) for u in subs]
        accs = part if accs is None else [accs[u] + part[u] for u in subs]
    for u in subs:
        _finish_t((ms[u], ls[u], accs[u]), gate_ref, 2, o_ref, u)


def _window(proj, wb, batch, seq):
    nt = proj.shape[1]
    rows = WIN_SUB * QT
    nqt = seq // rows
    cols = NSA_REP * QT
    slab = lambda cb0: pl.BlockSpec((1, seq, LANE), lambda b, g, t: (cb0 + g, b, 0))
    return pl.pallas_call(
        functools.partial(_win_kernel, seq=seq),
        grid=(batch, NSA_GROUPS, nqt),
        in_specs=_q_specs(nqt, rows) + [
            slab(CB_KW), slab(CB_VW),
            pl.BlockSpec((1, rows, LANE), lambda b, g, t: (CB_GATE + g, b * nqt + t, 0)),
            pl.BlockSpec((1, WINDOW + QT, cols), lambda b, g, t: (g, 0, 0))],
        out_specs=pl.BlockSpec((rows, cols), lambda b, g, t: (b * nqt + t, g)),
        out_shape=jax.ShapeDtypeStruct((nt, NSA_HEADS * LANE), F32),
        scratch_shapes=[pltpu.VMEM((seq + WINDOW, 2 * LANE), BF16),
                        pltpu.VMEM((seq // QT + WINDOW // QT, LANE, QT), BF16)],
        compiler_params=_cparams(("arbitrary", "arbitrary", "arbitrary")),
        name="nsa_window",
    )(proj, proj, proj, proj, proj, proj, proj, wb)


def _softmax_steps(ss, vt, states):
    n = range(len(ss))
    m_new = [jnp.maximum(states[u][0], jnp.max(ss[u], axis=0, keepdims=True)) for u in n]
    alpha = [jnp.exp2(states[u][0] - m_new[u]) for u in n]
    p = [jnp.exp2(ss[u] - m_new[u]) for u in n]
    l = [alpha[u] * states[u][1] + jnp.sum(p[u], axis=0, keepdims=True) for u in n]
    vts = vt if isinstance(vt, (list, tuple)) else [vt] * len(ss)
    pv = [_dot(vts[u], p[u].astype(BF16)) for u in n]
    return [(m_new[u], l[u], alpha[u] * states[u][2] + pv[u]) for u in n]


def _sel_kernel(q0, q1, q2, q3, k_ref, v_ref, sel_ref, gate_ref, nd_ref, npv_ref, oc_ref, ow_ref, o_ref,
                kb_ref, vt_ref, qa_ref, kn_ref, m_ref, l_ref, acc_ref, *, seq):
    step = pl.program_id(2)
    cols = NSA_REP * QT
    spb = KEY_BLK // SLC_BLOCK
    subs = range(SEL_SUB)

    @pl.when(step == 0)
    def _():
        blk = lax.broadcasted_iota(jnp.int32, (KEY_BLK, LANE), 0) // SLC_BLOCK
        onehot = jnp.where(blk == lax.broadcasted_iota(jnp.int32, (KEY_BLK, LANE), 1), 1.0, 0.0).astype(BF16)

        def fill(i, kmax2):
            r0 = pl.multiple_of(i * KEY_BLK, KEY_BLK)
            kblk = (k_ref[0, pl.ds(r0, KEY_BLK), :] * SCALE_LOG2).astype(BF16)
            kb_ref[pl.ds(r0, KEY_BLK), 0:LANE] = kblk
            kb_ref[pl.ds(r0, KEY_BLK), LANE:2 * LANE] = onehot
            for c in range(SEL_SUB):
                vt_ref[i, :, c * QT:(c + 1) * QT] = v_ref[0, pl.ds(r0 + c * QT, QT), :].T.astype(BF16)
            n2 = jnp.sum(jnp.square(kblk.astype(F32)), axis=1, keepdims=True)
            return jnp.maximum(kmax2, jnp.max(n2, axis=0, keepdims=True))

        kmax2 = lax.fori_loop(0, seq // KEY_BLK, fill, jnp.zeros((1, 1), F32))
        kn_ref[...] = jnp.sqrt(kmax2)
        qa_ref[...] = jnp.zeros_like(qa_ref)

    qts = [_q_transposed((q0, q1, q2, q3), u) for u in subs]
    for u in subs:
        qa_ref[u, 0:LANE, :] = qts[u]
    bounds = [jnp.sqrt(jnp.sum(jnp.square(qts[u].astype(F32)), axis=0, keepdims=True)) * kn_ref[...]
              * BOUND_SLACK + 1.0 for u in subs]

    def block(kb):
        r0 = pl.multiple_of(kb * KEY_BLK, KEY_BLK)
        return kb_ref[pl.ds(r0, KEY_BLK), :], vt_ref[kb]

    pad_rows = jnp.zeros((SEL_ROWS - spb, cols), F32)

    def scores(kb, k, rows_of=lambda u: KEY_BLK):
        for u in subs:
            rows = sel_ref[0, 0, pl.ds(pl.multiple_of(kb * spb, spb), spb), u * QT:(u + 1) * QT]
            rows = jnp.concatenate([rows] * NSA_REP, axis=1)
            qa_ref[u, LANE:LANE + SEL_ROWS, :] = jnp.concatenate([rows, pad_rows], axis=0).astype(BF16)
        return [_dot(k[0:rows_of(u)], qa_ref[u]) for u in subs]

    seen = lambda u: (u + 1) * QT
    k, vt = block(step)
    ss = scores(step, k, seen)
    ss = [ss[u] + nd_ref[0, u, 0:seen(u), :] for u in subs]
    states = _softmax_steps(ss, [vt[:, 0:seen(u)] for u in subs], [_init_state(cols) for _ in subs])

    for u in subs:
        m_ref[u], l_ref[u], acc_ref[u] = states[u]

    def prev_scores():
        k, vt = block(step - 1)
        ss = scores(step - 1, k)
        ss[0] = ss[0] + npv_ref[0]
        return ss, vt

    n_old = jnp.maximum(step - 1, 0)
    bounds[0] = bounds[0] + jnp.max(npv_ref[0], axis=0, keepdims=True)
    m_fix = [jnp.maximum(states[u][0], bounds[u]) for u in subs]
    gap = jnp.max(jnp.concatenate([m_fix[u] - states[u][0] for u in subs], axis=1))
    fixed_ok = gap < MAX_REF_GAP

    @pl.when(fixed_ok)
    def _():
        for u in subs:
            alpha = jnp.exp2(m_ref[u] - m_fix[u])
            l_ref[u] = alpha * l_ref[u]
            acc_ref[u] = alpha * acc_ref[u]
            m_ref[u] = m_fix[u]

        def fixed_step(ss, vt):
            ps = [jnp.exp2(ss[u] - m_ref[u]) for u in subs]
            for u in subs:
                l_ref[u] = l_ref[u] + jnp.sum(ps[u], axis=0, keepdims=True)
            pv = [_dot(vt, ps[u].astype(BF16)) for u in subs]
            for u in subs:
                acc_ref[u] = acc_ref[u] + pv[u]

        @pl.when(step >= 1)
        def _():
            fixed_step(*prev_scores())

        def body(kb, carry):
            k, vt = block(kb)
            fixed_step(scores(kb, k), vt)
            return carry

        lax.fori_loop(0, n_old, body, 0)

    @pl.when(jnp.logical_not(fixed_ok))
    def _():
        def online_step(ss, vt):
            st = _softmax_steps(ss, vt, [(m_ref[u], l_ref[u], acc_ref[u]) for u in subs])
            for u in subs:
                m_ref[u], l_ref[u], acc_ref[u] = st[u]

        @pl.when(step >= 1)
        def _():
            online_step(*prev_scores())

        def body(kb, carry):
            k, vt = block(kb)
            online_step(scores(kb, k), vt)
            return carry

        lax.fori_loop(0, n_old, body, 0)

    for u in subs:
        _finish_t((m_ref[u], l_ref[u], acc_ref[u]), gate_ref, 1, o_ref, u, others=(oc_ref, ow_ref))


def _selected(proj, sel_t, nd, npv, o_c, o_w, batch, seq):
    nt = proj.shape[1]
    rows = SEL_SUB * QT
    nsteps = seq // rows
    cols = NSA_REP * QT
    slab = lambda cb0: pl.BlockSpec((1, seq, LANE), lambda b, g, t: (cb0 + g, b, 0))
    return pl.pallas_call(
        functools.partial(_sel_kernel, seq=seq),
        grid=(batch, NSA_GROUPS, nsteps),
        in_specs=_q_specs(nsteps, rows) + [
            slab(CB_KS), slab(CB_VS),
            pl.BlockSpec((1, 1, LANE, rows), lambda b, g, t: (b, g, 0, t)),
            pl.BlockSpec((1, rows, LANE), lambda b, g, t: (CB_GATE + g, b * nsteps + t, 0)),
            pl.BlockSpec((1, SEL_SUB, KEY_BLK, cols), lambda b, g, t: (g, 0, 0, 0)),
            pl.BlockSpec((1, KEY_BLK, cols), lambda b, g, t: (g, 0, 0)),
            pl.BlockSpec((rows, cols), lambda b, g, t: (b * nsteps + t, g)),
            pl.BlockSpec((rows, cols), lambda b, g, t: (b * nsteps + t, g))],
        out_specs=pl.BlockSpec((rows, cols), lambda b, g, t: (b * nsteps + t, g)),
        out_shape=jax.ShapeDtypeStruct((nt, NSA_HEADS * LANE), BF16),
        scratch_shapes=[pltpu.VMEM((seq, 2 * LANE), BF16),
                        pltpu.VMEM((seq // KEY_BLK, LANE, KEY_BLK), BF16),
                        pltpu.VMEM((SEL_SUB, 2 * LANE, cols), BF16),
                        pltpu.VMEM((1, 1), F32),
                        pltpu.VMEM((SEL_SUB, 1, cols), F32), pltpu.VMEM((SEL_SUB, 1, cols), F32),
                        pltpu.VMEM((SEL_SUB, LANE, cols), F32)],
        compiler_params=_cparams(("arbitrary", "arbitrary", "arbitrary")),
        name="nsa_selected",
    )(proj, proj, proj, proj, proj, proj, sel_t, proj, nd, npv, o_c, o_w)


def _merge_kernel(oh_ref, on_ref, mgh_ref, mgn_ref, x_ref, mod_ref,
                  wh_ref, wn_ref, wo_ref, g_ref, b_ref, wr_ref, br_ref,
                  x1_ref, h2_ref, lg_ref):
    nblk = D_MODEL // LANE
    tm = x_ref.shape[0]
    halves = [slice(s * tm // MERGE_SPLIT, (s + 1) * tm // MERGE_SPLIT) for s in range(MERGE_SPLIT)]
    a_h = [_dot(oh_ref[rs, :], wh_ref[...]) for rs in halves]
    a_n = [_dot(on_ref[rs, :], wn_ref[...]) for rs in halves]
    gh = [jnp.concatenate([mgh_ref[c, rs, :] for c in range(nblk)], axis=-1) for rs in halves]
    gn = [jnp.concatenate([mgn_ref[c, rs, :] for c in range(nblk)], axis=-1) for rs in halves]
    merged = [(jax.nn.sigmoid(gh[s]) * a_h[s] + jax.nn.sigmoid(gn[s]) * a_n[s]).astype(BF16)
              for s in range(MERGE_SPLIT)]
    y = [(1.0 + mod_ref[0, 2:3, :]) * _dot(merged[s], wo_ref[...]) for s in range(MERGE_SPLIT)]
    x1 = [_layer_norm(ALPHA * x_ref[rs, :] + y[s]) * g_ref[...] + b_ref[...] for s, rs in enumerate(halves)]
    h2 = [_layer_norm(x1[s]) * (1.0 + mod_ref[0, 4:5, :]) + mod_ref[0, 3:4, :] for s in range(MERGE_SPLIT)]
    h_hi = [h2[s].astype(BF16) for s in range(MERGE_SPLIT)]
    h_lo = [(h2[s] - h_hi[s].astype(F32)).astype(BF16) for s in range(MERGE_SPLIT)]
    lg = [_dot(h_hi[s], wr_ref[0]) + _dot(h_hi[s], wr_ref[1]) + _dot(h_lo[s], wr_ref[0]) + br_ref[...]
          for s in range(MERGE_SPLIT)]
    for s, rs in enumerate(halves):
        x1_ref[rs, :] = x1[s]
        _store_rows(h2_ref, h2[s], rs.start)
        lg_ref[rs, :] = lg[s]


def _merge(o_h, o_n, proj, x2, mod, w_h, w_n, w_o, ln_g, ln_b, w_r, b_r, seq):
    nt, d = x2.shape
    tm = min(MERGE_ROWS, seq)
    nblk = d // LANE
    row = lambda w: pl.BlockSpec((tm, w), lambda i: (i, 0))
    full = lambda a: pl.BlockSpec(a.shape, lambda i: (0,) * a.ndim)
    return pl.pallas_call(
        _merge_kernel,
        grid=(nt // tm,),
        in_specs=[row(d), row(d),
                  pl.BlockSpec((nblk, tm, LANE), lambda i: (CB_MGH // nblk, i, 0)),
                  pl.BlockSpec((nblk, tm, LANE), lambda i: (CB_MGN // nblk, i, 0)),
                  row(d),
                  pl.BlockSpec((1, 6, d), lambda i: (i * tm // seq, 0, 0)),
                  full(w_h), full(w_n), full(w_o), full(ln_g), full(ln_b), full(w_r), full(b_r)],
        out_specs=[row(d), pl.BlockSpec((tm * ROW_TILES, LANE), lambda i: (i, 0)), row(LANE)],
        out_shape=[jax.ShapeDtypeStruct((nt, d), F32),
                   jax.ShapeDtypeStruct((nt * ROW_TILES, LANE), F32),
                   jax.ShapeDtypeStruct((nt, LANE), F32)],
        compiler_params=_cparams(("arbitrary",)),
        name="merge_outproj",
    )(o_h, o_n, proj, proj, x2, mod, w_h, w_n, w_o, ln_g, ln_b, w_r, b_r)


def _route_kernel(lg_ref, rec_ref, cnt_ref, rect_ref, carry_ref, *, tm):
    @pl.when(pl.program_id(0) == 0)
    def _():
        carry_ref[...] = jnp.zeros_like(carry_ref)

    lg = lg_ref[...]
    lane = lax.broadcasted_iota(jnp.int32, (tm, LANE), 1).astype(F32)
    far = float(LANE)
    gmask = lane < N_GROUPS
    gl = jnp.where(gmask, lg, -jnp.inf)
    gmax = jnp.max(gl, axis=-1, keepdims=True)
    gsum = jnp.sum(jnp.where(gmask, jnp.exp(gl - gmax), 0.0), axis=-1, keepdims=True)
    grp_p = 1.0 / gsum
    gidx = jnp.min(jnp.where(gl == gmax, lane, far), axis=-1, keepdims=True)
    lo = N_GROUPS + EXP_PER_GROUP * gidx
    emask = (lane >= lo) & (lane < lo + EXP_PER_GROUP)
    el = jnp.where(emask, lg, -jnp.inf)
    m1 = jnp.max(el, axis=-1, keepdims=True)
    i1 = jnp.min(jnp.where(el == m1, lane, far), axis=-1, keepdims=True)
    el2 = jnp.where(lane == i1, -jnp.inf, el)
    m2 = jnp.max(el2, axis=-1, keepdims=True)
    i2 = jnp.min(jnp.where(emask & (lane != i1) & (el2 == m2), lane, far), axis=-1, keepdims=True)
    e = jnp.exp(m2 - m1)
    w0 = grp_p / (1.0 + e)
    w1 = grp_p * e / (1.0 + e)

    oh0 = lane == i1
    oh1 = lane == i2
    f0 = jnp.where(oh0, 1.0, 0.0)
    f1 = jnp.where(oh1, 1.0, 0.0)
    ri = lax.broadcasted_iota(jnp.int32, (tm, tm), 0)
    ci = lax.broadcasted_iota(jnp.int32, (tm, tm), 1)
    before = jnp.where(ci < ri, 1.0, 0.0).astype(BF16)
    cum0 = _dot(before, f0.astype(BF16))
    cum1 = _dot(before, f1.astype(BF16))
    tot0 = jnp.sum(f0, axis=0, keepdims=True)
    tot1 = jnp.sum(f1, axis=0, keepdims=True)
    carry = carry_ref[...]
    rank0 = jnp.sum(jnp.where(oh0, carry + cum0, 0.0), axis=-1, keepdims=True)
    rank1 = jnp.sum(jnp.where(oh1, carry + tot0 + cum1, 0.0), axis=-1, keepdims=True)
    carry = carry + tot0 + tot1
    carry_ref[...] = carry
    cnt_ref[...] = carry

    rec = jnp.where(lane == 0, i1 - N_GROUPS, 0.0)
    rec = jnp.where(lane == 1, i2 - N_GROUPS, rec)
    rec = jnp.where(lane == 2, w0, rec)
    rec = jnp.where(lane == 3, w1, rec)
    rec = jnp.where(lane == 4, rank0, rec)
    rec = jnp.where(lane == 5, rank1, rec)
    rec_ref[...] = rec
    rect_ref[...] = jnp.concatenate([rec[c * LANE:(c + 1) * LANE, :].T[0:8, :] for c in range(tm // LANE)], axis=1)


def _route(logits):
    nt = logits.shape[0]
    tm = min(ROUTE_ROWS, nt)
    return pl.pallas_call(
        functools.partial(_route_kernel, tm=tm),
        grid=(nt // tm,),
        in_specs=[pl.BlockSpec((tm, LANE), lambda i: (i, 0))],
        out_specs=[pl.BlockSpec((tm, LANE), lambda i: (i, 0)),
                   pl.BlockSpec((1, LANE), lambda i: (0, 0)),
                   pl.BlockSpec((8, tm), lambda i: (0, i))],
        out_shape=[jax.ShapeDtypeStruct((nt, LANE), F32),
                   jax.ShapeDtypeStruct((1, LANE), F32),
                   jax.ShapeDtypeStruct((8, nt), F32)],
        scratch_shapes=[pltpu.VMEM((1, LANE), F32)],
        compiler_params=_cparams(("arbitrary",)),
        name="moe_route",
    )(logits)


def _row_copy(src, dst, sem):
    return pltpu.make_async_copy(src, dst, sem)


def _tile_of(r):
    return pl.ds(pl.multiple_of(r * ROW_TILES, ROW_TILES), ROW_TILES)


def _load_rows(ref, n, lead=()):
    return jnp.concatenate([ref[lead + (pl.ds(c, n, stride=ROW_TILES), slice(None))]
                            for c in range(ROW_TILES)], axis=1)


def _store_rows(ref, val, row0=0):
    n = val.shape[0]
    for c in range(ROW_TILES):
        ref[pl.ds(row0 * ROW_TILES + c, n, stride=ROW_TILES), :] = val[:, c * LANE:(c + 1) * LANE]


def _dispatch_kernel(dest0_ref, dest1_ref, zb_ref, h_ref, xp_ref, z_ref, sem, zsem, *, tm):
    step = pl.program_id(0)
    base = step * tm
    blk = MOE_ROWS * ROW_TILES

    @pl.when(step == 0)
    def _():
        z_ref[...] = jnp.zeros_like(z_ref)

        def zero_copy(j):
            b = jnp.maximum(zb_ref[j], 0)
            return _row_copy(z_ref, xp_ref.at[pl.ds(pl.multiple_of(b * blk, blk), blk), :], zsem)

        def start(j, carry):
            @pl.when(zb_ref[j] >= 0)
            def _():
                zero_copy(j).start()
            return carry

        def wait(j, carry):
            @pl.when(zb_ref[j] >= 0)
            def _():
                zero_copy(j).wait()
            return carry

        lax.fori_loop(0, 2 * N_EXPERTS, start, 0)
        lax.fori_loop(0, 2 * N_EXPERTS, wait, 0)

    def issue(r, carry):
        for dest_ref in (dest0_ref, dest1_ref):
            d = dest_ref[base + r]
            _row_copy(h_ref.at[_tile_of(r), :], xp_ref.at[_tile_of(d), :], sem).start()
        return carry

    lax.fori_loop(0, tm, issue, 0, unroll=ROW_DMA_UNROLL)
    for _ in range(2):
        _row_copy(h_ref, xp_ref.at[pl.ds(0, tm * ROW_TILES), :], sem).wait()


def _dispatch(dest0, dest1, zero_blocks, h2, n_blocks):
    nt = h2.shape[0] // ROW_TILES
    tm = min(DISPATCH_TOKENS, nt)
    return pl.pallas_call(
        functools.partial(_dispatch_kernel, tm=tm),
        grid_spec=pltpu.PrefetchScalarGridSpec(
            num_scalar_prefetch=3,
            grid=(nt // tm,),
            in_specs=[pl.BlockSpec((tm * ROW_TILES, LANE), lambda i, *_: (i, 0))],
            out_specs=pl.BlockSpec(memory_space=pl.ANY),
            scratch_shapes=[pltpu.VMEM((MOE_ROWS * ROW_TILES, LANE), F32),
                            pltpu.SemaphoreType.DMA(()), pltpu.SemaphoreType.DMA(())]),
        out_shape=jax.ShapeDtypeStruct((n_blocks * MOE_ROWS * ROW_TILES, LANE), F32),
        compiler_params=_cparams(("arbitrary",)),
        name="moe_dispatch",
    )(dest0, dest1, zero_blocks, h2)


def _expert_kernel(be_ref, nu_ref, x_ref, w1_ref, w3_ref, w2_ref, y_ref, w1b_ref, w3b_ref, w2b_ref):
    i = pl.program_id(0)

    @pl.when((i < nu_ref[0]) & ((i == 0) | (be_ref[i] != be_ref[jnp.maximum(i - 1, 0)])))
    def _():
        w1b_ref[...] = w1_ref[0].astype(BF16)
        w3b_ref[...] = w3_ref[0].astype(BF16)
        w2b_ref[...] = w2_ref[0].astype(BF16)

    @pl.when(i < nu_ref[0])
    def _():
        xb = _load_rows(x_ref, MOE_ROWS).astype(BF16)
        a = _dot(xb, w1b_ref[...])
        b = _dot(xb, w3b_ref[...])
        hmid = (a * jax.nn.sigmoid(a) * b).astype(BF16)
        _store_rows(y_ref, _dot(hmid, w2b_ref[...]))

    @pl.when(i >= nu_ref[0])
    def _():
        y_ref[...] = jnp.zeros_like(y_ref)


def _experts(block_expert, n_used, x_pad, w1, w3, w2):
    d, de = w1.shape[1], w1.shape[2]
    nb = x_pad.shape[0] // (MOE_ROWS * ROW_TILES)
    return pl.pallas_call(
        _expert_kernel,
        grid_spec=pltpu.PrefetchScalarGridSpec(
            num_scalar_prefetch=2,
            grid=(nb,),
            in_specs=[pl.BlockSpec((MOE_ROWS * ROW_TILES, LANE), lambda i, be, nu: (i, 0)),
                      pl.BlockSpec((1, d, de), lambda i, be, nu: (be[i], 0, 0)),
                      pl.BlockSpec((1, d, de), lambda i, be, nu: (be[i], 0, 0)),
                      pl.BlockSpec((1, de, d), lambda i, be, nu: (be[i], 0, 0))],
            out_specs=pl.BlockSpec((MOE_ROWS * ROW_TILES, LANE), lambda i, be, nu: (i, 0)),
            scratch_shapes=[pltpu.VMEM((d, de), BF16), pltpu.VMEM((d, de), BF16), pltpu.VMEM((de, d), BF16)]),
        out_shape=jax.ShapeDtypeStruct(x_pad.shape, F32),
        compiler_params=_cparams(("arbitrary",)),
        name="moe_experts",
    )(block_expert, n_used, x_pad, w1, w3, w2)


def _combine_kernel(dest0_ref, dest1_ref, yp_ref, rec_ref, x1_ref, mod_ref, g_ref, b_ref, o_ref,
                    buf_ref, sem, *, tm):
    step = pl.program_id(0)
    slot = step % 2

    def fetch(tile, to_slot):
        def issue(r, carry):
            for k, dest_ref in enumerate((dest0_ref, dest1_ref)):
                d = dest_ref[tile * tm + r]
                _row_copy(yp_ref.at[_tile_of(d), :], buf_ref.at[to_slot, k, _tile_of(r), :],
                          sem.at[to_slot]).start()
            return carry

        lax.fori_loop(0, tm, issue, 0, unroll=ROW_DMA_UNROLL)

    @pl.when(step == 0)
    def _():
        fetch(0, 0)

    @pl.when(step + 1 < pl.num_programs(0))
    def _():
        fetch(step + 1, 1 - slot)

    for k in range(2):
        _row_copy(yp_ref.at[pl.ds(0, tm * ROW_TILES), :], buf_ref.at[slot, k], sem.at[slot]).wait()

    rec = rec_ref[...]
    y = rec[:, 2:3] * _load_rows(buf_ref, tm, (slot, 0)) + rec[:, 3:4] * _load_rows(buf_ref, tm, (slot, 1))
    y = (1.0 + mod_ref[0, 5:6, :]) * y
    o_ref[...] = _layer_norm(ALPHA * x1_ref[...] + y) * g_ref[...] + b_ref[...]


def _combine(dest0, dest1, y_pad, rec, x1, mod, ln_g, ln_b, seq):
    nt, d = x1.shape
    tm = min(COMBINE_TOKENS, seq)
    return pl.pallas_call(
        functools.partial(_combine_kernel, tm=tm),
        grid_spec=pltpu.PrefetchScalarGridSpec(
            num_scalar_prefetch=2,
            grid=(nt // tm,),
            in_specs=[pl.BlockSpec(memory_space=pl.ANY),
                      pl.BlockSpec((tm, LANE), lambda i, *_: (i, 0)),
                      pl.BlockSpec((tm, d), lambda i, *_: (i, 0)),
                      pl.BlockSpec((1, 6, d), lambda i, *_: (i * tm // seq, 0, 0)),
                      pl.BlockSpec((1, d), lambda i, *_: (0, 0)),
                      pl.BlockSpec((1, d), lambda i, *_: (0, 0))],
            out_specs=pl.BlockSpec((tm, d), lambda i, *_: (i, 0)),
            scratch_shapes=[pltpu.VMEM((2, 2, tm * ROW_TILES, LANE), F32), pltpu.SemaphoreType.DMA((2,))]),
        out_shape=jax.ShapeDtypeStruct((nt, d), F32),
        compiler_params=_cparams(("arbitrary",)),
        name="moe_combine",
    )(dest0, dest1, y_pad, rec, x1, mod, ln_g, ln_b)


def _rel_bucket(dist):
    n = jnp.maximum(dist, 0)
    max_exact = REL_BUCKETS // 2
    nf = jnp.maximum(n, 1).astype(F32)
    large = max_exact + (jnp.log(nf / max_exact) / math.log(REL_MAX_DIST / max_exact)
                         * (REL_BUCKETS - max_exact)).astype(jnp.int32)
    large = jnp.minimum(large, REL_BUCKETS - 1)
    return jnp.where(n < max_exact, n, large)


def _bias_tables(rel_bias, seq):
    bucket_onehot = (_rel_bucket(jnp.arange(LANE))[:, None] == jnp.arange(REL_BUCKETS)).astype(F32)
    tab_d = jnp.einsum('db,hb->hd', bucket_onehot, rel_bias,
                       precision=lax.Precision.HIGHEST)
    tok = np.arange(QT)[None, :]
    key = np.arange(LANE)[:, None]
    far = tab_d[:, LANE - 1]
    cols = NSA_REP * QT

    def transposed(dist):
        idx = jnp.asarray(np.clip(dist, 0, LANE - 1).astype(np.int32))
        onehot = (idx[..., None] == jnp.arange(LANE, dtype=jnp.int32)).astype(F32)
        t = jnp.einsum('ijd,hd->hij', onehot, tab_d, precision=lax.Precision.HIGHEST)
        t = (t - far[:, None, None]) * LOG2E
        t = t.reshape(NSA_GROUPS, NSA_REP, LANE, QT).transpose(0, 2, 1, 3)
        return t.reshape(NSA_GROUPS, LANE, cols)

    t0t = transposed(tok - key)
    t1t = transposed(tok - key + QT)

    ns = seq // CMP_STRIDE
    d_c = tok - CMP_STRIDE * key + (CMP_STRIDE * CMP_PAD - (CMP_BLOCK - 1))
    seen = np.tile(d_c >= 0, (1, NSA_REP))
    recent = jnp.where(seen[None], transposed(d_c), NEG)
    tt = jnp.concatenate([jnp.zeros((NSA_GROUPS, ns, cols), F32), recent,
                          jnp.full((NSA_GROUPS, ns, cols), NEG, F32)], axis=1)

    rho = np.arange(WINDOW + QT)[:, None]
    tok_w = np.tile(np.arange(QT), NSA_REP)[None, :]
    band = (rho > tok_w) & (rho <= tok_w + WINDOW)
    rows = jnp.concatenate([jnp.zeros((NSA_GROUPS, WINDOW - QT, NSA_REP * QT), F32), t1t, t0t], axis=1)
    wb = jnp.where(band[None], rows, NEG)

    zeros = lambda n: jnp.zeros((NSA_GROUPS, n * QT, cols), F32)
    negs = lambda n: jnp.full((NSA_GROUPS, n * QT, cols), NEG, F32)
    diag = jnp.where(np.tile(tok >= key, (1, NSA_REP))[None], t0t, NEG)
    nd = jnp.stack([jnp.concatenate(([zeros(u - 1), t1t] if u else []) + [diag, negs(SEL_SUB - 1 - u)], axis=1)
                    for u in range(SEL_SUB)], axis=1)
    npv = jnp.concatenate([zeros(SEL_SUB - 1), t1t], axis=1)
    return nd, npv, wb, tt


def _overlap_matrix(seq):
    ns = seq // CMP_STRIDE
    nslc = seq // SLC_BLOCK
    ov = np.zeros((LANE, ns), np.float32)
    cs = np.arange(ns - 1) * CMP_STRIDE
    ss = np.arange(nslc) * SLC_BLOCK
    ov[:nslc, :ns - 1] = ((cs[None, :] < ss[:, None] + SLC_BLOCK) & (cs[None, :] + CMP_BLOCK > ss[:, None]))
    return jnp.asarray(ov, BF16)


def _reorder_cols(a):
    lead = a.shape[:-1]
    gate = a[..., MAIN_COLS:MAIN_COLS + GATE_COLS]
    per = GATE_COLS // NSA_GROUPS
    gate_blocks = []
    for g in range(NSA_GROUPS):
        gate_blocks.append(gate[..., g * per:(g + 1) * per])
        gate_blocks.append(jnp.zeros(lead + (LANE - per,), a.dtype))
    pad = jnp.zeros(lead + ((CB_MGH - CB_GATE - NSA_GROUPS) * LANE,), a.dtype)
    return jnp.concatenate([a[..., :MAIN_COLS]] + gate_blocks + [pad, a[..., MAIN_COLS + GATE_COLS:]], axis=-1)


def kernel(x, c, ada_w, ada_b, w_in, b_in, hg_lb_logits, hg_norm_w, cmp_pos_k, cmp_w1_k, cmp_b1_k, cmp_w2_k, cmp_pos_v, cmp_w1_v, cmp_b1_v, cmp_w2_v, rel_bias, w_br_hg, w_br_nsa, w_out, ln1_g, ln1_b, router_grp_w, router_grp_b, router_exp_w, router_exp_b, exp_w1, exp_w3, exp_w2, ln2_g, ln2_b):
    batch, seq, d = x.shape
    nt = batch * seq
    assert d == D_MODEL and seq % INPROJ_ROWS == 0 and seq // SLC_BLOCK <= LANE
    l = 0
    x2 = x.reshape(nt, d)

    c_pad = jnp.zeros((8, d), F32).at[:batch].set(c)
    mod = _adaln(c_pad, ada_w[l], ada_b[l][None])[:batch].reshape(batch, 6, d)

    proj = _inproj(x2, mod, _reorder_cols(w_in[l].astype(BF16)), _reorder_cols(b_in[l])[None], seq)

    lb_all = jnp.cumsum(jax.nn.softmax(hg_lb_logits.astype(F32), axis=0), axis=0)
    o_h = _hgrn(proj, lb_all[l][None], hg_norm_w[l][None], batch, seq)

    kvc = _compress(proj, jnp.stack([cmp_pos_k[l], cmp_pos_v[l]]),
                    jnp.stack([cmp_w1_k[l], cmp_w1_v[l]]).astype(BF16),
                    jnp.stack([cmp_b1_k[l], cmp_b1_v[l]])[:, None, :],
                    jnp.stack([cmp_w2_k[l], cmp_w2_v[l]]).astype(BF16), batch, seq)

    nd, npv, wb, tt = _bias_tables(rel_bias, seq)
    o_c, sel_t = _cmp_sel(proj, kvc, tt, _overlap_matrix(seq), batch, seq)
    o_w = _window(proj, wb, batch, seq)
    o_n = _selected(proj, sel_t, nd, npv, o_c, o_w, batch, seq)

    w_r = jnp.zeros((d, LANE), F32).at[:, :N_GROUPS].set(router_grp_w[l])
    w_r = w_r.at[:, N_GROUPS:N_GROUPS + N_EXPERTS].set(router_exp_w[l])
    w_r_hi = w_r.astype(BF16)
    w_r_lo = (w_r - w_r_hi.astype(F32)).astype(BF16)
    b_r = jnp.zeros((1, LANE), F32).at[0, :N_GROUPS].set(router_grp_b[l])
    b_r = b_r.at[0, N_GROUPS:N_GROUPS + N_EXPERTS].set(router_exp_b[l])
    x1, h2, logits = _merge(o_h, o_n, proj, x2, mod,
                            w_br_hg[l].astype(BF16), w_br_nsa[l].astype(BF16), w_out[l].astype(BF16),
                            ln1_g[l][None], ln1_b[l][None], jnp.stack([w_r_hi, w_r_lo]), b_r, seq)

    rec, cnt, rec_t = _route(logits)
    counts = cnt[0, N_GROUPS:N_GROUPS + N_EXPERTS].astype(jnp.int32)
    padded = (counts + MOE_ROWS - 1) // MOE_ROWS * MOE_ROWS
    pend = jnp.cumsum(padded)
    pstart = pend - padded
    n_assign = 2 * nt
    nb = n_assign // MOE_ROWS + N_EXPERTS
    slots = rec_t.astype(jnp.int32)
    expert_ids = jnp.arange(N_EXPERTS, dtype=jnp.int32)[:, None]
    slot_base = lambda e: jnp.sum(jnp.where(e[None, :] == expert_ids, pstart[:, None], 0), axis=0)
    dest0 = slot_base(slots[0]) + slots[4]
    dest1 = slot_base(slots[1]) + slots[5]
    block_start = jnp.arange(nb, dtype=jnp.int32) * MOE_ROWS
    block_expert = jnp.minimum(jnp.sum(pend[None, :] <= block_start[:, None], axis=1),
                               N_EXPERTS - 1).astype(jnp.int32)
    n_used = pend[-1] // MOE_ROWS
    spare = n_used + jnp.arange(N_EXPERTS, dtype=jnp.int32)
    zero_blocks = jnp.concatenate([jnp.where(padded > 0, pend // MOE_ROWS - 1, -1),
                                   jnp.where(spare < nb, spare, -1)]).astype(jnp.int32)

    x_pad = _dispatch(dest0, dest1, zero_blocks, h2, nb)
    y_pad = _experts(block_expert, n_used[None].astype(jnp.int32), x_pad, exp_w1[l], exp_w3[l], exp_w2[l])
    out = _combine(dest0, dest1, y_pad, rec, x1, mod, ln2_g[l][None], ln2_b[l][None], seq)
    return out.reshape(batch, seq, d)
```

```python
import functools
import math

import numpy as np
import jax
import jax.numpy as jnp
from jax import lax
from jax.experimental import pallas as pl
from jax.experimental.pallas import tpu as pltpu

F32 = jnp.float32
BF16 = jnp.bfloat16

D_MODEL = 1024
HG_HEADS = 8
HG_DK = 128
HG_DV = 128
HG_CHUNK = 32
HG_SUPER = 256
NSA_HEADS = 8
NSA_GROUPS = 2
NSA_REP = NSA_HEADS // NSA_GROUPS
NSA_DK = 128
CMP_BLOCK = 32
CMP_STRIDE = 16
SLC_BLOCK = 64
SLC_TOPK = 16
WINDOW = 512
FORCE_SCORE = 1e4
N_FORCED = 3
REL_BUCKETS = 32
REL_MAX_DIST = 128
N_GROUPS = 4
EXP_PER_GROUP = 8
N_EXPERTS = N_GROUPS * EXP_PER_GROUP
D_EXPERT = D_MODEL // 2
DEPTH = 1
ALPHA = (2 * DEPTH) ** 0.25

LANE = 128
QT = 128
NEG = -1e30
SCALE = NSA_DK ** -0.5
LOG2E = math.log2(math.e)
SCALE_LOG2 = SCALE * LOG2E
KEY_BLK = 512
BOUND_SLACK = 1.0 + 2.0 ** -10
MAX_REF_GAP = 64.0
SEL_ROWS = 16
SEL_SUB = KEY_BLK // QT
WIN_SUB = 8
CMP_SUB = 8
CMP_PAD = 120
VMEM_LIMIT = 56 * 1024 * 1024

CB_HQ, CB_HF, CB_HI, CB_HG = 0, 8, 16, 24
CB_NQ = 32
CB_KC, CB_VC, CB_KS, CB_VS, CB_KW, CB_VW = 40, 42, 44, 46, 48, 50
CB_GATE = 52
CB_MGH, CB_MGN = 56, 64
NCB = 72
MAIN_COLS = 52 * LANE
GATE_COLS = 3 * NSA_HEADS

INPROJ_ROWS = 2048
INPROJ_COL_BLOCKS = 8
HGRN_ROWS = 1024
MERGE_ROWS = 512
ROUTE_ROWS = 512
DISPATCH_TOKENS = 2048
COMBINE_TOKENS = 256
MERGE_SPLIT = 2
ROW_TILES = D_MODEL // LANE
MOE_ROWS = 512
ROW_DMA_UNROLL = 8


def _cparams(sem):
    return pltpu.CompilerParams(dimension_semantics=sem, vmem_limit_bytes=VMEM_LIMIT)


def _dot(a, b):
    return jnp.dot(a, b, preferred_element_type=F32)


def _dot_nt(a, b):
    return lax.dot_general(a, b, (((1,), (1,)), ((), ())), preferred_element_type=F32)


def _dot_tn(a, b):
    return lax.dot_general(a, b, (((0,), (0,)), ((), ())), preferred_element_type=F32)


def _split3(x):
    hi = x.astype(BF16)
    r = x - hi.astype(F32)
    mid = r.astype(BF16)
    lo = (r - mid.astype(F32)).astype(BF16)
    return hi, mid, lo


def _dot01(m01, x):
    hi, mid, lo = _split3(x)
    return _dot(m01, hi) + _dot(m01, mid) + _dot(m01, lo)


def _layer_norm(x, eps=1e-5):
    mu = jnp.mean(x, axis=-1, keepdims=True)
    xc = x - mu
    var = jnp.mean(xc * xc, axis=-1, keepdims=True)
    return xc * lax.rsqrt(var + eps)


def _adaln_kernel(c_ref, w_ref, b_ref, o_ref):
    c = c_ref[...]
    ca = c * jax.nn.sigmoid(c)
    o_ref[...] = jnp.dot(ca, w_ref[...], precision=lax.Precision.HIGHEST,
                         preferred_element_type=F32) + b_ref[...]


def _adaln(c_pad, w, b):
    rows, d = c_pad.shape
    n = w.shape[1]
    return pl.pallas_call(
        _adaln_kernel,
        grid=(n // d,),
        in_specs=[pl.BlockSpec((rows, d), lambda j: (0, 0)),
                  pl.BlockSpec((d, d), lambda j: (0, j)),
                  pl.BlockSpec((1, d), lambda j: (0, j))],
        out_specs=pl.BlockSpec((rows, d), lambda j: (0, j)),
        out_shape=jax.ShapeDtypeStruct((rows, n), F32),
        compiler_params=_cparams(("arbitrary",)),
        name="adaln",
    )(c_pad, w, b)


def _inproj_kernel(x_ref, mod_ref, w_ref, b_ref, o_ref, hn_ref, *, ncb_tile):
    @pl.when(pl.program_id(1) == 0)
    def _():
        hn = _layer_norm(x_ref[...])
        sh = mod_ref[0, 0:1, :]
        sc = mod_ref[0, 1:2, :]
        hn_ref[...] = (hn * (1.0 + sc) + sh).astype(BF16)

    res = _dot(hn_ref[...], w_ref[...]) + b_ref[...]
    for c in range(ncb_tile):
        o_ref[c] = res[:, c * LANE:(c + 1) * LANE]


def _inproj(x2, mod, w, b, seq):
    nt, d = x2.shape
    tm = min(INPROJ_ROWS, seq)
    ncb_tile = INPROJ_COL_BLOCKS
    tn = ncb_tile * LANE
    return pl.pallas_call(
        functools.partial(_inproj_kernel, ncb_tile=ncb_tile),
        grid=(nt // tm, NCB // ncb_tile),
        in_specs=[pl.BlockSpec((tm, d), lambda i, j: (i, 0)),
                  pl.BlockSpec((1, 6, d), lambda i, j: (i * tm // seq, 0, 0)),
                  pl.BlockSpec((d, tn), lambda i, j: (0, j)),
                  pl.BlockSpec((1, tn), lambda i, j: (0, j))],
        out_specs=pl.BlockSpec((ncb_tile, tm, LANE), lambda i, j: (j, i, 0)),
        out_shape=jax.ShapeDtypeStruct((NCB, nt, LANE), F32),
        scratch_shapes=[pltpu.VMEM((tm, d), BF16)],
        compiler_params=_cparams(("arbitrary", "arbitrary")),
        name="inproj",
    )(x2, mod, w, b)


def _hgrn_kernel(q_ref, f_ref, v_ref, g_ref, lb_ref, nw_ref, o_ref, st_ref, *, rows):
    @pl.when(pl.program_id(1) == 0)
    def _():
        st_ref[...] = jnp.zeros_like(st_ref)

    sup = HG_SUPER
    ri = lax.broadcasted_iota(jnp.int32, (sup, sup), 0)
    ci = lax.broadcasted_iota(jnp.int32, (sup, sup), 1)
    same = (ri // HG_CHUNK) == (ci // HG_CHUNK)
    cum_m = jnp.where(same & (ci <= ri), 1.0, 0.0).astype(BF16)
    rt = lax.broadcasted_iota(jnp.int32, (LANE, LANE), 0)
    ct = lax.broadcasted_iota(jnp.int32, (LANE, LANE), 1)
    tril = ((rt // HG_CHUNK) == (ct // HG_CHUNK)) & (ct <= rt)
    per = sup // HG_CHUNK
    groups = [slice(g * LANE, (g + 1) * LANE) for g in range(sup // LANE)]

    heads = range(HG_HEADS)
    hs = [slice(h * LANE, (h + 1) * LANE) for h in heads]

    def wide(ref, r0):
        return jnp.concatenate([ref[h, pl.ds(r0, sup), :] for h in heads], axis=1)

    def body(i, carry):
        r0 = pl.multiple_of(i * sup, sup)
        lb = lb_ref[...]
        f = lb + (1.0 - lb) * jax.nn.sigmoid(wide(f_ref, r0))
        lf = jnp.log(f)
        k = 1.0 - f
        hi = lf.astype(BF16)
        lo = (lf - hi.astype(F32)).astype(BF16)
        b = _dot(cum_m, hi) + _dot(cum_m, lo)
        chunk = [slice(c * HG_CHUNK, (c + 1) * HG_CHUNK) for c in range(per)]
        dec = [jnp.exp(b[(c + 1) * HG_CHUNK - 1:(c + 1) * HG_CHUNK, :]) for c in range(per)]
        dec_rows = jnp.concatenate([jnp.broadcast_to(dec[c], (HG_CHUNK, dec[c].shape[1])) for c in range(per)],
                                   axis=0)
        q_in = (wide(q_ref, r0) * jnp.exp(b)).astype(BF16)
        k_dec = k * jnp.exp(-b)
        k_in = k_dec.astype(BF16)
        k_end = (k_dec * dec_rows).astype(BF16)
        vb = wide(v_ref, r0).astype(BF16)

        att = [[jnp.where(tril, _dot_nt(q_in[g, hs[h]], k_in[g, hs[h]]), 0.0).astype(BF16) for g in groups]
               for h in heads]
        upd = [[_dot_tn(vb[chunk[c], hs[h]], k_end[chunk[c], hs[h]]) for c in range(per)] for h in heads]
        intra = [[_dot(att[h][n], vb[g, hs[h]]) for n, g in enumerate(groups)] for h in heads]

        st = [st_ref[h] for h in heads]
        inter = [[] for _ in heads]
        for c in range(per):
            for h in heads:
                inter[h].append(_dot_nt(q_in[chunk[c], hs[h]], st[h].astype(BF16)))
                st[h] = dec[c][:, hs[h]] * st[h] + upd[h][c]
        for h in heads:
            st_ref[h] = st[h]

        o = [jnp.concatenate(intra[h], axis=0) + jnp.concatenate(inter[h], axis=0) for h in heads]
        scale = [lax.rsqrt(jnp.mean(o[h] * o[h], axis=-1, keepdims=True) + 1e-6) for h in heads]
        on = jnp.concatenate([o[h] * scale[h] for h in heads], axis=1)
        on = on * nw_ref[...] * jax.nn.sigmoid(wide(g_ref, r0))
        o_ref[pl.ds(r0, sup), :] = on.astype(BF16)
        return carry

    lax.fori_loop(0, rows // sup, body, 0)


def _hgrn(proj, lb, nw, batch, seq):
    nt = proj.shape[1]
    tb = min(HGRN_ROWS, seq)
    nblk = seq // tb

    def slab(cb0):
        return pl.BlockSpec((HG_HEADS, tb, LANE), lambda b, t: (cb0 // HG_HEADS, b * nblk + t, 0))

    vec = pl.BlockSpec((1, HG_HEADS * LANE), lambda b, t: (0, 0))
    return pl.pallas_call(
        functools.partial(_hgrn_kernel, rows=tb),
        grid=(batch, nblk),
        in_specs=[slab(CB_HQ), slab(CB_HF), slab(CB_HI), slab(CB_HG), vec, vec],
        out_specs=pl.BlockSpec((tb, HG_HEADS * HG_DV), lambda b, t: (b * nblk + t, 0)),
        out_shape=jax.ShapeDtypeStruct((nt, HG_HEADS * HG_DV), BF16),
        scratch_shapes=[pltpu.VMEM((HG_HEADS, HG_DV, HG_DK), F32)],
        compiler_params=_cparams(("arbitrary", "arbitrary")),
        name="hgrn2",
    )(proj, proj, proj, proj, lb, nw)


def _compress_kernel(x_ref, pos_ref, w1_ref, b1_ref, w2_ref, o_ref, *, ns):
    p0 = jnp.zeros((ns, LANE), F32)
    p1 = jnp.zeros((ns, LANE), F32)
    for j in range(CMP_STRIDE):
        tok = x_ref[0, pl.ds(j, ns, stride=CMP_STRIDE), :]
        rows = slice(j * LANE, (j + 1) * LANE)
        late = slice((CMP_STRIDE + j) * LANE, (CMP_STRIDE + j + 1) * LANE)
        p0 = p0 + _dot((tok + pos_ref[0, j:j + 1, :]).astype(BF16), w1_ref[0, rows, :])
        p1 = p1 + _dot((tok + pos_ref[0, CMP_STRIDE + j:CMP_STRIDE + j + 1, :]).astype(BF16), w1_ref[0, late, :])
    h = p0 + pltpu.roll(p1, ns - 1, axis=0) + b1_ref[0]
    a = h * jax.nn.sigmoid(h)
    out = _dot(a.astype(BF16), w2_ref[0])
    row = lax.broadcasted_iota(jnp.int32, out.shape, 0)
    out = jnp.where(row < ns - 1, out, 0.0)
    o_ref[0, 0, 0] = out


def _compress(proj, pos, w1, b1, w2, batch, seq):
    ns = seq // CMP_STRIDE
    np_rows = ns
    width = CMP_STRIDE * LANE
    return pl.pallas_call(
        functools.partial(_compress_kernel, ns=ns),
        grid=(2, batch, NSA_GROUPS),
        in_specs=[pl.BlockSpec((1, seq, LANE), lambda s, b, g: (CB_KC + NSA_GROUPS * s + g, b, 0)),
                  pl.BlockSpec((1, CMP_BLOCK, LANE), lambda s, b, g: (s, 0, 0)),
                  pl.BlockSpec((1, 2 * width, LANE), lambda s, b, g: (s, 0, 0)),
                  pl.BlockSpec((1, 1, LANE), lambda s, b, g: (s, 0, 0)),
                  pl.BlockSpec((1, LANE, LANE), lambda s, b, g: (s, 0, 0))],
        out_specs=pl.BlockSpec((1, 1, 1, np_rows, LANE), lambda s, b, g: (s, b, g, 0, 0)),
        out_shape=jax.ShapeDtypeStruct((2, batch, NSA_GROUPS, np_rows, LANE), F32),
        compiler_params=_cparams(("arbitrary", "arbitrary", "arbitrary")),
        name="nsa_compress",
    )(proj, pos, w1, b1, w2)


def _cmp_sel_kernel(q0, q1, q2, q3, kc_ref, vc_ref, gate_ref, tt_ref, ov_ref, oc_ref, sel_ref,
                    kb_ref, vt_ref, imp_ref, *, np_rows):
    step = pl.program_id(2)
    cols = NSA_REP * QT
    ns = np_rows

    @pl.when(step == 0)
    def _():
        kb_ref[...] = (kc_ref[0, 0, 0] * SCALE_LOG2).astype(BF16)
        for c in range(np_rows // LANE):
            vt_ref[:, c * LANE:(c + 1) * LANE] = vc_ref[0, 0, 0, c * LANE:(c + 1) * LANE, :].T.astype(BF16)

    subs = range(CMP_SUB)
    tis = [step * CMP_SUB + u for u in subs]
    qts = [_q_transposed((q0, q1, q2, q3), u) for u in subs]

    def group_sum(p):
        tot = p[:, 0:QT]
        for r in range(1, NSA_REP):
            tot = tot + p[:, r * QT:(r + 1) * QT]
        return tot

    def attend(nrows):
        kb = kb_ref[0:nrows, :]
        bias = [tt_ref[0, pl.ds(pl.multiple_of(ns + CMP_PAD - tis[u] * (QT // CMP_STRIDE), 8), nrows), :]
                for u in subs]
        ss = [_dot(kb, qts[u]) + bias[u] for u in subs]
        ms = [jnp.max(ss[u], axis=0, keepdims=True) for u in subs]
        ps = [jnp.exp2(ss[u] - ms[u]) for u in subs]
        ls = [jnp.sum(ps[u], axis=0, keepdims=True) for u in subs]
        invs = [jnp.where(ms[u] > 0.5 * NEG, 1.0 / ls[u], 0.0) for u in subs]
        pn = [ps[u] * invs[u] for u in subs]
        vt = vt_ref[:, 0:nrows]
        os_ = [_dot(vt, pn[u].astype(BF16)) for u in subs]
        for u in subs:
            rows = slice(u * QT, (u + 1) * QT)
            gt = jax.nn.sigmoid(gate_ref[0, rows, :])
            for r in range(NSA_REP):
                oc_ref[rows, r * LANE:(r + 1) * LANE] = gt[:, 3 * r:3 * r + 1] * os_[u][:, r * QT:(r + 1) * QT].T
            nblk = nrows * CMP_STRIDE // SLC_BLOCK
            imp_ref[0:nblk, rows] = _dot01(ov_ref[0:nblk, 0:nrows], group_sum(pn[u]))

    width = CMP_SUB * QT

    def choose(nblk):
        assert FORCE_SCORE > NSA_REP and nblk >= SLC_TOPK
        jj = lax.broadcasted_iota(jnp.int32, (nblk, width), 0)
        tok = step * width + lax.broadcasted_iota(jnp.int32, (nblk, width), 1)
        cur = tok // SLC_BLOCK
        forced = (jj == 0) | (jj == cur) | (jj == cur - 1)
        score = jnp.where(forced, -jnp.inf, jnp.where(jj <= cur, imp_ref[0:nblk, :], -1.0))
        jf = jj.astype(F32)
        for _ in range(SLC_TOPK - N_FORCED):
            mval = jnp.max(score, axis=0, keepdims=True)
            first = jnp.min(jnp.where(score == mval, jf, float(LANE)), axis=0, keepdims=True)
            score = jnp.where(jf == first, -jnp.inf, score)
        sel_ref[0, 0, 0:nblk, :] = jnp.where(score == -jnp.inf, 0.0, NEG)
        if nblk < LANE:
            sel_ref[0, 0, nblk:LANE, :] = jnp.full((LANE - nblk, width), NEG, F32)

    visible = (step + 1) * CMP_SUB * (QT // CMP_STRIDE)
    ngroups = np_rows // LANE
    for g in range(1, ngroups + 1):
        upper = visible <= g * LANE if g < ngroups else True

        @pl.when((visible > (g - 1) * LANE) & upper)
        def _():
            attend(g * LANE)
            choose(g * LANE * CMP_STRIDE // SLC_BLOCK)


def _q_specs(nsteps, rows=QT):
    return [pl.BlockSpec((1, rows, LANE),
                         functools.partial(lambda b, g, t, r: (CB_NQ + NSA_REP * g + r, b * nsteps + t, 0), r=r))
            for r in range(NSA_REP)]


def _cmp_sel(proj, kvc, tt, ov_t, batch, seq):
    nt = proj.shape[1]
    rows = CMP_SUB * QT
    nsteps = seq // rows
    cols = NSA_REP * QT
    np_rows = kvc.shape[3]
    kv_spec = lambda s: pl.BlockSpec((1, 1, 1, np_rows, LANE), lambda b, g, t: (s, b, g, 0, 0))
    return pl.pallas_call(
        functools.partial(_cmp_sel_kernel, np_rows=np_rows),
        grid=(batch, NSA_GROUPS, nsteps),
        in_specs=_q_specs(nsteps, rows) + [
            kv_spec(0), kv_spec(1),
            pl.BlockSpec((1, rows, LANE), lambda b, g, t: (CB_GATE + g, b * nsteps + t, 0)),
            pl.BlockSpec((1, tt.shape[1], cols), lambda b, g, t: (g, 0, 0)),
            pl.BlockSpec((LANE, np_rows), lambda b, g, t: (0, 0))],
        out_specs=[pl.BlockSpec((rows, cols), lambda b, g, t: (b * nsteps + t, g)),
                   pl.BlockSpec((1, 1, LANE, rows), lambda b, g, t: (b, g, 0, t))],
        out_shape=[jax.ShapeDtypeStruct((nt, NSA_HEADS * LANE), F32),
                   jax.ShapeDtypeStruct((batch, NSA_GROUPS, LANE, seq), F32)],
        scratch_shapes=[pltpu.VMEM((np_rows, LANE), BF16), pltpu.VMEM((LANE, np_rows), BF16),
                        pltpu.VMEM((LANE, rows), F32)],
        compiler_params=_cparams(("arbitrary", "arbitrary", "arbitrary")),
        name="nsa_cmp_select",
    )(proj, proj, proj, proj, kvc, kvc, proj, tt, ov_t)


def _q_transposed(q_refs, sub=0):
    return jnp.concatenate([r[0, sub * QT:(sub + 1) * QT, :].T for r in q_refs], axis=1).astype(BF16)


def _finish_t(state, gate_ref, branch, o_ref, sub=0, others=()):
    _, l, acc = state
    o = acc / jnp.where(l == 0.0, 1.0, l)
    rows = slice(sub * QT, (sub + 1) * QT)
    gt = jax.nn.sigmoid(gate_ref[0, rows, :])
    for r in range(NSA_REP):
        col = 3 * r + branch
        lanes = slice(r * LANE, (r + 1) * LANE)
        val = gt[:, col:col + 1] * o[:, r * QT:(r + 1) * QT].T
        for other in others:
            val = val + other[rows, lanes]
        o_ref[rows, lanes] = val.astype(o_ref.dtype)


def _init_state(cols):
    return (jnp.full((1, cols), NEG, F32), jnp.zeros((1, cols), F32), jnp.zeros((LANE, cols), F32))


def _win_kernel(q0, q1, q2, q3, k_ref, v_ref, gate_ref, wb_ref, o_ref, kb_ref, vt_ref, *, seq):
    step = pl.program_id(2)
    cols = NSA_REP * QT
    wt = WINDOW // QT

    lane = lax.broadcasted_iota(jnp.int32, (QT, LANE), 1)
    pad_mark = jnp.where(lane == 0, 1.0, 0.0).astype(BF16)

    @pl.when(step == 0)
    def _():
        for i in range(wt):
            kb_ref[i * QT:(i + 1) * QT, 0:LANE] = jnp.zeros((QT, LANE), BF16)
            kb_ref[i * QT:(i + 1) * QT, LANE:2 * LANE] = pad_mark
            vt_ref[i] = jnp.zeros((LANE, QT), BF16)

        def fill(i, carry):
            r0 = pl.multiple_of(i * QT, QT)
            kb_ref[pl.ds(WINDOW + r0, QT), 0:LANE] = (k_ref[0, pl.ds(r0, QT), :] * SCALE_LOG2).astype(BF16)
            kb_ref[pl.ds(WINDOW + r0, QT), LANE:2 * LANE] = jnp.zeros((QT, LANE), BF16)
            vt_ref[wt + i] = v_ref[0, pl.ds(r0, QT), :].T.astype(BF16)
            return carry

        lax.fori_loop(0, seq // QT, fill, 0)

    subs = range(WIN_SUB)
    tis = [step * WIN_SUB + sub for sub in subs]
    row = lax.broadcasted_iota(jnp.int32, (LANE, cols), 0)
    pad_rows = jnp.where(row == 0, NEG, 0.0).astype(BF16)
    qts = [jnp.concatenate([_q_transposed((q0, q1, q2, q3), sub), pad_rows], axis=0) for sub in subs]
    ss = [_dot(kb_ref[pl.ds(pl.multiple_of(tis[u] * QT, QT), WINDOW + QT), :], qts[u]) + wb_ref[0]
          for u in subs]
    ms = [jnp.max(ss[u], axis=0, keepdims=True) for u in subs]
    ps = [jnp.exp2(ss[u] - ms[u]) for u in subs]
    ls = [jnp.sum(ps[u], axis=0, keepdims=True) for u in subs]
    pbs = [ps[u].astype(BF16) for u in subs]
    accs = None
    for d in range(0, wt + 1, 2):
        n = min(2, wt + 1 - d)
        vts = [jnp.concatenate([vt_ref[tis[u] + d + e] for e in range(n)], axis=1) for u in subs]
        part = [_dot(vts[u], pbs[u][d * QT:(d + n) * QT]) for u in subs]
        accs = part if accs is None else [accs[u] + part[u] for u in subs]
    for u in subs:
        _finish_t((ms[u], ls[u], accs[u]), gate_ref, 2, o_ref, u)


def _window(proj, wb, batch, seq):
    nt = proj.shape[1]
    rows = WIN_SUB * QT
    nqt = seq // rows
    cols = NSA_REP * QT
    slab = lambda cb0: pl.BlockSpec((1, seq, LANE), lambda b, g, t: (cb0 + g, b, 0))
    return pl.pallas_call(
        functools.partial(_win_kernel, seq=seq),
        grid=(batch, NSA_GROUPS, nqt),
        in_specs=_q_specs(nqt, rows) + [
            slab(CB_KW), slab(CB_VW),
            pl.BlockSpec((1, rows, LANE), lambda b, g, t: (CB_GATE + g, b * nqt + t, 0)),
            pl.BlockSpec((1, WINDOW + QT, cols), lambda b, g, t: (g, 0, 0))],
        out_specs=pl.BlockSpec((rows, cols), lambda b, g, t: (b * nqt + t, g)),
        out_shape=jax.ShapeDtypeStruct((nt, NSA_HEADS * LANE), F32),
        scratch_shapes=[pltpu.VMEM((seq + WINDOW, 2 * LANE), BF16),
                        pltpu.VMEM((seq // QT + WINDOW // QT, LANE, QT), BF16)],
        compiler_params=_cparams(("arbitrary", "arbitrary", "arbitrary")),
        name="nsa_window",
    )(proj, proj, proj, proj, proj, proj, proj, wb)


def _softmax_steps(ss, vt, states):
    n = range(len(ss))
    m_new = [jnp.maximum(states[u][0], jnp.max(ss[u], axis=0, keepdims=True)) for u in n]
    alpha = [jnp.exp2(states[u][0] - m_new[u]) for u in n]
    p = [jnp.exp2(ss[u] - m_new[u]) for u in n]
    l = [alpha[u] * states[u][1] + jnp.sum(p[u], axis=0, keepdims=True) for u in n]
    vts = vt if isinstance(vt, (list, tuple)) else [vt] * len(ss)
    pv = [_dot(vts[u], p[u].astype(BF16)) for u in n]
    return [(m_new[u], l[u], alpha[u] * states[u][2] + pv[u]) for u in n]


def _sel_kernel(q0, q1, q2, q3, k_ref, v_ref, sel_ref, gate_ref, nd_ref, npv_ref, oc_ref, ow_ref, o_ref,
                kb_ref, vt_ref, qa_ref, kn_ref, m_ref, l_ref, acc_ref, *, seq):
    step = pl.program_id(2)
    cols = NSA_REP * QT
    spb = KEY_BLK // SLC_BLOCK
    subs = range(SEL_SUB)

    @pl.when(step == 0)
    def _():
        blk = lax.broadcasted_iota(jnp.int32, (KEY_BLK, LANE), 0) // SLC_BLOCK
        onehot = jnp.where(blk == lax.broadcasted_iota(jnp.int32, (KEY_BLK, LANE), 1), 1.0, 0.0).astype(BF16)

        def fill(i, kmax2):
            r0 = pl.multiple_of(i * KEY_BLK, KEY_BLK)
            kblk = (k_ref[0, pl.ds(r0, KEY_BLK), :] * SCALE_LOG2).astype(BF16)
            kb_ref[pl.ds(r0, KEY_BLK), 0:LANE] = kblk
            kb_ref[pl.ds(r0, KEY_BLK), LANE:2 * LANE] = onehot
            for c in range(SEL_SUB):
                vt_ref[i, :, c * QT:(c + 1) * QT] = v_ref[0, pl.ds(r0 + c * QT, QT), :].T.astype(BF16)
            n2 = jnp.sum(jnp.square(kblk.astype(F32)), axis=1, keepdims=True)
            return jnp.maximum(kmax2, jnp.max(n2, axis=0, keepdims=True))

        kmax2 = lax.fori_loop(0, seq // KEY_BLK, fill, jnp.zeros((1, 1), F32))
        kn_ref[...] = jnp.sqrt(kmax2)
        qa_ref[...] = jnp.zeros_like(qa_ref)

    qts = [_q_transposed((q0, q1, q2, q3), u) for u in subs]
    for u in subs:
        qa_ref[u, 0:LANE, :] = qts[u]
    bounds = [jnp.sqrt(jnp.sum(jnp.square(qts[u].astype(F32)), axis=0, keepdims=True)) * kn_ref[...]
              * BOUND_SLACK + 1.0 for u in subs]

    def block(kb):
        r0 = pl.multiple_of(kb * KEY_BLK, KEY_BLK)
        return kb_ref[pl.ds(r0, KEY_BLK), :], vt_ref[kb]

    pad_rows = jnp.zeros((SEL_ROWS - spb, cols), F32)

    def scores(kb, k, rows_of=lambda u: KEY_BLK):
        for u in subs:
            rows = sel_ref[0, 0, pl.ds(pl.multiple_of(kb * spb, spb), spb), u * QT:(u + 1) * QT]
            rows = jnp.concatenate([rows] * NSA_REP, axis=1)
            qa_ref[u, LANE:LANE + SEL_ROWS, :] = jnp.concatenate([rows, pad_rows], axis=0).astype(BF16)
        return [_dot(k[0:rows_of(u)], qa_ref[u]) for u in subs]

    seen = lambda u: (u + 1) * QT
    k, vt = block(step)
    ss = scores(step, k, seen)
    ss = [ss[u] + nd_ref[0, u, 0:seen(u), :] for u in subs]
    states = _softmax_steps(ss, [vt[:, 0:seen(u)] for u in subs], [_init_state(cols) for _ in subs])

    for u in subs:
        m_ref[u], l_ref[u], acc_ref[u] = states[u]

    def prev_scores():
        k, vt = block(step - 1)
        ss = scores(step - 1, k)
        ss[0] = ss[0] + npv_ref[0]
        return ss, vt

    n_old = jnp.maximum(step - 1, 0)
    bounds[0] = bounds[0] + jnp.max(npv_ref[0], axis=0, keepdims=True)
    m_fix = [jnp.maximum(states[u][0], bounds[u]) for u in subs]
    gap = jnp.max(jnp.concatenate([m_fix[u] - states[u][0] for u in subs], axis=1))
    fixed_ok = gap < MAX_REF_GAP

    @pl.when(fixed_ok)
    def _():
        for u in subs:
            alpha = jnp.exp2(m_ref[u] - m_fix[u])
            l_ref[u] = alpha * l_ref[u]
            acc_ref[u] = alpha * acc_ref[u]
            m_ref[u] = m_fix[u]

        def fixed_step(ss, vt):
            ps = [jnp.exp2(ss[u] - m_ref[u]) for u in subs]
            for u in subs:
                l_ref[u] = l_ref[u] + jnp.sum(ps[u], axis=0, keepdims=True)
            pv = [_dot(vt, ps[u].astype(BF16)) for u in subs]
            for u in subs:
                acc_ref[u] = acc_ref[u] + pv[u]

        @pl.when(step >= 1)
        def _():
            fixed_step(*prev_scores())

        def body(kb, carry):
            k, vt = block(kb)
            fixed_step(scores(kb, k), vt)
            return carry

        lax.fori_loop(0, n_old, body, 0)

    @pl.when(jnp.logical_not(fixed_ok))
    def _():
        def online_step(ss, vt):
            st = _softmax_steps(ss, vt, [(m_ref[u], l_ref[u], acc_ref[u]) for u in subs])
            for u in subs:
                m_ref[u], l_ref[u], acc_ref[u] = st[u]

        @pl.when(step >= 1)
        def _():
            online_step(*prev_scores())

        def body(kb, carry):
            k, vt = block(kb)
            online_step(scores(kb, k), vt)
            return carry

        lax.fori_loop(0, n_old, body, 0)

    for u in subs:
        _finish_t((m_ref[u], l_ref[u], acc_ref[u]), gate_ref, 1, o_ref, u, others=(oc_ref, ow_ref))


def _selected(proj, sel_t, nd, npv, o_c, o_w, batch, seq):
    nt = proj.shape[1]
    rows = SEL_SUB * QT
    nsteps = seq // rows
    cols = NSA_REP * QT
    slab = lambda cb0: pl.BlockSpec((1, seq, LANE), lambda b, g, t: (cb0 + g, b, 0))
    return pl.pallas_call(
        functools.partial(_sel_kernel, seq=seq),
        grid=(batch, NSA_GROUPS, nsteps),
        in_specs=_q_specs(nsteps, rows) + [
            slab(CB_KS), slab(CB_VS),
            pl.BlockSpec((1, 1, LANE, rows), lambda b, g, t: (b, g, 0, t)),
            pl.BlockSpec((1, rows, LANE), lambda b, g, t: (CB_GATE + g, b * nsteps + t, 0)),
            pl.BlockSpec((1, SEL_SUB, KEY_BLK, cols), lambda b, g, t: (g, 0, 0, 0)),
            pl.BlockSpec((1, KEY_BLK, cols), lambda b, g, t: (g, 0, 0)),
            pl.BlockSpec((rows, cols), lambda b, g, t: (b * nsteps + t, g)),
            pl.BlockSpec((rows, cols), lambda b, g, t: (b * nsteps + t, g))],
        out_specs=pl.BlockSpec((rows, cols), lambda b, g, t: (b * nsteps + t, g)),
        out_shape=jax.ShapeDtypeStruct((nt, NSA_HEADS * LANE), BF16),
        scratch_shapes=[pltpu.VMEM((seq, 2 * LANE), BF16),
                        pltpu.VMEM((seq // KEY_BLK, LANE, KEY_BLK), BF16),
                        pltpu.VMEM((SEL_SUB, 2 * LANE, cols), BF16),
                        pltpu.VMEM((1, 1), F32),
                        pltpu.VMEM((SEL_SUB, 1, cols), F32), pltpu.VMEM((SEL_SUB, 1, cols), F32),
                        pltpu.VMEM((SEL_SUB, LANE, cols), F32)],
        compiler_params=_cparams(("arbitrary", "arbitrary", "arbitrary")),
        name="nsa_selected",
    )(proj, proj, proj, proj, proj, proj, sel_t, proj, nd, npv, o_c, o_w)


def _merge_kernel(oh_ref, on_ref, mgh_ref, mgn_ref, x_ref, mod_ref,
                  wh_ref, wn_ref, wo_ref, g_ref, b_ref, wr_ref, br_ref,
                  x1_ref, h2_ref, lg_ref):
    nblk = D_MODEL // LANE
    tm = x_ref.shape[0]
    halves = [slice(s * tm // MERGE_SPLIT, (s + 1) * tm // MERGE_SPLIT) for s in range(MERGE_SPLIT)]
    a_h = [_dot(oh_ref[rs, :], wh_ref[...]) for rs in halves]
    a_n = [_dot(on_ref[rs, :], wn_ref[...]) for rs in halves]
    gh = [jnp.concatenate([mgh_ref[c, rs, :] for c in range(nblk)], axis=-1) for rs in halves]
    gn = [jnp.concatenate([mgn_ref[c, rs, :] for c in range(nblk)], axis=-1) for rs in halves]
    merged = [(jax.nn.sigmoid(gh[s]) * a_h[s] + jax.nn.sigmoid(gn[s]) * a_n[s]).astype(BF16)
              for s in range(MERGE_SPLIT)]
    y = [(1.0 + mod_ref[0, 2:3, :]) * _dot(merged[s], wo_ref[...]) for s in range(MERGE_SPLIT)]
    x1 = [_layer_norm(ALPHA * x_ref[rs, :] + y[s]) * g_ref[...] + b_ref[...] for s, rs in enumerate(halves)]
    h2 = [_layer_norm(x1[s]) * (1.0 + mod_ref[0, 4:5, :]) + mod_ref[0, 3:4, :] for s in range(MERGE_SPLIT)]
    h_hi = [h2[s].astype(BF16) for s in range(MERGE_SPLIT)]
    h_lo = [(h2[s] - h_hi[s].astype(F32)).astype(BF16) for s in range(MERGE_SPLIT)]
    lg = [_dot(h_hi[s], wr_ref[0]) + _dot(h_hi[s], wr_ref[1]) + _dot(h_lo[s], wr_ref[0]) + br_ref[...]
          for s in range(MERGE_SPLIT)]
    for s, rs in enumerate(halves):
        x1_ref[rs, :] = x1[s]
        _store_rows(h2_ref, h2[s], rs.start)
        lg_ref[rs, :] = lg[s]


def _merge(o_h, o_n, proj, x2, mod, w_h, w_n, w_o, ln_g, ln_b, w_r, b_r, seq):
    nt, d = x2.shape
    tm = min(MERGE_ROWS, seq)
    nblk = d // LANE
    row = lambda w: pl.BlockSpec((tm, w), lambda i: (i, 0))
    full = lambda a: pl.BlockSpec(a.shape, lambda i: (0,) * a.ndim)
    return pl.pallas_call(
        _merge_kernel,
        grid=(nt // tm,),
        in_specs=[row(d), row(d),
                  pl.BlockSpec((nblk, tm, LANE), lambda i: (CB_MGH // nblk, i, 0)),
                  pl.BlockSpec((nblk, tm, LANE), lambda i: (CB_MGN // nblk, i, 0)),
                  row(d),
                  pl.BlockSpec((1, 6, d), lambda i: (i * tm // seq, 0, 0)),
                  full(w_h), full(w_n), full(w_o), full(ln_g), full(ln_b), full(w_r), full(b_r)],
        out_specs=[row(d), pl.BlockSpec((tm * ROW_TILES, LANE), lambda i: (i, 0)), row(LANE)],
        out_shape=[jax.ShapeDtypeStruct((nt, d), F32),
                   jax.ShapeDtypeStruct((nt * ROW_TILES, LANE), F32),
                   jax.ShapeDtypeStruct((nt, LANE), F32)],
        compiler_params=_cparams(("arbitrary",)),
        name="merge_outproj",
    )(o_h, o_n, proj, proj, x2, mod, w_h, w_n, w_o, ln_g, ln_b, w_r, b_r)


def _route_kernel(lg_ref, rec_ref, cnt_ref, rect_ref, carry_ref, *, tm):
    @pl.when(pl.program_id(0) == 0)
    def _():
        carry_ref[...] = jnp.zeros_like(carry_ref)

    lg = lg_ref[...]
    lane = lax.broadcasted_iota(jnp.int32, (tm, LANE), 1).astype(F32)
    far = float(LANE)
    gmask = lane < N_GROUPS
    gl = jnp.where(gmask, lg, -jnp.inf)
    gmax = jnp.max(gl, axis=-1, keepdims=True)
    gsum = jnp.sum(jnp.where(gmask, jnp.exp(gl - gmax), 0.0), axis=-1, keepdims=True)
    grp_p = 1.0 / gsum
    gidx = jnp.min(jnp.where(gl == gmax, lane, far), axis=-1, keepdims=True)
    lo = N_GROUPS + EXP_PER_GROUP * gidx
    emask = (lane >= lo) & (lane < lo + EXP_PER_GROUP)
    el = jnp.where(emask, lg, -jnp.inf)
    m1 = jnp.max(el, axis=-1, keepdims=True)
    i1 = jnp.min(jnp.where(el == m1, lane, far), axis=-1, keepdims=True)
    el2 = jnp.where(lane == i1, -jnp.inf, el)
    m2 = jnp.max(el2, axis=-1, keepdims=True)
    i2 = jnp.min(jnp.where(emask & (lane != i1) & (el2 == m2), lane, far), axis=-1, keepdims=True)
    e = jnp.exp(m2 - m1)
    w0 = grp_p / (1.0 + e)
    w1 = grp_p * e / (1.0 + e)

    oh0 = lane == i1
    oh1 = lane == i2
    f0 = jnp.where(oh0, 1.0, 0.0)
    f1 = jnp.where(oh1, 1.0, 0.0)
    ri = lax.broadcasted_iota(jnp.int32, (tm, tm), 0)
    ci = lax.broadcasted_iota(jnp.int32, (tm, tm), 1)
    before = jnp.where(ci < ri, 1.0, 0.0).astype(BF16)
    cum0 = _dot(before, f0.astype(BF16))
    cum1 = _dot(before, f1.astype(BF16))
    tot0 = jnp.sum(f0, axis=0, keepdims=True)
    tot1 = jnp.sum(f1, axis=0, keepdims=True)
    carry = carry_ref[...]
    rank0 = jnp.sum(jnp.where(oh0, carry + cum0, 0.0), axis=-1, keepdims=True)
    rank1 = jnp.sum(jnp.where(oh1, carry + tot0 + cum1, 0.0), axis=-1, keepdims=True)
    carry = carry + tot0 + tot1
    carry_ref[...] = carry
    cnt_ref[...] = carry

    rec = jnp.where(lane == 0, i1 - N_GROUPS, 0.0)
    rec = jnp.where(lane == 1, i2 - N_GROUPS, rec)
    rec = jnp.where(lane == 2, w0, rec)
    rec = jnp.where(lane == 3, w1, rec)
    rec = jnp.where(lane == 4, rank0, rec)
    rec = jnp.where(lane == 5, rank1, rec)
    rec_ref[...] = rec
    rect_ref[...] = jnp.concatenate([rec[c * LANE:(c + 1) * LANE, :].T[0:8, :] for c in range(tm // LANE)], axis=1)


def _route(logits):
    nt = logits.shape[0]
    tm = min(ROUTE_ROWS, nt)
    return pl.pallas_call(
        functools.partial(_route_kernel, tm=tm),
        grid=(nt // tm,),
        in_specs=[pl.BlockSpec((tm, LANE), lambda i: (i, 0))],
        out_specs=[pl.BlockSpec((tm, LANE), lambda i: (i, 0)),
                   pl.BlockSpec((1, LANE), lambda i: (0, 0)),
                   pl.BlockSpec((8, tm), lambda i: (0, i))],
        out_shape=[jax.ShapeDtypeStruct((nt, LANE), F32),
                   jax.ShapeDtypeStruct((1, LANE), F32),
                   jax.ShapeDtypeStruct((8, nt), F32)],
        scratch_shapes=[pltpu.VMEM((1, LANE), F32)],
        compiler_params=_cparams(("arbitrary",)),
        name="moe_route",
    )(logits)


def _row_copy(src, dst, sem):
    return pltpu.make_async_copy(src, dst, sem)


def _tile_of(r):
    return pl.ds(pl.multiple_of(r * ROW_TILES, ROW_TILES), ROW_TILES)


def _load_rows(ref, n, lead=()):
    return jnp.concatenate([ref[lead + (pl.ds(c, n, stride=ROW_TILES), slice(None))]
                            for c in range(ROW_TILES)], axis=1)


def _store_rows(ref, val, row0=0):
    n = val.shape[0]
    for c in range(ROW_TILES):
        ref[pl.ds(row0 * ROW_TILES + c, n, stride=ROW_TILES), :] = val[:, c * LANE:(c + 1) * LANE]


def _dispatch_kernel(dest0_ref, dest1_ref, zb_ref, h_ref, xp_ref, z_ref, sem, zsem, *, tm):
    step = pl.program_id(0)
    base = step * tm
    blk = MOE_ROWS * ROW_TILES

    @pl.when(step == 0)
    def _():
        z_ref[...] = jnp.zeros_like(z_ref)

        def zero_copy(j):
            b = jnp.maximum(zb_ref[j], 0)
            return _row_copy(z_ref, xp_ref.at[pl.ds(pl.multiple_of(b * blk, blk), blk), :], zsem)

        def start(j, carry):
            @pl.when(zb_ref[j] >= 0)
            def _():
                zero_copy(j).start()
            return carry

        def wait(j, carry):
            @pl.when(zb_ref[j] >= 0)
            def _():
                zero_copy(j).wait()
            return carry

        lax.fori_loop(0, 2 * N_EXPERTS, start, 0)
        lax.fori_loop(0, 2 * N_EXPERTS, wait, 0)

    def issue(r, carry):
        for dest_ref in (dest0_ref, dest1_ref):
            d = dest_ref[base + r]
            _row_copy(h_ref.at[_tile_of(r), :], xp_ref.at[_tile_of(d), :], sem).start()
        return carry

    lax.fori_loop(0, tm, issue, 0, unroll=ROW_DMA_UNROLL)
    for _ in range(2):
        _row_copy(h_ref, xp_ref.at[pl.ds(0, tm * ROW_TILES), :], sem).wait()


def _dispatch(dest0, dest1, zero_blocks, h2, n_blocks):
    nt = h2.shape[0] // ROW_TILES
    tm = min(DISPATCH_TOKENS, nt)
    return pl.pallas_call(
        functools.partial(_dispatch_kernel, tm=tm),
        grid_spec=pltpu.PrefetchScalarGridSpec(
            num_scalar_prefetch=3,
            grid=(nt // tm,),
            in_specs=[pl.BlockSpec((tm * ROW_TILES, LANE), lambda i, *_: (i, 0))],
            out_specs=pl.BlockSpec(memory_space=pl.ANY),
            scratch_shapes=[pltpu.VMEM((MOE_ROWS * ROW_TILES, LANE), F32),
                            pltpu.SemaphoreType.DMA(()), pltpu.SemaphoreType.DMA(())]),
        out_shape=jax.ShapeDtypeStruct((n_blocks * MOE_ROWS * ROW_TILES, LANE), F32),
        compiler_params=_cparams(("arbitrary",)),
        name="moe_dispatch",
    )(dest0, dest1, zero_blocks, h2)


def _expert_kernel(be_ref, nu_ref, x_ref, w1_ref, w3_ref, w2_ref, y_ref, w1b_ref, w3b_ref, w2b_ref):
    i = pl.program_id(0)

    @pl.when((i < nu_ref[0]) & ((i == 0) | (be_ref[i] != be_ref[jnp.maximum(i - 1, 0)])))
    def _():
        w1b_ref[...] = w1_ref[0].astype(BF16)
        w3b_ref[...] = w3_ref[0].astype(BF16)
        w2b_ref[...] = w2_ref[0].astype(BF16)

    @pl.when(i < nu_ref[0])
    def _():
        xb = _load_rows(x_ref, MOE_ROWS).astype(BF16)
        a = _dot(xb, w1b_ref[...])
        b = _dot(xb, w3b_ref[...])
        hmid = (a * jax.nn.sigmoid(a) * b).astype(BF16)
        _store_rows(y_ref, _dot(hmid, w2b_ref[...]))

    @pl.when(i >= nu_ref[0])
    def _():
        y_ref[...] = jnp.zeros_like(y_ref)


def _experts(block_expert, n_used, x_pad, w1, w3, w2):
    d, de = w1.shape[1], w1.shape[2]
    nb = x_pad.shape[0] // (MOE_ROWS * ROW_TILES)
    return pl.pallas_call(
        _expert_kernel,
        grid_spec=pltpu.PrefetchScalarGridSpec(
            num_scalar_prefetch=2,
            grid=(nb,),
            in_specs=[pl.BlockSpec((MOE_ROWS * ROW_TILES, LANE), lambda i, be, nu: (i, 0)),
                      pl.BlockSpec((1, d, de), lambda i, be, nu: (be[i], 0, 0)),
                      pl.BlockSpec((1, d, de), lambda i, be, nu: (be[i], 0, 0)),
                      pl.BlockSpec((1, de, d), lambda i, be, nu: (be[i], 0, 0))],
            out_specs=pl.BlockSpec((MOE_ROWS * ROW_TILES, LANE), lambda i, be, nu: (i, 0)),
            scratch_shapes=[pltpu.VMEM((d, de), BF16), pltpu.VMEM((d, de), BF16), pltpu.VMEM((de, d), BF16)]),
        out_shape=jax.ShapeDtypeStruct(x_pad.shape, F32),
        compiler_params=_cparams(("arbitrary",)),
        name="moe_experts",
    )(block_expert, n_used, x_pad, w1, w3, w2)


def _combine_kernel(dest0_ref, dest1_ref, yp_ref, rec_ref, x1_ref, mod_ref, g_ref, b_ref, o_ref,
                    buf_ref, sem, *, tm):
    step = pl.program_id(0)
    slot = step % 2

    def fetch(tile, to_slot):
        def issue(r, carry):
            for k, dest_ref in enumerate((dest0_ref, dest1_ref)):
                d = dest_ref[tile * tm + r]
                _row_copy(yp_ref.at[_tile_of(d), :], buf_ref.at[to_slot, k, _tile_of(r), :],
                          sem.at[to_slot]).start()
            return carry

        lax.fori_loop(0, tm, issue, 0, unroll=ROW_DMA_UNROLL)

    @pl.when(step == 0)
    def _():
        fetch(0, 0)

    @pl.when(step + 1 < pl.num_programs(0))
    def _():
        fetch(step + 1, 1 - slot)

    for k in range(2):
        _row_copy(yp_ref.at[pl.ds(0, tm * ROW_TILES), :], buf_ref.at[slot, k], sem.at[slot]).wait()

    rec = rec_ref[...]
    y = rec[:, 2:3] * _load_rows(buf_ref, tm, (slot, 0)) + rec[:, 3:4] * _load_rows(buf_ref, tm, (slot, 1))
    y = (1.0 + mod_ref[0, 5:6, :]) * y
    o_ref[...] = _layer_norm(ALPHA * x1_ref[...] + y) * g_ref[...] + b_ref[...]


def _combine(dest0, dest1, y_pad, rec, x1, mod, ln_g, ln_b, seq):
    nt, d = x1.shape
    tm = min(COMBINE_TOKENS, seq)
    return pl.pallas_call(
        functools.partial(_combine_kernel, tm=tm),
        grid_spec=pltpu.PrefetchScalarGridSpec(
            num_scalar_prefetch=2,
            grid=(nt // tm,),
            in_specs=[pl.BlockSpec(memory_space=pl.ANY),
                      pl.BlockSpec((tm, LANE), lambda i, *_: (i, 0)),
                      pl.BlockSpec((tm, d), lambda i, *_: (i, 0)),
                      pl.BlockSpec((1, 6, d), lambda i, *_: (i * tm // seq, 0, 0)),
                      pl.BlockSpec((1, d), lambda i, *_: (0, 0)),
                      pl.BlockSpec((1, d), lambda i, *_: (0, 0))],
            out_specs=pl.BlockSpec((tm, d), lambda i, *_: (i, 0)),
            scratch_shapes=[pltpu.VMEM((2, 2, tm * ROW_TILES, LANE), F32), pltpu.SemaphoreType.DMA((2,))]),
        out_shape=jax.ShapeDtypeStruct((nt, d), F32),
        compiler_params=_cparams(("arbitrary",)),
        name="moe_combine",
    )(dest0, dest1, y_pad, rec, x1, mod, ln_g, ln_b)


def _rel_bucket(dist):
    n = jnp.maximum(dist, 0)
    max_exact = REL_BUCKETS // 2
    nf = jnp.maximum(n, 1).astype(F32)
    large = max_exact + (jnp.log(nf / max_exact) / math.log(REL_MAX_DIST / max_exact)
                         * (REL_BUCKETS - max_exact)).astype(jnp.int32)
    large = jnp.minimum(large, REL_BUCKETS - 1)
    return jnp.where(n < max_exact, n, large)


def _bias_tables(rel_bias, seq):
    bucket_onehot = (_rel_bucket(jnp.arange(LANE))[:, None] == jnp.arange(REL_BUCKETS)).astype(F32)
    tab_d = jnp.einsum('db,hb->hd', bucket_onehot, rel_bias,
                       precision=lax.Precision.HIGHEST)
    tok = np.arange(QT)[None, :]
    key = np.arange(LANE)[:, None]
    far = tab_d[:, LANE - 1]
    cols = NSA_REP * QT

    def transposed(dist):
        idx = jnp.asarray(np.clip(dist, 0, LANE - 1).astype(np.int32))
        onehot = (idx[..., None] == jnp.arange(LANE, dtype=jnp.int32)).astype(F32)
        t = jnp.einsum('ijd,hd->hij', onehot, tab_d, precision=lax.Precision.HIGHEST)
        t = (t - far[:, None, None]) * LOG2E
        t = t.reshape(NSA_GROUPS, NSA_REP, LANE, QT).transpose(0, 2, 1, 3)
        return t.reshape(NSA_GROUPS, LANE, cols)

    t0t = transposed(tok - key)
    t1t = transposed(tok - key + QT)

    ns = seq // CMP_STRIDE
    d_c = tok - CMP_STRIDE * key + (CMP_STRIDE * CMP_PAD - (CMP_BLOCK - 1))
    seen = np.tile(d_c >= 0, (1, NSA_REP))
    recent = jnp.where(seen[None], transposed(d_c), NEG)
    tt = jnp.concatenate([jnp.zeros((NSA_GROUPS, ns, cols), F32), recent,
                          jnp.full((NSA_GROUPS, ns, cols), NEG, F32)], axis=1)

    rho = np.arange(WINDOW + QT)[:, None]
    tok_w = np.tile(np.arange(QT), NSA_REP)[None, :]
    band = (rho > tok_w) & (rho <= tok_w + WINDOW)
    rows = jnp.concatenate([jnp.zeros((NSA_GROUPS, WINDOW - QT, NSA_REP * QT), F32), t1t, t0t], axis=1)
    wb = jnp.where(band[None], rows, NEG)

    zeros = lambda n: jnp.zeros((NSA_GROUPS, n * QT, cols), F32)
    negs = lambda n: jnp.full((NSA_GROUPS, n * QT, cols), NEG, F32)
    diag = jnp.where(np.tile(tok >= key, (1, NSA_REP))[None], t0t, NEG)
    nd = jnp.stack([jnp.concatenate(([zeros(u - 1), t1t] if u else []) + [diag, negs(SEL_SUB - 1 - u)], axis=1)
                    for u in range(SEL_SUB)], axis=1)
    npv = jnp.concatenate([zeros(SEL_SUB - 1), t1t], axis=1)
    return nd, npv, wb, tt


def _overlap_matrix(seq):
    ns = seq // CMP_STRIDE
    nslc = seq // SLC_BLOCK
    ov = np.zeros((LANE, ns), np.float32)
    cs = np.arange(ns - 1) * CMP_STRIDE
    ss = np.arange(nslc) * SLC_BLOCK
    ov[:nslc, :ns - 1] = ((cs[None, :] < ss[:, None] + SLC_BLOCK) & (cs[None, :] + CMP_BLOCK > ss[:, None]))
    return jnp.asarray(ov, BF16)


def _reorder_cols(a):
    lead = a.shape[:-1]
    gate = a[..., MAIN_COLS:MAIN_COLS + GATE_COLS]
    per = GATE_COLS // NSA_GROUPS
    gate_blocks = []
    for g in range(NSA_GROUPS):
        gate_blocks.append(gate[..., g * per:(g + 1) * per])
        gate_blocks.append(jnp.zeros(lead + (LANE - per,), a.dtype))
    pad = jnp.zeros(lead + ((CB_MGH - CB_GATE - NSA_GROUPS) * LANE,), a.dtype)
    return jnp.concatenate([a[..., :MAIN_COLS]] + gate_blocks + [pad, a[..., MAIN_COLS + GATE_COLS:]], axis=-1)


def kernel(x, c, ada_w, ada_b, w_in, b_in, hg_lb_logits, hg_norm_w, cmp_pos_k, cmp_w1_k, cmp_b1_k, cmp_w2_k, cmp_pos_v, cmp_w1_v, cmp_b1_v, cmp_w2_v, rel_bias, w_br_hg, w_br_nsa, w_out, ln1_g, ln1_b, router_grp_w, router_grp_b, router_exp_w, router_exp_b, exp_w1, exp_w3, exp_w2, ln2_g, ln2_b):
    batch, seq, d = x.shape
    nt = batch * seq
    assert d == D_MODEL and seq % INPROJ_ROWS == 0 and seq // SLC_BLOCK <= LANE
    l = 0
    x2 = x.reshape(nt, d)

    c_pad = jnp.zeros((8, d), F32).at[:batch].set(c)
    mod = _adaln(c_pad, ada_w[l], ada_b[l][None])[:batch].reshape(batch, 6, d)

    proj = _inproj(x2, mod, _reorder_cols(w_in[l].astype(BF16)), _reorder_cols(b_in[l])[None], seq)

    lb_all = jnp.cumsum(jax.nn.softmax(hg_lb_logits.astype(F32), axis=0), axis=0)
    o_h = _hgrn(proj, lb_all[l][None], hg_norm_w[l][None], batch, seq)

    kvc = _compress(proj, jnp.stack([cmp_pos_k[l], cmp_pos_v[l]]),
                    jnp.stack([cmp_w1_k[l], cmp_w1_v[l]]).astype(BF16),
                    jnp.stack([cmp_b1_k[l], cmp_b1_v[l]])[:, None, :],
                    jnp.stack([cmp_w2_k[l], cmp_w2_v[l]]).astype(BF16), batch, seq)

    nd, npv, wb, tt = _bias_tables(rel_bias, seq)
    o_c, sel_t = _cmp_sel(proj, kvc, tt, _overlap_matrix(seq), batch, seq)
    o_w = _window(proj, wb, batch, seq)
    o_n = _selected(proj, sel_t, nd, npv, o_c, o_w, batch, seq)

    w_r = jnp.zeros((d, LANE), F32).at[:, :N_GROUPS].set(router_grp_w[l])
    w_r = w_r.at[:, N_GROUPS:N_GROUPS + N_EXPERTS].set(router_exp_w[l])
    w_r_hi = w_r.astype(BF16)
    w_r_lo = (w_r - w_r_hi.astype(F32)).astype(BF16)
    b_r = jnp.zeros((1, LANE), F32).at[0, :N_GROUPS].set(router_grp_b[l])
    b_r = b_r.at[0, N_GROUPS:N_GROUPS + N_EXPERTS].set(router_exp_b[l])
    x1, h2, logits = _merge(o_h, o_n, proj, x2, mod,
                            w_br_hg[l].astype(BF16), w_br_nsa[l].astype(BF16), w_out[l].astype(BF16),
                            ln1_g[l][None], ln1_b[l][None], jnp.stack([w_r_hi, w_r_lo]), b_r, seq)

    rec, cnt, rec_t = _route(logits)
    counts = cnt[0, N_GROUPS:N_GROUPS + N_EXPERTS].astype(jnp.int32)
    padded = (counts + MOE_ROWS - 1) // MOE_ROWS * MOE_ROWS
    pend = jnp.cumsum(padded)
    pstart = pend - padded
    n_assign = 2 * nt
    nb = n_assign // MOE_ROWS + N_EXPERTS
    slots = rec_t.astype(jnp.int32)
    expert_ids = jnp.arange(N_EXPERTS, dtype=jnp.int32)[:, None]
    slot_base = lambda e: jnp.sum(jnp.where(e[None, :] == expert_ids, pstart[:, None], 0), axis=0)
    dest0 = slot_base(slots[0]) + slots[4]
    dest1 = slot_base(slots[1]) + slots[5]
    block_start = jnp.arange(nb, dtype=jnp.int32) * MOE_ROWS
    block_expert = jnp.minimum(jnp.sum(pend[None, :] <= block_start[:, None], axis=1),
                               N_EXPERTS - 1).astype(jnp.int32)
    n_used = pend[-1] // MOE_ROWS
    spare = n_used + jnp.arange(N_EXPERTS, dtype=jnp.int32)
    zero_blocks = jnp.concatenate([jnp.where(padded > 0, pend // MOE_ROWS - 1, -1),
                                   jnp.where(spare < nb, spare, -1)]).astype(jnp.int32)

    x_pad = _dispatch(dest0, dest1, zero_blocks, h2, nb)
    y_pad = _experts(block_expert, n_used[None].astype(jnp.int32), x_pad, exp_w1[l], exp_w3[l], exp_w2[l])
    out = _combine(dest0, dest1, y_pad, rec, x1, mod, ln2_g[l][None], ln2_b[l][None], seq)
    return out.reshape(batch, seq, d)
```

```python
import functools
import math

import numpy as np
import jax
import jax.numpy as jnp
from jax import lax
from jax.experimental import pallas as pl
from jax.experimental.pallas import tpu as pltpu

F32 = jnp.float32
BF16 = jnp.bfloat16

D_MODEL = 1024
HG_HEADS = 8
HG_DK = 128
HG_DV = 128
HG_CHUNK = 32
HG_SUPER = 256
NSA_HEADS = 8
NSA_GROUPS = 2
NSA_REP = NSA_HEADS // NSA_GROUPS
NSA_DK = 128
CMP_BLOCK = 32
CMP_STRIDE = 16
SLC_BLOCK = 64
SLC_TOPK = 16
WINDOW = 512
FORCE_SCORE = 1e4
N_FORCED = 3
REL_BUCKETS = 32
REL_MAX_DIST = 128
N_GROUPS = 4
EXP_PER_GROUP = 8
N_EXPERTS = N_GROUPS * EXP_PER_GROUP
D_EXPERT = D_MODEL // 2
DEPTH = 1
ALPHA = (2 * DEPTH) ** 0.25

LANE = 128
QT = 128
NEG = -1e30
SCALE = NSA_DK ** -0.5
LOG2E = math.log2(math.e)
SCALE_LOG2 = SCALE * LOG2E
KEY_BLK = 512
BOUND_SLACK = 1.0 + 2.0 ** -10
MAX_REF_GAP = 64.0
SEL_ROWS = 16
SEL_SUB = KEY_BLK // QT
WIN_SUB = 8
CMP_SUB = 8
CMP_PAD = 120
VMEM_LIMIT = 56 * 1024 * 1024

CB_HQ, CB_HF, CB_HI, CB_HG = 0, 8, 16, 24
CB_NQ = 32
CB_KC, CB_VC, CB_KS, CB_VS, CB_KW, CB_VW = 40, 42, 44, 46, 48, 50
CB_GATE = 52
CB_MGH, CB_MGN = 56, 64
NCB = 72
MAIN_COLS = 52 * LANE
GATE_COLS = 3 * NSA_HEADS

INPROJ_ROWS = 2048
INPROJ_COL_BLOCKS = 8
HGRN_ROWS = 512
MERGE_ROWS = 512
ROUTE_ROWS = 512
DISPATCH_TOKENS = 2048
COMBINE_TOKENS = 256
MERGE_SPLIT = 2
ROW_TILES = D_MODEL // LANE
MOE_ROWS = 512
ROW_DMA_UNROLL = 8


def _cparams(sem):
    return pltpu.CompilerParams(dimension_semantics=sem, vmem_limit_bytes=VMEM_LIMIT)


def _dot(a, b):
    return jnp.dot(a, b, preferred_element_type=F32)


def _dot_nt(a, b):
    return lax.dot_general(a, b, (((1,), (1,)), ((), ())), preferred_element_type=F32)


def _dot_tn(a, b):
    return lax.dot_general(a, b, (((0,), (0,)), ((), ())), preferred_element_type=F32)


def _split3(x):
    hi = x.astype(BF16)
    r = x - hi.astype(F32)
    mid = r.astype(BF16)
    lo = (r - mid.astype(F32)).astype(BF16)
    return hi, mid, lo


def _dot01(m01, x):
    hi, mid, lo = _split3(x)
    return _dot(m01, hi) + _dot(m01, mid) + _dot(m01, lo)


def _layer_norm(x, eps=1e-5):
    mu = jnp.mean(x, axis=-1, keepdims=True)
    xc = x - mu
    var = jnp.mean(xc * xc, axis=-1, keepdims=True)
    return xc * lax.rsqrt(var + eps)


def _adaln_kernel(c_ref, w_ref, b_ref, o_ref):
    c = c_ref[...]
    ca = c * jax.nn.sigmoid(c)
    o_ref[...] = jnp.dot(ca, w_ref[...], precision=lax.Precision.HIGHEST,
                         preferred_element_type=F32) + b_ref[...]


def _adaln(c_pad, w, b):
    rows, d = c_pad.shape
    n = w.shape[1]
    return pl.pallas_call(
        _adaln_kernel,
        grid=(n // d,),
        in_specs=[pl.BlockSpec((rows, d), lambda j: (0, 0)),
                  pl.BlockSpec((d, d), lambda j: (0, j)),
                  pl.BlockSpec((1, d), lambda j: (0, j))],
        out_specs=pl.BlockSpec((rows, d), lambda j: (0, j)),
        out_shape=jax.ShapeDtypeStruct((rows, n), F32),
        compiler_params=_cparams(("arbitrary",)),
        name="adaln",
    )(c_pad, w, b)


def _inproj_kernel(x_ref, mod_ref, w_ref, b_ref, o_ref, hn_ref, *, ncb_tile):
    @pl.when(pl.program_id(1) == 0)
    def _():
        hn = _layer_norm(x_ref[...])
        sh = mod_ref[0, 0:1, :]
        sc = mod_ref[0, 1:2, :]
        hn_ref[...] = (hn * (1.0 + sc) + sh).astype(BF16)

    res = _dot(hn_ref[...], w_ref[...]) + b_ref[...]
    for c in range(ncb_tile):
        o_ref[c] = res[:, c * LANE:(c + 1) * LANE]


def _inproj(x2, mod, w, b, seq):
    nt, d = x2.shape
    tm = min(INPROJ_ROWS, seq)
    ncb_tile = INPROJ_COL_BLOCKS
    tn = ncb_tile * LANE
    return pl.pallas_call(
        functools.partial(_inproj_kernel, ncb_tile=ncb_tile),
        grid=(nt // tm, NCB // ncb_tile),
        in_specs=[pl.BlockSpec((tm, d), lambda i, j: (i, 0)),
                  pl.BlockSpec((1, 6, d), lambda i, j: (i * tm // seq, 0, 0)),
                  pl.BlockSpec((d, tn), lambda i, j: (0, j)),
                  pl.BlockSpec((1, tn), lambda i, j: (0, j))],
        out_specs=pl.BlockSpec((ncb_tile, tm, LANE), lambda i, j: (j, i, 0)),
        out_shape=jax.ShapeDtypeStruct((NCB, nt, LANE), F32),
        scratch_shapes=[pltpu.VMEM((tm, d), BF16)],
        compiler_params=_cparams(("arbitrary", "arbitrary")),
        name="inproj",
    )(x2, mod, w, b)


def _hgrn_kernel(q_ref, f_ref, v_ref, g_ref, lb_ref, nw_ref, o_ref, st_ref, *, rows):
    @pl.when(pl.program_id(1) == 0)
    def _():
        st_ref[...] = jnp.zeros_like(st_ref)

    sup = HG_SUPER
    ri = lax.broadcasted_iota(jnp.int32, (sup, sup), 0)
    ci = lax.broadcasted_iota(jnp.int32, (sup, sup), 1)
    same = (ri // HG_CHUNK) == (ci // HG_CHUNK)
    cum_m = jnp.where(same & (ci <= ri), 1.0, 0.0).astype(BF16)
    rt = lax.broadcasted_iota(jnp.int32, (LANE, LANE), 0)
    ct = lax.broadcasted_iota(jnp.int32, (LANE, LANE), 1)
    tril = ((rt // HG_CHUNK) == (ct // HG_CHUNK)) & (ct <= rt)
    per = sup // HG_CHUNK
    groups = [slice(g * LANE, (g + 1) * LANE) for g in range(sup // LANE)]

    heads = range(HG_HEADS)
    hs = [slice(h * LANE, (h + 1) * LANE) for h in heads]

    def wide(ref, r0):
        return jnp.concatenate([ref[h, pl.ds(r0, sup), :] for h in heads], axis=1)

    def body(i, carry):
        r0 = pl.multiple_of(i * sup, sup)
        lb = lb_ref[...]
        f = lb + (1.0 - lb) * jax.nn.sigmoid(wide(f_ref, r0))
        lf = jnp.log(f)
        k = 1.0 - f
        hi = lf.astype(BF16)
        lo = (lf - hi.astype(F32)).astype(BF16)
        b = _dot(cum_m, hi) + _dot(cum_m, lo)
        chunk = [slice(c * HG_CHUNK, (c + 1) * HG_CHUNK) for c in range(per)]
        dec = [jnp.exp(b[(c + 1) * HG_CHUNK - 1:(c + 1) * HG_CHUNK, :]) for c in range(per)]
        dec_rows = jnp.concatenate([jnp.broadcast_to(dec[c], (HG_CHUNK, dec[c].shape[1])) for c in range(per)],
                                   axis=0)
        q_in = (wide(q_ref, r0) * jnp.exp(b)).astype(BF16)
        k_dec = k * jnp.exp(-b)
        k_in = k_dec.astype(BF16)
        k_end = (k_dec * dec_rows).astype(BF16)
        vb = wide(v_ref, r0).astype(BF16)

        att = [[jnp.where(tril, _dot_nt(q_in[g, hs[h]], k_in[g, hs[h]]), 0.0).astype(BF16) for g in groups]
               for h in heads]
        upd = [[_dot_tn(vb[chunk[c], hs[h]], k_end[chunk[c], hs[h]]) for c in range(per)] for h in heads]
        intra = [[_dot(att[h][n], vb[g, hs[h]]) for n, g in enumerate(groups)] for h in heads]

        st = [st_ref[h] for h in heads]
        inter = [[] for _ in heads]
        for c in range(per):
            for h in heads:
                inter[h].append(_dot_nt(q_in[chunk[c], hs[h]], st[h].astype(BF16)))
                st[h] = dec[c][:, hs[h]] * st[h] + upd[h][c]
        for h in heads:
            st_ref[h] = st[h]

        o = [jnp.concatenate(intra[h], axis=0) + jnp.concatenate(inter[h], axis=0) for h in heads]
        scale = [lax.rsqrt(jnp.mean(o[h] * o[h], axis=-1, keepdims=True) + 1e-6) for h in heads]
        on = jnp.concatenate([o[h] * scale[h] for h in heads], axis=1)
        on = on * nw_ref[...] * jax.nn.sigmoid(wide(g_ref, r0))
        o_ref[pl.ds(r0, sup), :] = on.astype(BF16)
        return carry

    lax.fori_loop(0, rows // sup, body, 0)


def _hgrn(proj, lb, nw, batch, seq):
    nt = proj.shape[1]
    tb = min(HGRN_ROWS, seq)
    nblk = seq // tb

    def slab(cb0):
        return pl.BlockSpec((HG_HEADS, tb, LANE), lambda b, t: (cb0 // HG_HEADS, b * nblk + t, 0))

    vec = pl.BlockSpec((1, HG_HEADS * LANE), lambda b, t: (0, 0))
    return pl.pallas_call(
        functools.partial(_hgrn_kernel, rows=tb),
        grid=(batch, nblk),
        in_specs=[slab(CB_HQ), slab(CB_HF), slab(CB_HI), slab(CB_HG), vec, vec],
        out_specs=pl.BlockSpec((tb, HG_HEADS * HG_DV), lambda b, t: (b * nblk + t, 0)),
        out_shape=jax.ShapeDtypeStruct((nt, HG_HEADS * HG_DV), BF16),
        scratch_shapes=[pltpu.VMEM((HG_HEADS, HG_DV, HG_DK), F32)],
        compiler_params=_cparams(("arbitrary", "arbitrary")),
        name="hgrn2",
    )(proj, proj, proj, proj, lb, nw)


def _compress_kernel(x_ref, pos_ref, w1_ref, b1_ref, w2_ref, o_ref, *, ns):
    p0 = jnp.zeros((ns, LANE), F32)
    p1 = jnp.zeros((ns, LANE), F32)
    for j in range(CMP_STRIDE):
        tok = x_ref[0, pl.ds(j, ns, stride=CMP_STRIDE), :]
        rows = slice(j * LANE, (j + 1) * LANE)
        late = slice((CMP_STRIDE + j) * LANE, (CMP_STRIDE + j + 1) * LANE)
        p0 = p0 + _dot((tok + pos_ref[0, j:j + 1, :]).astype(BF16), w1_ref[0, rows, :])
        p1 = p1 + _dot((tok + pos_ref[0, CMP_STRIDE + j:CMP_STRIDE + j + 1, :]).astype(BF16), w1_ref[0, late, :])
    h = p0 + pltpu.roll(p1, ns - 1, axis=0) + b1_ref[0]
    a = h * jax.nn.sigmoid(h)
    out = _dot(a.astype(BF16), w2_ref[0])
    row = lax.broadcasted_iota(jnp.int32, out.shape, 0)
    out = jnp.where(row < ns - 1, out, 0.0)
    o_ref[0, 0, 0] = out


def _compress(proj, pos, w1, b1, w2, batch, seq):
    ns = seq // CMP_STRIDE
    np_rows = ns
    width = CMP_STRIDE * LANE
    return pl.pallas_call(
        functools.partial(_compress_kernel, ns=ns),
        grid=(2, batch, NSA_GROUPS),
        in_specs=[pl.BlockSpec((1, seq, LANE), lambda s, b, g: (CB_KC + NSA_GROUPS * s + g, b, 0)),
                  pl.BlockSpec((1, CMP_BLOCK, LANE), lambda s, b, g: (s, 0, 0)),
                  pl.BlockSpec((1, 2 * width, LANE), lambda s, b, g: (s, 0, 0)),
                  pl.BlockSpec((1, 1, LANE), lambda s, b, g: (s, 0, 0)),
                  pl.BlockSpec((1, LANE, LANE), lambda s, b, g: (s, 0, 0))],
        out_specs=pl.BlockSpec((1, 1, 1, np_rows, LANE), lambda s, b, g: (s, b, g, 0, 0)),
        out_shape=jax.ShapeDtypeStruct((2, batch, NSA_GROUPS, np_rows, LANE), F32),
        compiler_params=_cparams(("arbitrary", "arbitrary", "arbitrary")),
        name="nsa_compress",
    )(proj, pos, w1, b1, w2)


def _cmp_sel_kernel(q0, q1, q2, q3, kc_ref, vc_ref, gate_ref, tt_ref, ov_ref, oc_ref, sel_ref,
                    kb_ref, vt_ref, imp_ref, *, np_rows):
    step = pl.program_id(2)
    cols = NSA_REP * QT
    ns = np_rows

    @pl.when(step == 0)
    def _():
        kb_ref[...] = (kc_ref[0, 0, 0] * SCALE_LOG2).astype(BF16)
        for c in range(np_rows // LANE):
            vt_ref[:, c * LANE:(c + 1) * LANE] = vc_ref[0, 0, 0, c * LANE:(c + 1) * LANE, :].T.astype(BF16)

    subs = range(CMP_SUB)
    tis = [step * CMP_SUB + u for u in subs]
    qts = [_q_transposed((q0, q1, q2, q3), u) for u in subs]

    def group_sum(p):
        tot = p[:, 0:QT]
        for r in range(1, NSA_REP):
            tot = tot + p[:, r * QT:(r + 1) * QT]
        return tot

    def attend(nrows):
        kb = kb_ref[0:nrows, :]
        bias = [tt_ref[0, pl.ds(pl.multiple_of(ns + CMP_PAD - tis[u] * (QT // CMP_STRIDE), 8), nrows), :]
                for u in subs]
        ss = [_dot(kb, qts[u]) + bias[u] for u in subs]
        ms = [jnp.max(ss[u], axis=0, keepdims=True) for u in subs]
        ps = [jnp.exp2(ss[u] - ms[u]) for u in subs]
        ls = [jnp.sum(ps[u], axis=0, keepdims=True) for u in subs]
        invs = [jnp.where(ms[u] > 0.5 * NEG, 1.0 / ls[u], 0.0) for u in subs]
        pn = [ps[u] * invs[u] for u in subs]
        vt = vt_ref[:, 0:nrows]
        os_ = [_dot(vt, pn[u].astype(BF16)) for u in subs]
        for u in subs:
            rows = slice(u * QT, (u + 1) * QT)
            gt = jax.nn.sigmoid(gate_ref[0, rows, :])
            for r in range(NSA_REP):
                oc_ref[rows, r * LANE:(r + 1) * LANE] = gt[:, 3 * r:3 * r + 1] * os_[u][:, r * QT:(r + 1) * QT].T
            nblk = nrows * CMP_STRIDE // SLC_BLOCK
            imp_ref[0:nblk, rows] = _dot01(ov_ref[0:nblk, 0:nrows], group_sum(pn[u]))

    width = CMP_SUB * QT

    def choose(nblk):
        assert FORCE_SCORE > NSA_REP and nblk >= SLC_TOPK
        jj = lax.broadcasted_iota(jnp.int32, (nblk, width), 0)
        tok = step * width + lax.broadcasted_iota(jnp.int32, (nblk, width), 1)
        cur = tok // SLC_BLOCK
        forced = (jj == 0) | (jj == cur) | (jj == cur - 1)
        score = jnp.where(forced, -jnp.inf, jnp.where(jj <= cur, imp_ref[0:nblk, :], -1.0))
        jf = jj.astype(F32)
        for _ in range(SLC_TOPK - N_FORCED):
            mval = jnp.max(score, axis=0, keepdims=True)
            first = jnp.min(jnp.where(score == mval, jf, float(LANE)), axis=0, keepdims=True)
            score = jnp.where(jf == first, -jnp.inf, score)
        sel_ref[0, 0, 0:nblk, :] = jnp.where(score == -jnp.inf, 0.0, NEG)
        if nblk < LANE:
            sel_ref[0, 0, nblk:LANE, :] = jnp.full((LANE - nblk, width), NEG, F32)

    visible = (step + 1) * CMP_SUB * (QT // CMP_STRIDE)
    ngroups = np_rows // LANE
    for g in range(1, ngroups + 1):
        upper = visible <= g * LANE if g < ngroups else True

        @pl.when((visible > (g - 1) * LANE) & upper)
        def _():
            attend(g * LANE)
            choose(g * LANE * CMP_STRIDE // SLC_BLOCK)


def _q_specs(nsteps, rows=QT):
    return [pl.BlockSpec((1, rows, LANE),
                         functools.partial(lambda b, g, t, r: (CB_NQ + NSA_REP * g + r, b * nsteps + t, 0), r=r))
            for r in range(NSA_REP)]


def _cmp_sel(proj, kvc, tt, ov_t, batch, seq):
    nt = proj.shape[1]
    rows = CMP_SUB * QT
    nsteps = seq // rows
    cols = NSA_REP * QT
    np_rows = kvc.shape[3]
    kv_spec = lambda s: pl.BlockSpec((1, 1, 1, np_rows, LANE), lambda b, g, t: (s, b, g, 0, 0))
    return pl.pallas_call(
        functools.partial(_cmp_sel_kernel, np_rows=np_rows),
        grid=(batch, NSA_GROUPS, nsteps),
        in_specs=_q_specs(nsteps, rows) + [
            kv_spec(0), kv_spec(1),
            pl.BlockSpec((1, rows, LANE), lambda b, g, t: (CB_GATE + g, b * nsteps + t, 0)),
            pl.BlockSpec((1, tt.shape[1], cols), lambda b, g, t: (g, 0, 0)),
            pl.BlockSpec((LANE, np_rows), lambda b, g, t: (0, 0))],
        out_specs=[pl.BlockSpec((rows, cols), lambda b, g, t: (b * nsteps + t, g)),
                   pl.BlockSpec((1, 1, LANE, rows), lambda b, g, t: (b, g, 0, t))],
        out_shape=[jax.ShapeDtypeStruct((nt, NSA_HEADS * LANE), F32),
                   jax.ShapeDtypeStruct((batch, NSA_GROUPS, LANE, seq), F32)],
        scratch_shapes=[pltpu.VMEM((np_rows, LANE), BF16), pltpu.VMEM((LANE, np_rows), BF16),
                        pltpu.VMEM((LANE, rows), F32)],
        compiler_params=_cparams(("arbitrary", "arbitrary", "arbitrary")),
        name="nsa_cmp_select",
    )(proj, proj, proj, proj, kvc, kvc, proj, tt, ov_t)


def _q_transposed(q_refs, sub=0):
    return jnp.concatenate([r[0, sub * QT:(sub + 1) * QT, :].T for r in q_refs], axis=1).astype(BF16)


def _finish_t(state, gate_ref, branch, o_ref, sub=0, others=()):
    _, l, acc = state
    o = acc / jnp.where(l == 0.0, 1.0, l)
    rows = slice(sub * QT, (sub + 1) * QT)
    gt = jax.nn.sigmoid(gate_ref[0, rows, :])
    for r in range(NSA_REP):
        col = 3 * r + branch
        lanes = slice(r * LANE, (r + 1) * LANE)
        val = gt[:, col:col + 1] * o[:, r * QT:(r + 1) * QT].T
        for other in others:
            val = val + other[rows, lanes]
        o_ref[rows, lanes] = val.astype(o_ref.dtype)


def _init_state(cols):
    return (jnp.full((1, cols), NEG, F32), jnp.zeros((1, cols), F32), jnp.zeros((LANE, cols), F32))


def _win_kernel(q0, q1, q2, q3, k_ref, v_ref, gate_ref, wb_ref, o_ref, kb_ref, vt_ref, *, seq):
    step = pl.program_id(2)
    cols = NSA_REP * QT
    wt = WINDOW // QT

    lane = lax.broadcasted_iota(jnp.int32, (QT, LANE), 1)
    pad_mark = jnp.where(lane == 0, 1.0, 0.0).astype(BF16)

    @pl.when(step == 0)
    def _():
        for i in range(wt):
            kb_ref[i * QT:(i + 1) * QT, 0:LANE] = jnp.zeros((QT, LANE), BF16)
            kb_ref[i * QT:(i + 1) * QT, LANE:2 * LANE] = pad_mark
            vt_ref[i] = jnp.zeros((LANE, QT), BF16)

        def fill(i, carry):
            r0 = pl.multiple_of(i * QT, QT)
            kb_ref[pl.ds(WINDOW + r0, QT), 0:LANE] = (k_ref[0, pl.ds(r0, QT), :] * SCALE_LOG2).astype(BF16)
            kb_ref[pl.ds(WINDOW + r0, QT), LANE:2 * LANE] = jnp.zeros((QT, LANE), BF16)
            vt_ref[wt + i] = v_ref[0, pl.ds(r0, QT), :].T.astype(BF16)
            return carry

        lax.fori_loop(0, seq // QT, fill, 0)

    subs = range(WIN_SUB)
    tis = [step * WIN_SUB + sub for sub in subs]
    row = lax.broadcasted_iota(jnp.int32, (LANE, cols), 0)
    pad_rows = jnp.where(row == 0, NEG, 0.0).astype(BF16)
    qts = [jnp.concatenate([_q_transposed((q0, q1, q2, q3), sub), pad_rows], axis=0) for sub in subs]
    ss = [_dot(kb_ref[pl.ds(pl.multiple_of(tis[u] * QT, QT), WINDOW + QT), :], qts[u]) + wb_ref[0]
          for u in subs]
    ms = [jnp.max(ss[u], axis=0, keepdims=True) for u in subs]
    ps = [jnp.exp2(ss[u] - ms[u]) for u in subs]
    ls = [jnp.sum(ps[u], axis=0, keepdims=True) for u in subs]
    pbs = [ps[u].astype(BF16) for u in subs]
    accs = None
    for d in range(0, wt + 1, 2):
        n = min(2, wt + 1 - d)
        vts = [jnp.concatenate([vt_ref[tis[u] + d + e] for e in range(n)], axis=1) for u in subs]
        part = [_dot(vts[u], pbs[u][d * QT:(d + n) * QT]) for u in subs]
        accs = part if accs is None else [accs[u] + part[u] for u in subs]
    for u in subs:
        _finish_t((ms[u], ls[u], accs[u]), gate_ref, 2, o_ref, u)


def _window(proj, wb, batch, seq):
    nt = proj.shape[1]
    rows = WIN_SUB * QT
    nqt = seq // rows
    cols = NSA_REP * QT
    slab = lambda cb0: pl.BlockSpec((1, seq, LANE), lambda b, g, t: (cb0 + g, b, 0))
    return pl.pallas_call(
        functools.partial(_win_kernel, seq=seq),
        grid=(batch, NSA_GROUPS, nqt),
        in_specs=_q_specs(nqt, rows) + [
            slab(CB_KW), slab(CB_VW),
            pl.BlockSpec((1, rows, LANE), lambda b, g, t: (CB_GATE + g, b * nqt + t, 0)),
            pl.BlockSpec((1, WINDOW + QT, cols), lambda b, g, t: (g, 0, 0))],
        out_specs=pl.BlockSpec((rows, cols), lambda b, g, t: (b * nqt + t, g)),
        out_shape=jax.ShapeDtypeStruct((nt, NSA_HEADS * LANE), F32),
        scratch_shapes=[pltpu.VMEM((seq + WINDOW, 2 * LANE), BF16),
                        pltpu.VMEM((seq // QT + WINDOW // QT, LANE, QT), BF16)],
        compiler_params=_cparams(("arbitrary", "arbitrary", "arbitrary")),
        name="nsa_window",
    )(proj, proj, proj, proj, proj, proj, proj, wb)


def _softmax_steps(ss, vt, states):
    n = range(len(ss))
    m_new = [jnp.maximum(states[u][0], jnp.max(ss[u], axis=0, keepdims=True)) for u in n]
    alpha = [jnp.exp2(states[u][0] - m_new[u]) for u in n]
    p = [jnp.exp2(ss[u] - m_new[u]) for u in n]
    l = [alpha[u] * states[u][1] + jnp.sum(p[u], axis=0, keepdims=True) for u in n]
    vts = vt if isinstance(vt, (list, tuple)) else [vt] * len(ss)
    pv = [_dot(vts[u], p[u].astype(BF16)) for u in n]
    return [(m_new[u], l[u], alpha[u] * states[u][2] + pv[u]) for u in n]


def _sel_kernel(q0, q1, q2, q3, k_ref, v_ref, sel_ref, gate_ref, nd_ref, npv_ref, oc_ref, ow_ref, o_ref,
                kb_ref, vt_ref, qa_ref, kn_ref, m_ref, l_ref, acc_ref, *, seq):
    step = pl.program_id(2)
    cols = NSA_REP * QT
    spb = KEY_BLK // SLC_BLOCK
    subs = range(SEL_SUB)

    @pl.when(step == 0)
    def _():
        blk = lax.broadcasted_iota(jnp.int32, (KEY_BLK, LANE), 0) // SLC_BLOCK
        onehot = jnp.where(blk == lax.broadcasted_iota(jnp.int32, (KEY_BLK, LANE), 1), 1.0, 0.0).astype(BF16)

        def fill(i, kmax2):
            r0 = pl.multiple_of(i * KEY_BLK, KEY_BLK)
            kblk = (k_ref[0, pl.ds(r0, KEY_BLK), :] * SCALE_LOG2).astype(BF16)
            kb_ref[pl.ds(r0, KEY_BLK), 0:LANE] = kblk
            kb_ref[pl.ds(r0, KEY_BLK), LANE:2 * LANE] = onehot
            for c in range(SEL_SUB):
                vt_ref[i, :, c * QT:(c + 1) * QT] = v_ref[0, pl.ds(r0 + c * QT, QT), :].T.astype(BF16)
            n2 = jnp.sum(jnp.square(kblk.astype(F32)), axis=1, keepdims=True)
            return jnp.maximum(kmax2, jnp.max(n2, axis=0, keepdims=True))

        kmax2 = lax.fori_loop(0, seq // KEY_BLK, fill, jnp.zeros((1, 1), F32))
        kn_ref[...] = jnp.sqrt(kmax2)
        qa_ref[...] = jnp.zeros_like(qa_ref)

    qts = [_q_transposed((q0, q1, q2, q3), u) for u in subs]
    for u in subs:
        qa_ref[u, 0:LANE, :] = qts[u]
    bounds = [jnp.sqrt(jnp.sum(jnp.square(qts[u].astype(F32)), axis=0, keepdims=True)) * kn_ref[...]
              * BOUND_SLACK + 1.0 for u in subs]

    def block(kb):
        r0 = pl.multiple_of(kb * KEY_BLK, KEY_BLK)
        return kb_ref[pl.ds(r0, KEY_BLK), :], vt_ref[kb]

    pad_rows = jnp.zeros((SEL_ROWS - spb, cols), F32)

    def scores(kb, k, rows_of=lambda u: KEY_BLK):
        for u in subs:
            rows = sel_ref[0, 0, pl.ds(pl.multiple_of(kb * spb, spb), spb), u * QT:(u + 1) * QT]
            rows = jnp.concatenate([rows] * NSA_REP, axis=1)
            qa_ref[u, LANE:LANE + SEL_ROWS, :] = jnp.concatenate([rows, pad_rows], axis=0).astype(BF16)
        return [_dot(k[0:rows_of(u)], qa_ref[u]) for u in subs]

    seen = lambda u: (u + 1) * QT
    k, vt = block(step)
    ss = scores(step, k, seen)
    ss = [ss[u] + nd_ref[0, u, 0:seen(u), :] for u in subs]
    states = _softmax_steps(ss, [vt[:, 0:seen(u)] for u in subs], [_init_state(cols) for _ in subs])

    for u in subs:
        m_ref[u], l_ref[u], acc_ref[u] = states[u]

    def prev_scores():
        k, vt = block(step - 1)
        ss = scores(step - 1, k)
        ss[0] = ss[0] + npv_ref[0]
        return ss, vt

    n_old = jnp.maximum(step - 1, 0)
    bounds[0] = bounds[0] + jnp.max(npv_ref[0], axis=0, keepdims=True)
    m_fix = [jnp.maximum(states[u][0], bounds[u]) for u in subs]
    gap = jnp.max(jnp.concatenate([m_fix[u] - states[u][0] for u in subs], axis=1))
    fixed_ok = gap < MAX_REF_GAP

    @pl.when(fixed_ok)
    def _():
        for u in subs:
            alpha = jnp.exp2(m_ref[u] - m_fix[u])
            l_ref[u] = alpha * l_ref[u]
            acc_ref[u] = alpha * acc_ref[u]
            m_ref[u] = m_fix[u]

        def fixed_step(ss, vt):
            ps = [jnp.exp2(ss[u] - m_ref[u]) for u in subs]
            for u in subs:
                l_ref[u] = l_ref[u] + jnp.sum(ps[u], axis=0, keepdims=True)
            pv = [_dot(vt, ps[u].astype(BF16)) for u in subs]
            for u in subs:
                acc_ref[u] = acc_ref[u] + pv[u]

        @pl.when(step >= 1)
        def _():
            fixed_step(*prev_scores())

        def body(kb, carry):
            k, vt = block(kb)
            fixed_step(scores(kb, k), vt)
            return carry

        lax.fori_loop(0, n_old, body, 0)

    @pl.when(jnp.logical_not(fixed_ok))
    def _():
        def online_step(ss, vt):
            st = _softmax_steps(ss, vt, [(m_ref[u], l_ref[u], acc_ref[u]) for u in subs])
            for u in subs:
                m_ref[u], l_ref[u], acc_ref[u] = st[u]

        @pl.when(step >= 1)
        def _():
            online_step(*prev_scores())

        def body(kb, carry):
            k, vt = block(kb)
            online_step(scores(kb, k), vt)
            return carry

        lax.fori_loop(0, n_old, body, 0)

    for u in subs:
        _finish_t((m_ref[u], l_ref[u], acc_ref[u]), gate_ref, 1, o_ref, u, others=(oc_ref, ow_ref))


def _selected(proj, sel_t, nd, npv, o_c, o_w, batch, seq):
    nt = proj.shape[1]
    rows = SEL_SUB * QT
    nsteps = seq // rows
    cols = NSA_REP * QT
    slab = lambda cb0: pl.BlockSpec((1, seq, LANE), lambda b, g, t: (cb0 + g, b, 0))
    return pl.pallas_call(
        functools.partial(_sel_kernel, seq=seq),
        grid=(batch, NSA_GROUPS, nsteps),
        in_specs=_q_specs(nsteps, rows) + [
            slab(CB_KS), slab(CB_VS),
            pl.BlockSpec((1, 1, LANE, rows), lambda b, g, t: (b, g, 0, t)),
            pl.BlockSpec((1, rows, LANE), lambda b, g, t: (CB_GATE + g, b * nsteps + t, 0)),
            pl.BlockSpec((1, SEL_SUB, KEY_BLK, cols), lambda b, g, t: (g, 0, 0, 0)),
            pl.BlockSpec((1, KEY_BLK, cols), lambda b, g, t: (g, 0, 0)),
            pl.BlockSpec((rows, cols), lambda b, g, t: (b * nsteps + t, g)),
            pl.BlockSpec((rows, cols), lambda b, g, t: (b * nsteps + t, g))],
        out_specs=pl.BlockSpec((rows, cols), lambda b, g, t: (b * nsteps + t, g)),
        out_shape=jax.ShapeDtypeStruct((nt, NSA_HEADS * LANE), BF16),
        scratch_shapes=[pltpu.VMEM((seq, 2 * LANE), BF16),
                        pltpu.VMEM((seq // KEY_BLK, LANE, KEY_BLK), BF16),
                        pltpu.VMEM((SEL_SUB, 2 * LANE, cols), BF16),
                        pltpu.VMEM((1, 1), F32),
                        pltpu.VMEM((SEL_SUB, 1, cols), F32), pltpu.VMEM((SEL_SUB, 1, cols), F32),
                        pltpu.VMEM((SEL_SUB, LANE, cols), F32)],
        compiler_params=_cparams(("arbitrary", "arbitrary", "arbitrary")),
        name="nsa_selected",
    )(proj, proj, proj, proj, proj, proj, sel_t, proj, nd, npv, o_c, o_w)


def _merge_kernel(oh_ref, on_ref, mgh_ref, mgn_ref, x_ref, mod_ref,
                  wh_ref, wn_ref, wo_ref, g_ref, b_ref, wr_ref, br_ref,
                  x1_ref, h2_ref, lg_ref):
    nblk = D_MODEL // LANE
    tm = x_ref.shape[0]
    halves = [slice(s * tm // MERGE_SPLIT, (s + 1) * tm // MERGE_SPLIT) for s in range(MERGE_SPLIT)]
    a_h = [_dot(oh_ref[rs, :], wh_ref[...]) for rs in halves]
    a_n = [_dot(on_ref[rs, :], wn_ref[...]) for rs in halves]
    gh = [jnp.concatenate([mgh_ref[c, rs, :] for c in range(nblk)], axis=-1) for rs in halves]
    gn = [jnp.concatenate([mgn_ref[c, rs, :] for c in range(nblk)], axis=-1) for rs in halves]
    merged = [(jax.nn.sigmoid(gh[s]) * a_h[s] + jax.nn.sigmoid(gn[s]) * a_n[s]).astype(BF16)
              for s in range(MERGE_SPLIT)]
    y = [(1.0 + mod_ref[0, 2:3, :]) * _dot(merged[s], wo_ref[...]) for s in range(MERGE_SPLIT)]
    x1 = [_layer_norm(ALPHA * x_ref[rs, :] + y[s]) * g_ref[...] + b_ref[...] for s, rs in enumerate(halves)]
    h2 = [_layer_norm(x1[s]) * (1.0 + mod_ref[0, 4:5, :]) + mod_ref[0, 3:4, :] for s in range(MERGE_SPLIT)]
    h_hi = [h2[s].astype(BF16) for s in range(MERGE_SPLIT)]
    h_lo = [(h2[s] - h_hi[s].astype(F32)).astype(BF16) for s in range(MERGE_SPLIT)]
    lg = [_dot(h_hi[s], wr_ref[0]) + _dot(h_hi[s], wr_ref[1]) + _dot(h_lo[s], wr_ref[0]) + br_ref[...]
          for s in range(MERGE_SPLIT)]
    for s, rs in enumerate(halves):
        x1_ref[rs, :] = x1[s]
        _store_rows(h2_ref, h2[s], rs.start)
        lg_ref[rs, :] = lg[s]


def _merge(o_h, o_n, proj, x2, mod, w_h, w_n, w_o, ln_g, ln_b, w_r, b_r, seq):
    nt, d = x2.shape
    tm = min(MERGE_ROWS, seq)
    nblk = d // LANE
    row = lambda w: pl.BlockSpec((tm, w), lambda i: (i, 0))
    full = lambda a: pl.BlockSpec(a.shape, lambda i: (0,) * a.ndim)
    return pl.pallas_call(
        _merge_kernel,
        grid=(nt // tm,),
        in_specs=[row(d), row(d),
                  pl.BlockSpec((nblk, tm, LANE), lambda i: (CB_MGH // nblk, i, 0)),
                  pl.BlockSpec((nblk, tm, LANE), lambda i: (CB_MGN // nblk, i, 0)),
                  row(d),
                  pl.BlockSpec((1, 6, d), lambda i: (i * tm // seq, 0, 0)),
                  full(w_h), full(w_n), full(w_o), full(ln_g), full(ln_b), full(w_r), full(b_r)],
        out_specs=[row(d), pl.BlockSpec((tm * ROW_TILES, LANE), lambda i: (i, 0)), row(LANE)],
        out_shape=[jax.ShapeDtypeStruct((nt, d), F32),
                   jax.ShapeDtypeStruct((nt * ROW_TILES, LANE), F32),
                   jax.ShapeDtypeStruct((nt, LANE), F32)],
        compiler_params=_cparams(("arbitrary",)),
        name="merge_outproj",
    )(o_h, o_n, proj, proj, x2, mod, w_h, w_n, w_o, ln_g, ln_b, w_r, b_r)


def _route_kernel(lg_ref, rec_ref, cnt_ref, rect_ref, carry_ref, *, tm):
    @pl.when(pl.program_id(0) == 0)
    def _():
        carry_ref[...] = jnp.zeros_like(carry_ref)

    lg = lg_ref[...]
    lane = lax.broadcasted_iota(jnp.int32, (tm, LANE), 1).astype(F32)
    far = float(LANE)
    gmask = lane < N_GROUPS
    gl = jnp.where(gmask, lg, -jnp.inf)
    gmax = jnp.max(gl, axis=-1, keepdims=True)
    gsum = jnp.sum(jnp.where(gmask, jnp.exp(gl - gmax), 0.0), axis=-1, keepdims=True)
    grp_p = 1.0 / gsum
    gidx = jnp.min(jnp.where(gl == gmax, lane, far), axis=-1, keepdims=True)
    lo = N_GROUPS + EXP_PER_GROUP * gidx
    emask = (lane >= lo) & (lane < lo + EXP_PER_GROUP)
    el = jnp.where(emask, lg, -jnp.inf)
    m1 = jnp.max(el, axis=-1, keepdims=True)
    i1 = jnp.min(jnp.where(el == m1, lane, far), axis=-1, keepdims=True)
    el2 = jnp.where(lane == i1, -jnp.inf, el)
    m2 = jnp.max(el2, axis=-1, keepdims=True)
    i2 = jnp.min(jnp.where(emask & (lane != i1) & (el2 == m2), lane, far), axis=-1, keepdims=True)
    e = jnp.exp(m2 - m1)
    w0 = grp_p / (1.0 + e)
    w1 = grp_p * e / (1.0 + e)

    oh0 = lane == i1
    oh1 = lane == i2
    f0 = jnp.where(oh0, 1.0, 0.0)
    f1 = jnp.where(oh1, 1.0, 0.0)
    ri = lax.broadcasted_iota(jnp.int32, (tm, tm), 0)
    ci = lax.broadcasted_iota(jnp.int32, (tm, tm), 1)
    before = jnp.where(ci < ri, 1.0, 0.0).astype(BF16)
    cum0 = _dot(before, f0.astype(BF16))
    cum1 = _dot(before, f1.astype(BF16))
    tot0 = jnp.sum(f0, axis=0, keepdims=True)
    tot1 = jnp.sum(f1, axis=0, keepdims=True)
    carry = carry_ref[...]
    rank0 = jnp.sum(jnp.where(oh0, carry + cum0, 0.0), axis=-1, keepdims=True)
    rank1 = jnp.sum(jnp.where(oh1, carry + tot0 + cum1, 0.0), axis=-1, keepdims=True)
    carry = carry + tot0 + tot1
    carry_ref[...] = carry
    cnt_ref[...] = carry

    rec = jnp.where(lane == 0, i1 - N_GROUPS, 0.0)
    rec = jnp.where(lane == 1, i2 - N_GROUPS, rec)
    rec = jnp.where(lane == 2, w0, rec)
    rec = jnp.where(lane == 3, w1, rec)
    rec = jnp.where(lane == 4, rank0, rec)
    rec = jnp.where(lane == 5, rank1, rec)
    rec_ref[...] = rec
    rect_ref[...] = jnp.concatenate([rec[c * LANE:(c + 1) * LANE, :].T[0:8, :] for c in range(tm // LANE)], axis=1)


def _route(logits):
    nt = logits.shape[0]
    tm = min(ROUTE_ROWS, nt)
    return pl.pallas_call(
        functools.partial(_route_kernel, tm=tm),
        grid=(nt // tm,),
        in_specs=[pl.BlockSpec((tm, LANE), lambda i: (i, 0))],
        out_specs=[pl.BlockSpec((tm, LANE), lambda i: (i, 0)),
                   pl.BlockSpec((1, LANE), lambda i: (0, 0)),
                   pl.BlockSpec((8, tm), lambda i: (0, i))],
        out_shape=[jax.ShapeDtypeStruct((nt, LANE), F32),
                   jax.ShapeDtypeStruct((1, LANE), F32),
                   jax.ShapeDtypeStruct((8, nt), F32)],
        scratch_shapes=[pltpu.VMEM((1, LANE), F32)],
        compiler_params=_cparams(("arbitrary",)),
        name="moe_route",
    )(logits)


def _row_copy(src, dst, sem):
    return pltpu.make_async_copy(src, dst, sem)


def _tile_of(r):
    return pl.ds(pl.multiple_of(r * ROW_TILES, ROW_TILES), ROW_TILES)


def _load_rows(ref, n, lead=()):
    return jnp.concatenate([ref[lead + (pl.ds(c, n, stride=ROW_TILES), slice(None))]
                            for c in range(ROW_TILES)], axis=1)


def _store_rows(ref, val, row0=0):
    n = val.shape[0]
    for c in range(ROW_TILES):
        ref[pl.ds(row0 * ROW_TILES + c, n, stride=ROW_TILES), :] = val[:, c * LANE:(c + 1) * LANE]


def _dispatch_kernel(dest0_ref, dest1_ref, zb_ref, h_ref, xp_ref, z_ref, sem, zsem, *, tm):
    step = pl.program_id(0)
    base = step * tm
    blk = MOE_ROWS * ROW_TILES

    @pl.when(step == 0)
    def _():
        z_ref[...] = jnp.zeros_like(z_ref)

        def zero_copy(j):
            b = jnp.maximum(zb_ref[j], 0)
            return _row_copy(z_ref, xp_ref.at[pl.ds(pl.multiple_of(b * blk, blk), blk), :], zsem)

        def start(j, carry):
            @pl.when(zb_ref[j] >= 0)
            def _():
                zero_copy(j).start()
            return carry

        def wait(j, carry):
            @pl.when(zb_ref[j] >= 0)
            def _():
                zero_copy(j).wait()
            return carry

        lax.fori_loop(0, 2 * N_EXPERTS, start, 0)
        lax.fori_loop(0, 2 * N_EXPERTS, wait, 0)

    def issue(r, carry):
        for k, dest_ref in enumerate((dest0_ref, dest1_ref)):
            d = dest_ref[base + r]
            _row_copy(h_ref.at[_tile_of(r), :], xp_ref.at[_tile_of(d), :], sem).start(priority=k)
        return carry

    lax.fori_loop(0, tm, issue, 0, unroll=ROW_DMA_UNROLL)
    for _ in range(2):
        _row_copy(h_ref, xp_ref.at[pl.ds(0, tm * ROW_TILES), :], sem).wait()


def _dispatch(dest0, dest1, zero_blocks, h2, n_blocks):
    nt = h2.shape[0] // ROW_TILES
    tm = min(DISPATCH_TOKENS, nt)
    return pl.pallas_call(
        functools.partial(_dispatch_kernel, tm=tm),
        grid_spec=pltpu.PrefetchScalarGridSpec(
            num_scalar_prefetch=3,
            grid=(nt // tm,),
            in_specs=[pl.BlockSpec((tm * ROW_TILES, LANE), lambda i, *_: (i, 0))],
            out_specs=pl.BlockSpec(memory_space=pl.ANY),
            scratch_shapes=[pltpu.VMEM((MOE_ROWS * ROW_TILES, LANE), F32),
                            pltpu.SemaphoreType.DMA(()), pltpu.SemaphoreType.DMA(())]),
        out_shape=jax.ShapeDtypeStruct((n_blocks * MOE_ROWS * ROW_TILES, LANE), F32),
        compiler_params=_cparams(("arbitrary",)),
        name="moe_dispatch",
    )(dest0, dest1, zero_blocks, h2)


def _expert_kernel(be_ref, nu_ref, x_ref, w1_ref, w3_ref, w2_ref, y_ref, w1b_ref, w3b_ref, w2b_ref):
    i = pl.program_id(0)

    @pl.when((i < nu_ref[0]) & ((i == 0) | (be_ref[i] != be_ref[jnp.maximum(i - 1, 0)])))
    def _():
        w1b_ref[...] = w1_ref[0].astype(BF16)
        w3b_ref[...] = w3_ref[0].astype(BF16)
        w2b_ref[...] = w2_ref[0].astype(BF16)

    @pl.when(i < nu_ref[0])
    def _():
        xb = _load_rows(x_ref, MOE_ROWS).astype(BF16)
        a = _dot(xb, w1b_ref[...])
        b = _dot(xb, w3b_ref[...])
        hmid = (a * jax.nn.sigmoid(a) * b).astype(BF16)
        _store_rows(y_ref, _dot(hmid, w2b_ref[...]))

    @pl.when(i >= nu_ref[0])
    def _():
        y_ref[...] = jnp.zeros_like(y_ref)


def _experts(block_expert, n_used, x_pad, w1, w3, w2):
    d, de = w1.shape[1], w1.shape[2]
    nb = x_pad.shape[0] // (MOE_ROWS * ROW_TILES)
    return pl.pallas_call(
        _expert_kernel,
        grid_spec=pltpu.PrefetchScalarGridSpec(
            num_scalar_prefetch=2,
            grid=(nb,),
            in_specs=[pl.BlockSpec((MOE_ROWS * ROW_TILES, LANE), lambda i, be, nu: (i, 0)),
                      pl.BlockSpec((1, d, de), lambda i, be, nu: (be[i], 0, 0)),
                      pl.BlockSpec((1, d, de), lambda i, be, nu: (be[i], 0, 0)),
                      pl.BlockSpec((1, de, d), lambda i, be, nu: (be[i], 0, 0))],
            out_specs=pl.BlockSpec((MOE_ROWS * ROW_TILES, LANE), lambda i, be, nu: (i, 0)),
            scratch_shapes=[pltpu.VMEM((d, de), BF16), pltpu.VMEM((d, de), BF16), pltpu.VMEM((de, d), BF16)]),
        out_shape=jax.ShapeDtypeStruct(x_pad.shape, F32),
        compiler_params=_cparams(("arbitrary",)),
        name="moe_experts",
    )(block_expert, n_used, x_pad, w1, w3, w2)


def _combine_kernel(dest0_ref, dest1_ref, yp_ref, rec_ref, x1_ref, mod_ref, g_ref, b_ref, o_ref,
                    buf_ref, sem, *, tm):
    step = pl.program_id(0)
    slot = step % 2

    def fetch(tile, to_slot):
        def issue(r, carry):
            for k, dest_ref in enumerate((dest0_ref, dest1_ref)):
                d = dest_ref[tile * tm + r]
                _row_copy(yp_ref.at[_tile_of(d), :], buf_ref.at[to_slot, k, _tile_of(r), :],
                          sem.at[to_slot]).start(priority=k)
            return carry

        lax.fori_loop(0, tm, issue, 0, unroll=ROW_DMA_UNROLL)

    @pl.when(step == 0)
    def _():
        fetch(0, 0)

    @pl.when(step + 1 < pl.num_programs(0))
    def _():
        fetch(step + 1, 1 - slot)

    for k in range(2):
        _row_copy(yp_ref.at[pl.ds(0, tm * ROW_TILES), :], buf_ref.at[slot, k], sem.at[slot]).wait()

    rec = rec_ref[...]
    y = rec[:, 2:3] * _load_rows(buf_ref, tm, (slot, 0)) + rec[:, 3:4] * _load_rows(buf_ref, tm, (slot, 1))
    y = (1.0 + mod_ref[0, 5:6, :]) * y
    o_ref[...] = _layer_norm(ALPHA * x1_ref[...] + y) * g_ref[...] + b_ref[...]


def _combine(dest0, dest1, y_pad, rec, x1, mod, ln_g, ln_b, seq):
    nt, d = x1.shape
    tm = min(COMBINE_TOKENS, seq)
    return pl.pallas_call(
        functools.partial(_combine_kernel, tm=tm),
        grid_spec=pltpu.PrefetchScalarGridSpec(
            num_scalar_prefetch=2,
            grid=(nt // tm,),
            in_specs=[pl.BlockSpec(memory_space=pl.ANY),
                      pl.BlockSpec((tm, LANE), lambda i, *_: (i, 0)),
                      pl.BlockSpec((tm, d), lambda i, *_: (i, 0)),
                      pl.BlockSpec((1, 6, d), lambda i, *_: (i * tm // seq, 0, 0)),
                      pl.BlockSpec((1, d), lambda i, *_: (0, 0)),
                      pl.BlockSpec((1, d), lambda i, *_: (0, 0))],
            out_specs=pl.BlockSpec((tm, d), lambda i, *_: (i, 0)),
            scratch_shapes=[pltpu.VMEM((2, 2, tm * ROW_TILES, LANE), F32), pltpu.SemaphoreType.DMA((2,))]),
        out_shape=jax.ShapeDtypeStruct((nt, d), F32),
        compiler_params=_cparams(("arbitrary",)),
        name="moe_combine",
    )(dest0, dest1, y_pad, rec, x1, mod, ln_g, ln_b)


def _rel_bucket(dist):
    n = jnp.maximum(dist, 0)
    max_exact = REL_BUCKETS // 2
    nf = jnp.maximum(n, 1).astype(F32)
    large = max_exact + (jnp.log(nf / max_exact) / math.log(REL_MAX_DIST / max_exact)
                         * (REL_BUCKETS - max_exact)).astype(jnp.int32)
    large = jnp.minimum(large, REL_BUCKETS - 1)
    return jnp.where(n < max_exact, n, large)


def _bias_tables(rel_bias, seq):
    bucket_onehot = (_rel_bucket(jnp.arange(LANE))[:, None] == jnp.arange(REL_BUCKETS)).astype(F32)
    tab_d = jnp.einsum('db,hb->hd', bucket_onehot, rel_bias,
                       precision=lax.Precision.HIGHEST)
    tok = np.arange(QT)[None, :]
    key = np.arange(LANE)[:, None]
    far = tab_d[:, LANE - 1]
    cols = NSA_REP * QT

    def transposed(dist):
        idx = jnp.asarray(np.clip(dist, 0, LANE - 1).astype(np.int32))
        onehot = (idx[..., None] == jnp.arange(LANE, dtype=jnp.int32)).astype(F32)
        t = jnp.einsum('ijd,hd->hij', onehot, tab_d, precision=lax.Precision.HIGHEST)
        t = (t - far[:, None, None]) * LOG2E
        t = t.reshape(NSA_GROUPS, NSA_REP, LANE, QT).transpose(0, 2, 1, 3)
        return t.reshape(NSA_GROUPS, LANE, cols)

    t0t = transposed(tok - key)
    t1t = transposed(tok - key + QT)

    ns = seq // CMP_STRIDE
    d_c = tok - CMP_STRIDE * key + (CMP_STRIDE * CMP_PAD - (CMP_BLOCK - 1))
    seen = np.tile(d_c >= 0, (1, NSA_REP))
    recent = jnp.where(seen[None], transposed(d_c), NEG)
    tt = jnp.concatenate([jnp.zeros((NSA_GROUPS, ns, cols), F32), recent,
                          jnp.full((NSA_GROUPS, ns, cols), NEG, F32)], axis=1)

    rho = np.arange(WINDOW + QT)[:, None]
    tok_w = np.tile(np.arange(QT), NSA_REP)[None, :]
    band = (rho > tok_w) & (rho <= tok_w + WINDOW)
    rows = jnp.concatenate([jnp.zeros((NSA_GROUPS, WINDOW - QT, NSA_REP * QT), F32), t1t, t0t], axis=1)
    wb = jnp.where(band[None], rows, NEG)

    zeros = lambda n: jnp.zeros((NSA_GROUPS, n * QT, cols), F32)
    negs = lambda n: jnp.full((NSA_GROUPS, n * QT, cols), NEG, F32)
    diag = jnp.where(np.tile(tok >= key, (1, NSA_REP))[None], t0t, NEG)
    nd = jnp.stack([jnp.concatenate(([zeros(u - 1), t1t] if u else []) + [diag, negs(SEL_SUB - 1 - u)], axis=1)
                    for u in range(SEL_SUB)], axis=1)
    npv = jnp.concatenate([zeros(SEL_SUB - 1), t1t], axis=1)
    return nd, npv, wb, tt


def _overlap_matrix(seq):
    ns = seq // CMP_STRIDE
    nslc = seq // SLC_BLOCK
    ov = np.zeros((LANE, ns), np.float32)
    cs = np.arange(ns - 1) * CMP_STRIDE
    ss = np.arange(nslc) * SLC_BLOCK
    ov[:nslc, :ns - 1] = ((cs[None, :] < ss[:, None] + SLC_BLOCK) & (cs[None, :] + CMP_BLOCK > ss[:, None]))
    return jnp.asarray(ov, BF16)


def _reorder_cols(a):
    lead = a.shape[:-1]
    gate = a[..., MAIN_COLS:MAIN_COLS + GATE_COLS]
    per = GATE_COLS // NSA_GROUPS
    gate_blocks = []
    for g in range(NSA_GROUPS):
        gate_blocks.append(gate[..., g * per:(g + 1) * per])
        gate_blocks.append(jnp.zeros(lead + (LANE - per,), a.dtype))
    pad = jnp.zeros(lead + ((CB_MGH - CB_GATE - NSA_GROUPS) * LANE,), a.dtype)
    return jnp.concatenate([a[..., :MAIN_COLS]] + gate_blocks + [pad, a[..., MAIN_COLS + GATE_COLS:]], axis=-1)


def kernel(x, c, ada_w, ada_b, w_in, b_in, hg_lb_logits, hg_norm_w, cmp_pos_k, cmp_w1_k, cmp_b1_k, cmp_w2_k, cmp_pos_v, cmp_w1_v, cmp_b1_v, cmp_w2_v, rel_bias, w_br_hg, w_br_nsa, w_out, ln1_g, ln1_b, router_grp_w, router_grp_b, router_exp_w, router_exp_b, exp_w1, exp_w3, exp_w2, ln2_g, ln2_b):
    batch, seq, d = x.shape
    nt = batch * seq
    assert d == D_MODEL and seq % INPROJ_ROWS == 0 and seq // SLC_BLOCK <= LANE
    l = 0
    x2 = x.reshape(nt, d)

    c_pad = jnp.zeros((8, d), F32).at[:batch].set(c)
    mod = _adaln(c_pad, ada_w[l], ada_b[l][None])[:batch].reshape(batch, 6, d)

    proj = _inproj(x2, mod, _reorder_cols(w_in[l].astype(BF16)), _reorder_cols(b_in[l])[None], seq)

    lb_all = jnp.cumsum(jax.nn.softmax(hg_lb_logits.astype(F32), axis=0), axis=0)
    o_h = _hgrn(proj, lb_all[l][None], hg_norm_w[l][None], batch, seq)

    kvc = _compress(proj, jnp.stack([cmp_pos_k[l], cmp_pos_v[l]]),
                    jnp.stack([cmp_w1_k[l], cmp_w1_v[l]]).astype(BF16),
                    jnp.stack([cmp_b1_k[l], cmp_b1_v[l]])[:, None, :],
                    jnp.stack([cmp_w2_k[l], cmp_w2_v[l]]).astype(BF16), batch, seq)

    nd, npv, wb, tt = _bias_tables(rel_bias, seq)
    o_c, sel_t = _cmp_sel(proj, kvc, tt, _overlap_matrix(seq), batch, seq)
    o_w = _window(proj, wb, batch, seq)
    o_n = _selected(proj, sel_t, nd, npv, o_c, o_w, batch, seq)

    w_r = jnp.zeros((d, LANE), F32).at[:, :N_GROUPS].set(router_grp_w[l])
    w_r = w_r.at[:, N_GROUPS:N_GROUPS + N_EXPERTS].set(router_exp_w[l])
    w_r_hi = w_r.astype(BF16)
    w_r_lo = (w_r - w_r_hi.astype(F32)).astype(BF16)
    b_r = jnp.zeros((1, LANE), F32).at[0, :N_GROUPS].set(router_grp_b[l])
    b_r = b_r.at[0, N_GROUPS:N_GROUPS + N_EXPERTS].set(router_exp_b[l])
    x1, h2, logits = _merge(o_h, o_n, proj, x2, mod,
                            w_br_hg[l].astype(BF16), w_br_nsa[l].astype(BF16), w_out[l].astype(BF16),
                            ln1_g[l][None], ln1_b[l][None], jnp.stack([w_r_hi, w_r_lo]), b_r, seq)

    rec, cnt, rec_t = _route(logits)
    counts = cnt[0, N_GROUPS:N_GROUPS + N_EXPERTS].astype(jnp.int32)
    padded = (counts + MOE_ROWS - 1) // MOE_ROWS * MOE_ROWS
    pend = jnp.cumsum(padded)
    pstart = pend - padded
    n_assign = 2 * nt
    nb = n_assign // MOE_ROWS + N_EXPERTS
    slots = rec_t.astype(jnp.int32)
    expert_ids = jnp.arange(N_EXPERTS, dtype=jnp.int32)[:, None]
    slot_base = lambda e: jnp.sum(jnp.where(e[None, :] == expert_ids, pstart[:, None], 0), axis=0)
    dest0 = slot_base(slots[0]) + slots[4]
    dest1 = slot_base(slots[1]) + slots[5]
    block_start = jnp.arange(nb, dtype=jnp.int32) * MOE_ROWS
    block_expert = jnp.minimum(jnp.sum(pend[None, :] <= block_start[:, None], axis=1),
                               N_EXPERTS - 1).astype(jnp.int32)
    n_used = pend[-1] // MOE_ROWS
    spare = n_used + jnp.arange(N_EXPERTS, dtype=jnp.int32)
    zero_blocks = jnp.concatenate([jnp.where(padded > 0, pend // MOE_ROWS - 1, -1),
                                   jnp.where(spare < nb, spare, -1)]).astype(jnp.int32)

    x_pad = _dispatch(dest0, dest1, zero_blocks, h2, nb)
    y_pad = _experts(block_expert, n_used[None].astype(jnp.int32), x_pad, exp_w1[l], exp_w3[l], exp_w2[l])
    out = _combine(dest0, dest1, y_pad, rec, x1, mod, ln2_g[l][None], ln2_b[l][None], seq)
    return out.reshape(batch, seq, d)
```
